```python
import math
import jax, jax.numpy as jnp
from jax import lax
import numpy as np

D_MODEL = 2048
BATCH = 2
SEQ = 4096
DEPTH = 2
DEC_BATCH = 128
DEC_SEQ = 1
PAST_LEN = 2048
PAGE_SIZE = 128

N_A_LAYERS = (DEPTH + 1) // 2
N_C_LAYERS = DEPTH // 2
A_HEADS = 8
A_KDIM = 128
A_VDIM = D_MODEL // 2 // A_HEADS
A_WIDTH = A_HEADS * A_VDIM
HGRN_CHUNK = 64
B_HEADS = 8
B_HEAD_DIM = D_MODEL // 2 // B_HEADS
B_WIDTH = B_HEADS * B_HEAD_DIM
IDX_HEADS = 16
IDX_DIM = 64
TOPK_MAX = 256
Q_BLOCK = 128
REL_BUCKETS = 32
REL_MAX_DIST = 128
CONV_WIDTH = 3
C_WIDTH = D_MODEL
N_GROUPS = 4
EXP_PER_GROUP = 4
N_EXPERTS = N_GROUPS * EXP_PER_GROUP
TOPK_IN_GROUP = 2
D_EXPERT = 512
RMS_EPS = 1e-6
EVEN_SPLITS = (A_HEADS * A_KDIM, A_HEADS * A_KDIM, A_WIDTH, A_WIDTH,
               B_WIDTH, B_WIDTH, B_WIDTH, IDX_HEADS * IDX_DIM, IDX_DIM, IDX_HEADS)
EVEN_IN = sum(EVEN_SPLITS)

kernel_name = 'hgrn2_dsa_shortconv_hmoe_step'


def rms_norm(x, g):
    xf = x.astype(jnp.float32)
    y = xf * lax.rsqrt(jnp.mean(xf * xf, axis=-1, keepdims=True) + RMS_EPS)
    return (y * g.astype(jnp.float32)).astype(x.dtype)


def split_cols(z, sizes):
    offs = np.cumsum(np.array(sizes))[:-1].tolist()
    return jnp.split(z, offs, axis=-1)


def rel_bucket(dist):
    exact = REL_BUCKETS // 2
    d = jnp.maximum(dist, 0)
    df = jnp.maximum(d, 1).astype(jnp.float32)
    far = exact + (jnp.log(df / exact) / math.log(REL_MAX_DIST / exact)
                   * (REL_BUCKETS - exact)).astype(jnp.int32)
    return jnp.where(d < exact, d, jnp.minimum(far, REL_BUCKETS - 1))


def hgrn_lower_bounds(lb_logits):
    p = jax.nn.softmax(lb_logits.astype(jnp.float32), axis=0)
    return jnp.cumsum(p, axis=0)[:-1]


def hgrn2_scan(q, k, v, logf, s0):
    Bn, L, H, K = q.shape
    V = v.shape[-1]
    C = min(HGRN_CHUNK, L)
    n = -(-L // C)
    pad = n * C - L
    padf = lambda a: jnp.pad(a, ((0, 0), (0, pad), (0, 0), (0, 0)))
    blk = lambda a: padf(a).reshape(Bn, n, C, H, a.shape[-1]).transpose(1, 0, 3, 2, 4)
    qc, kc, vc, gc = blk(q), blk(k), blk(v), blk(logf)
    mask = jnp.tril(jnp.ones((C, C), dtype=bool))

    def step(S, inp):
        qt, kt, vt, gt = inp
        b = jnp.cumsum(gt, axis=2)
        diff = b[:, :, :, None, :] - b[:, :, None, :, :]
        decay = jnp.exp(jnp.where(mask[:, :, None], diff, -jnp.inf))
        A = jnp.einsum('bhtk,bhsk,bhtsk->bhts', qt, kt, decay)
        o = (jnp.einsum('bhts,bhsv->bhtv', A, vt)
             + jnp.einsum('bhtk,bhkv->bhtv', qt * jnp.exp(b), S))
        bl = b[:, :, -1:, :]
        S = (S * jnp.exp(bl[:, :, 0, :, None])
             + jnp.einsum('bhsk,bhsv->bhkv', kt * jnp.exp(bl - b), vt))
        return S, o

    S, o = lax.scan(step, s0, (qc, kc, vc, gc))
    o = o.transpose(1, 0, 3, 2, 4).reshape(Bn, n * C, H, V)[:, :L]
    return o, S


def hgrn2_mix(qa, fa, ia, ga, lb, g_norm, s0):
    Bn, L = qa.shape[:2]
    heads = lambda a, d: a.reshape(Bn, L, A_HEADS, d)
    f = lb + (1.0 - lb) * jax.nn.sigmoid(fa.astype(jnp.float32))
    q = heads(jax.nn.silu(qa.astype(jnp.float32)), A_KDIM)
    k = heads(1.0 - f, A_KDIM)
    logf = heads(jnp.log(f), A_KDIM)
    v = heads(ia.astype(jnp.float32), A_VDIM)
    o, s_new = hgrn2_scan(q, k, v, logf, s0.astype(jnp.float32))
    o = rms_norm(o, g_norm) * jax.nn.silu(heads(ga.astype(jnp.float32), A_VDIM))
    return o.reshape(Bn, L, A_WIDTH).astype(qa.dtype), s_new.astype(s0.dtype)


def dsa_attention(q, k_new, v_new, qi, ki_new, wi, rel_bias, past):
    Bn, Lq = q.shape[:2]
    bidx = jnp.arange(Bn)[:, None, None]
    if past is None:
        start = 0
        ki_all = ki_new

        def gather_kv(sel):
            return k_new[bidx, sel], v_new[bidx, sel]
    else:
        k_pool, v_pool, ki_pool, page_table = past
        start = PAST_LEN
        ki_past = ki_pool[page_table].reshape(Bn, PAST_LEN, IDX_DIM)
        ki_all = jnp.concatenate([ki_past, ki_new.astype(ki_past.dtype)], axis=1)

        def gather_kv(sel):
            in_past = (sel < PAST_LEN)[..., None, None]
            ps = jnp.minimum(sel, PAST_LEN - 1)
            phys = page_table[bidx, ps // PAGE_SIZE]
            off = ps % PAGE_SIZE
            ns = jnp.clip(sel - PAST_LEN, 0, Lq - 1)
            return (jnp.where(in_past, k_pool[phys, off], k_new[bidx, ns]),
                    jnp.where(in_past, v_pool[phys, off], v_new[bidx, ns]))

    n_keys = ki_all.shape[1]
    top_k = min(TOPK_MAX, n_keys // 4)
    key_pos = jnp.arange(n_keys)
    q_pos = start + jnp.arange(Lq)
    qbs = Q_BLOCK if Lq % Q_BLOCK == 0 else Lq
    nb = Lq // qbs
    to_blocks = lambda a: jnp.moveaxis(a.reshape(Bn, nb, qbs, *a.shape[2:]), 1, 0)

    def block(args):
        qb, qib, wib, pb = args
        s = jax.nn.relu(jnp.einsum('bqhd,bkd->bqhk', qib, ki_all) * IDX_DIM ** -0.5)
        score = jnp.einsum('bqh,bqhk->bqk', wib, s).astype(jnp.float32)
        score = jnp.where(key_pos[None, None, :] <= pb[None, :, None], score, -jnp.inf)
        _, sel = lax.top_k(score, top_k)
        k_sel, v_sel = gather_kv(sel)
        logits = jnp.einsum('bqhd,bqkhd->bqhk', qb, k_sel).astype(jnp.float32) * B_HEAD_DIM ** -0.5
        bias = rel_bias[rel_bucket(pb[None, :, None] - sel)]
        logits = logits + jnp.moveaxis(bias, -1, 2).astype(jnp.float32)
        valid = (sel <= pb[None, :, None])[:, :, None, :]
        p = jax.nn.softmax(jnp.where(valid, logits, -jnp.inf), axis=-1)
        return jnp.einsum('bqhk,bqkhd->bqhd', p.astype(v_sel.dtype), v_sel)

    out = lax.map(block, (to_blocks(q), to_blocks(qi), to_blocks(wi), q_pos.reshape(nb, qbs)))
    return jnp.moveaxis(out, 0, 1).reshape(Bn, Lq, B_HEADS, B_HEAD_DIM)


def even_mixer(h, w_in, w_out, lb, hgrn_g, rel_bias, s0, past):
    Bn, L = h.shape[:2]
    z = jnp.einsum('bld,de->ble', h, w_in)
    qa, fa, ia, ga, qb, kb, vb, qi, ki, wi = split_cols(z, EVEN_SPLITS)
    oa, s_new = hgrn2_mix(qa, fa, ia, ga, lb, hgrn_g, s0)
    qb = qb.reshape(Bn, L, B_HEADS, B_HEAD_DIM)
    kb = kb.reshape(Bn, L, B_HEADS, B_HEAD_DIM)
    vb = vb.reshape(Bn, L, B_HEADS, B_HEAD_DIM)
    qi = qi.reshape(Bn, L, IDX_HEADS, IDX_DIM)
    wi = wi * IDX_HEADS ** -0.5
    ob = dsa_attention(qb, kb, vb, qi, ki, wi, rel_bias, past)
    o = jnp.concatenate([oa, ob.reshape(Bn, L, B_WIDTH).astype(oa.dtype)], axis=-1)
    return jnp.einsum('ble,ed->bld', o, w_out), s_new, kb, vb, ki


def shortconv_mix(h, w_in, w_conv, w_out, conv_s0):
    L = h.shape[1]
    z = jnp.einsum('bld,de->ble', h, w_in)
    bg, cg, xt = jnp.split(z, 3, axis=-1)
    u = cg * xt
    up = jnp.concatenate([conv_s0.astype(u.dtype), u], axis=1)
    conv = sum(w_conv[j] * up[:, j:j + L] for j in range(CONV_WIDTH))
    y = jnp.einsum('blc,cd->bld', bg * conv, w_out)
    return y, up[:, -(CONV_WIDTH - 1):]


def hier_moe(h, wrg, brg, wre, bre, wg, wu, wd):
    pg = jax.nn.softmax(jnp.einsum('td,dg->tg', h, wrg).astype(jnp.float32) + brg, axis=-1)
    g_w, g_idx = lax.top_k(pg, 1)
    le = jnp.einsum('td,dge->tge', h, wre).astype(jnp.float32) + bre
    le = jnp.take_along_axis(le, g_idx[:, :, None], axis=1)[:, 0]
    e_w, e_idx = lax.top_k(jax.nn.softmax(le, axis=-1), TOPK_IN_GROUP)
    gate = g_w * e_w / jnp.sum(e_w, axis=-1, keepdims=True)
    eid = g_idx * EXP_PER_GROUP + e_idx
    combine = jnp.einsum('tk,tke->te', gate, jax.nn.one_hot(eid, N_EXPERTS, dtype=jnp.float32))
    hid = jax.nn.silu(jnp.einsum('td,edf->tef', h, wg)) * jnp.einsum('td,edf->tef', h, wu)
    hid = hid * combine[:, :, None].astype(hid.dtype)
    return jnp.einsum('tef,efd->td', hid, wd)


def trunk(x, hgrn_s0, conv_s0, past, norm_mix_g, norm_ffn_g, final_g, w_in_even, w_out_even,
          hgrn_lb_logits, hgrn_norm_g, rel_bias, w_in_conv, w_conv, w_out_conv,
          w_router_g, b_router_g, w_router_e, b_router_e, w_gate, w_up, w_down):
    Bn, L, D = x.shape
    lbs = hgrn_lower_bounds(hgrn_lb_logits)
    ks, vs, kis, ss, cs = [], [], [], [], []
    for l in range(DEPTH):
        h = rms_norm(x, norm_mix_g[l])
        if l % 2 == 0:
            a = l // 2
            past_l = None if past is None else (past[0][a], past[1][a], past[2][a], past[3])
            y, s_new, kb, vb, ki = even_mixer(h, w_in_even[a], w_out_even[a], lbs[a],
                                              hgrn_norm_g[a], rel_bias, hgrn_s0[a], past_l)
            ks.append(kb); vs.append(vb); kis.append(ki); ss.append(s_new)
        else:
            c = l // 2
            y, c_new = shortconv_mix(h, w_in_conv[c], w_conv[c], w_out_conv[c], conv_s0[c])
            cs.append(c_new)
        x = x + y.astype(x.dtype)
        h = rms_norm(x, norm_ffn_g[l])
        x = x + hier_moe(h.reshape(Bn * L, D), w_router_g[l], b_router_g[l], w_router_e[l],
                         b_router_e[l], w_gate[l], w_up[l], w_down[l]).reshape(Bn, L, D).astype(x.dtype)
    y = rms_norm(x, final_g)
    return y, jnp.stack(ks), jnp.stack(vs), jnp.stack(kis), jnp.stack(ss), jnp.stack(cs)


def setup_inputs(seed: int = 0) -> dict:
    key = jax.random.key(seed)
    ks = jax.random.split(key, 32)
    f32 = jnp.float32
    n_pages = PAST_LEN // PAGE_SIZE
    n_pool = (DEC_BATCH * n_pages * 5) // 4
    nrm = lambda k, shape, scale: jax.random.normal(k, shape, f32) * scale
    gain = lambda k, shape: 1.0 + 0.01 * jax.random.normal(k, shape, f32)
    perm = jax.random.permutation(ks[7], n_pool).astype(jnp.int32)
    return {
        'x_prompt': nrm(ks[0], (BATCH, SEQ, D_MODEL), 1.0),
        'x_sample': nrm(ks[1], (DEC_BATCH, DEC_SEQ, D_MODEL), 1.0),
        'cache_k': nrm(ks[2], (N_A_LAYERS, n_pool, PAGE_SIZE, B_HEADS, B_HEAD_DIM), 1.0),
        'cache_v': nrm(ks[3], (N_A_LAYERS, n_pool, PAGE_SIZE, B_HEADS, B_HEAD_DIM), 1.0),
        'cache_ki': nrm(ks[4], (N_A_LAYERS, n_pool, PAGE_SIZE, IDX_DIM), 1.0),
        'state_hgrn': nrm(ks[5], (N_A_LAYERS, DEC_BATCH, A_HEADS, A_KDIM, A_VDIM), 0.5),
        'state_conv': nrm(ks[6], (N_C_LAYERS, DEC_BATCH, CONV_WIDTH - 1, C_WIDTH), 1.0),
        'page_table': perm[:DEC_BATCH * n_pages].reshape(DEC_BATCH, n_pages),
        'norm_mix_g': gain(ks[8], (DEPTH, D_MODEL)),
        'norm_ffn_g': gain(ks[9], (DEPTH, D_MODEL)),
        'final_g': gain(ks[10], (D_MODEL,)),
        'w_in_even': nrm(ks[11], (N_A_LAYERS, D_MODEL, EVEN_IN), D_MODEL ** -0.5),
        'w_out_even': nrm(ks[12], (N_A_LAYERS, A_WIDTH + B_WIDTH, D_MODEL), (A_WIDTH + B_WIDTH) ** -0.5),
        'hgrn_lb_logits': nrm(ks[13], (N_A_LAYERS + 1, A_HEADS * A_KDIM), 0.5),
        'hgrn_norm_g': gain(ks[14], (N_A_LAYERS, A_VDIM)),
        'rel_bias': nrm(ks[15], (REL_BUCKETS, B_HEADS), 0.5),
        'w_in_conv': nrm(ks[16], (N_C_LAYERS, D_MODEL, 3 * C_WIDTH), D_MODEL ** -0.5),
        'w_conv': nrm(ks[17], (N_C_LAYERS, CONV_WIDTH, C_WIDTH), CONV_WIDTH ** -0.5),
        'w_out_conv': nrm(ks[18], (N_C_LAYERS, C_WIDTH, D_MODEL), C_WIDTH ** -0.5),
        'w_router_g': nrm(ks[19], (DEPTH, D_MODEL, N_GROUPS), D_MODEL ** -0.5),
        'b_router_g': nrm(ks[20], (DEPTH, N_GROUPS), 0.01),
        'w_router_e': nrm(ks[21], (DEPTH, D_MODEL, N_GROUPS, EXP_PER_GROUP), D_MODEL ** -0.5),
        'b_router_e': nrm(ks[22], (DEPTH, N_GROUPS, EXP_PER_GROUP), 0.01),
        'w_gate': nrm(ks[23], (DEPTH, N_EXPERTS, D_MODEL, D_EXPERT), D_MODEL ** -0.5),
        'w_up': nrm(ks[24], (DEPTH, N_EXPERTS, D_MODEL, D_EXPERT), D_MODEL ** -0.5),
        'w_down': nrm(ks[25], (DEPTH, N_EXPERTS, D_EXPERT, D_MODEL), D_EXPERT ** -0.5),
    }


def reference(x_prompt, x_sample, cache_k, cache_v, cache_ki, state_hgrn, state_conv, page_table,
              norm_mix_g, norm_ffn_g, final_g, w_in_even, w_out_even, hgrn_lb_logits, hgrn_norm_g,
              rel_bias, w_in_conv, w_conv, w_out_conv, w_router_g, b_router_g, w_router_e,
              b_router_e, w_gate, w_up, w_down):
    weights = (norm_mix_g, norm_ffn_g, final_g, w_in_even, w_out_even, hgrn_lb_logits, hgrn_norm_g,
               rel_bias, w_in_conv, w_conv, w_out_conv, w_router_g, b_router_g, w_router_e,
               b_router_e, w_gate, w_up, w_down)
    hgrn0 = jnp.zeros((N_A_LAYERS, x_prompt.shape[0], A_HEADS, A_KDIM, A_VDIM), x_prompt.dtype)
    conv0 = jnp.zeros((N_C_LAYERS, x_prompt.shape[0], CONV_WIDTH - 1, C_WIDTH), x_prompt.dtype)
    y_prompt, k_prompt, v_prompt, ki_prompt, hgrn_prompt, conv_prompt = trunk(
        x_prompt, hgrn0, conv0, None, *weights)
    y_sample, k_sample, v_sample, ki_sample, hgrn_sample, conv_sample = trunk(
        x_sample, state_hgrn, state_conv, (cache_k, cache_v, cache_ki, page_table), *weights)
    return (y_prompt, y_sample, k_prompt, v_prompt, ki_prompt, hgrn_prompt, conv_prompt,
            k_sample, v_sample, ki_sample, hgrn_sample, conv_sample)
```

```python
import functools
import math

import numpy as np
import jax
import jax.numpy as jnp
from jax import lax
from jax.experimental import pallas as pl
from jax.experimental.pallas import tpu as pltpu

F32 = jnp.float32
BF = jnp.bfloat16
I32 = jnp.int32

RMS_EPS = 1e-6
LANES = 128
NEG_BIG = -1e30
VMEM_LIMIT = 56 * 1024 * 1024

D_MODEL = 2048
HEADS = 8
HEAD_DIM = 128
IDX_HEADS = 16
IDX_DIM = 64
TOPK = 256
REL_BUCKETS = 32
REL_MAX_DIST = 128
N_EXPERTS = 16
EXP_PER_GROUP = 4
N_GROUPS = 4
D_EXPERT = 512
PAGE = 128

HGRN_CHUNK = 64
MOE_TILE = 256


def _cparams(sem):
    return pltpu.CompilerParams(dimension_semantics=sem, vmem_limit_bytes=VMEM_LIMIT)


def _dot(a, b):
    return jnp.dot(a, b, preferred_element_type=F32)


def _dot_nt(a, b):
    return lax.dot_general(a, b, (((1,), (1,)), ((), ())), preferred_element_type=F32)


def _dot_tn(a, b):
    return lax.dot_general(a, b, (((0,), (0,)), ((), ())), preferred_element_type=F32)


def _sigmoid(x):
    return 1.0 / (1.0 + jnp.exp(-x))


def _norm_mm_body(x_ref, g_ref, w_ref, o_ref, h_ref):
    @pl.when(pl.program_id(1) == 0)
    def _():
        x = x_ref[...]
        ms = jnp.mean(x * x, axis=-1, keepdims=True)
        h_ref[...] = (x * lax.rsqrt(ms + RMS_EPS) * g_ref[...]).astype(BF)

    o_ref[...] = _dot(h_ref[...], w_ref[...])


def norm_matmul(x, g, w_bf, tm, tn):
    m, k = x.shape
    n = w_bf.shape[1]
    return pl.pallas_call(
        _norm_mm_body,
        grid=(m // tm, n // tn),
        in_specs=[pl.BlockSpec((tm, k), lambda i, j: (i, 0)),
                  pl.BlockSpec((1, k), lambda i, j: (0, 0)),
                  pl.BlockSpec((k, tn), lambda i, j: (0, j))],
        out_specs=pl.BlockSpec((tm, tn), lambda i, j: (i, j)),
        out_shape=jax.ShapeDtypeStruct((m, n), F32),
        scratch_shapes=[pltpu.VMEM((tm, k), BF)],
        compiler_params=_cparams(("parallel", "arbitrary")),
        name="norm_matmul",
    )(x, g.reshape(1, k), w_bf)


def _mm_res_body(*refs, n_lhs):
    a_refs = refs[:n_lhs]
    w_refs = refs[n_lhs:2 * n_lhs]
    r_ref = refs[2 * n_lhs]
    o_ref = refs[2 * n_lhs + 1]
    s_refs = refs[2 * n_lhs + 2:]

    @pl.when(pl.program_id(1) == 0)
    def _():
        for a_ref, s_ref in zip(a_refs, s_refs):
            s_ref[...] = a_ref[...].astype(BF)

    acc = r_ref[...]
    for s_ref, w_ref in zip(s_refs, w_refs):
        acc = acc + _dot(s_ref[...], w_ref[...])
    o_ref[...] = acc


def matmul_residual(lhs, ws_bf, res, tm, tn):
    m, n = res.shape
    n_lhs = len(lhs)
    in_specs = ([pl.BlockSpec((tm, a.shape[1]), lambda i, j: (i, 0)) for a in lhs]
                + [pl.BlockSpec((w.shape[0], tn), lambda i, j: (0, j)) for w in ws_bf]
                + [pl.BlockSpec((tm, tn), lambda i, j: (i, j))])
    return pl.pallas_call(
        functools.partial(_mm_res_body, n_lhs=n_lhs),
        grid=(m // tm, n // tn),
        in_specs=in_specs,
        out_specs=pl.BlockSpec((tm, tn), lambda i, j: (i, j)),
        out_shape=jax.ShapeDtypeStruct((m, n), F32),
        scratch_shapes=[pltpu.VMEM((tm, a.shape[1]), BF) for a in lhs],
        compiler_params=_cparams(("parallel", "arbitrary")),
        name="matmul_residual",
    )(*lhs, *ws_bf, res)


def _rmsnorm_body(x_ref, g_ref, o_ref):
    x = x_ref[...]
    ms = jnp.mean(x * x, axis=-1, keepdims=True)
    o_ref[...] = x * lax.rsqrt(ms + RMS_EPS) * g_ref[...]


def rmsnorm(x, g, tm):
    m, k = x.shape
    return pl.pallas_call(
        _rmsnorm_body,
        grid=(m // tm,),
        in_specs=[pl.BlockSpec((tm, k), lambda i: (i, 0)),
                  pl.BlockSpec((1, k), lambda i: (0, 0))],
        out_specs=pl.BlockSpec((tm, k), lambda i: (i, 0)),
        out_shape=jax.ShapeDtypeStruct((m, k), F32),
        compiler_params=_cparams(("parallel",)),
        name="rmsnorm",
    )(x, g.reshape(1, k))


def _hgrn_static(c):
    levels = []
    m = 1
    while m < c:
        levels.append(m)
        m *= 2
    t = np.arange(c)
    rows = [t[None, :] <= t[:, None]]
    masks = [np.eye(c, dtype=bool)]
    for m in levels:
        blk = t // (2 * m)
        pos = t % (2 * m)
        bnd = blk * 2 * m + m - 1
        right = pos >= m
        left = pos < m
        rows.append((t[None, :] > bnd[:, None]) & (t[None, :] <= t[:, None]) & right[:, None])
        rows.append((t[None, :] > t[:, None]) & (t[None, :] <= bnd[:, None]) & left[:, None])
        masks.append((blk[:, None] == blk[None, :]) & right[:, None] & left[None, :])
    m_all = np.concatenate(rows, 0).astype(np.float32)
    masks = np.stack(masks).astype(np.float32)
    return m_all, masks, len(levels)


def _hgrn_gates(qa, fa, lb):
    f = lb + (1.0 - lb) * _sigmoid(fa)
    q = qa * _sigmoid(qa)
    return q, f


def _hgrn_prompt_body(qa_ref, fa_ref, ia_ref, ga_ref, lb_ref, gn_ref, mall_ref, masks_ref,
                      oa_ref, st_out_ref, st_ref, *, chunk, tblock, n_levels):
    c = chunk
    t_idx = pl.program_id(1)

    @pl.when(t_idx == 0)
    def _():
        st_ref[...] = jnp.zeros_like(st_ref)

    mall = mall_ref[...]
    gn = gn_ref[...]

    def chunk_step(ci, carry):
        r0 = pl.multiple_of(ci * c, c)
        for h in range(HEADS):
            cols = slice(h * HEAD_DIM, (h + 1) * HEAD_DIM)
            qa = qa_ref[0, pl.ds(r0, c), cols]
            fa = fa_ref[0, pl.ds(r0, c), cols]
            ia = ia_ref[0, pl.ds(r0, c), cols]
            ga = ga_ref[0, pl.ds(r0, c), cols]
            q, f = _hgrn_gates(qa, fa, lb_ref[:, cols])
            k = 1.0 - f
            g = jnp.log(f)
            g_hi = g.astype(BF)
            g_lo = (g - g_hi.astype(F32)).astype(BF)
            r = _dot(mall, g_hi) + _dot(mall, g_lo)
            b = r[0:c]
            b_last = b[c - 1:c]
            st = st_ref[h]
            v = ia.astype(BF)
            o = _dot_nt((q * jnp.exp(b)).astype(BF), st.astype(BF))
            a = masks_ref[0] * _dot_nt(q.astype(BF), k.astype(BF))
            for li in range(n_levels):
                e = r[(1 + 2 * li) * c:(2 + 2 * li) * c]
                fk = r[(2 + 2 * li) * c:(3 + 2 * li) * c]
                a = a + masks_ref[1 + li] * _dot_nt((q * jnp.exp(e)).astype(BF),
                                                    (k * jnp.exp(fk)).astype(BF))
            o = o + _dot(a.astype(BF), v)
            k_st = (k * jnp.exp(b_last - b)).astype(BF)
            st_ref[h] = st * jnp.exp(b_last) + _dot_tn(v, k_st)
            ms = jnp.mean(o * o, axis=-1, keepdims=True)
            y = o * lax.rsqrt(ms + RMS_EPS) * gn * (ga * _sigmoid(ga))
            oa_ref[0, pl.ds(r0, c), cols] = y
        return carry

    lax.fori_loop(0, tblock // c, chunk_step, 0)

    @pl.when(t_idx == pl.num_programs(1) - 1)
    def _():
        st_out_ref[0] = st_ref[...]


def hgrn_prompt(z3, lb, gn, tblock=256, chunk=HGRN_CHUNK):
    bsz, seq = z3.shape[:2]
    width = HEADS * HEAD_DIM
    m_all, masks, n_levels = _hgrn_static(chunk)
    zspec = lambda cb: pl.BlockSpec((1, tblock, width), lambda b, t, cb=cb: (b, t, cb))
    oa, st = pl.pallas_call(
        functools.partial(_hgrn_prompt_body, chunk=chunk, tblock=tblock, n_levels=n_levels),
        grid=(bsz, seq // tblock),
        in_specs=[zspec(0), zspec(1), zspec(2), zspec(3),
                  pl.BlockSpec((1, width), lambda b, t: (0, 0)),
                  pl.BlockSpec((1, HEAD_DIM), lambda b, t: (0, 0)),
                  pl.BlockSpec(m_all.shape, lambda b, t: (0, 0)),
                  pl.BlockSpec(masks.shape, lambda b, t: (0, 0, 0))],
        out_specs=[pl.BlockSpec((1, tblock, width), lambda b, t: (b, t, 0)),
                   pl.BlockSpec((1, HEADS, HEAD_DIM, HEAD_DIM), lambda b, t: (b, 0, 0, 0))],
        out_shape=[jax.ShapeDtypeStruct((bsz, seq, width), F32),
                   jax.ShapeDtypeStruct((bsz, HEADS, HEAD_DIM, HEAD_DIM), F32)],
        scratch_shapes=[pltpu.VMEM((HEADS, HEAD_DIM, HEAD_DIM), F32)],
        compiler_params=_cparams(("parallel", "arbitrary")),
        name="hgrn_prompt",
    )(z3, z3, z3, z3, lb.reshape(1, width), gn.reshape(1, HEAD_DIM),
      jnp.asarray(m_all, BF), jnp.asarray(masks, F32))
    return oa, jnp.swapaxes(st, -1, -2)


def _col(row, eye):
    return jnp.sum(eye * row, axis=1, keepdims=True)


def _hgrn_sample_body(qa_ref, fa_ref, ia_ref, ga_ref, lb_ref, gn_ref, s_ref, oa_ref, so_ref):
    eye = (lax.broadcasted_iota(I32, (HEAD_DIM, HEAD_DIM), 0)
           == lax.broadcasted_iota(I32, (HEAD_DIM, HEAD_DIM), 1)).astype(F32)
    q8, f8 = _hgrn_gates(qa_ref[0], fa_ref[0], lb_ref[...])
    ga = ga_ref[0]
    gate = ga * _sigmoid(ga)
    outs = []
    for h in range(HEADS):
        f_col = _col(f8[h:h + 1], eye)
        q_col = _col(q8[h:h + 1], eye)
        s_new = f_col * s_ref[0, h] + (1.0 - f_col) * ia_ref[0, h:h + 1]
        so_ref[0, h] = s_new
        outs.append(jnp.sum(q_col * s_new, axis=0, keepdims=True))
    o = jnp.concatenate(outs, axis=0)
    ms = jnp.mean(o * o, axis=-1, keepdims=True)
    oa_ref[0] = o * lax.rsqrt(ms + RMS_EPS) * gn_ref[...] * gate


def hgrn_sample(zs3, lb, gn, s0):
    bsz = zs3.shape[0]
    zspec = lambda cb: pl.BlockSpec((1, HEADS, HEAD_DIM), lambda b, cb=cb: (b, cb, 0))
    sspec = pl.BlockSpec((1, HEADS, HEAD_DIM, HEAD_DIM), lambda b: (b, 0, 0, 0))
    return pl.pallas_call(
        _hgrn_sample_body,
        grid=(bsz,),
        in_specs=[zspec(0), zspec(1), zspec(2), zspec(3),
                  pl.BlockSpec((HEADS, HEAD_DIM), lambda b: (0, 0)),
                  pl.BlockSpec((1, HEAD_DIM), lambda b: (0, 0)),
                  sspec],
        out_specs=[pl.BlockSpec((1, HEADS, HEAD_DIM), lambda b: (b, 0, 0)), sspec],
        out_shape=[jax.ShapeDtypeStruct((bsz, HEADS, HEAD_DIM), F32),
                   jax.ShapeDtypeStruct(s0.shape, F32)],
        compiler_params=_cparams(("parallel",)),
        name="hgrn_sample",
    )(zs3, zs3, zs3, zs3, lb.reshape(HEADS, HEAD_DIM), gn.reshape(1, HEAD_DIM), s0)


_KEY_NEG_INF = np.int32(np.uint32(0x807FFFFF).astype(np.int64) - (1 << 32))
_INT_MIN = np.int32(-(1 << 31))


def _count(u_ref, n_tiles, pred):
    acc = pred(u_ref[0]).astype(F32)
    for t in range(1, n_tiles):
        acc = acc + pred(u_ref[t]).astype(F32)
    return jnp.sum(acc, axis=-1, keepdims=True)


def _topk_mask(score_ref, u_ref, write_tile, n_tiles, k):
    rows = score_ref.shape[1]
    for t in range(n_tiles):
        bits = pltpu.bitcast(score_ref[t], I32)
        u_ref[t] = jnp.where(bits < 0, bits ^ np.int32(0x7FFFFFFF), bits)

    kf = float(k)
    cnt = _count(u_ref, n_tiles, lambda u: u >= 0)
    lo = jnp.where(cnt >= kf, np.int32(0), _INT_MIN)

    def bit_step(i, lo):
        cand = lo | (np.int32(1) << (30 - i))
        cnt = _count(u_ref, n_tiles, lambda u: u >= cand)
        return jnp.where(cnt >= kf, cand, lo)

    lo = lax.fori_loop(0, 31, bit_step, lo)
    c_gt = _count(u_ref, n_tiles, lambda u: u > lo)
    c_eq = _count(u_ref, n_tiles, lambda u: u == lo)
    need = kf - c_gt
    real = lo > _KEY_NEG_INF
    excess = jnp.where(real & (c_eq > need), 1.0, 0.0)
    any_excess = jnp.max(excess) > 0.0

    @pl.when(jnp.logical_not(any_excess))
    def _():
        for t in range(n_tiles):
            u = u_ref[t]
            write_tile(t, jnp.where((u >= lo) & (u > _KEY_NEG_INF), 1.0, 0.0))

    @pl.when(any_excess)
    def _():
        upper = (lax.broadcasted_iota(I32, (LANES, LANES), 0)
                 <= lax.broadcasted_iota(I32, (LANES, LANES), 1)).astype(BF)
        seen = jnp.zeros((rows, 1), F32)
        for t in range(n_tiles):
            u = u_ref[t]
            eq = jnp.where(u == lo, 1.0, 0.0)
            prefix = seen + _dot(eq.astype(BF), upper)
            take = (u > lo) | ((u == lo) & (prefix <= need))
            write_tile(t, jnp.where(take & (u > _KEY_NEG_INF), 1.0, 0.0))
            seen = seen + jnp.sum(eq, axis=-1, keepdims=True)


IDX_KCHUNK = 512


def _indexer_prompt_body(qi_ref, tailq_ref, tail_ref, mask_ref, score_ref, u_ref, *, seq):
    j = pl.program_id(1)
    n_tiles = seq // LANES
    tiles_per_chunk = IDX_KCHUNK // LANES
    qi = qi_ref[0].astype(BF)
    w = tailq_ref[0][:, IDX_DIM:IDX_DIM + IDX_HEADS] * (IDX_DIM ** -0.5 * IDX_HEADS ** -0.5)

    for t in range(n_tiles):
        score_ref[t] = jnp.full((LANES, LANES), -jnp.inf, F32)

    q_pos = j * LANES + lax.broadcasted_iota(I32, (LANES, IDX_KCHUNK), 0)

    def chunk_step(ci, carry):
        k0 = pl.multiple_of(ci * IDX_KCHUNK, IDX_KCHUNK)
        kic = tail_ref[0, pl.ds(k0, IDX_KCHUNK), :][:, 0:IDX_DIM].astype(BF)
        acc = jnp.zeros((LANES, IDX_KCHUNK), F32)
        for h in range(IDX_HEADS):
            s = _dot_nt(qi[:, h * IDX_DIM:(h + 1) * IDX_DIM], kic)
            acc = acc + w[:, h:h + 1] * jnp.maximum(s, 0.0)
        k_pos = k0 + lax.broadcasted_iota(I32, (LANES, IDX_KCHUNK), 1)
        acc = jnp.where(k_pos <= q_pos, acc, -jnp.inf)
        for i in range(tiles_per_chunk):
            score_ref[ci * tiles_per_chunk + i] = acc[:, i * LANES:(i + 1) * LANES]
        return carry

    lax.fori_loop(0, j // tiles_per_chunk + 1, chunk_step, 0)

    def write_tile(t, m):
        mask_ref[0, 0, t] = m.astype(BF)

    _topk_mask(score_ref, u_ref, write_tile, n_tiles, TOPK)


def indexer_prompt(z3, k):
    bsz, seq = z3.shape[:2]
    nqb = seq // LANES
    tail_block = z3.shape[2] // LANES - 1
    assert k == TOPK
    return pl.pallas_call(
        functools.partial(_indexer_prompt_body, seq=seq),
        grid=(bsz, nqb),
        in_specs=[pl.BlockSpec((1, LANES, IDX_HEADS * IDX_DIM), lambda b, j: (b, j, 7)),
                  pl.BlockSpec((1, LANES, LANES), lambda b, j: (b, j, tail_block)),
                  pl.BlockSpec((1, seq, LANES), lambda b, j: (b, 0, tail_block))],
        out_specs=pl.BlockSpec((1, 1, nqb, LANES, LANES), lambda b, j: (b, j, 0, 0, 0)),
        out_shape=jax.ShapeDtypeStruct((bsz, nqb, nqb, LANES, LANES), BF),
        scratch_shapes=[pltpu.VMEM((nqb, LANES, LANES), F32),
                        pltpu.VMEM((nqb, LANES, LANES), I32)],
        compiler_params=_cparams(("parallel", "arbitrary")),
        name="indexer_prompt",
    )(z3, z3, z3)


def _attn_prompt_body(q_ref, k_ref, v_ref, mask_ref, bias_ref, o_ref, m_ref, l_ref, acc_ref):
    j = pl.program_id(1)
    scale = HEAD_DIM ** -0.5
    q = q_ref[0].astype(BF)
    m_ref[...] = jnp.full(m_ref.shape, NEG_BIG, F32)
    l_ref[...] = jnp.zeros(l_ref.shape, F32)
    acc_ref[...] = jnp.zeros(acc_ref.shape, F32)

    def process(kb, bias_idx):
        k0 = pl.multiple_of(kb * LANES, LANES)
        kblk = k_ref[0, pl.ds(k0, LANES), :]
        vblk = v_ref[0, pl.ds(k0, LANES), :]
        sel = mask_ref[0, 0, kb].astype(F32) > 0.0
        for h in range(HEADS):
            cols = slice(h * HEAD_DIM, (h + 1) * HEAD_DIM)
            lg = _dot_nt(q[:, cols], kblk[:, cols]) * scale + bias_ref[bias_idx, h]
            lg = jnp.where(sel, lg, NEG_BIG)
            m_old = m_ref[h]
            m_new = jnp.maximum(m_old, jnp.max(lg, axis=-1, keepdims=True))
            p = jnp.where(sel, jnp.exp(lg - m_new), 0.0)
            alpha = jnp.exp(m_old - m_new)
            l_ref[h] = alpha * l_ref[h] + jnp.sum(p, axis=-1, keepdims=True)
            acc_ref[:, cols] = alpha * acc_ref[:, cols] + _dot(p.astype(BF), vblk[:, cols])
            m_ref[h] = m_new

    def far_step(kb, carry):
        process(kb, 2)
        return carry

    lax.fori_loop(0, jnp.maximum(j - 1, 0), far_step, 0)

    @pl.when(j > 0)
    def _():
        process(j - 1, 1)

    process(j, 0)

    for h in range(HEADS):
        cols = slice(h * HEAD_DIM, (h + 1) * HEAD_DIM)
        o_ref[0, :, cols] = acc_ref[:, cols] / l_ref[h]


def attn_prompt(z3, kv_bf, mask, bias_tiles):
    bsz, seq = z3.shape[:2]
    nqb = seq // LANES
    width = HEADS * HEAD_DIM
    return pl.pallas_call(
        _attn_prompt_body,
        grid=(bsz, nqb),
        in_specs=[pl.BlockSpec((1, LANES, width), lambda b, j: (b, j, 4)),
                  pl.BlockSpec((1, seq, width), lambda b, j: (b, 0, 0)),
                  pl.BlockSpec((1, seq, width), lambda b, j: (b, 0, 1)),
                  pl.BlockSpec((1, 1, nqb, LANES, LANES), lambda b, j: (b, j, 0, 0, 0)),
                  pl.BlockSpec(bias_tiles.shape, lambda b, j: (0, 0, 0, 0))],
        out_specs=pl.BlockSpec((1, LANES, width), lambda b, j: (b, j, 0)),
        out_shape=jax.ShapeDtypeStruct((bsz, seq, width), F32),
        scratch_shapes=[pltpu.VMEM((HEADS, LANES, LANES), F32),
                        pltpu.VMEM((HEADS, LANES, LANES), F32),
                        pltpu.VMEM((LANES, width), F32)],
        compiler_params=_cparams(("parallel", "arbitrary")),
        name="attn_prompt",
    )(z3, kv_bf, kv_bf, mask, bias_tiles)


def _indexer_sample_body(pt_ref, qi_ref, wcol_ref, kinew_ref, *rest, n_pages):
    ki_refs = rest[:n_pages]
    out_ref = rest[n_pages]
    qi = qi_ref[0].astype(BF)
    w = wcol_ref[0] * (IDX_DIM ** -0.5 * IDX_HEADS ** -0.5)
    for i in range(n_pages):
        s = _dot_nt(qi, ki_refs[i][0].astype(BF))
        out_ref[i, 0] = jnp.sum(w * jnp.maximum(s, 0.0), axis=0, keepdims=True)
    kn = kinew_ref[0].astype(BF).astype(F32)
    sn = jnp.sum(qi.astype(F32) * kn, axis=-1, keepdims=True)
    new = jnp.sum(w * jnp.maximum(sn, 0.0), axis=0, keepdims=True)
    lane = lax.broadcasted_iota(I32, (1, LANES), 1)
    out_ref[n_pages, 0] = jnp.where(lane == 0, new, -jnp.inf)


def indexer_sample(page_table, qi3, wcol, ki_new, ki_pool):
    bsz, n_pages = page_table.shape
    ki_spec = lambda i: pl.BlockSpec((1, PAGE, IDX_DIM), lambda b, pt, i=i: (pt[b, i], 0, 0))
    grid_spec = pltpu.PrefetchScalarGridSpec(
        num_scalar_prefetch=1,
        grid=(bsz,),
        in_specs=[pl.BlockSpec((1, IDX_HEADS, IDX_DIM), lambda b, pt: (b, 0, 0)),
                  pl.BlockSpec((1, IDX_HEADS, 1), lambda b, pt: (b, 0, 0)),
                  pl.BlockSpec((1, 1, IDX_DIM), lambda b, pt: (b, 0, 0))]
                 + [ki_spec(i) for i in range(n_pages)],
        out_specs=pl.BlockSpec((n_pages + 1, 1, 1, LANES), lambda b, pt: (0, b, 0, 0)),
    )
    out = pl.pallas_call(
        functools.partial(_indexer_sample_body, n_pages=n_pages),
        grid_spec=grid_spec,
        out_shape=jax.ShapeDtypeStruct((n_pages + 1, bsz, 1, LANES), F32),
        compiler_params=_cparams(("arbitrary",)),
        name="indexer_sample",
    )(page_table, qi3, wcol, ki_new, *([ki_pool] * n_pages))
    return out.reshape(n_pages + 1, bsz, LANES)


def _select_body(score_ref, mask_ref, u_ref, *, n_tiles, k):
    def write_tile(t, m):
        mask_ref[t] = m

    _topk_mask(score_ref, u_ref, write_tile, n_tiles, k)


def select_topk(scores, k):
    n_tiles, rows, _ = scores.shape
    return pl.pallas_call(
        functools.partial(_select_body, n_tiles=n_tiles, k=k),
        grid=(1,),
        in_specs=[pl.BlockSpec(scores.shape, lambda i: (0, 0, 0))],
        out_specs=pl.BlockSpec(scores.shape, lambda i: (0, 0, 0)),
        out_shape=jax.ShapeDtypeStruct(scores.shape, F32),
        scratch_shapes=[pltpu.VMEM(scores.shape, I32)],
        compiler_params=_cparams(("arbitrary",)),
        name="select_topk",
    )(scores)


def _block_diag(x8):
    row = lax.broadcasted_iota(I32, x8.shape, 0)
    return jnp.concatenate([jnp.where(row == h, x8, 0.0) for h in range(HEADS)], axis=1)


def _diag_blocks(x):
    row = lax.broadcasted_iota(I32, (HEADS, HEAD_DIM), 0)
    out = jnp.zeros((HEADS, HEAD_DIM), F32)
    for h in range(HEADS):
        out = out + jnp.where(row == h, x[:, h * HEAD_DIM:(h + 1) * HEAD_DIM], 0.0)
    return out


def _attn_sample_body(pt_ref, q_ref, kn_ref, vn_ref, k_ref, v_ref, mask_ref, mnew_ref, bias_ref,
                      o_ref, m_ref, l_ref, acc_ref, *, n_pages):
    p = pl.program_id(1)
    scale = HEAD_DIM ** -0.5
    q8 = q_ref[0]

    @pl.when(p == 0)
    def _():
        m_ref[...] = jnp.full(m_ref.shape, NEG_BIG, F32)
        l_ref[...] = jnp.zeros(l_ref.shape, F32)
        acc_ref[...] = jnp.zeros(acc_ref.shape, F32)

    bias = jnp.where(p == n_pages - 1, bias_ref[1], bias_ref[0])
    lg = _dot_nt(_block_diag(q8).astype(BF), k_ref[0].astype(BF)) * scale + bias
    sel = mask_ref[0, 0] > 0.0
    lg = jnp.where(sel, lg, NEG_BIG)
    m_old = m_ref[...]
    m_new = jnp.maximum(m_old, jnp.max(lg, axis=-1, keepdims=True))
    pr = jnp.where(sel, jnp.exp(lg - m_new), 0.0)
    alpha = jnp.exp(m_old - m_new)
    l_ref[...] = alpha * l_ref[...] + jnp.sum(pr, axis=-1, keepdims=True)
    acc_ref[...] = alpha[:, 0:1] * acc_ref[...] + _dot(pr.astype(BF), v_ref[0].astype(BF))
    m_ref[...] = m_new

    @pl.when(p == n_pages - 1)
    def _():
        qr = q8.astype(BF).astype(F32)
        kr = kn_ref[0].astype(BF).astype(F32)
        lgn = jnp.sum(qr * kr, axis=-1, keepdims=True) * scale + bias_ref[2][:, 0:1]
        seln = mnew_ref[0, 0][:, 0:1] > 0.0
        lgn = jnp.where(seln, lgn, NEG_BIG)
        m_old = m_ref[...][:, 0:1]
        m_fin = jnp.maximum(m_old, lgn)
        pn = jnp.where(seln, jnp.exp(lgn - m_fin), 0.0)
        alpha = jnp.exp(m_old - m_fin)
        l_fin = alpha * l_ref[...][:, 0:1] + pn
        vr = vn_ref[0].astype(BF).astype(F32)
        pnr = pn.astype(BF).astype(F32)
        out = alpha * _diag_blocks(acc_ref[...]) + pnr * vr
        o_ref[0] = out / l_fin


def attn_sample(page_table, zs3, k_pool, v_pool, mask4, bias_rows):
    bsz, n_pages = page_table.shape
    width = HEADS * HEAD_DIM
    zspec = lambda cb: pl.BlockSpec((1, HEADS, HEAD_DIM), lambda b, p, pt, cb=cb: (b, cb, 0))
    pool_spec = pl.BlockSpec((1, PAGE, width), lambda b, p, pt: (pt[b, p], 0, 0))
    grid_spec = pltpu.PrefetchScalarGridSpec(
        num_scalar_prefetch=1,
        grid=(bsz, n_pages),
        in_specs=[zspec(4), zspec(5), zspec(6), pool_spec, pool_spec,
                  pl.BlockSpec((1, 1, 1, LANES), lambda b, p, pt: (b, p, 0, 0)),
                  pl.BlockSpec((1, 1, 1, LANES), lambda b, p, pt: (b, n_pages, 0, 0)),
                  pl.BlockSpec(bias_rows.shape, lambda b, p, pt: (0, 0, 0))],
        out_specs=pl.BlockSpec((1, HEADS, HEAD_DIM), lambda b, p, pt: (b, 0, 0)),
        scratch_shapes=[pltpu.VMEM((HEADS, LANES), F32),
                        pltpu.VMEM((HEADS, LANES), F32),
                        pltpu.VMEM((HEADS, width), F32)],
    )
    return pl.pallas_call(
        functools.partial(_attn_sample_body, n_pages=n_pages),
        grid_spec=grid_spec,
        out_shape=jax.ShapeDtypeStruct((bsz, HEADS, HEAD_DIM), F32),
        compiler_params=_cparams(("arbitrary", "arbitrary")),
        name="attn_sample",
    )(page_table, zs3, zs3, zs3, k_pool, v_pool, mask4, mask4, bias_rows)


def _bucket_table(max_dist):
    exact = REL_BUCKETS // 2
    d = np.arange(max_dist + 1)
    df = np.maximum(d, 1).astype(np.float32)
    far = exact + (np.log(df / exact) / np.float32(math.log(REL_MAX_DIST / exact))
                   * (REL_BUCKETS - exact)).astype(np.int32)
    return np.where(d < exact, d, np.minimum(far, REL_BUCKETS - 1)).astype(np.int32)


def _bias_tables(rel_bias):
    tab = _bucket_table(2 * LANES)
    assert np.all(tab[REL_MAX_DIST:] == REL_BUCKETS - 1)
    i = np.arange(LANES)
    dist0 = np.maximum(i[:, None] - i[None, :], 0)
    dist1 = LANES + i[:, None] - i[None, :]
    far = np.full((LANES, LANES), REL_BUCKETS - 1)
    idx = np.stack([tab[dist0], tab[dist1], far])
    tiles = jnp.transpose(rel_bias[idx], (0, 3, 1, 2))
    last_page = tab[LANES - i]
    rows = jnp.stack([jnp.broadcast_to(rel_bias[REL_BUCKETS - 1][:, None], (HEADS, LANES)),
                      rel_bias[last_page].T,
                      jnp.broadcast_to(rel_bias[0][:, None], (HEADS, LANES))])
    return tiles.astype(F32), rows.astype(F32)


def _conv_prompt_body(bg_ref, cg_ref, xt_ref, w_ref, v_ref, st_ref, carry_ref, *, tblock):
    t = pl.program_id(1)

    @pl.when(t == 0)
    def _():
        carry_ref[...] = jnp.zeros_like(carry_ref)

    u = cg_ref[0] * xt_ref[0]
    row = lax.broadcasted_iota(I32, u.shape, 0)
    c0 = carry_ref[0:1]
    c1 = carry_ref[1:2]
    u1 = jnp.where(row == 0, c1, pltpu.roll(u, 1, axis=0))
    u2 = jnp.where(row == 0, c0, jnp.where(row == 1, c1, pltpu.roll(u, 2, axis=0)))
    conv = w_ref[0:1] * u2 + w_ref[1:2] * u1 + w_ref[2:3] * u
    v_ref[0] = bg_ref[0] * conv
    last = u[tblock - 2:tblock]
    carry_ref[0:2] = last

    @pl.when(t == pl.num_programs(1) - 1)
    def _():
        st_ref[0] = last


def conv_prompt(zc3, w_conv, tblock=256):
    bsz, seq = zc3.shape[:2]
    c = zc3.shape[2] // 3
    zspec = lambda cb: pl.BlockSpec((1, tblock, c), lambda b, t, cb=cb: (b, t, cb))
    return pl.pallas_call(
        functools.partial(_conv_prompt_body, tblock=tblock),
        grid=(bsz, seq // tblock),
        in_specs=[zspec(0), zspec(1), zspec(2), pl.BlockSpec((3, c), lambda b, t: (0, 0))],
        out_specs=[pl.BlockSpec((1, tblock, c), lambda b, t: (b, t, 0)),
                   pl.BlockSpec((1, 2, c), lambda b, t: (b, 0, 0))],
        out_shape=[jax.ShapeDtypeStruct((bsz, seq, c), F32),
                   jax.ShapeDtypeStruct((bsz, 2, c), F32)],
        scratch_shapes=[pltpu.VMEM((8, c), F32)],
        compiler_params=_cparams(("parallel", "arbitrary")),
        name="conv_prompt",
    )(zc3, zc3, zc3, w_conv)


def _conv_sample_body(bg_ref, cg_ref, xt_ref, w_ref, s0_ref, s1_ref, v_ref, n0_ref, n1_ref):
    u = cg_ref[...] * xt_ref[...]
    conv = w_ref[0:1] * s0_ref[...] + w_ref[1:2] * s1_ref[...] + w_ref[2:3] * u
    v_ref[...] = bg_ref[...] * conv
    n0_ref[...] = s1_ref[...]
    n1_ref[...] = u


def conv_sample(zc, w_conv, s0, s1):
    bsz = zc.shape[0]
    c = zc.shape[1] // 3
    zspec = lambda cb: pl.BlockSpec((bsz, c), lambda i, cb=cb: (0, cb))
    full = pl.BlockSpec((bsz, c), lambda i: (0, 0))
    return pl.pallas_call(
        _conv_sample_body,
        grid=(1,),
        in_specs=[zspec(0), zspec(1), zspec(2), pl.BlockSpec((3, c), lambda i: (0, 0)), full, full],
        out_specs=[full, full, full],
        out_shape=[jax.ShapeDtypeStruct((bsz, c), F32)] * 3,
        compiler_params=_cparams(("arbitrary",)),
        name="conv_sample",
    )(zc, zc, zc, w_conv, s0, s1)


def _router_body(x_ref, g_ref, wr_ref, br_ref, h_ref, route_ref):
    x = x_ref[...]
    ms = jnp.mean(x * x, axis=-1, keepdims=True)
    h = x * lax.rsqrt(ms + RMS_EPS) * g_ref[...]
    h_ref[...] = h.astype(BF)
    logits = jnp.dot(h, wr_ref[...], preferred_element_type=F32,
                     precision=lax.Precision.HIGHEST) + br_ref[...]
    lane = lax.broadcasted_iota(I32, logits.shape, 1)
    big = np.int32(1 << 20)
    lg = jnp.where(lane < N_GROUPS, logits, -jnp.inf)
    g_max = jnp.max(lg, axis=-1, keepdims=True)
    g_idx = jnp.min(jnp.where(lg == g_max, lane, big), axis=-1, keepdims=True)
    g_w = 1.0 / jnp.sum(jnp.exp(lg - g_max), axis=-1, keepdims=True)
    first = N_GROUPS + EXP_PER_GROUP * g_idx
    le = jnp.where((lane >= first) & (lane < first + EXP_PER_GROUP), logits, -jnp.inf)
    l1 = jnp.max(le, axis=-1, keepdims=True)
    i1 = jnp.min(jnp.where(le == l1, lane, big), axis=-1, keepdims=True)
    le2 = jnp.where(lane == i1, -jnp.inf, le)
    l2 = jnp.max(le2, axis=-1, keepdims=True)
    i2 = jnp.min(jnp.where(le2 == l2, lane, big), axis=-1, keepdims=True)
    r = jnp.exp(l2 - l1)
    w1 = g_w / (1.0 + r)
    w2 = g_w * r / (1.0 + r)
    e1 = (i1 - N_GROUPS).astype(F32)
    e2 = (i2 - N_GROUPS).astype(F32)
    route_ref[...] = jnp.where(lane == 0, e1, jnp.where(lane == 1, e2,
                               jnp.where(lane == 2, w1, jnp.where(lane == 3, w2, 0.0))))


def moe_router(x, g, wr, br, tm):
    m, k = x.shape
    return pl.pallas_call(
        _router_body,
        grid=(m // tm,),
        in_specs=[pl.BlockSpec((tm, k), lambda i: (i, 0)),
                  pl.BlockSpec((1, k), lambda i: (0, 0)),
                  pl.BlockSpec((k, LANES), lambda i: (0, 0)),
                  pl.BlockSpec((1, LANES), lambda i: (0, 0))],
        out_specs=[pl.BlockSpec((tm, k), lambda i: (i, 0)),
                   pl.BlockSpec((tm, LANES), lambda i: (i, 0))],
        out_shape=[jax.ShapeDtypeStruct((m, k), BF),
                   jax.ShapeDtypeStruct((m, LANES), F32)],
        compiler_params=_cparams(("parallel",)),
        name="moe_router",
    )(x, g.reshape(1, k), wr, br)


def _experts_body(te_ref, nu_ref, h_ref, gate_ref, wg_ref, wu_ref, wd_ref, o_ref):
    i = pl.program_id(0)

    @pl.when(i < nu_ref[0])
    def _():
        h = h_ref[...]
        a = _dot(h, wg_ref[0])
        b = _dot(h, wu_ref[0])
        hid = a * _sigmoid(a) * b * gate_ref[...]
        o_ref[...] = _dot(hid.astype(BF), wd_ref[0])

    @pl.when(i >= nu_ref[0])
    def _():
        o_ref[...] = jnp.zeros_like(o_ref)


def moe_experts(tile_expert, n_used, hs, gate_col, wg_bf, wu_bf, wd_bf):
    p, k = hs.shape
    f = wg_bf.shape[2]
    n_tiles = p // MOE_TILE
    grid_spec = pltpu.PrefetchScalarGridSpec(
        num_scalar_prefetch=2,
        grid=(n_tiles,),
        in_specs=[pl.BlockSpec((MOE_TILE, k), lambda i, te, nu: (i, 0)),
                  pl.BlockSpec((MOE_TILE, 1), lambda i, te, nu: (i, 0)),
                  pl.BlockSpec((1, k, f), lambda i, te, nu: (te[i], 0, 0)),
                  pl.BlockSpec((1, k, f), lambda i, te, nu: (te[i], 0, 0)),
                  pl.BlockSpec((1, f, k), lambda i, te, nu: (te[i], 0, 0))],
        out_specs=pl.BlockSpec((MOE_TILE, k), lambda i, te, nu: (i, 0)),
    )
    return pl.pallas_call(
        _experts_body,
        grid_spec=grid_spec,
        out_shape=jax.ShapeDtypeStruct((p, k), F32),
        compiler_params=_cparams(("arbitrary",)),
        name="moe_experts",
    )(tile_expert, n_used, hs, gate_col, wg_bf, wu_bf, wd_bf)


def hier_moe(xs, tms, g, wrg, brg, wre, bre, wg_bf, wu_bf, wd_bf):
    d = xs[0].shape[1]
    m = sum(x.shape[0] for x in xs)
    wr = jnp.zeros((d, LANES), F32)
    wr = wr.at[:, :N_GROUPS].set(wrg).at[:, N_GROUPS:N_GROUPS + N_EXPERTS].set(wre.reshape(d, N_EXPERTS))
    br = jnp.zeros((1, LANES), F32)
    br = br.at[0, :N_GROUPS].set(brg).at[0, N_GROUPS:N_GROUPS + N_EXPERTS].set(bre.reshape(N_EXPERTS))
    routed = [moe_router(x, g, wr, br, tm) for x, tm in zip(xs, tms)]
    h_bf = jnp.concatenate([r[0] for r in routed], axis=0)
    route = jnp.concatenate([r[1] for r in routed], axis=0)

    eid = route[:, 0:2].astype(I32).reshape(-1)
    gate = route[:, 2:4].reshape(-1)
    onehot = (eid[:, None] == jnp.arange(N_EXPERTS, dtype=I32)[None, :]).astype(I32)
    rank = jnp.take_along_axis(jnp.cumsum(onehot, axis=0) - onehot, eid[:, None], axis=1)[:, 0]
    counts = jnp.sum(onehot, axis=0)
    padded = ((counts + MOE_TILE - 1) // MOE_TILE) * MOE_TILE
    ends = jnp.cumsum(padded)
    pos = (ends - padded)[eid] + rank
    n_rows = 2 * m + N_EXPERTS * MOE_TILE
    n_rows = -(-n_rows // MOE_TILE) * MOE_TILE
    src = jnp.zeros((n_rows,), I32).at[pos].set(jnp.arange(2 * m, dtype=I32) // 2)
    gate_sorted = jnp.zeros((n_rows,), F32).at[pos].set(gate)
    tile_start = jnp.arange(n_rows // MOE_TILE, dtype=I32) * MOE_TILE
    tile_expert = jnp.minimum(jnp.searchsorted(ends, tile_start, side="right"),
                              N_EXPERTS - 1).astype(I32)
    n_used = (ends[-1] // MOE_TILE).astype(I32).reshape(1)

    hs = jnp.take(h_bf, src, axis=0)
    out = moe_experts(tile_expert, n_used, hs, gate_sorted[:, None], wg_bf, wu_bf, wd_bf)
    pos2 = pos.reshape(m, 2)
    res, r0 = [], 0
    for x in xs:
        pp = pos2[r0:r0 + x.shape[0]]
        res.append(x + jnp.take(out, pp[:, 0], axis=0) + jnp.take(out, pp[:, 1], axis=0))
        r0 += x.shape[0]
    return res


def kernel(x_prompt, x_sample, cache_k, cache_v, cache_ki, state_hgrn, state_conv, page_table,
           norm_mix_g, norm_ffn_g, final_g, w_in_even, w_out_even, hgrn_lb_logits, hgrn_norm_g,
           rel_bias, w_in_conv, w_conv, w_out_conv, w_router_g, b_router_g, w_router_e,
           b_router_e, w_gate, w_up, w_down):
    bsz, seq, d = x_prompt.shape
    dec = x_sample.shape[0]
    n_p = bsz * seq
    tm_p, tm_s = 512, 128
    tms = (tm_p, tm_s)
    assert n_p % tm_p == 0 and dec % tm_s == 0 and x_sample.shape[1] == 1
    width = HEADS * HEAD_DIM
    even_in = w_in_even.shape[2]
    even_pad = -(-even_in // LANES) * LANES
    tail0 = even_pad - LANES
    n_pool = cache_k.shape[1]

    xp = x_prompt.reshape(n_p, d)
    xs = x_sample.reshape(dec, d)

    lbs = jnp.cumsum(jax.nn.softmax(hgrn_lb_logits.astype(F32), axis=0), axis=0)[:-1]
    bias_tiles, bias_rows = _bias_tables(rel_bias)

    w_in = jnp.pad(w_in_even[0], ((0, 0), (0, even_pad - even_in))).astype(BF)
    zp = norm_matmul(xp, norm_mix_g[0], w_in, tm_p, 640)
    zs = norm_matmul(xs, norm_mix_g[0], w_in, tm_s, 640)
    zp3 = zp.reshape(bsz, seq, even_pad)
    zs3 = zs.reshape(dec, even_pad // LANES, LANES)

    oa_p, hgrn_p = hgrn_prompt(zp3, lbs[0], hgrn_norm_g[0])
    oa_s, hgrn_s = hgrn_sample(zs3, lbs[0], hgrn_norm_g[0], state_hgrn[0])

    mask_p = indexer_prompt(zp3, TOPK)
    kv_bf = zp3[:, :, 5 * width:7 * width].astype(BF)
    ob_p = attn_prompt(zp3, kv_bf, mask_p, bias_tiles)

    qi3 = zs[:, 7 * width:8 * width].reshape(dec, IDX_HEADS, IDX_DIM)
    ki_new = zs[:, tail0:tail0 + IDX_DIM].reshape(dec, 1, IDX_DIM)
    wcol = zs[:, tail0 + IDX_DIM:tail0 + IDX_DIM + IDX_HEADS].reshape(dec, IDX_HEADS, 1)
    scores_s = indexer_sample(page_table, qi3, wcol, ki_new, cache_ki[0])
    mask_s = select_topk(scores_s, TOPK)
    mask_s4 = jnp.transpose(mask_s, (1, 0, 2))[:, :, None, :]
    ob_s = attn_sample(page_table, zs3, cache_k[0].reshape(n_pool, PAGE, width),
                       cache_v[0].reshape(n_pool, PAGE, width), mask_s4, bias_rows)

    w_out = w_out_even[0].astype(BF)
    w_out_ab = [w_out[:width], w_out[width:]]
    xp = matmul_residual([oa_p.reshape(n_p, width), ob_p.reshape(n_p, width)], w_out_ab, xp, tm_p, 512)
    xs = matmul_residual([oa_s.reshape(dec, width), ob_s.reshape(dec, width)], w_out_ab, xs, tm_s, 512)

    xp, xs = hier_moe([xp, xs], tms, norm_ffn_g[0], w_router_g[0], b_router_g[0], w_router_e[0],
                      b_router_e[0], w_gate[0].astype(BF), w_up[0].astype(BF), w_down[0].astype(BF))

    w_in_c = w_in_conv[0].astype(BF)
    zcp = norm_matmul(xp, norm_mix_g[1], w_in_c, tm_p, 512)
    zcs = norm_matmul(xs, norm_mix_g[1], w_in_c, tm_s, 512)
    cw = zcp.shape[1] // 3
    v_p, conv_p = conv_prompt(zcp.reshape(bsz, seq, 3 * cw), w_conv[0])
    v_s, cs0, cs1 = conv_sample(zcs, w_conv[0], state_conv[0, :, 0], state_conv[0, :, 1])
    w_out_c = [w_out_conv[0].astype(BF)]
    xp = matmul_residual([v_p.reshape(n_p, cw)], w_out_c, xp, tm_p, 512)
    xs = matmul_residual([v_s], w_out_c, xs, tm_s, 512)

    xp, xs = hier_moe([xp, xs], tms, norm_ffn_g[1], w_router_g[1], b_router_g[1], w_router_e[1],
                      b_router_e[1], w_gate[1].astype(BF), w_up[1].astype(BF), w_down[1].astype(BF))

    yp = rmsnorm(xp, final_g, tm_p)
    ys = rmsnorm(xs, final_g, tm_s)

    kcol, vcol = 5 * width, 6 * width
    heads = lambda a, n: a.reshape(1, *n, HEADS, HEAD_DIM)
    return (yp.reshape(bsz, seq, d),
            ys.reshape(dec, 1, d),
            heads(zp[:, kcol:kcol + width], (bsz, seq)),
            heads(zp[:, vcol:vcol + width], (bsz, seq)),
            zp[:, tail0:tail0 + IDX_DIM].reshape(1, bsz, seq, IDX_DIM),
            hgrn_p[None],
            conv_p[None],
            heads(zs[:, kcol:kcol + width], (dec, 1)),
            heads(zs[:, vcol:vcol + width], (dec, 1)),
            zs[:, tail0:tail0 + IDX_DIM].reshape(1, dec, 1, IDX_DIM),
            hgrn_s[None],
            jnp.stack([cs0, cs1], axis=1)[None])
```

```python
import functools
import math

import numpy as np
import jax
import jax.numpy as jnp
from jax import lax
from jax.experimental import pallas as pl
from jax.experimental.pallas import tpu as pltpu

F32 = jnp.float32
BF = jnp.bfloat16
I32 = jnp.int32

RMS_EPS = 1e-6
LANES = 128
NEG_BIG = -1e30
VMEM_LIMIT = 56 * 1024 * 1024

D_MODEL = 2048
HEADS = 8
HEAD_DIM = 128
IDX_HEADS = 16
IDX_DIM = 64
TOPK = 256
REL_BUCKETS = 32
REL_MAX_DIST = 128
N_EXPERTS = 16
EXP_PER_GROUP = 4
N_GROUPS = 4
D_EXPERT = 512
PAGE = 128

HGRN_CHUNK = 64
MOE_TILE = 256


def _cparams(sem):
    return pltpu.CompilerParams(dimension_semantics=sem, vmem_limit_bytes=VMEM_LIMIT)


def _dot(a, b):
    return jnp.dot(a, b, preferred_element_type=F32)


def _dot_nt(a, b):
    return lax.dot_general(a, b, (((1,), (1,)), ((), ())), preferred_element_type=F32)


def _dot_tn(a, b):
    return lax.dot_general(a, b, (((0,), (0,)), ((), ())), preferred_element_type=F32)


def _sigmoid(x):
    return 1.0 / (1.0 + jnp.exp(-x))


def _norm_mm_body(x_ref, g_ref, w_ref, o_ref, h_ref):
    @pl.when(pl.program_id(1) == 0)
    def _():
        x = x_ref[...]
        ms = jnp.mean(x * x, axis=-1, keepdims=True)
        h_ref[...] = (x * lax.rsqrt(ms + RMS_EPS) * g_ref[...]).astype(BF)

    o_ref[...] = _dot(h_ref[...], w_ref[...])


def norm_matmul(x, g, w_bf, tm, tn):
    m, k = x.shape
    n = w_bf.shape[1]
    return pl.pallas_call(
        _norm_mm_body,
        grid=(m // tm, n // tn),
        in_specs=[pl.BlockSpec((tm, k), lambda i, j: (i, 0)),
                  pl.BlockSpec((1, k), lambda i, j: (0, 0)),
                  pl.BlockSpec((k, tn), lambda i, j: (0, j))],
        out_specs=pl.BlockSpec((tm, tn), lambda i, j: (i, j)),
        out_shape=jax.ShapeDtypeStruct((m, n), F32),
        scratch_shapes=[pltpu.VMEM((tm, k), BF)],
        compiler_params=_cparams(("parallel", "arbitrary")),
        name="norm_matmul",
    )(x, g.reshape(1, k), w_bf)


def _mm_res_body(*refs, n_lhs):
    a_refs = refs[:n_lhs]
    w_refs = refs[n_lhs:2 * n_lhs]
    r_ref = refs[2 * n_lhs]
    o_ref = refs[2 * n_lhs + 1]
    s_refs = refs[2 * n_lhs + 2:]

    @pl.when(pl.program_id(1) == 0)
    def _():
        for a_ref, s_ref in zip(a_refs, s_refs):
            s_ref[...] = a_ref[...].astype(BF)

    acc = r_ref[...]
    for s_ref, w_ref in zip(s_refs, w_refs):
        acc = acc + _dot(s_ref[...], w_ref[...])
    o_ref[...] = acc


def matmul_residual(lhs, ws_bf, res, tm, tn):
    m, n = res.shape
    n_lhs = len(lhs)
    in_specs = ([pl.BlockSpec((tm, a.shape[1]), lambda i, j: (i, 0)) for a in lhs]
                + [pl.BlockSpec((w.shape[0], tn), lambda i, j: (0, j)) for w in ws_bf]
                + [pl.BlockSpec((tm, tn), lambda i, j: (i, j))])
    return pl.pallas_call(
        functools.partial(_mm_res_body, n_lhs=n_lhs),
        grid=(m // tm, n // tn),
        in_specs=in_specs,
        out_specs=pl.BlockSpec((tm, tn), lambda i, j: (i, j)),
        out_shape=jax.ShapeDtypeStruct((m, n), F32),
        scratch_shapes=[pltpu.VMEM((tm, a.shape[1]), BF) for a in lhs],
        compiler_params=_cparams(("parallel", "arbitrary")),
        name="matmul_residual",
    )(*lhs, *ws_bf, res)


def _rmsnorm_body(x_ref, g_ref, o_ref):
    x = x_ref[...]
    ms = jnp.mean(x * x, axis=-1, keepdims=True)
    o_ref[...] = x * lax.rsqrt(ms + RMS_EPS) * g_ref[...]


def rmsnorm(x, g, tm):
    m, k = x.shape
    return pl.pallas_call(
        _rmsnorm_body,
        grid=(m // tm,),
        in_specs=[pl.BlockSpec((tm, k), lambda i: (i, 0)),
                  pl.BlockSpec((1, k), lambda i: (0, 0))],
        out_specs=pl.BlockSpec((tm, k), lambda i: (i, 0)),
        out_shape=jax.ShapeDtypeStruct((m, k), F32),
        compiler_params=_cparams(("parallel",)),
        name="rmsnorm",
    )(x, g.reshape(1, k))


def _hgrn_static(c):
    levels = []
    m = 1
    while m < c:
        levels.append(m)
        m *= 2
    t = np.arange(c)
    rows = [t[None, :] <= t[:, None]]
    masks = [np.eye(c, dtype=bool)]
    for m in levels:
        blk = t // (2 * m)
        pos = t % (2 * m)
        bnd = blk * 2 * m + m - 1
        right = pos >= m
        left = pos < m
        rows.append((t[None, :] > bnd[:, None]) & (t[None, :] <= t[:, None]) & right[:, None])
        rows.append((t[None, :] > t[:, None]) & (t[None, :] <= bnd[:, None]) & left[:, None])
        masks.append((blk[:, None] == blk[None, :]) & right[:, None] & left[None, :])
    m_all = np.concatenate(rows, 0).astype(np.float32)
    masks = np.stack(masks).astype(np.float32)
    return m_all, masks, len(levels)


def _hgrn_gates(qa, fa, lb):
    f = lb + (1.0 - lb) * _sigmoid(fa)
    q = qa * _sigmoid(qa)
    return q, f


def _hgrn_prompt_body(qa_ref, fa_ref, ia_ref, ga_ref, lb_ref, gn_ref, mall_ref, masks_ref,
                      oa_ref, st_out_ref, st_ref, *, chunk, tblock, n_levels):
    c = chunk
    t_idx = pl.program_id(1)

    @pl.when(t_idx == 0)
    def _():
        st_ref[...] = jnp.zeros_like(st_ref)

    mall = mall_ref[...]
    gn = gn_ref[...]

    def chunk_step(ci, carry):
        r0 = pl.multiple_of(ci * c, c)
        for h in range(HEADS):
            cols = slice(h * HEAD_DIM, (h + 1) * HEAD_DIM)
            qa = qa_ref[0, pl.ds(r0, c), cols]
            fa = fa_ref[0, pl.ds(r0, c), cols]
            ia = ia_ref[0, pl.ds(r0, c), cols]
            ga = ga_ref[0, pl.ds(r0, c), cols]
            q, f = _hgrn_gates(qa, fa, lb_ref[:, cols])
            k = 1.0 - f
            g = jnp.log(f)
            g_hi = g.astype(BF)
            g_lo = (g - g_hi.astype(F32)).astype(BF)
            r = _dot(mall, g_hi) + _dot(mall, g_lo)
            b = r[0:c]
            b_last = b[c - 1:c]
            st = st_ref[h]
            v = ia.astype(BF)
            o = _dot_nt((q * jnp.exp(b)).astype(BF), st.astype(BF))
            a = masks_ref[0] * _dot_nt(q.astype(BF), k.astype(BF))
            for li in range(n_levels):
                e = r[(1 + 2 * li) * c:(2 + 2 * li) * c]
                fk = r[(2 + 2 * li) * c:(3 + 2 * li) * c]
                a = a + masks_ref[1 + li] * _dot_nt((q * jnp.exp(e)).astype(BF),
                                                    (k * jnp.exp(fk)).astype(BF))
            o = o + _dot(a.astype(BF), v)
            k_st = (k * jnp.exp(b_last - b)).astype(BF)
            st_ref[h] = st * jnp.exp(b_last) + _dot_tn(v, k_st)
            ms = jnp.mean(o * o, axis=-1, keepdims=True)
            y = o * lax.rsqrt(ms + RMS_EPS) * gn * (ga * _sigmoid(ga))
            oa_ref[0, pl.ds(r0, c), cols] = y
        return carry

    lax.fori_loop(0, tblock // c, chunk_step, 0)

    @pl.when(t_idx == pl.num_programs(1) - 1)
    def _():
        st_out_ref[0] = st_ref[...]


def hgrn_prompt(z3, lb, gn, tblock=256, chunk=HGRN_CHUNK):
    bsz, seq = z3.shape[:2]
    width = HEADS * HEAD_DIM
    m_all, masks, n_levels = _hgrn_static(chunk)
    zspec = lambda cb: pl.BlockSpec((1, tblock, width), lambda b, t, cb=cb: (b, t, cb))
    oa, st = pl.pallas_call(
        functools.partial(_hgrn_prompt_body, chunk=chunk, tblock=tblock, n_levels=n_levels),
        grid=(bsz, seq // tblock),
        in_specs=[zspec(0), zspec(1), zspec(2), zspec(3),
                  pl.BlockSpec((1, width), lambda b, t: (0, 0)),
                  pl.BlockSpec((1, HEAD_DIM), lambda b, t: (0, 0)),
                  pl.BlockSpec(m_all.shape, lambda b, t: (0, 0)),
                  pl.BlockSpec(masks.shape, lambda b, t: (0, 0, 0))],
        out_specs=[pl.BlockSpec((1, tblock, width), lambda b, t: (b, t, 0)),
                   pl.BlockSpec((1, HEADS, HEAD_DIM, HEAD_DIM), lambda b, t: (b, 0, 0, 0))],
        out_shape=[jax.ShapeDtypeStruct((bsz, seq, width), F32),
                   jax.ShapeDtypeStruct((bsz, HEADS, HEAD_DIM, HEAD_DIM), F32)],
        scratch_shapes=[pltpu.VMEM((HEADS, HEAD_DIM, HEAD_DIM), F32)],
        compiler_params=_cparams(("parallel", "arbitrary")),
        name="hgrn_prompt",
    )(z3, z3, z3, z3, lb.reshape(1, width), gn.reshape(1, HEAD_DIM),
      jnp.asarray(m_all, BF), jnp.asarray(masks, F32))
    return oa, jnp.swapaxes(st, -1, -2)


def _col(row, eye):
    return jnp.sum(eye * row, axis=1, keepdims=True)


def _hgrn_sample_body(qa_ref, fa_ref, ia_ref, ga_ref, lb_ref, gn_ref, s_ref, oa_ref, so_ref):
    eye = (lax.broadcasted_iota(I32, (HEAD_DIM, HEAD_DIM), 0)
           == lax.broadcasted_iota(I32, (HEAD_DIM, HEAD_DIM), 1)).astype(F32)
    q8, f8 = _hgrn_gates(qa_ref[0], fa_ref[0], lb_ref[...])
    ga = ga_ref[0]
    gate = ga * _sigmoid(ga)
    outs = []
    for h in range(HEADS):
        f_col = _col(f8[h:h + 1], eye)
        q_col = _col(q8[h:h + 1], eye)
        s_new = f_col * s_ref[0, h] + (1.0 - f_col) * ia_ref[0, h:h + 1]
        so_ref[0, h] = s_new
        outs.append(jnp.sum(q_col * s_new, axis=0, keepdims=True))
    o = jnp.concatenate(outs, axis=0)
    ms = jnp.mean(o * o, axis=-1, keepdims=True)
    oa_ref[0] = o * lax.rsqrt(ms + RMS_EPS) * gn_ref[...] * gate


def hgrn_sample(zs3, lb, gn, s0):
    bsz = zs3.shape[0]
    zspec = lambda cb: pl.BlockSpec((1, HEADS, HEAD_DIM), lambda b, cb=cb: (b, cb, 0))
    sspec = pl.BlockSpec((1, HEADS, HEAD_DIM, HEAD_DIM), lambda b: (b, 0, 0, 0))
    return pl.pallas_call(
        _hgrn_sample_body,
        grid=(bsz,),
        in_specs=[zspec(0), zspec(1), zspec(2), zspec(3),
                  pl.BlockSpec((HEADS, HEAD_DIM), lambda b: (0, 0)),
                  pl.BlockSpec((1, HEAD_DIM), lambda b: (0, 0)),
                  sspec],
        out_specs=[pl.BlockSpec((1, HEADS, HEAD_DIM), lambda b: (b, 0, 0)), sspec],
        out_shape=[jax.ShapeDtypeStruct((bsz, HEADS, HEAD_DIM), F32),
                   jax.ShapeDtypeStruct(s0.shape, F32)],
        compiler_params=_cparams(("parallel",)),
        name="hgrn_sample",
    )(zs3, zs3, zs3, zs3, lb.reshape(HEADS, HEAD_DIM), gn.reshape(1, HEAD_DIM), s0)


_KEY_NEG_INF = np.int32(np.uint32(0x807FFFFF).astype(np.int64) - (1 << 32))
_INT_MIN = np.int32(-(1 << 31))


def _count(u_ref, n_tiles, pred):
    acc = pred(u_ref[0]).astype(F32)
    for t in range(1, n_tiles):
        acc = acc + pred(u_ref[t]).astype(F32)
    return jnp.sum(acc, axis=-1, keepdims=True)


def _topk_mask(score_ref, u_ref, write_tile, n_tiles, k):
    rows = score_ref.shape[1]
    for t in range(n_tiles):
        bits = pltpu.bitcast(score_ref[t], I32)
        u_ref[t] = jnp.where(bits < 0, bits ^ np.int32(0x7FFFFFFF), bits)

    kf = float(k)
    cnt = _count(u_ref, n_tiles, lambda u: u >= 0)
    lo = jnp.where(cnt >= kf, np.int32(0), _INT_MIN)

    def bit_step(i, lo):
        cand = lo | (np.int32(1) << (30 - i))
        cnt = _count(u_ref, n_tiles, lambda u: u >= cand)
        return jnp.where(cnt >= kf, cand, lo)

    lo = lax.fori_loop(0, 31, bit_step, lo)
    c_gt = _count(u_ref, n_tiles, lambda u: u > lo)
    c_eq = _count(u_ref, n_tiles, lambda u: u == lo)
    need = kf - c_gt
    real = lo > _KEY_NEG_INF
    excess = jnp.where(real & (c_eq > need), 1.0, 0.0)
    any_excess = jnp.max(excess) > 0.0

    @pl.when(jnp.logical_not(any_excess))
    def _():
        for t in range(n_tiles):
            u = u_ref[t]
            write_tile(t, jnp.where((u >= lo) & (u > _KEY_NEG_INF), 1.0, 0.0))

    @pl.when(any_excess)
    def _():
        upper = (lax.broadcasted_iota(I32, (LANES, LANES), 0)
                 <= lax.broadcasted_iota(I32, (LANES, LANES), 1)).astype(BF)
        seen = jnp.zeros((rows, 1), F32)
        for t in range(n_tiles):
            u = u_ref[t]
            eq = jnp.where(u == lo, 1.0, 0.0)
            prefix = seen + _dot(eq.astype(BF), upper)
            take = (u > lo) | ((u == lo) & (prefix <= need))
            write_tile(t, jnp.where(take & (u > _KEY_NEG_INF), 1.0, 0.0))
            seen = seen + jnp.sum(eq, axis=-1, keepdims=True)


IDX_KCHUNK = 512


def _indexer_prompt_body(qi_ref, tailq_ref, tail_ref, mask_ref, score_ref, u_ref, *, seq):
    j = pl.program_id(1)
    n_tiles = seq // LANES
    tiles_per_chunk = IDX_KCHUNK // LANES
    qi = qi_ref[0].astype(BF)
    w = tailq_ref[0][:, IDX_DIM:IDX_DIM + IDX_HEADS] * (IDX_DIM ** -0.5 * IDX_HEADS ** -0.5)

    for t in range(n_tiles):
        score_ref[t] = jnp.full((LANES, LANES), -jnp.inf, F32)

    q_pos = j * LANES + lax.broadcasted_iota(I32, (LANES, IDX_KCHUNK), 0)

    def chunk_step(ci, carry):
        k0 = pl.multiple_of(ci * IDX_KCHUNK, IDX_KCHUNK)
        kic = tail_ref[0, pl.ds(k0, IDX_KCHUNK), :][:, 0:IDX_DIM].astype(BF)
        acc = jnp.zeros((LANES, IDX_KCHUNK), F32)
        for h in range(IDX_HEADS):
            s = _dot_nt(qi[:, h * IDX_DIM:(h + 1) * IDX_DIM], kic)
            acc = acc + w[:, h:h + 1] * jnp.maximum(s, 0.0)
        k_pos = k0 + lax.broadcasted_iota(I32, (LANES, IDX_KCHUNK), 1)
        acc = jnp.where(k_pos <= q_pos, acc, -jnp.inf)
        for i in range(tiles_per_chunk):
            score_ref[ci * tiles_per_chunk + i] = acc[:, i * LANES:(i + 1) * LANES]
        return carry

    lax.fori_loop(0, j // tiles_per_chunk + 1, chunk_step, 0)

    def write_tile(t, m):
        mask_ref[0, 0, t] = m.astype(BF)

    _topk_mask(score_ref, u_ref, write_tile, n_tiles, TOPK)


def indexer_prompt(z3, k):
    bsz, seq = z3.shape[:2]
    nqb = seq // LANES
    tail_block = z3.shape[2] // LANES - 1
    assert k == TOPK
    return pl.pallas_call(
        functools.partial(_indexer_prompt_body, seq=seq),
        grid=(bsz, nqb),
        in_specs=[pl.BlockSpec((1, LANES, IDX_HEADS * IDX_DIM), lambda b, j: (b, j, 7)),
                  pl.BlockSpec((1, LANES, LANES), lambda b, j: (b, j, tail_block)),
                  pl.BlockSpec((1, seq, LANES), lambda b, j: (b, 0, tail_block))],
        out_specs=pl.BlockSpec((1, 1, nqb, LANES, LANES), lambda b, j: (b, j, 0, 0, 0)),
        out_shape=jax.ShapeDtypeStruct((bsz, nqb, nqb, LANES, LANES), BF),
        scratch_shapes=[pltpu.VMEM((nqb, LANES, LANES), F32),
                        pltpu.VMEM((nqb, LANES, LANES), I32)],
        compiler_params=_cparams(("parallel", "arbitrary")),
        name="indexer_prompt",
    )(z3, z3, z3)


ATTN_KSTEP = 512
ATTN_TILES = ATTN_KSTEP // LANES


def _attn_prompt_body(bfar_ref, q_ref, k_ref, v_ref, mask_ref, bias_ref, o_ref, m_ref, l_ref, acc_ref):
    j = pl.program_id(1)
    scale = HEAD_DIM ** -0.5
    q = q_ref[0].astype(BF)
    m_ref[...] = jnp.full(m_ref.shape, NEG_BIG, F32)
    l_ref[...] = jnp.zeros(l_ref.shape, F32)
    acc_ref[...] = jnp.zeros(acc_ref.shape, F32)

    def process(sb, near):
        k0 = pl.multiple_of(sb * ATTN_KSTEP, ATTN_KSTEP)
        kblk = k_ref[0, pl.ds(k0, ATTN_KSTEP), :]
        vblk = v_ref[0, pl.ds(k0, ATTN_KSTEP), :]
        tiles = [sb * ATTN_TILES + i for i in range(ATTN_TILES)]
        sel = jnp.concatenate([mask_ref[0, 0, t] for t in tiles], axis=1).astype(F32) > 0.0
        if near:
            bidx = [jnp.where(j - t == 0, 0, jnp.where(j - t == 1, 1, 2)) for t in tiles]
        for h in range(HEADS):
            cols = slice(h * HEAD_DIM, (h + 1) * HEAD_DIM)
            lg = _dot_nt(q[:, cols], kblk[:, cols]) * scale
            if near:
                lg = lg + jnp.concatenate([bias_ref[bi, h] for bi in bidx], axis=1)
            else:
                lg = lg + bfar_ref[h]
            lg = jnp.where(sel, lg, NEG_BIG)
            m_old = m_ref[h]
            m_new = jnp.maximum(m_old, jnp.max(lg, axis=-1, keepdims=True))
            p = jnp.exp(lg - jnp.concatenate([m_new] * ATTN_TILES, axis=1))
            alpha = jnp.exp(m_old - m_new)
            l_ref[h] = alpha * l_ref[h] + jnp.sum(p, axis=-1, keepdims=True)
            acc_ref[:, cols] = alpha * acc_ref[:, cols] + _dot(p.astype(BF), vblk[:, cols])
            m_ref[h] = m_new

    n_far = jnp.maximum((j - 1) // ATTN_TILES, 0)

    def far_step(sb, carry):
        process(sb, False)
        return carry

    def near_step(sb, carry):
        process(sb, True)
        return carry

    lax.fori_loop(0, n_far, far_step, 0)
    lax.fori_loop(n_far, j // ATTN_TILES + 1, near_step, 0)

    for h in range(HEADS):
        cols = slice(h * HEAD_DIM, (h + 1) * HEAD_DIM)
        o_ref[0, :, cols] = acc_ref[:, cols] / l_ref[h]


def attn_prompt(z3, kv_bf, mask, bias_tiles, bias_far):
    bsz, seq = z3.shape[:2]
    nqb = seq // LANES
    width = HEADS * HEAD_DIM
    assert seq % ATTN_KSTEP == 0
    return pl.pallas_call(
        _attn_prompt_body,
        grid=(bsz, nqb),
        in_specs=[pl.BlockSpec(memory_space=pltpu.SMEM),
                  pl.BlockSpec((1, LANES, width), lambda b, j: (b, j, 4)),
                  pl.BlockSpec((1, seq, width), lambda b, j: (b, 0, 0)),
                  pl.BlockSpec((1, seq, width), lambda b, j: (b, 0, 1)),
                  pl.BlockSpec((1, 1, nqb, LANES, LANES), lambda b, j: (b, j, 0, 0, 0)),
                  pl.BlockSpec(bias_tiles.shape, lambda b, j: (0, 0, 0, 0))],
        out_specs=pl.BlockSpec((1, LANES, width), lambda b, j: (b, j, 0)),
        out_shape=jax.ShapeDtypeStruct((bsz, seq, width), F32),
        scratch_shapes=[pltpu.VMEM((HEADS, LANES, LANES), F32),
                        pltpu.VMEM((HEADS, LANES, LANES), F32),
                        pltpu.VMEM((LANES, width), F32)],
        compiler_params=_cparams(("parallel", "arbitrary")),
        name="attn_prompt",
    )(bias_far, z3, kv_bf, kv_bf, mask, bias_tiles)


def _indexer_sample_body(pt_ref, qi_ref, wcol_ref, kinew_ref, *rest, n_pages):
    ki_refs = rest[:n_pages]
    out_ref = rest[n_pages]
    qi = qi_ref[0].astype(BF)
    w = wcol_ref[0] * (IDX_DIM ** -0.5 * IDX_HEADS ** -0.5)
    for i in range(n_pages):
        s = _dot_nt(qi, ki_refs[i][0].astype(BF))
        out_ref[i, 0] = jnp.sum(w * jnp.maximum(s, 0.0), axis=0, keepdims=True)
    kn = kinew_ref[0].astype(BF).astype(F32)
    sn = jnp.sum(qi.astype(F32) * kn, axis=-1, keepdims=True)
    new = jnp.sum(w * jnp.maximum(sn, 0.0), axis=0, keepdims=True)
    lane = lax.broadcasted_iota(I32, (1, LANES), 1)
    out_ref[n_pages, 0] = jnp.where(lane == 0, new, -jnp.inf)


def indexer_sample(page_table, qi3, wcol, ki_new, ki_pool):
    bsz, n_pages = page_table.shape
    ki_spec = lambda i: pl.BlockSpec((1, PAGE, IDX_DIM), lambda b, pt, i=i: (pt[b, i], 0, 0))
    grid_spec = pltpu.PrefetchScalarGridSpec(
        num_scalar_prefetch=1,
        grid=(bsz,),
        in_specs=[pl.BlockSpec((1, IDX_HEADS, IDX_DIM), lambda b, pt: (b, 0, 0)),
                  pl.BlockSpec((1, IDX_HEADS, 1), lambda b, pt: (b, 0, 0)),
                  pl.BlockSpec((1, 1, IDX_DIM), lambda b, pt: (b, 0, 0))]
                 + [ki_spec(i) for i in range(n_pages)],
        out_specs=pl.BlockSpec((n_pages + 1, 1, 1, LANES), lambda b, pt: (0, b, 0, 0)),
    )
    out = pl.pallas_call(
        functools.partial(_indexer_sample_body, n_pages=n_pages),
        grid_spec=grid_spec,
        out_shape=jax.ShapeDtypeStruct((n_pages + 1, bsz, 1, LANES), F32),
        compiler_params=_cparams(("arbitrary",)),
        name="indexer_sample",
    )(page_table, qi3, wcol, ki_new, *([ki_pool] * n_pages))
    return out.reshape(n_pages + 1, bsz, LANES)


def _select_body(score_ref, mask_ref, u_ref, *, n_tiles, k):
    def write_tile(t, m):
        mask_ref[t] = m

    _topk_mask(score_ref, u_ref, write_tile, n_tiles, k)


def select_topk(scores, k):
    n_tiles, rows, _ = scores.shape
    return pl.pallas_call(
        functools.partial(_select_body, n_tiles=n_tiles, k=k),
        grid=(1,),
        in_specs=[pl.BlockSpec(scores.shape, lambda i: (0, 0, 0))],
        out_specs=pl.BlockSpec(scores.shape, lambda i: (0, 0, 0)),
        out_shape=jax.ShapeDtypeStruct(scores.shape, F32),
        scratch_shapes=[pltpu.VMEM(scores.shape, I32)],
        compiler_params=_cparams(("arbitrary",)),
        name="select_topk",
    )(scores)


SAMPLE_PAGE_GROUP = 8


def _attn_sample_body(pt_ref, idx_ref, off_ref, q_ref, kn_ref, vn_ref, btab_ref, *rest,
                      n_pages, past_len):
    g = SAMPLE_PAGE_GROUP
    k_refs = rest[:g]
    v_refs = rest[g:2 * g]
    o_ref, kbuf, vbuf, bbuf = rest[2 * g:]
    b = pl.program_id(0)
    s = pl.program_id(1)
    scale = HEAD_DIM ** -0.5

    for i in range(g):
        page = s * g + i

        def gather(t, carry, i=i, page=page):
            key = idx_ref[b, t]
            r = key - page * PAGE
            kbuf[t] = k_refs[i][0, r]
            vbuf[t] = v_refs[i][0, r]
            bbuf[t] = btab_ref[jnp.minimum(past_len - key, REL_MAX_DIST)]
            return carry

        lax.fori_loop(off_ref[b, page], off_ref[b, page + 1], gather, 0)

    @pl.when(s == pl.num_programs(1) - 1)
    def _():
        @pl.when(off_ref[b, n_pages + 1] > off_ref[b, n_pages])
        def _():
            kbuf[TOPK - 1] = kn_ref[0]
            vbuf[TOPK - 1] = vn_ref[0]
            bbuf[TOPK - 1] = btab_ref[0]

        q8 = q_ref[0]
        prod = (kbuf[...] * q8[None]).reshape(TOPK * HEADS, HEAD_DIM)
        ones = jnp.ones((HEAD_DIM, LANES), BF)
        hi = prod.astype(BF)
        lo = (prod - hi.astype(F32)).astype(BF)
        lg = (_dot(hi, ones) + _dot(lo, ones)).reshape(TOPK, HEADS, LANES) * scale + bbuf[...]
        m = jnp.max(lg, axis=0, keepdims=True)
        p = jnp.exp(lg - m)
        l = jnp.sum(p, axis=0)
        o_ref[0] = jnp.sum(p * vbuf[...], axis=0) / l


def attn_sample(page_table, sel_idx, sel_off, zs3, k_pool, v_pool, bias_by_dist, past_len):
    bsz, n_pages = page_table.shape
    g = SAMPLE_PAGE_GROUP
    assert n_pages % g == 0
    zspec = lambda cb: pl.BlockSpec((1, HEADS, HEAD_DIM), lambda b, s, pt, ix, of, cb=cb: (b, cb, 0))
    pool_spec = lambda i: pl.BlockSpec((1, PAGE, HEADS, HEAD_DIM),
                                       lambda b, s, pt, ix, of, i=i: (pt[b, s * g + i], 0, 0, 0))
    grid_spec = pltpu.PrefetchScalarGridSpec(
        num_scalar_prefetch=3,
        grid=(bsz, n_pages // g),
        in_specs=[zspec(4), zspec(5), zspec(6),
                  pl.BlockSpec(bias_by_dist.shape, lambda b, s, pt, ix, of: (0, 0, 0))]
                 + [pool_spec(i) for i in range(g)] * 2,
        out_specs=pl.BlockSpec((1, HEADS, HEAD_DIM), lambda b, s, pt, ix, of: (b, 0, 0)),
        scratch_shapes=[pltpu.VMEM((TOPK, HEADS, HEAD_DIM), F32)] * 3,
    )
    return pl.pallas_call(
        functools.partial(_attn_sample_body, n_pages=n_pages, past_len=past_len),
        grid_spec=grid_spec,
        out_shape=jax.ShapeDtypeStruct((bsz, HEADS, HEAD_DIM), F32),
        compiler_params=_cparams(("arbitrary", "arbitrary")),
        name="attn_sample",
    )(page_table, sel_idx, sel_off, zs3, zs3, zs3, bias_by_dist,
      *([k_pool] * g), *([v_pool] * g))


def _bucket_table(max_dist):
    exact = REL_BUCKETS // 2
    d = np.arange(max_dist + 1)
    df = np.maximum(d, 1).astype(np.float32)
    far = exact + (np.log(df / exact) / np.float32(math.log(REL_MAX_DIST / exact))
                   * (REL_BUCKETS - exact)).astype(np.int32)
    return np.where(d < exact, d, np.minimum(far, REL_BUCKETS - 1)).astype(np.int32)


def _bias_tables(rel_bias):
    tab = _bucket_table(2 * LANES)
    assert np.all(tab[REL_MAX_DIST:] == REL_BUCKETS - 1)
    i = np.arange(LANES)
    dist0 = np.maximum(i[:, None] - i[None, :], 0)
    dist1 = LANES + i[:, None] - i[None, :]
    far = np.full((LANES, LANES), REL_BUCKETS - 1)
    idx = np.stack([tab[dist0], tab[dist1], far])
    tiles = jnp.transpose(rel_bias[idx], (0, 3, 1, 2))
    by_dist = jnp.broadcast_to(rel_bias[tab[:REL_MAX_DIST + 1]][:, :, None],
                               (REL_MAX_DIST + 1, HEADS, LANES))
    return tiles.astype(F32), rel_bias[REL_BUCKETS - 1].astype(F32), by_dist.astype(F32)


def _conv_prompt_body(bg_ref, cg_ref, xt_ref, w_ref, v_ref, st_ref, carry_ref, *, tblock):
    t = pl.program_id(1)

    @pl.when(t == 0)
    def _():
        carry_ref[...] = jnp.zeros_like(carry_ref)

    u = cg_ref[0] * xt_ref[0]
    row = lax.broadcasted_iota(I32, u.shape, 0)
    c0 = carry_ref[0:1]
    c1 = carry_ref[1:2]
    u1 = jnp.where(row == 0, c1, pltpu.roll(u, 1, axis=0))
    u2 = jnp.where(row == 0, c0, jnp.where(row == 1, c1, pltpu.roll(u, 2, axis=0)))
    conv = w_ref[0:1] * u2 + w_ref[1:2] * u1 + w_ref[2:3] * u
    v_ref[0] = bg_ref[0] * conv
    last = u[tblock - 2:tblock]
    carry_ref[0:2] = last

    @pl.when(t == pl.num_programs(1) - 1)
    def _():
        st_ref[0] = last


def conv_prompt(zc3, w_conv, tblock=256):
    bsz, seq = zc3.shape[:2]
    c = zc3.shape[2] // 3
    zspec = lambda cb: pl.BlockSpec((1, tblock, c), lambda b, t, cb=cb: (b, t, cb))
    return pl.pallas_call(
        functools.partial(_conv_prompt_body, tblock=tblock),
        grid=(bsz, seq // tblock),
        in_specs=[zspec(0), zspec(1), zspec(2), pl.BlockSpec((3, c), lambda b, t: (0, 0))],
        out_specs=[pl.BlockSpec((1, tblock, c), lambda b, t: (b, t, 0)),
                   pl.BlockSpec((1, 2, c), lambda b, t: (b, 0, 0))],
        out_shape=[jax.ShapeDtypeStruct((bsz, seq, c), F32),
                   jax.ShapeDtypeStruct((bsz, 2, c), F32)],
        scratch_shapes=[pltpu.VMEM((8, c), F32)],
        compiler_params=_cparams(("parallel", "arbitrary")),
        name="conv_prompt",
    )(zc3, zc3, zc3, w_conv)


def _conv_sample_body(bg_ref, cg_ref, xt_ref, w_ref, s0_ref, s1_ref, v_ref, n0_ref, n1_ref):
    u = cg_ref[...] * xt_ref[...]
    conv = w_ref[0:1] * s0_ref[...] + w_ref[1:2] * s1_ref[...] + w_ref[2:3] * u
    v_ref[...] = bg_ref[...] * conv
    n0_ref[...] = s1_ref[...]
    n1_ref[...] = u


def conv_sample(zc, w_conv, s0, s1):
    bsz = zc.shape[0]
    c = zc.shape[1] // 3
    zspec = lambda cb: pl.BlockSpec((bsz, c), lambda i, cb=cb: (0, cb))
    full = pl.BlockSpec((bsz, c), lambda i: (0, 0))
    return pl.pallas_call(
        _conv_sample_body,
        grid=(1,),
        in_specs=[zspec(0), zspec(1), zspec(2), pl.BlockSpec((3, c), lambda i: (0, 0)), full, full],
        out_specs=[full, full, full],
        out_shape=[jax.ShapeDtypeStruct((bsz, c), F32)] * 3,
        compiler_params=_cparams(("arbitrary",)),
        name="conv_sample",
    )(zc, zc, zc, w_conv, s0, s1)


def _router_body(x_ref, g_ref, wr_ref, br_ref, h_ref, route_ref):
    x = x_ref[...]
    ms = jnp.mean(x * x, axis=-1, keepdims=True)
    h = x * lax.rsqrt(ms + RMS_EPS) * g_ref[...]
    h_ref[...] = h.astype(BF)
    logits = jnp.dot(h, wr_ref[...], preferred_element_type=F32,
                     precision=lax.Precision.HIGHEST) + br_ref[...]
    lane = lax.broadcasted_iota(I32, logits.shape, 1)
    big = np.int32(1 << 20)
    lg = jnp.where(lane < N_GROUPS, logits, -jnp.inf)
    g_max = jnp.max(lg, axis=-1, keepdims=True)
    g_idx = jnp.min(jnp.where(lg == g_max, lane, big), axis=-1, keepdims=True)
    g_w = 1.0 / jnp.sum(jnp.exp(lg - g_max), axis=-1, keepdims=True)
    first = N_GROUPS + EXP_PER_GROUP * g_idx
    le = jnp.where((lane >= first) & (lane < first + EXP_PER_GROUP), logits, -jnp.inf)
    l1 = jnp.max(le, axis=-1, keepdims=True)
    i1 = jnp.min(jnp.where(le == l1, lane, big), axis=-1, keepdims=True)
    le2 = jnp.where(lane == i1, -jnp.inf, le)
    l2 = jnp.max(le2, axis=-1, keepdims=True)
    i2 = jnp.min(jnp.where(le2 == l2, lane, big), axis=-1, keepdims=True)
    r = jnp.exp(l2 - l1)
    w1 = g_w / (1.0 + r)
    w2 = g_w * r / (1.0 + r)
    e1 = (i1 - N_GROUPS).astype(F32)
    e2 = (i2 - N_GROUPS).astype(F32)
    route_ref[...] = jnp.where(lane == 0, e1, jnp.where(lane == 1, e2,
                               jnp.where(lane == 2, w1, jnp.where(lane == 3, w2, 0.0))))


def moe_router(x, g, wr, br, tm):
    m, k = x.shape
    return pl.pallas_call(
        _router_body,
        grid=(m // tm,),
        in_specs=[pl.BlockSpec((tm, k), lambda i: (i, 0)),
                  pl.BlockSpec((1, k), lambda i: (0, 0)),
                  pl.BlockSpec((k, LANES), lambda i: (0, 0)),
                  pl.BlockSpec((1, LANES), lambda i: (0, 0))],
        out_specs=[pl.BlockSpec((tm, k), lambda i: (i, 0)),
                   pl.BlockSpec((tm, LANES), lambda i: (i, 0))],
        out_shape=[jax.ShapeDtypeStruct((m, k), BF),
                   jax.ShapeDtypeStruct((m, LANES), F32)],
        compiler_params=_cparams(("parallel",)),
        name="moe_router",
    )(x, g.reshape(1, k), wr, br)


def _experts_body(te_ref, nu_ref, h_ref, gate_ref, wg_ref, wu_ref, wd_ref, o_ref):
    i = pl.program_id(0)

    @pl.when(i < nu_ref[0])
    def _():
        h = h_ref[...]
        a = _dot(h, wg_ref[0])
        b = _dot(h, wu_ref[0])
        hid = a * _sigmoid(a) * b * gate_ref[...]
        o_ref[...] = _dot(hid.astype(BF), wd_ref[0])

    @pl.when(i >= nu_ref[0])
    def _():
        o_ref[...] = jnp.zeros_like(o_ref)


def moe_experts(tile_expert, n_used, hs, gate_col, wg_bf, wu_bf, wd_bf):
    p, k = hs.shape
    f = wg_bf.shape[2]
    n_tiles = p // MOE_TILE
    grid_spec = pltpu.PrefetchScalarGridSpec(
        num_scalar_prefetch=2,
        grid=(n_tiles,),
        in_specs=[pl.BlockSpec((MOE_TILE, k), lambda i, te, nu: (i, 0)),
                  pl.BlockSpec((MOE_TILE, 1), lambda i, te, nu: (i, 0)),
                  pl.BlockSpec((1, k, f), lambda i, te, nu: (te[i], 0, 0)),
                  pl.BlockSpec((1, k, f), lambda i, te, nu: (te[i], 0, 0)),
                  pl.BlockSpec((1, f, k), lambda i, te, nu: (te[i], 0, 0))],
        out_specs=pl.BlockSpec((MOE_TILE, k), lambda i, te, nu: (i, 0)),
    )
    return pl.pallas_call(
        _experts_body,
        grid_spec=grid_spec,
        out_shape=jax.ShapeDtypeStruct((p, k), F32),
        compiler_params=_cparams(("arbitrary",)),
        name="moe_experts",
    )(tile_expert, n_used, hs, gate_col, wg_bf, wu_bf, wd_bf)


def hier_moe(xs, tms, g, wrg, brg, wre, bre, wg_bf, wu_bf, wd_bf):
    d = xs[0].shape[1]
    m = sum(x.shape[0] for x in xs)
    wr = jnp.zeros((d, LANES), F32)
    wr = wr.at[:, :N_GROUPS].set(wrg).at[:, N_GROUPS:N_GROUPS + N_EXPERTS].set(wre.reshape(d, N_EXPERTS))
    br = jnp.zeros((1, LANES), F32)
    br = br.at[0, :N_GROUPS].set(brg).at[0, N_GROUPS:N_GROUPS + N_EXPERTS].set(bre.reshape(N_EXPERTS))
    routed = [moe_router(x, g, wr, br, tm) for x, tm in zip(xs, tms)]
    h_bf = jnp.concatenate([r[0] for r in routed], axis=0)
    route = jnp.concatenate([r[1] for r in routed], axis=0)

    eid = route[:, 0:2].astype(I32).reshape(-1)
    gate = route[:, 2:4].reshape(-1)
    onehot = (eid[:, None] == jnp.arange(N_EXPERTS, dtype=I32)[None, :]).astype(I32)
    rank = jnp.take_along_axis(jnp.cumsum(onehot, axis=0) - onehot, eid[:, None], axis=1)[:, 0]
    counts = jnp.sum(onehot, axis=0)
    padded = ((counts + MOE_TILE - 1) // MOE_TILE) * MOE_TILE
    ends = jnp.cumsum(padded)
    pos = (ends - padded)[eid] + rank
    n_rows = 2 * m + N_EXPERTS * MOE_TILE
    n_rows = -(-n_rows // MOE_TILE) * MOE_TILE
    src = jnp.zeros((n_rows,), I32).at[pos].set(jnp.arange(2 * m, dtype=I32) // 2)
    gate_sorted = jnp.zeros((n_rows,), F32).at[pos].set(gate)
    tile_start = jnp.arange(n_rows // MOE_TILE, dtype=I32) * MOE_TILE
    tile_expert = jnp.minimum(jnp.searchsorted(ends, tile_start, side="right"),
                              N_EXPERTS - 1).astype(I32)
    n_used = (ends[-1] // MOE_TILE).astype(I32).reshape(1)

    hs = jnp.take(h_bf, src, axis=0)
    out = moe_experts(tile_expert, n_used, hs, gate_sorted[:, None], wg_bf, wu_bf, wd_bf)
    pos2 = pos.reshape(m, 2)
    res, r0 = [], 0
    for x in xs:
        pp = pos2[r0:r0 + x.shape[0]]
        res.append(x + jnp.take(out, pp[:, 0], axis=0) + jnp.take(out, pp[:, 1], axis=0))
        r0 += x.shape[0]
    return res


def kernel(x_prompt, x_sample, cache_k, cache_v, cache_ki, state_hgrn, state_conv, page_table,
           norm_mix_g, norm_ffn_g, final_g, w_in_even, w_out_even, hgrn_lb_logits, hgrn_norm_g,
           rel_bias, w_in_conv, w_conv, w_out_conv, w_router_g, b_router_g, w_router_e,
           b_router_e, w_gate, w_up, w_down):
    bsz, seq, d = x_prompt.shape
    dec = x_sample.shape[0]
    n_p = bsz * seq
    tm_p, tm_s = 512, 128
    tms = (tm_p, tm_s)
    assert n_p % tm_p == 0 and dec % tm_s == 0 and x_sample.shape[1] == 1
    width = HEADS * HEAD_DIM
    even_in = w_in_even.shape[2]
    even_pad = -(-even_in // LANES) * LANES
    tail0 = even_pad - LANES
    n_pool = cache_k.shape[1]

    xp = x_prompt.reshape(n_p, d)
    xs = x_sample.reshape(dec, d)

    lbs = jnp.cumsum(jax.nn.softmax(hgrn_lb_logits.astype(F32), axis=0), axis=0)[:-1]
    bias_tiles, bias_far, bias_by_dist = _bias_tables(rel_bias)

    w_in = jnp.pad(w_in_even[0], ((0, 0), (0, even_pad - even_in))).astype(BF)
    zp = norm_matmul(xp, norm_mix_g[0], w_in, tm_p, 640)
    zs = norm_matmul(xs, norm_mix_g[0], w_in, tm_s, 640)
    zp3 = zp.reshape(bsz, seq, even_pad)
    zs3 = zs.reshape(dec, even_pad // LANES, LANES)

    oa_p, hgrn_p = hgrn_prompt(zp3, lbs[0], hgrn_norm_g[0])
    oa_s, hgrn_s = hgrn_sample(zs3, lbs[0], hgrn_norm_g[0], state_hgrn[0])

    mask_p = indexer_prompt(zp3, TOPK)
    kv_bf = zp3[:, :, 5 * width:7 * width].astype(BF)
    ob_p = attn_prompt(zp3, kv_bf, mask_p, bias_tiles, bias_far)

    qi3 = zs[:, 7 * width:8 * width].reshape(dec, IDX_HEADS, IDX_DIM)
    ki_new = zs[:, tail0:tail0 + IDX_DIM].reshape(dec, 1, IDX_DIM)
    wcol = zs[:, tail0 + IDX_DIM:tail0 + IDX_DIM + IDX_HEADS].reshape(dec, IDX_HEADS, 1)
    pages = page_table + 0 * n_pool
    scores_s = indexer_sample(pages, qi3, wcol, ki_new,
                              cache_ki.reshape(-1, PAGE, IDX_DIM))
    mask_s = select_topk(scores_s, TOPK)
    sel = jnp.transpose(mask_s, (1, 0, 2)).astype(I32)
    csum = jnp.cumsum(sel.reshape(dec, -1), axis=1)
    want = jnp.arange(1, TOPK + 1, dtype=I32)
    sel_idx = jax.vmap(lambda c: jnp.searchsorted(c, want, side="left"))(csum).astype(I32)
    sel_off = jnp.concatenate([jnp.zeros((dec, 1), I32),
                               jnp.cumsum(jnp.sum(sel, axis=2), axis=1)], axis=1).astype(I32)
    n_pages = page_table.shape[1]
    ob_s = attn_sample(pages, sel_idx, sel_off, zs3,
                       cache_k.reshape(-1, PAGE, HEADS, HEAD_DIM),
                       cache_v.reshape(-1, PAGE, HEADS, HEAD_DIM), bias_by_dist, n_pages * PAGE)

    w_out = w_out_even[0].astype(BF)
    w_out_ab = [w_out[:width], w_out[width:]]
    xp = matmul_residual([oa_p.reshape(n_p, width), ob_p.reshape(n_p, width)], w_out_ab, xp, tm_p, 512)
    xs = matmul_residual([oa_s.reshape(dec, width), ob_s.reshape(dec, width)], w_out_ab, xs, tm_s, 512)

    xp, xs = hier_moe([xp, xs], tms, norm_ffn_g[0], w_router_g[0], b_router_g[0], w_router_e[0],
                      b_router_e[0], w_gate[0].astype(BF), w_up[0].astype(BF), w_down[0].astype(BF))

    w_in_c = w_in_conv[0].astype(BF)
    zcp = norm_matmul(xp, norm_mix_g[1], w_in_c, tm_p, 512)
    zcs = norm_matmul(xs, norm_mix_g[1], w_in_c, tm_s, 512)
    cw = zcp.shape[1] // 3
    v_p, conv_p = conv_prompt(zcp.reshape(bsz, seq, 3 * cw), w_conv[0])
    v_s, cs0, cs1 = conv_sample(zcs, w_conv[0], state_conv[0, :, 0], state_conv[0, :, 1])
    w_out_c = [w_out_conv[0].astype(BF)]
    xp = matmul_residual([v_p.reshape(n_p, cw)], w_out_c, xp, tm_p, 512)
    xs = matmul_residual([v_s], w_out_c, xs, tm_s, 512)

    xp, xs = hier_moe([xp, xs], tms, norm_ffn_g[1], w_router_g[1], b_router_g[1], w_router_e[1],
                      b_router_e[1], w_gate[1].astype(BF), w_up[1].astype(BF), w_down[1].astype(BF))

    yp = rmsnorm(xp, final_g, tm_p)
    ys = rmsnorm(xs, final_g, tm_s)

    kcol, vcol = 5 * width, 6 * width
    heads = lambda a, n: a.reshape(1, *n, HEADS, HEAD_DIM)
    return (yp.reshape(bsz, seq, d),
            ys.reshape(dec, 1, d),
            heads(zp[:, kcol:kcol + width], (bsz, seq)),
            heads(zp[:, vcol:vcol + width], (bsz, seq)),
            zp[:, tail0:tail0 + IDX_DIM].reshape(1, bsz, seq, IDX_DIM),
            hgrn_p[None],
            conv_p[None],
            heads(zs[:, kcol:kcol + width], (dec, 1)),
            heads(zs[:, vcol:vcol + width], (dec, 1)),
            zs[:, tail0:tail0 + IDX_DIM].reshape(1, dec, 1, IDX_DIM),
            hgrn_s[None],
            jnp.stack([cs0, cs1], axis=1)[None])
```

```python
import functools
import math

import numpy as np
import jax
import jax.numpy as jnp
from jax import lax
from jax.experimental import pallas as pl
from jax.experimental.pallas import tpu as pltpu

F32 = jnp.float32
BF = jnp.bfloat16
I32 = jnp.int32

RMS_EPS = 1e-6
LANES = 128
NEG_BIG = -1e30
VMEM_LIMIT = 56 * 1024 * 1024

D_MODEL = 2048
HEADS = 8
HEAD_DIM = 128
IDX_HEADS = 16
IDX_DIM = 64
TOPK = 256
REL_BUCKETS = 32
REL_MAX_DIST = 128
N_EXPERTS = 16
EXP_PER_GROUP = 4
N_GROUPS = 4
D_EXPERT = 512
PAGE = 128

HGRN_CHUNK = 64
MOE_TILE = 256


def _cparams(sem):
    return pltpu.CompilerParams(dimension_semantics=sem, vmem_limit_bytes=VMEM_LIMIT)


def _dot(a, b):
    return jnp.dot(a, b, preferred_element_type=F32)


def _dot_nt(a, b):
    return lax.dot_general(a, b, (((1,), (1,)), ((), ())), preferred_element_type=F32)


def _dot_tn(a, b):
    return lax.dot_general(a, b, (((0,), (0,)), ((), ())), preferred_element_type=F32)


def _sigmoid(x):
    return 1.0 / (1.0 + jnp.exp(-x))


def _norm_mm_body(x_ref, g_ref, w_ref, o_ref, h_ref):
    @pl.when(pl.program_id(1) == 0)
    def _():
        x = x_ref[...]
        ms = jnp.mean(x * x, axis=-1, keepdims=True)
        h_ref[...] = (x * lax.rsqrt(ms + RMS_EPS) * g_ref[...]).astype(BF)

    o_ref[...] = _dot(h_ref[...], w_ref[...])


def norm_matmul(x, g, w_bf, tm, tn):
    m, k = x.shape
    n = w_bf.shape[1]
    return pl.pallas_call(
        _norm_mm_body,
        grid=(m // tm, n // tn),
        in_specs=[pl.BlockSpec((tm, k), lambda i, j: (i, 0)),
                  pl.BlockSpec((1, k), lambda i, j: (0, 0)),
                  pl.BlockSpec((k, tn), lambda i, j: (0, j))],
        out_specs=pl.BlockSpec((tm, tn), lambda i, j: (i, j)),
        out_shape=jax.ShapeDtypeStruct((m, n), F32),
        scratch_shapes=[pltpu.VMEM((tm, k), BF)],
        compiler_params=_cparams(("parallel", "arbitrary")),
        name="norm_matmul",
    )(x, g.reshape(1, k), w_bf)


def _mm_res_body(*refs, n_lhs):
    a_refs = refs[:n_lhs]
    w_refs = refs[n_lhs:2 * n_lhs]
    r_ref = refs[2 * n_lhs]
    o_ref = refs[2 * n_lhs + 1]
    s_refs = refs[2 * n_lhs + 2:]

    @pl.when(pl.program_id(1) == 0)
    def _():
        for a_ref, s_ref in zip(a_refs, s_refs):
            s_ref[...] = a_ref[...].astype(BF)

    acc = r_ref[...]
    for s_ref, w_ref in zip(s_refs, w_refs):
        acc = acc + _dot(s_ref[...], w_ref[...])
    o_ref[...] = acc


def matmul_residual(lhs, ws_bf, res, tm, tn):
    m, n = res.shape
    n_lhs = len(lhs)
    in_specs = ([pl.BlockSpec((tm, a.shape[1]), lambda i, j: (i, 0)) for a in lhs]
                + [pl.BlockSpec((w.shape[0], tn), lambda i, j: (0, j)) for w in ws_bf]
                + [pl.BlockSpec((tm, tn), lambda i, j: (i, j))])
    return pl.pallas_call(
        functools.partial(_mm_res_body, n_lhs=n_lhs),
        grid=(m // tm, n // tn),
        in_specs=in_specs,
        out_specs=pl.BlockSpec((tm, tn), lambda i, j: (i, j)),
        out_shape=jax.ShapeDtypeStruct((m, n), F32),
        scratch_shapes=[pltpu.VMEM((tm, a.shape[1]), BF) for a in lhs],
        compiler_params=_cparams(("parallel", "arbitrary")),
        name="matmul_residual",
    )(*lhs, *ws_bf, res)


def _rmsnorm_body(x_ref, g_ref, o_ref):
    x = x_ref[...]
    ms = jnp.mean(x * x, axis=-1, keepdims=True)
    o_ref[...] = x * lax.rsqrt(ms + RMS_EPS) * g_ref[...]


def rmsnorm(x, g, tm):
    m, k = x.shape
    return pl.pallas_call(
        _rmsnorm_body,
        grid=(m // tm,),
        in_specs=[pl.BlockSpec((tm, k), lambda i: (i, 0)),
                  pl.BlockSpec((1, k), lambda i: (0, 0))],
        out_specs=pl.BlockSpec((tm, k), lambda i: (i, 0)),
        out_shape=jax.ShapeDtypeStruct((m, k), F32),
        compiler_params=_cparams(("parallel",)),
        name="rmsnorm",
    )(x, g.reshape(1, k))


def _hgrn_static(c):
    levels = []
    m = 1
    while m < c:
        levels.append(m)
        m *= 2
    t = np.arange(c)
    rows = [t[None, :] <= t[:, None]]
    masks = [np.eye(c, dtype=bool)]
    for m in levels:
        blk = t // (2 * m)
        pos = t % (2 * m)
        bnd = blk * 2 * m + m - 1
        right = pos >= m
        left = pos < m
        e_rows = (t[None, :] > bnd[:, None]) & (t[None, :] <= t[:, None]) & right[:, None]
        f_rows = (t[None, :] > t[:, None]) & (t[None, :] <= bnd[:, None]) & left[:, None]
        rows.append(e_rows | f_rows)
        masks.append((blk[:, None] == blk[None, :]) & right[:, None] & left[None, :])
    m_all = np.stack(rows).astype(np.float32)
    masks = np.stack(masks).astype(np.float32)
    return m_all, masks, len(levels)


def _hgrn_gates(qa, fa, lb):
    f = lb + (1.0 - lb) * _sigmoid(fa)
    q = qa * _sigmoid(qa)
    return q, f


def _hgrn_prompt_body(qa_ref, fa_ref, ia_ref, ga_ref, lb_ref, gn_ref, mall_ref, masks_ref,
                      oa_ref, st_out_ref, st_ref, *, chunk, tblock, n_levels):
    c = chunk
    t_idx = pl.program_id(1)

    @pl.when(t_idx == 0)
    def _():
        st_ref[...] = jnp.zeros_like(st_ref)

    gn = gn_ref[...]

    def chunk_step(ci, carry):
        r0 = pl.multiple_of(ci * c, c)
        for h in range(HEADS):
            cols = slice(h * HEAD_DIM, (h + 1) * HEAD_DIM)
            qa = qa_ref[0, pl.ds(r0, c), cols]
            fa = fa_ref[0, pl.ds(r0, c), cols]
            ia = ia_ref[0, pl.ds(r0, c), cols]
            ga = ga_ref[0, pl.ds(r0, c), cols]
            q, f = _hgrn_gates(qa, fa, lb_ref[:, cols])
            k = 1.0 - f
            g = jnp.log(f)
            g_hi = g.astype(BF)
            g_lo = (g - g_hi.astype(F32)).astype(BF)
            b = _dot(mall_ref[0], g_hi) + _dot(mall_ref[0], g_lo)
            b_last = b[c - 1:c]
            st = st_ref[h]
            v = ia.astype(BF)
            o = _dot_nt((q * jnp.exp(b)).astype(BF), st.astype(BF))
            a = masks_ref[0] * _dot_nt(q.astype(BF), k.astype(BF))
            for li in range(n_levels):
                ml = mall_ref[1 + li]
                w = jnp.exp(_dot(ml, g_hi) + _dot(ml, g_lo))
                a = a + masks_ref[1 + li] * _dot_nt((q * w).astype(BF), (k * w).astype(BF))
            o = o + _dot(a.astype(BF), v)
            k_st = (k * jnp.exp(b_last - b)).astype(BF)
            st_ref[h] = st * jnp.exp(b_last) + _dot_tn(v, k_st)
            ms = jnp.mean(o * o, axis=-1, keepdims=True)
            y = o * lax.rsqrt(ms + RMS_EPS) * gn * (ga * _sigmoid(ga))
            oa_ref[0, pl.ds(r0, c), cols] = y
        return carry

    lax.fori_loop(0, tblock // c, chunk_step, 0)

    @pl.when(t_idx == pl.num_programs(1) - 1)
    def _():
        st_out_ref[0] = st_ref[...]


def hgrn_prompt(z3, lb, gn, tblock=256, chunk=HGRN_CHUNK):
    bsz, seq = z3.shape[:2]
    width = HEADS * HEAD_DIM
    m_all, masks, n_levels = _hgrn_static(chunk)
    zspec = lambda cb: pl.BlockSpec((1, tblock, width), lambda b, t, cb=cb: (b, t, cb))
    oa, st = pl.pallas_call(
        functools.partial(_hgrn_prompt_body, chunk=chunk, tblock=tblock, n_levels=n_levels),
        grid=(bsz, seq // tblock),
        in_specs=[zspec(0), zspec(1), zspec(2), zspec(3),
                  pl.BlockSpec((1, width), lambda b, t: (0, 0)),
                  pl.BlockSpec((1, HEAD_DIM), lambda b, t: (0, 0)),
                  pl.BlockSpec(m_all.shape, lambda b, t: (0, 0, 0)),
                  pl.BlockSpec(masks.shape, lambda b, t: (0, 0, 0))],
        out_specs=[pl.BlockSpec((1, tblock, width), lambda b, t: (b, t, 0)),
                   pl.BlockSpec((1, HEADS, HEAD_DIM, HEAD_DIM), lambda b, t: (b, 0, 0, 0))],
        out_shape=[jax.ShapeDtypeStruct((bsz, seq, width), F32),
                   jax.ShapeDtypeStruct((bsz, HEADS, HEAD_DIM, HEAD_DIM), F32)],
        scratch_shapes=[pltpu.VMEM((HEADS, HEAD_DIM, HEAD_DIM), F32)],
        compiler_params=_cparams(("parallel", "arbitrary")),
        name="hgrn_prompt",
    )(z3, z3, z3, z3, lb.reshape(1, width), gn.reshape(1, HEAD_DIM),
      jnp.asarray(m_all, BF), jnp.asarray(masks, F32))
    return oa, jnp.swapaxes(st, -1, -2)


def _col(row, eye):
    return jnp.sum(eye * row, axis=1, keepdims=True)


def _hgrn_sample_body(qa_ref, fa_ref, ia_ref, ga_ref, lb_ref, gn_ref, s_ref, oa_ref, so_ref):
    eye = (lax.broadcasted_iota(I32, (HEAD_DIM, HEAD_DIM), 0)
           == lax.broadcasted_iota(I32, (HEAD_DIM, HEAD_DIM), 1)).astype(F32)
    q8, f8 = _hgrn_gates(qa_ref[0], fa_ref[0], lb_ref[...])
    ga = ga_ref[0]
    gate = ga * _sigmoid(ga)
    outs = []
    for h in range(HEADS):
        f_col = _col(f8[h:h + 1], eye)
        q_col = _col(q8[h:h + 1], eye)
        s_new = f_col * s_ref[0, h] + (1.0 - f_col) * ia_ref[0, h:h + 1]
        so_ref[0, h] = s_new
        outs.append(jnp.sum(q_col * s_new, axis=0, keepdims=True))
    o = jnp.concatenate(outs, axis=0)
    ms = jnp.mean(o * o, axis=-1, keepdims=True)
    oa_ref[0] = o * lax.rsqrt(ms + RMS_EPS) * gn_ref[...] * gate


def hgrn_sample(zs3, lb, gn, s0):
    bsz = zs3.shape[0]
    zspec = lambda cb: pl.BlockSpec((1, HEADS, HEAD_DIM), lambda b, cb=cb: (b, cb, 0))
    sspec = pl.BlockSpec((1, HEADS, HEAD_DIM, HEAD_DIM), lambda b: (b, 0, 0, 0))
    return pl.pallas_call(
        _hgrn_sample_body,
        grid=(bsz,),
        in_specs=[zspec(0), zspec(1), zspec(2), zspec(3),
                  pl.BlockSpec((HEADS, HEAD_DIM), lambda b: (0, 0)),
                  pl.BlockSpec((1, HEAD_DIM), lambda b: (0, 0)),
                  sspec],
        out_specs=[pl.BlockSpec((1, HEADS, HEAD_DIM), lambda b: (b, 0, 0)), sspec],
        out_shape=[jax.ShapeDtypeStruct((bsz, HEADS, HEAD_DIM), F32),
                   jax.ShapeDtypeStruct(s0.shape, F32)],
        compiler_params=_cparams(("parallel",)),
        name="hgrn_sample",
    )(zs3, zs3, zs3, zs3, lb.reshape(HEADS, HEAD_DIM), gn.reshape(1, HEAD_DIM), s0)


_KEY_NEG_INF = np.int32(np.uint32(0x807FFFFF).astype(np.int64) - (1 << 32))
_INT_MIN = np.int32(-(1 << 31))


def _count(u_ref, n_groups, group, pred):
    rows = u_ref.shape[1]

    def body(gi, acc):
        for i in range(group):
            acc = acc + pred(u_ref[gi * group + i]).astype(F32)
        return acc

    acc = lax.fori_loop(0, n_groups, body, jnp.zeros((rows, LANES), F32))
    return jnp.sum(acc, axis=-1, keepdims=True)


def _topk_mask(score_ref, u_ref, write_tile, n_tiles, n_groups, group, k):
    rows = score_ref.shape[1]
    n_live = n_groups * group

    def to_key(t, carry):
        bits = pltpu.bitcast(score_ref[t], I32)
        u_ref[t] = jnp.where(bits < 0, bits ^ np.int32(0x7FFFFFFF), bits)
        return carry

    lax.fori_loop(0, n_live, to_key, 0)

    kf = float(k)
    cnt = _count(u_ref, n_groups, group, lambda u: u >= 0)
    lo = jnp.where(cnt >= kf, np.int32(0), _INT_MIN)

    def bit_step(i, lo):
        cand = lo | (np.int32(1) << (30 - i))
        cnt = _count(u_ref, n_groups, group, lambda u: u >= cand)
        return jnp.where(cnt >= kf, cand, lo)

    lo = lax.fori_loop(0, 31, bit_step, lo)
    c_gt = _count(u_ref, n_groups, group, lambda u: u > lo)
    c_eq = _count(u_ref, n_groups, group, lambda u: u == lo)
    need = kf - c_gt
    real = lo > _KEY_NEG_INF
    excess = jnp.where(real & (c_eq > need), 1.0, 0.0)
    any_excess = jnp.max(excess) > 0.0

    @pl.when(jnp.logical_not(any_excess))
    def _():
        def emit(t, carry):
            u = u_ref[t]
            write_tile(t, jnp.where((u >= lo) & (u > _KEY_NEG_INF), 1.0, 0.0))
            return carry

        lax.fori_loop(0, n_live, emit, 0)

    @pl.when(any_excess)
    def _():
        upper = (lax.broadcasted_iota(I32, (LANES, LANES), 0)
                 <= lax.broadcasted_iota(I32, (LANES, LANES), 1)).astype(BF)

        def emit(t, seen):
            u = u_ref[t]
            eq = jnp.where(u == lo, 1.0, 0.0)
            prefix = seen + _dot(eq.astype(BF), upper)
            take = (u > lo) | ((u == lo) & (prefix <= need))
            write_tile(t, jnp.where(take & (u > _KEY_NEG_INF), 1.0, 0.0))
            return seen + jnp.sum(eq, axis=-1, keepdims=True)

        lax.fori_loop(0, n_live, emit, jnp.zeros((rows, 1), F32))

    def clear(t, carry):
        write_tile(t, jnp.zeros((rows, LANES), F32))
        return carry

    lax.fori_loop(n_live, n_tiles, clear, 0)


IDX_KCHUNK = 512
IDX_QROWS = 256


def _indexer_prompt_body(qi_ref, tailq_ref, tail_ref, mask_ref, score_ref, u_ref, *, seq):
    j = pl.program_id(1)
    rows = IDX_QROWS
    qblocks = rows // LANES
    n_tiles = seq // LANES
    tiles_per_chunk = IDX_KCHUNK // LANES
    qi = qi_ref[0].astype(BF)
    w = tailq_ref[0][:, IDX_DIM:IDX_DIM + IDX_HEADS] * (IDX_DIM ** -0.5 * IDX_HEADS ** -0.5)

    n_chunks = (j * rows + rows - 1) // IDX_KCHUNK + 1
    q_pos = j * rows + lax.broadcasted_iota(I32, (rows, IDX_KCHUNK), 0)

    def chunk_step(ci, carry):
        k0 = pl.multiple_of(ci * IDX_KCHUNK, IDX_KCHUNK)
        kic = tail_ref[0, pl.ds(k0, IDX_KCHUNK), :][:, 0:IDX_DIM].astype(BF)
        acc = jnp.zeros((rows, IDX_KCHUNK), F32)
        for h in range(IDX_HEADS):
            s = _dot_nt(qi[:, h * IDX_DIM:(h + 1) * IDX_DIM], kic)
            acc = acc + w[:, h:h + 1] * jnp.maximum(s, 0.0)
        k_pos = k0 + lax.broadcasted_iota(I32, (rows, IDX_KCHUNK), 1)
        acc = jnp.where(k_pos <= q_pos, acc, -jnp.inf)
        for i in range(tiles_per_chunk):
            score_ref[ci * tiles_per_chunk + i] = acc[:, i * LANES:(i + 1) * LANES]
        return carry

    lax.fori_loop(0, n_chunks, chunk_step, 0)

    def write_tile(t, m):
        mb = m.astype(BF)
        for qb in range(qblocks):
            mask_ref[0, qb, t] = mb[qb * LANES:(qb + 1) * LANES]

    _topk_mask(score_ref, u_ref, write_tile, n_tiles, n_chunks, tiles_per_chunk, TOPK)


def indexer_prompt(z3, k):
    bsz, seq = z3.shape[:2]
    nqb = seq // LANES
    rows = IDX_QROWS
    tail_block = z3.shape[2] // LANES - 1
    assert k == TOPK and seq % rows == 0 and seq % IDX_KCHUNK == 0
    return pl.pallas_call(
        functools.partial(_indexer_prompt_body, seq=seq),
        grid=(bsz, seq // rows),
        in_specs=[pl.BlockSpec((1, rows, IDX_HEADS * IDX_DIM), lambda b, j: (b, j, 7)),
                  pl.BlockSpec((1, rows, LANES), lambda b, j: (b, j, tail_block)),
                  pl.BlockSpec((1, seq, LANES), lambda b, j: (b, 0, tail_block))],
        out_specs=pl.BlockSpec((1, rows // LANES, nqb, LANES, LANES), lambda b, j: (b, j, 0, 0, 0)),
        out_shape=jax.ShapeDtypeStruct((bsz, nqb, nqb, LANES, LANES), BF),
        scratch_shapes=[pltpu.VMEM((nqb, rows, LANES), F32),
                        pltpu.VMEM((nqb, rows, LANES), I32)],
        compiler_params=_cparams(("parallel", "arbitrary")),
        name="indexer_prompt",
    )(z3, z3, z3)


ATTN_KSTEP = 512
ATTN_TILES = ATTN_KSTEP // LANES


def _attn_prompt_body(bfar_ref, q_ref, k_ref, v_ref, mask_ref, bias_ref, o_ref, m_ref, l_ref, acc_ref):
    j = pl.program_id(1)
    scale = HEAD_DIM ** -0.5
    q = q_ref[0].astype(BF)
    m_ref[...] = jnp.full(m_ref.shape, NEG_BIG, F32)
    l_ref[...] = jnp.zeros(l_ref.shape, F32)
    acc_ref[...] = jnp.zeros(acc_ref.shape, F32)

    def process(sb, near):
        k0 = pl.multiple_of(sb * ATTN_KSTEP, ATTN_KSTEP)
        kblk = k_ref[0, pl.ds(k0, ATTN_KSTEP), :]
        vblk = v_ref[0, pl.ds(k0, ATTN_KSTEP), :]
        tiles = [sb * ATTN_TILES + i for i in range(ATTN_TILES)]
        sel = jnp.concatenate([mask_ref[0, 0, t] for t in tiles], axis=1).astype(F32) > 0.0
        if near:
            bidx = [jnp.where(j - t == 0, 0, jnp.where(j - t == 1, 1, 2)) for t in tiles]
        for h in range(HEADS):
            cols = slice(h * HEAD_DIM, (h + 1) * HEAD_DIM)
            lg = _dot_nt(q[:, cols], kblk[:, cols]) * scale
            if near:
                lg = lg + jnp.concatenate([bias_ref[bi, h] for bi in bidx], axis=1)
            else:
                lg = lg + bfar_ref[h]
            lg = jnp.where(sel, lg, NEG_BIG)
            m_old = m_ref[h]
            m_new = jnp.maximum(m_old, jnp.max(lg, axis=-1, keepdims=True))
            p = jnp.exp(lg - jnp.concatenate([m_new] * ATTN_TILES, axis=1))
            alpha = jnp.exp(m_old - m_new)
            l_ref[h] = alpha * l_ref[h] + jnp.sum(p, axis=-1, keepdims=True)
            acc_ref[:, cols] = alpha * acc_ref[:, cols] + _dot(p.astype(BF), vblk[:, cols])
            m_ref[h] = m_new

    n_far = jnp.maximum((j - 1) // ATTN_TILES, 0)

    def far_step(sb, carry):
        process(sb, False)
        return carry

    def near_step(sb, carry):
        process(sb, True)
        return carry

    lax.fori_loop(0, n_far, far_step, 0)
    lax.fori_loop(n_far, j // ATTN_TILES + 1, near_step, 0)

    for h in range(HEADS):
        cols = slice(h * HEAD_DIM, (h + 1) * HEAD_DIM)
        o_ref[0, :, cols] = acc_ref[:, cols] / l_ref[h]


def attn_prompt(z3, kv_bf, mask, bias_tiles, bias_far):
    bsz, seq = z3.shape[:2]
    nqb = seq // LANES
    width = HEADS * HEAD_DIM
    assert seq % ATTN_KSTEP == 0
    return pl.pallas_call(
        _attn_prompt_body,
        grid=(bsz, nqb),
        in_specs=[pl.BlockSpec(memory_space=pltpu.SMEM),
                  pl.BlockSpec((1, LANES, width), lambda b, j: (b, j, 4)),
                  pl.BlockSpec((1, seq, width), lambda b, j: (b, 0, 0)),
                  pl.BlockSpec((1, seq, width), lambda b, j: (b, 0, 1)),
                  pl.BlockSpec((1, 1, nqb, LANES, LANES), lambda b, j: (b, j, 0, 0, 0)),
                  pl.BlockSpec(bias_tiles.shape, lambda b, j: (0, 0, 0, 0))],
        out_specs=pl.BlockSpec((1, LANES, width), lambda b, j: (b, j, 0)),
        out_shape=jax.ShapeDtypeStruct((bsz, seq, width), F32),
        scratch_shapes=[pltpu.VMEM((HEADS, LANES, LANES), F32),
                        pltpu.VMEM((HEADS, LANES, LANES), F32),
                        pltpu.VMEM((LANES, width), F32)],
        compiler_params=_cparams(("parallel", "arbitrary")),
        name="attn_prompt",
    )(bias_far, z3, kv_bf, kv_bf, mask, bias_tiles)


def _indexer_sample_body(pt_ref, qi_ref, wcol_ref, kinew_ref, *rest, n_pages):
    ki_refs = rest[:n_pages]
    out_ref = rest[n_pages]
    qi = qi_ref[0].astype(BF)
    w = wcol_ref[0] * (IDX_DIM ** -0.5 * IDX_HEADS ** -0.5)
    for i in range(n_pages):
        s = _dot_nt(qi, ki_refs[i][0].astype(BF))
        out_ref[i, 0] = jnp.sum(w * jnp.maximum(s, 0.0), axis=0, keepdims=True)
    kn = kinew_ref[0].astype(BF).astype(F32)
    sn = jnp.sum(qi.astype(F32) * kn, axis=-1, keepdims=True)
    new = jnp.sum(w * jnp.maximum(sn, 0.0), axis=0, keepdims=True)
    lane = lax.broadcasted_iota(I32, (1, LANES), 1)
    out_ref[n_pages, 0] = jnp.where(lane == 0, new, -jnp.inf)


def indexer_sample(page_table, qi3, wcol, ki_new, ki_pool):
    bsz, n_pages = page_table.shape
    ki_spec = lambda i: pl.BlockSpec((1, PAGE, IDX_DIM), lambda b, pt, i=i: (pt[b, i], 0, 0))
    grid_spec = pltpu.PrefetchScalarGridSpec(
        num_scalar_prefetch=1,
        grid=(bsz,),
        in_specs=[pl.BlockSpec((1, IDX_HEADS, IDX_DIM), lambda b, pt: (b, 0, 0)),
                  pl.BlockSpec((1, IDX_HEADS, 1), lambda b, pt: (b, 0, 0)),
                  pl.BlockSpec((1, 1, IDX_DIM), lambda b, pt: (b, 0, 0))]
                 + [ki_spec(i) for i in range(n_pages)],
        out_specs=pl.BlockSpec((n_pages + 1, 1, 1, LANES), lambda b, pt: (0, b, 0, 0)),
    )
    out = pl.pallas_call(
        functools.partial(_indexer_sample_body, n_pages=n_pages),
        grid_spec=grid_spec,
        out_shape=jax.ShapeDtypeStruct((n_pages + 1, bsz, 1, LANES), F32),
        compiler_params=_cparams(("arbitrary",)),
        name="indexer_sample",
    )(page_table, qi3, wcol, ki_new, *([ki_pool] * n_pages))
    return out.reshape(n_pages + 1, bsz, LANES)


def _select_body(score_ref, mask_ref, u_ref, *, n_tiles, k):
    def write_tile(t, m):
        mask_ref[t] = m

    _topk_mask(score_ref, u_ref, write_tile, n_tiles, n_tiles, 1, k)


def select_topk(scores, k):
    n_tiles, rows, _ = scores.shape
    return pl.pallas_call(
        functools.partial(_select_body, n_tiles=n_tiles, k=k),
        grid=(1,),
        in_specs=[pl.BlockSpec(scores.shape, lambda i: (0, 0, 0))],
        out_specs=pl.BlockSpec(scores.shape, lambda i: (0, 0, 0)),
        out_shape=jax.ShapeDtypeStruct(scores.shape, F32),
        scratch_shapes=[pltpu.VMEM(scores.shape, I32)],
        compiler_params=_cparams(("arbitrary",)),
        name="select_topk",
    )(scores)


SAMPLE_PAGE_GROUP = 8


def _attn_sample_body(pt_ref, idx_ref, off_ref, q_ref, kn_ref, vn_ref, btab_ref, *rest,
                      n_pages, past_len):
    g = SAMPLE_PAGE_GROUP
    k_refs = rest[:g]
    v_refs = rest[g:2 * g]
    o_ref, kbuf, vbuf, bbuf = rest[2 * g:]
    b = pl.program_id(0)
    s = pl.program_id(1)
    scale = HEAD_DIM ** -0.5

    for i in range(g):
        page = s * g + i

        def gather(t, carry, i=i, page=page):
            key = idx_ref[b, t]
            r = key - page * PAGE
            kbuf[t] = k_refs[i][0, r]
            vbuf[t] = v_refs[i][0, r]
            bbuf[t] = btab_ref[jnp.minimum(past_len - key, REL_MAX_DIST)]
            return carry

        lax.fori_loop(off_ref[b, page], off_ref[b, page + 1], gather, 0)

    @pl.when(s == pl.num_programs(1) - 1)
    def _():
        @pl.when(off_ref[b, n_pages + 1] > off_ref[b, n_pages])
        def _():
            kbuf[TOPK - 1] = kn_ref[0]
            vbuf[TOPK - 1] = vn_ref[0]
            bbuf[TOPK - 1] = btab_ref[0]

        q8 = q_ref[0]
        prod = (kbuf[...] * q8[None]).reshape(TOPK * HEADS, HEAD_DIM)
        ones = jnp.ones((HEAD_DIM, LANES), BF)
        hi = prod.astype(BF)
        lo = (prod - hi.astype(F32)).astype(BF)
        lg = (_dot(hi, ones) + _dot(lo, ones)).reshape(TOPK, HEADS, LANES) * scale + bbuf[...]
        m = jnp.max(lg, axis=0, keepdims=True)
        p = jnp.exp(lg - m)
        l = jnp.sum(p, axis=0)
        o_ref[0] = jnp.sum(p * vbuf[...], axis=0) / l


def attn_sample(page_table, sel_idx, sel_off, zs3, k_pool, v_pool, bias_by_dist, past_len):
    bsz, n_pages = page_table.shape
    g = SAMPLE_PAGE_GROUP
    assert n_pages % g == 0
    zspec = lambda cb: pl.BlockSpec((1, HEADS, HEAD_DIM), lambda b, s, pt, ix, of, cb=cb: (b, cb, 0))
    pool_spec = lambda i: pl.BlockSpec((1, PAGE, HEADS, HEAD_DIM),
                                       lambda b, s, pt, ix, of, i=i: (pt[b, s * g + i], 0, 0, 0))
    grid_spec = pltpu.PrefetchScalarGridSpec(
        num_scalar_prefetch=3,
        grid=(bsz, n_pages // g),
        in_specs=[zspec(4), zspec(5), zspec(6),
                  pl.BlockSpec(bias_by_dist.shape, lambda b, s, pt, ix, of: (0, 0, 0))]
                 + [pool_spec(i) for i in range(g)] * 2,
        out_specs=pl.BlockSpec((1, HEADS, HEAD_DIM), lambda b, s, pt, ix, of: (b, 0, 0)),
        scratch_shapes=[pltpu.VMEM((TOPK, HEADS, HEAD_DIM), F32)] * 3,
    )
    return pl.pallas_call(
        functools.partial(_attn_sample_body, n_pages=n_pages, past_len=past_len),
        grid_spec=grid_spec,
        out_shape=jax.ShapeDtypeStruct((bsz, HEADS, HEAD_DIM), F32),
        compiler_params=_cparams(("arbitrary", "arbitrary")),
        name="attn_sample",
    )(page_table, sel_idx, sel_off, zs3, zs3, zs3, bias_by_dist,
      *([k_pool] * g), *([v_pool] * g))


def _bucket_table(max_dist):
    exact = REL_BUCKETS // 2
    d = np.arange(max_dist + 1)
    df = np.maximum(d, 1).astype(np.float32)
    far = exact + (np.log(df / exact) / np.float32(math.log(REL_MAX_DIST / exact))
                   * (REL_BUCKETS - exact)).astype(np.int32)
    return np.where(d < exact, d, np.minimum(far, REL_BUCKETS - 1)).astype(np.int32)


def _bias_tables(rel_bias):
    tab = _bucket_table(2 * LANES)
    assert np.all(tab[REL_MAX_DIST:] == REL_BUCKETS - 1)
    i = np.arange(LANES)
    dist0 = np.maximum(i[:, None] - i[None, :], 0)
    dist1 = LANES + i[:, None] - i[None, :]
    far = np.full((LANES, LANES), REL_BUCKETS - 1)
    idx = np.stack([tab[dist0], tab[dist1], far])
    tiles = jnp.transpose(rel_bias[idx], (0, 3, 1, 2))
    by_dist = jnp.broadcast_to(rel_bias[tab[:REL_MAX_DIST + 1]][:, :, None],
                               (REL_MAX_DIST + 1, HEADS, LANES))
    return tiles.astype(F32), rel_bias[REL_BUCKETS - 1].astype(F32), by_dist.astype(F32)


def _conv_prompt_body(bg_ref, cg_ref, xt_ref, w_ref, v_ref, st_ref, carry_ref, *, tblock):
    t = pl.program_id(1)

    @pl.when(t == 0)
    def _():
        carry_ref[...] = jnp.zeros_like(carry_ref)

    u = cg_ref[0] * xt_ref[0]
    row = lax.broadcasted_iota(I32, u.shape, 0)
    c0 = carry_ref[0:1]
    c1 = carry_ref[1:2]
    u1 = jnp.where(row == 0, c1, pltpu.roll(u, 1, axis=0))
    u2 = jnp.where(row == 0, c0, jnp.where(row == 1, c1, pltpu.roll(u, 2, axis=0)))
    conv = w_ref[0:1] * u2 + w_ref[1:2] * u1 + w_ref[2:3] * u
    v_ref[0] = bg_ref[0] * conv
    last = u[tblock - 2:tblock]
    carry_ref[0:2] = last

    @pl.when(t == pl.num_programs(1) - 1)
    def _():
        st_ref[0] = last


def conv_prompt(zc3, w_conv, tblock=256):
    bsz, seq = zc3.shape[:2]
    c = zc3.shape[2] // 3
    zspec = lambda cb: pl.BlockSpec((1, tblock, c), lambda b, t, cb=cb: (b, t, cb))
    return pl.pallas_call(
        functools.partial(_conv_prompt_body, tblock=tblock),
        grid=(bsz, seq // tblock),
        in_specs=[zspec(0), zspec(1), zspec(2), pl.BlockSpec((3, c), lambda b, t: (0, 0))],
        out_specs=[pl.BlockSpec((1, tblock, c), lambda b, t: (b, t, 0)),
                   pl.BlockSpec((1, 2, c), lambda b, t: (b, 0, 0))],
        out_shape=[jax.ShapeDtypeStruct((bsz, seq, c), F32),
                   jax.ShapeDtypeStruct((bsz, 2, c), F32)],
        scratch_shapes=[pltpu.VMEM((8, c), F32)],
        compiler_params=_cparams(("parallel", "arbitrary")),
        name="conv_prompt",
    )(zc3, zc3, zc3, w_conv)


def _conv_sample_body(bg_ref, cg_ref, xt_ref, w_ref, s0_ref, s1_ref, v_ref, n0_ref, n1_ref):
    u = cg_ref[...] * xt_ref[...]
    conv = w_ref[0:1] * s0_ref[...] + w_ref[1:2] * s1_ref[...] + w_ref[2:3] * u
    v_ref[...] = bg_ref[...] * conv
    n0_ref[...] = s1_ref[...]
    n1_ref[...] = u


def conv_sample(zc, w_conv, s0, s1):
    bsz = zc.shape[0]
    c = zc.shape[1] // 3
    zspec = lambda cb: pl.BlockSpec((bsz, c), lambda i, cb=cb: (0, cb))
    full = pl.BlockSpec((bsz, c), lambda i: (0, 0))
    return pl.pallas_call(
        _conv_sample_body,
        grid=(1,),
        in_specs=[zspec(0), zspec(1), zspec(2), pl.BlockSpec((3, c), lambda i: (0, 0)), full, full],
        out_specs=[full, full, full],
        out_shape=[jax.ShapeDtypeStruct((bsz, c), F32)] * 3,
        compiler_params=_cparams(("arbitrary",)),
        name="conv_sample",
    )(zc, zc, zc, w_conv, s0, s1)


def _router_body(x_ref, g_ref, wr_ref, br_ref, h_ref, route_ref):
    x = x_ref[...]
    ms = jnp.mean(x * x, axis=-1, keepdims=True)
    h = x * lax.rsqrt(ms + RMS_EPS) * g_ref[...]
    h_ref[...] = h.astype(BF)
    logits = jnp.dot(h, wr_ref[...], preferred_element_type=F32,
                     precision=lax.Precision.HIGHEST) + br_ref[...]
    lane = lax.broadcasted_iota(I32, logits.shape, 1)
    big = np.int32(1 << 20)
    lg = jnp.where(lane < N_GROUPS, logits, -jnp.inf)
    g_max = jnp.max(lg, axis=-1, keepdims=True)
    g_idx = jnp.min(jnp.where(lg == g_max, lane, big), axis=-1, keepdims=True)
    g_w = 1.0 / jnp.sum(jnp.exp(lg - g_max), axis=-1, keepdims=True)
    first = N_GROUPS + EXP_PER_GROUP * g_idx
    le = jnp.where((lane >= first) & (lane < first + EXP_PER_GROUP), logits, -jnp.inf)
    l1 = jnp.max(le, axis=-1, keepdims=True)
    i1 = jnp.min(jnp.where(le == l1, lane, big), axis=-1, keepdims=True)
    le2 = jnp.where(lane == i1, -jnp.inf, le)
    l2 = jnp.max(le2, axis=-1, keepdims=True)
    i2 = jnp.min(jnp.where(le2 == l2, lane, big), axis=-1, keepdims=True)
    r = jnp.exp(l2 - l1)
    w1 = g_w / (1.0 + r)
    w2 = g_w * r / (1.0 + r)
    e1 = (i1 - N_GROUPS).astype(F32)
    e2 = (i2 - N_GROUPS).astype(F32)
    route_ref[...] = jnp.where(lane == 0, e1, jnp.where(lane == 1, e2,
                               jnp.where(lane == 2, w1, jnp.where(lane == 3, w2, 0.0))))


def moe_router(x, g, wr, br, tm):
    m, k = x.shape
    return pl.pallas_call(
        _router_body,
        grid=(m // tm,),
        in_specs=[pl.BlockSpec((tm, k), lambda i: (i, 0)),
                  pl.BlockSpec((1, k), lambda i: (0, 0)),
                  pl.BlockSpec((k, LANES), lambda i: (0, 0)),
                  pl.BlockSpec((1, LANES), lambda i: (0, 0))],
        out_specs=[pl.BlockSpec((tm, k), lambda i: (i, 0)),
                   pl.BlockSpec((tm, LANES), lambda i: (i, 0))],
        out_shape=[jax.ShapeDtypeStruct((m, k), BF),
                   jax.ShapeDtypeStruct((m, LANES), F32)],
        compiler_params=_cparams(("parallel",)),
        name="moe_router",
    )(x, g.reshape(1, k), wr, br)


def _experts_body(te_ref, nu_ref, h_ref, gate_ref, wg_ref, wu_ref, wd_ref, o_ref):
    i = pl.program_id(0)

    @pl.when(i < nu_ref[0])
    def _():
        h = h_ref[...]
        a = _dot(h, wg_ref[0])
        b = _dot(h, wu_ref[0])
        hid = a * _sigmoid(a) * b * gate_ref[...]
        o_ref[...] = _dot(hid.astype(BF), wd_ref[0])

    @pl.when(i >= nu_ref[0])
    def _():
        o_ref[...] = jnp.zeros_like(o_ref)


def moe_experts(tile_expert, n_used, hs, gate_col, wg_bf, wu_bf, wd_bf):
    p, k = hs.shape
    f = wg_bf.shape[2]
    n_tiles = p // MOE_TILE
    grid_spec = pltpu.PrefetchScalarGridSpec(
        num_scalar_prefetch=2,
        grid=(n_tiles,),
        in_specs=[pl.BlockSpec((MOE_TILE, k), lambda i, te, nu: (i, 0)),
                  pl.BlockSpec((MOE_TILE, 1), lambda i, te, nu: (i, 0)),
                  pl.BlockSpec((1, k, f), lambda i, te, nu: (te[i], 0, 0)),
                  pl.BlockSpec((1, k, f), lambda i, te, nu: (te[i], 0, 0)),
                  pl.BlockSpec((1, f, k), lambda i, te, nu: (te[i], 0, 0))],
        out_specs=pl.BlockSpec((MOE_TILE, k), lambda i, te, nu: (i, 0)),
    )
    return pl.pallas_call(
        _experts_body,
        grid_spec=grid_spec,
        out_shape=jax.ShapeDtypeStruct((p, k), F32),
        compiler_params=_cparams(("arbitrary",)),
        name="moe_experts",
    )(tile_expert, n_used, hs, gate_col, wg_bf, wu_bf, wd_bf)


def hier_moe(xs, tms, g, wrg, brg, wre, bre, wg_bf, wu_bf, wd_bf):
    d = xs[0].shape[1]
    m = sum(x.shape[0] for x in xs)
    wr = jnp.zeros((d, LANES), F32)
    wr = wr.at[:, :N_GROUPS].set(wrg).at[:, N_GROUPS:N_GROUPS + N_EXPERTS].set(wre.reshape(d, N_EXPERTS))
    br = jnp.zeros((1, LANES), F32)
    br = br.at[0, :N_GROUPS].set(brg).at[0, N_GROUPS:N_GROUPS + N_EXPERTS].set(bre.reshape(N_EXPERTS))
    routed = [moe_router(x, g, wr, br, tm) for x, tm in zip(xs, tms)]
    h_bf = jnp.concatenate([r[0] for r in routed], axis=0)
    route = jnp.concatenate([r[1] for r in routed], axis=0)

    eid = route[:, 0:2].astype(I32).reshape(-1)
    gate = route[:, 2:4].reshape(-1)
    onehot = (eid[:, None] == jnp.arange(N_EXPERTS, dtype=I32)[None, :]).astype(I32)
    rank = jnp.take_along_axis(jnp.cumsum(onehot, axis=0) - onehot, eid[:, None], axis=1)[:, 0]
    counts = jnp.sum(onehot, axis=0)
    padded = ((counts + MOE_TILE - 1) // MOE_TILE) * MOE_TILE
    ends = jnp.cumsum(padded)
    pos = (ends - padded)[eid] + rank
    n_rows = 2 * m + N_EXPERTS * MOE_TILE
    n_rows = -(-n_rows // MOE_TILE) * MOE_TILE
    src = jnp.zeros((n_rows,), I32).at[pos].set(jnp.arange(2 * m, dtype=I32) // 2)
    gate_sorted = jnp.zeros((n_rows,), F32).at[pos].set(gate)
    tile_start = jnp.arange(n_rows // MOE_TILE, dtype=I32) * MOE_TILE
    tile_expert = jnp.minimum(jnp.searchsorted(ends, tile_start, side="right"),
                              N_EXPERTS - 1).astype(I32)
    n_used = (ends[-1] // MOE_TILE).astype(I32).reshape(1)

    hs = jnp.take(h_bf, src, axis=0)
    out = moe_experts(tile_expert, n_used, hs, gate_sorted[:, None], wg_bf, wu_bf, wd_bf)
    pos2 = pos.reshape(m, 2)
    res, r0 = [], 0
    for x in xs:
        pp = pos2[r0:r0 + x.shape[0]]
        res.append(x + jnp.take(out, pp[:, 0], axis=0) + jnp.take(out, pp[:, 1], axis=0))
        r0 += x.shape[0]
    return res


def kernel(x_prompt, x_sample, cache_k, cache_v, cache_ki, state_hgrn, state_conv, page_table,
           norm_mix_g, norm_ffn_g, final_g, w_in_even, w_out_even, hgrn_lb_logits, hgrn_norm_g,
           rel_bias, w_in_conv, w_conv, w_out_conv, w_router_g, b_router_g, w_router_e,
           b_router_e, w_gate, w_up, w_down):
    bsz, seq, d = x_prompt.shape
    dec = x_sample.shape[0]
    n_p = bsz * seq
    tm_p, tm_s = 512, 128
    tms = (tm_p, tm_s)
    assert n_p % tm_p == 0 and dec % tm_s == 0 and x_sample.shape[1] == 1
    width = HEADS * HEAD_DIM
    even_in = w_in_even.shape[2]
    even_pad = -(-even_in // LANES) * LANES
    tail0 = even_pad - LANES
    n_pool = cache_k.shape[1]

    xp = x_prompt.reshape(n_p, d)
    xs = x_sample.reshape(dec, d)

    lbs = jnp.cumsum(jax.nn.softmax(hgrn_lb_logits.astype(F32), axis=0), axis=0)[:-1]
    bias_tiles, bias_far, bias_by_dist = _bias_tables(rel_bias)

    w_in = jnp.pad(w_in_even[0], ((0, 0), (0, even_pad - even_in))).astype(BF)
    zp = norm_matmul(xp, norm_mix_g[0], w_in, 2 * tm_p, 640)
    zs = norm_matmul(xs, norm_mix_g[0], w_in, tm_s, 640)
    zp3 = zp.reshape(bsz, seq, even_pad)
    zs3 = zs.reshape(dec, even_pad // LANES, LANES)

    oa_p, hgrn_p = hgrn_prompt(zp3, lbs[0], hgrn_norm_g[0])
    oa_s, hgrn_s = hgrn_sample(zs3, lbs[0], hgrn_norm_g[0], state_hgrn[0])

    mask_p = indexer_prompt(zp3, TOPK)
    kv_bf = zp3[:, :, 5 * width:7 * width].astype(BF)
    ob_p = attn_prompt(zp3, kv_bf, mask_p, bias_tiles, bias_far)

    qi3 = zs[:, 7 * width:8 * width].reshape(dec, IDX_HEADS, IDX_DIM)
    ki_new = zs[:, tail0:tail0 + IDX_DIM].reshape(dec, 1, IDX_DIM)
    wcol = zs[:, tail0 + IDX_DIM:tail0 + IDX_DIM + IDX_HEADS].reshape(dec, IDX_HEADS, 1)
    pages = page_table + 0 * n_pool
    scores_s = indexer_sample(pages, qi3, wcol, ki_new,
                              cache_ki.reshape(-1, PAGE, IDX_DIM))
    mask_s = select_topk(scores_s, TOPK)
    sel = jnp.transpose(mask_s, (1, 0, 2)).astype(I32)
    csum = jnp.cumsum(sel.reshape(dec, -1), axis=1)
    want = jnp.arange(1, TOPK + 1, dtype=I32)
    sel_idx = jax.vmap(lambda c: jnp.searchsorted(c, want, side="left"))(csum).astype(I32)
    sel_off = jnp.concatenate([jnp.zeros((dec, 1), I32),
                               jnp.cumsum(jnp.sum(sel, axis=2), axis=1)], axis=1).astype(I32)
    n_pages = page_table.shape[1]
    ob_s = attn_sample(pages, sel_idx, sel_off, zs3,
                       cache_k.reshape(-1, PAGE, HEADS, HEAD_DIM),
                       cache_v.reshape(-1, PAGE, HEADS, HEAD_DIM), bias_by_dist, n_pages * PAGE)

    w_out = w_out_even[0].astype(BF)
    w_out_ab = [w_out[:width], w_out[width:]]
    xp = matmul_residual([oa_p.reshape(n_p, width), ob_p.reshape(n_p, width)], w_out_ab, xp, 2 * tm_p, 512)
    xs = matmul_residual([oa_s.reshape(dec, width), ob_s.reshape(dec, width)], w_out_ab, xs, tm_s, 512)

    xp, xs = hier_moe([xp, xs], tms, norm_ffn_g[0], w_router_g[0], b_router_g[0], w_router_e[0],
                      b_router_e[0], w_gate[0].astype(BF), w_up[0].astype(BF), w_down[0].astype(BF))

    w_in_c = w_in_conv[0].astype(BF)
    zcp = norm_matmul(xp, norm_mix_g[1], w_in_c, 2 * tm_p, 1024)
    zcs = norm_matmul(xs, norm_mix_g[1], w_in_c, tm_s, 512)
    cw = zcp.shape[1] // 3
    v_p, conv_p = conv_prompt(zcp.reshape(bsz, seq, 3 * cw), w_conv[0])
    v_s, cs0, cs1 = conv_sample(zcs, w_conv[0], state_conv[0, :, 0], state_conv[0, :, 1])
    w_out_c = [w_out_conv[0].astype(BF)]
    xp = matmul_residual([v_p.reshape(n_p, cw)], w_out_c, xp, 2 * tm_p, 512)
    xs = matmul_residual([v_s], w_out_c, xs, tm_s, 512)

    xp, xs = hier_moe([xp, xs], tms, norm_ffn_g[1], w_router_g[1], b_router_g[1], w_router_e[1],
                      b_router_e[1], w_gate[1].astype(BF), w_up[1].astype(BF), w_down[1].astype(BF))

    yp = rmsnorm(xp, final_g, tm_p)
    ys = rmsnorm(xs, final_g, tm_s)

    kcol, vcol = 5 * width, 6 * width
    heads = lambda a, n: a.reshape(1, *n, HEADS, HEAD_DIM)
    return (yp.reshape(bsz, seq, d),
            ys.reshape(dec, 1, d),
            heads(zp[:, kcol:kcol + width], (bsz, seq)),
            heads(zp[:, vcol:vcol + width], (bsz, seq)),
            zp[:, tail0:tail0 + IDX_DIM].reshape(1, bsz, seq, IDX_DIM),
            hgrn_p[None],
            conv_p[None],
            heads(zs[:, kcol:kcol + width], (dec, 1)),
            heads(zs[:, vcol:vcol + width], (dec, 1)),
            zs[:, tail0:tail0 + IDX_DIM].reshape(1, dec, 1, IDX_DIM),
            hgrn_s[None],
            jnp.stack([cs0, cs1], axis=1)[None])
```

```python
import functools
import math

import numpy as np
import jax
import jax.numpy as jnp
from jax import lax
from jax.experimental import pallas as pl
from jax.experimental.pallas import tpu as pltpu

F32 = jnp.float32
BF = jnp.bfloat16
I32 = jnp.int32

RMS_EPS = 1e-6
LANES = 128
NEG_BIG = -1e30
VMEM_LIMIT = 56 * 1024 * 1024

D_MODEL = 2048
HEADS = 8
HEAD_DIM = 128
IDX_HEADS = 16
IDX_DIM = 64
TOPK = 256
REL_BUCKETS = 32
REL_MAX_DIST = 128
N_EXPERTS = 16
EXP_PER_GROUP = 4
N_GROUPS = 4
D_EXPERT = 512
PAGE = 128

HGRN_CHUNK = 64
MOE_TILE = 256


def _cparams(sem):
    return pltpu.CompilerParams(dimension_semantics=sem, vmem_limit_bytes=VMEM_LIMIT)


def _dot(a, b):
    return jnp.dot(a, b, preferred_element_type=F32)


def _dot_nt(a, b):
    return lax.dot_general(a, b, (((1,), (1,)), ((), ())), preferred_element_type=F32)


def _dot_tn(a, b):
    return lax.dot_general(a, b, (((0,), (0,)), ((), ())), preferred_element_type=F32)


def _sigmoid(x):
    return 1.0 / (1.0 + jnp.exp(-x))


def _norm_mm_body(x_ref, g_ref, w_ref, o_ref, h_ref):
    @pl.when(pl.program_id(1) == 0)
    def _():
        x = x_ref[...]
        ms = jnp.mean(x * x, axis=-1, keepdims=True)
        h_ref[...] = (x * lax.rsqrt(ms + RMS_EPS) * g_ref[...]).astype(BF)

    o_ref[...] = _dot(h_ref[...], w_ref[...])


def norm_matmul(x, g, w_bf, tm, tn):
    m, k = x.shape
    n = w_bf.shape[1]
    return pl.pallas_call(
        _norm_mm_body,
        grid=(m // tm, n // tn),
        in_specs=[pl.BlockSpec((tm, k), lambda i, j: (i, 0)),
                  pl.BlockSpec((1, k), lambda i, j: (0, 0)),
                  pl.BlockSpec((k, tn), lambda i, j: (0, j))],
        out_specs=pl.BlockSpec((tm, tn), lambda i, j: (i, j)),
        out_shape=jax.ShapeDtypeStruct((m, n), F32),
        scratch_shapes=[pltpu.VMEM((tm, k), BF)],
        compiler_params=_cparams(("parallel", "arbitrary")),
        name="norm_matmul",
    )(x, g.reshape(1, k), w_bf)


def _mm_res_body(*refs, n_lhs):
    a_refs = refs[:n_lhs]
    w_refs = refs[n_lhs:2 * n_lhs]
    r_ref = refs[2 * n_lhs]
    o_ref = refs[2 * n_lhs + 1]
    s_refs = refs[2 * n_lhs + 2:]

    @pl.when(pl.program_id(1) == 0)
    def _():
        for a_ref, s_ref in zip(a_refs, s_refs):
            s_ref[...] = a_ref[...].astype(BF)

    acc = r_ref[...]
    for s_ref, w_ref in zip(s_refs, w_refs):
        acc = acc + _dot(s_ref[...], w_ref[...])
    o_ref[...] = acc


def matmul_residual(lhs, ws_bf, res, tm, tn):
    m, n = res.shape
    n_lhs = len(lhs)
    in_specs = ([pl.BlockSpec((tm, a.shape[1]), lambda i, j: (i, 0)) for a in lhs]
                + [pl.BlockSpec((w.shape[0], tn), lambda i, j: (0, j)) for w in ws_bf]
                + [pl.BlockSpec((tm, tn), lambda i, j: (i, j))])
    return pl.pallas_call(
        functools.partial(_mm_res_body, n_lhs=n_lhs),
        grid=(m // tm, n // tn),
        in_specs=in_specs,
        out_specs=pl.BlockSpec((tm, tn), lambda i, j: (i, j)),
        out_shape=jax.ShapeDtypeStruct((m, n), F32),
        scratch_shapes=[pltpu.VMEM((tm, a.shape[1]), BF) for a in lhs],
        compiler_params=_cparams(("parallel", "arbitrary")),
        name="matmul_residual",
    )(*lhs, *ws_bf, res)


def _rmsnorm_body(x_ref, g_ref, o_ref):
    x = x_ref[...]
    ms = jnp.mean(x * x, axis=-1, keepdims=True)
    o_ref[...] = x * lax.rsqrt(ms + RMS_EPS) * g_ref[...]


def rmsnorm(x, g, tm):
    m, k = x.shape
    return pl.pallas_call(
        _rmsnorm_body,
        grid=(m // tm,),
        in_specs=[pl.BlockSpec((tm, k), lambda i: (i, 0)),
                  pl.BlockSpec((1, k), lambda i: (0, 0))],
        out_specs=pl.BlockSpec((tm, k), lambda i: (i, 0)),
        out_shape=jax.ShapeDtypeStruct((m, k), F32),
        compiler_params=_cparams(("parallel",)),
        name="rmsnorm",
    )(x, g.reshape(1, k))


def _hgrn_static(c):
    levels = []
    m = 1
    while m < c:
        levels.append(m)
        m *= 2
    t = np.arange(c)
    rows = [t[None, :] <= t[:, None]]
    masks = [np.eye(c, dtype=bool)]
    for m in levels:
        blk = t // (2 * m)
        pos = t % (2 * m)
        bnd = blk * 2 * m + m - 1
        right = pos >= m
        left = pos < m
        e_rows = (t[None, :] > bnd[:, None]) & (t[None, :] <= t[:, None]) & right[:, None]
        f_rows = (t[None, :] > t[:, None]) & (t[None, :] <= bnd[:, None]) & left[:, None]
        rows.append(e_rows | f_rows)
        masks.append((blk[:, None] == blk[None, :]) & right[:, None] & left[None, :])
    m_all = np.stack(rows).astype(np.float32)
    masks = np.stack(masks).astype(np.float32)
    return m_all, masks, len(levels)


def _hgrn_gates(qa, fa, lb):
    f = lb + (1.0 - lb) * _sigmoid(fa)
    q = qa * _sigmoid(qa)
    return q, f


def _hgrn_prompt_body(qa_ref, fa_ref, ia_ref, ga_ref, lb_ref, gn_ref, mall_ref, masks_ref,
                      oa_ref, st_out_ref, st_ref, *, chunk, tblock, n_levels):
    c = chunk
    t_idx = pl.program_id(1)

    @pl.when(t_idx == 0)
    def _():
        st_ref[...] = jnp.zeros_like(st_ref)

    gn = gn_ref[...]

    def chunk_step(ci, carry):
        r0 = pl.multiple_of(ci * c, c)
        for h in range(HEADS):
            cols = slice(h * HEAD_DIM, (h + 1) * HEAD_DIM)
            qa = qa_ref[0, pl.ds(r0, c), cols]
            fa = fa_ref[0, pl.ds(r0, c), cols]
            ia = ia_ref[0, pl.ds(r0, c), cols]
            ga = ga_ref[0, pl.ds(r0, c), cols]
            q, f = _hgrn_gates(qa, fa, lb_ref[:, cols])
            k = 1.0 - f
            g = jnp.log(f)
            g_hi = g.astype(BF)
            g_lo = (g - g_hi.astype(F32)).astype(BF)
            b = _dot(mall_ref[0], g_hi) + _dot(mall_ref[0], g_lo)
            b_last = b[c - 1:c]
            st = st_ref[h]
            v = ia.astype(BF)
            o = _dot_nt((q * jnp.exp(b)).astype(BF), st.astype(BF))
            a = masks_ref[0] * _dot_nt(q.astype(BF), k.astype(BF))
            for li in range(n_levels):
                ml = mall_ref[1 + li]
                w = jnp.exp(_dot(ml, g_hi) + _dot(ml, g_lo))
                a = a + masks_ref[1 + li] * _dot_nt((q * w).astype(BF), (k * w).astype(BF))
            o = o + _dot(a.astype(BF), v)
            k_st = (k * jnp.exp(b_last - b)).astype(BF)
            st_ref[h] = st * jnp.exp(b_last) + _dot_tn(v, k_st)
            ms = jnp.mean(o * o, axis=-1, keepdims=True)
            y = o * lax.rsqrt(ms + RMS_EPS) * gn * (ga * _sigmoid(ga))
            oa_ref[0, pl.ds(r0, c), cols] = y
        return carry

    lax.fori_loop(0, tblock // c, chunk_step, 0)

    @pl.when(t_idx == pl.num_programs(1) - 1)
    def _():
        st_out_ref[0] = st_ref[...]


def hgrn_prompt(z3, lb, gn, tblock=256, chunk=HGRN_CHUNK):
    bsz, seq = z3.shape[:2]
    width = HEADS * HEAD_DIM
    m_all, masks, n_levels = _hgrn_static(chunk)
    zspec = lambda cb: pl.BlockSpec((1, tblock, width), lambda b, t, cb=cb: (b, t, cb))
    oa, st = pl.pallas_call(
        functools.partial(_hgrn_prompt_body, chunk=chunk, tblock=tblock, n_levels=n_levels),
        grid=(bsz, seq // tblock),
        in_specs=[zspec(0), zspec(1), zspec(2), zspec(3),
                  pl.BlockSpec((1, width), lambda b, t: (0, 0)),
                  pl.BlockSpec((1, HEAD_DIM), lambda b, t: (0, 0)),
                  pl.BlockSpec(m_all.shape, lambda b, t: (0, 0, 0)),
                  pl.BlockSpec(masks.shape, lambda b, t: (0, 0, 0))],
        out_specs=[pl.BlockSpec((1, tblock, width), lambda b, t: (b, t, 0)),
                   pl.BlockSpec((1, HEADS, HEAD_DIM, HEAD_DIM), lambda b, t: (b, 0, 0, 0))],
        out_shape=[jax.ShapeDtypeStruct((bsz, seq, width), F32),
                   jax.ShapeDtypeStruct((bsz, HEADS, HEAD_DIM, HEAD_DIM), F32)],
        scratch_shapes=[pltpu.VMEM((HEADS, HEAD_DIM, HEAD_DIM), F32)],
        compiler_params=_cparams(("parallel", "arbitrary")),
        name="hgrn_prompt",
    )(z3, z3, z3, z3, lb.reshape(1, width), gn.reshape(1, HEAD_DIM),
      jnp.asarray(m_all, BF), jnp.asarray(masks, F32))
    return oa, jnp.swapaxes(st, -1, -2)


def _col(row, eye):
    return jnp.sum(eye * row, axis=1, keepdims=True)


def _hgrn_sample_body(qa_ref, fa_ref, ia_ref, ga_ref, lb_ref, gn_ref, s_ref, oa_ref, so_ref):
    eye = (lax.broadcasted_iota(I32, (HEAD_DIM, HEAD_DIM), 0)
           == lax.broadcasted_iota(I32, (HEAD_DIM, HEAD_DIM), 1)).astype(F32)
    q8, f8 = _hgrn_gates(qa_ref[0], fa_ref[0], lb_ref[...])
    ga = ga_ref[0]
    gate = ga * _sigmoid(ga)
    outs = []
    for h in range(HEADS):
        f_col = _col(f8[h:h + 1], eye)
        q_col = _col(q8[h:h + 1], eye)
        s_new = f_col * s_ref[0, h] + (1.0 - f_col) * ia_ref[0, h:h + 1]
        so_ref[0, h] = s_new
        outs.append(jnp.sum(q_col * s_new, axis=0, keepdims=True))
    o = jnp.concatenate(outs, axis=0)
    ms = jnp.mean(o * o, axis=-1, keepdims=True)
    oa_ref[0] = o * lax.rsqrt(ms + RMS_EPS) * gn_ref[...] * gate


def hgrn_sample(zs3, lb, gn, s0):
    bsz = zs3.shape[0]
    zspec = lambda cb: pl.BlockSpec((1, HEADS, HEAD_DIM), lambda b, cb=cb: (b, cb, 0))
    sspec = pl.BlockSpec((1, HEADS, HEAD_DIM, HEAD_DIM), lambda b: (b, 0, 0, 0))
    return pl.pallas_call(
        _hgrn_sample_body,
        grid=(bsz,),
        in_specs=[zspec(0), zspec(1), zspec(2), zspec(3),
                  pl.BlockSpec((HEADS, HEAD_DIM), lambda b: (0, 0)),
                  pl.BlockSpec((1, HEAD_DIM), lambda b: (0, 0)),
                  sspec],
        out_specs=[pl.BlockSpec((1, HEADS, HEAD_DIM), lambda b: (b, 0, 0)), sspec],
        out_shape=[jax.ShapeDtypeStruct((bsz, HEADS, HEAD_DIM), F32),
                   jax.ShapeDtypeStruct(s0.shape, F32)],
        compiler_params=_cparams(("parallel",)),
        name="hgrn_sample",
    )(zs3, zs3, zs3, zs3, lb.reshape(HEADS, HEAD_DIM), gn.reshape(1, HEAD_DIM), s0)


_KEY_NEG_INF = np.int32(np.uint32(0x807FFFFF).astype(np.int64) - (1 << 32))
_INT_MIN = np.int32(-(1 << 31))


def _count(u_ref, n_groups, group, thr, cmp):
    rows = u_ref.shape[1]
    step = min(rows, LANES)
    parts = []
    for r0 in range(0, rows, step):
        t = jnp.broadcast_to(thr[r0:r0 + step], (step, LANES))

        def body(gi, acc, r0=r0, t=t):
            for i in range(group):
                acc = acc + cmp(u_ref[gi * group + i, r0:r0 + step], t).astype(F32)
            return acc

        acc = lax.fori_loop(0, n_groups, body, jnp.zeros((step, LANES), F32))
        parts.append(jnp.sum(acc, axis=-1, keepdims=True))
    return parts[0] if len(parts) == 1 else jnp.concatenate(parts, axis=0)


def _topk_mask(score_ref, u_ref, write_tile, n_tiles, n_groups, group, k):
    rows = score_ref.shape[1]
    n_live = n_groups * group

    def to_key(t, carry):
        bits = pltpu.bitcast(score_ref[t], I32)
        u_ref[t] = jnp.where(bits < 0, bits ^ np.int32(0x7FFFFFFF), bits)
        return carry

    lax.fori_loop(0, n_live, to_key, 0)

    kf = float(k)
    ge = lambda u, t: u >= t
    cnt = _count(u_ref, n_groups, group, jnp.zeros((rows, 1), I32), ge)
    lo = jnp.where(cnt >= kf, np.int32(0), _INT_MIN)

    def bit_step(i, lo):
        cand = lo | (np.int32(1) << (30 - i))
        cnt = _count(u_ref, n_groups, group, cand, ge)
        return jnp.where(cnt >= kf, cand, lo)

    lo = lax.fori_loop(0, 31, bit_step, lo)
    c_gt = _count(u_ref, n_groups, group, lo, lambda u, t: u > t)
    c_eq = _count(u_ref, n_groups, group, lo, lambda u, t: u == t)
    need = kf - c_gt
    real = lo > _KEY_NEG_INF
    excess = jnp.where(real & (c_eq > need), 1.0, 0.0)
    any_excess = jnp.max(excess) > 0.0

    @pl.when(jnp.logical_not(any_excess))
    def _():
        def emit(t, carry):
            u = u_ref[t]
            write_tile(t, jnp.where((u >= lo) & (u > _KEY_NEG_INF), 1.0, 0.0))
            return carry

        lax.fori_loop(0, n_live, emit, 0)

    @pl.when(any_excess)
    def _():
        upper = (lax.broadcasted_iota(I32, (LANES, LANES), 0)
                 <= lax.broadcasted_iota(I32, (LANES, LANES), 1)).astype(BF)

        def emit(t, seen):
            u = u_ref[t]
            eq = jnp.where(u == lo, 1.0, 0.0)
            prefix = seen + _dot(eq.astype(BF), upper)
            take = (u > lo) | ((u == lo) & (prefix <= need))
            write_tile(t, jnp.where(take & (u > _KEY_NEG_INF), 1.0, 0.0))
            return seen + jnp.sum(eq, axis=-1, keepdims=True)

        lax.fori_loop(0, n_live, emit, jnp.zeros((rows, 1), F32))

    def clear(t, carry):
        write_tile(t, jnp.zeros((rows, LANES), F32))
        return carry

    lax.fori_loop(n_live, n_tiles, clear, 0)


IDX_KCHUNK = 512
IDX_QROWS = 256


def _indexer_prompt_body(qi_ref, tailq_ref, tail_ref, mask_ref, score_ref, u_ref, *, seq):
    j = pl.program_id(1)
    rows = IDX_QROWS
    qblocks = rows // LANES
    n_tiles = seq // LANES
    tiles_per_chunk = IDX_KCHUNK // LANES
    qi = qi_ref[0].astype(BF)
    w = tailq_ref[0][:, IDX_DIM:IDX_DIM + IDX_HEADS] * (IDX_DIM ** -0.5 * IDX_HEADS ** -0.5)

    n_chunks = (j * rows + rows - 1) // IDX_KCHUNK + 1
    q_pos = j * rows + lax.broadcasted_iota(I32, (rows, IDX_KCHUNK), 0)

    def chunk_step(ci, carry):
        k0 = pl.multiple_of(ci * IDX_KCHUNK, IDX_KCHUNK)
        kic = tail_ref[0, pl.ds(k0, IDX_KCHUNK), :][:, 0:IDX_DIM].astype(BF)
        acc = jnp.zeros((rows, IDX_KCHUNK), F32)
        for h in range(IDX_HEADS):
            s = _dot_nt(qi[:, h * IDX_DIM:(h + 1) * IDX_DIM], kic)
            acc = acc + w[:, h:h + 1] * jnp.maximum(s, 0.0)
        k_pos = k0 + lax.broadcasted_iota(I32, (rows, IDX_KCHUNK), 1)
        acc = jnp.where(k_pos <= q_pos, acc, -jnp.inf)
        for i in range(tiles_per_chunk):
            score_ref[ci * tiles_per_chunk + i] = acc[:, i * LANES:(i + 1) * LANES]
        return carry

    lax.fori_loop(0, n_chunks, chunk_step, 0)

    def write_tile(t, m):
        mb = m.astype(BF)
        for qb in range(qblocks):
            mask_ref[0, qb, t] = mb[qb * LANES:(qb + 1) * LANES]

    _topk_mask(score_ref, u_ref, write_tile, n_tiles, n_chunks, tiles_per_chunk, TOPK)


def indexer_prompt(z3, k):
    bsz, seq = z3.shape[:2]
    nqb = seq // LANES
    rows = IDX_QROWS
    tail_block = z3.shape[2] // LANES - 1
    assert k == TOPK and seq % rows == 0 and seq % IDX_KCHUNK == 0
    return pl.pallas_call(
        functools.partial(_indexer_prompt_body, seq=seq),
        grid=(bsz, seq // rows),
        in_specs=[pl.BlockSpec((1, rows, IDX_HEADS * IDX_DIM), lambda b, j: (b, j, 7)),
                  pl.BlockSpec((1, rows, LANES), lambda b, j: (b, j, tail_block)),
                  pl.BlockSpec((1, seq, LANES), lambda b, j: (b, 0, tail_block))],
        out_specs=pl.BlockSpec((1, rows // LANES, nqb, LANES, LANES), lambda b, j: (b, j, 0, 0, 0)),
        out_shape=jax.ShapeDtypeStruct((bsz, nqb, nqb, LANES, LANES), BF),
        scratch_shapes=[pltpu.VMEM((nqb, rows, LANES), F32),
                        pltpu.VMEM((nqb, rows, LANES), I32)],
        compiler_params=_cparams(("parallel", "arbitrary")),
        name="indexer_prompt",
    )(z3, z3, z3)


ATTN_KSTEP = 512
ATTN_TILES = ATTN_KSTEP // LANES


def _attn_prompt_body(bfar_ref, q_ref, k_ref, v_ref, mask_ref, bias_ref, o_ref, m_ref, l_ref, acc_ref):
    j = pl.program_id(1)
    scale = HEAD_DIM ** -0.5
    q = q_ref[0].astype(BF)
    m_ref[...] = jnp.full(m_ref.shape, NEG_BIG, F32)
    l_ref[...] = jnp.zeros(l_ref.shape, F32)
    acc_ref[...] = jnp.zeros(acc_ref.shape, F32)

    def process(sb, near):
        k0 = pl.multiple_of(sb * ATTN_KSTEP, ATTN_KSTEP)
        kblk = k_ref[0, pl.ds(k0, ATTN_KSTEP), :]
        vblk = v_ref[0, pl.ds(k0, ATTN_KSTEP), :]
        tiles = [sb * ATTN_TILES + i for i in range(ATTN_TILES)]
        sel = jnp.concatenate([mask_ref[0, 0, t] for t in tiles], axis=1).astype(F32) > 0.0
        if near:
            bidx = [jnp.where(j - t == 0, 0, jnp.where(j - t == 1, 1, 2)) for t in tiles]
        for h in range(HEADS):
            cols = slice(h * HEAD_DIM, (h + 1) * HEAD_DIM)
            lg = _dot_nt(q[:, cols], kblk[:, cols]) * scale
            if near:
                lg = lg + jnp.concatenate([bias_ref[bi, h] for bi in bidx], axis=1)
            else:
                lg = lg + bfar_ref[h]
            lg = jnp.where(sel, lg, NEG_BIG)
            m_old = m_ref[h]
            m_new = jnp.maximum(m_old, jnp.max(lg, axis=-1, keepdims=True))
            p = jnp.exp(lg - jnp.concatenate([m_new] * ATTN_TILES, axis=1))
            alpha = jnp.exp(m_old - m_new)
            l_ref[h] = alpha * l_ref[h] + jnp.sum(p, axis=-1, keepdims=True)
            acc_ref[:, cols] = alpha * acc_ref[:, cols] + _dot(p.astype(BF), vblk[:, cols])
            m_ref[h] = m_new

    n_far = jnp.maximum((j - 1) // ATTN_TILES, 0)

    def far_step(sb, carry):
        process(sb, False)
        return carry

    def near_step(sb, carry):
        process(sb, True)
        return carry

    lax.fori_loop(0, n_far, far_step, 0)
    lax.fori_loop(n_far, j // ATTN_TILES + 1, near_step, 0)

    for h in range(HEADS):
        cols = slice(h * HEAD_DIM, (h + 1) * HEAD_DIM)
        o_ref[0, :, cols] = acc_ref[:, cols] / l_ref[h]


def attn_prompt(z3, kv_bf, mask, bias_tiles, bias_far):
    bsz, seq = z3.shape[:2]
    nqb = seq // LANES
    width = HEADS * HEAD_DIM
    assert seq % ATTN_KSTEP == 0
    return pl.pallas_call(
        _attn_prompt_body,
        grid=(bsz, nqb),
        in_specs=[pl.BlockSpec(memory_space=pltpu.SMEM),
                  pl.BlockSpec((1, LANES, width), lambda b, j: (b, j, 4)),
                  pl.BlockSpec((1, seq, width), lambda b, j: (b, 0, 0)),
                  pl.BlockSpec((1, seq, width), lambda b, j: (b, 0, 1)),
                  pl.BlockSpec((1, 1, nqb, LANES, LANES), lambda b, j: (b, j, 0, 0, 0)),
                  pl.BlockSpec(bias_tiles.shape, lambda b, j: (0, 0, 0, 0))],
        out_specs=pl.BlockSpec((1, LANES, width), lambda b, j: (b, j, 0)),
        out_shape=jax.ShapeDtypeStruct((bsz, seq, width), F32),
        scratch_shapes=[pltpu.VMEM((HEADS, LANES, LANES), F32),
                        pltpu.VMEM((HEADS, LANES, LANES), F32),
                        pltpu.VMEM((LANES, width), F32)],
        compiler_params=_cparams(("parallel", "arbitrary")),
        name="attn_prompt",
    )(bias_far, z3, kv_bf, kv_bf, mask, bias_tiles)


def _indexer_sample_body(pt_ref, qi_ref, wcol_ref, kinew_ref, *rest, n_pages):
    ki_refs = rest[:n_pages]
    out_ref = rest[n_pages]
    qi = qi_ref[0].astype(BF)
    w = wcol_ref[0] * (IDX_DIM ** -0.5 * IDX_HEADS ** -0.5)
    for i in range(n_pages):
        s = _dot_nt(qi, ki_refs[i][0].astype(BF))
        out_ref[i, 0] = jnp.sum(w * jnp.maximum(s, 0.0), axis=0, keepdims=True)
    kn = kinew_ref[0].astype(BF).astype(F32)
    sn = jnp.sum(qi.astype(F32) * kn, axis=-1, keepdims=True)
    new = jnp.sum(w * jnp.maximum(sn, 0.0), axis=0, keepdims=True)
    lane = lax.broadcasted_iota(I32, (1, LANES), 1)
    out_ref[n_pages, 0] = jnp.where(lane == 0, new, -jnp.inf)


def indexer_sample(page_table, qi3, wcol, ki_new, ki_pool):
    bsz, n_pages = page_table.shape
    ki_spec = lambda i: pl.BlockSpec((1, PAGE, IDX_DIM), lambda b, pt, i=i: (pt[b, i], 0, 0))
    grid_spec = pltpu.PrefetchScalarGridSpec(
        num_scalar_prefetch=1,
        grid=(bsz,),
        in_specs=[pl.BlockSpec((1, IDX_HEADS, IDX_DIM), lambda b, pt: (b, 0, 0)),
                  pl.BlockSpec((1, IDX_HEADS, 1), lambda b, pt: (b, 0, 0)),
                  pl.BlockSpec((1, 1, IDX_DIM), lambda b, pt: (b, 0, 0))]
                 + [ki_spec(i) for i in range(n_pages)],
        out_specs=pl.BlockSpec((n_pages + 1, 1, 1, LANES), lambda b, pt: (0, b, 0, 0)),
    )
    out = pl.pallas_call(
        functools.partial(_indexer_sample_body, n_pages=n_pages),
        grid_spec=grid_spec,
        out_shape=jax.ShapeDtypeStruct((n_pages + 1, bsz, 1, LANES), F32),
        compiler_params=_cparams(("arbitrary",)),
        name="indexer_sample",
    )(page_table, qi3, wcol, ki_new, *([ki_pool] * n_pages))
    return out.reshape(n_pages + 1, bsz, LANES)


def _select_body(score_ref, mask_ref, u_ref, *, n_tiles, k):
    def write_tile(t, m):
        mask_ref[t] = m

    _topk_mask(score_ref, u_ref, write_tile, n_tiles, n_tiles, 1, k)


def select_topk(scores, k):
    n_tiles, rows, _ = scores.shape
    return pl.pallas_call(
        functools.partial(_select_body, n_tiles=n_tiles, k=k),
        grid=(1,),
        in_specs=[pl.BlockSpec(scores.shape, lambda i: (0, 0, 0))],
        out_specs=pl.BlockSpec(scores.shape, lambda i: (0, 0, 0)),
        out_shape=jax.ShapeDtypeStruct(scores.shape, F32),
        scratch_shapes=[pltpu.VMEM(scores.shape, I32)],
        compiler_params=_cparams(("arbitrary",)),
        name="select_topk",
    )(scores)


SAMPLE_PAGE_GROUP = 8


def _attn_sample_body(pt_ref, idx_ref, off_ref, q_ref, kn_ref, vn_ref, btab_ref, *rest,
                      n_pages, past_len):
    g = SAMPLE_PAGE_GROUP
    k_refs = rest[:g]
    v_refs = rest[g:2 * g]
    o_ref, kbuf, vbuf, bbuf = rest[2 * g:]
    b = pl.program_id(0)
    s = pl.program_id(1)
    scale = HEAD_DIM ** -0.5

    for i in range(g):
        page = s * g + i

        def gather(t, carry, i=i, page=page):
            key = idx_ref[b, t]
            r = key - page * PAGE
            kbuf[t] = k_refs[i][0, r]
            vbuf[t] = v_refs[i][0, r]
            bbuf[t] = btab_ref[jnp.minimum(past_len - key, REL_MAX_DIST)]
            return carry

        lax.fori_loop(off_ref[b, page], off_ref[b, page + 1], gather, 0)

    @pl.when(s == pl.num_programs(1) - 1)
    def _():
        @pl.when(off_ref[b, n_pages + 1] > off_ref[b, n_pages])
        def _():
            kbuf[TOPK - 1] = kn_ref[0]
            vbuf[TOPK - 1] = vn_ref[0]
            bbuf[TOPK - 1] = btab_ref[0]

        q8 = q_ref[0]
        prod = (kbuf[...] * q8[None]).reshape(TOPK * HEADS, HEAD_DIM)
        ones = jnp.ones((HEAD_DIM, LANES), BF)
        hi = prod.astype(BF)
        lo = (prod - hi.astype(F32)).astype(BF)
        lg = (_dot(hi, ones) + _dot(lo, ones)).reshape(TOPK, HEADS, LANES) * scale + bbuf[...]
        m = jnp.max(lg, axis=0, keepdims=True)
        p = jnp.exp(lg - m)
        l = jnp.sum(p, axis=0)
        o_ref[0] = jnp.sum(p * vbuf[...], axis=0) / l


def attn_sample(page_table, sel_idx, sel_off, zs3, k_pool, v_pool, bias_by_dist, past_len):
    bsz, n_pages = page_table.shape
    g = SAMPLE_PAGE_GROUP
    assert n_pages % g == 0
    zspec = lambda cb: pl.BlockSpec((1, HEADS, HEAD_DIM), lambda b, s, pt, ix, of, cb=cb: (b, cb, 0))
    pool_spec = lambda i: pl.BlockSpec((1, PAGE, HEADS, HEAD_DIM),
                                       lambda b, s, pt, ix, of, i=i: (pt[b, s * g + i], 0, 0, 0))
    grid_spec = pltpu.PrefetchScalarGridSpec(
        num_scalar_prefetch=3,
        grid=(bsz, n_pages // g),
        in_specs=[zspec(4), zspec(5), zspec(6),
                  pl.BlockSpec(bias_by_dist.shape, lambda b, s, pt, ix, of: (0, 0, 0))]
                 + [pool_spec(i) for i in range(g)] * 2,
        out_specs=pl.BlockSpec((1, HEADS, HEAD_DIM), lambda b, s, pt, ix, of: (b, 0, 0)),
        scratch_shapes=[pltpu.VMEM((TOPK, HEADS, HEAD_DIM), F32)] * 3,
    )
    return pl.pallas_call(
        functools.partial(_attn_sample_body, n_pages=n_pages, past_len=past_len),
        grid_spec=grid_spec,
        out_shape=jax.ShapeDtypeStruct((bsz, HEADS, HEAD_DIM), F32),
        compiler_params=_cparams(("arbitrary", "arbitrary")),
        name="attn_sample",
    )(page_table, sel_idx, sel_off, zs3, zs3, zs3, bias_by_dist,
      *([k_pool] * g), *([v_pool] * g))


def _bucket_table(max_dist):
    exact = REL_BUCKETS // 2
    d = np.arange(max_dist + 1)
    df = np.maximum(d, 1).astype(np.float32)
    far = exact + (np.log(df / exact) / np.float32(math.log(REL_MAX_DIST / exact))
                   * (REL_BUCKETS - exact)).astype(np.int32)
    return np.where(d < exact, d, np.minimum(far, REL_BUCKETS - 1)).astype(np.int32)


def _bias_tables(rel_bias):
    tab = _bucket_table(2 * LANES)
    assert np.all(tab[REL_MAX_DIST:] == REL_BUCKETS - 1)
    i = np.arange(LANES)
    dist0 = np.maximum(i[:, None] - i[None, :], 0)
    dist1 = LANES + i[:, None] - i[None, :]
    far = np.full((LANES, LANES), REL_BUCKETS - 1)
    idx = np.stack([tab[dist0], tab[dist1], far])
    tiles = jnp.transpose(rel_bias[idx], (0, 3, 1, 2))
    by_dist = jnp.broadcast_to(rel_bias[tab[:REL_MAX_DIST + 1]][:, :, None],
                               (REL_MAX_DIST + 1, HEADS, LANES))
    return tiles.astype(F32), rel_bias[REL_BUCKETS - 1].astype(F32), by_dist.astype(F32)


def _conv_prompt_body(bg_ref, cg_ref, xt_ref, w_ref, v_ref, st_ref, carry_ref, *, tblock):
    t = pl.program_id(1)

    @pl.when(t == 0)
    def _():
        carry_ref[...] = jnp.zeros_like(carry_ref)

    u = cg_ref[0] * xt_ref[0]
    row = lax.broadcasted_iota(I32, u.shape, 0)
    c0 = carry_ref[0:1]
    c1 = carry_ref[1:2]
    u1 = jnp.where(row == 0, c1, pltpu.roll(u, 1, axis=0))
    u2 = jnp.where(row == 0, c0, jnp.where(row == 1, c1, pltpu.roll(u, 2, axis=0)))
    conv = w_ref[0:1] * u2 + w_ref[1:2] * u1 + w_ref[2:3] * u
    v_ref[0] = bg_ref[0] * conv
    last = u[tblock - 2:tblock]
    carry_ref[0:2] = last

    @pl.when(t == pl.num_programs(1) - 1)
    def _():
        st_ref[0] = last


def conv_prompt(zc3, w_conv, tblock=256):
    bsz, seq = zc3.shape[:2]
    c = zc3.shape[2] // 3
    zspec = lambda cb: pl.BlockSpec((1, tblock, c), lambda b, t, cb=cb: (b, t, cb))
    return pl.pallas_call(
        functools.partial(_conv_prompt_body, tblock=tblock),
        grid=(bsz, seq // tblock),
        in_specs=[zspec(0), zspec(1), zspec(2), pl.BlockSpec((3, c), lambda b, t: (0, 0))],
        out_specs=[pl.BlockSpec((1, tblock, c), lambda b, t: (b, t, 0)),
                   pl.BlockSpec((1, 2, c), lambda b, t: (b, 0, 0))],
        out_shape=[jax.ShapeDtypeStruct((bsz, seq, c), F32),
                   jax.ShapeDtypeStruct((bsz, 2, c), F32)],
        scratch_shapes=[pltpu.VMEM((8, c), F32)],
        compiler_params=_cparams(("parallel", "arbitrary")),
        name="conv_prompt",
    )(zc3, zc3, zc3, w_conv)


def _conv_sample_body(bg_ref, cg_ref, xt_ref, w_ref, s0_ref, s1_ref, v_ref, n0_ref, n1_ref):
    u = cg_ref[...] * xt_ref[...]
    conv = w_ref[0:1] * s0_ref[...] + w_ref[1:2] * s1_ref[...] + w_ref[2:3] * u
    v_ref[...] = bg_ref[...] * conv
    n0_ref[...] = s1_ref[...]
    n1_ref[...] = u


def conv_sample(zc, w_conv, s0, s1):
    bsz = zc.shape[0]
    c = zc.shape[1] // 3
    zspec = lambda cb: pl.BlockSpec((bsz, c), lambda i, cb=cb: (0, cb))
    full = pl.BlockSpec((bsz, c), lambda i: (0, 0))
    return pl.pallas_call(
        _conv_sample_body,
        grid=(1,),
        in_specs=[zspec(0), zspec(1), zspec(2), pl.BlockSpec((3, c), lambda i: (0, 0)), full, full],
        out_specs=[full, full, full],
        out_shape=[jax.ShapeDtypeStruct((bsz, c), F32)] * 3,
        compiler_params=_cparams(("arbitrary",)),
        name="conv_sample",
    )(zc, zc, zc, w_conv, s0, s1)


def _router_body(x_ref, g_ref, wr_ref, br_ref, h_ref, route_ref):
    x = x_ref[...]
    ms = jnp.mean(x * x, axis=-1, keepdims=True)
    h = x * lax.rsqrt(ms + RMS_EPS) * g_ref[...]
    h_ref[...] = h.astype(BF)
    logits = jnp.dot(h, wr_ref[...], preferred_element_type=F32,
                     precision=lax.Precision.HIGHEST) + br_ref[...]
    lane = lax.broadcasted_iota(I32, logits.shape, 1)
    big = np.int32(1 << 20)
    lg = jnp.where(lane < N_GROUPS, logits, -jnp.inf)
    g_max = jnp.max(lg, axis=-1, keepdims=True)
    g_idx = jnp.min(jnp.where(lg == g_max, lane, big), axis=-1, keepdims=True)
    g_w = 1.0 / jnp.sum(jnp.exp(lg - g_max), axis=-1, keepdims=True)
    first = N_GROUPS + EXP_PER_GROUP * g_idx
    le = jnp.where((lane >= first) & (lane < first + EXP_PER_GROUP), logits, -jnp.inf)
    l1 = jnp.max(le, axis=-1, keepdims=True)
    i1 = jnp.min(jnp.where(le == l1, lane, big), axis=-1, keepdims=True)
    le2 = jnp.where(lane == i1, -jnp.inf, le)
    l2 = jnp.max(le2, axis=-1, keepdims=True)
    i2 = jnp.min(jnp.where(le2 == l2, lane, big), axis=-1, keepdims=True)
    r = jnp.exp(l2 - l1)
    w1 = g_w / (1.0 + r)
    w2 = g_w * r / (1.0 + r)
    e1 = (i1 - N_GROUPS).astype(F32)
    e2 = (i2 - N_GROUPS).astype(F32)
    route_ref[...] = jnp.where(lane == 0, e1, jnp.where(lane == 1, e2,
                               jnp.where(lane == 2, w1, jnp.where(lane == 3, w2, 0.0))))


def moe_router(x, g, wr, br, tm):
    m, k = x.shape
    return pl.pallas_call(
        _router_body,
        grid=(m // tm,),
        in_specs=[pl.BlockSpec((tm, k), lambda i: (i, 0)),
                  pl.BlockSpec((1, k), lambda i: (0, 0)),
                  pl.BlockSpec((k, LANES), lambda i: (0, 0)),
                  pl.BlockSpec((1, LANES), lambda i: (0, 0))],
        out_specs=[pl.BlockSpec((tm, k), lambda i: (i, 0)),
                   pl.BlockSpec((tm, LANES), lambda i: (i, 0))],
        out_shape=[jax.ShapeDtypeStruct((m, k), BF),
                   jax.ShapeDtypeStruct((m, LANES), F32)],
        compiler_params=_cparams(("parallel",)),
        name="moe_router",
    )(x, g.reshape(1, k), wr, br)


def _experts_body(te_ref, nu_ref, h_ref, gate_ref, wg_ref, wu_ref, wd_ref, o_ref,
                  wg_bf, wu_bf, wd_bf):
    i = pl.program_id(0)

    @pl.when((i == 0) | (te_ref[i] != te_ref[jnp.maximum(i - 1, 0)]))
    def _():
        wg_bf[...] = wg_ref[0].astype(BF)
        wu_bf[...] = wu_ref[0].astype(BF)
        wd_bf[...] = wd_ref[0].astype(BF)

    @pl.when(i < nu_ref[0])
    def _():
        h = h_ref[...]
        a = _dot(h, wg_bf[...])
        b = _dot(h, wu_bf[...])
        hid = a * _sigmoid(a) * b * gate_ref[...]
        o_ref[...] = _dot(hid.astype(BF), wd_bf[...])

    @pl.when(i >= nu_ref[0])
    def _():
        o_ref[...] = jnp.zeros_like(o_ref)


def moe_experts(tile_expert, n_used, hs, gate_col, wg, wu, wd):
    p, k = hs.shape
    f = wg.shape[2]
    n_tiles = p // MOE_TILE
    grid_spec = pltpu.PrefetchScalarGridSpec(
        num_scalar_prefetch=2,
        grid=(n_tiles,),
        in_specs=[pl.BlockSpec((MOE_TILE, k), lambda i, te, nu: (i, 0)),
                  pl.BlockSpec((MOE_TILE, 1), lambda i, te, nu: (i, 0)),
                  pl.BlockSpec((1, k, f), lambda i, te, nu: (te[i], 0, 0)),
                  pl.BlockSpec((1, k, f), lambda i, te, nu: (te[i], 0, 0)),
                  pl.BlockSpec((1, f, k), lambda i, te, nu: (te[i], 0, 0))],
        out_specs=pl.BlockSpec((MOE_TILE, k), lambda i, te, nu: (i, 0)),
        scratch_shapes=[pltpu.VMEM((k, f), BF), pltpu.VMEM((k, f), BF), pltpu.VMEM((f, k), BF)],
    )
    return pl.pallas_call(
        _experts_body,
        grid_spec=grid_spec,
        out_shape=jax.ShapeDtypeStruct((p, k), F32),
        compiler_params=_cparams(("arbitrary",)),
        name="moe_experts",
    )(tile_expert, n_used, hs, gate_col, wg, wu, wd)


def _rank_within_expert(onehot):
    n, e = onehot.shape
    blk = LANES
    assert n % blk == 0
    oh = onehot.astype(F32).reshape(n // blk, blk, e)
    strict = jnp.asarray(np.tril(np.ones((blk, blk), np.float32), -1))
    within = jnp.einsum("ij,bjk->bik", strict, oh, precision=lax.Precision.HIGHEST)
    totals = jnp.sum(oh, axis=1)
    before = jnp.cumsum(totals, axis=0) - totals
    rank = (within + before[:, None, :]).reshape(n, e)
    return jnp.sum(rank * onehot.astype(F32), axis=1).astype(I32), jnp.sum(totals, axis=0).astype(I32)


def hier_moe(xs, tms, g, wrg, brg, wre, bre, wg, wu, wd, layer):
    d = xs[0].shape[1]
    m = sum(x.shape[0] for x in xs)
    wr = jnp.zeros((d, LANES), F32)
    wr = wr.at[:, :N_GROUPS].set(wrg).at[:, N_GROUPS:N_GROUPS + N_EXPERTS].set(wre.reshape(d, N_EXPERTS))
    br = jnp.zeros((1, LANES), F32)
    br = br.at[0, :N_GROUPS].set(brg).at[0, N_GROUPS:N_GROUPS + N_EXPERTS].set(bre.reshape(N_EXPERTS))
    routed = [moe_router(x, g, wr, br, tm) for x, tm in zip(xs, tms)]
    h_bf = jnp.concatenate([r[0] for r in routed], axis=0)
    route = jnp.concatenate([r[1] for r in routed], axis=0)

    eid = route[:, 0:2].astype(I32).reshape(-1)
    gate = route[:, 2:4].reshape(-1)
    onehot = eid[:, None] == jnp.arange(N_EXPERTS, dtype=I32)[None, :]
    rank, counts = _rank_within_expert(onehot)
    padded = ((counts + MOE_TILE - 1) // MOE_TILE) * MOE_TILE
    ends = jnp.cumsum(padded)
    pos = (ends - padded)[eid] + rank
    n_rows = 2 * m + N_EXPERTS * MOE_TILE
    n_rows = -(-n_rows // MOE_TILE) * MOE_TILE
    pair = jnp.zeros((n_rows,), I32).at[pos].set(jnp.arange(2 * m, dtype=I32))
    tile_start = jnp.arange(n_rows // MOE_TILE, dtype=I32) * MOE_TILE
    tile_expert = jnp.minimum(jnp.sum(tile_start[:, None] >= ends[None, :], axis=1),
                              N_EXPERTS - 1).astype(I32)
    n_used = (ends[-1] // MOE_TILE).astype(I32).reshape(1)

    hs = jnp.take(h_bf, pair // 2, axis=0)
    out = moe_experts(tile_expert + layer * N_EXPERTS, n_used, hs, jnp.take(gate, pair)[:, None],
                      wg, wu, wd)
    pos2 = pos.reshape(m, 2)
    res, r0 = [], 0
    for x in xs:
        pp = pos2[r0:r0 + x.shape[0]]
        res.append(x + jnp.take(out, pp[:, 0], axis=0) + jnp.take(out, pp[:, 1], axis=0))
        r0 += x.shape[0]
    return res


def kernel(x_prompt, x_sample, cache_k, cache_v, cache_ki, state_hgrn, state_conv, page_table,
           norm_mix_g, norm_ffn_g, final_g, w_in_even, w_out_even, hgrn_lb_logits, hgrn_norm_g,
           rel_bias, w_in_conv, w_conv, w_out_conv, w_router_g, b_router_g, w_router_e,
           b_router_e, w_gate, w_up, w_down):
    bsz, seq, d = x_prompt.shape
    dec = x_sample.shape[0]
    n_p = bsz * seq
    tm_p, tm_s = 512, 128
    tms = (tm_p, tm_s)
    assert n_p % tm_p == 0 and dec % tm_s == 0 and x_sample.shape[1] == 1
    width = HEADS * HEAD_DIM
    even_in = w_in_even.shape[2]
    even_pad = -(-even_in // LANES) * LANES
    tail0 = even_pad - LANES
    n_pool = cache_k.shape[1]

    xp = x_prompt.reshape(n_p, d)
    xs = x_sample.reshape(dec, d)

    lbs = jnp.cumsum(jax.nn.softmax(hgrn_lb_logits.astype(F32), axis=0), axis=0)[:-1]
    bias_tiles, bias_far, bias_by_dist = _bias_tables(rel_bias)
    expert_w = [w.reshape(-1, *w.shape[2:]) for w in (w_gate, w_up, w_down)]

    w_in = jnp.pad(w_in_even[0], ((0, 0), (0, even_pad - even_in))).astype(BF)
    zp = norm_matmul(xp, norm_mix_g[0], w_in, 2 * tm_p, 640)
    zs = norm_matmul(xs, norm_mix_g[0], w_in, tm_s, 640)
    zp3 = zp.reshape(bsz, seq, even_pad)
    zs3 = zs.reshape(dec, even_pad // LANES, LANES)

    oa_p, hgrn_p = hgrn_prompt(zp3, lbs[0], hgrn_norm_g[0])
    oa_s, hgrn_s = hgrn_sample(zs3, lbs[0], hgrn_norm_g[0], state_hgrn[0])

    mask_p = indexer_prompt(zp3, TOPK)
    kv_bf = zp3[:, :, 5 * width:7 * width].astype(BF)
    ob_p = attn_prompt(zp3, kv_bf, mask_p, bias_tiles, bias_far)

    qi3 = zs[:, 7 * width:8 * width].reshape(dec, IDX_HEADS, IDX_DIM)
    ki_new = zs[:, tail0:tail0 + IDX_DIM].reshape(dec, 1, IDX_DIM)
    wcol = zs[:, tail0 + IDX_DIM:tail0 + IDX_DIM + IDX_HEADS].reshape(dec, IDX_HEADS, 1)
    pages = page_table + 0 * n_pool
    scores_s = indexer_sample(pages, qi3, wcol, ki_new,
                              cache_ki.reshape(-1, PAGE, IDX_DIM))
    mask_s = select_topk(scores_s, TOPK)
    sel = jnp.transpose(mask_s, (1, 0, 2))
    n_pages = page_table.shape[1]
    sel_off = jnp.concatenate([jnp.zeros((dec, 1), F32),
                               jnp.cumsum(jnp.sum(sel, axis=2), axis=1)], axis=1)
    slot = jnp.arange(TOPK, dtype=F32)
    page_of = jnp.sum(sel_off[:, None, 1:] <= slot[None, :, None], axis=2)
    page_1h = (page_of[:, :, None] == jnp.arange(n_pages + 1)[None, None, :]).astype(F32)
    local = slot[None, :] - jnp.einsum("brp,bp->br", page_1h, sel_off[:, :-1],
                                       precision=lax.Precision.HIGHEST)
    within = jnp.einsum("brp,bpl->brl", page_1h, jnp.cumsum(sel, axis=2),
                        precision=lax.Precision.HIGHEST)
    lane_of = jnp.sum(within <= local[:, :, None], axis=2)
    sel_idx = (page_of * PAGE + lane_of).astype(I32)
    sel_off = sel_off.astype(I32)
    ob_s = attn_sample(pages, sel_idx, sel_off, zs3,
                       cache_k.reshape(-1, PAGE, HEADS, HEAD_DIM),
                       cache_v.reshape(-1, PAGE, HEADS, HEAD_DIM), bias_by_dist, n_pages * PAGE)

    w_out = w_out_even[0].astype(BF)
    w_out_ab = [w_out[:width], w_out[width:]]
    xp = matmul_residual([oa_p.reshape(n_p, width), ob_p.reshape(n_p, width)], w_out_ab, xp, 2 * tm_p, 512)
    xs = matmul_residual([oa_s.reshape(dec, width), ob_s.reshape(dec, width)], w_out_ab, xs, tm_s, 512)

    xp, xs = hier_moe([xp, xs], tms, norm_ffn_g[0], w_router_g[0], b_router_g[0], w_router_e[0],
                      b_router_e[0], *expert_w, 0)

    w_in_c = w_in_conv[0].astype(BF)
    zcp = norm_matmul(xp, norm_mix_g[1], w_in_c, 2 * tm_p, 1024)
    zcs = norm_matmul(xs, norm_mix_g[1], w_in_c, tm_s, 512)
    cw = zcp.shape[1] // 3
    v_p, conv_p = conv_prompt(zcp.reshape(bsz, seq, 3 * cw), w_conv[0])
    v_s, cs0, cs1 = conv_sample(zcs, w_conv[0], state_conv[0, :, 0], state_conv[0, :, 1])
    w_out_c = [w_out_conv[0].astype(BF)]
    xp = matmul_residual([v_p.reshape(n_p, cw)], w_out_c, xp, 2 * tm_p, 512)
    xs = matmul_residual([v_s], w_out_c, xs, tm_s, 512)

    xp, xs = hier_moe([xp, xs], tms, norm_ffn_g[1], w_router_g[1], b_router_g[1], w_router_e[1],
                      b_router_e[1], *expert_w, 1)

    yp = rmsnorm(xp, final_g, tm_p)
    ys = rmsnorm(xs, final_g, tm_s)

    kcol, vcol = 5 * width, 6 * width
    heads = lambda a, n: a.reshape(1, *n, HEADS, HEAD_DIM)
    return (yp.reshape(bsz, seq, d),
            ys.reshape(dec, 1, d),
            heads(zp[:, kcol:kcol + width], (bsz, seq)),
            heads(zp[:, vcol:vcol + width], (bsz, seq)),
            zp[:, tail0:tail0 + IDX_DIM].reshape(1, bsz, seq, IDX_DIM),
            hgrn_p[None],
            conv_p[None],
            heads(zs[:, kcol:kcol + width], (dec, 1)),
            heads(zs[:, vcol:vcol + width], (dec, 1)),
            zs[:, tail0:tail0 + IDX_DIM].reshape(1, dec, 1, IDX_DIM),
            hgrn_s[None],
            jnp.stack([cs0, cs1], axis=1)[None])
```

```python
import functools
import math

import numpy as np
import jax
import jax.numpy as jnp
from jax import lax
from jax.experimental import pallas as pl
from jax.experimental.pallas import tpu as pltpu
from jax.experimental.pallas import tpu_sc as plsc

F32 = jnp.float32
BF = jnp.bfloat16
I32 = jnp.int32

RMS_EPS = 1e-6
LANES = 128
NEG_BIG = -1e30
VMEM_LIMIT = 56 * 1024 * 1024

D_MODEL = 2048
HEADS = 8
HEAD_DIM = 128
IDX_HEADS = 16
IDX_DIM = 64
TOPK = 256
REL_BUCKETS = 32
REL_MAX_DIST = 128
N_EXPERTS = 16
EXP_PER_GROUP = 4
N_GROUPS = 4
D_EXPERT = 512
PAGE = 128

HGRN_CHUNK = 64
MOE_TILE = 256


def _cparams(sem):
    return pltpu.CompilerParams(dimension_semantics=sem, vmem_limit_bytes=VMEM_LIMIT)


def _dot(a, b):
    return jnp.dot(a, b, preferred_element_type=F32)


def _dot_nt(a, b):
    return lax.dot_general(a, b, (((1,), (1,)), ((), ())), preferred_element_type=F32)


def _dot_tn(a, b):
    return lax.dot_general(a, b, (((0,), (0,)), ((), ())), preferred_element_type=F32)


def _sigmoid(x):
    return 1.0 / (1.0 + jnp.exp(-x))


def _norm_mm_body(x_ref, g_ref, w_ref, o_ref, h_ref):
    @pl.when(pl.program_id(1) == 0)
    def _():
        x = x_ref[...]
        ms = jnp.mean(x * x, axis=-1, keepdims=True)
        h_ref[...] = (x * lax.rsqrt(ms + RMS_EPS) * g_ref[...]).astype(BF)

    o_ref[...] = _dot(h_ref[...], w_ref[...])


def norm_matmul(x, g, w_bf, tm, tn):
    m, k = x.shape
    n = w_bf.shape[1]
    return pl.pallas_call(
        _norm_mm_body,
        grid=(m // tm, n // tn),
        in_specs=[pl.BlockSpec((tm, k), lambda i, j: (i, 0)),
                  pl.BlockSpec((1, k), lambda i, j: (0, 0)),
                  pl.BlockSpec((k, tn), lambda i, j: (0, j))],
        out_specs=pl.BlockSpec((tm, tn), lambda i, j: (i, j)),
        out_shape=jax.ShapeDtypeStruct((m, n), F32),
        scratch_shapes=[pltpu.VMEM((tm, k), BF)],
        compiler_params=_cparams(("parallel", "arbitrary")),
        name="norm_matmul",
    )(x, g.reshape(1, k), w_bf)


def _mm_res_body(*refs, n_lhs):
    a_refs = refs[:n_lhs]
    w_refs = refs[n_lhs:2 * n_lhs]
    r_ref = refs[2 * n_lhs]
    o_ref = refs[2 * n_lhs + 1]
    s_refs = refs[2 * n_lhs + 2:]

    @pl.when(pl.program_id(1) == 0)
    def _():
        for a_ref, s_ref in zip(a_refs, s_refs):
            s_ref[...] = a_ref[...].astype(BF)

    acc = r_ref[...]
    for s_ref, w_ref in zip(s_refs, w_refs):
        acc = acc + _dot(s_ref[...], w_ref[...])
    o_ref[...] = acc


def matmul_residual(lhs, ws_bf, res, tm, tn):
    m, n = res.shape
    n_lhs = len(lhs)
    in_specs = ([pl.BlockSpec((tm, a.shape[1]), lambda i, j: (i, 0)) for a in lhs]
                + [pl.BlockSpec((w.shape[0], tn), lambda i, j: (0, j)) for w in ws_bf]
                + [pl.BlockSpec((tm, tn), lambda i, j: (i, j))])
    return pl.pallas_call(
        functools.partial(_mm_res_body, n_lhs=n_lhs),
        grid=(m // tm, n // tn),
        in_specs=in_specs,
        out_specs=pl.BlockSpec((tm, tn), lambda i, j: (i, j)),
        out_shape=jax.ShapeDtypeStruct((m, n), F32),
        scratch_shapes=[pltpu.VMEM((tm, a.shape[1]), BF) for a in lhs],
        compiler_params=_cparams(("parallel", "arbitrary")),
        name="matmul_residual",
    )(*lhs, *ws_bf, res)


def _rmsnorm_body(x_ref, g_ref, o_ref):
    x = x_ref[...]
    ms = jnp.mean(x * x, axis=-1, keepdims=True)
    o_ref[...] = x * lax.rsqrt(ms + RMS_EPS) * g_ref[...]


def rmsnorm(x, g, tm):
    m, k = x.shape
    return pl.pallas_call(
        _rmsnorm_body,
        grid=(m // tm,),
        in_specs=[pl.BlockSpec((tm, k), lambda i: (i, 0)),
                  pl.BlockSpec((1, k), lambda i: (0, 0))],
        out_specs=pl.BlockSpec((tm, k), lambda i: (i, 0)),
        out_shape=jax.ShapeDtypeStruct((m, k), F32),
        compiler_params=_cparams(("parallel",)),
        name="rmsnorm",
    )(x, g.reshape(1, k))


def _hgrn_static(c):
    levels = []
    m = 1
    while m < c:
        levels.append(m)
        m *= 2
    t = np.arange(c)
    rows = [t[None, :] <= t[:, None]]
    masks = [np.eye(c, dtype=bool)]
    for m in levels:
        blk = t // (2 * m)
        pos = t % (2 * m)
        bnd = blk * 2 * m + m - 1
        right = pos >= m
        left = pos < m
        e_rows = (t[None, :] > bnd[:, None]) & (t[None, :] <= t[:, None]) & right[:, None]
        f_rows = (t[None, :] > t[:, None]) & (t[None, :] <= bnd[:, None]) & left[:, None]
        rows.append(e_rows | f_rows)
        masks.append((blk[:, None] == blk[None, :]) & right[:, None] & left[None, :])
    m_all = np.stack(rows).astype(np.float32)
    masks = np.stack(masks).astype(np.float32)
    return m_all, masks, len(levels)


def _hgrn_gates(qa, fa, lb):
    f = lb + (1.0 - lb) * _sigmoid(fa)
    q = qa * _sigmoid(qa)
    return q, f


def _hgrn_prompt_body(qa_ref, fa_ref, ia_ref, ga_ref, lb_ref, gn_ref, mall_ref, masks_ref,
                      oa_ref, st_out_ref, st_ref, *, chunk, tblock, n_levels):
    c = chunk
    t_idx = pl.program_id(1)

    @pl.when(t_idx == 0)
    def _():
        st_ref[...] = jnp.zeros_like(st_ref)

    gn = gn_ref[...]

    def chunk_step(ci, carry):
        r0 = pl.multiple_of(ci * c, c)
        for h in range(HEADS):
            cols = slice(h * HEAD_DIM, (h + 1) * HEAD_DIM)
            qa = qa_ref[0, pl.ds(r0, c), cols]
            fa = fa_ref[0, pl.ds(r0, c), cols]
            ia = ia_ref[0, pl.ds(r0, c), cols]
            ga = ga_ref[0, pl.ds(r0, c), cols]
            q, f = _hgrn_gates(qa, fa, lb_ref[:, cols])
            k = 1.0 - f
            g = jnp.log(f)
            g_hi = g.astype(BF)
            g_lo = (g - g_hi.astype(F32)).astype(BF)
            b = _dot(mall_ref[0], g_hi) + _dot(mall_ref[0], g_lo)
            b_last = b[c - 1:c]
            st = st_ref[h]
            v = ia.astype(BF)
            o = _dot_nt((q * jnp.exp(b)).astype(BF), st.astype(BF))
            a = masks_ref[0] * _dot_nt(q.astype(BF), k.astype(BF))
            for li in range(n_levels):
                ml = mall_ref[1 + li]
                w = jnp.exp(_dot(ml, g_hi) + _dot(ml, g_lo))
                a = a + masks_ref[1 + li] * _dot_nt((q * w).astype(BF), (k * w).astype(BF))
            o = o + _dot(a.astype(BF), v)
            k_st = (k * jnp.exp(b_last - b)).astype(BF)
            st_ref[h] = st * jnp.exp(b_last) + _dot_tn(v, k_st)
            ms = jnp.mean(o * o, axis=-1, keepdims=True)
            y = o * lax.rsqrt(ms + RMS_EPS) * gn * (ga * _sigmoid(ga))
            oa_ref[0, pl.ds(r0, c), cols] = y
        return carry

    lax.fori_loop(0, tblock // c, chunk_step, 0)

    @pl.when(t_idx == pl.num_programs(1) - 1)
    def _():
        st_out_ref[0] = st_ref[...]


def hgrn_prompt(z3, lb, gn, tblock=256, chunk=HGRN_CHUNK):
    bsz, seq = z3.shape[:2]
    width = HEADS * HEAD_DIM
    m_all, masks, n_levels = _hgrn_static(chunk)
    zspec = lambda cb: pl.BlockSpec((1, tblock, width), lambda b, t, cb=cb: (b, t, cb))
    oa, st = pl.pallas_call(
        functools.partial(_hgrn_prompt_body, chunk=chunk, tblock=tblock, n_levels=n_levels),
        grid=(bsz, seq // tblock),
        in_specs=[zspec(0), zspec(1), zspec(2), zspec(3),
                  pl.BlockSpec((1, width), lambda b, t: (0, 0)),
                  pl.BlockSpec((1, HEAD_DIM), lambda b, t: (0, 0)),
                  pl.BlockSpec(m_all.shape, lambda b, t: (0, 0, 0)),
                  pl.BlockSpec(masks.shape, lambda b, t: (0, 0, 0))],
        out_specs=[pl.BlockSpec((1, tblock, width), lambda b, t: (b, t, 0)),
                   pl.BlockSpec((1, HEADS, HEAD_DIM, HEAD_DIM), lambda b, t: (b, 0, 0, 0))],
        out_shape=[jax.ShapeDtypeStruct((bsz, seq, width), F32),
                   jax.ShapeDtypeStruct((bsz, HEADS, HEAD_DIM, HEAD_DIM), F32)],
        scratch_shapes=[pltpu.VMEM((HEADS, HEAD_DIM, HEAD_DIM), F32)],
        compiler_params=_cparams(("parallel", "arbitrary")),
        name="hgrn_prompt",
    )(z3, z3, z3, z3, lb.reshape(1, width), gn.reshape(1, HEAD_DIM),
      jnp.asarray(m_all, BF), jnp.asarray(masks, F32))
    return oa, jnp.swapaxes(st, -1, -2)


def _col(row, eye):
    return jnp.sum(eye * row, axis=1, keepdims=True)


def _hgrn_sample_body(qa_ref, fa_ref, ia_ref, ga_ref, lb_ref, gn_ref, s_ref, oa_ref, so_ref):
    eye = (lax.broadcasted_iota(I32, (HEAD_DIM, HEAD_DIM), 0)
           == lax.broadcasted_iota(I32, (HEAD_DIM, HEAD_DIM), 1)).astype(F32)
    q8, f8 = _hgrn_gates(qa_ref[0], fa_ref[0], lb_ref[...])
    ga = ga_ref[0]
    gate = ga * _sigmoid(ga)
    outs = []
    for h in range(HEADS):
        f_col = _col(f8[h:h + 1], eye)
        q_col = _col(q8[h:h + 1], eye)
        s_new = f_col * s_ref[0, h] + (1.0 - f_col) * ia_ref[0, h:h + 1]
        so_ref[0, h] = s_new
        outs.append(jnp.sum(q_col * s_new, axis=0, keepdims=True))
    o = jnp.concatenate(outs, axis=0)
    ms = jnp.mean(o * o, axis=-1, keepdims=True)
    oa_ref[0] = o * lax.rsqrt(ms + RMS_EPS) * gn_ref[...] * gate


def hgrn_sample(zs3, lb, gn, s0):
    bsz = zs3.shape[0]
    zspec = lambda cb: pl.BlockSpec((1, HEADS, HEAD_DIM), lambda b, cb=cb: (b, cb, 0))
    sspec = pl.BlockSpec((1, HEADS, HEAD_DIM, HEAD_DIM), lambda b: (b, 0, 0, 0))
    return pl.pallas_call(
        _hgrn_sample_body,
        grid=(bsz,),
        in_specs=[zspec(0), zspec(1), zspec(2), zspec(3),
                  pl.BlockSpec((HEADS, HEAD_DIM), lambda b: (0, 0)),
                  pl.BlockSpec((1, HEAD_DIM), lambda b: (0, 0)),
                  sspec],
        out_specs=[pl.BlockSpec((1, HEADS, HEAD_DIM), lambda b: (b, 0, 0)), sspec],
        out_shape=[jax.ShapeDtypeStruct((bsz, HEADS, HEAD_DIM), F32),
                   jax.ShapeDtypeStruct(s0.shape, F32)],
        compiler_params=_cparams(("parallel",)),
        name="hgrn_sample",
    )(zs3, zs3, zs3, zs3, lb.reshape(HEADS, HEAD_DIM), gn.reshape(1, HEAD_DIM), s0)


_KEY_NEG_INF = np.int32(np.uint32(0x807FFFFF).astype(np.int64) - (1 << 32))
_INT_MIN = np.int32(-(1 << 31))


def _count(u_ref, n_groups, group, thr, cmp):
    rows = u_ref.shape[1]
    step = min(rows, LANES)
    parts = []
    for r0 in range(0, rows, step):
        t = jnp.broadcast_to(thr[r0:r0 + step], (step, LANES))

        def body(gi, acc, r0=r0, t=t):
            for i in range(group):
                acc = acc + cmp(u_ref[gi * group + i, r0:r0 + step], t).astype(F32)
            return acc

        acc = lax.fori_loop(0, n_groups, body, jnp.zeros((step, LANES), F32))
        parts.append(jnp.sum(acc, axis=-1, keepdims=True))
    return parts[0] if len(parts) == 1 else jnp.concatenate(parts, axis=0)


def _topk_mask(score_ref, u_ref, write_tile, n_tiles, n_groups, group, k):
    rows = score_ref.shape[1]
    n_live = n_groups * group

    def to_key(t, carry):
        bits = pltpu.bitcast(score_ref[t], I32)
        u_ref[t] = jnp.where(bits < 0, bits ^ np.int32(0x7FFFFFFF), bits)
        return carry

    lax.fori_loop(0, n_live, to_key, 0)

    kf = float(k)
    ge = lambda u, t: u >= t
    cnt = _count(u_ref, n_groups, group, jnp.zeros((rows, 1), I32), ge)
    lo = jnp.where(cnt >= kf, np.int32(0), _INT_MIN)

    def bit_step(i, lo):
        cand = lo | (np.int32(1) << (30 - i))
        cnt = _count(u_ref, n_groups, group, cand, ge)
        return jnp.where(cnt >= kf, cand, lo)

    lo = lax.fori_loop(0, 31, bit_step, lo)
    c_gt = _count(u_ref, n_groups, group, lo, lambda u, t: u > t)
    c_eq = _count(u_ref, n_groups, group, lo, lambda u, t: u == t)
    need = kf - c_gt
    real = lo > _KEY_NEG_INF
    excess = jnp.where(real & (c_eq > need), 1.0, 0.0)
    any_excess = jnp.max(excess) > 0.0

    @pl.when(jnp.logical_not(any_excess))
    def _():
        def emit(t, carry):
            u = u_ref[t]
            write_tile(t, jnp.where((u >= lo) & (u > _KEY_NEG_INF), 1.0, 0.0))
            return carry

        lax.fori_loop(0, n_live, emit, 0)

    @pl.when(any_excess)
    def _():
        upper = (lax.broadcasted_iota(I32, (LANES, LANES), 0)
                 <= lax.broadcasted_iota(I32, (LANES, LANES), 1)).astype(BF)

        def emit(t, seen):
            u = u_ref[t]
            eq = jnp.where(u == lo, 1.0, 0.0)
            prefix = seen + _dot(eq.astype(BF), upper)
            take = (u > lo) | ((u == lo) & (prefix <= need))
            write_tile(t, jnp.where(take & (u > _KEY_NEG_INF), 1.0, 0.0))
            return seen + jnp.sum(eq, axis=-1, keepdims=True)

        lax.fori_loop(0, n_live, emit, jnp.zeros((rows, 1), F32))

    def clear(t, carry):
        write_tile(t, jnp.zeros((rows, LANES), F32))
        return carry

    lax.fori_loop(n_live, n_tiles, clear, 0)


IDX_KCHUNK = 512
IDX_QROWS = 256


def _indexer_prompt_body(qi_ref, tailq_ref, tail_ref, mask_ref, score_ref, u_ref, *, seq):
    j = pl.program_id(1)
    rows = IDX_QROWS
    qblocks = rows // LANES
    n_tiles = seq // LANES
    tiles_per_chunk = IDX_KCHUNK // LANES
    qi = qi_ref[0].astype(BF)
    w = tailq_ref[0][:, IDX_DIM:IDX_DIM + IDX_HEADS] * (IDX_DIM ** -0.5 * IDX_HEADS ** -0.5)

    n_chunks = (j * rows + rows - 1) // IDX_KCHUNK + 1
    q_pos = j * rows + lax.broadcasted_iota(I32, (rows, IDX_KCHUNK), 0)

    def chunk_step(ci, carry):
        k0 = pl.multiple_of(ci * IDX_KCHUNK, IDX_KCHUNK)
        kic = tail_ref[0, pl.ds(k0, IDX_KCHUNK), :][:, 0:IDX_DIM].astype(BF)
        acc = jnp.zeros((rows, IDX_KCHUNK), F32)
        for h in range(IDX_HEADS):
            s = _dot_nt(qi[:, h * IDX_DIM:(h + 1) * IDX_DIM], kic)
            acc = acc + w[:, h:h + 1] * jnp.maximum(s, 0.0)
        k_pos = k0 + lax.broadcasted_iota(I32, (rows, IDX_KCHUNK), 1)
        acc = jnp.where(k_pos <= q_pos, acc, -jnp.inf)
        for i in range(tiles_per_chunk):
            score_ref[ci * tiles_per_chunk + i] = acc[:, i * LANES:(i + 1) * LANES]
        return carry

    lax.fori_loop(0, n_chunks, chunk_step, 0)

    def write_tile(t, m):
        mb = m.astype(BF)
        for qb in range(qblocks):
            mask_ref[0, qb, t] = mb[qb * LANES:(qb + 1) * LANES]

    _topk_mask(score_ref, u_ref, write_tile, n_tiles, n_chunks, tiles_per_chunk, TOPK)


def indexer_prompt(z3, k):
    bsz, seq = z3.shape[:2]
    nqb = seq // LANES
    rows = IDX_QROWS
    tail_block = z3.shape[2] // LANES - 1
    assert k == TOPK and seq % rows == 0 and seq % IDX_KCHUNK == 0
    return pl.pallas_call(
        functools.partial(_indexer_prompt_body, seq=seq),
        grid=(bsz, seq // rows),
        in_specs=[pl.BlockSpec((1, rows, IDX_HEADS * IDX_DIM), lambda b, j: (b, j, 7)),
                  pl.BlockSpec((1, rows, LANES), lambda b, j: (b, j, tail_block)),
                  pl.BlockSpec((1, seq, LANES), lambda b, j: (b, 0, tail_block))],
        out_specs=pl.BlockSpec((1, rows // LANES, nqb, LANES, LANES), lambda b, j: (b, j, 0, 0, 0)),
        out_shape=jax.ShapeDtypeStruct((bsz, nqb, nqb, LANES, LANES), BF),
        scratch_shapes=[pltpu.VMEM((nqb, rows, LANES), F32),
                        pltpu.VMEM((nqb, rows, LANES), I32)],
        compiler_params=_cparams(("parallel", "arbitrary")),
        name="indexer_prompt",
    )(z3, z3, z3)


ATTN_KSTEP = 512
ATTN_TILES = ATTN_KSTEP // LANES


def _attn_prompt_body(bfar_ref, q_ref, k_ref, v_ref, mask_ref, bias_ref, o_ref, m_ref, l_ref, acc_ref):
    j = pl.program_id(1)
    scale = HEAD_DIM ** -0.5
    q = q_ref[0].astype(BF)
    m_ref[...] = jnp.full(m_ref.shape, NEG_BIG, F32)
    l_ref[...] = jnp.zeros(l_ref.shape, F32)
    acc_ref[...] = jnp.zeros(acc_ref.shape, F32)

    def process(sb, near):
        k0 = pl.multiple_of(sb * ATTN_KSTEP, ATTN_KSTEP)
        kblk = k_ref[0, pl.ds(k0, ATTN_KSTEP), :]
        vblk = v_ref[0, pl.ds(k0, ATTN_KSTEP), :]
        tiles = [sb * ATTN_TILES + i for i in range(ATTN_TILES)]
        sel = jnp.concatenate([mask_ref[0, 0, t] for t in tiles], axis=1).astype(F32) > 0.0
        if near:
            bidx = [jnp.where(j - t == 0, 0, jnp.where(j - t == 1, 1, 2)) for t in tiles]
        for h in range(HEADS):
            cols = slice(h * HEAD_DIM, (h + 1) * HEAD_DIM)
            lg = _dot_nt(q[:, cols], kblk[:, cols]) * scale
            if near:
                lg = lg + jnp.concatenate([bias_ref[bi, h] for bi in bidx], axis=1)
            else:
                lg = lg + bfar_ref[h]
            lg = jnp.where(sel, lg, NEG_BIG)
            m_old = m_ref[h]
            m_new = jnp.maximum(m_old, jnp.max(lg, axis=-1, keepdims=True))
            p = jnp.exp(lg - jnp.concatenate([m_new] * ATTN_TILES, axis=1))
            alpha = jnp.exp(m_old - m_new)
            l_ref[h] = alpha * l_ref[h] + jnp.sum(p, axis=-1, keepdims=True)
            acc_ref[:, cols] = alpha * acc_ref[:, cols] + _dot(p.astype(BF), vblk[:, cols])
            m_ref[h] = m_new

    n_far = jnp.maximum((j - 1) // ATTN_TILES, 0)

    def far_step(sb, carry):
        process(sb, False)
        return carry

    def near_step(sb, carry):
        process(sb, True)
        return carry

    lax.fori_loop(0, n_far, far_step, 0)
    lax.fori_loop(n_far, j // ATTN_TILES + 1, near_step, 0)

    for h in range(HEADS):
        cols = slice(h * HEAD_DIM, (h + 1) * HEAD_DIM)
        o_ref[0, :, cols] = acc_ref[:, cols] / l_ref[h]


def attn_prompt(z3, kv_bf, mask, bias_tiles, bias_far):
    bsz, seq = z3.shape[:2]
    nqb = seq // LANES
    width = HEADS * HEAD_DIM
    assert seq % ATTN_KSTEP == 0
    return pl.pallas_call(
        _attn_prompt_body,
        grid=(bsz, nqb),
        in_specs=[pl.BlockSpec(memory_space=pltpu.SMEM),
                  pl.BlockSpec((1, LANES, width), lambda b, j: (b, j, 4)),
                  pl.BlockSpec((1, seq, width), lambda b, j: (b, 0, 0)),
                  pl.BlockSpec((1, seq, width), lambda b, j: (b, 0, 1)),
                  pl.BlockSpec((1, 1, nqb, LANES, LANES), lambda b, j: (b, j, 0, 0, 0)),
                  pl.BlockSpec(bias_tiles.shape, lambda b, j: (0, 0, 0, 0))],
        out_specs=pl.BlockSpec((1, LANES, width), lambda b, j: (b, j, 0)),
        out_shape=jax.ShapeDtypeStruct((bsz, seq, width), F32),
        scratch_shapes=[pltpu.VMEM((HEADS, LANES, LANES), F32),
                        pltpu.VMEM((HEADS, LANES, LANES), F32),
                        pltpu.VMEM((LANES, width), F32)],
        compiler_params=_cparams(("parallel", "arbitrary")),
        name="attn_prompt",
    )(bias_far, z3, kv_bf, kv_bf, mask, bias_tiles)


def _indexer_sample_body(pt_ref, qi_ref, wcol_ref, kinew_ref, *rest, n_pages):
    ki_refs = rest[:n_pages]
    out_ref = rest[n_pages]
    qi = qi_ref[0].astype(BF)
    w = wcol_ref[0] * (IDX_DIM ** -0.5 * IDX_HEADS ** -0.5)
    for i in range(n_pages):
        s = _dot_nt(qi, ki_refs[i][0].astype(BF))
        out_ref[i, 0] = jnp.sum(w * jnp.maximum(s, 0.0), axis=0, keepdims=True)
    kn = kinew_ref[0].astype(BF).astype(F32)
    sn = jnp.sum(qi.astype(F32) * kn, axis=-1, keepdims=True)
    new = jnp.sum(w * jnp.maximum(sn, 0.0), axis=0, keepdims=True)
    lane = lax.broadcasted_iota(I32, (1, LANES), 1)
    out_ref[n_pages, 0] = jnp.where(lane == 0, new, -jnp.inf)


def indexer_sample(page_table, qi3, wcol, ki_new, ki_pool):
    bsz, n_pages = page_table.shape
    ki_spec = lambda i: pl.BlockSpec((1, PAGE, IDX_DIM), lambda b, pt, i=i: (pt[b, i], 0, 0))
    grid_spec = pltpu.PrefetchScalarGridSpec(
        num_scalar_prefetch=1,
        grid=(bsz,),
        in_specs=[pl.BlockSpec((1, IDX_HEADS, IDX_DIM), lambda b, pt: (b, 0, 0)),
                  pl.BlockSpec((1, IDX_HEADS, 1), lambda b, pt: (b, 0, 0)),
                  pl.BlockSpec((1, 1, IDX_DIM), lambda b, pt: (b, 0, 0))]
                 + [ki_spec(i) for i in range(n_pages)],
        out_specs=pl.BlockSpec((n_pages + 1, 1, 1, LANES), lambda b, pt: (0, b, 0, 0)),
    )
    out = pl.pallas_call(
        functools.partial(_indexer_sample_body, n_pages=n_pages),
        grid_spec=grid_spec,
        out_shape=jax.ShapeDtypeStruct((n_pages + 1, bsz, 1, LANES), F32),
        compiler_params=_cparams(("arbitrary",)),
        name="indexer_sample",
    )(page_table, qi3, wcol, ki_new, *([ki_pool] * n_pages))
    return out.reshape(n_pages + 1, bsz, LANES)


def _select_body(score_ref, mask_ref, u_ref, *, n_tiles, k):
    def write_tile(t, m):
        mask_ref[t] = m

    _topk_mask(score_ref, u_ref, write_tile, n_tiles, n_tiles, 1, k)


def select_topk(scores, k):
    n_tiles, rows, _ = scores.shape
    return pl.pallas_call(
        functools.partial(_select_body, n_tiles=n_tiles, k=k),
        grid=(1,),
        in_specs=[pl.BlockSpec(scores.shape, lambda i: (0, 0, 0))],
        out_specs=pl.BlockSpec(scores.shape, lambda i: (0, 0, 0)),
        out_shape=jax.ShapeDtypeStruct(scores.shape, F32),
        scratch_shapes=[pltpu.VMEM(scores.shape, I32)],
        compiler_params=_cparams(("arbitrary",)),
        name="select_topk",
    )(scores)


SC_CORES = 2
SC_SUBCORES = 16
SC_GATHER_CHUNK = 32


def sc_gather_rows(tables, idx):
    n_rows = idx.shape[0]
    workers = SC_CORES * SC_SUBCORES
    per_worker = n_rows // workers
    chunk = SC_GATHER_CHUNK
    assert n_rows % workers == 0 and per_worker % chunk == 0 and chunk % 8 == 0
    row_shape = tables[0].shape[1:]
    n_tab = len(tables)
    mesh = plsc.VectorSubcoreMesh(core_axis_name="c", subcore_axis_name="s",
                                  num_cores=SC_CORES, num_subcores=SC_SUBCORES)

    def body(*refs):
        tab_refs = refs[:n_tab]
        idx_hbm = refs[n_tab]
        out_refs = refs[n_tab + 1:2 * n_tab + 1]
        idx_v, rows_v, sem = refs[2 * n_tab + 1:]
        wid = lax.axis_index("s") * SC_CORES + lax.axis_index("c")

        @pl.loop(0, per_worker // chunk)
        def _(ci):
            off = pl.multiple_of(wid * per_worker + ci * chunk, 8)
            pltpu.sync_copy(idx_hbm.at[pl.ds(off, chunk)], idx_v)
            for tab, out in zip(tab_refs, out_refs):
                pltpu.async_copy(tab.at[idx_v], rows_v, sem).wait()
                pltpu.sync_copy(rows_v, out.at[pl.ds(off, chunk)])

    return pl.kernel(
        body,
        out_type=[jax.ShapeDtypeStruct((n_rows, *row_shape), t.dtype) for t in tables],
        mesh=mesh,
        scratch_types=[pltpu.VMEM((chunk,), I32),
                       pltpu.VMEM((chunk, *row_shape), tables[0].dtype),
                       pltpu.SemaphoreType.DMA],
        compiler_params=pltpu.CompilerParams(use_tc_tiling_on_sc=True),
        name="sc_gather_rows",
    )(*tables, idx)


def _attn_compact_body(dist_ref, q_ref, kn_ref, vn_ref, btab_ref, kc_ref, vc_ref, o_ref, bbuf):
    b = pl.program_id(0)
    scale = HEAD_DIM ** -0.5

    def fill(t, carry):
        bbuf[t] = btab_ref[dist_ref[b, t]]
        return carry

    lax.fori_loop(0, TOPK, fill, 0)

    is_new = dist_ref[b, TOPK - 1] == 0
    last = lax.broadcasted_iota(I32, (TOPK, 1, 1), 0) == TOPK - 1
    kc = jnp.where(last & is_new, kn_ref[0][None], kc_ref[...])
    vc = jnp.where(last & is_new, vn_ref[0][None], vc_ref[...])
    q8 = q_ref[0]
    prod = (kc * q8[None]).reshape(TOPK * HEADS, HEAD_DIM)
    ones = jnp.ones((HEAD_DIM, LANES), BF)
    hi = prod.astype(BF)
    lo = (prod - hi.astype(F32)).astype(BF)
    lg = (_dot(hi, ones) + _dot(lo, ones)).reshape(TOPK, HEADS, LANES) * scale + bbuf[...]
    m = jnp.max(lg, axis=0, keepdims=True)
    p = jnp.exp(lg - m)
    l = jnp.sum(p, axis=0)
    o_ref[0] = jnp.sum(p * vc, axis=0) / l


def attn_sample_compact(dist, zs3, k_sel, v_sel, bias_by_dist):
    bsz = dist.shape[0]
    zspec = lambda cb: pl.BlockSpec((1, HEADS, HEAD_DIM), lambda b, d, cb=cb: (b, cb, 0))
    sel_spec = pl.BlockSpec((TOPK, HEADS, HEAD_DIM), lambda b, d: (b, 0, 0))
    grid_spec = pltpu.PrefetchScalarGridSpec(
        num_scalar_prefetch=1,
        grid=(bsz,),
        in_specs=[zspec(4), zspec(5), zspec(6),
                  pl.BlockSpec(bias_by_dist.shape, lambda b, d: (0, 0, 0)),
                  sel_spec, sel_spec],
        out_specs=pl.BlockSpec((1, HEADS, HEAD_DIM), lambda b, d: (b, 0, 0)),
        scratch_shapes=[pltpu.VMEM((TOPK, HEADS, HEAD_DIM), F32)],
    )
    return pl.pallas_call(
        _attn_compact_body,
        grid_spec=grid_spec,
        out_shape=jax.ShapeDtypeStruct((bsz, HEADS, HEAD_DIM), F32),
        compiler_params=_cparams(("arbitrary",)),
        name="attn_sample_compact",
    )(dist, zs3, zs3, zs3, bias_by_dist, k_sel, v_sel)


SAMPLE_PAGE_GROUP = 8


def _attn_sample_body(pt_ref, idx_ref, off_ref, q_ref, kn_ref, vn_ref, btab_ref, *rest,
                      n_pages, past_len):
    g = SAMPLE_PAGE_GROUP
    k_refs = rest[:g]
    v_refs = rest[g:2 * g]
    o_ref, kbuf, vbuf, bbuf = rest[2 * g:]
    b = pl.program_id(0)
    s = pl.program_id(1)
    scale = HEAD_DIM ** -0.5

    for i in range(g):
        page = s * g + i

        def gather(t, carry, i=i, page=page):
            key = idx_ref[b, t]
            r = key - page * PAGE
            kbuf[t] = k_refs[i][0, r]
            vbuf[t] = v_refs[i][0, r]
            bbuf[t] = btab_ref[jnp.minimum(past_len - key, REL_MAX_DIST)]
            return carry

        lax.fori_loop(off_ref[b, page], off_ref[b, page + 1], gather, 0)

    @pl.when(s == pl.num_programs(1) - 1)
    def _():
        @pl.when(off_ref[b, n_pages + 1] > off_ref[b, n_pages])
        def _():
            kbuf[TOPK - 1] = kn_ref[0]
            vbuf[TOPK - 1] = vn_ref[0]
            bbuf[TOPK - 1] = btab_ref[0]

        q8 = q_ref[0]
        prod = (kbuf[...] * q8[None]).reshape(TOPK * HEADS, HEAD_DIM)
        ones = jnp.ones((HEAD_DIM, LANES), BF)
        hi = prod.astype(BF)
        lo = (prod - hi.astype(F32)).astype(BF)
        lg = (_dot(hi, ones) + _dot(lo, ones)).reshape(TOPK, HEADS, LANES) * scale + bbuf[...]
        m = jnp.max(lg, axis=0, keepdims=True)
        p = jnp.exp(lg - m)
        l = jnp.sum(p, axis=0)
        o_ref[0] = jnp.sum(p * vbuf[...], axis=0) / l


def attn_sample(page_table, sel_idx, sel_off, zs3, k_pool, v_pool, bias_by_dist, past_len):
    bsz, n_pages = page_table.shape
    g = SAMPLE_PAGE_GROUP
    assert n_pages % g == 0
    zspec = lambda cb: pl.BlockSpec((1, HEADS, HEAD_DIM), lambda b, s, pt, ix, of, cb=cb: (b, cb, 0))
    pool_spec = lambda i: pl.BlockSpec((1, PAGE, HEADS, HEAD_DIM),
                                       lambda b, s, pt, ix, of, i=i: (pt[b, s * g + i], 0, 0, 0))
    grid_spec = pltpu.PrefetchScalarGridSpec(
        num_scalar_prefetch=3,
        grid=(bsz, n_pages // g),
        in_specs=[zspec(4), zspec(5), zspec(6),
                  pl.BlockSpec(bias_by_dist.shape, lambda b, s, pt, ix, of: (0, 0, 0))]
                 + [pool_spec(i) for i in range(g)] * 2,
        out_specs=pl.BlockSpec((1, HEADS, HEAD_DIM), lambda b, s, pt, ix, of: (b, 0, 0)),
        scratch_shapes=[pltpu.VMEM((TOPK, HEADS, HEAD_DIM), F32)] * 3,
    )
    return pl.pallas_call(
        functools.partial(_attn_sample_body, n_pages=n_pages, past_len=past_len),
        grid_spec=grid_spec,
        out_shape=jax.ShapeDtypeStruct((bsz, HEADS, HEAD_DIM), F32),
        compiler_params=_cparams(("arbitrary", "arbitrary")),
        name="attn_sample",
    )(page_table, sel_idx, sel_off, zs3, zs3, zs3, bias_by_dist,
      *([k_pool] * g), *([v_pool] * g))


def _bucket_table(max_dist):
    exact = REL_BUCKETS // 2
    d = np.arange(max_dist + 1)
    df = np.maximum(d, 1).astype(np.float32)
    far = exact + (np.log(df / exact) / np.float32(math.log(REL_MAX_DIST / exact))
                   * (REL_BUCKETS - exact)).astype(np.int32)
    return np.where(d < exact, d, np.minimum(far, REL_BUCKETS - 1)).astype(np.int32)


def _bias_tables(rel_bias):
    tab = _bucket_table(2 * LANES)
    assert np.all(tab[REL_MAX_DIST:] == REL_BUCKETS - 1)
    i = np.arange(LANES)
    dist0 = np.maximum(i[:, None] - i[None, :], 0)
    dist1 = LANES + i[:, None] - i[None, :]
    far = np.full((LANES, LANES), REL_BUCKETS - 1)
    idx = np.stack([tab[dist0], tab[dist1], far])
    tiles = jnp.transpose(rel_bias[idx], (0, 3, 1, 2))
    by_dist = jnp.broadcast_to(rel_bias[tab[:REL_MAX_DIST + 1]][:, :, None],
                               (REL_MAX_DIST + 1, HEADS, LANES))
    return tiles.astype(F32), rel_bias[REL_BUCKETS - 1].astype(F32), by_dist.astype(F32)


def _conv_prompt_body(bg_ref, cg_ref, xt_ref, w_ref, v_ref, st_ref, carry_ref, *, tblock):
    t = pl.program_id(1)

    @pl.when(t == 0)
    def _():
        carry_ref[...] = jnp.zeros_like(carry_ref)

    u = cg_ref[0] * xt_ref[0]
    row = lax.broadcasted_iota(I32, u.shape, 0)
    c0 = carry_ref[0:1]
    c1 = carry_ref[1:2]
    u1 = jnp.where(row == 0, c1, pltpu.roll(u, 1, axis=0))
    u2 = jnp.where(row == 0, c0, jnp.where(row == 1, c1, pltpu.roll(u, 2, axis=0)))
    conv = w_ref[0:1] * u2 + w_ref[1:2] * u1 + w_ref[2:3] * u
    v_ref[0] = bg_ref[0] * conv
    last = u[tblock - 2:tblock]
    carry_ref[0:2] = last

    @pl.when(t == pl.num_programs(1) - 1)
    def _():
        st_ref[0] = last


def conv_prompt(zc3, w_conv, tblock=256):
    bsz, seq = zc3.shape[:2]
    c = zc3.shape[2] // 3
    zspec = lambda cb: pl.BlockSpec((1, tblock, c), lambda b, t, cb=cb: (b, t, cb))
    return pl.pallas_call(
        functools.partial(_conv_prompt_body, tblock=tblock),
        grid=(bsz, seq // tblock),
        in_specs=[zspec(0), zspec(1), zspec(2), pl.BlockSpec((3, c), lambda b, t: (0, 0))],
        out_specs=[pl.BlockSpec((1, tblock, c), lambda b, t: (b, t, 0)),
                   pl.BlockSpec((1, 2, c), lambda b, t: (b, 0, 0))],
        out_shape=[jax.ShapeDtypeStruct((bsz, seq, c), F32),
                   jax.ShapeDtypeStruct((bsz, 2, c), F32)],
        scratch_shapes=[pltpu.VMEM((8, c), F32)],
        compiler_params=_cparams(("parallel", "arbitrary")),
        name="conv_prompt",
    )(zc3, zc3, zc3, w_conv)


def _conv_sample_body(bg_ref, cg_ref, xt_ref, w_ref, s0_ref, s1_ref, v_ref, n0_ref, n1_ref):
    u = cg_ref[...] * xt_ref[...]
    conv = w_ref[0:1] * s0_ref[...] + w_ref[1:2] * s1_ref[...] + w_ref[2:3] * u
    v_ref[...] = bg_ref[...] * conv
    n0_ref[...] = s1_ref[...]
    n1_ref[...] = u


def conv_sample(zc, w_conv, s0, s1):
    bsz = zc.shape[0]
    c = zc.shape[1] // 3
    zspec = lambda cb: pl.BlockSpec((bsz, c), lambda i, cb=cb: (0, cb))
    full = pl.BlockSpec((bsz, c), lambda i: (0, 0))
    return pl.pallas_call(
        _conv_sample_body,
        grid=(1,),
        in_specs=[zspec(0), zspec(1), zspec(2), pl.BlockSpec((3, c), lambda i: (0, 0)), full, full],
        out_specs=[full, full, full],
        out_shape=[jax.ShapeDtypeStruct((bsz, c), F32)] * 3,
        compiler_params=_cparams(("arbitrary",)),
        name="conv_sample",
    )(zc, zc, zc, w_conv, s0, s1)


def _router_body(x_ref, g_ref, wr_ref, br_ref, h_ref, route_ref):
    x = x_ref[...]
    ms = jnp.mean(x * x, axis=-1, keepdims=True)
    h = x * lax.rsqrt(ms + RMS_EPS) * g_ref[...]
    h_ref[...] = h.astype(BF)
    logits = jnp.dot(h, wr_ref[...], preferred_element_type=F32,
                     precision=lax.Precision.HIGHEST) + br_ref[...]
    lane = lax.broadcasted_iota(I32, logits.shape, 1)
    big = np.int32(1 << 20)
    lg = jnp.where(lane < N_GROUPS, logits, -jnp.inf)
    g_max = jnp.max(lg, axis=-1, keepdims=True)
    g_idx = jnp.min(jnp.where(lg == g_max, lane, big), axis=-1, keepdims=True)
    g_w = 1.0 / jnp.sum(jnp.exp(lg - g_max), axis=-1, keepdims=True)
    first = N_GROUPS + EXP_PER_GROUP * g_idx
    le = jnp.where((lane >= first) & (lane < first + EXP_PER_GROUP), logits, -jnp.inf)
    l1 = jnp.max(le, axis=-1, keepdims=True)
    i1 = jnp.min(jnp.where(le == l1, lane, big), axis=-1, keepdims=True)
    le2 = jnp.where(lane == i1, -jnp.inf, le)
    l2 = jnp.max(le2, axis=-1, keepdims=True)
    i2 = jnp.min(jnp.where(le2 == l2, lane, big), axis=-1, keepdims=True)
    r = jnp.exp(l2 - l1)
    w1 = g_w / (1.0 + r)
    w2 = g_w * r / (1.0 + r)
    e1 = (i1 - N_GROUPS).astype(F32)
    e2 = (i2 - N_GROUPS).astype(F32)
    route_ref[...] = jnp.where(lane == 0, e1, jnp.where(lane == 1, e2,
                               jnp.where(lane == 2, w1, jnp.where(lane == 3, w2, 0.0))))


def moe_router(x, g, wr, br, tm):
    m, k = x.shape
    return pl.pallas_call(
        _router_body,
        grid=(m // tm,),
        in_specs=[pl.BlockSpec((tm, k), lambda i: (i, 0)),
                  pl.BlockSpec((1, k), lambda i: (0, 0)),
                  pl.BlockSpec((k, LANES), lambda i: (0, 0)),
                  pl.BlockSpec((1, LANES), lambda i: (0, 0))],
        out_specs=[pl.BlockSpec((tm, k), lambda i: (i, 0)),
                   pl.BlockSpec((tm, LANES), lambda i: (i, 0))],
        out_shape=[jax.ShapeDtypeStruct((m, k), BF),
                   jax.ShapeDtypeStruct((m, LANES), F32)],
        compiler_params=_cparams(("parallel",)),
        name="moe_router",
    )(x, g.reshape(1, k), wr, br)


def _experts_body(te_ref, nu_ref, h_ref, gate_ref, wg_ref, wu_ref, wd_ref, o_ref,
                  wg_bf, wu_bf, wd_bf):
    i = pl.program_id(0)

    @pl.when((i == 0) | (te_ref[i] != te_ref[jnp.maximum(i - 1, 0)]))
    def _():
        wg_bf[...] = wg_ref[0].astype(BF)
        wu_bf[...] = wu_ref[0].astype(BF)
        wd_bf[...] = wd_ref[0].astype(BF)

    @pl.when(i < nu_ref[0])
    def _():
        h = h_ref[...]
        a = _dot(h, wg_bf[...])
        b = _dot(h, wu_bf[...])
        hid = a * _sigmoid(a) * b * gate_ref[...]
        o_ref[...] = _dot(hid.astype(BF), wd_bf[...])

    @pl.when(i >= nu_ref[0])
    def _():
        o_ref[...] = jnp.zeros_like(o_ref)


def moe_experts(tile_expert, n_used, hs, gate_col, wg, wu, wd):
    p, k = hs.shape
    f = wg.shape[2]
    n_tiles = p // MOE_TILE
    grid_spec = pltpu.PrefetchScalarGridSpec(
        num_scalar_prefetch=2,
        grid=(n_tiles,),
        in_specs=[pl.BlockSpec((MOE_TILE, k), lambda i, te, nu: (i, 0)),
                  pl.BlockSpec((MOE_TILE, 1), lambda i, te, nu: (i, 0)),
                  pl.BlockSpec((1, k, f), lambda i, te, nu: (te[i], 0, 0)),
                  pl.BlockSpec((1, k, f), lambda i, te, nu: (te[i], 0, 0)),
                  pl.BlockSpec((1, f, k), lambda i, te, nu: (te[i], 0, 0))],
        out_specs=pl.BlockSpec((MOE_TILE, k), lambda i, te, nu: (i, 0)),
        scratch_shapes=[pltpu.VMEM((k, f), BF), pltpu.VMEM((k, f), BF), pltpu.VMEM((f, k), BF)],
    )
    return pl.pallas_call(
        _experts_body,
        grid_spec=grid_spec,
        out_shape=jax.ShapeDtypeStruct((p, k), F32),
        compiler_params=_cparams(("arbitrary",)),
        name="moe_experts",
    )(tile_expert, n_used, hs, gate_col, wg, wu, wd)


def _rank_within_expert(onehot):
    n, e = onehot.shape
    blk = LANES
    assert n % blk == 0
    oh = onehot.astype(F32).reshape(n // blk, blk, e)
    strict = jnp.asarray(np.tril(np.ones((blk, blk), np.float32), -1))
    within = jnp.einsum("ij,bjk->bik", strict, oh, precision=lax.Precision.HIGHEST)
    totals = jnp.sum(oh, axis=1)
    before = jnp.cumsum(totals, axis=0) - totals
    rank = (within + before[:, None, :]).reshape(n, e)
    return jnp.sum(rank * onehot.astype(F32), axis=1).astype(I32), jnp.sum(totals, axis=0).astype(I32)


def hier_moe(xs, tms, g, wrg, brg, wre, bre, wg, wu, wd, layer):
    d = xs[0].shape[1]
    m = sum(x.shape[0] for x in xs)
    wr = jnp.zeros((d, LANES), F32)
    wr = wr.at[:, :N_GROUPS].set(wrg).at[:, N_GROUPS:N_GROUPS + N_EXPERTS].set(wre.reshape(d, N_EXPERTS))
    br = jnp.zeros((1, LANES), F32)
    br = br.at[0, :N_GROUPS].set(brg).at[0, N_GROUPS:N_GROUPS + N_EXPERTS].set(bre.reshape(N_EXPERTS))
    routed = [moe_router(x, g, wr, br, tm) for x, tm in zip(xs, tms)]
    h_bf = jnp.concatenate([r[0] for r in routed], axis=0)
    route = jnp.concatenate([r[1] for r in routed], axis=0)

    eid = route[:, 0:2].astype(I32).reshape(-1)
    gate = route[:, 2:4].reshape(-1)
    onehot = eid[:, None] == jnp.arange(N_EXPERTS, dtype=I32)[None, :]
    rank, counts = _rank_within_expert(onehot)
    padded = ((counts + MOE_TILE - 1) // MOE_TILE) * MOE_TILE
    ends = jnp.cumsum(padded)
    pos = (ends - padded)[eid] + rank
    n_rows = 2 * m + N_EXPERTS * MOE_TILE
    n_rows = -(-n_rows // MOE_TILE) * MOE_TILE
    pair = jnp.zeros((n_rows,), I32).at[pos].set(jnp.arange(2 * m, dtype=I32))
    tile_start = jnp.arange(n_rows // MOE_TILE, dtype=I32) * MOE_TILE
    tile_expert = jnp.minimum(jnp.sum(tile_start[:, None] >= ends[None, :], axis=1),
                              N_EXPERTS - 1).astype(I32)
    n_used = (ends[-1] // MOE_TILE).astype(I32).reshape(1)

    hs = jnp.take(h_bf, pair // 2, axis=0)
    out = moe_experts(tile_expert + layer * N_EXPERTS, n_used, hs, jnp.take(gate, pair)[:, None],
                      wg, wu, wd)
    pos2 = pos.reshape(m, 2)
    res, r0 = [], 0
    for x in xs:
        pp = pos2[r0:r0 + x.shape[0]]
        res.append(x + jnp.take(out, pp[:, 0], axis=0) + jnp.take(out, pp[:, 1], axis=0))
        r0 += x.shape[0]
    return res


def kernel(x_prompt, x_sample, cache_k, cache_v, cache_ki, state_hgrn, state_conv, page_table,
           norm_mix_g, norm_ffn_g, final_g, w_in_even, w_out_even, hgrn_lb_logits, hgrn_norm_g,
           rel_bias, w_in_conv, w_conv, w_out_conv, w_router_g, b_router_g, w_router_e,
           b_router_e, w_gate, w_up, w_down):
    bsz, seq, d = x_prompt.shape
    dec = x_sample.shape[0]
    n_p = bsz * seq
    tm_p, tm_s = 512, 128
    tms = (tm_p, tm_s)
    assert n_p % tm_p == 0 and dec % tm_s == 0 and x_sample.shape[1] == 1
    width = HEADS * HEAD_DIM
    even_in = w_in_even.shape[2]
    even_pad = -(-even_in // LANES) * LANES
    tail0 = even_pad - LANES
    n_pool = cache_k.shape[1]

    xp = x_prompt.reshape(n_p, d)
    xs = x_sample.reshape(dec, d)

    lbs = jnp.cumsum(jax.nn.softmax(hgrn_lb_logits.astype(F32), axis=0), axis=0)[:-1]
    bias_tiles, bias_far, bias_by_dist = _bias_tables(rel_bias)
    expert_w = [w.reshape(-1, *w.shape[2:]) for w in (w_gate, w_up, w_down)]

    w_in = jnp.pad(w_in_even[0], ((0, 0), (0, even_pad - even_in))).astype(BF)
    zp = norm_matmul(xp, norm_mix_g[0], w_in, 2 * tm_p, 640)
    zs = norm_matmul(xs, norm_mix_g[0], w_in, tm_s, 640)
    zp3 = zp.reshape(bsz, seq, even_pad)
    zs3 = zs.reshape(dec, even_pad // LANES, LANES)

    oa_p, hgrn_p = hgrn_prompt(zp3, lbs[0], hgrn_norm_g[0])
    oa_s, hgrn_s = hgrn_sample(zs3, lbs[0], hgrn_norm_g[0], state_hgrn[0])

    mask_p = indexer_prompt(zp3, TOPK)
    kv_bf = zp3[:, :, 5 * width:7 * width].astype(BF)
    ob_p = attn_prompt(zp3, kv_bf, mask_p, bias_tiles, bias_far)

    qi3 = zs[:, 7 * width:8 * width].reshape(dec, IDX_HEADS, IDX_DIM)
    ki_new = zs[:, tail0:tail0 + IDX_DIM].reshape(dec, 1, IDX_DIM)
    wcol = zs[:, tail0 + IDX_DIM:tail0 + IDX_DIM + IDX_HEADS].reshape(dec, IDX_HEADS, 1)
    pages = page_table + 0 * n_pool
    scores_s = indexer_sample(pages, qi3, wcol, ki_new,
                              cache_ki.reshape(-1, PAGE, IDX_DIM))
    mask_s = select_topk(scores_s, TOPK)
    sel = jnp.transpose(mask_s, (1, 0, 2))
    n_pages = page_table.shape[1]
    sel_off = jnp.concatenate([jnp.zeros((dec, 1), F32),
                               jnp.cumsum(jnp.sum(sel, axis=2), axis=1)], axis=1)
    slot = jnp.arange(TOPK, dtype=F32)
    page_of = jnp.sum(sel_off[:, None, 1:] <= slot[None, :, None], axis=2)
    page_1h = (page_of[:, :, None] == jnp.arange(n_pages + 1)[None, None, :]).astype(F32)
    local = slot[None, :] - jnp.einsum("brp,bp->br", page_1h, sel_off[:, :-1],
                                       precision=lax.Precision.HIGHEST)
    within = jnp.einsum("brp,bpl->brl", page_1h, jnp.cumsum(sel, axis=2),
                        precision=lax.Precision.HIGHEST)
    lane_of = jnp.sum(within <= local[:, :, None], axis=2)
    sel_idx = (page_of * PAGE + lane_of).astype(I32)
    past = n_pages * PAGE
    in_past = sel_idx < past
    page_id = jnp.take_along_axis(pages, jnp.minimum(sel_idx, past - 1) // PAGE, axis=1)
    rows = jnp.where(in_past, page_id * PAGE + sel_idx % PAGE, 0).reshape(-1)
    k_sel, v_sel = sc_gather_rows([cache_k.reshape(-1, HEADS, HEAD_DIM),
                                   cache_v.reshape(-1, HEADS, HEAD_DIM)], rows)
    dist = jnp.minimum(past - sel_idx, REL_MAX_DIST).astype(I32)
    ob_s = attn_sample_compact(dist, zs3, k_sel, v_sel, bias_by_dist)

    w_out = w_out_even[0].astype(BF)
    w_out_ab = [w_out[:width], w_out[width:]]
    xp = matmul_residual([oa_p.reshape(n_p, width), ob_p.reshape(n_p, width)], w_out_ab, xp, 2 * tm_p, 512)
    xs = matmul_residual([oa_s.reshape(dec, width), ob_s.reshape(dec, width)], w_out_ab, xs, tm_s, 512)

    xp, xs = hier_moe([xp, xs], tms, norm_ffn_g[0], w_router_g[0], b_router_g[0], w_router_e[0],
                      b_router_e[0], *expert_w, 0)

    w_in_c = w_in_conv[0].astype(BF)
    zcp = norm_matmul(xp, norm_mix_g[1], w_in_c, 2 * tm_p, 1024)
    zcs = norm_matmul(xs, norm_mix_g[1], w_in_c, tm_s, 512)
    cw = zcp.shape[1] // 3
    v_p, conv_p = conv_prompt(zcp.reshape(bsz, seq, 3 * cw), w_conv[0])
    v_s, cs0, cs1 = conv_sample(zcs, w_conv[0], state_conv[0, :, 0], state_conv[0, :, 1])
    w_out_c = [w_out_conv[0].astype(BF)]
    xp = matmul_residual([v_p.reshape(n_p, cw)], w_out_c, xp, 2 * tm_p, 512)
    xs = matmul_residual([v_s], w_out_c, xs, tm_s, 512)

    xp, xs = hier_moe([xp, xs], tms, norm_ffn_g[1], w_router_g[1], b_router_g[1], w_router_e[1],
                      b_router_e[1], *expert_w, 1)

    yp = rmsnorm(xp, final_g, tm_p)
    ys = rmsnorm(xs, final_g, tm_s)

    kcol, vcol = 5 * width, 6 * width
    heads = lambda a, n: a.reshape(1, *n, HEADS, HEAD_DIM)
    return (yp.reshape(bsz, seq, d),
            ys.reshape(dec, 1, d),
            heads(zp[:, kcol:kcol + width], (bsz, seq)),
            heads(zp[:, vcol:vcol + width], (bsz, seq)),
            zp[:, tail0:tail0 + IDX_DIM].reshape(1, bsz, seq, IDX_DIM),
            hgrn_p[None],
            conv_p[None],
            heads(zs[:, kcol:kcol + width], (dec, 1)),
            heads(zs[:, vcol:vcol + width], (dec, 1)),
            zs[:, tail0:tail0 + IDX_DIM].reshape(1, dec, 1, IDX_DIM),
            hgrn_s[None],
            jnp.stack([cs0, cs1], axis=1)[None])
```

```python
import functools
import math

import numpy as np
import jax
import jax.numpy as jnp
from jax import lax
from jax.experimental import pallas as pl
from jax.experimental.pallas import tpu as pltpu
from jax.experimental.pallas import tpu_sc as plsc

F32 = jnp.float32
BF = jnp.bfloat16
I32 = jnp.int32

RMS_EPS = 1e-6
LANES = 128
NEG_BIG = -1e30
VMEM_LIMIT = 56 * 1024 * 1024

D_MODEL = 2048
HEADS = 8
HEAD_DIM = 128
IDX_HEADS = 16
IDX_DIM = 64
TOPK = 256
REL_BUCKETS = 32
REL_MAX_DIST = 128
N_EXPERTS = 16
EXP_PER_GROUP = 4
N_GROUPS = 4
D_EXPERT = 512
PAGE = 128

HGRN_CHUNK = 64
MOE_TILE = 256


def _cparams(sem):
    return pltpu.CompilerParams(dimension_semantics=sem, vmem_limit_bytes=VMEM_LIMIT)


def _dot(a, b):
    return jnp.dot(a, b, preferred_element_type=F32)


def _dot_nt(a, b):
    return lax.dot_general(a, b, (((1,), (1,)), ((), ())), preferred_element_type=F32)


def _dot_tn(a, b):
    return lax.dot_general(a, b, (((0,), (0,)), ((), ())), preferred_element_type=F32)


def _sigmoid(x):
    return 1.0 / (1.0 + jnp.exp(-x))


def _norm_mm_body(x_ref, g_ref, w_ref, o_ref, h_ref):
    @pl.when(pl.program_id(1) == 0)
    def _():
        x = x_ref[...]
        ms = jnp.mean(x * x, axis=-1, keepdims=True)
        h_ref[...] = (x * lax.rsqrt(ms + RMS_EPS) * g_ref[...]).astype(BF)

    o_ref[...] = _dot(h_ref[...], w_ref[...])


def norm_matmul(x, g, w_bf, tm, tn):
    m, k = x.shape
    n = w_bf.shape[1]
    return pl.pallas_call(
        _norm_mm_body,
        grid=(m // tm, n // tn),
        in_specs=[pl.BlockSpec((tm, k), lambda i, j: (i, 0)),
                  pl.BlockSpec((1, k), lambda i, j: (0, 0)),
                  pl.BlockSpec((k, tn), lambda i, j: (0, j))],
        out_specs=pl.BlockSpec((tm, tn), lambda i, j: (i, j)),
        out_shape=jax.ShapeDtypeStruct((m, n), F32),
        scratch_shapes=[pltpu.VMEM((tm, k), BF)],
        compiler_params=_cparams(("parallel", "arbitrary")),
        name="norm_matmul",
    )(x, g.reshape(1, k), w_bf)


def _mm_res_body(*refs, n_lhs):
    a_refs = refs[:n_lhs]
    w_refs = refs[n_lhs:2 * n_lhs]
    r_ref = refs[2 * n_lhs]
    o_ref = refs[2 * n_lhs + 1]
    s_refs = refs[2 * n_lhs + 2:]

    @pl.when(pl.program_id(1) == 0)
    def _():
        for a_ref, s_ref in zip(a_refs, s_refs):
            s_ref[...] = a_ref[...].astype(BF)

    acc = r_ref[...]
    for s_ref, w_ref in zip(s_refs, w_refs):
        acc = acc + _dot(s_ref[...], w_ref[...])
    o_ref[...] = acc


def matmul_residual(lhs, ws_bf, res, tm, tn):
    m, n = res.shape
    n_lhs = len(lhs)
    in_specs = ([pl.BlockSpec((tm, a.shape[1]), lambda i, j: (i, 0)) for a in lhs]
                + [pl.BlockSpec((w.shape[0], tn), lambda i, j: (0, j)) for w in ws_bf]
                + [pl.BlockSpec((tm, tn), lambda i, j: (i, j))])
    return pl.pallas_call(
        functools.partial(_mm_res_body, n_lhs=n_lhs),
        grid=(m // tm, n // tn),
        in_specs=in_specs,
        out_specs=pl.BlockSpec((tm, tn), lambda i, j: (i, j)),
        out_shape=jax.ShapeDtypeStruct((m, n), F32),
        scratch_shapes=[pltpu.VMEM((tm, a.shape[1]), BF) for a in lhs],
        compiler_params=_cparams(("parallel", "arbitrary")),
        name="matmul_residual",
    )(*lhs, *ws_bf, res)


def _rmsnorm_body(x_ref, g_ref, o_ref):
    x = x_ref[...]
    ms = jnp.mean(x * x, axis=-1, keepdims=True)
    o_ref[...] = x * lax.rsqrt(ms + RMS_EPS) * g_ref[...]


def rmsnorm(x, g, tm):
    m, k = x.shape
    return pl.pallas_call(
        _rmsnorm_body,
        grid=(m // tm,),
        in_specs=[pl.BlockSpec((tm, k), lambda i: (i, 0)),
                  pl.BlockSpec((1, k), lambda i: (0, 0))],
        out_specs=pl.BlockSpec((tm, k), lambda i: (i, 0)),
        out_shape=jax.ShapeDtypeStruct((m, k), F32),
        compiler_params=_cparams(("parallel",)),
        name="rmsnorm",
    )(x, g.reshape(1, k))


def _hgrn_static(c):
    levels = []
    m = 1
    while m < c:
        levels.append(m)
        m *= 2
    t = np.arange(c)
    rows = [t[None, :] <= t[:, None]]
    masks = [np.eye(c, dtype=bool)]
    for m in levels:
        blk = t // (2 * m)
        pos = t % (2 * m)
        bnd = blk * 2 * m + m - 1
        right = pos >= m
        left = pos < m
        e_rows = (t[None, :] > bnd[:, None]) & (t[None, :] <= t[:, None]) & right[:, None]
        f_rows = (t[None, :] > t[:, None]) & (t[None, :] <= bnd[:, None]) & left[:, None]
        rows.append(e_rows | f_rows)
        masks.append((blk[:, None] == blk[None, :]) & right[:, None] & left[None, :])
    m_all = np.stack(rows).astype(np.float32)
    masks = np.stack(masks).astype(np.float32)
    return m_all, masks, len(levels)


def _hgrn_gates(qa, fa, lb):
    f = lb + (1.0 - lb) * _sigmoid(fa)
    q = qa * _sigmoid(qa)
    return q, f


def _hgrn_prompt_body(qa_ref, fa_ref, ia_ref, ga_ref, lb_ref, gn_ref, mall_ref, masks_ref,
                      oa_ref, st_out_ref, st_ref, *, chunk, tblock, n_levels):
    c = chunk
    t_idx = pl.program_id(1)

    @pl.when(t_idx == 0)
    def _():
        st_ref[...] = jnp.zeros_like(st_ref)

    gn = gn_ref[...]

    def chunk_step(ci, carry):
        r0 = pl.multiple_of(ci * c, c)
        for h in range(HEADS):
            cols = slice(h * HEAD_DIM, (h + 1) * HEAD_DIM)
            qa = qa_ref[0, pl.ds(r0, c), cols]
            fa = fa_ref[0, pl.ds(r0, c), cols]
            ia = ia_ref[0, pl.ds(r0, c), cols]
            ga = ga_ref[0, pl.ds(r0, c), cols]
            q, f = _hgrn_gates(qa, fa, lb_ref[:, cols])
            k = 1.0 - f
            g = jnp.log(f)
            g_hi = g.astype(BF)
            g_lo = (g - g_hi.astype(F32)).astype(BF)
            b = _dot(mall_ref[0], g_hi) + _dot(mall_ref[0], g_lo)
            b_last = b[c - 1:c]
            st = st_ref[h]
            v = ia.astype(BF)
            o = _dot_nt((q * jnp.exp(b)).astype(BF), st.astype(BF))
            a = masks_ref[0] * _dot_nt(q.astype(BF), k.astype(BF))
            for li in range(n_levels):
                ml = mall_ref[1 + li]
                w = jnp.exp(_dot(ml, g_hi) + _dot(ml, g_lo))
                a = a + masks_ref[1 + li] * _dot_nt((q * w).astype(BF), (k * w).astype(BF))
            o = o + _dot(a.astype(BF), v)
            k_st = (k * jnp.exp(b_last - b)).astype(BF)
            st_ref[h] = st * jnp.exp(b_last) + _dot_tn(v, k_st)
            ms = jnp.mean(o * o, axis=-1, keepdims=True)
            y = o * lax.rsqrt(ms + RMS_EPS) * gn * (ga * _sigmoid(ga))
            oa_ref[0, pl.ds(r0, c), cols] = y
        return carry

    lax.fori_loop(0, tblock // c, chunk_step, 0)

    @pl.when(t_idx == pl.num_programs(1) - 1)
    def _():
        st_out_ref[0] = st_ref[...]


def hgrn_prompt(z3, lb, gn, tblock=256, chunk=HGRN_CHUNK):
    bsz, seq = z3.shape[:2]
    width = HEADS * HEAD_DIM
    m_all, masks, n_levels = _hgrn_static(chunk)
    zspec = lambda cb: pl.BlockSpec((1, tblock, width), lambda b, t, cb=cb: (b, t, cb))
    oa, st = pl.pallas_call(
        functools.partial(_hgrn_prompt_body, chunk=chunk, tblock=tblock, n_levels=n_levels),
        grid=(bsz, seq // tblock),
        in_specs=[zspec(0), zspec(1), zspec(2), zspec(3),
                  pl.BlockSpec((1, width), lambda b, t: (0, 0)),
                  pl.BlockSpec((1, HEAD_DIM), lambda b, t: (0, 0)),
                  pl.BlockSpec(m_all.shape, lambda b, t: (0, 0, 0)),
                  pl.BlockSpec(masks.shape, lambda b, t: (0, 0, 0))],
        out_specs=[pl.BlockSpec((1, tblock, width), lambda b, t: (b, t, 0)),
                   pl.BlockSpec((1, HEADS, HEAD_DIM, HEAD_DIM), lambda b, t: (b, 0, 0, 0))],
        out_shape=[jax.ShapeDtypeStruct((bsz, seq, width), F32),
                   jax.ShapeDtypeStruct((bsz, HEADS, HEAD_DIM, HEAD_DIM), F32)],
        scratch_shapes=[pltpu.VMEM((HEADS, HEAD_DIM, HEAD_DIM), F32)],
        compiler_params=_cparams(("parallel", "arbitrary")),
        name="hgrn_prompt",
    )(z3, z3, z3, z3, lb.reshape(1, width), gn.reshape(1, HEAD_DIM),
      jnp.asarray(m_all, BF), jnp.asarray(masks, F32))
    return oa, jnp.swapaxes(st, -1, -2)


def _col(row, eye):
    return jnp.sum(eye * row, axis=1, keepdims=True)


def _hgrn_sample_body(qa_ref, fa_ref, ia_ref, ga_ref, lb_ref, gn_ref, s_ref, oa_ref, so_ref):
    eye = (lax.broadcasted_iota(I32, (HEAD_DIM, HEAD_DIM), 0)
           == lax.broadcasted_iota(I32, (HEAD_DIM, HEAD_DIM), 1)).astype(F32)
    q8, f8 = _hgrn_gates(qa_ref[0], fa_ref[0], lb_ref[...])
    ga = ga_ref[0]
    gate = ga * _sigmoid(ga)
    outs = []
    for h in range(HEADS):
        f_col = _col(f8[h:h + 1], eye)
        q_col = _col(q8[h:h + 1], eye)
        s_new = f_col * s_ref[0, h] + (1.0 - f_col) * ia_ref[0, h:h + 1]
        so_ref[0, h] = s_new
        outs.append(jnp.sum(q_col * s_new, axis=0, keepdims=True))
    o = jnp.concatenate(outs, axis=0)
    ms = jnp.mean(o * o, axis=-1, keepdims=True)
    oa_ref[0] = o * lax.rsqrt(ms + RMS_EPS) * gn_ref[...] * gate


def hgrn_sample(zs3, lb, gn, s0):
    bsz = zs3.shape[0]
    zspec = lambda cb: pl.BlockSpec((1, HEADS, HEAD_DIM), lambda b, cb=cb: (b, cb, 0))
    sspec = pl.BlockSpec((1, HEADS, HEAD_DIM, HEAD_DIM), lambda b: (b, 0, 0, 0))
    return pl.pallas_call(
        _hgrn_sample_body,
        grid=(bsz,),
        in_specs=[zspec(0), zspec(1), zspec(2), zspec(3),
                  pl.BlockSpec((HEADS, HEAD_DIM), lambda b: (0, 0)),
                  pl.BlockSpec((1, HEAD_DIM), lambda b: (0, 0)),
                  sspec],
        out_specs=[pl.BlockSpec((1, HEADS, HEAD_DIM), lambda b: (b, 0, 0)), sspec],
        out_shape=[jax.ShapeDtypeStruct((bsz, HEADS, HEAD_DIM), F32),
                   jax.ShapeDtypeStruct(s0.shape, F32)],
        compiler_params=_cparams(("parallel",)),
        name="hgrn_sample",
    )(zs3, zs3, zs3, zs3, lb.reshape(HEADS, HEAD_DIM), gn.reshape(1, HEAD_DIM), s0)


_KEY_NEG_INF = np.int32(np.uint32(0x807FFFFF).astype(np.int64) - (1 << 32))
_INT_MIN = np.int32(-(1 << 31))


def _count(u_ref, n_groups, group, thr, cmp):
    rows = u_ref.shape[1]
    step = min(rows, LANES)
    parts = []
    for r0 in range(0, rows, step):
        t = jnp.broadcast_to(thr[r0:r0 + step], (step, LANES))

        def body(gi, acc, r0=r0, t=t):
            for i in range(group):
                acc = acc + cmp(u_ref[gi * group + i, r0:r0 + step], t).astype(F32)
            return acc

        acc = lax.fori_loop(0, n_groups, body, jnp.zeros((step, LANES), F32))
        parts.append(jnp.sum(acc, axis=-1, keepdims=True))
    return parts[0] if len(parts) == 1 else jnp.concatenate(parts, axis=0)


def _topk_mask(score_ref, u_ref, write_tile, n_tiles, n_groups, group, k):
    rows = score_ref.shape[1]
    n_live = n_groups * group

    def to_key(t, carry):
        bits = pltpu.bitcast(score_ref[t], I32)
        u_ref[t] = jnp.where(bits < 0, bits ^ np.int32(0x7FFFFFFF), bits)
        return carry

    lax.fori_loop(0, n_live, to_key, 0)

    kf = float(k)
    ge = lambda u, t: u >= t
    cnt = _count(u_ref, n_groups, group, jnp.zeros((rows, 1), I32), ge)
    lo = jnp.where(cnt >= kf, np.int32(0), _INT_MIN)

    def bit_step(i, lo):
        cand = lo | (np.int32(1) << (30 - i))
        cnt = _count(u_ref, n_groups, group, cand, ge)
        return jnp.where(cnt >= kf, cand, lo)

    lo = lax.fori_loop(0, 31, bit_step, lo)
    c_gt = _count(u_ref, n_groups, group, lo, lambda u, t: u > t)
    c_eq = _count(u_ref, n_groups, group, lo, lambda u, t: u == t)
    need = kf - c_gt
    real = lo > _KEY_NEG_INF
    excess = jnp.where(real & (c_eq > need), 1.0, 0.0)
    any_excess = jnp.max(excess) > 0.0

    @pl.when(jnp.logical_not(any_excess))
    def _():
        def emit(t, carry):
            u = u_ref[t]
            write_tile(t, jnp.where((u >= lo) & (u > _KEY_NEG_INF), 1.0, 0.0))
            return carry

        lax.fori_loop(0, n_live, emit, 0)

    @pl.when(any_excess)
    def _():
        upper = (lax.broadcasted_iota(I32, (LANES, LANES), 0)
                 <= lax.broadcasted_iota(I32, (LANES, LANES), 1)).astype(BF)

        def emit(t, seen):
            u = u_ref[t]
            eq = jnp.where(u == lo, 1.0, 0.0)
            prefix = seen + _dot(eq.astype(BF), upper)
            take = (u > lo) | ((u == lo) & (prefix <= need))
            write_tile(t, jnp.where(take & (u > _KEY_NEG_INF), 1.0, 0.0))
            return seen + jnp.sum(eq, axis=-1, keepdims=True)

        lax.fori_loop(0, n_live, emit, jnp.zeros((rows, 1), F32))

    def clear(t, carry):
        write_tile(t, jnp.zeros((rows, LANES), F32))
        return carry

    lax.fori_loop(n_live, n_tiles, clear, 0)


IDX_KCHUNK = 512
IDX_QROWS = 256


def _indexer_prompt_body(qi_ref, tailq_ref, tail_ref, mask_ref, score_ref, u_ref, *, seq):
    j = pl.program_id(1)
    rows = IDX_QROWS
    qblocks = rows // LANES
    n_tiles = seq // LANES
    tiles_per_chunk = IDX_KCHUNK // LANES
    qi = qi_ref[0].astype(BF)
    w = tailq_ref[0][:, IDX_DIM:IDX_DIM + IDX_HEADS] * (IDX_DIM ** -0.5 * IDX_HEADS ** -0.5)

    n_chunks = (j * rows + rows - 1) // IDX_KCHUNK + 1
    q_pos = j * rows + lax.broadcasted_iota(I32, (rows, IDX_KCHUNK), 0)

    def chunk_step(ci, carry):
        k0 = pl.multiple_of(ci * IDX_KCHUNK, IDX_KCHUNK)
        kic = tail_ref[0, pl.ds(k0, IDX_KCHUNK), :][:, 0:IDX_DIM].astype(BF)
        acc = jnp.zeros((rows, IDX_KCHUNK), F32)
        for h in range(IDX_HEADS):
            s = _dot_nt(qi[:, h * IDX_DIM:(h + 1) * IDX_DIM], kic)
            acc = acc + w[:, h:h + 1] * jnp.maximum(s, 0.0)
        k_pos = k0 + lax.broadcasted_iota(I32, (rows, IDX_KCHUNK), 1)
        acc = jnp.where(k_pos <= q_pos, acc, -jnp.inf)
        for i in range(tiles_per_chunk):
            score_ref[ci * tiles_per_chunk + i] = acc[:, i * LANES:(i + 1) * LANES]
        return carry

    lax.fori_loop(0, n_chunks, chunk_step, 0)

    def write_tile(t, m):
        mb = m.astype(BF)
        for qb in range(qblocks):
            mask_ref[0, qb, t] = mb[qb * LANES:(qb + 1) * LANES]

    _topk_mask(score_ref, u_ref, write_tile, n_tiles, n_chunks, tiles_per_chunk, TOPK)


def indexer_prompt(z3, k):
    bsz, seq = z3.shape[:2]
    nqb = seq // LANES
    rows = IDX_QROWS
    tail_block = z3.shape[2] // LANES - 1
    assert k == TOPK and seq % rows == 0 and seq % IDX_KCHUNK == 0
    return pl.pallas_call(
        functools.partial(_indexer_prompt_body, seq=seq),
        grid=(bsz, seq // rows),
        in_specs=[pl.BlockSpec((1, rows, IDX_HEADS * IDX_DIM), lambda b, j: (b, j, 7)),
                  pl.BlockSpec((1, rows, LANES), lambda b, j: (b, j, tail_block)),
                  pl.BlockSpec((1, seq, LANES), lambda b, j: (b, 0, tail_block))],
        out_specs=pl.BlockSpec((1, rows // LANES, nqb, LANES, LANES), lambda b, j: (b, j, 0, 0, 0)),
        out_shape=jax.ShapeDtypeStruct((bsz, nqb, nqb, LANES, LANES), BF),
        scratch_shapes=[pltpu.VMEM((nqb, rows, LANES), F32),
                        pltpu.VMEM((nqb, rows, LANES), I32)],
        compiler_params=_cparams(("parallel", "arbitrary")),
        name="indexer_prompt",
    )(z3, z3, z3)


ATTN_KSTEP = 512
ATTN_TILES = ATTN_KSTEP // LANES


def _attn_prompt_body(bfar_ref, q_ref, k_ref, v_ref, mask_ref, bias_ref, o_ref, m_ref, l_ref, acc_ref):
    j = pl.program_id(1)
    scale = HEAD_DIM ** -0.5
    q = q_ref[0].astype(BF)
    m_ref[...] = jnp.full(m_ref.shape, NEG_BIG, F32)
    l_ref[...] = jnp.zeros(l_ref.shape, F32)
    acc_ref[...] = jnp.zeros(acc_ref.shape, F32)

    def process(sb, near):
        k0 = pl.multiple_of(sb * ATTN_KSTEP, ATTN_KSTEP)
        kblk = k_ref[0, pl.ds(k0, ATTN_KSTEP), :]
        vblk = v_ref[0, pl.ds(k0, ATTN_KSTEP), :]
        tiles = [sb * ATTN_TILES + i for i in range(ATTN_TILES)]
        sel = jnp.concatenate([mask_ref[0, 0, t] for t in tiles], axis=1).astype(F32) > 0.0
        if near:
            bidx = [jnp.where(j - t == 0, 0, jnp.where(j - t == 1, 1, 2)) for t in tiles]
        for h in range(HEADS):
            cols = slice(h * HEAD_DIM, (h + 1) * HEAD_DIM)
            lg = _dot_nt(q[:, cols], kblk[:, cols]) * scale
            if near:
                lg = lg + jnp.concatenate([bias_ref[bi, h] for bi in bidx], axis=1)
            else:
                lg = lg + bfar_ref[h]
            lg = jnp.where(sel, lg, NEG_BIG)
            m_old = m_ref[h]
            m_new = jnp.maximum(m_old, jnp.max(lg, axis=-1, keepdims=True))
            p = jnp.exp(lg - jnp.concatenate([m_new] * ATTN_TILES, axis=1))
            alpha = jnp.exp(m_old - m_new)
            l_ref[h] = alpha * l_ref[h] + jnp.sum(p, axis=-1, keepdims=True)
            acc_ref[:, cols] = alpha * acc_ref[:, cols] + _dot(p.astype(BF), vblk[:, cols])
            m_ref[h] = m_new

    n_far = jnp.maximum((j - 1) // ATTN_TILES, 0)

    def far_step(sb, carry):
        process(sb, False)
        return carry

    def near_step(sb, carry):
        process(sb, True)
        return carry

    lax.fori_loop(0, n_far, far_step, 0)
    lax.fori_loop(n_far, j // ATTN_TILES + 1, near_step, 0)

    for h in range(HEADS):
        cols = slice(h * HEAD_DIM, (h + 1) * HEAD_DIM)
        o_ref[0, :, cols] = acc_ref[:, cols] / l_ref[h]


def attn_prompt(z3, kv_bf, mask, bias_tiles, bias_far):
    bsz, seq = z3.shape[:2]
    nqb = seq // LANES
    width = HEADS * HEAD_DIM
    assert seq % ATTN_KSTEP == 0
    return pl.pallas_call(
        _attn_prompt_body,
        grid=(bsz, nqb),
        in_specs=[pl.BlockSpec(memory_space=pltpu.SMEM),
                  pl.BlockSpec((1, LANES, width), lambda b, j: (b, j, 4)),
                  pl.BlockSpec((1, seq, width), lambda b, j: (b, 0, 0)),
                  pl.BlockSpec((1, seq, width), lambda b, j: (b, 0, 1)),
                  pl.BlockSpec((1, 1, nqb, LANES, LANES), lambda b, j: (b, j, 0, 0, 0)),
                  pl.BlockSpec(bias_tiles.shape, lambda b, j: (0, 0, 0, 0))],
        out_specs=pl.BlockSpec((1, LANES, width), lambda b, j: (b, j, 0)),
        out_shape=jax.ShapeDtypeStruct((bsz, seq, width), F32),
        scratch_shapes=[pltpu.VMEM((HEADS, LANES, LANES), F32),
                        pltpu.VMEM((HEADS, LANES, LANES), F32),
                        pltpu.VMEM((LANES, width), F32)],
        compiler_params=_cparams(("parallel", "arbitrary")),
        name="attn_prompt",
    )(bias_far, z3, kv_bf, kv_bf, mask, bias_tiles)


def _indexer_sample_body(pt_ref, qi_ref, wcol_ref, kinew_ref, *rest, n_pages):
    ki_refs = rest[:n_pages]
    out_ref = rest[n_pages]
    qi = qi_ref[0].astype(BF)
    w = wcol_ref[0] * (IDX_DIM ** -0.5 * IDX_HEADS ** -0.5)
    for i in range(n_pages):
        s = _dot_nt(qi, ki_refs[i][0].astype(BF))
        out_ref[i, 0] = jnp.sum(w * jnp.maximum(s, 0.0), axis=0, keepdims=True)
    kn = kinew_ref[0].astype(BF).astype(F32)
    sn = jnp.sum(qi.astype(F32) * kn, axis=-1, keepdims=True)
    new = jnp.sum(w * jnp.maximum(sn, 0.0), axis=0, keepdims=True)
    lane = lax.broadcasted_iota(I32, (1, LANES), 1)
    out_ref[n_pages, 0] = jnp.where(lane == 0, new, -jnp.inf)


def indexer_sample(page_table, qi3, wcol, ki_new, ki_pool):
    bsz, n_pages = page_table.shape
    ki_spec = lambda i: pl.BlockSpec((1, PAGE, IDX_DIM), lambda b, pt, i=i: (pt[b, i], 0, 0))
    grid_spec = pltpu.PrefetchScalarGridSpec(
        num_scalar_prefetch=1,
        grid=(bsz,),
        in_specs=[pl.BlockSpec((1, IDX_HEADS, IDX_DIM), lambda b, pt: (b, 0, 0)),
                  pl.BlockSpec((1, IDX_HEADS, 1), lambda b, pt: (b, 0, 0)),
                  pl.BlockSpec((1, 1, IDX_DIM), lambda b, pt: (b, 0, 0))]
                 + [ki_spec(i) for i in range(n_pages)],
        out_specs=pl.BlockSpec((n_pages + 1, 1, 1, LANES), lambda b, pt: (0, b, 0, 0)),
    )
    out = pl.pallas_call(
        functools.partial(_indexer_sample_body, n_pages=n_pages),
        grid_spec=grid_spec,
        out_shape=jax.ShapeDtypeStruct((n_pages + 1, bsz, 1, LANES), F32),
        compiler_params=_cparams(("arbitrary",)),
        name="indexer_sample",
    )(page_table, qi3, wcol, ki_new, *([ki_pool] * n_pages))
    return out.reshape(n_pages + 1, bsz, LANES)


def _select_body(score_ref, mask_ref, u_ref, *, n_tiles, k):
    def write_tile(t, m):
        mask_ref[t] = m

    _topk_mask(score_ref, u_ref, write_tile, n_tiles, n_tiles, 1, k)


def select_topk(scores, k):
    n_tiles, rows, _ = scores.shape
    return pl.pallas_call(
        functools.partial(_select_body, n_tiles=n_tiles, k=k),
        grid=(1,),
        in_specs=[pl.BlockSpec(scores.shape, lambda i: (0, 0, 0))],
        out_specs=pl.BlockSpec(scores.shape, lambda i: (0, 0, 0)),
        out_shape=jax.ShapeDtypeStruct(scores.shape, F32),
        scratch_shapes=[pltpu.VMEM(scores.shape, I32)],
        compiler_params=_cparams(("arbitrary",)),
        name="select_topk",
    )(scores)


SC_CORES = 2
SC_SUBCORES = 16
SC_GATHER_CHUNK = 32


def sc_gather_rows(tables, idx):
    n_rows = idx.shape[0]
    workers = SC_CORES * SC_SUBCORES
    per_worker = n_rows // workers
    chunk = SC_GATHER_CHUNK
    assert n_rows % workers == 0 and per_worker % chunk == 0 and chunk % 8 == 0
    row_shape = tables[0].shape[1:]
    n_tab = len(tables)
    mesh = plsc.VectorSubcoreMesh(core_axis_name="c", subcore_axis_name="s",
                                  num_cores=SC_CORES, num_subcores=SC_SUBCORES)

    def body(*refs):
        tab_refs = refs[:n_tab]
        idx_hbm = refs[n_tab]
        out_refs = refs[n_tab + 1:2 * n_tab + 1]
        idx_v, rows_v, sem = refs[2 * n_tab + 1:]
        wid = lax.axis_index("s") * SC_CORES + lax.axis_index("c")

        @pl.loop(0, per_worker // chunk)
        def _(ci):
            off = pl.multiple_of(wid * per_worker + ci * chunk, 8)
            pltpu.sync_copy(idx_hbm.at[pl.ds(off, chunk)], idx_v)
            for tab, out in zip(tab_refs, out_refs):
                pltpu.async_copy(tab.at[idx_v], rows_v, sem).wait()
                pltpu.sync_copy(rows_v, out.at[pl.ds(off, chunk)])

    return pl.kernel(
        body,
        out_type=[jax.ShapeDtypeStruct((n_rows, *row_shape), t.dtype) for t in tables],
        mesh=mesh,
        scratch_types=[pltpu.VMEM((chunk,), I32),
                       pltpu.VMEM((chunk, *row_shape), tables[0].dtype),
                       pltpu.SemaphoreType.DMA],
        compiler_params=pltpu.CompilerParams(use_tc_tiling_on_sc=True),
        name="sc_gather_rows",
    )(*tables, idx)


SAMPLE_SEQ_GROUP = 4


def _attn_compact_body(dist_ref, near_ref, q_ref, kn_ref, vn_ref, btab_ref, kc_ref, vc_ref,
                       o_ref, bbuf):
    step = pl.program_id(0)
    scale = HEAD_DIM ** -0.5
    ones = jnp.ones((HEAD_DIM, LANES), BF)
    far_bias = btab_ref[REL_MAX_DIST]
    last = lax.broadcasted_iota(I32, (TOPK, 1, 1), 0) == TOPK - 1
    for g in range(SAMPLE_SEQ_GROUP):
        b = step * SAMPLE_SEQ_GROUP + g
        bbuf[g] = jnp.broadcast_to(far_bias[None], (TOPK, HEADS, LANES))

        def fill(t, carry, g=g, b=b):
            bbuf[g, t] = btab_ref[dist_ref[b, t]]
            return carry

        lax.fori_loop(near_ref[b], TOPK, fill, 0)

        is_new = dist_ref[b, TOPK - 1] == 0
        rows = slice(g * TOPK, (g + 1) * TOPK)
        kc = jnp.where(last & is_new, kn_ref[g][None], kc_ref[rows])
        vc = jnp.where(last & is_new, vn_ref[g][None], vc_ref[rows])
        prod = (kc * q_ref[g][None]).reshape(TOPK * HEADS, HEAD_DIM)
        hi = prod.astype(BF)
        lo = (prod - hi.astype(F32)).astype(BF)
        lg = (_dot(hi, ones) + _dot(lo, ones)).reshape(TOPK, HEADS, LANES) * scale + bbuf[g]
        m = jnp.max(lg, axis=0, keepdims=True)
        p = jnp.exp(lg - m)
        l = jnp.sum(p, axis=0)
        o_ref[g] = jnp.sum(p * vc, axis=0) / l


def attn_sample_compact(dist, first_near, zs3, k_sel, v_sel, bias_by_dist):
    bsz = dist.shape[0]
    g = SAMPLE_SEQ_GROUP
    assert bsz % g == 0
    zspec = lambda cb: pl.BlockSpec((g, HEADS, HEAD_DIM), lambda i, d, n, cb=cb: (i, cb, 0))
    sel_spec = pl.BlockSpec((g * TOPK, HEADS, HEAD_DIM), lambda i, d, n: (i, 0, 0))
    grid_spec = pltpu.PrefetchScalarGridSpec(
        num_scalar_prefetch=2,
        grid=(bsz // g,),
        in_specs=[zspec(4), zspec(5), zspec(6),
                  pl.BlockSpec(bias_by_dist.shape, lambda i, d, n: (0, 0, 0)),
                  sel_spec, sel_spec],
        out_specs=pl.BlockSpec((g, HEADS, HEAD_DIM), lambda i, d, n: (i, 0, 0)),
        scratch_shapes=[pltpu.VMEM((g, TOPK, HEADS, HEAD_DIM), F32)],
    )
    return pl.pallas_call(
        _attn_compact_body,
        grid_spec=grid_spec,
        out_shape=jax.ShapeDtypeStruct((bsz, HEADS, HEAD_DIM), F32),
        compiler_params=_cparams(("arbitrary",)),
        name="attn_sample_compact",
    )(dist, first_near, zs3, zs3, zs3, bias_by_dist, k_sel, v_sel)


def _bucket_table(max_dist):
    exact = REL_BUCKETS // 2
    d = np.arange(max_dist + 1)
    df = np.maximum(d, 1).astype(np.float32)
    far = exact + (np.log(df / exact) / np.float32(math.log(REL_MAX_DIST / exact))
                   * (REL_BUCKETS - exact)).astype(np.int32)
    return np.where(d < exact, d, np.minimum(far, REL_BUCKETS - 1)).astype(np.int32)


def _bias_tables(rel_bias):
    tab = _bucket_table(2 * LANES)
    assert np.all(tab[REL_MAX_DIST:] == REL_BUCKETS - 1)
    i = np.arange(LANES)
    dist0 = np.maximum(i[:, None] - i[None, :], 0)
    dist1 = LANES + i[:, None] - i[None, :]
    far = np.full((LANES, LANES), REL_BUCKETS - 1)
    idx = np.stack([tab[dist0], tab[dist1], far])
    tiles = jnp.transpose(rel_bias[idx], (0, 3, 1, 2))
    by_dist = jnp.broadcast_to(rel_bias[tab[:REL_MAX_DIST + 1]][:, :, None],
                               (REL_MAX_DIST + 1, HEADS, LANES))
    return tiles.astype(F32), rel_bias[REL_BUCKETS - 1].astype(F32), by_dist.astype(F32)


def _conv_prompt_body(bg_ref, cg_ref, xt_ref, w_ref, v_ref, st_ref, carry_ref, *, tblock):
    t = pl.program_id(1)

    @pl.when(t == 0)
    def _():
        carry_ref[...] = jnp.zeros_like(carry_ref)

    u = cg_ref[0] * xt_ref[0]
    row = lax.broadcasted_iota(I32, u.shape, 0)
    c0 = carry_ref[0:1]
    c1 = carry_ref[1:2]
    u1 = jnp.where(row == 0, c1, pltpu.roll(u, 1, axis=0))
    u2 = jnp.where(row == 0, c0, jnp.where(row == 1, c1, pltpu.roll(u, 2, axis=0)))
    conv = w_ref[0:1] * u2 + w_ref[1:2] * u1 + w_ref[2:3] * u
    v_ref[0] = bg_ref[0] * conv
    last = u[tblock - 2:tblock]
    carry_ref[0:2] = last

    @pl.when(t == pl.num_programs(1) - 1)
    def _():
        st_ref[0] = last


def conv_prompt(zc3, w_conv, tblock=256):
    bsz, seq = zc3.shape[:2]
    c = zc3.shape[2] // 3
    zspec = lambda cb: pl.BlockSpec((1, tblock, c), lambda b, t, cb=cb: (b, t, cb))
    return pl.pallas_call(
        functools.partial(_conv_prompt_body, tblock=tblock),
        grid=(bsz, seq // tblock),
        in_specs=[zspec(0), zspec(1), zspec(2), pl.BlockSpec((3, c), lambda b, t: (0, 0))],
        out_specs=[pl.BlockSpec((1, tblock, c), lambda b, t: (b, t, 0)),
                   pl.BlockSpec((1, 2, c), lambda b, t: (b, 0, 0))],
        out_shape=[jax.ShapeDtypeStruct((bsz, seq, c), F32),
                   jax.ShapeDtypeStruct((bsz, 2, c), F32)],
        scratch_shapes=[pltpu.VMEM((8, c), F32)],
        compiler_params=_cparams(("parallel", "arbitrary")),
        name="conv_prompt",
    )(zc3, zc3, zc3, w_conv)


def _conv_sample_body(bg_ref, cg_ref, xt_ref, w_ref, s0_ref, s1_ref, v_ref, n0_ref, n1_ref):
    u = cg_ref[...] * xt_ref[...]
    conv = w_ref[0:1] * s0_ref[...] + w_ref[1:2] * s1_ref[...] + w_ref[2:3] * u
    v_ref[...] = bg_ref[...] * conv
    n0_ref[...] = s1_ref[...]
    n1_ref[...] = u


def conv_sample(zc, w_conv, s0, s1):
    bsz = zc.shape[0]
    c = zc.shape[1] // 3
    zspec = lambda cb: pl.BlockSpec((bsz, c), lambda i, cb=cb: (0, cb))
    full = pl.BlockSpec((bsz, c), lambda i: (0, 0))
    return pl.pallas_call(
        _conv_sample_body,
        grid=(1,),
        in_specs=[zspec(0), zspec(1), zspec(2), pl.BlockSpec((3, c), lambda i: (0, 0)), full, full],
        out_specs=[full, full, full],
        out_shape=[jax.ShapeDtypeStruct((bsz, c), F32)] * 3,
        compiler_params=_cparams(("arbitrary",)),
        name="conv_sample",
    )(zc, zc, zc, w_conv, s0, s1)


def _router_body(x_ref, g_ref, wr_ref, br_ref, h_ref, route_ref):
    x = x_ref[...]
    ms = jnp.mean(x * x, axis=-1, keepdims=True)
    h = x * lax.rsqrt(ms + RMS_EPS) * g_ref[...]
    hb = h.astype(BF)
    h_ref[...] = hb
    logits = _dot(hb, wr_ref[...]) + br_ref[...]
    lane = lax.broadcasted_iota(I32, logits.shape, 1)
    big = np.int32(1 << 20)
    lg = jnp.where(lane < N_GROUPS, logits, -jnp.inf)
    g_max = jnp.max(lg, axis=-1, keepdims=True)
    g_idx = jnp.min(jnp.where(lg == g_max, lane, big), axis=-1, keepdims=True)
    g_w = 1.0 / jnp.sum(jnp.exp(lg - g_max), axis=-1, keepdims=True)
    first = N_GROUPS + EXP_PER_GROUP * g_idx
    le = jnp.where((lane >= first) & (lane < first + EXP_PER_GROUP), logits, -jnp.inf)
    l1 = jnp.max(le, axis=-1, keepdims=True)
    i1 = jnp.min(jnp.where(le == l1, lane, big), axis=-1, keepdims=True)
    le2 = jnp.where(lane == i1, -jnp.inf, le)
    l2 = jnp.max(le2, axis=-1, keepdims=True)
    i2 = jnp.min(jnp.where(le2 == l2, lane, big), axis=-1, keepdims=True)
    r = jnp.exp(l2 - l1)
    w1 = g_w / (1.0 + r)
    w2 = g_w * r / (1.0 + r)
    e1 = (i1 - N_GROUPS).astype(F32)
    e2 = (i2 - N_GROUPS).astype(F32)
    route_ref[...] = jnp.where(lane == 0, e1, jnp.where(lane == 1, e2,
                               jnp.where(lane == 2, w1, jnp.where(lane == 3, w2, 0.0))))


def moe_router(x, g, wr, br, tm):
    m, k = x.shape
    return pl.pallas_call(
        _router_body,
        grid=(m // tm,),
        in_specs=[pl.BlockSpec((tm, k), lambda i: (i, 0)),
                  pl.BlockSpec((1, k), lambda i: (0, 0)),
                  pl.BlockSpec((k, LANES), lambda i: (0, 0)),
                  pl.BlockSpec((1, LANES), lambda i: (0, 0))],
        out_specs=[pl.BlockSpec((tm, k), lambda i: (i, 0)),
                   pl.BlockSpec((tm, LANES), lambda i: (i, 0))],
        out_shape=[jax.ShapeDtypeStruct((m, k), BF),
                   jax.ShapeDtypeStruct((m, LANES), F32)],
        compiler_params=_cparams(("parallel",)),
        name="moe_router",
    )(x, g.reshape(1, k), wr, br)


def _experts_body(te_ref, nu_ref, h_ref, wg_ref, wu_ref, wd_ref, o_ref, wg_bf, wu_bf, wd_bf):
    i = pl.program_id(0)

    @pl.when((i == 0) | (te_ref[i] != te_ref[jnp.maximum(i - 1, 0)]))
    def _():
        wg_bf[...] = wg_ref[0].astype(BF)
        wu_bf[...] = wu_ref[0].astype(BF)
        wd_bf[...] = wd_ref[0].astype(BF)

    @pl.when(i < nu_ref[0])
    def _():
        h = h_ref[...]
        a = _dot(h, wg_bf[...])
        b = _dot(h, wu_bf[...])
        hid = a * _sigmoid(a) * b
        o_ref[...] = _dot(hid.astype(BF), wd_bf[...])

    @pl.when(i >= nu_ref[0])
    def _():
        o_ref[...] = jnp.zeros_like(o_ref)


def moe_experts(tile_expert, n_used, hs, wg, wu, wd):
    p, k = hs.shape
    f = wg.shape[2]
    n_tiles = p // MOE_TILE
    grid_spec = pltpu.PrefetchScalarGridSpec(
        num_scalar_prefetch=2,
        grid=(n_tiles,),
        in_specs=[pl.BlockSpec((MOE_TILE, k), lambda i, te, nu: (i, 0)),
                  pl.BlockSpec((1, k, f), lambda i, te, nu: (te[i], 0, 0)),
                  pl.BlockSpec((1, k, f), lambda i, te, nu: (te[i], 0, 0)),
                  pl.BlockSpec((1, f, k), lambda i, te, nu: (te[i], 0, 0))],
        out_specs=pl.BlockSpec((MOE_TILE, k), lambda i, te, nu: (i, 0)),
        scratch_shapes=[pltpu.VMEM((k, f), BF), pltpu.VMEM((k, f), BF), pltpu.VMEM((f, k), BF)],
    )
    return pl.pallas_call(
        _experts_body,
        grid_spec=grid_spec,
        out_shape=jax.ShapeDtypeStruct((p, k), F32),
        compiler_params=_cparams(("arbitrary",)),
        name="moe_experts",
    )(tile_expert, n_used, hs, wg, wu, wd)


def _rank_within_expert(onehot):
    n, e = onehot.shape
    blk = LANES
    assert n % blk == 0
    oh = onehot.astype(F32).reshape(n // blk, blk, e)
    strict = jnp.asarray(np.tril(np.ones((blk, blk), np.float32), -1))
    within = jnp.einsum("ij,bjk->bik", strict, oh, precision=lax.Precision.HIGHEST)
    totals = jnp.sum(oh, axis=1)
    before = jnp.cumsum(totals, axis=0) - totals
    rank = (within + before[:, None, :]).reshape(n, e)
    return jnp.sum(rank * onehot.astype(F32), axis=1).astype(I32), jnp.sum(totals, axis=0).astype(I32)


def hier_moe(xs, tms, g, wrg, brg, wre, bre, wg, wu, wd, layer):
    d = xs[0].shape[1]
    m = sum(x.shape[0] for x in xs)
    wr = jnp.zeros((d, LANES), F32)
    wr = wr.at[:, :N_GROUPS].set(wrg).at[:, N_GROUPS:N_GROUPS + N_EXPERTS].set(wre.reshape(d, N_EXPERTS))
    br = jnp.zeros((1, LANES), F32)
    br = br.at[0, :N_GROUPS].set(brg).at[0, N_GROUPS:N_GROUPS + N_EXPERTS].set(bre.reshape(N_EXPERTS))
    routed = [moe_router(x, g, wr.astype(BF), br, tm) for x, tm in zip(xs, tms)]
    h_bf = jnp.concatenate([r[0] for r in routed], axis=0)
    route = jnp.concatenate([r[1] for r in routed], axis=0)

    eid = route[:, 0:2].astype(I32).reshape(-1)
    onehot = eid[:, None] == jnp.arange(N_EXPERTS, dtype=I32)[None, :]
    rank, counts = _rank_within_expert(onehot)
    padded = ((counts + MOE_TILE - 1) // MOE_TILE) * MOE_TILE
    ends = jnp.cumsum(padded)
    pos = jnp.sum(jnp.where(onehot, (ends - padded)[None, :], 0), axis=1) + rank
    n_rows = 2 * m + N_EXPERTS * MOE_TILE
    n_rows = -(-n_rows // MOE_TILE) * MOE_TILE
    token = jnp.zeros((n_rows,), I32).at[pos].set(jnp.arange(2 * m, dtype=I32) // 2)
    tile_start = jnp.arange(n_rows // MOE_TILE, dtype=I32) * MOE_TILE
    tile_expert = jnp.minimum(jnp.sum(tile_start[:, None] >= ends[None, :], axis=1),
                              N_EXPERTS - 1).astype(I32)
    n_used = (ends[-1] // MOE_TILE).astype(I32).reshape(1)

    hs = jnp.take(h_bf, token, axis=0)
    out = moe_experts(tile_expert + layer * N_EXPERTS, n_used, hs, wg, wu, wd)
    pos2 = pos.reshape(m, 2)
    res, r0 = [], 0
    for x in xs:
        rows = slice(r0, r0 + x.shape[0])
        res.append(x + route[rows, 2:3] * jnp.take(out, pos2[rows, 0], axis=0)
                   + route[rows, 3:4] * jnp.take(out, pos2[rows, 1], axis=0))
        r0 += x.shape[0]
    return res


def kernel(x_prompt, x_sample, cache_k, cache_v, cache_ki, state_hgrn, state_conv, page_table,
           norm_mix_g, norm_ffn_g, final_g, w_in_even, w_out_even, hgrn_lb_logits, hgrn_norm_g,
           rel_bias, w_in_conv, w_conv, w_out_conv, w_router_g, b_router_g, w_router_e,
           b_router_e, w_gate, w_up, w_down):
    bsz, seq, d = x_prompt.shape
    dec = x_sample.shape[0]
    n_p = bsz * seq
    tm_p, tm_s = 512, 128
    tms = (tm_p, tm_s)
    assert n_p % tm_p == 0 and dec % tm_s == 0 and x_sample.shape[1] == 1
    width = HEADS * HEAD_DIM
    even_in = w_in_even.shape[2]
    even_pad = -(-even_in // LANES) * LANES
    tail0 = even_pad - LANES
    n_pool = cache_k.shape[1]

    xp = x_prompt.reshape(n_p, d)
    xs = x_sample.reshape(dec, d)

    lbs = jnp.cumsum(jax.nn.softmax(hgrn_lb_logits.astype(F32), axis=0), axis=0)[:-1]
    bias_tiles, bias_far, bias_by_dist = _bias_tables(rel_bias)
    expert_w = [w.reshape(-1, *w.shape[2:]) for w in (w_gate, w_up, w_down)]

    w_in = jnp.pad(w_in_even[0], ((0, 0), (0, even_pad - even_in))).astype(BF)
    zp = norm_matmul(xp, norm_mix_g[0], w_in, 2 * tm_p, 640)
    zs = norm_matmul(xs, norm_mix_g[0], w_in, tm_s, 640)
    zp3 = zp.reshape(bsz, seq, even_pad)
    zs3 = zs.reshape(dec, even_pad // LANES, LANES)

    oa_p, hgrn_p = hgrn_prompt(zp3, lbs[0], hgrn_norm_g[0])
    oa_s, hgrn_s = hgrn_sample(zs3, lbs[0], hgrn_norm_g[0], state_hgrn[0])

    mask_p = indexer_prompt(zp3, TOPK)
    kv_bf = zp3[:, :, 5 * width:7 * width].astype(BF)
    ob_p = attn_prompt(zp3, kv_bf, mask_p, bias_tiles, bias_far)

    qi3 = zs[:, 7 * width:8 * width].reshape(dec, IDX_HEADS, IDX_DIM)
    ki_new = zs[:, tail0:tail0 + IDX_DIM].reshape(dec, 1, IDX_DIM)
    wcol = zs[:, tail0 + IDX_DIM:tail0 + IDX_DIM + IDX_HEADS].reshape(dec, IDX_HEADS, 1)
    pages = page_table + 0 * n_pool
    scores_s = indexer_sample(pages, qi3, wcol, ki_new,
                              cache_ki.reshape(-1, PAGE, IDX_DIM))
    mask_s = select_topk(scores_s, TOPK)
    sel = jnp.transpose(mask_s, (1, 0, 2))
    n_pages = page_table.shape[1]
    sel_off = jnp.concatenate([jnp.zeros((dec, 1), F32),
                               jnp.cumsum(jnp.sum(sel, axis=2), axis=1)], axis=1)
    slot = jnp.arange(TOPK, dtype=F32)
    page_of = jnp.sum(sel_off[:, None, 1:] <= slot[None, :, None], axis=2)
    page_1h = (page_of[:, :, None] == jnp.arange(n_pages + 1)[None, None, :]).astype(F32)
    local = slot[None, :] - jnp.einsum("brp,bp->br", page_1h, sel_off[:, :-1],
                                       precision=lax.Precision.HIGHEST)
    within = jnp.einsum("brp,bpl->brl", page_1h, jnp.cumsum(sel, axis=2),
                        precision=lax.Precision.HIGHEST)
    lane_of = jnp.sum(within <= local[:, :, None], axis=2)
    sel_idx = (page_of * PAGE + lane_of).astype(I32)
    past = n_pages * PAGE
    page_id = jnp.einsum("brp,bp->br", page_1h[:, :, :n_pages], pages.astype(F32),
                         precision=lax.Precision.HIGHEST).astype(I32)
    rows = jnp.where(page_of < n_pages, page_id * PAGE + lane_of, 0).astype(I32).reshape(-1)
    k_sel, v_sel = sc_gather_rows([cache_k.reshape(-1, HEADS, HEAD_DIM),
                                   cache_v.reshape(-1, HEADS, HEAD_DIM)], rows)
    dist = jnp.minimum(past - sel_idx, REL_MAX_DIST).astype(I32)
    first_near = jnp.sum(dist >= REL_MAX_DIST, axis=1).astype(I32)
    ob_s = attn_sample_compact(dist, first_near, zs3, k_sel, v_sel, bias_by_dist)

    w_out = w_out_even[0].astype(BF)
    w_out_ab = [w_out[:width], w_out[width:]]
    xp = matmul_residual([oa_p.reshape(n_p, width), ob_p.reshape(n_p, width)], w_out_ab, xp, 2 * tm_p, 512)
    xs = matmul_residual([oa_s.reshape(dec, width), ob_s.reshape(dec, width)], w_out_ab, xs, tm_s, 512)

    xp, xs = hier_moe([xp, xs], tms, norm_ffn_g[0], w_router_g[0], b_router_g[0], w_router_e[0],
                      b_router_e[0], *expert_w, 0)

    w_in_c = w_in_conv[0].astype(BF)
    zcp = norm_matmul(xp, norm_mix_g[1], w_in_c, 2 * tm_p, 1024)
    zcs = norm_matmul(xs, norm_mix_g[1], w_in_c, tm_s, 512)
    cw = zcp.shape[1] // 3
    v_p, conv_p = conv_prompt(zcp.reshape(bsz, seq, 3 * cw), w_conv[0])
    v_s, cs0, cs1 = conv_sample(zcs, w_conv[0], state_conv[0, :, 0], state_conv[0, :, 1])
    w_out_c = [w_out_conv[0].astype(BF)]
    xp = matmul_residual([v_p.reshape(n_p, cw)], w_out_c, xp, 2 * tm_p, 512)
    xs = matmul_residual([v_s], w_out_c, xs, tm_s, 512)

    xp, xs = hier_moe([xp, xs], tms, norm_ffn_g[1], w_router_g[1], b_router_g[1], w_router_e[1],
                      b_router_e[1], *expert_w, 1)

    yp = rmsnorm(xp, final_g, tm_p)
    ys = rmsnorm(xs, final_g, tm_s)

    kcol, vcol = 5 * width, 6 * width
    heads = lambda a, n: a.reshape(1, *n, HEADS, HEAD_DIM)
    return (yp.reshape(bsz, seq, d),
            ys.reshape(dec, 1, d),
            heads(zp[:, kcol:kcol + width], (bsz, seq)),
            heads(zp[:, vcol:vcol + width], (bsz, seq)),
            zp[:, tail0:tail0 + IDX_DIM].reshape(1, bsz, seq, IDX_DIM),
            hgrn_p[None],
            conv_p[None],
            heads(zs[:, kcol:kcol + width], (dec, 1)),
            heads(zs[:, vcol:vcol + width], (dec, 1)),
            zs[:, tail0:tail0 + IDX_DIM].reshape(1, dec, 1, IDX_DIM),
            hgrn_s[None],
            jnp.stack([cs0, cs1], axis=1)[None])
```

```python
import functools
import math

import numpy as np
import jax
import jax.numpy as jnp
from jax import lax
from jax.experimental import pallas as pl
from jax.experimental.pallas import tpu as pltpu
from jax.experimental.pallas import tpu_sc as plsc

F32 = jnp.float32
BF = jnp.bfloat16
I32 = jnp.int32

RMS_EPS = 1e-6
LANES = 128
NEG_BIG = -1e30
VMEM_LIMIT = 56 * 1024 * 1024

D_MODEL = 2048
HEADS = 8
HEAD_DIM = 128
IDX_HEADS = 16
IDX_DIM = 64
TOPK = 256
REL_BUCKETS = 32
REL_MAX_DIST = 128
N_EXPERTS = 16
EXP_PER_GROUP = 4
N_GROUPS = 4
D_EXPERT = 512
PAGE = 128

HGRN_CHUNK = 64
MOE_TILE = 256


def _cparams(sem):
    return pltpu.CompilerParams(dimension_semantics=sem, vmem_limit_bytes=VMEM_LIMIT)


def _dot(a, b):
    return jnp.dot(a, b, preferred_element_type=F32)


def _dot_nt(a, b):
    return lax.dot_general(a, b, (((1,), (1,)), ((), ())), preferred_element_type=F32)


def _dot_tn(a, b):
    return lax.dot_general(a, b, (((0,), (0,)), ((), ())), preferred_element_type=F32)


def _sigmoid(x):
    return 1.0 / (1.0 + jnp.exp(-x))


def _norm_mm_body(x_ref, g_ref, w_ref, o_ref, h_ref):
    @pl.when(pl.program_id(1) == 0)
    def _():
        x = x_ref[...]
        ms = jnp.mean(x * x, axis=-1, keepdims=True)
        h_ref[...] = (x * lax.rsqrt(ms + RMS_EPS) * g_ref[...]).astype(BF)

    o_ref[...] = _dot(h_ref[...], w_ref[...])


def norm_matmul(x, g, w_bf, tm, tn):
    m, k = x.shape
    n = w_bf.shape[1]
    return pl.pallas_call(
        _norm_mm_body,
        grid=(m // tm, n // tn),
        in_specs=[pl.BlockSpec((tm, k), lambda i, j: (i, 0)),
                  pl.BlockSpec((1, k), lambda i, j: (0, 0)),
                  pl.BlockSpec((k, tn), lambda i, j: (0, j))],
        out_specs=pl.BlockSpec((tm, tn), lambda i, j: (i, j)),
        out_shape=jax.ShapeDtypeStruct((m, n), F32),
        scratch_shapes=[pltpu.VMEM((tm, k), BF)],
        compiler_params=_cparams(("parallel", "arbitrary")),
        name="norm_matmul",
    )(x, g.reshape(1, k), w_bf)


def _mm_res_body(*refs, n_lhs):
    a_refs = refs[:n_lhs]
    w_refs = refs[n_lhs:2 * n_lhs]
    r_ref = refs[2 * n_lhs]
    o_ref = refs[2 * n_lhs + 1]
    s_refs = refs[2 * n_lhs + 2:]

    @pl.when(pl.program_id(1) == 0)
    def _():
        for a_ref, s_ref in zip(a_refs, s_refs):
            s_ref[...] = a_ref[...].astype(BF)

    acc = r_ref[...]
    for s_ref, w_ref in zip(s_refs, w_refs):
        acc = acc + _dot(s_ref[...], w_ref[...])
    o_ref[...] = acc


def matmul_residual(lhs, ws_bf, res, tm, tn):
    m, n = res.shape
    n_lhs = len(lhs)
    in_specs = ([pl.BlockSpec((tm, a.shape[1]), lambda i, j: (i, 0)) for a in lhs]
                + [pl.BlockSpec((w.shape[0], tn), lambda i, j: (0, j)) for w in ws_bf]
                + [pl.BlockSpec((tm, tn), lambda i, j: (i, j))])
    return pl.pallas_call(
        functools.partial(_mm_res_body, n_lhs=n_lhs),
        grid=(m // tm, n // tn),
        in_specs=in_specs,
        out_specs=pl.BlockSpec((tm, tn), lambda i, j: (i, j)),
        out_shape=jax.ShapeDtypeStruct((m, n), F32),
        scratch_shapes=[pltpu.VMEM((tm, a.shape[1]), BF) for a in lhs],
        compiler_params=_cparams(("parallel", "arbitrary")),
        name="matmul_residual",
    )(*lhs, *ws_bf, res)


def _rmsnorm_body(x_ref, g_ref, o_ref):
    x = x_ref[...]
    ms = jnp.mean(x * x, axis=-1, keepdims=True)
    o_ref[...] = x * lax.rsqrt(ms + RMS_EPS) * g_ref[...]


def rmsnorm(x, g, tm):
    m, k = x.shape
    return pl.pallas_call(
        _rmsnorm_body,
        grid=(m // tm,),
        in_specs=[pl.BlockSpec((tm, k), lambda i: (i, 0)),
                  pl.BlockSpec((1, k), lambda i: (0, 0))],
        out_specs=pl.BlockSpec((tm, k), lambda i: (i, 0)),
        out_shape=jax.ShapeDtypeStruct((m, k), F32),
        compiler_params=_cparams(("parallel",)),
        name="rmsnorm",
    )(x, g.reshape(1, k))


def _hgrn_static(c):
    levels = []
    m = 1
    while m < c:
        levels.append(m)
        m *= 2
    t = np.arange(c)
    rows = [t[None, :] <= t[:, None]]
    masks = [np.eye(c, dtype=bool)]
    for m in levels:
        blk = t // (2 * m)
        pos = t % (2 * m)
        bnd = blk * 2 * m + m - 1
        right = pos >= m
        left = pos < m
        e_rows = (t[None, :] > bnd[:, None]) & (t[None, :] <= t[:, None]) & right[:, None]
        f_rows = (t[None, :] > t[:, None]) & (t[None, :] <= bnd[:, None]) & left[:, None]
        rows.append(e_rows | f_rows)
        masks.append((blk[:, None] == blk[None, :]) & right[:, None] & left[None, :])
    m_all = np.stack(rows).astype(np.float32)
    masks = np.stack(masks).astype(np.float32)
    return m_all, masks, len(levels)


def _hgrn_gates(qa, fa, lb):
    f = lb + (1.0 - lb) * _sigmoid(fa)
    q = qa * _sigmoid(qa)
    return q, f


def _hgrn_prompt_body(qa_ref, fa_ref, ia_ref, ga_ref, lb_ref, gn_ref, mall_ref, masks_ref,
                      oa_ref, st_out_ref, st_ref, *, chunk, tblock, n_levels):
    c = chunk
    t_idx = pl.program_id(1)

    @pl.when(t_idx == 0)
    def _():
        st_ref[...] = jnp.zeros_like(st_ref)

    gn = gn_ref[...]

    def chunk_step(ci, carry):
        r0 = pl.multiple_of(ci * c, c)
        for h in range(HEADS):
            cols = slice(h * HEAD_DIM, (h + 1) * HEAD_DIM)
            qa = qa_ref[0, pl.ds(r0, c), cols]
            fa = fa_ref[0, pl.ds(r0, c), cols]
            ia = ia_ref[0, pl.ds(r0, c), cols]
            ga = ga_ref[0, pl.ds(r0, c), cols]
            q, f = _hgrn_gates(qa, fa, lb_ref[:, cols])
            k = 1.0 - f
            g = jnp.log(f)
            g_hi = g.astype(BF)
            g_lo = (g - g_hi.astype(F32)).astype(BF)
            b = _dot(mall_ref[0], g_hi) + _dot(mall_ref[0], g_lo)
            b_last = b[c - 1:c]
            st = st_ref[h]
            v = ia.astype(BF)
            o = _dot_nt((q * jnp.exp(b)).astype(BF), st.astype(BF))
            a = masks_ref[0] * _dot_nt(q.astype(BF), k.astype(BF))
            for li in range(n_levels):
                ml = mall_ref[1 + li]
                w = jnp.exp(_dot(ml, g_hi) + _dot(ml, g_lo))
                a = a + masks_ref[1 + li] * _dot_nt((q * w).astype(BF), (k * w).astype(BF))
            o = o + _dot(a.astype(BF), v)
            k_st = (k * jnp.exp(b_last - b)).astype(BF)
            st_ref[h] = st * jnp.exp(b_last) + _dot_tn(v, k_st)
            ms = jnp.mean(o * o, axis=-1, keepdims=True)
            y = o * lax.rsqrt(ms + RMS_EPS) * gn * (ga * _sigmoid(ga))
            oa_ref[0, pl.ds(r0, c), cols] = y
        return carry

    lax.fori_loop(0, tblock // c, chunk_step, 0)

    @pl.when(t_idx == pl.num_programs(1) - 1)
    def _():
        st_out_ref[0] = st_ref[...]


def hgrn_prompt(z3, lb, gn, tblock=256, chunk=HGRN_CHUNK):
    bsz, seq = z3.shape[:2]
    width = HEADS * HEAD_DIM
    m_all, masks, n_levels = _hgrn_static(chunk)
    zspec = lambda cb: pl.BlockSpec((1, tblock, width), lambda b, t, cb=cb: (b, t, cb))
    oa, st = pl.pallas_call(
        functools.partial(_hgrn_prompt_body, chunk=chunk, tblock=tblock, n_levels=n_levels),
        grid=(bsz, seq // tblock),
        in_specs=[zspec(0), zspec(1), zspec(2), zspec(3),
                  pl.BlockSpec((1, width), lambda b, t: (0, 0)),
                  pl.BlockSpec((1, HEAD_DIM), lambda b, t: (0, 0)),
                  pl.BlockSpec(m_all.shape, lambda b, t: (0, 0, 0)),
                  pl.BlockSpec(masks.shape, lambda b, t: (0, 0, 0))],
        out_specs=[pl.BlockSpec((1, tblock, width), lambda b, t: (b, t, 0)),
                   pl.BlockSpec((1, HEADS, HEAD_DIM, HEAD_DIM), lambda b, t: (b, 0, 0, 0))],
        out_shape=[jax.ShapeDtypeStruct((bsz, seq, width), F32),
                   jax.ShapeDtypeStruct((bsz, HEADS, HEAD_DIM, HEAD_DIM), F32)],
        scratch_shapes=[pltpu.VMEM((HEADS, HEAD_DIM, HEAD_DIM), F32)],
        compiler_params=_cparams(("parallel", "arbitrary")),
        name="hgrn_prompt",
    )(z3, z3, z3, z3, lb.reshape(1, width), gn.reshape(1, HEAD_DIM),
      jnp.asarray(m_all, BF), jnp.asarray(masks, F32))
    return oa, jnp.swapaxes(st, -1, -2)


def _col(row, eye):
    return jnp.sum(eye * row, axis=1, keepdims=True)


def _hgrn_sample_body(qa_ref, fa_ref, ia_ref, ga_ref, lb_ref, gn_ref, s_ref, oa_ref, so_ref):
    eye = (lax.broadcasted_iota(I32, (HEAD_DIM, HEAD_DIM), 0)
           == lax.broadcasted_iota(I32, (HEAD_DIM, HEAD_DIM), 1)).astype(F32)
    q8, f8 = _hgrn_gates(qa_ref[0], fa_ref[0], lb_ref[...])
    ga = ga_ref[0]
    gate = ga * _sigmoid(ga)
    outs = []
    for h in range(HEADS):
        f_col = _col(f8[h:h + 1], eye)
        q_col = _col(q8[h:h + 1], eye)
        s_new = f_col * s_ref[0, h] + (1.0 - f_col) * ia_ref[0, h:h + 1]
        so_ref[0, h] = s_new
        outs.append(jnp.sum(q_col * s_new, axis=0, keepdims=True))
    o = jnp.concatenate(outs, axis=0)
    ms = jnp.mean(o * o, axis=-1, keepdims=True)
    oa_ref[0] = o * lax.rsqrt(ms + RMS_EPS) * gn_ref[...] * gate


def hgrn_sample(zs3, lb, gn, s0):
    bsz = zs3.shape[0]
    zspec = lambda cb: pl.BlockSpec((1, HEADS, HEAD_DIM), lambda b, cb=cb: (b, cb, 0))
    sspec = pl.BlockSpec((1, HEADS, HEAD_DIM, HEAD_DIM), lambda b: (b, 0, 0, 0))
    return pl.pallas_call(
        _hgrn_sample_body,
        grid=(bsz,),
        in_specs=[zspec(0), zspec(1), zspec(2), zspec(3),
                  pl.BlockSpec((HEADS, HEAD_DIM), lambda b: (0, 0)),
                  pl.BlockSpec((1, HEAD_DIM), lambda b: (0, 0)),
                  sspec],
        out_specs=[pl.BlockSpec((1, HEADS, HEAD_DIM), lambda b: (b, 0, 0)), sspec],
        out_shape=[jax.ShapeDtypeStruct((bsz, HEADS, HEAD_DIM), F32),
                   jax.ShapeDtypeStruct(s0.shape, F32)],
        compiler_params=_cparams(("parallel",)),
        name="hgrn_sample",
    )(zs3, zs3, zs3, zs3, lb.reshape(HEADS, HEAD_DIM), gn.reshape(1, HEAD_DIM), s0)


_KEY_NEG_INF = np.int32(np.uint32(0x807FFFFF).astype(np.int64) - (1 << 32))
_INT_MIN = np.int32(-(1 << 31))


def _count(u_ref, n_groups, group, thr, cmp):
    rows = u_ref.shape[1]
    step = min(rows, LANES)
    parts = []
    for r0 in range(0, rows, step):
        t = jnp.broadcast_to(thr[r0:r0 + step], (step, LANES))

        def body(gi, acc, r0=r0, t=t):
            for i in range(group):
                acc = acc + cmp(u_ref[gi * group + i, r0:r0 + step], t).astype(F32)
            return acc

        acc = lax.fori_loop(0, n_groups, body, jnp.zeros((step, LANES), F32))
        parts.append(jnp.sum(acc, axis=-1, keepdims=True))
    return parts[0] if len(parts) == 1 else jnp.concatenate(parts, axis=0)


def _topk_mask(score_ref, u_ref, write_tile, n_tiles, n_groups, group, k):
    rows = score_ref.shape[1]
    n_live = n_groups * group

    def to_key(t, carry):
        bits = pltpu.bitcast(score_ref[t], I32)
        u_ref[t] = jnp.where(bits < 0, bits ^ np.int32(0x7FFFFFFF), bits)
        return carry

    lax.fori_loop(0, n_live, to_key, 0)

    kf = float(k)
    ge = lambda u, t: u >= t
    cnt = _count(u_ref, n_groups, group, jnp.zeros((rows, 1), I32), ge)
    lo = jnp.where(cnt >= kf, np.int32(0), _INT_MIN)

    def bit_step(i, lo):
        cand = lo | (np.int32(1) << (30 - i))
        cnt = _count(u_ref, n_groups, group, cand, ge)
        return jnp.where(cnt >= kf, cand, lo)

    lo = lax.fori_loop(0, 31, bit_step, lo)
    c_gt = _count(u_ref, n_groups, group, lo, lambda u, t: u > t)
    c_eq = _count(u_ref, n_groups, group, lo, lambda u, t: u == t)
    need = kf - c_gt
    real = lo > _KEY_NEG_INF
    excess = jnp.where(real & (c_eq > need), 1.0, 0.0)
    any_excess = jnp.max(excess) > 0.0

    @pl.when(jnp.logical_not(any_excess))
    def _():
        def emit(t, carry):
            u = u_ref[t]
            write_tile(t, jnp.where((u >= lo) & (u > _KEY_NEG_INF), 1.0, 0.0))
            return carry

        lax.fori_loop(0, n_live, emit, 0)

    @pl.when(any_excess)
    def _():
        upper = (lax.broadcasted_iota(I32, (LANES, LANES), 0)
                 <= lax.broadcasted_iota(I32, (LANES, LANES), 1)).astype(BF)

        def emit(t, seen):
            u = u_ref[t]
            eq = jnp.where(u == lo, 1.0, 0.0)
            prefix = seen + _dot(eq.astype(BF), upper)
            take = (u > lo) | ((u == lo) & (prefix <= need))
            write_tile(t, jnp.where(take & (u > _KEY_NEG_INF), 1.0, 0.0))
            return seen + jnp.sum(eq, axis=-1, keepdims=True)

        lax.fori_loop(0, n_live, emit, jnp.zeros((rows, 1), F32))

    def clear(t, carry):
        write_tile(t, jnp.zeros((rows, LANES), F32))
        return carry

    lax.fori_loop(n_live, n_tiles, clear, 0)


IDX_KCHUNK = 512
IDX_QROWS = 256


def _indexer_prompt_body(qi_ref, tailq_ref, tail_ref, mask_ref, score_ref, u_ref, *, seq):
    j = pl.program_id(1)
    rows = IDX_QROWS
    qblocks = rows // LANES
    n_tiles = seq // LANES
    tiles_per_chunk = IDX_KCHUNK // LANES
    qi = qi_ref[0].astype(BF)
    w = tailq_ref[0][:, IDX_DIM:IDX_DIM + IDX_HEADS] * (IDX_DIM ** -0.5 * IDX_HEADS ** -0.5)

    n_chunks = (j * rows + rows - 1) // IDX_KCHUNK + 1
    q_pos = j * rows + lax.broadcasted_iota(I32, (rows, IDX_KCHUNK), 0)

    def chunk_step(ci, carry):
        k0 = pl.multiple_of(ci * IDX_KCHUNK, IDX_KCHUNK)
        kic = tail_ref[0, pl.ds(k0, IDX_KCHUNK), :][:, 0:IDX_DIM].astype(BF)
        acc = jnp.zeros((rows, IDX_KCHUNK), F32)
        for h in range(IDX_HEADS):
            s = _dot_nt(qi[:, h * IDX_DIM:(h + 1) * IDX_DIM], kic)
            acc = acc + w[:, h:h + 1] * jnp.maximum(s, 0.0)
        k_pos = k0 + lax.broadcasted_iota(I32, (rows, IDX_KCHUNK), 1)
        acc = jnp.where(k_pos <= q_pos, acc, -jnp.inf)
        for i in range(tiles_per_chunk):
            score_ref[ci * tiles_per_chunk + i] = acc[:, i * LANES:(i + 1) * LANES]
        return carry

    lax.fori_loop(0, n_chunks, chunk_step, 0)

    def write_tile(t, m):
        mb = m.astype(BF)
        for qb in range(qblocks):
            mask_ref[0, qb, t] = mb[qb * LANES:(qb + 1) * LANES]

    _topk_mask(score_ref, u_ref, write_tile, n_tiles, n_chunks, tiles_per_chunk, TOPK)


def indexer_prompt(z3, k):
    bsz, seq = z3.shape[:2]
    nqb = seq // LANES
    rows = IDX_QROWS
    tail_block = z3.shape[2] // LANES - 1
    assert k == TOPK and seq % rows == 0 and seq % IDX_KCHUNK == 0
    return pl.pallas_call(
        functools.partial(_indexer_prompt_body, seq=seq),
        grid=(bsz, seq // rows),
        in_specs=[pl.BlockSpec((1, rows, IDX_HEADS * IDX_DIM), lambda b, j: (b, j, 7)),
                  pl.BlockSpec((1, rows, LANES), lambda b, j: (b, j, tail_block)),
                  pl.BlockSpec((1, seq, LANES), lambda b, j: (b, 0, tail_block))],
        out_specs=pl.BlockSpec((1, rows // LANES, nqb, LANES, LANES), lambda b, j: (b, j, 0, 0, 0)),
        out_shape=jax.ShapeDtypeStruct((bsz, nqb, nqb, LANES, LANES), BF),
        scratch_shapes=[pltpu.VMEM((nqb, rows, LANES), F32),
                        pltpu.VMEM((nqb, rows, LANES), I32)],
        compiler_params=_cparams(("parallel", "arbitrary")),
        name="indexer_prompt",
    )(z3, z3, z3)


ATTN_KSTEP = 512
ATTN_TILES = ATTN_KSTEP // LANES


def _attn_prompt_body(bfar_ref, q_ref, k_ref, v_ref, mask_ref, bias_ref, o_ref, m_ref, l_ref, acc_ref):
    j = pl.program_id(1)
    scale = HEAD_DIM ** -0.5
    q = q_ref[0].astype(BF)
    m_ref[...] = jnp.full(m_ref.shape, NEG_BIG, F32)
    l_ref[...] = jnp.zeros(l_ref.shape, F32)
    acc_ref[...] = jnp.zeros(acc_ref.shape, F32)

    def process(sb, near):
        k0 = pl.multiple_of(sb * ATTN_KSTEP, ATTN_KSTEP)
        kblk = k_ref[0, pl.ds(k0, ATTN_KSTEP), :]
        vblk = v_ref[0, pl.ds(k0, ATTN_KSTEP), :]
        tiles = [sb * ATTN_TILES + i for i in range(ATTN_TILES)]
        sel = jnp.concatenate([mask_ref[0, 0, t] for t in tiles], axis=1).astype(F32) > 0.0
        if near:
            bidx = [jnp.where(j - t == 0, 0, jnp.where(j - t == 1, 1, 2)) for t in tiles]
        for h in range(HEADS):
            cols = slice(h * HEAD_DIM, (h + 1) * HEAD_DIM)
            lg = _dot_nt(q[:, cols], kblk[:, cols]) * scale
            if near:
                lg = lg + jnp.concatenate([bias_ref[bi, h] for bi in bidx], axis=1)
            else:
                lg = lg + bfar_ref[h]
            lg = jnp.where(sel, lg, NEG_BIG)
            m_old = m_ref[h]
            m_new = jnp.maximum(m_old, jnp.max(lg, axis=-1, keepdims=True))
            p = jnp.exp(lg - jnp.concatenate([m_new] * ATTN_TILES, axis=1))
            alpha = jnp.exp(m_old - m_new)
            l_ref[h] = alpha * l_ref[h] + jnp.sum(p, axis=-1, keepdims=True)
            acc_ref[:, cols] = alpha * acc_ref[:, cols] + _dot(p.astype(BF), vblk[:, cols])
            m_ref[h] = m_new

    n_far = jnp.maximum((j - 1) // ATTN_TILES, 0)

    def far_step(sb, carry):
        process(sb, False)
        return carry

    def near_step(sb, carry):
        process(sb, True)
        return carry

    lax.fori_loop(0, n_far, far_step, 0)
    lax.fori_loop(n_far, j // ATTN_TILES + 1, near_step, 0)

    for h in range(HEADS):
        cols = slice(h * HEAD_DIM, (h + 1) * HEAD_DIM)
        o_ref[0, :, cols] = acc_ref[:, cols] / l_ref[h]


def attn_prompt(z3, kv_bf, mask, bias_tiles, bias_far):
    bsz, seq = z3.shape[:2]
    nqb = seq // LANES
    width = HEADS * HEAD_DIM
    assert seq % ATTN_KSTEP == 0
    return pl.pallas_call(
        _attn_prompt_body,
        grid=(bsz, nqb),
        in_specs=[pl.BlockSpec(memory_space=pltpu.SMEM),
                  pl.BlockSpec((1, LANES, width), lambda b, j: (b, j, 4)),
                  pl.BlockSpec((1, seq, width), lambda b, j: (b, 0, 0)),
                  pl.BlockSpec((1, seq, width), lambda b, j: (b, 0, 1)),
                  pl.BlockSpec((1, 1, nqb, LANES, LANES), lambda b, j: (b, j, 0, 0, 0)),
                  pl.BlockSpec(bias_tiles.shape, lambda b, j: (0, 0, 0, 0))],
        out_specs=pl.BlockSpec((1, LANES, width), lambda b, j: (b, j, 0)),
        out_shape=jax.ShapeDtypeStruct((bsz, seq, width), F32),
        scratch_shapes=[pltpu.VMEM((HEADS, LANES, LANES), F32),
                        pltpu.VMEM((HEADS, LANES, LANES), F32),
                        pltpu.VMEM((LANES, width), F32)],
        compiler_params=_cparams(("parallel", "arbitrary")),
        name="attn_prompt",
    )(bias_far, z3, kv_bf, kv_bf, mask, bias_tiles)


def _indexer_sample_body(pt_ref, qi_ref, wcol_ref, kinew_ref, *rest, n_pages):
    ki_refs = rest[:n_pages]
    out_ref = rest[n_pages]
    qi = qi_ref[0].astype(BF)
    w = wcol_ref[0] * (IDX_DIM ** -0.5 * IDX_HEADS ** -0.5)
    for i in range(n_pages):
        s = _dot(qi, ki_refs[i][0].astype(BF))
        out_ref[i, 0] = jnp.sum(w * jnp.maximum(s, 0.0), axis=0, keepdims=True)
    kn = kinew_ref[0].astype(BF).astype(F32)
    sn = jnp.sum(qi.astype(F32) * kn, axis=-1, keepdims=True)
    new = jnp.sum(w * jnp.maximum(sn, 0.0), axis=0, keepdims=True)
    lane = lax.broadcasted_iota(I32, (1, LANES), 1)
    out_ref[n_pages, 0] = jnp.where(lane == 0, new, -jnp.inf)


def indexer_sample(page_table, qi3, wcol, ki_new, ki_pool_t):
    bsz, n_pages = page_table.shape
    ki_spec = lambda i: pl.BlockSpec((1, IDX_DIM, PAGE), lambda b, pt, i=i: (pt[b, i], 0, 0))
    grid_spec = pltpu.PrefetchScalarGridSpec(
        num_scalar_prefetch=1,
        grid=(bsz,),
        in_specs=[pl.BlockSpec((1, IDX_HEADS, IDX_DIM), lambda b, pt: (b, 0, 0)),
                  pl.BlockSpec((1, IDX_HEADS, 1), lambda b, pt: (b, 0, 0)),
                  pl.BlockSpec((1, 1, IDX_DIM), lambda b, pt: (b, 0, 0))]
                 + [ki_spec(i) for i in range(n_pages)],
        out_specs=pl.BlockSpec((n_pages + 1, 1, 1, LANES), lambda b, pt: (0, b, 0, 0)),
    )
    out = pl.pallas_call(
        functools.partial(_indexer_sample_body, n_pages=n_pages),
        grid_spec=grid_spec,
        out_shape=jax.ShapeDtypeStruct((n_pages + 1, bsz, 1, LANES), F32),
        compiler_params=_cparams(("arbitrary",)),
        name="indexer_sample",
    )(page_table, qi3, wcol, ki_new, *([ki_pool_t] * n_pages))
    return out.reshape(n_pages + 1, bsz, LANES)


def _select_body(score_ref, mask_ref, u_ref, *, n_tiles, k):
    def write_tile(t, m):
        mask_ref[t] = m

    _topk_mask(score_ref, u_ref, write_tile, n_tiles, n_tiles, 1, k)


def select_topk(scores, k):
    n_tiles, rows, _ = scores.shape
    return pl.pallas_call(
        functools.partial(_select_body, n_tiles=n_tiles, k=k),
        grid=(1,),
        in_specs=[pl.BlockSpec(scores.shape, lambda i: (0, 0, 0))],
        out_specs=pl.BlockSpec(scores.shape, lambda i: (0, 0, 0)),
        out_shape=jax.ShapeDtypeStruct(scores.shape, F32),
        scratch_shapes=[pltpu.VMEM(scores.shape, I32)],
        compiler_params=_cparams(("arbitrary",)),
        name="select_topk",
    )(scores)


SC_CORES = 2
SC_SUBCORES = 16
SC_GATHER_CHUNK = 32


def sc_gather_rows(tables, idx):
    n_rows = idx.shape[0]
    workers = SC_CORES * SC_SUBCORES
    per_worker = n_rows // workers
    chunk = SC_GATHER_CHUNK
    assert n_rows % workers == 0 and per_worker % chunk == 0 and chunk % 8 == 0
    row_shape = tables[0].shape[1:]
    n_tab = len(tables)
    mesh = plsc.VectorSubcoreMesh(core_axis_name="c", subcore_axis_name="s",
                                  num_cores=SC_CORES, num_subcores=SC_SUBCORES)

    def body(*refs):
        tab_refs = refs[:n_tab]
        idx_hbm = refs[n_tab]
        out_refs = refs[n_tab + 1:2 * n_tab + 1]
        idx_v, rows_v, sem = refs[2 * n_tab + 1:]
        wid = lax.axis_index("s") * SC_CORES + lax.axis_index("c")

        @pl.loop(0, per_worker // chunk)
        def _(ci):
            off = pl.multiple_of(wid * per_worker + ci * chunk, 8)
            pltpu.sync_copy(idx_hbm.at[pl.ds(off, chunk)], idx_v)
            for tab, out in zip(tab_refs, out_refs):
                pltpu.async_copy(tab.at[idx_v], rows_v, sem).wait()
                pltpu.sync_copy(rows_v, out.at[pl.ds(off, chunk)])

    return pl.kernel(
        body,
        out_type=[jax.ShapeDtypeStruct((n_rows, *row_shape), t.dtype) for t in tables],
        mesh=mesh,
        scratch_types=[pltpu.VMEM((chunk,), I32),
                       pltpu.VMEM((chunk, *row_shape), tables[0].dtype),
                       pltpu.SemaphoreType.DMA],
        compiler_params=pltpu.CompilerParams(use_tc_tiling_on_sc=True),
        name="sc_gather_rows",
    )(*tables, idx)


SAMPLE_SEQ_GROUP = 4


def _attn_compact_body(dist_ref, near_ref, q_ref, kn_ref, vn_ref, btab_ref, kc_ref, vc_ref,
                       o_ref, bbuf):
    step = pl.program_id(0)
    scale = HEAD_DIM ** -0.5
    ones = jnp.ones((HEAD_DIM, LANES), BF)
    far_bias = btab_ref[REL_MAX_DIST]
    last = lax.broadcasted_iota(I32, (TOPK, 1, 1), 0) == TOPK - 1
    for g in range(SAMPLE_SEQ_GROUP):
        b = step * SAMPLE_SEQ_GROUP + g
        bbuf[g] = jnp.broadcast_to(far_bias[None], (TOPK, HEADS, LANES))

        def fill(t, carry, g=g, b=b):
            bbuf[g, t] = btab_ref[dist_ref[b, t]]
            return carry

        lax.fori_loop(near_ref[b], TOPK, fill, 0)

        is_new = dist_ref[b, TOPK - 1] == 0
        rows = slice(g * TOPK, (g + 1) * TOPK)
        rnd = lambda a: a.astype(BF).astype(F32)
        kc = rnd(jnp.where(last & is_new, kn_ref[g][None], kc_ref[rows]))
        vc = rnd(jnp.where(last & is_new, vn_ref[g][None], vc_ref[rows]))
        prod = (kc * rnd(q_ref[g])[None]).reshape(TOPK * HEADS, HEAD_DIM)
        hi = prod.astype(BF)
        lo = (prod - hi.astype(F32)).astype(BF)
        lg = (_dot(hi, ones) + _dot(lo, ones)).reshape(TOPK, HEADS, LANES) * scale + bbuf[g]
        m = jnp.max(lg, axis=0, keepdims=True)
        p = jnp.exp(lg - m)
        l = jnp.sum(p, axis=0)
        o_ref[g] = jnp.sum(rnd(p) * vc, axis=0) / l


def attn_sample_compact(dist, first_near, zs3, k_sel, v_sel, bias_by_dist):
    bsz = dist.shape[0]
    g = SAMPLE_SEQ_GROUP
    assert bsz % g == 0
    zspec = lambda cb: pl.BlockSpec((g, HEADS, HEAD_DIM), lambda i, d, n, cb=cb: (i, cb, 0))
    sel_spec = pl.BlockSpec((g * TOPK, HEADS, HEAD_DIM), lambda i, d, n: (i, 0, 0))
    grid_spec = pltpu.PrefetchScalarGridSpec(
        num_scalar_prefetch=2,
        grid=(bsz // g,),
        in_specs=[zspec(4), zspec(5), zspec(6),
                  pl.BlockSpec(bias_by_dist.shape, lambda i, d, n: (0, 0, 0)),
                  sel_spec, sel_spec],
        out_specs=pl.BlockSpec((g, HEADS, HEAD_DIM), lambda i, d, n: (i, 0, 0)),
        scratch_shapes=[pltpu.VMEM((g, TOPK, HEADS, HEAD_DIM), F32)],
    )
    return pl.pallas_call(
        _attn_compact_body,
        grid_spec=grid_spec,
        out_shape=jax.ShapeDtypeStruct((bsz, HEADS, HEAD_DIM), F32),
        compiler_params=_cparams(("arbitrary",)),
        name="attn_sample_compact",
    )(dist, first_near, zs3, zs3, zs3, bias_by_dist, k_sel, v_sel)


def _bucket_table(max_dist):
    exact = REL_BUCKETS // 2
    d = np.arange(max_dist + 1)
    df = np.maximum(d, 1).astype(np.float32)
    far = exact + (np.log(df / exact) / np.float32(math.log(REL_MAX_DIST / exact))
                   * (REL_BUCKETS - exact)).astype(np.int32)
    return np.where(d < exact, d, np.minimum(far, REL_BUCKETS - 1)).astype(np.int32)


def _bias_tables(rel_bias):
    tab = _bucket_table(2 * LANES)
    assert np.all(tab[REL_MAX_DIST:] == REL_BUCKETS - 1)
    i = np.arange(LANES)
    dist0 = np.maximum(i[:, None] - i[None, :], 0)
    dist1 = LANES + i[:, None] - i[None, :]
    far = np.full((LANES, LANES), REL_BUCKETS - 1)
    idx = np.stack([tab[dist0], tab[dist1], far])
    buckets = jnp.arange(REL_BUCKETS)
    lookup = lambda ix, spec: jnp.einsum(spec, (jnp.asarray(ix)[..., None] == buckets).astype(F32),
                                         rel_bias.astype(F32), precision=lax.Precision.HIGHEST)
    tiles = lookup(idx, "tijb,bh->thij")
    by_dist = jnp.broadcast_to(lookup(tab[:REL_MAX_DIST + 1], "db,bh->dh")[:, :, None],
                               (REL_MAX_DIST + 1, HEADS, LANES))
    return tiles.astype(F32), rel_bias[REL_BUCKETS - 1].astype(F32), by_dist.astype(F32)


def _conv_prompt_body(bg_ref, cg_ref, xt_ref, w_ref, v_ref, st_ref, carry_ref, *, tblock):
    t = pl.program_id(1)

    @pl.when(t == 0)
    def _():
        carry_ref[...] = jnp.zeros_like(carry_ref)

    u = cg_ref[0] * xt_ref[0]
    row = lax.broadcasted_iota(I32, u.shape, 0)
    c0 = carry_ref[0:1]
    c1 = carry_ref[1:2]
    u1 = jnp.where(row == 0, c1, pltpu.roll(u, 1, axis=0))
    u2 = jnp.where(row == 0, c0, jnp.where(row == 1, c1, pltpu.roll(u, 2, axis=0)))
    conv = w_ref[0:1] * u2 + w_ref[1:2] * u1 + w_ref[2:3] * u
    v_ref[0] = bg_ref[0] * conv
    last = u[tblock - 2:tblock]
    carry_ref[0:2] = last

    @pl.when(t == pl.num_programs(1) - 1)
    def _():
        st_ref[0] = last


def conv_prompt(zc3, w_conv, tblock=256):
    bsz, seq = zc3.shape[:2]
    c = zc3.shape[2] // 3
    zspec = lambda cb: pl.BlockSpec((1, tblock, c), lambda b, t, cb=cb: (b, t, cb))
    return pl.pallas_call(
        functools.partial(_conv_prompt_body, tblock=tblock),
        grid=(bsz, seq // tblock),
        in_specs=[zspec(0), zspec(1), zspec(2), pl.BlockSpec((3, c), lambda b, t: (0, 0))],
        out_specs=[pl.BlockSpec((1, tblock, c), lambda b, t: (b, t, 0)),
                   pl.BlockSpec((1, 2, c), lambda b, t: (b, 0, 0))],
        out_shape=[jax.ShapeDtypeStruct((bsz, seq, c), F32),
                   jax.ShapeDtypeStruct((bsz, 2, c), F32)],
        scratch_shapes=[pltpu.VMEM((8, c), F32)],
        compiler_params=_cparams(("parallel", "arbitrary")),
        name="conv_prompt",
    )(zc3, zc3, zc3, w_conv)


def _conv_sample_body(bg_ref, cg_ref, xt_ref, w_ref, s0_ref, s1_ref, v_ref, n0_ref, n1_ref):
    u = cg_ref[...] * xt_ref[...]
    conv = w_ref[0:1] * s0_ref[...] + w_ref[1:2] * s1_ref[...] + w_ref[2:3] * u
    v_ref[...] = bg_ref[...] * conv
    n0_ref[...] = s1_ref[...]
    n1_ref[...] = u


def conv_sample(zc, w_conv, s0, s1):
    bsz = zc.shape[0]
    c = zc.shape[1] // 3
    zspec = lambda cb: pl.BlockSpec((bsz, c), lambda i, cb=cb: (0, cb))
    full = pl.BlockSpec((bsz, c), lambda i: (0, 0))
    return pl.pallas_call(
        _conv_sample_body,
        grid=(1,),
        in_specs=[zspec(0), zspec(1), zspec(2), pl.BlockSpec((3, c), lambda i: (0, 0)), full, full],
        out_specs=[full, full, full],
        out_shape=[jax.ShapeDtypeStruct((bsz, c), F32)] * 3,
        compiler_params=_cparams(("arbitrary",)),
        name="conv_sample",
    )(zc, zc, zc, w_conv, s0, s1)


def _router_body(x_ref, g_ref, wr_ref, br_ref, h_ref, route_ref):
    x = x_ref[...]
    ms = jnp.mean(x * x, axis=-1, keepdims=True)
    h = x * lax.rsqrt(ms + RMS_EPS) * g_ref[...]
    hb = h.astype(BF)
    half = h.shape[1] // 2
    bits = pltpu.bitcast(hb.astype(F32), jnp.uint32)
    h_ref[...] = (bits[:, :half] & jnp.uint32(0xFFFF0000)) | (bits[:, half:] >> 16)
    logits = _dot(hb, wr_ref[...]) + br_ref[...]
    lane = lax.broadcasted_iota(I32, logits.shape, 1)
    big = np.int32(1 << 20)
    lg = jnp.where(lane < N_GROUPS, logits, -jnp.inf)
    g_max = jnp.max(lg, axis=-1, keepdims=True)
    g_idx = jnp.min(jnp.where(lg == g_max, lane, big), axis=-1, keepdims=True)
    g_w = 1.0 / jnp.sum(jnp.exp(lg - g_max), axis=-1, keepdims=True)
    first = N_GROUPS + EXP_PER_GROUP * g_idx
    le = jnp.where((lane >= first) & (lane < first + EXP_PER_GROUP), logits, -jnp.inf)
    l1 = jnp.max(le, axis=-1, keepdims=True)
    i1 = jnp.min(jnp.where(le == l1, lane, big), axis=-1, keepdims=True)
    le2 = jnp.where(lane == i1, -jnp.inf, le)
    l2 = jnp.max(le2, axis=-1, keepdims=True)
    i2 = jnp.min(jnp.where(le2 == l2, lane, big), axis=-1, keepdims=True)
    r = jnp.exp(l2 - l1)
    w1 = g_w / (1.0 + r)
    w2 = g_w * r / (1.0 + r)
    e1 = (i1 - N_GROUPS).astype(F32)
    e2 = (i2 - N_GROUPS).astype(F32)
    route_ref[...] = jnp.where(lane == 0, e1, jnp.where(lane == 1, e2,
                               jnp.where(lane == 2, w1, jnp.where(lane == 3, w2, 0.0))))


def moe_router(x, g, wr, br, tm):
    m, k = x.shape
    return pl.pallas_call(
        _router_body,
        grid=(m // tm,),
        in_specs=[pl.BlockSpec((tm, k), lambda i: (i, 0)),
                  pl.BlockSpec((1, k), lambda i: (0, 0)),
                  pl.BlockSpec((k, LANES), lambda i: (0, 0)),
                  pl.BlockSpec((1, LANES), lambda i: (0, 0))],
        out_specs=[pl.BlockSpec((tm, k // 2), lambda i: (i, 0)),
                   pl.BlockSpec((tm, LANES), lambda i: (i, 0))],
        out_shape=[jax.ShapeDtypeStruct((m, k // 2), jnp.uint32),
                   jax.ShapeDtypeStruct((m, LANES), F32)],
        compiler_params=_cparams(("parallel",)),
        name="moe_router",
    )(x, g.reshape(1, k), wr, br)


def _experts_body(te_ref, nu_ref, h_ref, wg_ref, wu_ref, wd_ref, o_ref, wg_bf, wu_bf, wd_bf):
    i = pl.program_id(0)

    @pl.when((i == 0) | (te_ref[i] != te_ref[jnp.maximum(i - 1, 0)]))
    def _():
        wg_bf[...] = wg_ref[0].astype(BF)
        wu_bf[...] = wu_ref[0].astype(BF)
        wd_bf[...] = wd_ref[0].astype(BF)

    @pl.when(i < nu_ref[0])
    def _():
        words = h_ref[...]
        left = pltpu.bitcast(words & jnp.uint32(0xFFFF0000), F32)
        right = pltpu.bitcast(words << 16, F32)
        h = jnp.concatenate([left, right], axis=1).astype(BF)
        a = _dot(h, wg_bf[...])
        b = _dot(h, wu_bf[...])
        hid = a * _sigmoid(a) * b
        o_ref[...] = _dot(hid.astype(BF), wd_bf[...])

    @pl.when(i >= nu_ref[0])
    def _():
        o_ref[...] = jnp.zeros_like(o_ref)


def moe_experts(tile_expert, n_used, hs, wg, wu, wd):
    p = hs.shape[0]
    k = wg.shape[1]
    f = wg.shape[2]
    n_tiles = p // MOE_TILE
    grid_spec = pltpu.PrefetchScalarGridSpec(
        num_scalar_prefetch=2,
        grid=(n_tiles,),
        in_specs=[pl.BlockSpec((MOE_TILE, k // 2), lambda i, te, nu: (i, 0)),
                  pl.BlockSpec((1, k, f), lambda i, te, nu: (te[i], 0, 0)),
                  pl.BlockSpec((1, k, f), lambda i, te, nu: (te[i], 0, 0)),
                  pl.BlockSpec((1, f, k), lambda i, te, nu: (te[i], 0, 0))],
        out_specs=pl.BlockSpec((MOE_TILE, k), lambda i, te, nu: (i, 0)),
        scratch_shapes=[pltpu.VMEM((k, f), BF), pltpu.VMEM((k, f), BF), pltpu.VMEM((f, k), BF)],
    )
    return pl.pallas_call(
        _experts_body,
        grid_spec=grid_spec,
        out_shape=jax.ShapeDtypeStruct((p, k), F32),
        compiler_params=_cparams(("arbitrary",)),
        name="moe_experts",
    )(tile_expert, n_used, hs, wg, wu, wd)


def _rank_within_expert(onehot):
    n, e = onehot.shape
    blk = LANES
    assert n % blk == 0
    oh = onehot.astype(F32).reshape(n // blk, blk, e)
    strict = jnp.asarray(np.tril(np.ones((blk, blk), np.float32), -1))
    within = jnp.einsum("ij,bjk->bik", strict, oh, precision=lax.Precision.HIGHEST)
    totals = jnp.sum(oh, axis=1)
    before = jnp.cumsum(totals, axis=0) - totals
    rank = (within + before[:, None, :]).reshape(n, e)
    return jnp.sum(rank * onehot.astype(F32), axis=1).astype(I32), jnp.sum(totals, axis=0).astype(I32)


def hier_moe(xs, tms, g, wrg, brg, wre, bre, wg, wu, wd, layer):
    d = xs[0].shape[1]
    m = sum(x.shape[0] for x in xs)
    wr = jnp.zeros((d, LANES), F32)
    wr = wr.at[:, :N_GROUPS].set(wrg).at[:, N_GROUPS:N_GROUPS + N_EXPERTS].set(wre.reshape(d, N_EXPERTS))
    br = jnp.zeros((1, LANES), F32)
    br = br.at[0, :N_GROUPS].set(brg).at[0, N_GROUPS:N_GROUPS + N_EXPERTS].set(bre.reshape(N_EXPERTS))
    routed = [moe_router(x, g, wr.astype(BF), br, tm) for x, tm in zip(xs, tms)]
    h_bf = jnp.concatenate([r[0] for r in routed], axis=0)
    route = jnp.concatenate([r[1] for r in routed], axis=0)

    eid = route[:, 0:2].astype(I32).reshape(-1)
    onehot = eid[:, None] == jnp.arange(N_EXPERTS, dtype=I32)[None, :]
    rank, counts = _rank_within_expert(onehot)
    padded = ((counts + MOE_TILE - 1) // MOE_TILE) * MOE_TILE
    ends = jnp.cumsum(padded)
    pos = jnp.sum(jnp.where(onehot, (ends - padded)[None, :], 0), axis=1) + rank
    n_rows = 2 * m + N_EXPERTS * MOE_TILE
    n_rows = -(-n_rows // MOE_TILE) * MOE_TILE
    token = jnp.zeros((n_rows,), I32).at[pos].set(jnp.arange(2 * m, dtype=I32) // 2)
    tile_start = jnp.arange(n_rows // MOE_TILE, dtype=I32) * MOE_TILE
    tile_expert = jnp.minimum(jnp.sum(tile_start[:, None] >= ends[None, :], axis=1),
                              N_EXPERTS - 1).astype(I32)
    n_used = (ends[-1] // MOE_TILE).astype(I32).reshape(1)

    hs = jnp.take(h_bf, token, axis=0, mode="clip")
    out = moe_experts(tile_expert + layer * N_EXPERTS, n_used, hs, wg, wu, wd)
    pos2 = pos.reshape(m, 2)
    res, r0 = [], 0
    for x in xs:
        rows = slice(r0, r0 + x.shape[0])
        res.append(x + route[rows, 2:3] * jnp.take(out, pos2[rows, 0], axis=0, mode="clip")
                   + route[rows, 3:4] * jnp.take(out, pos2[rows, 1], axis=0, mode="clip"))
        r0 += x.shape[0]
    return res


def kernel(x_prompt, x_sample, cache_k, cache_v, cache_ki, state_hgrn, state_conv, page_table,
           norm_mix_g, norm_ffn_g, final_g, w_in_even, w_out_even, hgrn_lb_logits, hgrn_norm_g,
           rel_bias, w_in_conv, w_conv, w_out_conv, w_router_g, b_router_g, w_router_e,
           b_router_e, w_gate, w_up, w_down):
    bsz, seq, d = x_prompt.shape
    dec = x_sample.shape[0]
    n_p = bsz * seq
    tm_p, tm_s = 512, 128
    tms = (tm_p, tm_s)
    assert n_p % tm_p == 0 and dec % tm_s == 0 and x_sample.shape[1] == 1
    width = HEADS * HEAD_DIM
    even_in = w_in_even.shape[2]
    even_pad = -(-even_in // LANES) * LANES
    tail0 = even_pad - LANES
    n_pool = cache_k.shape[1]

    xp = x_prompt.reshape(n_p, d)
    xs = x_sample.reshape(dec, d)

    lbs = jnp.cumsum(jax.nn.softmax(hgrn_lb_logits.astype(F32), axis=0), axis=0)[:-1]
    bias_tiles, bias_far, bias_by_dist = _bias_tables(rel_bias)
    expert_w = [w.reshape(-1, *w.shape[2:]) for w in (w_gate, w_up, w_down)]

    w_in = jnp.pad(w_in_even[0], ((0, 0), (0, even_pad - even_in))).astype(BF)
    zp = norm_matmul(xp, norm_mix_g[0], w_in, 2 * tm_p, 640)
    zs = norm_matmul(xs, norm_mix_g[0], w_in, tm_s, 640)
    zp3 = zp.reshape(bsz, seq, even_pad)
    zs3 = zs.reshape(dec, even_pad // LANES, LANES)

    oa_p, hgrn_p = hgrn_prompt(zp3, lbs[0], hgrn_norm_g[0])
    oa_s, hgrn_s = hgrn_sample(zs3, lbs[0], hgrn_norm_g[0], state_hgrn[0])

    mask_p = indexer_prompt(zp3, TOPK)
    kv_bf = zp3[:, :, 5 * width:7 * width].astype(BF)
    ob_p = attn_prompt(zp3, kv_bf, mask_p, bias_tiles, bias_far)

    qi3 = zs[:, 7 * width:8 * width].reshape(dec, IDX_HEADS, IDX_DIM)
    ki_new = zs[:, tail0:tail0 + IDX_DIM].reshape(dec, 1, IDX_DIM)
    wcol = zs[:, tail0 + IDX_DIM:tail0 + IDX_DIM + IDX_HEADS].reshape(dec, IDX_HEADS, 1)
    pages = page_table + 0 * n_pool
    scores_s = indexer_sample(pages, qi3, wcol, ki_new,
                              jnp.swapaxes(cache_ki, -1, -2).reshape(-1, IDX_DIM, PAGE))
    mask_s = select_topk(scores_s, TOPK)
    sel = jnp.transpose(mask_s, (1, 0, 2))
    n_pages = page_table.shape[1]
    sel_off = jnp.concatenate([jnp.zeros((dec, 1), F32),
                               jnp.cumsum(jnp.sum(sel, axis=2), axis=1)], axis=1)
    slot = jnp.arange(TOPK, dtype=F32)
    page_of = jnp.sum(sel_off[:, None, 1:] <= slot[None, :, None], axis=2)
    page_1h = (page_of[:, :, None] == jnp.arange(n_pages + 1)[None, None, :]).astype(F32)
    local = slot[None, :] - jnp.einsum("brp,bp->br", page_1h, sel_off[:, :-1],
                                       precision=lax.Precision.HIGHEST)
    within = jnp.einsum("brp,bpl->brl", page_1h, jnp.cumsum(sel, axis=2),
                        precision=lax.Precision.HIGHEST)
    lane_of = jnp.sum(within <= local[:, :, None], axis=2)
    sel_idx = (page_of * PAGE + lane_of).astype(I32)
    past = n_pages * PAGE
    page_id = jnp.einsum("brp,bp->br", page_1h[:, :, :n_pages], pages.astype(F32),
                         precision=lax.Precision.HIGHEST).astype(I32)
    rows = jnp.where(page_of < n_pages, page_id * PAGE + lane_of, 0).astype(I32).reshape(-1)
    k_sel, v_sel = sc_gather_rows([cache_k.reshape(-1, HEADS, HEAD_DIM),
                                   cache_v.reshape(-1, HEADS, HEAD_DIM)], rows)
    dist = jnp.minimum(past - sel_idx, REL_MAX_DIST).astype(I32)
    first_near = jnp.sum(dist >= REL_MAX_DIST, axis=1).astype(I32)
    ob_s = attn_sample_compact(dist, first_near, zs3, k_sel, v_sel, bias_by_dist)

    w_out = w_out_even[0].astype(BF)
    w_out_ab = [w_out[:width], w_out[width:]]
    xp = matmul_residual([oa_p.reshape(n_p, width), ob_p.reshape(n_p, width)], w_out_ab, xp, 2 * tm_p, 512)
    xs = matmul_residual([oa_s.reshape(dec, width), ob_s.reshape(dec, width)], w_out_ab, xs, tm_s, 512)

    xp, xs = hier_moe([xp, xs], tms, norm_ffn_g[0], w_router_g[0], b_router_g[0], w_router_e[0],
                      b_router_e[0], *expert_w, 0)

    w_in_c = w_in_conv[0].astype(BF)
    zcp = norm_matmul(xp, norm_mix_g[1], w_in_c, 2 * tm_p, 1024)
    zcs = norm_matmul(xs, norm_mix_g[1], w_in_c, tm_s, 512)
    cw = zcp.shape[1] // 3
    v_p, conv_p = conv_prompt(zcp.reshape(bsz, seq, 3 * cw), w_conv[0])
    v_s, cs0, cs1 = conv_sample(zcs, w_conv[0], state_conv[0, :, 0], state_conv[0, :, 1])
    w_out_c = [w_out_conv[0].astype(BF)]
    xp = matmul_residual([v_p.reshape(n_p, cw)], w_out_c, xp, 2 * tm_p, 512)
    xs = matmul_residual([v_s], w_out_c, xs, tm_s, 512)

    xp, xs = hier_moe([xp, xs], tms, norm_ffn_g[1], w_router_g[1], b_router_g[1], w_router_e[1],
                      b_router_e[1], *expert_w, 1)

    yp = rmsnorm(xp, final_g, tm_p)
    ys = rmsnorm(xs, final_g, tm_s)

    kcol, vcol = 5 * width, 6 * width
    heads = lambda a, n: a.reshape(1, *n, HEADS, HEAD_DIM)
    return (yp.reshape(bsz, seq, d),
            ys.reshape(dec, 1, d),
            heads(zp[:, kcol:kcol + width], (bsz, seq)),
            heads(zp[:, vcol:vcol + width], (bsz, seq)),
            zp[:, tail0:tail0 + IDX_DIM].reshape(1, bsz, seq, IDX_DIM),
            hgrn_p[None],
            conv_p[None],
            heads(zs[:, kcol:kcol + width], (dec, 1)),
            heads(zs[:, vcol:vcol + width], (dec, 1)),
            zs[:, tail0:tail0 + IDX_DIM].reshape(1, dec, 1, IDX_DIM),
            hgrn_s[None],
            jnp.stack([cs0, cs1], axis=1)[None])
```

```python
import functools
import math

import numpy as np
import jax
import jax.numpy as jnp
from jax import lax
from jax.experimental import pallas as pl
from jax.experimental.pallas import tpu as pltpu
from jax.experimental.pallas import tpu_sc as plsc

F32 = jnp.float32
BF = jnp.bfloat16
I32 = jnp.int32

RMS_EPS = 1e-6
LANES = 128
NEG_BIG = -1e30
VMEM_LIMIT = 56 * 1024 * 1024

D_MODEL = 2048
HEADS = 8
HEAD_DIM = 128
IDX_HEADS = 16
IDX_DIM = 64
TOPK = 256
REL_BUCKETS = 32
REL_MAX_DIST = 128
N_EXPERTS = 16
EXP_PER_GROUP = 4
N_GROUPS = 4
D_EXPERT = 512
PAGE = 128

HGRN_CHUNK = 64
MOE_TILE = 256


def _cparams(sem):
    return pltpu.CompilerParams(dimension_semantics=sem, vmem_limit_bytes=VMEM_LIMIT)


def _dot(a, b):
    return jnp.dot(a, b, preferred_element_type=F32)


def _dot_nt(a, b):
    return lax.dot_general(a, b, (((1,), (1,)), ((), ())), preferred_element_type=F32)


def _dot_tn(a, b):
    return lax.dot_general(a, b, (((0,), (0,)), ((), ())), preferred_element_type=F32)


def _sigmoid(x):
    return 1.0 / (1.0 + jnp.exp(-x))


def _norm_mm_body(x_ref, g_ref, w_ref, o_ref, h_ref):
    @pl.when(pl.program_id(1) == 0)
    def _():
        x = x_ref[...]
        ms = jnp.mean(x * x, axis=-1, keepdims=True)
        h_ref[...] = (x * lax.rsqrt(ms + RMS_EPS) * g_ref[...]).astype(BF)

    o_ref[...] = _dot(h_ref[...], w_ref[...])


def norm_matmul(x, g, w_bf, tm, tn):
    m, k = x.shape
    n = w_bf.shape[1]
    return pl.pallas_call(
        _norm_mm_body,
        grid=(m // tm, n // tn),
        in_specs=[pl.BlockSpec((tm, k), lambda i, j: (i, 0)),
                  pl.BlockSpec((1, k), lambda i, j: (0, 0)),
                  pl.BlockSpec((k, tn), lambda i, j: (0, j))],
        out_specs=pl.BlockSpec((tm, tn), lambda i, j: (i, j)),
        out_shape=jax.ShapeDtypeStruct((m, n), F32),
        scratch_shapes=[pltpu.VMEM((tm, k), BF)],
        compiler_params=_cparams(("parallel", "arbitrary")),
        name="norm_matmul",
    )(x, g.reshape(1, k), w_bf)


def _mm_res_body(*refs, n_lhs):
    a_refs = refs[:n_lhs]
    w_refs = refs[n_lhs:2 * n_lhs]
    r_ref = refs[2 * n_lhs]
    o_ref = refs[2 * n_lhs + 1]
    s_refs = refs[2 * n_lhs + 2:]

    @pl.when(pl.program_id(1) == 0)
    def _():
        for a_ref, s_ref in zip(a_refs, s_refs):
            s_ref[...] = a_ref[...].astype(BF)

    acc = r_ref[...]
    for s_ref, w_ref in zip(s_refs, w_refs):
        acc = acc + _dot(s_ref[...], w_ref[...])
    o_ref[...] = acc


def matmul_residual(lhs, ws_bf, res, tm, tn):
    m, n = res.shape
    n_lhs = len(lhs)
    in_specs = ([pl.BlockSpec((tm, a.shape[1]), lambda i, j: (i, 0)) for a in lhs]
                + [pl.BlockSpec((w.shape[0], tn), lambda i, j: (0, j)) for w in ws_bf]
                + [pl.BlockSpec((tm, tn), lambda i, j: (i, j))])
    return pl.pallas_call(
        functools.partial(_mm_res_body, n_lhs=n_lhs),
        grid=(m // tm, n // tn),
        in_specs=in_specs,
        out_specs=pl.BlockSpec((tm, tn), lambda i, j: (i, j)),
        out_shape=jax.ShapeDtypeStruct((m, n), F32),
        scratch_shapes=[pltpu.VMEM((tm, a.shape[1]), BF) for a in lhs],
        compiler_params=_cparams(("parallel", "arbitrary")),
        name="matmul_residual",
    )(*lhs, *ws_bf, res)


def _rmsnorm_body(x_ref, g_ref, o_ref):
    x = x_ref[...]
    ms = jnp.mean(x * x, axis=-1, keepdims=True)
    o_ref[...] = x * lax.rsqrt(ms + RMS_EPS) * g_ref[...]


def rmsnorm(x, g, tm):
    m, k = x.shape
    return pl.pallas_call(
        _rmsnorm_body,
        grid=(m // tm,),
        in_specs=[pl.BlockSpec((tm, k), lambda i: (i, 0)),
                  pl.BlockSpec((1, k), lambda i: (0, 0))],
        out_specs=pl.BlockSpec((tm, k), lambda i: (i, 0)),
        out_shape=jax.ShapeDtypeStruct((m, k), F32),
        compiler_params=_cparams(("parallel",)),
        name="rmsnorm",
    )(x, g.reshape(1, k))


def _hgrn_static(c):
    levels = []
    m = 1
    while m < c:
        levels.append(m)
        m *= 2
    t = np.arange(c)
    rows = [t[None, :] <= t[:, None]]
    masks = [np.eye(c, dtype=bool)]
    for m in levels:
        blk = t // (2 * m)
        pos = t % (2 * m)
        bnd = blk * 2 * m + m - 1
        right = pos >= m
        left = pos < m
        e_rows = (t[None, :] > bnd[:, None]) & (t[None, :] <= t[:, None]) & right[:, None]
        f_rows = (t[None, :] > t[:, None]) & (t[None, :] <= bnd[:, None]) & left[:, None]
        rows.append(e_rows | f_rows)
        masks.append((blk[:, None] == blk[None, :]) & right[:, None] & left[None, :])
    m_all = np.stack(rows).astype(np.float32)
    masks = np.stack(masks).astype(np.float32)
    return m_all, masks, len(levels)


def _hgrn_gates(qa, fa, lb):
    f = lb + (1.0 - lb) * _sigmoid(fa)
    q = qa * _sigmoid(qa)
    return q, f


def _hgrn_prompt_body(qa_ref, fa_ref, ia_ref, ga_ref, lb_ref, gn_ref, mall_ref, masks_ref,
                      oa_ref, st_out_ref, st_ref, *, chunk, tblock, n_levels):
    c = chunk
    t_idx = pl.program_id(1)

    @pl.when(t_idx == 0)
    def _():
        st_ref[...] = jnp.zeros_like(st_ref)

    gn = gn_ref[...]

    def chunk_step(ci, carry):
        r0 = pl.multiple_of(ci * c, c)
        for h in range(HEADS):
            cols = slice(h * HEAD_DIM, (h + 1) * HEAD_DIM)
            qa = qa_ref[0, pl.ds(r0, c), cols]
            fa = fa_ref[0, pl.ds(r0, c), cols]
            ia = ia_ref[0, pl.ds(r0, c), cols]
            ga = ga_ref[0, pl.ds(r0, c), cols]
            q, f = _hgrn_gates(qa, fa, lb_ref[:, cols])
            k = 1.0 - f
            g = jnp.log(f)
            g_hi = g.astype(BF)
            g_lo = (g - g_hi.astype(F32)).astype(BF)
            b = _dot(mall_ref[0], g_hi) + _dot(mall_ref[0], g_lo)
            b_last = b[c - 1:c]
            st = st_ref[h]
            v = ia.astype(BF)
            o = _dot_nt((q * jnp.exp(b)).astype(BF), st.astype(BF))
            a = masks_ref[0] * _dot_nt(q.astype(BF), k.astype(BF))
            for li in range(n_levels):
                ml = mall_ref[1 + li]
                w = jnp.exp(_dot(ml, g_hi) + _dot(ml, g_lo))
                a = a + masks_ref[1 + li] * _dot_nt((q * w).astype(BF), (k * w).astype(BF))
            o = o + _dot(a.astype(BF), v)
            k_st = (k * jnp.exp(b_last - b)).astype(BF)
            st_ref[h] = st * jnp.exp(b_last) + _dot_tn(v, k_st)
            ms = jnp.mean(o * o, axis=-1, keepdims=True)
            y = o * lax.rsqrt(ms + RMS_EPS) * gn * (ga * _sigmoid(ga))
            oa_ref[0, pl.ds(r0, c), cols] = y
        return carry

    lax.fori_loop(0, tblock // c, chunk_step, 0)

    @pl.when(t_idx == pl.num_programs(1) - 1)
    def _():
        st_out_ref[0] = st_ref[...]


def hgrn_prompt(z3, lb, gn, tblock=256, chunk=HGRN_CHUNK):
    bsz, seq = z3.shape[:2]
    width = HEADS * HEAD_DIM
    m_all, masks, n_levels = _hgrn_static(chunk)
    zspec = lambda cb: pl.BlockSpec((1, tblock, width), lambda b, t, cb=cb: (b, t, cb))
    oa, st = pl.pallas_call(
        functools.partial(_hgrn_prompt_body, chunk=chunk, tblock=tblock, n_levels=n_levels),
        grid=(bsz, seq // tblock),
        in_specs=[zspec(0), zspec(1), zspec(2), zspec(3),
                  pl.BlockSpec((1, width), lambda b, t: (0, 0)),
                  pl.BlockSpec((1, HEAD_DIM), lambda b, t: (0, 0)),
                  pl.BlockSpec(m_all.shape, lambda b, t: (0, 0, 0)),
                  pl.BlockSpec(masks.shape, lambda b, t: (0, 0, 0))],
        out_specs=[pl.BlockSpec((1, tblock, width), lambda b, t: (b, t, 0)),
                   pl.BlockSpec((1, HEADS, HEAD_DIM, HEAD_DIM), lambda b, t: (b, 0, 0, 0))],
        out_shape=[jax.ShapeDtypeStruct((bsz, seq, width), F32),
                   jax.ShapeDtypeStruct((bsz, HEADS, HEAD_DIM, HEAD_DIM), F32)],
        scratch_shapes=[pltpu.VMEM((HEADS, HEAD_DIM, HEAD_DIM), F32)],
        compiler_params=_cparams(("parallel", "arbitrary")),
        name="hgrn_prompt",
    )(z3, z3, z3, z3, lb.reshape(1, width), gn.reshape(1, HEAD_DIM),
      jnp.asarray(m_all, BF), jnp.asarray(masks, F32))
    return oa, jnp.swapaxes(st, -1, -2)


def _col(row, eye):
    return jnp.sum(eye * row, axis=1, keepdims=True)


def _hgrn_sample_body(qa_ref, fa_ref, ia_ref, ga_ref, lb_ref, gn_ref, s_ref, oa_ref, so_ref):
    eye = (lax.broadcasted_iota(I32, (HEAD_DIM, HEAD_DIM), 0)
           == lax.broadcasted_iota(I32, (HEAD_DIM, HEAD_DIM), 1)).astype(F32)
    q8, f8 = _hgrn_gates(qa_ref[0], fa_ref[0], lb_ref[...])
    ga = ga_ref[0]
    gate = ga * _sigmoid(ga)
    outs = []
    for h in range(HEADS):
        f_col = _col(f8[h:h + 1], eye)
        q_col = _col(q8[h:h + 1], eye)
        s_new = f_col * s_ref[0, h] + (1.0 - f_col) * ia_ref[0, h:h + 1]
        so_ref[0, h] = s_new
        outs.append(jnp.sum(q_col * s_new, axis=0, keepdims=True))
    o = jnp.concatenate(outs, axis=0)
    ms = jnp.mean(o * o, axis=-1, keepdims=True)
    oa_ref[0] = o * lax.rsqrt(ms + RMS_EPS) * gn_ref[...] * gate


def hgrn_sample(zs3, lb, gn, s0):
    bsz = zs3.shape[0]
    zspec = lambda cb: pl.BlockSpec((1, HEADS, HEAD_DIM), lambda b, cb=cb: (b, cb, 0))
    sspec = pl.BlockSpec((1, HEADS, HEAD_DIM, HEAD_DIM), lambda b: (b, 0, 0, 0))
    return pl.pallas_call(
        _hgrn_sample_body,
        grid=(bsz,),
        in_specs=[zspec(0), zspec(1), zspec(2), zspec(3),
                  pl.BlockSpec((HEADS, HEAD_DIM), lambda b: (0, 0)),
                  pl.BlockSpec((1, HEAD_DIM), lambda b: (0, 0)),
                  sspec],
        out_specs=[pl.BlockSpec((1, HEADS, HEAD_DIM), lambda b: (b, 0, 0)), sspec],
        out_shape=[jax.ShapeDtypeStruct((bsz, HEADS, HEAD_DIM), F32),
                   jax.ShapeDtypeStruct(s0.shape, F32)],
        compiler_params=_cparams(("parallel",)),
        name="hgrn_sample",
    )(zs3, zs3, zs3, zs3, lb.reshape(HEADS, HEAD_DIM), gn.reshape(1, HEAD_DIM), s0)


_KEY_NEG_INF = np.int32(np.uint32(0x807FFFFF).astype(np.int64) - (1 << 32))
_INT_MIN = np.int32(-(1 << 31))


def _count(u_ref, n_groups, group, thr, cmp):
    rows = u_ref.shape[1]
    step = min(rows, LANES)
    parts = []
    for r0 in range(0, rows, step):
        t = jnp.broadcast_to(thr[r0:r0 + step], (step, LANES))

        def body(gi, acc, r0=r0, t=t):
            for i in range(group):
                acc = acc + cmp(u_ref[gi * group + i, r0:r0 + step], t).astype(F32)
            return acc

        acc = lax.fori_loop(0, n_groups, body, jnp.zeros((step, LANES), F32))
        parts.append(jnp.sum(acc, axis=-1, keepdims=True))
    return parts[0] if len(parts) == 1 else jnp.concatenate(parts, axis=0)


def _topk_mask(score_ref, u_ref, write_tile, n_tiles, n_groups, group, k):
    rows = score_ref.shape[1]
    n_live = n_groups * group

    def to_key(t, carry):
        bits = pltpu.bitcast(score_ref[t], I32)
        u_ref[t] = jnp.where(bits < 0, bits ^ np.int32(0x7FFFFFFF), bits)
        return carry

    lax.fori_loop(0, n_live, to_key, 0)

    kf = float(k)
    ge = lambda u, t: u >= t
    cnt = _count(u_ref, n_groups, group, jnp.zeros((rows, 1), I32), ge)
    lo = jnp.where(cnt >= kf, np.int32(0), _INT_MIN)

    def bit_step(i, lo):
        cand = lo | (np.int32(1) << (30 - i))
        cnt = _count(u_ref, n_groups, group, cand, ge)
        return jnp.where(cnt >= kf, cand, lo)

    lo = lax.fori_loop(0, 31, bit_step, lo)
    c_gt = _count(u_ref, n_groups, group, lo, lambda u, t: u > t)
    c_eq = _count(u_ref, n_groups, group, lo, lambda u, t: u == t)
    need = kf - c_gt
    real = lo > _KEY_NEG_INF
    excess = jnp.where(real & (c_eq > need), 1.0, 0.0)
    any_excess = jnp.max(excess) > 0.0

    @pl.when(jnp.logical_not(any_excess))
    def _():
        def emit(t, carry):
            u = u_ref[t]
            write_tile(t, jnp.where((u >= lo) & (u > _KEY_NEG_INF), 1.0, 0.0))
            return carry

        lax.fori_loop(0, n_live, emit, 0)

    @pl.when(any_excess)
    def _():
        upper = (lax.broadcasted_iota(I32, (LANES, LANES), 0)
                 <= lax.broadcasted_iota(I32, (LANES, LANES), 1)).astype(BF)

        def emit(t, seen):
            u = u_ref[t]
            eq = jnp.where(u == lo, 1.0, 0.0)
            prefix = seen + _dot(eq.astype(BF), upper)
            take = (u > lo) | ((u == lo) & (prefix <= need))
            write_tile(t, jnp.where(take & (u > _KEY_NEG_INF), 1.0, 0.0))
            return seen + jnp.sum(eq, axis=-1, keepdims=True)

        lax.fori_loop(0, n_live, emit, jnp.zeros((rows, 1), F32))

    def clear(t, carry):
        write_tile(t, jnp.zeros((rows, LANES), F32))
        return carry

    lax.fori_loop(n_live, n_tiles, clear, 0)


IDX_KCHUNK = 512
IDX_QROWS = 256


def _indexer_prompt_body(qi_ref, tailq_ref, tail_ref, mask_ref, score_ref, u_ref, *, seq):
    j = pl.program_id(1)
    rows = IDX_QROWS
    qblocks = rows // LANES
    n_tiles = seq // LANES
    tiles_per_chunk = IDX_KCHUNK // LANES
    qi = qi_ref[0].astype(BF)
    w = tailq_ref[0][:, IDX_DIM:IDX_DIM + IDX_HEADS] * (IDX_DIM ** -0.5 * IDX_HEADS ** -0.5)

    n_chunks = (j * rows + rows - 1) // IDX_KCHUNK + 1
    q_pos = j * rows + lax.broadcasted_iota(I32, (rows, IDX_KCHUNK), 0)

    def chunk_step(ci, carry):
        k0 = pl.multiple_of(ci * IDX_KCHUNK, IDX_KCHUNK)
        kic = tail_ref[0, pl.ds(k0, IDX_KCHUNK), :][:, 0:IDX_DIM].astype(BF)
        acc = jnp.zeros((rows, IDX_KCHUNK), F32)
        for h in range(IDX_HEADS):
            s = _dot_nt(qi[:, h * IDX_DIM:(h + 1) * IDX_DIM], kic)
            acc = acc + w[:, h:h + 1] * jnp.maximum(s, 0.0)
        k_pos = k0 + lax.broadcasted_iota(I32, (rows, IDX_KCHUNK), 1)
        acc = jnp.where(k_pos <= q_pos, acc, -jnp.inf)
        for i in range(tiles_per_chunk):
            score_ref[ci * tiles_per_chunk + i] = acc[:, i * LANES:(i + 1) * LANES]
        return carry

    lax.fori_loop(0, n_chunks, chunk_step, 0)

    def write_tile(t, m):
        mb = m.astype(BF)
        for qb in range(qblocks):
            mask_ref[0, qb, t] = mb[qb * LANES:(qb + 1) * LANES]

    _topk_mask(score_ref, u_ref, write_tile, n_tiles, n_chunks, tiles_per_chunk, TOPK)


def indexer_prompt(z3, k):
    bsz, seq = z3.shape[:2]
    nqb = seq // LANES
    rows = IDX_QROWS
    tail_block = z3.shape[2] // LANES - 1
    assert k == TOPK and seq % rows == 0 and seq % IDX_KCHUNK == 0
    return pl.pallas_call(
        functools.partial(_indexer_prompt_body, seq=seq),
        grid=(bsz, seq // rows),
        in_specs=[pl.BlockSpec((1, rows, IDX_HEADS * IDX_DIM), lambda b, j: (b, j, 7)),
                  pl.BlockSpec((1, rows, LANES), lambda b, j: (b, j, tail_block)),
                  pl.BlockSpec((1, seq, LANES), lambda b, j: (b, 0, tail_block))],
        out_specs=pl.BlockSpec((1, rows // LANES, nqb, LANES, LANES), lambda b, j: (b, j, 0, 0, 0)),
        out_shape=jax.ShapeDtypeStruct((bsz, nqb, nqb, LANES, LANES), BF),
        scratch_shapes=[pltpu.VMEM((nqb, rows, LANES), F32),
                        pltpu.VMEM((nqb, rows, LANES), I32)],
        compiler_params=_cparams(("parallel", "arbitrary")),
        name="indexer_prompt",
    )(z3, z3, z3)


ATTN_KSTEP = 512
ATTN_TILES = ATTN_KSTEP // LANES


def _attn_prompt_body(bfar_ref, q_ref, k_ref, v_ref, mask_ref, bias_ref, o_ref, m_ref, l_ref, acc_ref):
    j = pl.program_id(1)
    scale = HEAD_DIM ** -0.5
    q = q_ref[0].astype(BF)
    m_ref[...] = jnp.full(m_ref.shape, NEG_BIG, F32)
    l_ref[...] = jnp.zeros(l_ref.shape, F32)
    acc_ref[...] = jnp.zeros(acc_ref.shape, F32)

    def process(sb, near):
        k0 = pl.multiple_of(sb * ATTN_KSTEP, ATTN_KSTEP)
        kblk = k_ref[0, pl.ds(k0, ATTN_KSTEP), :]
        vblk = v_ref[0, pl.ds(k0, ATTN_KSTEP), :]
        tiles = [sb * ATTN_TILES + i for i in range(ATTN_TILES)]
        sel = jnp.concatenate([mask_ref[0, 0, t] for t in tiles], axis=1).astype(F32) > 0.0
        if near:
            bidx = [jnp.where(j - t == 0, 0, jnp.where(j - t == 1, 1, 2)) for t in tiles]
        for h in range(HEADS):
            cols = slice(h * HEAD_DIM, (h + 1) * HEAD_DIM)
            lg = _dot_nt(q[:, cols], kblk[:, cols]) * scale
            if near:
                lg = lg + jnp.concatenate([bias_ref[bi, h] for bi in bidx], axis=1)
            else:
                lg = lg + bfar_ref[h]
            lg = jnp.where(sel, lg, NEG_BIG)
            m_old = m_ref[h]
            m_new = jnp.maximum(m_old, jnp.max(lg, axis=-1, keepdims=True))
            p = jnp.exp(lg - jnp.concatenate([m_new] * ATTN_TILES, axis=1))
            alpha = jnp.exp(m_old - m_new)
            l_ref[h] = alpha * l_ref[h] + jnp.sum(p, axis=-1, keepdims=True)
            acc_ref[:, cols] = alpha * acc_ref[:, cols] + _dot(p.astype(BF), vblk[:, cols])
            m_ref[h] = m_new

    n_far = jnp.maximum((j - 1) // ATTN_TILES, 0)

    def far_step(sb, carry):
        process(sb, False)
        return carry

    def near_step(sb, carry):
        process(sb, True)
        return carry

    lax.fori_loop(0, n_far, far_step, 0)
    lax.fori_loop(n_far, j // ATTN_TILES + 1, near_step, 0)

    for h in range(HEADS):
        cols = slice(h * HEAD_DIM, (h + 1) * HEAD_DIM)
        o_ref[0, :, cols] = acc_ref[:, cols] / l_ref[h]


def attn_prompt(z3, kv_bf, mask, bias_tiles, bias_far):
    bsz, seq = z3.shape[:2]
    nqb = seq // LANES
    width = HEADS * HEAD_DIM
    assert seq % ATTN_KSTEP == 0
    return pl.pallas_call(
        _attn_prompt_body,
        grid=(bsz, nqb),
        in_specs=[pl.BlockSpec(memory_space=pltpu.SMEM),
                  pl.BlockSpec((1, LANES, width), lambda b, j: (b, j, 4)),
                  pl.BlockSpec((1, seq, width), lambda b, j: (b, 0, 0)),
                  pl.BlockSpec((1, seq, width), lambda b, j: (b, 0, 1)),
                  pl.BlockSpec((1, 1, nqb, LANES, LANES), lambda b, j: (b, j, 0, 0, 0)),
                  pl.BlockSpec(bias_tiles.shape, lambda b, j: (0, 0, 0, 0))],
        out_specs=pl.BlockSpec((1, LANES, width), lambda b, j: (b, j, 0)),
        out_shape=jax.ShapeDtypeStruct((bsz, seq, width), F32),
        scratch_shapes=[pltpu.VMEM((HEADS, LANES, LANES), F32),
                        pltpu.VMEM((HEADS, LANES, LANES), F32),
                        pltpu.VMEM((LANES, width), F32)],
        compiler_params=_cparams(("parallel", "arbitrary")),
        name="attn_prompt",
    )(bias_far, z3, kv_bf, kv_bf, mask, bias_tiles)


QW = 256
SUB = 8


def _row_all(x8, op):
    return jnp.broadcast_to(op(x8, axis=0, keepdims=True), x8.shape)


def _count_t(u_ref, n_groups, group, thr_row, cmp):
    qw = u_ref.shape[2]
    thr = jnp.broadcast_to(thr_row, (SUB, qw))

    def body(gi, acc):
        for i in range(group):
            u = u_ref[gi * group + i].reshape(LANES // SUB, SUB, qw)
            acc = acc + jnp.sum(cmp(u, thr[None]).astype(F32), axis=0)
        return acc

    acc = lax.fori_loop(0, n_groups, body, jnp.zeros((SUB, qw), F32))
    return jnp.sum(acc, axis=0, keepdims=True)


def _topk_mask_t(score_ref, u_ref, write_tile, n_tiles, n_groups, group, k):
    qw = score_ref.shape[2]
    n_live = n_groups * group

    def to_key(t, carry):
        bits = pltpu.bitcast(score_ref[t], I32)
        u_ref[t] = jnp.where(bits < 0, bits ^ np.int32(0x7FFFFFFF), bits)
        return carry

    lax.fori_loop(0, n_live, to_key, 0)

    kf = float(k)
    ge = lambda u, t: u >= t
    cnt = _count_t(u_ref, n_groups, group, jnp.zeros((1, qw), I32), ge)
    lo = jnp.where(cnt >= kf, np.int32(0), _INT_MIN)

    def bit_step(i, lo):
        cand = lo | (np.int32(1) << (30 - i))
        cnt = _count_t(u_ref, n_groups, group, cand, ge)
        return jnp.where(cnt >= kf, cand, lo)

    lo = lax.fori_loop(0, 31, bit_step, lo)
    c_gt = _count_t(u_ref, n_groups, group, lo, lambda u, t: u > t)
    c_eq = _count_t(u_ref, n_groups, group, lo, lambda u, t: u == t)
    need = kf - c_gt
    excess = jnp.where((lo > _KEY_NEG_INF) & (c_eq > need), 1.0, 0.0)
    any_excess = jnp.max(excess) > 0.0
    lo_b = jnp.broadcast_to(lo, (LANES, qw))

    @pl.when(jnp.logical_not(any_excess))
    def _():
        def emit(t, carry):
            u = u_ref[t]
            write_tile(t, jnp.where((u >= lo_b) & (u > _KEY_NEG_INF), 1.0, 0.0))
            return carry

        lax.fori_loop(0, n_live, emit, 0)

    @pl.when(any_excess)
    def _():
        lower = (lax.broadcasted_iota(I32, (LANES, LANES), 1)
                 <= lax.broadcasted_iota(I32, (LANES, LANES), 0)).astype(BF)

        def emit(t, seen):
            u = u_ref[t]
            eq = jnp.where(u == lo_b, 1.0, 0.0)
            prefix = seen + _dot(lower, eq.astype(BF))
            take = (u > lo_b) | ((u == lo_b) & (prefix <= need))
            write_tile(t, jnp.where(take & (u > _KEY_NEG_INF), 1.0, 0.0))
            return seen + jnp.sum(eq, axis=0, keepdims=True)

        lax.fori_loop(0, n_live, emit, jnp.zeros((1, qw), F32))

    def clear(t, carry):
        write_tile(t, jnp.zeros((LANES, qw), F32))
        return carry

    lax.fori_loop(n_live, n_tiles, clear, 0)


def _indexer_t_body(qi_ref, wt_ref, tail_ref, mask_ref, score_ref, u_ref, *, seq):
    j = pl.program_id(1)
    n_tiles = seq // LANES
    tiles_per_chunk = IDX_KCHUNK // LANES
    qi = qi_ref[0].astype(BF)
    wt = wt_ref[0] * (IDX_DIM ** -0.5 * IDX_HEADS ** -0.5)

    n_chunks = (j * QW + QW - 1) // IDX_KCHUNK + 1
    q_pos = j * QW + lax.broadcasted_iota(I32, (IDX_KCHUNK, QW), 1)

    def chunk_step(ci, carry):
        k0 = pl.multiple_of(ci * IDX_KCHUNK, IDX_KCHUNK)
        kic = tail_ref[0, pl.ds(k0, IDX_KCHUNK), :][:, 0:IDX_DIM].astype(BF)
        acc = jnp.zeros((IDX_KCHUNK, QW), F32)
        for h in range(IDX_HEADS):
            s = _dot_nt(kic, qi[:, h * IDX_DIM:(h + 1) * IDX_DIM])
            acc = acc + wt[h:h + 1] * jnp.maximum(s, 0.0)
        k_pos = k0 + lax.broadcasted_iota(I32, (IDX_KCHUNK, QW), 0)
        acc = jnp.where(k_pos <= q_pos, acc, -jnp.inf)
        for i in range(tiles_per_chunk):
            score_ref[ci * tiles_per_chunk + i] = acc[i * LANES:(i + 1) * LANES]
        return carry

    lax.fori_loop(0, n_chunks, chunk_step, 0)

    def write_tile(t, m):
        mask_ref[0, 0, t] = m.astype(BF)

    _topk_mask_t(score_ref, u_ref, write_tile, n_tiles, n_chunks, tiles_per_chunk, TOPK)


def indexer_prompt_t(z3, wt):
    bsz, seq = z3.shape[:2]
    n_tiles = seq // LANES
    tail_block = z3.shape[2] // LANES - 1
    assert seq % QW == 0 and seq % IDX_KCHUNK == 0
    return pl.pallas_call(
        functools.partial(_indexer_t_body, seq=seq),
        grid=(bsz, seq // QW),
        in_specs=[pl.BlockSpec((1, QW, IDX_HEADS * IDX_DIM), lambda b, j: (b, j, 7)),
                  pl.BlockSpec((1, IDX_HEADS, QW), lambda b, j: (b, 0, j)),
                  pl.BlockSpec((1, seq, LANES), lambda b, j: (b, 0, tail_block))],
        out_specs=pl.BlockSpec((1, 1, n_tiles, LANES, QW), lambda b, j: (b, j, 0, 0, 0)),
        out_shape=jax.ShapeDtypeStruct((bsz, seq // QW, n_tiles, LANES, QW), BF),
        scratch_shapes=[pltpu.VMEM((n_tiles, LANES, QW), F32),
                        pltpu.VMEM((n_tiles, LANES, QW), I32)],
        compiler_params=_cparams(("parallel", "arbitrary")),
        name="indexer_prompt_t",
    )(z3, wt, z3)


def _attn_t_body(bfar_ref, q_ref, k_ref, vt_ref, mask_ref, bias_ref, o_ref, m_ref, l_ref, acc_ref):
    j = pl.program_id(1)
    scale = HEAD_DIM ** -0.5
    qsub = QW // LANES
    q = q_ref[0].astype(BF)
    m_ref[...] = jnp.full(m_ref.shape, NEG_BIG, F32)
    l_ref[...] = jnp.zeros(l_ref.shape, F32)
    acc_ref[...] = jnp.zeros(acc_ref.shape, F32)

    def process(sb, near):
        k0 = pl.multiple_of(sb * ATTN_KSTEP, ATTN_KSTEP)
        kblk = k_ref[0, pl.ds(k0, ATTN_KSTEP), :]
        tiles = [sb * ATTN_TILES + i for i in range(ATTN_TILES)]
        sel = jnp.concatenate([mask_ref[0, 0, t] for t in tiles], axis=0).astype(F32) > 0.0
        if near:
            def tile_bias(t, h):
                row = []
                for s in range(qsub):
                    d = j * qsub + s - t
                    row.append(bias_ref[jnp.where(d == 0, 0, jnp.where(d == 1, 1, 2)), h])
                return jnp.concatenate(row, axis=1)
        for h in range(HEADS):
            cols = slice(h * HEAD_DIM, (h + 1) * HEAD_DIM)
            lg = _dot_nt(kblk[:, cols], q[:, cols]) * scale
            if near:
                lg = lg + jnp.concatenate([tile_bias(t, h) for t in tiles], axis=0)
            else:
                lg = lg + bfar_ref[h]
            lg = jnp.where(sel, lg, NEG_BIG)
            m_old = m_ref[h]
            part = jnp.max(lg.reshape(ATTN_KSTEP // SUB, SUB, QW), axis=0)
            m_new = jnp.maximum(m_old, _row_all(part, jnp.max))
            p = jnp.exp(lg - m_new[0:1])
            alpha = jnp.exp(m_old - m_new)
            psum = jnp.sum(p.reshape(ATTN_KSTEP // SUB, SUB, QW), axis=0)
            l_ref[h] = alpha * l_ref[h] + _row_all(psum, jnp.sum)
            pv = _dot(vt_ref[0, sb, cols, :], p.astype(BF))
            acc_ref[cols, :] = alpha[0:1] * acc_ref[cols, :] + pv
            m_ref[h] = m_new

    n_far = jnp.maximum((qsub * j - 1) // ATTN_TILES, 0)
    n_steps = (qsub * j + qsub - 1) // ATTN_TILES + 1

    def far_step(sb, carry):
        process(sb, False)
        return carry

    def near_step(sb, carry):
        process(sb, True)
        return carry

    lax.fori_loop(0, n_far, far_step, 0)
    lax.fori_loop(n_far, n_steps, near_step, 0)

    for h in range(HEADS):
        cols = slice(h * HEAD_DIM, (h + 1) * HEAD_DIM)
        o_ref[0, cols, :] = acc_ref[cols, :] / l_ref[h][0:1]


def attn_prompt_t(z3, k_bf, vt_bf, mask_t, bias_tiles_t, bias_far):
    bsz, seq = z3.shape[:2]
    n_tiles = seq // LANES
    width = HEADS * HEAD_DIM
    assert seq % ATTN_KSTEP == 0 and seq % QW == 0
    return pl.pallas_call(
        _attn_t_body,
        grid=(bsz, seq // QW),
        in_specs=[pl.BlockSpec(memory_space=pltpu.SMEM),
                  pl.BlockSpec((1, QW, width), lambda b, j: (b, j, 4)),
                  pl.BlockSpec((1, seq, width), lambda b, j: (b, 0, 0)),
                  pl.BlockSpec((1, seq // ATTN_KSTEP, width, ATTN_KSTEP), lambda b, j: (b, 0, 0, 0)),
                  pl.BlockSpec((1, 1, n_tiles, LANES, QW), lambda b, j: (b, j, 0, 0, 0)),
                  pl.BlockSpec(bias_tiles_t.shape, lambda b, j: (0, 0, 0, 0))],
        out_specs=pl.BlockSpec((1, width, QW), lambda b, j: (b, 0, j)),
        out_shape=jax.ShapeDtypeStruct((bsz, width, seq), F32),
        scratch_shapes=[pltpu.VMEM((HEADS, SUB, QW), F32),
                        pltpu.VMEM((HEADS, SUB, QW), F32),
                        pltpu.VMEM((width, QW), F32)],
        compiler_params=_cparams(("parallel", "arbitrary")),
        name="attn_prompt_t",
    )(bias_far, z3, k_bf, vt_bf, mask_t, bias_tiles_t)


def _indexer_sample_body(pt_ref, qi_ref, wcol_ref, kinew_ref, *rest, n_pages):
    ki_refs = rest[:n_pages]
    out_ref = rest[n_pages]
    qi = qi_ref[0].astype(BF)
    w = wcol_ref[0] * (IDX_DIM ** -0.5 * IDX_HEADS ** -0.5)
    for i in range(n_pages):
        s = _dot(qi, ki_refs[i][0].astype(BF))
        out_ref[i, 0] = jnp.sum(w * jnp.maximum(s, 0.0), axis=0, keepdims=True)
    kn = kinew_ref[0].astype(BF).astype(F32)
    sn = jnp.sum(qi.astype(F32) * kn, axis=-1, keepdims=True)
    new = jnp.sum(w * jnp.maximum(sn, 0.0), axis=0, keepdims=True)
    lane = lax.broadcasted_iota(I32, (1, LANES), 1)
    out_ref[n_pages, 0] = jnp.where(lane == 0, new, -jnp.inf)


def indexer_sample(page_table, qi3, wcol, ki_new, ki_pool_t):
    bsz, n_pages = page_table.shape
    ki_spec = lambda i: pl.BlockSpec((1, IDX_DIM, PAGE), lambda b, pt, i=i: (pt[b, i], 0, 0))
    grid_spec = pltpu.PrefetchScalarGridSpec(
        num_scalar_prefetch=1,
        grid=(bsz,),
        in_specs=[pl.BlockSpec((1, IDX_HEADS, IDX_DIM), lambda b, pt: (b, 0, 0)),
                  pl.BlockSpec((1, IDX_HEADS, 1), lambda b, pt: (b, 0, 0)),
                  pl.BlockSpec((1, 1, IDX_DIM), lambda b, pt: (b, 0, 0))]
                 + [ki_spec(i) for i in range(n_pages)],
        out_specs=pl.BlockSpec((n_pages + 1, 1, 1, LANES), lambda b, pt: (0, b, 0, 0)),
    )
    out = pl.pallas_call(
        functools.partial(_indexer_sample_body, n_pages=n_pages),
        grid_spec=grid_spec,
        out_shape=jax.ShapeDtypeStruct((n_pages + 1, bsz, 1, LANES), F32),
        compiler_params=_cparams(("arbitrary",)),
        name="indexer_sample",
    )(page_table, qi3, wcol, ki_new, *([ki_pool_t] * n_pages))
    return out.reshape(n_pages + 1, bsz, LANES)


def _select_body(score_ref, mask_ref, u_ref, *, n_tiles, k):
    def write_tile(t, m):
        mask_ref[t] = m

    _topk_mask(score_ref, u_ref, write_tile, n_tiles, n_tiles, 1, k)


def select_topk(scores, k):
    n_tiles, rows, _ = scores.shape
    return pl.pallas_call(
        functools.partial(_select_body, n_tiles=n_tiles, k=k),
        grid=(1,),
        in_specs=[pl.BlockSpec(scores.shape, lambda i: (0, 0, 0))],
        out_specs=pl.BlockSpec(scores.shape, lambda i: (0, 0, 0)),
        out_shape=jax.ShapeDtypeStruct(scores.shape, F32),
        scratch_shapes=[pltpu.VMEM(scores.shape, I32)],
        compiler_params=_cparams(("arbitrary",)),
        name="select_topk",
    )(scores)


SC_CORES = 2
SC_SUBCORES = 16
SC_GATHER_CHUNK = 32


def sc_gather_rows(tables, idx):
    n_rows = idx.shape[0]
    workers = SC_CORES * SC_SUBCORES
    per_worker = n_rows // workers
    chunk = SC_GATHER_CHUNK
    assert n_rows % workers == 0 and per_worker % chunk == 0 and chunk % 8 == 0
    row_shape = tables[0].shape[1:]
    n_tab = len(tables)
    mesh = plsc.VectorSubcoreMesh(core_axis_name="c", subcore_axis_name="s",
                                  num_cores=SC_CORES, num_subcores=SC_SUBCORES)

    def body(*refs):
        tab_refs = refs[:n_tab]
        idx_hbm = refs[n_tab]
        out_refs = refs[n_tab + 1:2 * n_tab + 1]
        idx_v, rows_v, sem = refs[2 * n_tab + 1:]
        wid = lax.axis_index("s") * SC_CORES + lax.axis_index("c")

        @pl.loop(0, per_worker // chunk)
        def _(ci):
            off = pl.multiple_of(wid * per_worker + ci * chunk, 8)
            pltpu.sync_copy(idx_hbm.at[pl.ds(off, chunk)], idx_v)
            for tab, out in zip(tab_refs, out_refs):
                pltpu.async_copy(tab.at[idx_v], rows_v, sem).wait()
                pltpu.sync_copy(rows_v, out.at[pl.ds(off, chunk)])

    return pl.kernel(
        body,
        out_type=[jax.ShapeDtypeStruct((n_rows, *row_shape), t.dtype) for t in tables],
        mesh=mesh,
        scratch_types=[pltpu.VMEM((chunk,), I32),
                       pltpu.VMEM((chunk, *row_shape), tables[0].dtype),
                       pltpu.SemaphoreType.DMA],
        compiler_params=pltpu.CompilerParams(use_tc_tiling_on_sc=True),
        name="sc_gather_rows",
    )(*tables, idx)


SAMPLE_SEQ_GROUP = 4


def _attn_compact_body(dist_ref, near_ref, q_ref, kn_ref, vn_ref, btab_ref, kc_ref, vc_ref,
                       o_ref, bbuf):
    step = pl.program_id(0)
    scale = HEAD_DIM ** -0.5
    ones = jnp.ones((HEAD_DIM, LANES), BF)
    far_bias = btab_ref[REL_MAX_DIST]
    last = lax.broadcasted_iota(I32, (TOPK, 1, 1), 0) == TOPK - 1
    for g in range(SAMPLE_SEQ_GROUP):
        b = step * SAMPLE_SEQ_GROUP + g
        bbuf[g] = jnp.broadcast_to(far_bias[None], (TOPK, HEADS, LANES))

        def fill(t, carry, g=g, b=b):
            bbuf[g, t] = btab_ref[dist_ref[b, t]]
            return carry

        lax.fori_loop(near_ref[b], TOPK, fill, 0)

        is_new = dist_ref[b, TOPK - 1] == 0
        rows = slice(g * TOPK, (g + 1) * TOPK)
        rnd = lambda a: a.astype(BF).astype(F32)
        kc = rnd(jnp.where(last & is_new, kn_ref[g][None], kc_ref[rows]))
        vc = rnd(jnp.where(last & is_new, vn_ref[g][None], vc_ref[rows]))
        prod = (kc * rnd(q_ref[g])[None]).reshape(TOPK * HEADS, HEAD_DIM)
        hi = prod.astype(BF)
        lo = (prod - hi.astype(F32)).astype(BF)
        lg = (_dot(hi, ones) + _dot(lo, ones)).reshape(TOPK, HEADS, LANES) * scale + bbuf[g]
        m = jnp.max(lg, axis=0, keepdims=True)
        p = jnp.exp(lg - m)
        l = jnp.sum(p, axis=0)
        o_ref[g] = jnp.sum(rnd(p) * vc, axis=0) / l


def attn_sample_compact(dist, first_near, zs3, k_sel, v_sel, bias_by_dist):
    bsz = dist.shape[0]
    g = SAMPLE_SEQ_GROUP
    assert bsz % g == 0
    zspec = lambda cb: pl.BlockSpec((g, HEADS, HEAD_DIM), lambda i, d, n, cb=cb: (i, cb, 0))
    sel_spec = pl.BlockSpec((g * TOPK, HEADS, HEAD_DIM), lambda i, d, n: (i, 0, 0))
    grid_spec = pltpu.PrefetchScalarGridSpec(
        num_scalar_prefetch=2,
        grid=(bsz // g,),
        in_specs=[zspec(4), zspec(5), zspec(6),
                  pl.BlockSpec(bias_by_dist.shape, lambda i, d, n: (0, 0, 0)),
                  sel_spec, sel_spec],
        out_specs=pl.BlockSpec((g, HEADS, HEAD_DIM), lambda i, d, n: (i, 0, 0)),
        scratch_shapes=[pltpu.VMEM((g, TOPK, HEADS, HEAD_DIM), F32)],
    )
    return pl.pallas_call(
        _attn_compact_body,
        grid_spec=grid_spec,
        out_shape=jax.ShapeDtypeStruct((bsz, HEADS, HEAD_DIM), F32),
        compiler_params=_cparams(("arbitrary",)),
        name="attn_sample_compact",
    )(dist, first_near, zs3, zs3, zs3, bias_by_dist, k_sel, v_sel)


def _bucket_table(max_dist):
    exact = REL_BUCKETS // 2
    d = np.arange(max_dist + 1)
    df = np.maximum(d, 1).astype(np.float32)
    far = exact + (np.log(df / exact) / np.float32(math.log(REL_MAX_DIST / exact))
                   * (REL_BUCKETS - exact)).astype(np.int32)
    return np.where(d < exact, d, np.minimum(far, REL_BUCKETS - 1)).astype(np.int32)


def _bias_tables(rel_bias):
    tab = _bucket_table(2 * LANES)
    assert np.all(tab[REL_MAX_DIST:] == REL_BUCKETS - 1)
    i = np.arange(LANES)
    dist0 = np.maximum(i[:, None] - i[None, :], 0)
    dist1 = LANES + i[:, None] - i[None, :]
    far = np.full((LANES, LANES), REL_BUCKETS - 1)
    idx = np.stack([tab[dist0], tab[dist1], far])
    buckets = jnp.arange(REL_BUCKETS)
    lookup = lambda ix, spec: jnp.einsum(spec, (jnp.asarray(ix)[..., None] == buckets).astype(F32),
                                         rel_bias.astype(F32), precision=lax.Precision.HIGHEST)
    tiles = lookup(idx, "tijb,bh->thij")
    by_dist = jnp.broadcast_to(lookup(tab[:REL_MAX_DIST + 1], "db,bh->dh")[:, :, None],
                               (REL_MAX_DIST + 1, HEADS, LANES))
    return tiles.astype(F32), rel_bias[REL_BUCKETS - 1].astype(F32), by_dist.astype(F32)


def _conv_prompt_body(bg_ref, cg_ref, xt_ref, w_ref, v_ref, st_ref, carry_ref, *, tblock):
    t = pl.program_id(1)

    @pl.when(t == 0)
    def _():
        carry_ref[...] = jnp.zeros_like(carry_ref)

    u = cg_ref[0] * xt_ref[0]
    row = lax.broadcasted_iota(I32, u.shape, 0)
    c0 = carry_ref[0:1]
    c1 = carry_ref[1:2]
    u1 = jnp.where(row == 0, c1, pltpu.roll(u, 1, axis=0))
    u2 = jnp.where(row == 0, c0, jnp.where(row == 1, c1, pltpu.roll(u, 2, axis=0)))
    conv = w_ref[0:1] * u2 + w_ref[1:2] * u1 + w_ref[2:3] * u
    v_ref[0] = bg_ref[0] * conv
    last = u[tblock - 2:tblock]
    carry_ref[0:2] = last

    @pl.when(t == pl.num_programs(1) - 1)
    def _():
        st_ref[0] = last


def conv_prompt(zc3, w_conv, tblock=256):
    bsz, seq = zc3.shape[:2]
    c = zc3.shape[2] // 3
    zspec = lambda cb: pl.BlockSpec((1, tblock, c), lambda b, t, cb=cb: (b, t, cb))
    return pl.pallas_call(
        functools.partial(_conv_prompt_body, tblock=tblock),
        grid=(bsz, seq // tblock),
        in_specs=[zspec(0), zspec(1), zspec(2), pl.BlockSpec((3, c), lambda b, t: (0, 0))],
        out_specs=[pl.BlockSpec((1, tblock, c), lambda b, t: (b, t, 0)),
                   pl.BlockSpec((1, 2, c), lambda b, t: (b, 0, 0))],
        out_shape=[jax.ShapeDtypeStruct((bsz, seq, c), F32),
                   jax.ShapeDtypeStruct((bsz, 2, c), F32)],
        scratch_shapes=[pltpu.VMEM((8, c), F32)],
        compiler_params=_cparams(("parallel", "arbitrary")),
        name="conv_prompt",
    )(zc3, zc3, zc3, w_conv)


def _conv_sample_body(bg_ref, cg_ref, xt_ref, w_ref, s0_ref, s1_ref, v_ref, n0_ref, n1_ref):
    u = cg_ref[...] * xt_ref[...]
    conv = w_ref[0:1] * s0_ref[...] + w_ref[1:2] * s1_ref[...] + w_ref[2:3] * u
    v_ref[...] = bg_ref[...] * conv
    n0_ref[...] = s1_ref[...]
    n1_ref[...] = u


def conv_sample(zc, w_conv, s0, s1):
    bsz = zc.shape[0]
    c = zc.shape[1] // 3
    zspec = lambda cb: pl.BlockSpec((bsz, c), lambda i, cb=cb: (0, cb))
    full = pl.BlockSpec((bsz, c), lambda i: (0, 0))
    return pl.pallas_call(
        _conv_sample_body,
        grid=(1,),
        in_specs=[zspec(0), zspec(1), zspec(2), pl.BlockSpec((3, c), lambda i: (0, 0)), full, full],
        out_specs=[full, full, full],
        out_shape=[jax.ShapeDtypeStruct((bsz, c), F32)] * 3,
        compiler_params=_cparams(("arbitrary",)),
        name="conv_sample",
    )(zc, zc, zc, w_conv, s0, s1)


def _router_body(x_ref, g_ref, wr_ref, br_ref, h_ref, route_ref):
    x = x_ref[...]
    ms = jnp.mean(x * x, axis=-1, keepdims=True)
    h = x * lax.rsqrt(ms + RMS_EPS) * g_ref[...]
    hb = h.astype(BF)
    half = h.shape[1] // 2
    bits = pltpu.bitcast(hb.astype(F32), jnp.uint32)
    h_ref[...] = (bits[:, :half] & jnp.uint32(0xFFFF0000)) | (bits[:, half:] >> 16)
    logits = _dot(hb, wr_ref[...]) + br_ref[...]
    lane = lax.broadcasted_iota(I32, logits.shape, 1)
    big = np.int32(1 << 20)
    lg = jnp.where(lane < N_GROUPS, logits, -jnp.inf)
    g_max = jnp.max(lg, axis=-1, keepdims=True)
    g_idx = jnp.min(jnp.where(lg == g_max, lane, big), axis=-1, keepdims=True)
    g_w = 1.0 / jnp.sum(jnp.exp(lg - g_max), axis=-1, keepdims=True)
    first = N_GROUPS + EXP_PER_GROUP * g_idx
    le = jnp.where((lane >= first) & (lane < first + EXP_PER_GROUP), logits, -jnp.inf)
    l1 = jnp.max(le, axis=-1, keepdims=True)
    i1 = jnp.min(jnp.where(le == l1, lane, big), axis=-1, keepdims=True)
    le2 = jnp.where(lane == i1, -jnp.inf, le)
    l2 = jnp.max(le2, axis=-1, keepdims=True)
    i2 = jnp.min(jnp.where(le2 == l2, lane, big), axis=-1, keepdims=True)
    r = jnp.exp(l2 - l1)
    w1 = g_w / (1.0 + r)
    w2 = g_w * r / (1.0 + r)
    e1 = (i1 - N_GROUPS).astype(F32)
    e2 = (i2 - N_GROUPS).astype(F32)
    route_ref[...] = jnp.where(lane == 0, e1, jnp.where(lane == 1, e2,
                               jnp.where(lane == 2, w1, jnp.where(lane == 3, w2, 0.0))))


def moe_router(x, g, wr, br, tm):
    m, k = x.shape
    return pl.pallas_call(
        _router_body,
        grid=(m // tm,),
        in_specs=[pl.BlockSpec((tm, k), lambda i: (i, 0)),
                  pl.BlockSpec((1, k), lambda i: (0, 0)),
                  pl.BlockSpec((k, LANES), lambda i: (0, 0)),
                  pl.BlockSpec((1, LANES), lambda i: (0, 0))],
        out_specs=[pl.BlockSpec((tm, k // 2), lambda i: (i, 0)),
                   pl.BlockSpec((tm, LANES), lambda i: (i, 0))],
        out_shape=[jax.ShapeDtypeStruct((m, k // 2), jnp.uint32),
                   jax.ShapeDtypeStruct((m, LANES), F32)],
        compiler_params=_cparams(("parallel",)),
        name="moe_router",
    )(x, g.reshape(1, k), wr, br)


def _experts_body(te_ref, nu_ref, h_ref, wg_ref, wu_ref, wd_ref, o_ref, wg_bf, wu_bf, wd_bf):
    i = pl.program_id(0)

    @pl.when((i == 0) | (te_ref[i] != te_ref[jnp.maximum(i - 1, 0)]))
    def _():
        wg_bf[...] = wg_ref[0].astype(BF)
        wu_bf[...] = wu_ref[0].astype(BF)
        wd_bf[...] = wd_ref[0].astype(BF)

    @pl.when(i < nu_ref[0])
    def _():
        words = h_ref[...]
        left = pltpu.bitcast(words & jnp.uint32(0xFFFF0000), F32)
        right = pltpu.bitcast(words << 16, F32)
        h = jnp.concatenate([left, right], axis=1).astype(BF)
        a = _dot(h, wg_bf[...])
        b = _dot(h, wu_bf[...])
        hid = a * _sigmoid(a) * b
        o_ref[...] = _dot(hid.astype(BF), wd_bf[...])

    @pl.when(i >= nu_ref[0])
    def _():
        o_ref[...] = jnp.zeros_like(o_ref)


def moe_experts(tile_expert, n_used, hs, wg, wu, wd):
    p = hs.shape[0]
    k = wg.shape[1]
    f = wg.shape[2]
    n_tiles = p // MOE_TILE
    grid_spec = pltpu.PrefetchScalarGridSpec(
        num_scalar_prefetch=2,
        grid=(n_tiles,),
        in_specs=[pl.BlockSpec((MOE_TILE, k // 2), lambda i, te, nu: (i, 0)),
                  pl.BlockSpec((1, k, f), lambda i, te, nu: (te[i], 0, 0)),
                  pl.BlockSpec((1, k, f), lambda i, te, nu: (te[i], 0, 0)),
                  pl.BlockSpec((1, f, k), lambda i, te, nu: (te[i], 0, 0))],
        out_specs=pl.BlockSpec((MOE_TILE, k), lambda i, te, nu: (i, 0)),
        scratch_shapes=[pltpu.VMEM((k, f), BF), pltpu.VMEM((k, f), BF), pltpu.VMEM((f, k), BF)],
    )
    return pl.pallas_call(
        _experts_body,
        grid_spec=grid_spec,
        out_shape=jax.ShapeDtypeStruct((p, k), F32),
        compiler_params=_cparams(("arbitrary",)),
        name="moe_experts",
    )(tile_expert, n_used, hs, wg, wu, wd)


def _rank_within_expert(onehot):
    n, e = onehot.shape
    blk = LANES
    assert n % blk == 0
    oh = onehot.astype(F32).reshape(n // blk, blk, e)
    strict = jnp.asarray(np.tril(np.ones((blk, blk), np.float32), -1))
    within = jnp.einsum("ij,bjk->bik", strict, oh, precision=lax.Precision.HIGHEST)
    totals = jnp.sum(oh, axis=1)
    before = jnp.cumsum(totals, axis=0) - totals
    rank = (within + before[:, None, :]).reshape(n, e)
    return jnp.sum(rank * onehot.astype(F32), axis=1).astype(I32), jnp.sum(totals, axis=0).astype(I32)


def hier_moe(xs, tms, g, wrg, brg, wre, bre, wg, wu, wd, layer):
    d = xs[0].shape[1]
    m = sum(x.shape[0] for x in xs)
    wr = jnp.zeros((d, LANES), F32)
    wr = wr.at[:, :N_GROUPS].set(wrg).at[:, N_GROUPS:N_GROUPS + N_EXPERTS].set(wre.reshape(d, N_EXPERTS))
    br = jnp.zeros((1, LANES), F32)
    br = br.at[0, :N_GROUPS].set(brg).at[0, N_GROUPS:N_GROUPS + N_EXPERTS].set(bre.reshape(N_EXPERTS))
    routed = [moe_router(x, g, wr.astype(BF), br, tm) for x, tm in zip(xs, tms)]
    h_bf = jnp.concatenate([r[0] for r in routed], axis=0)
    route = jnp.concatenate([r[1] for r in routed], axis=0)

    eid = route[:, 0:2].astype(I32).reshape(-1)
    onehot = eid[:, None] == jnp.arange(N_EXPERTS, dtype=I32)[None, :]
    rank, counts = _rank_within_expert(onehot)
    padded = ((counts + MOE_TILE - 1) // MOE_TILE) * MOE_TILE
    ends = jnp.cumsum(padded)
    pos = jnp.sum(jnp.where(onehot, (ends - padded)[None, :], 0), axis=1) + rank
    n_rows = 2 * m + N_EXPERTS * MOE_TILE
    n_rows = -(-n_rows // MOE_TILE) * MOE_TILE
    token = jnp.zeros((n_rows,), I32).at[pos].set(jnp.arange(2 * m, dtype=I32) // 2)
    tile_start = jnp.arange(n_rows // MOE_TILE, dtype=I32) * MOE_TILE
    tile_expert = jnp.minimum(jnp.sum(tile_start[:, None] >= ends[None, :], axis=1),
                              N_EXPERTS - 1).astype(I32)
    n_used = (ends[-1] // MOE_TILE).astype(I32).reshape(1)

    hs = jnp.take(h_bf, token, axis=0, mode="clip")
    out = moe_experts(tile_expert + layer * N_EXPERTS, n_used, hs, wg, wu, wd)
    pos2 = pos.reshape(m, 2)
    res, r0 = [], 0
    for x in xs:
        rows = slice(r0, r0 + x.shape[0])
        res.append(x + route[rows, 2:3] * jnp.take(out, pos2[rows, 0], axis=0, mode="clip")
                   + route[rows, 3:4] * jnp.take(out, pos2[rows, 1], axis=0, mode="clip"))
        r0 += x.shape[0]
    return res


def kernel(x_prompt, x_sample, cache_k, cache_v, cache_ki, state_hgrn, state_conv, page_table,
           norm_mix_g, norm_ffn_g, final_g, w_in_even, w_out_even, hgrn_lb_logits, hgrn_norm_g,
           rel_bias, w_in_conv, w_conv, w_out_conv, w_router_g, b_router_g, w_router_e,
           b_router_e, w_gate, w_up, w_down):
    bsz, seq, d = x_prompt.shape
    dec = x_sample.shape[0]
    n_p = bsz * seq
    tm_p, tm_s = 512, 128
    tms = (tm_p, tm_s)
    assert n_p % tm_p == 0 and dec % tm_s == 0 and x_sample.shape[1] == 1
    width = HEADS * HEAD_DIM
    even_in = w_in_even.shape[2]
    even_pad = -(-even_in // LANES) * LANES
    tail0 = even_pad - LANES
    n_pool = cache_k.shape[1]

    xp = x_prompt.reshape(n_p, d)
    xs = x_sample.reshape(dec, d)

    lbs = jnp.cumsum(jax.nn.softmax(hgrn_lb_logits.astype(F32), axis=0), axis=0)[:-1]
    bias_tiles, bias_far, bias_by_dist = _bias_tables(rel_bias)
    expert_w = [w.reshape(-1, *w.shape[2:]) for w in (w_gate, w_up, w_down)]

    w_in = jnp.pad(w_in_even[0], ((0, 0), (0, even_pad - even_in))).astype(BF)
    zp = norm_matmul(xp, norm_mix_g[0], w_in, 2 * tm_p, 640)
    zs = norm_matmul(xs, norm_mix_g[0], w_in, tm_s, 640)
    zp3 = zp.reshape(bsz, seq, even_pad)
    zs3 = zs.reshape(dec, even_pad // LANES, LANES)

    oa_p, hgrn_p = hgrn_prompt(zp3, lbs[0], hgrn_norm_g[0])
    oa_s, hgrn_s = hgrn_sample(zs3, lbs[0], hgrn_norm_g[0], state_hgrn[0])

    wt = jnp.swapaxes(zp3[:, :, tail0 + IDX_DIM:tail0 + IDX_DIM + IDX_HEADS], 1, 2)
    mask_p = indexer_prompt_t(zp3, wt)
    k_bf = zp3[:, :, 5 * width:6 * width].astype(BF)
    vt_bf = jnp.swapaxes(zp3[:, :, 6 * width:7 * width].astype(BF)
                         .reshape(bsz, seq // ATTN_KSTEP, ATTN_KSTEP, width), 2, 3)
    ob_p = jnp.swapaxes(attn_prompt_t(zp3, k_bf, vt_bf, mask_p,
                                      jnp.swapaxes(bias_tiles, -1, -2), bias_far), 1, 2)

    qi3 = zs[:, 7 * width:8 * width].reshape(dec, IDX_HEADS, IDX_DIM)
    ki_new = zs[:, tail0:tail0 + IDX_DIM].reshape(dec, 1, IDX_DIM)
    wcol = zs[:, tail0 + IDX_DIM:tail0 + IDX_DIM + IDX_HEADS].reshape(dec, IDX_HEADS, 1)
    pages = page_table + 0 * n_pool
    scores_s = indexer_sample(pages, qi3, wcol, ki_new,
                              jnp.swapaxes(cache_ki, -1, -2).reshape(-1, IDX_DIM, PAGE))
    mask_s = select_topk(scores_s, TOPK)
    sel = jnp.transpose(mask_s, (1, 0, 2))
    n_pages = page_table.shape[1]
    sel_off = jnp.concatenate([jnp.zeros((dec, 1), F32),
                               jnp.cumsum(jnp.sum(sel, axis=2), axis=1)], axis=1)
    slot = jnp.arange(TOPK, dtype=F32)
    page_of = jnp.sum(sel_off[:, None, 1:] <= slot[None, :, None], axis=2)
    page_1h = (page_of[:, :, None] == jnp.arange(n_pages + 1)[None, None, :]).astype(F32)
    local = slot[None, :] - jnp.einsum("brp,bp->br", page_1h, sel_off[:, :-1],
                                       precision=lax.Precision.HIGHEST)
    within = jnp.einsum("brp,bpl->brl", page_1h, jnp.cumsum(sel, axis=2),
                        precision=lax.Precision.HIGHEST)
    lane_of = jnp.sum(within <= local[:, :, None], axis=2)
    sel_idx = (page_of * PAGE + lane_of).astype(I32)
    past = n_pages * PAGE
    page_id = jnp.einsum("brp,bp->br", page_1h[:, :, :n_pages], pages.astype(F32),
                         precision=lax.Precision.HIGHEST).astype(I32)
    rows = jnp.where(page_of < n_pages, page_id * PAGE + lane_of, 0).astype(I32).reshape(-1)
    k_sel, v_sel = sc_gather_rows([cache_k.reshape(-1, HEADS, HEAD_DIM),
                                   cache_v.reshape(-1, HEADS, HEAD_DIM)], rows)
    dist = jnp.minimum(past - sel_idx, REL_MAX_DIST).astype(I32)
    first_near = jnp.sum(dist >= REL_MAX_DIST, axis=1).astype(I32)
    ob_s = attn_sample_compact(dist, first_near, zs3, k_sel, v_sel, bias_by_dist)

    w_out = w_out_even[0].astype(BF)
    w_out_ab = [w_out[:width], w_out[width:]]
    xp = matmul_residual([oa_p.reshape(n_p, width), ob_p.reshape(n_p, width)], w_out_ab, xp, 2 * tm_p, 512)
    xs = matmul_residual([oa_s.reshape(dec, width), ob_s.reshape(dec, width)], w_out_ab, xs, tm_s, 512)

    xp, xs = hier_moe([xp, xs], tms, norm_ffn_g[0], w_router_g[0], b_router_g[0], w_router_e[0],
                      b_router_e[0], *expert_w, 0)

    w_in_c = w_in_conv[0].astype(BF)
    zcp = norm_matmul(xp, norm_mix_g[1], w_in_c, 2 * tm_p, 1024)
    zcs = norm_matmul(xs, norm_mix_g[1], w_in_c, tm_s, 512)
    cw = zcp.shape[1] // 3
    v_p, conv_p = conv_prompt(zcp.reshape(bsz, seq, 3 * cw), w_conv[0])
    v_s, cs0, cs1 = conv_sample(zcs, w_conv[0], state_conv[0, :, 0], state_conv[0, :, 1])
    w_out_c = [w_out_conv[0].astype(BF)]
    xp = matmul_residual([v_p.reshape(n_p, cw)], w_out_c, xp, 2 * tm_p, 512)
    xs = matmul_residual([v_s], w_out_c, xs, tm_s, 512)

    xp, xs = hier_moe([xp, xs], tms, norm_ffn_g[1], w_router_g[1], b_router_g[1], w_router_e[1],
                      b_router_e[1], *expert_w, 1)

    yp = rmsnorm(xp, final_g, tm_p)
    ys = rmsnorm(xs, final_g, tm_s)

    kcol, vcol = 5 * width, 6 * width
    heads = lambda a, n: a.reshape(1, *n, HEADS, HEAD_DIM)
    return (yp.reshape(bsz, seq, d),
            ys.reshape(dec, 1, d),
            heads(zp[:, kcol:kcol + width], (bsz, seq)),
            heads(zp[:, vcol:vcol + width], (bsz, seq)),
            zp[:, tail0:tail0 + IDX_DIM].reshape(1, bsz, seq, IDX_DIM),
            hgrn_p[None],
            conv_p[None],
            heads(zs[:, kcol:kcol + width], (dec, 1)),
            heads(zs[:, vcol:vcol + width], (dec, 1)),
            zs[:, tail0:tail0 + IDX_DIM].reshape(1, dec, 1, IDX_DIM),
            hgrn_s[None],
            jnp.stack([cs0, cs1], axis=1)[None])
```

```python
import functools
import math

import numpy as np
import jax
import jax.numpy as jnp
from jax import lax
from jax.experimental import pallas as pl
from jax.experimental.pallas import tpu as pltpu
from jax.experimental.pallas import tpu_sc as plsc

F32 = jnp.float32
BF = jnp.bfloat16
I32 = jnp.int32

RMS_EPS = 1e-6
LANES = 128
NEG_BIG = -1e30
VMEM_LIMIT = 56 * 1024 * 1024

D_MODEL = 2048
HEADS = 8
HEAD_DIM = 128
IDX_HEADS = 16
IDX_DIM = 64
TOPK = 256
REL_BUCKETS = 32
REL_MAX_DIST = 128
N_EXPERTS = 16
EXP_PER_GROUP = 4
N_GROUPS = 4
D_EXPERT = 512
PAGE = 128

HGRN_CHUNK = 64
MOE_TILE = 512
IN_PROJ_TN = 768
TAIL_BLOCK = 8 * HEADS * HEAD_DIM // LANES


def _cparams(sem):
    return pltpu.CompilerParams(dimension_semantics=sem, vmem_limit_bytes=VMEM_LIMIT)


def _dot(a, b):
    return jnp.dot(a, b, preferred_element_type=F32)


def _dot_nt(a, b):
    return lax.dot_general(a, b, (((1,), (1,)), ((), ())), preferred_element_type=F32)


def _dot_tn(a, b):
    return lax.dot_general(a, b, (((0,), (0,)), ((), ())), preferred_element_type=F32)


def _sigmoid(x):
    return 1.0 / (1.0 + jnp.exp(-x))


def _norm_mm_body(x_ref, g_ref, w_ref, o_ref, h_ref):
    @pl.when(pl.program_id(1) == 0)
    def _():
        x = x_ref[...]
        ms = jnp.mean(x * x, axis=-1, keepdims=True)
        h_ref[...] = (x * lax.rsqrt(ms + RMS_EPS) * g_ref[...]).astype(BF)

    o_ref[...] = _dot(h_ref[...], w_ref[...])


def norm_matmul(x, g, w_bf, tm, tn):
    m, k = x.shape
    n = w_bf.shape[1]
    return pl.pallas_call(
        _norm_mm_body,
        grid=(m // tm, n // tn),
        in_specs=[pl.BlockSpec((tm, k), lambda i, j: (i, 0)),
                  pl.BlockSpec((1, k), lambda i, j: (0, 0)),
                  pl.BlockSpec((k, tn), lambda i, j: (0, j))],
        out_specs=pl.BlockSpec((tm, tn), lambda i, j: (i, j)),
        out_shape=jax.ShapeDtypeStruct((m, n), F32),
        scratch_shapes=[pltpu.VMEM((tm, k), BF)],
        compiler_params=_cparams(("parallel", "arbitrary")),
        name="norm_matmul",
    )(x, g.reshape(1, k), w_bf)


def _mm_res_body(*refs, n_lhs):
    a_refs = refs[:n_lhs]
    w_refs = refs[n_lhs:2 * n_lhs]
    r_ref = refs[2 * n_lhs]
    o_ref = refs[2 * n_lhs + 1]
    s_refs = refs[2 * n_lhs + 2:]

    @pl.when(pl.program_id(1) == 0)
    def _():
        for a_ref, s_ref in zip(a_refs, s_refs):
            s_ref[...] = a_ref[...].astype(BF)

    acc = r_ref[...]
    for s_ref, w_ref in zip(s_refs, w_refs):
        acc = acc + _dot(s_ref[...], w_ref[...])
    o_ref[...] = acc


def matmul_residual(lhs, ws_bf, res, tm, tn):
    m, n = res.shape
    n_lhs = len(lhs)
    in_specs = ([pl.BlockSpec((tm, a.shape[1]), lambda i, j: (i, 0)) for a in lhs]
                + [pl.BlockSpec((w.shape[0], tn), lambda i, j: (0, j)) for w in ws_bf]
                + [pl.BlockSpec((tm, tn), lambda i, j: (i, j))])
    return pl.pallas_call(
        functools.partial(_mm_res_body, n_lhs=n_lhs),
        grid=(m // tm, n // tn),
        in_specs=in_specs,
        out_specs=pl.BlockSpec((tm, tn), lambda i, j: (i, j)),
        out_shape=jax.ShapeDtypeStruct((m, n), F32),
        scratch_shapes=[pltpu.VMEM((tm, a.shape[1]), BF) for a in lhs],
        compiler_params=_cparams(("parallel", "arbitrary")),
        name="matmul_residual",
    )(*lhs, *ws_bf, res)


def _rmsnorm_body(x_ref, g_ref, o_ref):
    x = x_ref[...]
    ms = jnp.mean(x * x, axis=-1, keepdims=True)
    o_ref[...] = x * lax.rsqrt(ms + RMS_EPS) * g_ref[...]


def rmsnorm(x, g, tm):
    m, k = x.shape
    return pl.pallas_call(
        _rmsnorm_body,
        grid=(m // tm,),
        in_specs=[pl.BlockSpec((tm, k), lambda i: (i, 0)),
                  pl.BlockSpec((1, k), lambda i: (0, 0))],
        out_specs=pl.BlockSpec((tm, k), lambda i: (i, 0)),
        out_shape=jax.ShapeDtypeStruct((m, k), F32),
        compiler_params=_cparams(("parallel",)),
        name="rmsnorm",
    )(x, g.reshape(1, k))


def _hgrn_static(c):
    levels = []
    m = 1
    while m < c:
        levels.append(m)
        m *= 2
    t = np.arange(c)
    rows = [t[None, :] <= t[:, None]]
    masks = [np.eye(c, dtype=bool)]
    for m in levels:
        blk = t // (2 * m)
        pos = t % (2 * m)
        bnd = blk * 2 * m + m - 1
        right = pos >= m
        left = pos < m
        e_rows = (t[None, :] > bnd[:, None]) & (t[None, :] <= t[:, None]) & right[:, None]
        f_rows = (t[None, :] > t[:, None]) & (t[None, :] <= bnd[:, None]) & left[:, None]
        rows.append(e_rows | f_rows)
        masks.append((blk[:, None] == blk[None, :]) & right[:, None] & left[None, :])
    m_all = np.stack(rows).astype(np.float32)
    masks = np.stack(masks).astype(np.float32)
    return m_all, masks, len(levels)


def _hgrn_gates(qa, fa, lb):
    f = lb + (1.0 - lb) * _sigmoid(fa)
    q = qa * _sigmoid(qa)
    return q, f


def _hgrn_prompt_body(qa_ref, fa_ref, ia_ref, ga_ref, lb_ref, gn_ref, mall_ref, masks_ref,
                      oa_ref, st_out_ref, st_ref, *, chunk, tblock, n_levels):
    c = chunk
    t_idx = pl.program_id(1)

    @pl.when(t_idx == 0)
    def _():
        st_ref[...] = jnp.zeros_like(st_ref)

    gn = gn_ref[...]

    def chunk_step(ci, carry):
        r0 = pl.multiple_of(ci * c, c)
        for h in range(HEADS):
            cols = slice(h * HEAD_DIM, (h + 1) * HEAD_DIM)
            qa = qa_ref[0, pl.ds(r0, c), cols]
            fa = fa_ref[0, pl.ds(r0, c), cols]
            ia = ia_ref[0, pl.ds(r0, c), cols]
            ga = ga_ref[0, pl.ds(r0, c), cols]
            q, f = _hgrn_gates(qa, fa, lb_ref[:, cols])
            k = 1.0 - f
            g = jnp.log(f)
            g_hi = g.astype(BF)
            g_lo = (g - g_hi.astype(F32)).astype(BF)
            b = _dot(mall_ref[0], g_hi) + _dot(mall_ref[0], g_lo)
            b_last = b[c - 1:c]
            st = st_ref[h]
            v = ia.astype(BF)
            o = _dot_nt((q * jnp.exp(b)).astype(BF), st.astype(BF))
            a = masks_ref[0] * _dot_nt(q.astype(BF), k.astype(BF))
            for li in range(n_levels):
                ml = mall_ref[1 + li]
                w = jnp.exp(_dot(ml, g_hi) + _dot(ml, g_lo))
                a = a + masks_ref[1 + li] * _dot_nt((q * w).astype(BF), (k * w).astype(BF))
            o = o + _dot(a.astype(BF), v)
            k_st = (k * jnp.exp(b_last - b)).astype(BF)
            st_ref[h] = st * jnp.exp(b_last) + _dot_tn(v, k_st)
            ms = jnp.mean(o * o, axis=-1, keepdims=True)
            y = o * lax.rsqrt(ms + RMS_EPS) * gn * (ga * _sigmoid(ga))
            oa_ref[0, pl.ds(r0, c), cols] = y
        return carry

    lax.fori_loop(0, tblock // c, chunk_step, 0)

    @pl.when(t_idx == pl.num_programs(1) - 1)
    def _():
        st_out_ref[0] = st_ref[...]


def hgrn_prompt(z3, lb, gn, tblock=256, chunk=HGRN_CHUNK):
    bsz, seq = z3.shape[:2]
    width = HEADS * HEAD_DIM
    m_all, masks, n_levels = _hgrn_static(chunk)
    zspec = lambda cb: pl.BlockSpec((1, tblock, width), lambda b, t, cb=cb: (b, t, cb))
    oa, st = pl.pallas_call(
        functools.partial(_hgrn_prompt_body, chunk=chunk, tblock=tblock, n_levels=n_levels),
        grid=(bsz, seq // tblock),
        in_specs=[zspec(0), zspec(1), zspec(2), zspec(3),
                  pl.BlockSpec((1, width), lambda b, t: (0, 0)),
                  pl.BlockSpec((1, HEAD_DIM), lambda b, t: (0, 0)),
                  pl.BlockSpec(m_all.shape, lambda b, t: (0, 0, 0)),
                  pl.BlockSpec(masks.shape, lambda b, t: (0, 0, 0))],
        out_specs=[pl.BlockSpec((1, tblock, width), lambda b, t: (b, t, 0)),
                   pl.BlockSpec((1, HEADS, HEAD_DIM, HEAD_DIM), lambda b, t: (b, 0, 0, 0))],
        out_shape=[jax.ShapeDtypeStruct((bsz, seq, width), F32),
                   jax.ShapeDtypeStruct((bsz, HEADS, HEAD_DIM, HEAD_DIM), F32)],
        scratch_shapes=[pltpu.VMEM((HEADS, HEAD_DIM, HEAD_DIM), F32)],
        compiler_params=_cparams(("parallel", "arbitrary")),
        name="hgrn_prompt",
    )(z3, z3, z3, z3, lb.reshape(1, width), gn.reshape(1, HEAD_DIM),
      jnp.asarray(m_all, BF), jnp.asarray(masks, F32))
    return oa, jnp.swapaxes(st, -1, -2)


def _col(row, eye):
    return jnp.sum(eye * row, axis=1, keepdims=True)


def _hgrn_sample_body(qa_ref, fa_ref, ia_ref, ga_ref, lb_ref, gn_ref, s_ref, oa_ref, so_ref):
    eye = (lax.broadcasted_iota(I32, (HEAD_DIM, HEAD_DIM), 0)
           == lax.broadcasted_iota(I32, (HEAD_DIM, HEAD_DIM), 1)).astype(F32)
    for g in range(SAMPLE_SEQ_GROUP):
        q8, f8 = _hgrn_gates(qa_ref[g], fa_ref[g], lb_ref[...])
        ga = ga_ref[g]
        gate = ga * _sigmoid(ga)
        outs = []
        for h in range(HEADS):
            f_col = _col(f8[h:h + 1], eye)
            q_col = _col(q8[h:h + 1], eye)
            s_new = f_col * s_ref[g, h] + (1.0 - f_col) * ia_ref[g, h:h + 1]
            so_ref[g, h] = s_new
            outs.append(jnp.sum(q_col * s_new, axis=0, keepdims=True))
        o = jnp.concatenate(outs, axis=0)
        ms = jnp.mean(o * o, axis=-1, keepdims=True)
        oa_ref[g] = o * lax.rsqrt(ms + RMS_EPS) * gn_ref[...] * gate


def hgrn_sample(zs3, lb, gn, s0):
    bsz = zs3.shape[0]
    g = SAMPLE_SEQ_GROUP
    assert bsz % g == 0
    zspec = lambda cb: pl.BlockSpec((g, HEADS, HEAD_DIM), lambda b, cb=cb: (b, cb, 0))
    sspec = pl.BlockSpec((g, HEADS, HEAD_DIM, HEAD_DIM), lambda b: (b, 0, 0, 0))
    return pl.pallas_call(
        _hgrn_sample_body,
        grid=(bsz // g,),
        in_specs=[zspec(0), zspec(1), zspec(2), zspec(3),
                  pl.BlockSpec((HEADS, HEAD_DIM), lambda b: (0, 0)),
                  pl.BlockSpec((1, HEAD_DIM), lambda b: (0, 0)),
                  sspec],
        out_specs=[pl.BlockSpec((g, HEADS, HEAD_DIM), lambda b: (b, 0, 0)), sspec],
        out_shape=[jax.ShapeDtypeStruct((bsz, HEADS, HEAD_DIM), F32),
                   jax.ShapeDtypeStruct(s0.shape, F32)],
        compiler_params=_cparams(("parallel",)),
        name="hgrn_sample",
    )(zs3, zs3, zs3, zs3, lb.reshape(HEADS, HEAD_DIM), gn.reshape(1, HEAD_DIM), s0)


_KEY_NEG_INF = np.int32(np.uint32(0x807FFFFF).astype(np.int64) - (1 << 32))
_INT_MIN = np.int32(-(1 << 31))


def _count(u_ref, n_groups, group, thr, cmp):
    rows = u_ref.shape[1]
    step = min(rows, LANES)
    parts = []
    for r0 in range(0, rows, step):
        t = jnp.broadcast_to(thr[r0:r0 + step], (step, LANES))

        def body(gi, acc, r0=r0, t=t):
            for i in range(group):
                acc = acc + cmp(u_ref[gi * group + i, r0:r0 + step], t).astype(F32)
            return acc

        acc = lax.fori_loop(0, n_groups, body, jnp.zeros((step, LANES), F32))
        parts.append(jnp.sum(acc, axis=-1, keepdims=True))
    return parts[0] if len(parts) == 1 else jnp.concatenate(parts, axis=0)


def _topk_mask(score_ref, u_ref, write_tile, n_tiles, n_groups, group, k):
    rows = score_ref.shape[1]
    n_live = n_groups * group

    def to_key(t, carry):
        bits = pltpu.bitcast(score_ref[t], I32)
        u_ref[t] = jnp.where(bits < 0, bits ^ np.int32(0x7FFFFFFF), bits)
        return carry

    lax.fori_loop(0, n_live, to_key, 0)

    kf = float(k)
    ge = lambda u, t: u >= t
    cnt = _count(u_ref, n_groups, group, jnp.zeros((rows, 1), I32), ge)
    lo = jnp.where(cnt >= kf, np.int32(0), _INT_MIN)

    def bit_step(i, lo):
        cand = lo | (np.int32(1) << (30 - i))
        cnt = _count(u_ref, n_groups, group, cand, ge)
        return jnp.where(cnt >= kf, cand, lo)

    lo = lax.fori_loop(0, 31, bit_step, lo)
    c_gt = _count(u_ref, n_groups, group, lo, lambda u, t: u > t)
    c_eq = _count(u_ref, n_groups, group, lo, lambda u, t: u == t)
    need = kf - c_gt
    real = lo > _KEY_NEG_INF
    excess = jnp.where(real & (c_eq > need), 1.0, 0.0)
    any_excess = jnp.max(excess) > 0.0

    @pl.when(jnp.logical_not(any_excess))
    def _():
        def emit(t, carry):
            u = u_ref[t]
            write_tile(t, jnp.where((u >= lo) & (u > _KEY_NEG_INF), 1.0, 0.0))
            return carry

        lax.fori_loop(0, n_live, emit, 0)

    @pl.when(any_excess)
    def _():
        upper = (lax.broadcasted_iota(I32, (LANES, LANES), 0)
                 <= lax.broadcasted_iota(I32, (LANES, LANES), 1)).astype(BF)

        def emit(t, seen):
            u = u_ref[t]
            eq = jnp.where(u == lo, 1.0, 0.0)
            prefix = seen + _dot(eq.astype(BF), upper)
            take = (u > lo) | ((u == lo) & (prefix <= need))
            write_tile(t, jnp.where(take & (u > _KEY_NEG_INF), 1.0, 0.0))
            return seen + jnp.sum(eq, axis=-1, keepdims=True)

        lax.fori_loop(0, n_live, emit, jnp.zeros((rows, 1), F32))

    def clear(t, carry):
        write_tile(t, jnp.zeros((rows, LANES), F32))
        return carry

    lax.fori_loop(n_live, n_tiles, clear, 0)


IDX_KCHUNK = 512
IDX_QROWS = 256


def _indexer_prompt_body(qi_ref, tailq_ref, tail_ref, mask_ref, score_ref, u_ref, *, seq):
    j = pl.program_id(1)
    rows = IDX_QROWS
    qblocks = rows // LANES
    n_tiles = seq // LANES
    tiles_per_chunk = IDX_KCHUNK // LANES
    qi = qi_ref[0].astype(BF)
    w = tailq_ref[0][:, IDX_DIM:IDX_DIM + IDX_HEADS] * (IDX_DIM ** -0.5 * IDX_HEADS ** -0.5)

    n_chunks = (j * rows + rows - 1) // IDX_KCHUNK + 1
    q_pos = j * rows + lax.broadcasted_iota(I32, (rows, IDX_KCHUNK), 0)

    def chunk_step(ci, carry):
        k0 = pl.multiple_of(ci * IDX_KCHUNK, IDX_KCHUNK)
        kic = tail_ref[0, pl.ds(k0, IDX_KCHUNK), :][:, 0:IDX_DIM].astype(BF)
        acc = jnp.zeros((rows, IDX_KCHUNK), F32)
        for h in range(IDX_HEADS):
            s = _dot_nt(qi[:, h * IDX_DIM:(h + 1) * IDX_DIM], kic)
            acc = acc + w[:, h:h + 1] * jnp.maximum(s, 0.0)
        k_pos = k0 + lax.broadcasted_iota(I32, (rows, IDX_KCHUNK), 1)
        acc = jnp.where(k_pos <= q_pos, acc, -jnp.inf)
        for i in range(tiles_per_chunk):
            score_ref[ci * tiles_per_chunk + i] = acc[:, i * LANES:(i + 1) * LANES]
        return carry

    lax.fori_loop(0, n_chunks, chunk_step, 0)

    def write_tile(t, m):
        mb = m.astype(BF)
        for qb in range(qblocks):
            mask_ref[0, qb, t] = mb[qb * LANES:(qb + 1) * LANES]

    _topk_mask(score_ref, u_ref, write_tile, n_tiles, n_chunks, tiles_per_chunk, TOPK)


def indexer_prompt(z3, k):
    bsz, seq = z3.shape[:2]
    nqb = seq // LANES
    rows = IDX_QROWS
    tail_block = TAIL_BLOCK
    assert k == TOPK and seq % rows == 0 and seq % IDX_KCHUNK == 0
    return pl.pallas_call(
        functools.partial(_indexer_prompt_body, seq=seq),
        grid=(bsz, seq // rows),
        in_specs=[pl.BlockSpec((1, rows, IDX_HEADS * IDX_DIM), lambda b, j: (b, j, 7)),
                  pl.BlockSpec((1, rows, LANES), lambda b, j: (b, j, tail_block)),
                  pl.BlockSpec((1, seq, LANES), lambda b, j: (b, 0, tail_block))],
        out_specs=pl.BlockSpec((1, rows // LANES, nqb, LANES, LANES), lambda b, j: (b, j, 0, 0, 0)),
        out_shape=jax.ShapeDtypeStruct((bsz, nqb, nqb, LANES, LANES), BF),
        scratch_shapes=[pltpu.VMEM((nqb, rows, LANES), F32),
                        pltpu.VMEM((nqb, rows, LANES), I32)],
        compiler_params=_cparams(("parallel", "arbitrary")),
        name="indexer_prompt",
    )(z3, z3, z3)


ATTN_KSTEP = 512
ATTN_TILES = ATTN_KSTEP // LANES


def _attn_prompt_body(bfar_ref, q_ref, k_ref, v_ref, mask_ref, bias_ref, o_ref, m_ref, l_ref, acc_ref):
    j = pl.program_id(1)
    scale = HEAD_DIM ** -0.5
    q = q_ref[0].astype(BF)
    m_ref[...] = jnp.full(m_ref.shape, NEG_BIG, F32)
    l_ref[...] = jnp.zeros(l_ref.shape, F32)
    acc_ref[...] = jnp.zeros(acc_ref.shape, F32)

    def process(sb, near):
        k0 = pl.multiple_of(sb * ATTN_KSTEP, ATTN_KSTEP)
        kblk = k_ref[0, pl.ds(k0, ATTN_KSTEP), :]
        vblk = v_ref[0, pl.ds(k0, ATTN_KSTEP), :]
        tiles = [sb * ATTN_TILES + i for i in range(ATTN_TILES)]
        sel = jnp.concatenate([mask_ref[0, 0, t] for t in tiles], axis=1).astype(F32) > 0.0
        if near:
            bidx = [jnp.where(j - t == 0, 0, jnp.where(j - t == 1, 1, 2)) for t in tiles]
        for h in range(HEADS):
            cols = slice(h * HEAD_DIM, (h + 1) * HEAD_DIM)
            lg = _dot_nt(q[:, cols], kblk[:, cols]) * scale
            if near:
                lg = lg + jnp.concatenate([bias_ref[bi, h] for bi in bidx], axis=1)
            else:
                lg = lg + bfar_ref[h]
            lg = jnp.where(sel, lg, NEG_BIG)
            m_old = m_ref[h]
            m_new = jnp.maximum(m_old, jnp.max(lg, axis=-1, keepdims=True))
            p = jnp.exp(lg - jnp.concatenate([m_new] * ATTN_TILES, axis=1))
            alpha = jnp.exp(m_old - m_new)
            l_ref[h] = alpha * l_ref[h] + jnp.sum(p, axis=-1, keepdims=True)
            acc_ref[:, cols] = alpha * acc_ref[:, cols] + _dot(p.astype(BF), vblk[:, cols])
            m_ref[h] = m_new

    n_far = jnp.maximum((j - 1) // ATTN_TILES, 0)

    def far_step(sb, carry):
        process(sb, False)
        return carry

    def near_step(sb, carry):
        process(sb, True)
        return carry

    lax.fori_loop(0, n_far, far_step, 0)
    lax.fori_loop(n_far, j // ATTN_TILES + 1, near_step, 0)

    for h in range(HEADS):
        cols = slice(h * HEAD_DIM, (h + 1) * HEAD_DIM)
        o_ref[0, :, cols] = acc_ref[:, cols] / l_ref[h]


def attn_prompt(z3, kv_bf, mask, bias_tiles, bias_far):
    bsz, seq = z3.shape[:2]
    nqb = seq // LANES
    width = HEADS * HEAD_DIM
    assert seq % ATTN_KSTEP == 0
    return pl.pallas_call(
        _attn_prompt_body,
        grid=(bsz, nqb),
        in_specs=[pl.BlockSpec(memory_space=pltpu.SMEM),
                  pl.BlockSpec((1, LANES, width), lambda b, j: (b, j, 4)),
                  pl.BlockSpec((1, seq, width), lambda b, j: (b, 0, 0)),
                  pl.BlockSpec((1, seq, width), lambda b, j: (b, 0, 1)),
                  pl.BlockSpec((1, 1, nqb, LANES, LANES), lambda b, j: (b, j, 0, 0, 0)),
                  pl.BlockSpec(bias_tiles.shape, lambda b, j: (0, 0, 0, 0))],
        out_specs=pl.BlockSpec((1, LANES, width), lambda b, j: (b, j, 0)),
        out_shape=jax.ShapeDtypeStruct((bsz, seq, width), F32),
        scratch_shapes=[pltpu.VMEM((HEADS, LANES, LANES), F32),
                        pltpu.VMEM((HEADS, LANES, LANES), F32),
                        pltpu.VMEM((LANES, width), F32)],
        compiler_params=_cparams(("parallel", "arbitrary")),
        name="attn_prompt",
    )(bias_far, z3, kv_bf, kv_bf, mask, bias_tiles)


QW = 256
SUB = 8


def _row_all(x8, op):
    return jnp.broadcast_to(op(x8, axis=0, keepdims=True), x8.shape)


def _count_t(u_ref, n_groups, group, thr_row, cmp):
    qw = u_ref.shape[2]
    thr = jnp.broadcast_to(thr_row, (SUB, qw))

    def body(gi, acc):
        for i in range(group):
            u = u_ref[gi * group + i].reshape(LANES // SUB, SUB, qw)
            acc = acc + jnp.sum(cmp(u, thr[None]).astype(F32), axis=0)
        return acc

    acc = lax.fori_loop(0, n_groups, body, jnp.zeros((SUB, qw), F32))
    return jnp.sum(acc, axis=0, keepdims=True)


def _topk_mask_t(score_ref, u_ref, write_tile, n_tiles, n_groups, group, k):
    qw = score_ref.shape[2]
    n_live = n_groups * group

    def to_key(t, carry):
        bits = pltpu.bitcast(score_ref[t], I32)
        u_ref[t] = jnp.where(bits < 0, bits ^ np.int32(0x7FFFFFFF), bits)
        return carry

    lax.fori_loop(0, n_live, to_key, 0)

    kf = float(k)
    ge = lambda u, t: u >= t
    cnt = _count_t(u_ref, n_groups, group, jnp.zeros((1, qw), I32), ge)
    lo = jnp.where(cnt >= kf, np.int32(0), _INT_MIN)

    def bit_step(i, lo):
        cand = lo | (np.int32(1) << (30 - i))
        cnt = _count_t(u_ref, n_groups, group, cand, ge)
        return jnp.where(cnt >= kf, cand, lo)

    lo = lax.fori_loop(0, 31, bit_step, lo)
    c_gt = _count_t(u_ref, n_groups, group, lo, lambda u, t: u > t)
    c_eq = _count_t(u_ref, n_groups, group, lo, lambda u, t: u == t)
    need = kf - c_gt
    excess = jnp.where((lo > _KEY_NEG_INF) & (c_eq > need), 1.0, 0.0)
    any_excess = jnp.max(excess) > 0.0
    lo_b = jnp.broadcast_to(lo, (LANES, qw))

    @pl.when(jnp.logical_not(any_excess))
    def _():
        def emit(t, carry):
            u = u_ref[t]
            write_tile(t, jnp.where((u >= lo_b) & (u > _KEY_NEG_INF), 1.0, 0.0))
            return carry

        lax.fori_loop(0, n_live, emit, 0)

    @pl.when(any_excess)
    def _():
        lower = (lax.broadcasted_iota(I32, (LANES, LANES), 1)
                 <= lax.broadcasted_iota(I32, (LANES, LANES), 0)).astype(BF)

        def emit(t, seen):
            u = u_ref[t]
            eq = jnp.where(u == lo_b, 1.0, 0.0)
            prefix = seen + _dot(lower, eq.astype(BF))
            take = (u > lo_b) | ((u == lo_b) & (prefix <= need))
            write_tile(t, jnp.where(take & (u > _KEY_NEG_INF), 1.0, 0.0))
            return seen + jnp.sum(eq, axis=0, keepdims=True)

        lax.fori_loop(0, n_live, emit, jnp.zeros((1, qw), F32))

    def clear(t, carry):
        write_tile(t, jnp.zeros((LANES, qw), F32))
        return carry

    lax.fori_loop(n_live, n_tiles, clear, 0)


def _indexer_t_body(qi_ref, wt_ref, tail_ref, mask_ref, score_ref, u_ref, *, seq):
    j = pl.program_id(1)
    n_tiles = seq // LANES
    tiles_per_chunk = IDX_KCHUNK // LANES
    qi = qi_ref[0].astype(BF)
    wt = wt_ref[0] * (IDX_DIM ** -0.5 * IDX_HEADS ** -0.5)

    n_chunks = (j * QW + QW - 1) // IDX_KCHUNK + 1
    q_pos = j * QW + lax.broadcasted_iota(I32, (IDX_KCHUNK, QW), 1)

    def chunk_step(ci, carry):
        k0 = pl.multiple_of(ci * IDX_KCHUNK, IDX_KCHUNK)
        kic = tail_ref[0, pl.ds(k0, IDX_KCHUNK), :][:, 0:IDX_DIM].astype(BF)
        acc = jnp.zeros((IDX_KCHUNK, QW), F32)
        for h in range(IDX_HEADS):
            s = _dot_nt(kic, qi[:, h * IDX_DIM:(h + 1) * IDX_DIM])
            acc = acc + wt[h:h + 1] * jnp.maximum(s, 0.0)
        k_pos = k0 + lax.broadcasted_iota(I32, (IDX_KCHUNK, QW), 0)
        acc = jnp.where(k_pos <= q_pos, acc, -jnp.inf)
        for i in range(tiles_per_chunk):
            score_ref[ci * tiles_per_chunk + i] = acc[i * LANES:(i + 1) * LANES]
        return carry

    lax.fori_loop(0, n_chunks, chunk_step, 0)

    def write_tile(t, m):
        mask_ref[0, 0, t] = m.astype(BF)

    _topk_mask_t(score_ref, u_ref, write_tile, n_tiles, n_chunks, tiles_per_chunk, TOPK)


def indexer_prompt_t(z3, wt):
    bsz, seq = z3.shape[:2]
    n_tiles = seq // LANES
    tail_block = TAIL_BLOCK
    assert seq % QW == 0 and seq % IDX_KCHUNK == 0
    return pl.pallas_call(
        functools.partial(_indexer_t_body, seq=seq),
        grid=(bsz, seq // QW),
        in_specs=[pl.BlockSpec((1, QW, IDX_HEADS * IDX_DIM), lambda b, j: (b, j, 7)),
                  pl.BlockSpec((1, IDX_HEADS, QW), lambda b, j: (b, 0, j)),
                  pl.BlockSpec((1, seq, LANES), lambda b, j: (b, 0, tail_block))],
        out_specs=pl.BlockSpec((1, 1, n_tiles, LANES, QW), lambda b, j: (b, j, 0, 0, 0)),
        out_shape=jax.ShapeDtypeStruct((bsz, seq // QW, n_tiles, LANES, QW), BF),
        scratch_shapes=[pltpu.VMEM((n_tiles, LANES, QW), F32),
                        pltpu.VMEM((n_tiles, LANES, QW), I32)],
        compiler_params=_cparams(("parallel", "arbitrary")),
        name="indexer_prompt_t",
    )(z3, wt, z3)


def _attn_t_body(bfar_ref, q_ref, k_ref, vt_ref, mask_ref, bias_ref, o_ref, m_ref, l_ref, acc_ref):
    j = pl.program_id(1)
    scale = HEAD_DIM ** -0.5
    qsub = QW // LANES
    q = q_ref[0].astype(BF)
    m_ref[...] = jnp.full(m_ref.shape, NEG_BIG, F32)
    l_ref[...] = jnp.zeros(l_ref.shape, F32)
    acc_ref[...] = jnp.zeros(acc_ref.shape, F32)

    def process(sb, near):
        k0 = pl.multiple_of(sb * ATTN_KSTEP, ATTN_KSTEP)
        kblk = k_ref[0, pl.ds(k0, ATTN_KSTEP), :]
        vblk = vt_ref[0, pl.ds(k0, ATTN_KSTEP), :]
        tiles = [sb * ATTN_TILES + i for i in range(ATTN_TILES)]
        sel = jnp.concatenate([mask_ref[0, 0, t] for t in tiles], axis=0).astype(F32) > 0.0
        if near:
            def tile_bias(t, h):
                row = []
                for s in range(qsub):
                    d = j * qsub + s - t
                    row.append(bias_ref[jnp.where(d == 0, 0, jnp.where(d == 1, 1, 2)), h])
                return jnp.concatenate(row, axis=1)
        for h in range(HEADS):
            cols = slice(h * HEAD_DIM, (h + 1) * HEAD_DIM)
            lg = _dot_nt(kblk[:, cols], q[:, cols]) * scale
            if near:
                lg = lg + jnp.concatenate([tile_bias(t, h) for t in tiles], axis=0)
            else:
                lg = lg + bfar_ref[h]
            lg = jnp.where(sel, lg, NEG_BIG)
            m_old = m_ref[h]
            part = jnp.max(lg.reshape(ATTN_KSTEP // SUB, SUB, QW), axis=0)
            m_new = jnp.maximum(m_old, _row_all(part, jnp.max))
            p = jnp.exp(lg - m_new[0:1])
            alpha = jnp.exp(m_old - m_new)
            psum = jnp.sum(p.reshape(ATTN_KSTEP // SUB, SUB, QW), axis=0)
            l_ref[h] = alpha * l_ref[h] + _row_all(psum, jnp.sum)
            pv = _dot_tn(vblk[:, cols], p.astype(BF))
            acc_ref[cols, :] = alpha[0:1] * acc_ref[cols, :] + pv
            m_ref[h] = m_new

    n_far = jnp.maximum((qsub * j - 1) // ATTN_TILES, 0)
    n_steps = (qsub * j + qsub - 1) // ATTN_TILES + 1

    def far_step(sb, carry):
        process(sb, False)
        return carry

    def near_step(sb, carry):
        process(sb, True)
        return carry

    lax.fori_loop(0, n_far, far_step, 0)
    lax.fori_loop(n_far, n_steps, near_step, 0)

    for h in range(HEADS):
        cols = slice(h * HEAD_DIM, (h + 1) * HEAD_DIM)
        o_ref[0, cols, :] = acc_ref[cols, :] / l_ref[h][0:1]


def attn_prompt_t(z3, k_bf, vt_bf, mask_t, bias_tiles_t, bias_far):
    bsz, seq = z3.shape[:2]
    n_tiles = seq // LANES
    width = HEADS * HEAD_DIM
    assert seq % ATTN_KSTEP == 0 and seq % QW == 0
    return pl.pallas_call(
        _attn_t_body,
        grid=(bsz, seq // QW),
        in_specs=[pl.BlockSpec(memory_space=pltpu.SMEM),
                  pl.BlockSpec((1, QW, width), lambda b, j: (b, j, 4)),
                  pl.BlockSpec((1, seq, width), lambda b, j: (b, 0, 0)),
                  pl.BlockSpec((1, seq, width), lambda b, j: (b, 0, 0)),
                  pl.BlockSpec((1, 1, n_tiles, LANES, QW), lambda b, j: (b, j, 0, 0, 0)),
                  pl.BlockSpec(bias_tiles_t.shape, lambda b, j: (0, 0, 0, 0))],
        out_specs=pl.BlockSpec((1, width, QW), lambda b, j: (b, 0, j)),
        out_shape=jax.ShapeDtypeStruct((bsz, width, seq), F32),
        scratch_shapes=[pltpu.VMEM((HEADS, SUB, QW), F32),
                        pltpu.VMEM((HEADS, SUB, QW), F32),
                        pltpu.VMEM((width, QW), F32)],
        compiler_params=_cparams(("parallel", "arbitrary")),
        name="attn_prompt_t",
    )(bias_far, z3, k_bf, vt_bf, mask_t, bias_tiles_t)


def _indexer_sample_body(pt_ref, qi_ref, wcol_ref, kinew_ref, *rest, n_pages):
    ki_refs = rest[:n_pages]
    out_ref = rest[n_pages]
    qi = qi_ref[0].astype(BF)
    w = wcol_ref[0] * (IDX_DIM ** -0.5 * IDX_HEADS ** -0.5)
    for i in range(n_pages):
        s = _dot(qi, ki_refs[i][0].astype(BF))
        out_ref[i, 0] = jnp.sum(w * jnp.maximum(s, 0.0), axis=0, keepdims=True)
    kn = kinew_ref[0].astype(BF).astype(F32)
    sn = jnp.sum(qi.astype(F32) * kn, axis=-1, keepdims=True)
    new = jnp.sum(w * jnp.maximum(sn, 0.0), axis=0, keepdims=True)
    lane = lax.broadcasted_iota(I32, (1, LANES), 1)
    out_ref[n_pages, 0] = jnp.where(lane == 0, new, -jnp.inf)


def indexer_sample(page_table, qi3, wcol, ki_new, ki_pool_t):
    bsz, n_pages = page_table.shape
    ki_spec = lambda i: pl.BlockSpec((1, IDX_DIM, PAGE), lambda b, pt, i=i: (pt[b, i], 0, 0))
    grid_spec = pltpu.PrefetchScalarGridSpec(
        num_scalar_prefetch=1,
        grid=(bsz,),
        in_specs=[pl.BlockSpec((1, IDX_HEADS, IDX_DIM), lambda b, pt: (b, 0, 0)),
                  pl.BlockSpec((1, IDX_HEADS, 1), lambda b, pt: (b, 0, 0)),
                  pl.BlockSpec((1, 1, IDX_DIM), lambda b, pt: (b, 0, 0))]
                 + [ki_spec(i) for i in range(n_pages)],
        out_specs=pl.BlockSpec((n_pages + 1, 1, 1, LANES), lambda b, pt: (0, b, 0, 0)),
    )
    out = pl.pallas_call(
        functools.partial(_indexer_sample_body, n_pages=n_pages),
        grid_spec=grid_spec,
        out_shape=jax.ShapeDtypeStruct((n_pages + 1, bsz, 1, LANES), F32),
        compiler_params=_cparams(("arbitrary",)),
        name="indexer_sample",
    )(page_table, qi3, wcol, ki_new, *([ki_pool_t] * n_pages))
    return out.reshape(n_pages + 1, bsz, LANES)


def _select_body(score_ref, mask_ref, u_ref, *, n_tiles, k):
    def write_tile(t, m):
        mask_ref[t] = m

    _topk_mask(score_ref, u_ref, write_tile, n_tiles, n_tiles, 1, k)


def select_topk(scores, k):
    n_tiles, rows, _ = scores.shape
    return pl.pallas_call(
        functools.partial(_select_body, n_tiles=n_tiles, k=k),
        grid=(1,),
        in_specs=[pl.BlockSpec(scores.shape, lambda i: (0, 0, 0))],
        out_specs=pl.BlockSpec(scores.shape, lambda i: (0, 0, 0)),
        out_shape=jax.ShapeDtypeStruct(scores.shape, F32),
        scratch_shapes=[pltpu.VMEM(scores.shape, I32)],
        compiler_params=_cparams(("arbitrary",)),
        name="select_topk",
    )(scores)


SC_CORES = 2
SC_SUBCORES = 16
SC_GATHER_CHUNK = 32


def sc_gather_rows(tables, idx):
    n_rows = idx.shape[0]
    workers = SC_CORES * SC_SUBCORES
    per_worker = n_rows // workers
    chunk = SC_GATHER_CHUNK
    assert n_rows % workers == 0 and per_worker % chunk == 0 and chunk % 8 == 0
    row_shape = tables[0].shape[1:]
    n_tab = len(tables)
    mesh = plsc.VectorSubcoreMesh(core_axis_name="c", subcore_axis_name="s",
                                  num_cores=SC_CORES, num_subcores=SC_SUBCORES)

    def body(*refs):
        tab_refs = refs[:n_tab]
        idx_hbm = refs[n_tab]
        out_refs = refs[n_tab + 1:2 * n_tab + 1]
        idx_v, rows_v, sem = refs[2 * n_tab + 1:]
        wid = lax.axis_index("s") * SC_CORES + lax.axis_index("c")

        @pl.loop(0, per_worker // chunk)
        def _(ci):
            off = pl.multiple_of(wid * per_worker + ci * chunk, 8)
            pltpu.sync_copy(idx_hbm.at[pl.ds(off, chunk)], idx_v)
            for tab, out in zip(tab_refs, out_refs):
                pltpu.async_copy(tab.at[idx_v], rows_v, sem).wait()
                pltpu.sync_copy(rows_v, out.at[pl.ds(off, chunk)])

    return pl.kernel(
        body,
        out_type=[jax.ShapeDtypeStruct((n_rows, *row_shape), t.dtype) for t in tables],
        mesh=mesh,
        scratch_types=[pltpu.VMEM((chunk,), I32),
                       pltpu.VMEM((chunk, *row_shape), tables[0].dtype),
                       pltpu.SemaphoreType.DMA],
        compiler_params=pltpu.CompilerParams(use_tc_tiling_on_sc=True),
        name="sc_gather_rows",
    )(*tables, idx)


SAMPLE_SEQ_GROUP = 4


def _attn_compact_body(dist_ref, near_ref, q_ref, kn_ref, vn_ref, btab_ref, kc_ref, vc_ref,
                       o_ref, bbuf):
    step = pl.program_id(0)
    scale = HEAD_DIM ** -0.5
    ones = jnp.ones((HEAD_DIM, LANES), BF)
    far_bias = btab_ref[REL_MAX_DIST]
    last = lax.broadcasted_iota(I32, (TOPK, 1, 1), 0) == TOPK - 1
    for g in range(SAMPLE_SEQ_GROUP):
        b = step * SAMPLE_SEQ_GROUP + g
        bbuf[g] = jnp.broadcast_to(far_bias[None], (TOPK, HEADS, LANES))

        def fill(t, carry, g=g, b=b):
            bbuf[g, t] = btab_ref[dist_ref[b, t]]
            return carry

        lax.fori_loop(near_ref[b], TOPK, fill, 0)

        is_new = dist_ref[b, TOPK - 1] == 0
        rows = slice(g * TOPK, (g + 1) * TOPK)
        rnd = lambda a: a.astype(BF).astype(F32)
        kc = rnd(jnp.where(last & is_new, kn_ref[g][None], kc_ref[rows]))
        vc = rnd(jnp.where(last & is_new, vn_ref[g][None], vc_ref[rows]))
        prod = (kc * rnd(q_ref[g])[None]).reshape(TOPK * HEADS, HEAD_DIM)
        hi = prod.astype(BF)
        lo = (prod - hi.astype(F32)).astype(BF)
        lg = (_dot(hi, ones) + _dot(lo, ones)).reshape(TOPK, HEADS, LANES) * scale + bbuf[g]
        m = jnp.max(lg, axis=0, keepdims=True)
        p = jnp.exp(lg - m)
        l = jnp.sum(p, axis=0)
        o_ref[g] = jnp.sum(rnd(p) * vc, axis=0) / l


def attn_sample_compact(dist, first_near, zs3, k_sel, v_sel, bias_by_dist):
    bsz = dist.shape[0]
    g = SAMPLE_SEQ_GROUP
    assert bsz % g == 0
    zspec = lambda cb: pl.BlockSpec((g, HEADS, HEAD_DIM), lambda i, d, n, cb=cb: (i, cb, 0))
    sel_spec = pl.BlockSpec((g * TOPK, HEADS, HEAD_DIM), lambda i, d, n: (i, 0, 0))
    grid_spec = pltpu.PrefetchScalarGridSpec(
        num_scalar_prefetch=2,
        grid=(bsz // g,),
        in_specs=[zspec(4), zspec(5), zspec(6),
                  pl.BlockSpec(bias_by_dist.shape, lambda i, d, n: (0, 0, 0)),
                  sel_spec, sel_spec],
        out_specs=pl.BlockSpec((g, HEADS, HEAD_DIM), lambda i, d, n: (i, 0, 0)),
        scratch_shapes=[pltpu.VMEM((g, TOPK, HEADS, HEAD_DIM), F32)],
    )
    return pl.pallas_call(
        _attn_compact_body,
        grid_spec=grid_spec,
        out_shape=jax.ShapeDtypeStruct((bsz, HEADS, HEAD_DIM), F32),
        compiler_params=_cparams(("arbitrary",)),
        name="attn_sample_compact",
    )(dist, first_near, zs3, zs3, zs3, bias_by_dist, k_sel, v_sel)


def _bucket_table(max_dist):
    exact = REL_BUCKETS // 2
    d = np.arange(max_dist + 1)
    df = np.maximum(d, 1).astype(np.float32)
    far = exact + (np.log(df / exact) / np.float32(math.log(REL_MAX_DIST / exact))
                   * (REL_BUCKETS - exact)).astype(np.int32)
    return np.where(d < exact, d, np.minimum(far, REL_BUCKETS - 1)).astype(np.int32)


def _bias_tables(rel_bias):
    tab = _bucket_table(2 * LANES)
    assert np.all(tab[REL_MAX_DIST:] == REL_BUCKETS - 1)
    i = np.arange(LANES)
    dist0 = np.maximum(i[:, None] - i[None, :], 0)
    dist1 = LANES + i[:, None] - i[None, :]
    far = np.full((LANES, LANES), REL_BUCKETS - 1)
    idx = np.stack([tab[dist0], tab[dist1], far])
    buckets = jnp.arange(REL_BUCKETS)
    lookup = lambda ix, spec: jnp.einsum(spec, (jnp.asarray(ix)[..., None] == buckets).astype(F32),
                                         rel_bias.astype(F32), precision=lax.Precision.HIGHEST)
    tiles = lookup(idx, "tijb,bh->thij")
    by_dist = jnp.broadcast_to(lookup(tab[:REL_MAX_DIST + 1], "db,bh->dh")[:, :, None],
                               (REL_MAX_DIST + 1, HEADS, LANES))
    return tiles.astype(F32), rel_bias[REL_BUCKETS - 1].astype(F32), by_dist.astype(F32)


def _conv_prompt_body(bg_ref, cg_ref, xt_ref, w_ref, v_ref, st_ref, carry_ref, *, tblock):
    t = pl.program_id(1)

    @pl.when(t == 0)
    def _():
        carry_ref[...] = jnp.zeros_like(carry_ref)

    u = cg_ref[0] * xt_ref[0]
    row = lax.broadcasted_iota(I32, u.shape, 0)
    c0 = carry_ref[0:1]
    c1 = carry_ref[1:2]
    u1 = jnp.where(row == 0, c1, pltpu.roll(u, 1, axis=0))
    u2 = jnp.where(row == 0, c0, jnp.where(row == 1, c1, pltpu.roll(u, 2, axis=0)))
    conv = w_ref[0:1] * u2 + w_ref[1:2] * u1 + w_ref[2:3] * u
    v_ref[0] = bg_ref[0] * conv
    last = u[tblock - 2:tblock]
    carry_ref[0:2] = last

    @pl.when(t == pl.num_programs(1) - 1)
    def _():
        st_ref[0] = last


def conv_prompt(zc3, w_conv, tblock=256):
    bsz, seq = zc3.shape[:2]
    c = zc3.shape[2] // 3
    zspec = lambda cb: pl.BlockSpec((1, tblock, c), lambda b, t, cb=cb: (b, t, cb))
    return pl.pallas_call(
        functools.partial(_conv_prompt_body, tblock=tblock),
        grid=(bsz, seq // tblock),
        in_specs=[zspec(0), zspec(1), zspec(2), pl.BlockSpec((3, c), lambda b, t: (0, 0))],
        out_specs=[pl.BlockSpec((1, tblock, c), lambda b, t: (b, t, 0)),
                   pl.BlockSpec((1, 2, c), lambda b, t: (b, 0, 0))],
        out_shape=[jax.ShapeDtypeStruct((bsz, seq, c), F32),
                   jax.ShapeDtypeStruct((bsz, 2, c), F32)],
        scratch_shapes=[pltpu.VMEM((8, c), F32)],
        compiler_params=_cparams(("parallel", "arbitrary")),
        name="conv_prompt",
    )(zc3, zc3, zc3, w_conv)


def _conv_sample_body(bg_ref, cg_ref, xt_ref, w_ref, s0_ref, s1_ref, v_ref, n0_ref, n1_ref):
    u = cg_ref[...] * xt_ref[...]
    conv = w_ref[0:1] * s0_ref[...] + w_ref[1:2] * s1_ref[...] + w_ref[2:3] * u
    v_ref[...] = bg_ref[...] * conv
    n0_ref[...] = s1_ref[...]
    n1_ref[...] = u


def conv_sample(zc, w_conv, s0, s1):
    bsz = zc.shape[0]
    c = zc.shape[1] // 3
    zspec = lambda cb: pl.BlockSpec((bsz, c), lambda i, cb=cb: (0, cb))
    full = pl.BlockSpec((bsz, c), lambda i: (0, 0))
    return pl.pallas_call(
        _conv_sample_body,
        grid=(1,),
        in_specs=[zspec(0), zspec(1), zspec(2), pl.BlockSpec((3, c), lambda i: (0, 0)), full, full],
        out_specs=[full, full, full],
        out_shape=[jax.ShapeDtypeStruct((bsz, c), F32)] * 3,
        compiler_params=_cparams(("arbitrary",)),
        name="conv_sample",
    )(zc, zc, zc, w_conv, s0, s1)


def _router_body(x_ref, g_ref, wr_ref, br_ref, h_ref, route_ref):
    x = x_ref[...]
    ms = jnp.mean(x * x, axis=-1, keepdims=True)
    h = x * lax.rsqrt(ms + RMS_EPS) * g_ref[...]
    hb = h.astype(BF)
    half = h.shape[1] // 2
    bits = pltpu.bitcast(hb.astype(F32), jnp.uint32)
    h_ref[...] = (bits[:, :half] & jnp.uint32(0xFFFF0000)) | (bits[:, half:] >> 16)
    logits = _dot(hb, wr_ref[...]) + br_ref[...]
    lane = lax.broadcasted_iota(I32, logits.shape, 1)
    big = np.int32(1 << 20)
    lg = jnp.where(lane < N_GROUPS, logits, -jnp.inf)
    g_max = jnp.max(lg, axis=-1, keepdims=True)
    g_idx = jnp.min(jnp.where(lg == g_max, lane, big), axis=-1, keepdims=True)
    g_w = 1.0 / jnp.sum(jnp.exp(lg - g_max), axis=-1, keepdims=True)
    first = N_GROUPS + EXP_PER_GROUP * g_idx
    le = jnp.where((lane >= first) & (lane < first + EXP_PER_GROUP), logits, -jnp.inf)
    l1 = jnp.max(le, axis=-1, keepdims=True)
    i1 = jnp.min(jnp.where(le == l1, lane, big), axis=-1, keepdims=True)
    le2 = jnp.where(lane == i1, -jnp.inf, le)
    l2 = jnp.max(le2, axis=-1, keepdims=True)
    i2 = jnp.min(jnp.where(le2 == l2, lane, big), axis=-1, keepdims=True)
    r = jnp.exp(l2 - l1)
    w1 = g_w / (1.0 + r)
    w2 = g_w * r / (1.0 + r)
    e1 = (i1 - N_GROUPS).astype(F32)
    e2 = (i2 - N_GROUPS).astype(F32)
    route_ref[...] = jnp.where(lane == 0, e1, jnp.where(lane == 1, e2,
                               jnp.where(lane == 2, w1, jnp.where(lane == 3, w2, 0.0))))


def moe_router(x, g, wr, br, tm):
    m, k = x.shape
    return pl.pallas_call(
        _router_body,
        grid=(m // tm,),
        in_specs=[pl.BlockSpec((tm, k), lambda i: (i, 0)),
                  pl.BlockSpec((1, k), lambda i: (0, 0)),
                  pl.BlockSpec((k, LANES), lambda i: (0, 0)),
                  pl.BlockSpec((1, LANES), lambda i: (0, 0))],
        out_specs=[pl.BlockSpec((tm, k // 2), lambda i: (i, 0)),
                   pl.BlockSpec((tm, LANES), lambda i: (i, 0))],
        out_shape=[jax.ShapeDtypeStruct((m, k // 2), jnp.uint32),
                   jax.ShapeDtypeStruct((m, LANES), F32)],
        compiler_params=_cparams(("parallel",)),
        name="moe_router",
    )(x, g.reshape(1, k), wr, br)


def _experts_body(te_ref, nu_ref, h_ref, wg_ref, wu_ref, wd_ref, o_ref, wg_bf, wu_bf, wd_bf):
    i = pl.program_id(0)

    @pl.when((i == 0) | (te_ref[i] != te_ref[jnp.maximum(i - 1, 0)]))
    def _():
        wg_bf[...] = wg_ref[0].astype(BF)
        wu_bf[...] = wu_ref[0].astype(BF)
        wd_bf[...] = wd_ref[0].astype(BF)

    @pl.when(i < nu_ref[0])
    def _():
        words = h_ref[...]
        left = pltpu.bitcast(words & jnp.uint32(0xFFFF0000), F32)
        right = pltpu.bitcast(words << 16, F32)
        h = jnp.concatenate([left, right], axis=1).astype(BF)
        a = _dot(h, wg_bf[...])
        b = _dot(h, wu_bf[...])
        hid = a * _sigmoid(a) * b
        o_ref[...] = _dot(hid.astype(BF), wd_bf[...])

    @pl.when(i >= nu_ref[0])
    def _():
        o_ref[...] = jnp.zeros_like(o_ref)


def moe_experts(tile_expert, n_used, hs, wg, wu, wd):
    p = hs.shape[0]
    k = wg.shape[1]
    f = wg.shape[2]
    n_tiles = p // MOE_TILE
    grid_spec = pltpu.PrefetchScalarGridSpec(
        num_scalar_prefetch=2,
        grid=(n_tiles,),
        in_specs=[pl.BlockSpec((MOE_TILE, k // 2), lambda i, te, nu: (i, 0)),
                  pl.BlockSpec((1, k, f), lambda i, te, nu: (te[i], 0, 0)),
                  pl.BlockSpec((1, k, f), lambda i, te, nu: (te[i], 0, 0)),
                  pl.BlockSpec((1, f, k), lambda i, te, nu: (te[i], 0, 0))],
        out_specs=pl.BlockSpec((MOE_TILE, k), lambda i, te, nu: (i, 0)),
        scratch_shapes=[pltpu.VMEM((k, f), BF), pltpu.VMEM((k, f), BF), pltpu.VMEM((f, k), BF)],
    )
    return pl.pallas_call(
        _experts_body,
        grid_spec=grid_spec,
        out_shape=jax.ShapeDtypeStruct((p, k), F32),
        compiler_params=_cparams(("arbitrary",)),
        name="moe_experts",
    )(tile_expert, n_used, hs, wg, wu, wd)


def _rank_within_expert(onehot):
    n, e = onehot.shape
    blk = LANES
    assert n % blk == 0
    oh = onehot.astype(F32).reshape(n // blk, blk, e)
    strict = jnp.asarray(np.tril(np.ones((blk, blk), np.float32), -1))
    within = jnp.einsum("ij,bjk->bik", strict, oh, precision=lax.Precision.HIGHEST)
    totals = jnp.sum(oh, axis=1)
    before = jnp.cumsum(totals, axis=0) - totals
    rank = (within + before[:, None, :]).reshape(n, e)
    return jnp.sum(rank * onehot.astype(F32), axis=1).astype(I32), jnp.sum(totals, axis=0).astype(I32)


def hier_moe(xs, tms, g, wrg, brg, wre, bre, wg, wu, wd, layer):
    d = xs[0].shape[1]
    m = sum(x.shape[0] for x in xs)
    wr = jnp.zeros((d, LANES), F32)
    wr = wr.at[:, :N_GROUPS].set(wrg).at[:, N_GROUPS:N_GROUPS + N_EXPERTS].set(wre.reshape(d, N_EXPERTS))
    br = jnp.zeros((1, LANES), F32)
    br = br.at[0, :N_GROUPS].set(brg).at[0, N_GROUPS:N_GROUPS + N_EXPERTS].set(bre.reshape(N_EXPERTS))
    routed = [moe_router(x, g, wr.astype(BF), br, tm) for x, tm in zip(xs, tms)]
    h_bf = jnp.concatenate([r[0] for r in routed], axis=0)
    route = jnp.concatenate([r[1] for r in routed], axis=0)

    eid = route[:, 0:2].astype(I32).reshape(-1)
    onehot = eid[:, None] == jnp.arange(N_EXPERTS, dtype=I32)[None, :]
    rank, counts = _rank_within_expert(onehot)
    padded = ((counts + MOE_TILE - 1) // MOE_TILE) * MOE_TILE
    ends = jnp.cumsum(padded)
    pos = jnp.sum(jnp.where(onehot, (ends - padded)[None, :], 0), axis=1) + rank
    n_rows = 2 * m + N_EXPERTS * MOE_TILE
    n_rows = -(-n_rows // MOE_TILE) * MOE_TILE
    token = jnp.zeros((n_rows,), I32).at[pos].set(jnp.arange(2 * m, dtype=I32) // 2)
    tile_start = jnp.arange(n_rows // MOE_TILE, dtype=I32) * MOE_TILE
    tile_expert = jnp.minimum(jnp.sum(tile_start[:, None] >= ends[None, :], axis=1),
                              N_EXPERTS - 1).astype(I32)
    n_used = (ends[-1] // MOE_TILE).astype(I32).reshape(1)

    hs = jnp.take(h_bf, token, axis=0, mode="clip")
    out = moe_experts(tile_expert + layer * N_EXPERTS, n_used, hs, wg, wu, wd)
    pos2 = pos.reshape(m, 2)
    res, r0 = [], 0
    for x in xs:
        rows = slice(r0, r0 + x.shape[0])
        res.append(x + route[rows, 2:3] * jnp.take(out, pos2[rows, 0], axis=0, mode="clip")
                   + route[rows, 3:4] * jnp.take(out, pos2[rows, 1], axis=0, mode="clip"))
        r0 += x.shape[0]
    return res


def kernel(x_prompt, x_sample, cache_k, cache_v, cache_ki, state_hgrn, state_conv, page_table,
           norm_mix_g, norm_ffn_g, final_g, w_in_even, w_out_even, hgrn_lb_logits, hgrn_norm_g,
           rel_bias, w_in_conv, w_conv, w_out_conv, w_router_g, b_router_g, w_router_e,
           b_router_e, w_gate, w_up, w_down):
    bsz, seq, d = x_prompt.shape
    dec = x_sample.shape[0]
    n_p = bsz * seq
    tm_p, tm_s = 512, 128
    tms = (tm_p, tm_s)
    assert n_p % tm_p == 0 and dec % tm_s == 0 and x_sample.shape[1] == 1
    width = HEADS * HEAD_DIM
    even_in = w_in_even.shape[2]
    even_pad = -(-even_in // IN_PROJ_TN) * IN_PROJ_TN
    tail0 = TAIL_BLOCK * LANES
    n_pool = cache_k.shape[1]

    xp = x_prompt.reshape(n_p, d)
    xs = x_sample.reshape(dec, d)

    lbs = jnp.cumsum(jax.nn.softmax(hgrn_lb_logits.astype(F32), axis=0), axis=0)[:-1]
    bias_tiles, bias_far, bias_by_dist = _bias_tables(rel_bias)
    expert_w = [w.reshape(-1, *w.shape[2:]) for w in (w_gate, w_up, w_down)]

    w_in = jnp.pad(w_in_even[0], ((0, 0), (0, even_pad - even_in))).astype(BF)
    zp = norm_matmul(xp, norm_mix_g[0], w_in, 2 * tm_p, IN_PROJ_TN)
    zs = norm_matmul(xs, norm_mix_g[0], w_in, tm_s, IN_PROJ_TN)
    zp3 = zp.reshape(bsz, seq, even_pad)
    zs3 = zs.reshape(dec, even_pad // LANES, LANES)

    oa_p, hgrn_p = hgrn_prompt(zp3, lbs[0], hgrn_norm_g[0])
    oa_s, hgrn_s = hgrn_sample(zs3, lbs[0], hgrn_norm_g[0], state_hgrn[0])

    wt = jnp.swapaxes(zp3[:, :, tail0 + IDX_DIM:tail0 + IDX_DIM + IDX_HEADS], 1, 2)
    mask_p = indexer_prompt_t(zp3, wt)
    k_bf = zp3[:, :, 5 * width:6 * width].astype(BF)
    v_bf = zp3[:, :, 6 * width:7 * width].astype(BF)
    ob_p = jnp.swapaxes(attn_prompt_t(zp3, k_bf, v_bf, mask_p,
                                      jnp.swapaxes(bias_tiles, -1, -2), bias_far), 1, 2)

    qi3 = zs[:, 7 * width:8 * width].reshape(dec, IDX_HEADS, IDX_DIM)
    ki_new = zs[:, tail0:tail0 + IDX_DIM].reshape(dec, 1, IDX_DIM)
    wcol = zs[:, tail0 + IDX_DIM:tail0 + IDX_DIM + IDX_HEADS].reshape(dec, IDX_HEADS, 1)
    pages = page_table + 0 * n_pool
    scores_s = indexer_sample(pages, qi3, wcol, ki_new,
                              jnp.swapaxes(cache_ki, -1, -2).reshape(-1, IDX_DIM, PAGE))
    mask_s = select_topk(scores_s, TOPK)
    sel = jnp.transpose(mask_s, (1, 0, 2))
    n_pages = page_table.shape[1]
    sel_off = jnp.concatenate([jnp.zeros((dec, 1), F32),
                               jnp.cumsum(jnp.sum(sel, axis=2), axis=1)], axis=1)
    slot = jnp.arange(TOPK, dtype=F32)
    page_of = jnp.sum(sel_off[:, None, 1:] <= slot[None, :, None], axis=2)
    page_1h = (page_of[:, :, None] == jnp.arange(n_pages + 1)[None, None, :]).astype(F32)
    local = slot[None, :] - jnp.einsum("brp,bp->br", page_1h, sel_off[:, :-1],
                                       precision=lax.Precision.HIGHEST)
    within = jnp.einsum("brp,bpl->brl", page_1h, jnp.cumsum(sel, axis=2),
                        precision=lax.Precision.HIGHEST)
    lane_of = jnp.sum(within <= local[:, :, None], axis=2)
    sel_idx = (page_of * PAGE + lane_of).astype(I32)
    past = n_pages * PAGE
    page_id = jnp.einsum("brp,bp->br", page_1h[:, :, :n_pages], pages.astype(F32),
                         precision=lax.Precision.HIGHEST).astype(I32)
    rows = jnp.where(page_of < n_pages, page_id * PAGE + lane_of, 0).astype(I32).reshape(-1)
    k_sel, v_sel = sc_gather_rows([cache_k.reshape(-1, HEADS, HEAD_DIM),
                                   cache_v.reshape(-1, HEADS, HEAD_DIM)], rows)
    dist = jnp.minimum(past - sel_idx, REL_MAX_DIST).astype(I32)
    first_near = jnp.sum(dist >= REL_MAX_DIST, axis=1).astype(I32)
    ob_s = attn_sample_compact(dist, first_near, zs3, k_sel, v_sel, bias_by_dist)

    w_out = w_out_even[0].astype(BF)
    w_out_ab = [w_out[:width], w_out[width:]]
    xp = matmul_residual([oa_p.reshape(n_p, width), ob_p.reshape(n_p, width)], w_out_ab, xp, 2 * tm_p, 512)
    xs = matmul_residual([oa_s.reshape(dec, width), ob_s.reshape(dec, width)], w_out_ab, xs, tm_s, 512)

    xp, xs = hier_moe([xp, xs], tms, norm_ffn_g[0], w_router_g[0], b_router_g[0], w_router_e[0],
                      b_router_e[0], *expert_w, 0)

    w_in_c = w_in_conv[0].astype(BF)
    zcp = norm_matmul(xp, norm_mix_g[1], w_in_c, 2 * tm_p, 1024)
    zcs = norm_matmul(xs, norm_mix_g[1], w_in_c, tm_s, 512)
    cw = zcp.shape[1] // 3
    v_p, conv_p = conv_prompt(zcp.reshape(bsz, seq, 3 * cw), w_conv[0])
    v_s, cs0, cs1 = conv_sample(zcs, w_conv[0], state_conv[0, :, 0], state_conv[0, :, 1])
    w_out_c = [w_out_conv[0].astype(BF)]
    xp = matmul_residual([v_p.reshape(n_p, cw)], w_out_c, xp, 2 * tm_p, 512)
    xs = matmul_residual([v_s], w_out_c, xs, tm_s, 512)

    xp, xs = hier_moe([xp, xs], tms, norm_ffn_g[1], w_router_g[1], b_router_g[1], w_router_e[1],
                      b_router_e[1], *expert_w, 1)

    yp = rmsnorm(xp, final_g, tm_p)
    ys = rmsnorm(xs, final_g, tm_s)

    kcol, vcol = 5 * width, 6 * width
    heads = lambda a, n: a.reshape(1, *n, HEADS, HEAD_DIM)
    return (yp.reshape(bsz, seq, d),
            ys.reshape(dec, 1, d),
            heads(zp[:, kcol:kcol + width], (bsz, seq)),
            heads(zp[:, vcol:vcol + width], (bsz, seq)),
            zp[:, tail0:tail0 + IDX_DIM].reshape(1, bsz, seq, IDX_DIM),
            hgrn_p[None],
            conv_p[None],
            heads(zs[:, kcol:kcol + width], (dec, 1)),
            heads(zs[:, vcol:vcol + width], (dec, 1)),
            zs[:, tail0:tail0 + IDX_DIM].reshape(1, dec, 1, IDX_DIM),
            hgrn_s[None],
            jnp.stack([cs0, cs1], axis=1)[None])
```

```python
import functools
import math

import numpy as np
import jax
import jax.numpy as jnp
from jax import lax
from jax.experimental import pallas as pl
from jax.experimental.pallas import tpu as pltpu
from jax.experimental.pallas import tpu_sc as plsc

F32 = jnp.float32
BF = jnp.bfloat16
I32 = jnp.int32

RMS_EPS = 1e-6
LANES = 128
NEG_BIG = -1e30
VMEM_LIMIT = 56 * 1024 * 1024

D_MODEL = 2048
HEADS = 8
HEAD_DIM = 128
IDX_HEADS = 16
IDX_DIM = 64
TOPK = 256
REL_BUCKETS = 32
REL_MAX_DIST = 128
N_EXPERTS = 16
EXP_PER_GROUP = 4
N_GROUPS = 4
D_EXPERT = 512
PAGE = 128

HGRN_CHUNK = 128
MOE_TILE = 256
IN_PROJ_TN = 640
TAIL_BLOCK = 8 * HEADS * HEAD_DIM // LANES


def _cparams(sem):
    return pltpu.CompilerParams(dimension_semantics=sem, vmem_limit_bytes=VMEM_LIMIT)


def _dot(a, b):
    return jnp.dot(a, b, preferred_element_type=F32)


def _dot_nt(a, b):
    return lax.dot_general(a, b, (((1,), (1,)), ((), ())), preferred_element_type=F32)


def _dot_tn(a, b):
    return lax.dot_general(a, b, (((0,), (0,)), ((), ())), preferred_element_type=F32)


def _sigmoid(x):
    return 1.0 / (1.0 + jnp.exp(-x))


def _norm_mm_body(x_ref, g_ref, w_ref, o_ref, h_ref):
    @pl.when(pl.program_id(1) == 0)
    def _():
        x = x_ref[...]
        ms = jnp.mean(x * x, axis=-1, keepdims=True)
        h_ref[...] = (x * lax.rsqrt(ms + RMS_EPS) * g_ref[...]).astype(BF)

    o_ref[...] = _dot(h_ref[...], w_ref[...])


def norm_matmul(x, g, w_bf, tm, tn):
    m, k = x.shape
    n = w_bf.shape[1]
    return pl.pallas_call(
        _norm_mm_body,
        grid=(m // tm, n // tn),
        in_specs=[pl.BlockSpec((tm, k), lambda i, j: (i, 0)),
                  pl.BlockSpec((1, k), lambda i, j: (0, 0)),
                  pl.BlockSpec((k, tn), lambda i, j: (0, j))],
        out_specs=pl.BlockSpec((tm, tn), lambda i, j: (i, j)),
        out_shape=jax.ShapeDtypeStruct((m, n), F32),
        scratch_shapes=[pltpu.VMEM((tm, k), BF)],
        compiler_params=_cparams(("parallel", "arbitrary")),
        name="norm_matmul",
    )(x, g.reshape(1, k), w_bf)


def _mm_res_body(*refs, n_lhs):
    a_refs = refs[:n_lhs]
    w_refs = refs[n_lhs:2 * n_lhs]
    r_ref = refs[2 * n_lhs]
    o_ref = refs[2 * n_lhs + 1]
    s_refs = refs[2 * n_lhs + 2:]

    @pl.when(pl.program_id(1) == 0)
    def _():
        for a_ref, s_ref in zip(a_refs, s_refs):
            s_ref[...] = a_ref[...].astype(BF)

    acc = r_ref[...]
    for s_ref, w_ref in zip(s_refs, w_refs):
        acc = acc + _dot(s_ref[...], w_ref[...])
    o_ref[...] = acc


def matmul_residual(lhs, ws_bf, res, tm, tn):
    m, n = res.shape
    n_lhs = len(lhs)
    in_specs = ([pl.BlockSpec((tm, a.shape[1]), lambda i, j: (i, 0)) for a in lhs]
                + [pl.BlockSpec((w.shape[0], tn), lambda i, j: (0, j)) for w in ws_bf]
                + [pl.BlockSpec((tm, tn), lambda i, j: (i, j))])
    return pl.pallas_call(
        functools.partial(_mm_res_body, n_lhs=n_lhs),
        grid=(m // tm, n // tn),
        in_specs=in_specs,
        out_specs=pl.BlockSpec((tm, tn), lambda i, j: (i, j)),
        out_shape=jax.ShapeDtypeStruct((m, n), F32),
        scratch_shapes=[pltpu.VMEM((tm, a.shape[1]), BF) for a in lhs],
        compiler_params=_cparams(("parallel", "arbitrary")),
        name="matmul_residual",
    )(*lhs, *ws_bf, res)


def _rmsnorm_body(x_ref, g_ref, o_ref):
    x = x_ref[...]
    ms = jnp.mean(x * x, axis=-1, keepdims=True)
    o_ref[...] = x * lax.rsqrt(ms + RMS_EPS) * g_ref[...]


def rmsnorm(x, g, tm):
    m, k = x.shape
    return pl.pallas_call(
        _rmsnorm_body,
        grid=(m // tm,),
        in_specs=[pl.BlockSpec((tm, k), lambda i: (i, 0)),
                  pl.BlockSpec((1, k), lambda i: (0, 0))],
        out_specs=pl.BlockSpec((tm, k), lambda i: (i, 0)),
        out_shape=jax.ShapeDtypeStruct((m, k), F32),
        compiler_params=_cparams(("parallel",)),
        name="rmsnorm",
    )(x, g.reshape(1, k))


def _hgrn_static(c):
    levels = []
    m = 1
    while m < c:
        levels.append(m)
        m *= 2
    t = np.arange(c)
    rows = [t[None, :] <= t[:, None]]
    masks = [np.eye(c, dtype=bool)]
    for m in levels:
        blk = t // (2 * m)
        pos = t % (2 * m)
        bnd = blk * 2 * m + m - 1
        right = pos >= m
        left = pos < m
        e_rows = (t[None, :] > bnd[:, None]) & (t[None, :] <= t[:, None]) & right[:, None]
        f_rows = (t[None, :] > t[:, None]) & (t[None, :] <= bnd[:, None]) & left[:, None]
        rows.append(e_rows | f_rows)
        masks.append((blk[:, None] == blk[None, :]) & right[:, None] & left[None, :])
    m_all = np.stack(rows).astype(np.float32)
    masks = np.stack(masks).astype(np.float32)
    return m_all, masks, len(levels)


def _hgrn_gates(qa, fa, lb):
    f = lb + (1.0 - lb) * _sigmoid(fa)
    q = qa * _sigmoid(qa)
    return q, f


def _hgrn_prompt_body(qa_ref, fa_ref, ia_ref, ga_ref, lb_ref, gn_ref, mall_ref, masks_ref,
                      oa_ref, st_out_ref, st_ref, *, chunk, tblock, n_levels):
    c = chunk
    t_idx = pl.program_id(1)

    @pl.when(t_idx == 0)
    def _():
        st_ref[...] = jnp.zeros_like(st_ref)

    gn = gn_ref[...]

    def chunk_step(ci, carry):
        r0 = pl.multiple_of(ci * c, c)
        for h in range(HEADS):
            cols = slice(h * HEAD_DIM, (h + 1) * HEAD_DIM)
            qa = qa_ref[0, pl.ds(r0, c), cols]
            fa = fa_ref[0, pl.ds(r0, c), cols]
            ia = ia_ref[0, pl.ds(r0, c), cols]
            ga = ga_ref[0, pl.ds(r0, c), cols]
            q, f = _hgrn_gates(qa, fa, lb_ref[:, cols])
            k = 1.0 - f
            g = jnp.log(f)
            g_hi = g.astype(BF)
            g_lo = (g - g_hi.astype(F32)).astype(BF)
            b = _dot(mall_ref[0], g_hi) + _dot(mall_ref[0], g_lo)
            b_last = b[c - 1:c]
            st = st_ref[h]
            v = ia.astype(BF)
            o = _dot_nt((q * jnp.exp(b)).astype(BF), st.astype(BF))
            a = masks_ref[0] * _dot_nt(q.astype(BF), k.astype(BF))
            for li in range(n_levels):
                ml = mall_ref[1 + li]
                w = jnp.exp(_dot(ml, g_hi) + _dot(ml, g_lo))
                a = a + masks_ref[1 + li] * _dot_nt((q * w).astype(BF), (k * w).astype(BF))
            o = o + _dot(a.astype(BF), v)
            k_st = (k * jnp.exp(b_last - b)).astype(BF)
            st_ref[h] = st * jnp.exp(b_last) + _dot_tn(v, k_st)
            ms = jnp.mean(o * o, axis=-1, keepdims=True)
            y = o * lax.rsqrt(ms + RMS_EPS) * gn * (ga * _sigmoid(ga))
            oa_ref[0, pl.ds(r0, c), cols] = y
        return carry

    lax.fori_loop(0, tblock // c, chunk_step, 0)

    @pl.when(t_idx == pl.num_programs(1) - 1)
    def _():
        st_out_ref[0] = st_ref[...]


def hgrn_prompt(z3, lb, gn, tblock=256, chunk=HGRN_CHUNK):
    bsz, seq = z3.shape[:2]
    width = HEADS * HEAD_DIM
    m_all, masks, n_levels = _hgrn_static(chunk)
    zspec = lambda cb: pl.BlockSpec((1, tblock, width), lambda b, t, cb=cb: (b, t, cb))
    oa, st = pl.pallas_call(
        functools.partial(_hgrn_prompt_body, chunk=chunk, tblock=tblock, n_levels=n_levels),
        grid=(bsz, seq // tblock),
        in_specs=[zspec(0), zspec(1), zspec(2), zspec(3),
                  pl.BlockSpec((1, width), lambda b, t: (0, 0)),
                  pl.BlockSpec((1, HEAD_DIM), lambda b, t: (0, 0)),
                  pl.BlockSpec(m_all.shape, lambda b, t: (0, 0, 0)),
                  pl.BlockSpec(masks.shape, lambda b, t: (0, 0, 0))],
        out_specs=[pl.BlockSpec((1, tblock, width), lambda b, t: (b, t, 0)),
                   pl.BlockSpec((1, HEADS, HEAD_DIM, HEAD_DIM), lambda b, t: (b, 0, 0, 0))],
        out_shape=[jax.ShapeDtypeStruct((bsz, seq, width), F32),
                   jax.ShapeDtypeStruct((bsz, HEADS, HEAD_DIM, HEAD_DIM), F32)],
        scratch_shapes=[pltpu.VMEM((HEADS, HEAD_DIM, HEAD_DIM), F32)],
        compiler_params=_cparams(("parallel", "arbitrary")),
        name="hgrn_prompt",
    )(z3, z3, z3, z3, lb.reshape(1, width), gn.reshape(1, HEAD_DIM),
      jnp.asarray(m_all, BF), jnp.asarray(masks, F32))
    return oa, jnp.swapaxes(st, -1, -2)


def _col(row, eye):
    return jnp.sum(eye * row, axis=1, keepdims=True)


def _hgrn_sample_body(qa_ref, fa_ref, ia_ref, ga_ref, lb_ref, gn_ref, s_ref, oa_ref, so_ref):
    eye = (lax.broadcasted_iota(I32, (HEAD_DIM, HEAD_DIM), 0)
           == lax.broadcasted_iota(I32, (HEAD_DIM, HEAD_DIM), 1)).astype(F32)
    for g in range(SAMPLE_SEQ_GROUP):
        q8, f8 = _hgrn_gates(qa_ref[g], fa_ref[g], lb_ref[...])
        ga = ga_ref[g]
        gate = ga * _sigmoid(ga)
        outs = []
        for h in range(HEADS):
            f_col = _col(f8[h:h + 1], eye)
            q_col = _col(q8[h:h + 1], eye)
            s_new = f_col * s_ref[g, h] + (1.0 - f_col) * ia_ref[g, h:h + 1]
            so_ref[g, h] = s_new
            outs.append(jnp.sum(q_col * s_new, axis=0, keepdims=True))
        o = jnp.concatenate(outs, axis=0)
        ms = jnp.mean(o * o, axis=-1, keepdims=True)
        oa_ref[g] = o * lax.rsqrt(ms + RMS_EPS) * gn_ref[...] * gate


def hgrn_sample(zs3, lb, gn, s0):
    bsz = zs3.shape[0]
    g = SAMPLE_SEQ_GROUP
    assert bsz % g == 0
    zspec = lambda cb: pl.BlockSpec((g, HEADS, HEAD_DIM), lambda b, cb=cb: (b, cb, 0))
    sspec = pl.BlockSpec((g, HEADS, HEAD_DIM, HEAD_DIM), lambda b: (b, 0, 0, 0))
    return pl.pallas_call(
        _hgrn_sample_body,
        grid=(bsz // g,),
        in_specs=[zspec(0), zspec(1), zspec(2), zspec(3),
                  pl.BlockSpec((HEADS, HEAD_DIM), lambda b: (0, 0)),
                  pl.BlockSpec((1, HEAD_DIM), lambda b: (0, 0)),
                  sspec],
        out_specs=[pl.BlockSpec((g, HEADS, HEAD_DIM), lambda b: (b, 0, 0)), sspec],
        out_shape=[jax.ShapeDtypeStruct((bsz, HEADS, HEAD_DIM), F32),
                   jax.ShapeDtypeStruct(s0.shape, F32)],
        compiler_params=_cparams(("parallel",)),
        name="hgrn_sample",
    )(zs3, zs3, zs3, zs3, lb.reshape(HEADS, HEAD_DIM), gn.reshape(1, HEAD_DIM), s0)


_KEY_NEG_INF = np.int32(np.uint32(0x807FFFFF).astype(np.int64) - (1 << 32))
_INT_MIN = np.int32(-(1 << 31))


def _count(u_ref, n_groups, group, thr, cmp):
    rows = u_ref.shape[1]
    step = min(rows, LANES)
    parts = []
    for r0 in range(0, rows, step):
        t = jnp.broadcast_to(thr[r0:r0 + step], (step, LANES))

        def body(gi, acc, r0=r0, t=t):
            for i in range(group):
                acc = acc + cmp(u_ref[gi * group + i, r0:r0 + step], t).astype(F32)
            return acc

        acc = lax.fori_loop(0, n_groups, body, jnp.zeros((step, LANES), F32))
        parts.append(jnp.sum(acc, axis=-1, keepdims=True))
    return parts[0] if len(parts) == 1 else jnp.concatenate(parts, axis=0)


def _topk_mask(score_ref, u_ref, write_tile, n_tiles, n_groups, group, k):
    rows = score_ref.shape[1]
    n_live = n_groups * group

    def to_key(t, carry):
        bits = pltpu.bitcast(score_ref[t], I32)
        u_ref[t] = jnp.where(bits < 0, bits ^ np.int32(0x7FFFFFFF), bits)
        return carry

    lax.fori_loop(0, n_live, to_key, 0)

    kf = float(k)
    ge = lambda u, t: u >= t
    cnt = _count(u_ref, n_groups, group, jnp.zeros((rows, 1), I32), ge)
    lo = jnp.where(cnt >= kf, np.int32(0), _INT_MIN)

    def bit_step(i, lo):
        cand = lo | (np.int32(1) << (30 - i))
        cnt = _count(u_ref, n_groups, group, cand, ge)
        return jnp.where(cnt >= kf, cand, lo)

    lo = lax.fori_loop(0, 31, bit_step, lo)
    c_gt = _count(u_ref, n_groups, group, lo, lambda u, t: u > t)
    c_eq = _count(u_ref, n_groups, group, lo, lambda u, t: u == t)
    need = kf - c_gt
    real = lo > _KEY_NEG_INF
    excess = jnp.where(real & (c_eq > need), 1.0, 0.0)
    any_excess = jnp.max(excess) > 0.0

    @pl.when(jnp.logical_not(any_excess))
    def _():
        def emit(t, carry):
            u = u_ref[t]
            write_tile(t, jnp.where((u >= lo) & (u > _KEY_NEG_INF), 1.0, 0.0))
            return carry

        lax.fori_loop(0, n_live, emit, 0)

    @pl.when(any_excess)
    def _():
        upper = (lax.broadcasted_iota(I32, (LANES, LANES), 0)
                 <= lax.broadcasted_iota(I32, (LANES, LANES), 1)).astype(BF)

        def emit(t, seen):
            u = u_ref[t]
            eq = jnp.where(u == lo, 1.0, 0.0)
            prefix = seen + _dot(eq.astype(BF), upper)
            take = (u > lo) | ((u == lo) & (prefix <= need))
            write_tile(t, jnp.where(take & (u > _KEY_NEG_INF), 1.0, 0.0))
            return seen + jnp.sum(eq, axis=-1, keepdims=True)

        lax.fori_loop(0, n_live, emit, jnp.zeros((rows, 1), F32))

    def clear(t, carry):
        write_tile(t, jnp.zeros((rows, LANES), F32))
        return carry

    lax.fori_loop(n_live, n_tiles, clear, 0)


IDX_KCHUNK = 512
IDX_QROWS = 256


def _indexer_prompt_body(qi_ref, tailq_ref, tail_ref, mask_ref, score_ref, u_ref, *, seq):
    j = pl.program_id(1)
    rows = IDX_QROWS
    qblocks = rows // LANES
    n_tiles = seq // LANES
    tiles_per_chunk = IDX_KCHUNK // LANES
    qi = qi_ref[0].astype(BF)
    w = tailq_ref[0][:, IDX_DIM:IDX_DIM + IDX_HEADS] * (IDX_DIM ** -0.5 * IDX_HEADS ** -0.5)

    n_chunks = (j * rows + rows - 1) // IDX_KCHUNK + 1
    q_pos = j * rows + lax.broadcasted_iota(I32, (rows, IDX_KCHUNK), 0)

    def chunk_step(ci, carry):
        k0 = pl.multiple_of(ci * IDX_KCHUNK, IDX_KCHUNK)
        kic = tail_ref[0, pl.ds(k0, IDX_KCHUNK), :][:, 0:IDX_DIM].astype(BF)
        acc = jnp.zeros((rows, IDX_KCHUNK), F32)
        for h in range(IDX_HEADS):
            s = _dot_nt(qi[:, h * IDX_DIM:(h + 1) * IDX_DIM], kic)
            acc = acc + w[:, h:h + 1] * jnp.maximum(s, 0.0)
        k_pos = k0 + lax.broadcasted_iota(I32, (rows, IDX_KCHUNK), 1)
        acc = jnp.where(k_pos <= q_pos, acc, -jnp.inf)
        for i in range(tiles_per_chunk):
            score_ref[ci * tiles_per_chunk + i] = acc[:, i * LANES:(i + 1) * LANES]
        return carry

    lax.fori_loop(0, n_chunks, chunk_step, 0)

    def write_tile(t, m):
        mb = m.astype(BF)
        for qb in range(qblocks):
            mask_ref[0, qb, t] = mb[qb * LANES:(qb + 1) * LANES]

    _topk_mask(score_ref, u_ref, write_tile, n_tiles, n_chunks, tiles_per_chunk, TOPK)


def indexer_prompt(z3, k):
    bsz, seq = z3.shape[:2]
    nqb = seq // LANES
    rows = IDX_QROWS
    tail_block = TAIL_BLOCK
    assert k == TOPK and seq % rows == 0 and seq % IDX_KCHUNK == 0
    return pl.pallas_call(
        functools.partial(_indexer_prompt_body, seq=seq),
        grid=(bsz, seq // rows),
        in_specs=[pl.BlockSpec((1, rows, IDX_HEADS * IDX_DIM), lambda b, j: (b, j, 7)),
                  pl.BlockSpec((1, rows, LANES), lambda b, j: (b, j, tail_block)),
                  pl.BlockSpec((1, seq, LANES), lambda b, j: (b, 0, tail_block))],
        out_specs=pl.BlockSpec((1, rows // LANES, nqb, LANES, LANES), lambda b, j: (b, j, 0, 0, 0)),
        out_shape=jax.ShapeDtypeStruct((bsz, nqb, nqb, LANES, LANES), BF),
        scratch_shapes=[pltpu.VMEM((nqb, rows, LANES), F32),
                        pltpu.VMEM((nqb, rows, LANES), I32)],
        compiler_params=_cparams(("parallel", "arbitrary")),
        name="indexer_prompt",
    )(z3, z3, z3)


ATTN_KSTEP = 512
ATTN_TILES = ATTN_KSTEP // LANES


def _attn_prompt_body(bfar_ref, q_ref, k_ref, v_ref, mask_ref, bias_ref, o_ref, m_ref, l_ref, acc_ref):
    j = pl.program_id(1)
    scale = HEAD_DIM ** -0.5
    q = q_ref[0].astype(BF)
    m_ref[...] = jnp.full(m_ref.shape, NEG_BIG, F32)
    l_ref[...] = jnp.zeros(l_ref.shape, F32)
    acc_ref[...] = jnp.zeros(acc_ref.shape, F32)

    def process(sb, near):
        k0 = pl.multiple_of(sb * ATTN_KSTEP, ATTN_KSTEP)
        kblk = k_ref[0, pl.ds(k0, ATTN_KSTEP), :]
        vblk = v_ref[0, pl.ds(k0, ATTN_KSTEP), :]
        tiles = [sb * ATTN_TILES + i for i in range(ATTN_TILES)]
        sel = jnp.concatenate([mask_ref[0, 0, t] for t in tiles], axis=1).astype(F32) > 0.0
        if near:
            bidx = [jnp.where(j - t == 0, 0, jnp.where(j - t == 1, 1, 2)) for t in tiles]
        for h in range(HEADS):
            cols = slice(h * HEAD_DIM, (h + 1) * HEAD_DIM)
            lg = _dot_nt(q[:, cols], kblk[:, cols]) * scale
            if near:
                lg = lg + jnp.concatenate([bias_ref[bi, h] for bi in bidx], axis=1)
            else:
                lg = lg + bfar_ref[h]
            lg = jnp.where(sel, lg, NEG_BIG)
            m_old = m_ref[h]
            m_new = jnp.maximum(m_old, jnp.max(lg, axis=-1, keepdims=True))
            p = jnp.exp(lg - jnp.concatenate([m_new] * ATTN_TILES, axis=1))
            alpha = jnp.exp(m_old - m_new)
            l_ref[h] = alpha * l_ref[h] + jnp.sum(p, axis=-1, keepdims=True)
            acc_ref[:, cols] = alpha * acc_ref[:, cols] + _dot(p.astype(BF), vblk[:, cols])
            m_ref[h] = m_new

    n_far = jnp.maximum((j - 1) // ATTN_TILES, 0)

    def far_step(sb, carry):
        process(sb, False)
        return carry

    def near_step(sb, carry):
        process(sb, True)
        return carry

    lax.fori_loop(0, n_far, far_step, 0)
    lax.fori_loop(n_far, j // ATTN_TILES + 1, near_step, 0)

    for h in range(HEADS):
        cols = slice(h * HEAD_DIM, (h + 1) * HEAD_DIM)
        o_ref[0, :, cols] = acc_ref[:, cols] / l_ref[h]


def attn_prompt(z3, kv_bf, mask, bias_tiles, bias_far):
    bsz, seq = z3.shape[:2]
    nqb = seq // LANES
    width = HEADS * HEAD_DIM
    assert seq % ATTN_KSTEP == 0
    return pl.pallas_call(
        _attn_prompt_body,
        grid=(bsz, nqb),
        in_specs=[pl.BlockSpec(memory_space=pltpu.SMEM),
                  pl.BlockSpec((1, LANES, width), lambda b, j: (b, j, 4)),
                  pl.BlockSpec((1, seq, width), lambda b, j: (b, 0, 0)),
                  pl.BlockSpec((1, seq, width), lambda b, j: (b, 0, 1)),
                  pl.BlockSpec((1, 1, nqb, LANES, LANES), lambda b, j: (b, j, 0, 0, 0)),
                  pl.BlockSpec(bias_tiles.shape, lambda b, j: (0, 0, 0, 0))],
        out_specs=pl.BlockSpec((1, LANES, width), lambda b, j: (b, j, 0)),
        out_shape=jax.ShapeDtypeStruct((bsz, seq, width), F32),
        scratch_shapes=[pltpu.VMEM((HEADS, LANES, LANES), F32),
                        pltpu.VMEM((HEADS, LANES, LANES), F32),
                        pltpu.VMEM((LANES, width), F32)],
        compiler_params=_cparams(("parallel", "arbitrary")),
        name="attn_prompt",
    )(bias_far, z3, kv_bf, kv_bf, mask, bias_tiles)


QW = 256
SUB = 8


def _row_all(x8, op):
    return jnp.broadcast_to(op(x8, axis=0, keepdims=True), x8.shape)


def _count_t(u_ref, n_groups, group, thr_row, cmp):
    qw = u_ref.shape[2]
    thr = jnp.broadcast_to(thr_row, (SUB, qw))

    def body(gi, acc):
        for i in range(group):
            u = u_ref[gi * group + i].reshape(LANES // SUB, SUB, qw)
            acc = acc + jnp.sum(cmp(u, thr[None]).astype(F32), axis=0)
        return acc

    acc = lax.fori_loop(0, n_groups, body, jnp.zeros((SUB, qw), F32))
    return jnp.sum(acc, axis=0, keepdims=True)


def _topk_mask_t(score_ref, u_ref, write_tile, n_tiles, n_groups, group, k):
    qw = score_ref.shape[2]
    n_live = n_groups * group

    def to_key(t, carry):
        bits = pltpu.bitcast(score_ref[t], I32)
        u_ref[t] = jnp.where(bits < 0, bits ^ np.int32(0x7FFFFFFF), bits)
        return carry

    lax.fori_loop(0, n_live, to_key, 0)

    kf = float(k)
    ge = lambda u, t: u >= t
    cnt = _count_t(u_ref, n_groups, group, jnp.zeros((1, qw), I32), ge)
    lo = jnp.where(cnt >= kf, np.int32(0), _INT_MIN)

    def bit_step(i, lo):
        cand = lo | (np.int32(1) << (30 - i))
        cnt = _count_t(u_ref, n_groups, group, cand, ge)
        return jnp.where(cnt >= kf, cand, lo)

    lo = lax.fori_loop(0, 31, bit_step, lo)
    c_gt = _count_t(u_ref, n_groups, group, lo, lambda u, t: u > t)
    c_eq = _count_t(u_ref, n_groups, group, lo, lambda u, t: u == t)
    need = kf - c_gt
    excess = jnp.where((lo > _KEY_NEG_INF) & (c_eq > need), 1.0, 0.0)
    any_excess = jnp.max(excess) > 0.0
    lo_b = jnp.broadcast_to(lo, (LANES, qw))

    @pl.when(jnp.logical_not(any_excess))
    def _():
        def emit(t, carry):
            u = u_ref[t]
            write_tile(t, jnp.where((u >= lo_b) & (u > _KEY_NEG_INF), 1.0, 0.0))
            return carry

        lax.fori_loop(0, n_live, emit, 0)

    @pl.when(any_excess)
    def _():
        lower = (lax.broadcasted_iota(I32, (LANES, LANES), 1)
                 <= lax.broadcasted_iota(I32, (LANES, LANES), 0)).astype(BF)

        def emit(t, seen):
            u = u_ref[t]
            eq = jnp.where(u == lo_b, 1.0, 0.0)
            prefix = seen + _dot(lower, eq.astype(BF))
            take = (u > lo_b) | ((u == lo_b) & (prefix <= need))
            write_tile(t, jnp.where(take & (u > _KEY_NEG_INF), 1.0, 0.0))
            return seen + jnp.sum(eq, axis=0, keepdims=True)

        lax.fori_loop(0, n_live, emit, jnp.zeros((1, qw), F32))

    def clear(t, carry):
        write_tile(t, jnp.zeros((LANES, qw), F32))
        return carry

    lax.fori_loop(n_live, n_tiles, clear, 0)


def _indexer_t_body(qi_ref, wt_ref, tail_ref, mask_ref, score_ref, u_ref, *, seq):
    j = pl.program_id(1)
    n_tiles = seq // LANES
    tiles_per_chunk = IDX_KCHUNK // LANES
    qi = qi_ref[0].astype(BF)
    wt = wt_ref[0] * (IDX_DIM ** -0.5 * IDX_HEADS ** -0.5)

    n_chunks = (j * QW + QW - 1) // IDX_KCHUNK + 1
    q_pos = j * QW + lax.broadcasted_iota(I32, (IDX_KCHUNK, QW), 1)

    def chunk_step(ci, carry):
        k0 = pl.multiple_of(ci * IDX_KCHUNK, IDX_KCHUNK)
        kic = tail_ref[0, pl.ds(k0, IDX_KCHUNK), :][:, 0:IDX_DIM].astype(BF)
        acc = jnp.zeros((IDX_KCHUNK, QW), F32)
        for h in range(IDX_HEADS):
            s = _dot_nt(kic, qi[:, h * IDX_DIM:(h + 1) * IDX_DIM])
            acc = acc + wt[h:h + 1] * jnp.maximum(s, 0.0)
        k_pos = k0 + lax.broadcasted_iota(I32, (IDX_KCHUNK, QW), 0)
        acc = jnp.where(k_pos <= q_pos, acc, -jnp.inf)
        for i in range(tiles_per_chunk):
            score_ref[ci * tiles_per_chunk + i] = acc[i * LANES:(i + 1) * LANES]
        return carry

    lax.fori_loop(0, n_chunks, chunk_step, 0)

    def write_tile(t, m):
        mask_ref[0, 0, t] = m.astype(BF)

    _topk_mask_t(score_ref, u_ref, write_tile, n_tiles, n_chunks, tiles_per_chunk, TOPK)


def indexer_prompt_t(z3, wt):
    bsz, seq = z3.shape[:2]
    n_tiles = seq // LANES
    tail_block = TAIL_BLOCK
    assert seq % QW == 0 and seq % IDX_KCHUNK == 0
    return pl.pallas_call(
        functools.partial(_indexer_t_body, seq=seq),
        grid=(bsz, seq // QW),
        in_specs=[pl.BlockSpec((1, QW, IDX_HEADS * IDX_DIM), lambda b, j: (b, j, 7)),
                  pl.BlockSpec((1, IDX_HEADS, QW), lambda b, j: (b, 0, j)),
                  pl.BlockSpec((1, seq, LANES), lambda b, j: (b, 0, tail_block))],
        out_specs=pl.BlockSpec((1, 1, n_tiles, LANES, QW), lambda b, j: (b, j, 0, 0, 0)),
        out_shape=jax.ShapeDtypeStruct((bsz, seq // QW, n_tiles, LANES, QW), BF),
        scratch_shapes=[pltpu.VMEM((n_tiles, LANES, QW), F32),
                        pltpu.VMEM((n_tiles, LANES, QW), I32)],
        compiler_params=_cparams(("parallel", "arbitrary")),
        name="indexer_prompt_t",
    )(z3, wt, z3)


def _attn_t_body(bfar_ref, q_ref, k_ref, vt_ref, mask_ref, bias_ref, o_ref, m_ref, l_ref, acc_ref):
    j = pl.program_id(1)
    scale = HEAD_DIM ** -0.5
    qsub = QW // LANES
    q = q_ref[0].astype(BF)
    m_ref[...] = jnp.full(m_ref.shape, NEG_BIG, F32)
    l_ref[...] = jnp.zeros(l_ref.shape, F32)
    acc_ref[...] = jnp.zeros(acc_ref.shape, F32)

    def process(sb, near):
        k0 = pl.multiple_of(sb * ATTN_KSTEP, ATTN_KSTEP)
        kblk = k_ref[0, pl.ds(k0, ATTN_KSTEP), :]
        vblk = vt_ref[0, pl.ds(k0, ATTN_KSTEP), :]
        tiles = [sb * ATTN_TILES + i for i in range(ATTN_TILES)]
        sel = jnp.concatenate([mask_ref[0, 0, t] for t in tiles], axis=0).astype(F32) > 0.0
        if near:
            def tile_bias(t, h):
                row = []
                for s in range(qsub):
                    d = j * qsub + s - t
                    row.append(bias_ref[jnp.where(d == 0, 0, jnp.where(d == 1, 1, 2)), h])
                return jnp.concatenate(row, axis=1)
        for h in range(HEADS):
            cols = slice(h * HEAD_DIM, (h + 1) * HEAD_DIM)
            lg = _dot_nt(kblk[:, cols], q[:, cols]) * scale
            if near:
                lg = lg + jnp.concatenate([tile_bias(t, h) for t in tiles], axis=0)
            else:
                lg = lg + bfar_ref[h]
            lg = jnp.where(sel, lg, NEG_BIG)
            m_old = m_ref[h]
            part = jnp.max(lg.reshape(ATTN_KSTEP // SUB, SUB, QW), axis=0)
            m_new = jnp.maximum(m_old, _row_all(part, jnp.max))
            p = jnp.exp(lg - m_new[0:1])
            alpha = jnp.exp(m_old - m_new)
            psum = jnp.sum(p.reshape(ATTN_KSTEP // SUB, SUB, QW), axis=0)
            l_ref[h] = alpha * l_ref[h] + _row_all(psum, jnp.sum)
            pv = _dot_tn(vblk[:, cols], p.astype(BF))
            acc_ref[cols, :] = alpha[0:1] * acc_ref[cols, :] + pv
            m_ref[h] = m_new

    n_far = jnp.maximum((qsub * j - 1) // ATTN_TILES, 0)
    n_steps = (qsub * j + qsub - 1) // ATTN_TILES + 1

    def far_step(sb, carry):
        process(sb, False)
        return carry

    def near_step(sb, carry):
        process(sb, True)
        return carry

    lax.fori_loop(0, n_far, far_step, 0)
    lax.fori_loop(n_far, n_steps, near_step, 0)

    for h in range(HEADS):
        cols = slice(h * HEAD_DIM, (h + 1) * HEAD_DIM)
        o_ref[0, cols, :] = acc_ref[cols, :] / l_ref[h][0:1]


def attn_prompt_t(z3, kv_bf, mask_t, bias_tiles_t, bias_far):
    bsz, seq = z3.shape[:2]
    n_tiles = seq // LANES
    width = HEADS * HEAD_DIM
    assert seq % ATTN_KSTEP == 0 and seq % QW == 0
    return pl.pallas_call(
        _attn_t_body,
        grid=(bsz, seq // QW),
        in_specs=[pl.BlockSpec(memory_space=pltpu.SMEM),
                  pl.BlockSpec((1, QW, width), lambda b, j: (b, j, 4)),
                  pl.BlockSpec((1, seq, width), lambda b, j: (b, 0, 0)),
                  pl.BlockSpec((1, seq, width), lambda b, j: (b, 0, 1)),
                  pl.BlockSpec((1, 1, n_tiles, LANES, QW), lambda b, j: (b, j, 0, 0, 0)),
                  pl.BlockSpec(bias_tiles_t.shape, lambda b, j: (0, 0, 0, 0))],
        out_specs=pl.BlockSpec((1, width, QW), lambda b, j: (b, 0, j)),
        out_shape=jax.ShapeDtypeStruct((bsz, width, seq), F32),
        scratch_shapes=[pltpu.VMEM((HEADS, SUB, QW), F32),
                        pltpu.VMEM((HEADS, SUB, QW), F32),
                        pltpu.VMEM((width, QW), F32)],
        compiler_params=_cparams(("parallel", "arbitrary")),
        name="attn_prompt_t",
    )(bias_far, z3, kv_bf, kv_bf, mask_t, bias_tiles_t)


def _indexer_sample_body(pt_ref, qi_ref, wcol_ref, kinew_ref, *rest, n_pages):
    ki_refs = rest[:n_pages]
    out_ref = rest[n_pages]
    qi = qi_ref[0].astype(BF)
    w = wcol_ref[0] * (IDX_DIM ** -0.5 * IDX_HEADS ** -0.5)
    for i in range(n_pages):
        s = _dot(qi, ki_refs[i][0].astype(BF))
        out_ref[i, 0] = jnp.sum(w * jnp.maximum(s, 0.0), axis=0, keepdims=True)
    kn = kinew_ref[0].astype(BF).astype(F32)
    sn = jnp.sum(qi.astype(F32) * kn, axis=-1, keepdims=True)
    new = jnp.sum(w * jnp.maximum(sn, 0.0), axis=0, keepdims=True)
    lane = lax.broadcasted_iota(I32, (1, LANES), 1)
    out_ref[n_pages, 0] = jnp.where(lane == 0, new, -jnp.inf)


def indexer_sample(page_table, qi3, wcol, ki_new, ki_pool_t):
    bsz, n_pages = page_table.shape
    ki_spec = lambda i: pl.BlockSpec((1, IDX_DIM, PAGE), lambda b, pt, i=i: (pt[b, i], 0, 0))
    grid_spec = pltpu.PrefetchScalarGridSpec(
        num_scalar_prefetch=1,
        grid=(bsz,),
        in_specs=[pl.BlockSpec((1, IDX_HEADS, IDX_DIM), lambda b, pt: (b, 0, 0)),
                  pl.BlockSpec((1, IDX_HEADS, 1), lambda b, pt: (b, 0, 0)),
                  pl.BlockSpec((1, 1, IDX_DIM), lambda b, pt: (b, 0, 0))]
                 + [ki_spec(i) for i in range(n_pages)],
        out_specs=pl.BlockSpec((n_pages + 1, 1, 1, LANES), lambda b, pt: (0, b, 0, 0)),
    )
    out = pl.pallas_call(
        functools.partial(_indexer_sample_body, n_pages=n_pages),
        grid_spec=grid_spec,
        out_shape=jax.ShapeDtypeStruct((n_pages + 1, bsz, 1, LANES), F32),
        compiler_params=_cparams(("arbitrary",)),
        name="indexer_sample",
    )(page_table, qi3, wcol, ki_new, *([ki_pool_t] * n_pages))
    return out.reshape(n_pages + 1, bsz, LANES)


def _select_body(score_ref, mask_ref, u_ref, *, n_tiles, k):
    def write_tile(t, m):
        mask_ref[t] = m

    _topk_mask(score_ref, u_ref, write_tile, n_tiles, n_tiles, 1, k)


def select_topk(scores, k):
    n_tiles, rows, _ = scores.shape
    return pl.pallas_call(
        functools.partial(_select_body, n_tiles=n_tiles, k=k),
        grid=(1,),
        in_specs=[pl.BlockSpec(scores.shape, lambda i: (0, 0, 0))],
        out_specs=pl.BlockSpec(scores.shape, lambda i: (0, 0, 0)),
        out_shape=jax.ShapeDtypeStruct(scores.shape, F32),
        scratch_shapes=[pltpu.VMEM(scores.shape, I32)],
        compiler_params=_cparams(("arbitrary",)),
        name="select_topk",
    )(scores)


SC_CORES = 2
SC_SUBCORES = 16
SC_GATHER_CHUNK = 32


def sc_gather_rows(tables, idx):
    n_rows = idx.shape[0]
    workers = SC_CORES * SC_SUBCORES
    per_worker = n_rows // workers
    chunk = SC_GATHER_CHUNK
    assert n_rows % workers == 0 and per_worker % chunk == 0 and chunk % 8 == 0
    row_shape = tables[0].shape[1:]
    n_tab = len(tables)
    mesh = plsc.VectorSubcoreMesh(core_axis_name="c", subcore_axis_name="s",
                                  num_cores=SC_CORES, num_subcores=SC_SUBCORES)

    def body(*refs):
        tab_refs = refs[:n_tab]
        idx_hbm = refs[n_tab]
        out_refs = refs[n_tab + 1:2 * n_tab + 1]
        idx_v, rows_v, sem = refs[2 * n_tab + 1:]
        wid = lax.axis_index("s") * SC_CORES + lax.axis_index("c")

        @pl.loop(0, per_worker // chunk)
        def _(ci):
            off = pl.multiple_of(wid * per_worker + ci * chunk, 8)
            pltpu.sync_copy(idx_hbm.at[pl.ds(off, chunk)], idx_v)
            for tab, out in zip(tab_refs, out_refs):
                pltpu.async_copy(tab.at[idx_v], rows_v, sem).wait()
                pltpu.sync_copy(rows_v, out.at[pl.ds(off, chunk)])

    return pl.kernel(
        body,
        out_type=[jax.ShapeDtypeStruct((n_rows, *row_shape), t.dtype) for t in tables],
        mesh=mesh,
        scratch_types=[pltpu.VMEM((chunk,), I32),
                       pltpu.VMEM((chunk, *row_shape), tables[0].dtype),
                       pltpu.SemaphoreType.DMA],
        compiler_params=pltpu.CompilerParams(use_tc_tiling_on_sc=True),
        name="sc_gather_rows",
    )(*tables, idx)


SAMPLE_SEQ_GROUP = 4


def _attn_compact_body(dist_ref, near_ref, q_ref, kn_ref, vn_ref, btab_ref, kc_ref, vc_ref,
                       o_ref, bbuf):
    step = pl.program_id(0)
    scale = HEAD_DIM ** -0.5
    ones = jnp.ones((HEAD_DIM, LANES), BF)
    far_bias = btab_ref[REL_MAX_DIST]
    last = lax.broadcasted_iota(I32, (TOPK, 1, 1), 0) == TOPK - 1
    for g in range(SAMPLE_SEQ_GROUP):
        b = step * SAMPLE_SEQ_GROUP + g
        bbuf[g] = jnp.broadcast_to(far_bias[None], (TOPK, HEADS, LANES))

        def fill(t, carry, g=g, b=b):
            bbuf[g, t] = btab_ref[dist_ref[b, t]]
            return carry

        lax.fori_loop(near_ref[b], TOPK, fill, 0)

        is_new = dist_ref[b, TOPK - 1] == 0
        rows = slice(g * TOPK, (g + 1) * TOPK)
        rnd = lambda a: a.astype(BF).astype(F32)
        kc = rnd(jnp.where(last & is_new, kn_ref[g][None], kc_ref[rows]))
        vc = rnd(jnp.where(last & is_new, vn_ref[g][None], vc_ref[rows]))
        prod = (kc * rnd(q_ref[g])[None]).reshape(TOPK * HEADS, HEAD_DIM)
        hi = prod.astype(BF)
        lo = (prod - hi.astype(F32)).astype(BF)
        lg = (_dot(hi, ones) + _dot(lo, ones)).reshape(TOPK, HEADS, LANES) * scale + bbuf[g]
        m = jnp.max(lg, axis=0, keepdims=True)
        p = jnp.exp(lg - m)
        l = jnp.sum(p, axis=0)
        o_ref[g] = jnp.sum(rnd(p) * vc, axis=0) / l


def attn_sample_compact(dist, first_near, zs3, k_sel, v_sel, bias_by_dist):
    bsz = dist.shape[0]
    g = SAMPLE_SEQ_GROUP
    assert bsz % g == 0
    zspec = lambda cb: pl.BlockSpec((g, HEADS, HEAD_DIM), lambda i, d, n, cb=cb: (i, cb, 0))
    sel_spec = pl.BlockSpec((g * TOPK, HEADS, HEAD_DIM), lambda i, d, n: (i, 0, 0))
    grid_spec = pltpu.PrefetchScalarGridSpec(
        num_scalar_prefetch=2,
        grid=(bsz // g,),
        in_specs=[zspec(4), zspec(5), zspec(6),
                  pl.BlockSpec(bias_by_dist.shape, lambda i, d, n: (0, 0, 0)),
                  sel_spec, sel_spec],
        out_specs=pl.BlockSpec((g, HEADS, HEAD_DIM), lambda i, d, n: (i, 0, 0)),
        scratch_shapes=[pltpu.VMEM((g, TOPK, HEADS, HEAD_DIM), F32)],
    )
    return pl.pallas_call(
        _attn_compact_body,
        grid_spec=grid_spec,
        out_shape=jax.ShapeDtypeStruct((bsz, HEADS, HEAD_DIM), F32),
        compiler_params=_cparams(("arbitrary",)),
        name="attn_sample_compact",
    )(dist, first_near, zs3, zs3, zs3, bias_by_dist, k_sel, v_sel)


def _bucket_table(max_dist):
    exact = REL_BUCKETS // 2
    d = np.arange(max_dist + 1)
    df = np.maximum(d, 1).astype(np.float32)
    far = exact + (np.log(df / exact) / np.float32(math.log(REL_MAX_DIST / exact))
                   * (REL_BUCKETS - exact)).astype(np.int32)
    return np.where(d < exact, d, np.minimum(far, REL_BUCKETS - 1)).astype(np.int32)


def _bias_tables(rel_bias):
    tab = _bucket_table(2 * LANES)
    assert np.all(tab[REL_MAX_DIST:] == REL_BUCKETS - 1)
    i = np.arange(LANES)
    dist0 = np.maximum(i[:, None] - i[None, :], 0)
    dist1 = LANES + i[:, None] - i[None, :]
    far = np.full((LANES, LANES), REL_BUCKETS - 1)
    idx = np.stack([tab[dist0], tab[dist1], far])
    buckets = jnp.arange(REL_BUCKETS)
    lookup = lambda ix, spec: jnp.einsum(spec, (jnp.asarray(ix)[..., None] == buckets).astype(F32),
                                         rel_bias.astype(F32), precision=lax.Precision.HIGHEST)
    tiles = lookup(idx, "tijb,bh->thij")
    by_dist = jnp.broadcast_to(lookup(tab[:REL_MAX_DIST + 1], "db,bh->dh")[:, :, None],
                               (REL_MAX_DIST + 1, HEADS, LANES))
    return tiles.astype(F32), rel_bias[REL_BUCKETS - 1].astype(F32), by_dist.astype(F32)


def _conv_prompt_body(bg_ref, cg_ref, xt_ref, w_ref, v_ref, st_ref, carry_ref, *, tblock):
    t = pl.program_id(1)

    @pl.when(t == 0)
    def _():
        carry_ref[...] = jnp.zeros_like(carry_ref)

    u = cg_ref[0] * xt_ref[0]
    row = lax.broadcasted_iota(I32, u.shape, 0)
    c0 = carry_ref[0:1]
    c1 = carry_ref[1:2]
    u1 = jnp.where(row == 0, c1, pltpu.roll(u, 1, axis=0))
    u2 = jnp.where(row == 0, c0, jnp.where(row == 1, c1, pltpu.roll(u, 2, axis=0)))
    conv = w_ref[0:1] * u2 + w_ref[1:2] * u1 + w_ref[2:3] * u
    v_ref[0] = bg_ref[0] * conv
    last = u[tblock - 2:tblock]
    carry_ref[0:2] = last

    @pl.when(t == pl.num_programs(1) - 1)
    def _():
        st_ref[0] = last


def conv_prompt(zc3, w_conv, tblock=256):
    bsz, seq = zc3.shape[:2]
    c = zc3.shape[2] // 3
    zspec = lambda cb: pl.BlockSpec((1, tblock, c), lambda b, t, cb=cb: (b, t, cb))
    return pl.pallas_call(
        functools.partial(_conv_prompt_body, tblock=tblock),
        grid=(bsz, seq // tblock),
        in_specs=[zspec(0), zspec(1), zspec(2), pl.BlockSpec((3, c), lambda b, t: (0, 0))],
        out_specs=[pl.BlockSpec((1, tblock, c), lambda b, t: (b, t, 0)),
                   pl.BlockSpec((1, 2, c), lambda b, t: (b, 0, 0))],
        out_shape=[jax.ShapeDtypeStruct((bsz, seq, c), F32),
                   jax.ShapeDtypeStruct((bsz, 2, c), F32)],
        scratch_shapes=[pltpu.VMEM((8, c), F32)],
        compiler_params=_cparams(("parallel", "arbitrary")),
        name="conv_prompt",
    )(zc3, zc3, zc3, w_conv)


def _conv_sample_body(bg_ref, cg_ref, xt_ref, w_ref, s0_ref, s1_ref, v_ref, n0_ref, n1_ref):
    u = cg_ref[...] * xt_ref[...]
    conv = w_ref[0:1] * s0_ref[...] + w_ref[1:2] * s1_ref[...] + w_ref[2:3] * u
    v_ref[...] = bg_ref[...] * conv
    n0_ref[...] = s1_ref[...]
    n1_ref[...] = u


def conv_sample(zc, w_conv, s0, s1):
    bsz = zc.shape[0]
    c = zc.shape[1] // 3
    zspec = lambda cb: pl.BlockSpec((bsz, c), lambda i, cb=cb: (0, cb))
    full = pl.BlockSpec((bsz, c), lambda i: (0, 0))
    return pl.pallas_call(
        _conv_sample_body,
        grid=(1,),
        in_specs=[zspec(0), zspec(1), zspec(2), pl.BlockSpec((3, c), lambda i: (0, 0)), full, full],
        out_specs=[full, full, full],
        out_shape=[jax.ShapeDtypeStruct((bsz, c), F32)] * 3,
        compiler_params=_cparams(("arbitrary",)),
        name="conv_sample",
    )(zc, zc, zc, w_conv, s0, s1)


def _router_body(x_ref, g_ref, wr_ref, br_ref, h_ref, route_ref):
    x = x_ref[...]
    ms = jnp.mean(x * x, axis=-1, keepdims=True)
    h = x * lax.rsqrt(ms + RMS_EPS) * g_ref[...]
    hb = h.astype(BF)
    half = h.shape[1] // 2
    bits = pltpu.bitcast(hb.astype(F32), jnp.uint32)
    h_ref[...] = (bits[:, :half] & jnp.uint32(0xFFFF0000)) | (bits[:, half:] >> 16)
    logits = _dot(hb, wr_ref[...]) + br_ref[...]
    lane = lax.broadcasted_iota(I32, logits.shape, 1)
    big = np.int32(1 << 20)
    lg = jnp.where(lane < N_GROUPS, logits, -jnp.inf)
    g_max = jnp.max(lg, axis=-1, keepdims=True)
    g_idx = jnp.min(jnp.where(lg == g_max, lane, big), axis=-1, keepdims=True)
    g_w = 1.0 / jnp.sum(jnp.exp(lg - g_max), axis=-1, keepdims=True)
    first = N_GROUPS + EXP_PER_GROUP * g_idx
    le = jnp.where((lane >= first) & (lane < first + EXP_PER_GROUP), logits, -jnp.inf)
    l1 = jnp.max(le, axis=-1, keepdims=True)
    i1 = jnp.min(jnp.where(le == l1, lane, big), axis=-1, keepdims=True)
    le2 = jnp.where(lane == i1, -jnp.inf, le)
    l2 = jnp.max(le2, axis=-1, keepdims=True)
    i2 = jnp.min(jnp.where(le2 == l2, lane, big), axis=-1, keepdims=True)
    r = jnp.exp(l2 - l1)
    w1 = g_w / (1.0 + r)
    w2 = g_w * r / (1.0 + r)
    e1 = (i1 - N_GROUPS).astype(F32)
    e2 = (i2 - N_GROUPS).astype(F32)
    route_ref[...] = jnp.where(lane == 0, e1, jnp.where(lane == 1, e2,
                               jnp.where(lane == 2, w1, jnp.where(lane == 3, w2, 0.0))))


def moe_router(x, g, wr, br, tm):
    m, k = x.shape
    return pl.pallas_call(
        _router_body,
        grid=(m // tm,),
        in_specs=[pl.BlockSpec((tm, k), lambda i: (i, 0)),
                  pl.BlockSpec((1, k), lambda i: (0, 0)),
                  pl.BlockSpec((k, LANES), lambda i: (0, 0)),
                  pl.BlockSpec((1, LANES), lambda i: (0, 0))],
        out_specs=[pl.BlockSpec((tm, k // 2), lambda i: (i, 0)),
                   pl.BlockSpec((tm, LANES), lambda i: (i, 0))],
        out_shape=[jax.ShapeDtypeStruct((m, k // 2), jnp.uint32),
                   jax.ShapeDtypeStruct((m, LANES), F32)],
        compiler_params=_cparams(("parallel",)),
        name="moe_router",
    )(x, g.reshape(1, k), wr, br)


def _experts_body(te_ref, nu_ref, h_ref, wg_ref, wu_ref, wd_ref, o_ref, wg_bf, wu_bf, wd_bf):
    i = pl.program_id(0)

    @pl.when((i == 0) | (te_ref[i] != te_ref[jnp.maximum(i - 1, 0)]))
    def _():
        wg_bf[...] = wg_ref[0].astype(BF)
        wu_bf[...] = wu_ref[0].astype(BF)
        wd_bf[...] = wd_ref[0].astype(BF)

    @pl.when(i < nu_ref[0])
    def _():
        words = h_ref[...]
        left = pltpu.bitcast(words & jnp.uint32(0xFFFF0000), F32)
        right = pltpu.bitcast(words << 16, F32)
        h = jnp.concatenate([left, right], axis=1).astype(BF)
        a = _dot(h, wg_bf[...])
        b = _dot(h, wu_bf[...])
        hid = a * _sigmoid(a) * b
        o_ref[...] = _dot(hid.astype(BF), wd_bf[...])

    @pl.when(i >= nu_ref[0])
    def _():
        o_ref[...] = jnp.zeros_like(o_ref)


def moe_experts(tile_expert, n_used, hs, wg, wu, wd):
    p = hs.shape[0]
    k = wg.shape[1]
    f = wg.shape[2]
    n_tiles = p // MOE_TILE
    grid_spec = pltpu.PrefetchScalarGridSpec(
        num_scalar_prefetch=2,
        grid=(n_tiles,),
        in_specs=[pl.BlockSpec((MOE_TILE, k // 2), lambda i, te, nu: (i, 0)),
                  pl.BlockSpec((1, k, f), lambda i, te, nu: (te[i], 0, 0)),
                  pl.BlockSpec((1, k, f), lambda i, te, nu: (te[i], 0, 0)),
                  pl.BlockSpec((1, f, k), lambda i, te, nu: (te[i], 0, 0))],
        out_specs=pl.BlockSpec((MOE_TILE, k), lambda i, te, nu: (i, 0)),
        scratch_shapes=[pltpu.VMEM((k, f), BF), pltpu.VMEM((k, f), BF), pltpu.VMEM((f, k), BF)],
    )
    return pl.pallas_call(
        _experts_body,
        grid_spec=grid_spec,
        out_shape=jax.ShapeDtypeStruct((p, k), F32),
        compiler_params=_cparams(("arbitrary",)),
        name="moe_experts",
    )(tile_expert, n_used, hs, wg, wu, wd)


def _rank_within_expert(onehot):
    n, e = onehot.shape
    blk = LANES
    assert n % blk == 0
    oh = onehot.astype(F32).reshape(n // blk, blk, e)
    strict = jnp.asarray(np.tril(np.ones((blk, blk), np.float32), -1))
    within = jnp.einsum("ij,bjk->bik", strict, oh, precision=lax.Precision.HIGHEST)
    totals = jnp.sum(oh, axis=1)
    before = jnp.cumsum(totals, axis=0) - totals
    rank = (within + before[:, None, :]).reshape(n, e)
    return jnp.sum(rank * onehot.astype(F32), axis=1).astype(I32), jnp.sum(totals, axis=0).astype(I32)


def hier_moe(xs, tms, g, wrg, brg, wre, bre, wg, wu, wd, layer):
    d = xs[0].shape[1]
    m = sum(x.shape[0] for x in xs)
    wr = jnp.zeros((d, LANES), F32)
    wr = wr.at[:, :N_GROUPS].set(wrg).at[:, N_GROUPS:N_GROUPS + N_EXPERTS].set(wre.reshape(d, N_EXPERTS))
    br = jnp.zeros((1, LANES), F32)
    br = br.at[0, :N_GROUPS].set(brg).at[0, N_GROUPS:N_GROUPS + N_EXPERTS].set(bre.reshape(N_EXPERTS))
    routed = [moe_router(x, g, wr.astype(BF), br, tm) for x, tm in zip(xs, tms)]
    h_bf = jnp.concatenate([r[0] for r in routed], axis=0)
    route = jnp.concatenate([r[1] for r in routed], axis=0)

    eid = route[:, 0:2].astype(I32).reshape(-1)
    onehot = eid[:, None] == jnp.arange(N_EXPERTS, dtype=I32)[None, :]
    rank, counts = _rank_within_expert(onehot)
    padded = ((counts + MOE_TILE - 1) // MOE_TILE) * MOE_TILE
    ends = jnp.cumsum(padded)
    pos = jnp.sum(jnp.where(onehot, (ends - padded)[None, :], 0), axis=1) + rank
    n_rows = 2 * m + N_EXPERTS * MOE_TILE
    n_rows = -(-n_rows // MOE_TILE) * MOE_TILE
    token = jnp.zeros((n_rows,), I32).at[pos].set(jnp.arange(2 * m, dtype=I32) // 2)
    tile_start = jnp.arange(n_rows // MOE_TILE, dtype=I32) * MOE_TILE
    tile_expert = jnp.minimum(jnp.sum(tile_start[:, None] >= ends[None, :], axis=1),
                              N_EXPERTS - 1).astype(I32)
    n_used = (ends[-1] // MOE_TILE).astype(I32).reshape(1)

    hs = jnp.take(h_bf, token, axis=0, mode="clip")
    out = moe_experts(tile_expert + layer * N_EXPERTS, n_used, hs, wg, wu, wd)
    pos2 = pos.reshape(m, 2)
    res, r0 = [], 0
    for x in xs:
        rows = slice(r0, r0 + x.shape[0])
        res.append(x + route[rows, 2:3] * jnp.take(out, pos2[rows, 0], axis=0, mode="clip")
                   + route[rows, 3:4] * jnp.take(out, pos2[rows, 1], axis=0, mode="clip"))
        r0 += x.shape[0]
    return res


def kernel(x_prompt, x_sample, cache_k, cache_v, cache_ki, state_hgrn, state_conv, page_table,
           norm_mix_g, norm_ffn_g, final_g, w_in_even, w_out_even, hgrn_lb_logits, hgrn_norm_g,
           rel_bias, w_in_conv, w_conv, w_out_conv, w_router_g, b_router_g, w_router_e,
           b_router_e, w_gate, w_up, w_down):
    bsz, seq, d = x_prompt.shape
    dec = x_sample.shape[0]
    n_p = bsz * seq
    tm_p, tm_s = 512, 128
    tms = (tm_p, tm_s)
    assert n_p % tm_p == 0 and dec % tm_s == 0 and x_sample.shape[1] == 1
    width = HEADS * HEAD_DIM
    even_in = w_in_even.shape[2]
    even_pad = -(-even_in // IN_PROJ_TN) * IN_PROJ_TN
    tail0 = TAIL_BLOCK * LANES
    n_pool = cache_k.shape[1]

    xp = x_prompt.reshape(n_p, d)
    xs = x_sample.reshape(dec, d)

    lbs = jnp.cumsum(jax.nn.softmax(hgrn_lb_logits.astype(F32), axis=0), axis=0)[:-1]
    bias_tiles, bias_far, bias_by_dist = _bias_tables(rel_bias)
    expert_w = [w.reshape(-1, *w.shape[2:]) for w in (w_gate, w_up, w_down)]

    w_in = jnp.pad(w_in_even[0], ((0, 0), (0, even_pad - even_in))).astype(BF)
    zp = norm_matmul(xp, norm_mix_g[0], w_in, 2 * tm_p, IN_PROJ_TN)
    zs = norm_matmul(xs, norm_mix_g[0], w_in, tm_s, IN_PROJ_TN)
    zp3 = zp.reshape(bsz, seq, even_pad)
    zs3 = zs.reshape(dec, even_pad // LANES, LANES)

    oa_p, hgrn_p = hgrn_prompt(zp3, lbs[0], hgrn_norm_g[0])
    oa_s, hgrn_s = hgrn_sample(zs3, lbs[0], hgrn_norm_g[0], state_hgrn[0])

    wt = jnp.swapaxes(zp3[:, :, tail0 + IDX_DIM:tail0 + IDX_DIM + IDX_HEADS], 1, 2)
    mask_p = indexer_prompt_t(zp3, wt)
    kv_bf = zp3[:, :, 5 * width:7 * width].astype(BF)
    ob_p = jnp.swapaxes(attn_prompt_t(zp3, kv_bf, mask_p,
                                      jnp.swapaxes(bias_tiles, -1, -2), bias_far), 1, 2)

    qi3 = zs[:, 7 * width:8 * width].reshape(dec, IDX_HEADS, IDX_DIM)
    ki_new = zs[:, tail0:tail0 + IDX_DIM].reshape(dec, 1, IDX_DIM)
    wcol = zs[:, tail0 + IDX_DIM:tail0 + IDX_DIM + IDX_HEADS].reshape(dec, IDX_HEADS, 1)
    pages = page_table + 0 * n_pool
    scores_s = indexer_sample(pages, qi3, wcol, ki_new,
                              jnp.swapaxes(cache_ki, -1, -2).reshape(-1, IDX_DIM, PAGE))
    mask_s = select_topk(scores_s, TOPK)
    sel = jnp.transpose(mask_s, (1, 0, 2))
    n_pages = page_table.shape[1]
    sel_off = jnp.concatenate([jnp.zeros((dec, 1), F32),
                               jnp.cumsum(jnp.sum(sel, axis=2), axis=1)], axis=1)
    slot = jnp.arange(TOPK, dtype=F32)
    page_of = jnp.sum(sel_off[:, None, 1:] <= slot[None, :, None], axis=2)
    page_1h = (page_of[:, :, None] == jnp.arange(n_pages + 1)[None, None, :]).astype(F32)
    local = slot[None, :] - jnp.einsum("brp,bp->br", page_1h, sel_off[:, :-1],
                                       precision=lax.Precision.HIGHEST)
    within = jnp.einsum("brp,bpl->brl", page_1h, jnp.cumsum(sel, axis=2),
                        precision=lax.Precision.HIGHEST)
    lane_of = jnp.sum(within <= local[:, :, None], axis=2)
    sel_idx = (page_of * PAGE + lane_of).astype(I32)
    past = n_pages * PAGE
    page_id = jnp.einsum("brp,bp->br", page_1h[:, :, :n_pages], pages.astype(F32),
                         precision=lax.Precision.HIGHEST).astype(I32)
    rows = jnp.where(page_of < n_pages, page_id * PAGE + lane_of, 0).astype(I32).reshape(-1)
    k_sel, v_sel = sc_gather_rows([cache_k.reshape(-1, HEADS, HEAD_DIM),
                                   cache_v.reshape(-1, HEADS, HEAD_DIM)], rows)
    dist = jnp.minimum(past - sel_idx, REL_MAX_DIST).astype(I32)
    first_near = jnp.sum(dist >= REL_MAX_DIST, axis=1).astype(I32)
    ob_s = attn_sample_compact(dist, first_near, zs3, k_sel, v_sel, bias_by_dist)

    w_out = w_out_even[0].astype(BF)
    w_out_ab = [w_out[:width], w_out[width:]]
    xp = matmul_residual([oa_p.reshape(n_p, width), ob_p.reshape(n_p, width)], w_out_ab, xp, 2 * tm_p, 512)
    xs = matmul_residual([oa_s.reshape(dec, width), ob_s.reshape(dec, width)], w_out_ab, xs, tm_s, 512)

    xp, xs = hier_moe([xp, xs], tms, norm_ffn_g[0], w_router_g[0], b_router_g[0], w_router_e[0],
                      b_router_e[0], *expert_w, 0)

    w_in_c = w_in_conv[0].astype(BF)
    zcp = norm_matmul(xp, norm_mix_g[1], w_in_c, 2 * tm_p, 1024)
    zcs = norm_matmul(xs, norm_mix_g[1], w_in_c, tm_s, 512)
    cw = zcp.shape[1] // 3
    v_p, conv_p = conv_prompt(zcp.reshape(bsz, seq, 3 * cw), w_conv[0])
    v_s, cs0, cs1 = conv_sample(zcs, w_conv[0], state_conv[0, :, 0], state_conv[0, :, 1])
    w_out_c = [w_out_conv[0].astype(BF)]
    xp = matmul_residual([v_p.reshape(n_p, cw)], w_out_c, xp, 2 * tm_p, 512)
    xs = matmul_residual([v_s], w_out_c, xs, tm_s, 512)

    xp, xs = hier_moe([xp, xs], tms, norm_ffn_g[1], w_router_g[1], b_router_g[1], w_router_e[1],
                      b_router_e[1], *expert_w, 1)

    yp = rmsnorm(xp, final_g, tm_p)
    ys = rmsnorm(xs, final_g, tm_s)

    kcol, vcol = 5 * width, 6 * width
    heads = lambda a, n: a.reshape(1, *n, HEADS, HEAD_DIM)
    return (yp.reshape(bsz, seq, d),
            ys.reshape(dec, 1, d),
            heads(zp[:, kcol:kcol + width], (bsz, seq)),
            heads(zp[:, vcol:vcol + width], (bsz, seq)),
            zp[:, tail0:tail0 + IDX_DIM].reshape(1, bsz, seq, IDX_DIM),
            hgrn_p[None],
            conv_p[None],
            heads(zs[:, kcol:kcol + width], (dec, 1)),
            heads(zs[:, vcol:vcol + width], (dec, 1)),
            zs[:, tail0:tail0 + IDX_DIM].reshape(1, dec, 1, IDX_DIM),
            hgrn_s[None],
            jnp.stack([cs0, cs1], axis=1)[None])
```

```python
import functools
import math

import numpy as np
import jax
import jax.numpy as jnp
from jax import lax
from jax.experimental import pallas as pl
from jax.experimental.pallas import tpu as pltpu
from jax.experimental.pallas import tpu_sc as plsc

F32 = jnp.float32
BF = jnp.bfloat16
I32 = jnp.int32

RMS_EPS = 1e-6
LANES = 128
NEG_BIG = -1e30
VMEM_LIMIT = 56 * 1024 * 1024

D_MODEL = 2048
HEADS = 8
HEAD_DIM = 128
IDX_HEADS = 16
IDX_DIM = 64
TOPK = 256
REL_BUCKETS = 32
REL_MAX_DIST = 128
N_EXPERTS = 16
EXP_PER_GROUP = 4
N_GROUPS = 4
D_EXPERT = 512
PAGE = 128

HGRN_CHUNK = 128
HGRN_HEAD_GROUP = 8
MOE_TILE = 256
IN_PROJ_TN = 640
TAIL_BLOCK = 8 * HEADS * HEAD_DIM // LANES


def _cparams(sem):
    return pltpu.CompilerParams(dimension_semantics=sem, vmem_limit_bytes=VMEM_LIMIT)


def _dot(a, b):
    return jnp.dot(a, b, preferred_element_type=F32)


def _dot_nt(a, b):
    return lax.dot_general(a, b, (((1,), (1,)), ((), ())), preferred_element_type=F32)


def _dot_tn(a, b):
    return lax.dot_general(a, b, (((0,), (0,)), ((), ())), preferred_element_type=F32)


def _sigmoid(x):
    return 1.0 / (1.0 + jnp.exp(-x))


def _norm_mm_body(x_ref, g_ref, w_ref, o_ref, h_ref):
    @pl.when(pl.program_id(1) == 0)
    def _():
        x = x_ref[...]
        ms = jnp.mean(x * x, axis=-1, keepdims=True)
        h_ref[...] = (x * lax.rsqrt(ms + RMS_EPS) * g_ref[...]).astype(BF)

    o_ref[...] = _dot(h_ref[...], w_ref[...])


def norm_matmul(x, g, w_bf, tm, tn):
    m, k = x.shape
    n = w_bf.shape[1]
    return pl.pallas_call(
        _norm_mm_body,
        grid=(m // tm, n // tn),
        in_specs=[pl.BlockSpec((tm, k), lambda i, j: (i, 0)),
                  pl.BlockSpec((1, k), lambda i, j: (0, 0)),
                  pl.BlockSpec((k, tn), lambda i, j: (0, j))],
        out_specs=pl.BlockSpec((tm, tn), lambda i, j: (i, j)),
        out_shape=jax.ShapeDtypeStruct((m, n), F32),
        scratch_shapes=[pltpu.VMEM((tm, k), BF)],
        compiler_params=_cparams(("parallel", "arbitrary")),
        name="norm_matmul",
    )(x, g.reshape(1, k), w_bf)


def _mm_res_body(*refs, n_lhs):
    a_refs = refs[:n_lhs]
    w_refs = refs[n_lhs:2 * n_lhs]
    r_ref = refs[2 * n_lhs]
    o_ref = refs[2 * n_lhs + 1]
    s_refs = refs[2 * n_lhs + 2:]

    @pl.when(pl.program_id(1) == 0)
    def _():
        for a_ref, s_ref in zip(a_refs, s_refs):
            s_ref[...] = a_ref[...].astype(BF)

    acc = r_ref[...]
    for s_ref, w_ref in zip(s_refs, w_refs):
        acc = acc + _dot(s_ref[...], w_ref[...])
    o_ref[...] = acc


def matmul_residual(lhs, ws_bf, res, tm, tn):
    m, n = res.shape
    n_lhs = len(lhs)
    in_specs = ([pl.BlockSpec((tm, a.shape[1]), lambda i, j: (i, 0)) for a in lhs]
                + [pl.BlockSpec((w.shape[0], tn), lambda i, j: (0, j)) for w in ws_bf]
                + [pl.BlockSpec((tm, tn), lambda i, j: (i, j))])
    return pl.pallas_call(
        functools.partial(_mm_res_body, n_lhs=n_lhs),
        grid=(m // tm, n // tn),
        in_specs=in_specs,
        out_specs=pl.BlockSpec((tm, tn), lambda i, j: (i, j)),
        out_shape=jax.ShapeDtypeStruct((m, n), F32),
        scratch_shapes=[pltpu.VMEM((tm, a.shape[1]), BF) for a in lhs],
        compiler_params=_cparams(("parallel", "arbitrary")),
        name="matmul_residual",
    )(*lhs, *ws_bf, res)


def _rmsnorm_body(x_ref, g_ref, o_ref):
    x = x_ref[...]
    ms = jnp.mean(x * x, axis=-1, keepdims=True)
    o_ref[...] = x * lax.rsqrt(ms + RMS_EPS) * g_ref[...]


def rmsnorm(x, g, tm):
    m, k = x.shape
    return pl.pallas_call(
        _rmsnorm_body,
        grid=(m // tm,),
        in_specs=[pl.BlockSpec((tm, k), lambda i: (i, 0)),
                  pl.BlockSpec((1, k), lambda i: (0, 0))],
        out_specs=pl.BlockSpec((tm, k), lambda i: (i, 0)),
        out_shape=jax.ShapeDtypeStruct((m, k), F32),
        compiler_params=_cparams(("parallel",)),
        name="rmsnorm",
    )(x, g.reshape(1, k))


def _hgrn_static(c):
    levels = []
    m = 1
    while m < c:
        levels.append(m)
        m *= 2
    t = np.arange(c)
    rows = [t[None, :] <= t[:, None]]
    masks = [np.eye(c, dtype=bool)]
    for m in levels:
        blk = t // (2 * m)
        pos = t % (2 * m)
        bnd = blk * 2 * m + m - 1
        right = pos >= m
        left = pos < m
        e_rows = (t[None, :] > bnd[:, None]) & (t[None, :] <= t[:, None]) & right[:, None]
        f_rows = (t[None, :] > t[:, None]) & (t[None, :] <= bnd[:, None]) & left[:, None]
        rows.append(e_rows | f_rows)
        masks.append((blk[:, None] == blk[None, :]) & right[:, None] & left[None, :])
    m_all = np.stack(rows).astype(np.float32)
    masks = np.stack(masks).astype(np.float32)
    return m_all, masks, len(levels)


def _hgrn_gates(qa, fa, lb):
    f = lb + (1.0 - lb) * _sigmoid(fa)
    q = qa * _sigmoid(qa)
    return q, f


def _hgrn_prompt_body(qa_ref, fa_ref, ia_ref, ga_ref, lb_ref, gn_ref, mall_ref, masks_ref,
                      oa_ref, st_out_ref, st_ref, *, chunk, tblock, n_levels):
    c = chunk
    t_idx = pl.program_id(1)

    @pl.when(t_idx == 0)
    def _():
        st_ref[...] = jnp.zeros_like(st_ref)

    gn = gn_ref[...]

    def chunk_step(ci, carry):
        r0 = pl.multiple_of(ci * c, c)
        for h0 in range(0, HEADS, HGRN_HEAD_GROUP):
            hs = range(h0, h0 + HGRN_HEAD_GROUP)
            cols = {h: slice(h * HEAD_DIM, (h + 1) * HEAD_DIM) for h in hs}
            q, k, g_hi, g_lo, v = {}, {}, {}, {}, {}
            for h in hs:
                q[h], f = _hgrn_gates(qa_ref[0, pl.ds(r0, c), cols[h]],
                                      fa_ref[0, pl.ds(r0, c), cols[h]], lb_ref[:, cols[h]])
                k[h] = 1.0 - f
                g = jnp.log(f)
                g_hi[h] = g.astype(BF)
                g_lo[h] = (g - g_hi[h].astype(F32)).astype(BF)
                v[h] = ia_ref[0, pl.ds(r0, c), cols[h]].astype(BF)
            b = {h: _dot(mall_ref[0], g_hi[h]) + _dot(mall_ref[0], g_lo[h]) for h in hs}
            o = {h: _dot_nt((q[h] * jnp.exp(b[h])).astype(BF), st_ref[h].astype(BF)) for h in hs}
            a = {h: masks_ref[0] * _dot_nt(q[h].astype(BF), k[h].astype(BF)) for h in hs}
            for li in range(n_levels):
                ml = mall_ref[1 + li]
                w = {h: jnp.exp(_dot(ml, g_hi[h]) + _dot(ml, g_lo[h])) for h in hs}
                for h in hs:
                    a[h] = a[h] + masks_ref[1 + li] * _dot_nt((q[h] * w[h]).astype(BF),
                                                              (k[h] * w[h]).astype(BF))
            for h in hs:
                o[h] = o[h] + _dot(a[h].astype(BF), v[h])
            for h in hs:
                b_last = b[h][c - 1:c]
                k_st = (k[h] * jnp.exp(b_last - b[h])).astype(BF)
                st_ref[h] = st_ref[h] * jnp.exp(b_last) + _dot_tn(v[h], k_st)
            for h in hs:
                ga = ga_ref[0, pl.ds(r0, c), cols[h]]
                ms = jnp.mean(o[h] * o[h], axis=-1, keepdims=True)
                oa_ref[0, pl.ds(r0, c), cols[h]] = (o[h] * lax.rsqrt(ms + RMS_EPS) * gn
                                                    * (ga * _sigmoid(ga)))
        return carry

    lax.fori_loop(0, tblock // c, chunk_step, 0)

    @pl.when(t_idx == pl.num_programs(1) - 1)
    def _():
        st_out_ref[0] = st_ref[...]


def hgrn_prompt(z3, lb, gn, tblock=256, chunk=HGRN_CHUNK):
    bsz, seq = z3.shape[:2]
    width = HEADS * HEAD_DIM
    m_all, masks, n_levels = _hgrn_static(chunk)
    zspec = lambda cb: pl.BlockSpec((1, tblock, width), lambda b, t, cb=cb: (b, t, cb))
    oa, st = pl.pallas_call(
        functools.partial(_hgrn_prompt_body, chunk=chunk, tblock=tblock, n_levels=n_levels),
        grid=(bsz, seq // tblock),
        in_specs=[zspec(0), zspec(1), zspec(2), zspec(3),
                  pl.BlockSpec((1, width), lambda b, t: (0, 0)),
                  pl.BlockSpec((1, HEAD_DIM), lambda b, t: (0, 0)),
                  pl.BlockSpec(m_all.shape, lambda b, t: (0, 0, 0)),
                  pl.BlockSpec(masks.shape, lambda b, t: (0, 0, 0))],
        out_specs=[pl.BlockSpec((1, tblock, width), lambda b, t: (b, t, 0)),
                   pl.BlockSpec((1, HEADS, HEAD_DIM, HEAD_DIM), lambda b, t: (b, 0, 0, 0))],
        out_shape=[jax.ShapeDtypeStruct((bsz, seq, width), F32),
                   jax.ShapeDtypeStruct((bsz, HEADS, HEAD_DIM, HEAD_DIM), F32)],
        scratch_shapes=[pltpu.VMEM((HEADS, HEAD_DIM, HEAD_DIM), F32)],
        compiler_params=_cparams(("parallel", "arbitrary")),
        name="hgrn_prompt",
    )(z3, z3, z3, z3, lb.reshape(1, width), gn.reshape(1, HEAD_DIM),
      jnp.asarray(m_all, BF), jnp.asarray(masks, F32))
    return oa, jnp.swapaxes(st, -1, -2)


def _col(row, eye):
    return jnp.sum(eye * row, axis=1, keepdims=True)


def _hgrn_sample_body(qa_ref, fa_ref, ia_ref, ga_ref, lb_ref, gn_ref, s_ref, oa_ref, so_ref):
    eye = (lax.broadcasted_iota(I32, (HEAD_DIM, HEAD_DIM), 0)
           == lax.broadcasted_iota(I32, (HEAD_DIM, HEAD_DIM), 1)).astype(F32)
    rnd = lambda a: a.astype(BF).astype(F32)
    for g in range(SAMPLE_SEQ_GROUP):
        q8, f8 = _hgrn_gates(qa_ref[g], fa_ref[g], lb_ref[...])
        ga = ga_ref[g]
        gate = ga * _sigmoid(ga)
        kr = rnd(1.0 - f8)
        vr = rnd(ia_ref[g])
        qfr = rnd(q8 * f8)
        qk = rnd(jnp.sum(rnd(q8) * kr, axis=-1, keepdims=True))
        outs = []
        for h in range(HEADS):
            f_col = _col(f8[h:h + 1], eye)
            s_old = s_ref[g, h]
            so_ref[g, h] = f_col * s_old + _col(kr[h:h + 1], eye) * vr[h:h + 1]
            outs.append(qk[h:h + 1] * vr[h:h + 1]
                        + jnp.sum(_col(qfr[h:h + 1], eye) * rnd(s_old), axis=0, keepdims=True))
        o = jnp.concatenate(outs, axis=0)
        ms = jnp.mean(o * o, axis=-1, keepdims=True)
        oa_ref[g] = o * lax.rsqrt(ms + RMS_EPS) * gn_ref[...] * gate


def hgrn_sample(zs3, lb, gn, s0):
    bsz = zs3.shape[0]
    g = SAMPLE_SEQ_GROUP
    assert bsz % g == 0
    zspec = lambda cb: pl.BlockSpec((g, HEADS, HEAD_DIM), lambda b, cb=cb: (b, cb, 0))
    sspec = pl.BlockSpec((g, HEADS, HEAD_DIM, HEAD_DIM), lambda b: (b, 0, 0, 0))
    return pl.pallas_call(
        _hgrn_sample_body,
        grid=(bsz // g,),
        in_specs=[zspec(0), zspec(1), zspec(2), zspec(3),
                  pl.BlockSpec((HEADS, HEAD_DIM), lambda b: (0, 0)),
                  pl.BlockSpec((1, HEAD_DIM), lambda b: (0, 0)),
                  sspec],
        out_specs=[pl.BlockSpec((g, HEADS, HEAD_DIM), lambda b: (b, 0, 0)), sspec],
        out_shape=[jax.ShapeDtypeStruct((bsz, HEADS, HEAD_DIM), F32),
                   jax.ShapeDtypeStruct(s0.shape, F32)],
        compiler_params=_cparams(("parallel",)),
        name="hgrn_sample",
    )(zs3, zs3, zs3, zs3, lb.reshape(HEADS, HEAD_DIM), gn.reshape(1, HEAD_DIM), s0)


_KEY_NEG_INF = np.int32(np.uint32(0x807FFFFF).astype(np.int64) - (1 << 32))
_INT_MIN = np.int32(-(1 << 31))


def _count(u_ref, n_groups, group, thr, cmp):
    rows = u_ref.shape[1]
    step = min(rows, LANES)
    parts = []
    for r0 in range(0, rows, step):
        t = jnp.broadcast_to(thr[r0:r0 + step], (step, LANES))

        def body(gi, acc, r0=r0, t=t):
            for i in range(group):
                acc = acc + cmp(u_ref[gi * group + i, r0:r0 + step], t).astype(F32)
            return acc

        acc = lax.fori_loop(0, n_groups, body, jnp.zeros((step, LANES), F32))
        parts.append(jnp.sum(acc, axis=-1, keepdims=True))
    return parts[0] if len(parts) == 1 else jnp.concatenate(parts, axis=0)


def _topk_mask(score_ref, u_ref, write_tile, n_tiles, n_groups, group, k):
    rows = score_ref.shape[1]
    n_live = n_groups * group

    def to_key(t, carry):
        bits = pltpu.bitcast(score_ref[t], I32)
        u_ref[t] = jnp.where(bits < 0, bits ^ np.int32(0x7FFFFFFF), bits)
        return carry

    lax.fori_loop(0, n_live, to_key, 0)

    kf = float(k)
    ge = lambda u, t: u >= t
    cnt = _count(u_ref, n_groups, group, jnp.zeros((rows, 1), I32), ge)
    lo = jnp.where(cnt >= kf, np.int32(0), _INT_MIN)

    def bit_step(i, lo):
        cand = lo | (np.int32(1) << (30 - i))
        cnt = _count(u_ref, n_groups, group, cand, ge)
        return jnp.where(cnt >= kf, cand, lo)

    lo = lax.fori_loop(0, 31, bit_step, lo)
    c_gt = _count(u_ref, n_groups, group, lo, lambda u, t: u > t)
    c_eq = _count(u_ref, n_groups, group, lo, lambda u, t: u == t)
    need = kf - c_gt
    real = lo > _KEY_NEG_INF
    excess = jnp.where(real & (c_eq > need), 1.0, 0.0)
    any_excess = jnp.max(excess) > 0.0

    @pl.when(jnp.logical_not(any_excess))
    def _():
        def emit(t, carry):
            u = u_ref[t]
            write_tile(t, jnp.where((u >= lo) & (u > _KEY_NEG_INF), 1.0, 0.0))
            return carry

        lax.fori_loop(0, n_live, emit, 0)

    @pl.when(any_excess)
    def _():
        upper = (lax.broadcasted_iota(I32, (LANES, LANES), 0)
                 <= lax.broadcasted_iota(I32, (LANES, LANES), 1)).astype(BF)

        def emit(t, seen):
            u = u_ref[t]
            eq = jnp.where(u == lo, 1.0, 0.0)
            prefix = seen + _dot(eq.astype(BF), upper)
            take = (u > lo) | ((u == lo) & (prefix <= need))
            write_tile(t, jnp.where(take & (u > _KEY_NEG_INF), 1.0, 0.0))
            return seen + jnp.sum(eq, axis=-1, keepdims=True)

        lax.fori_loop(0, n_live, emit, jnp.zeros((rows, 1), F32))

    def clear(t, carry):
        write_tile(t, jnp.zeros((rows, LANES), F32))
        return carry

    lax.fori_loop(n_live, n_tiles, clear, 0)


IDX_KCHUNK = 512
IDX_QROWS = 256


def _indexer_prompt_body(qi_ref, tailq_ref, tail_ref, mask_ref, score_ref, u_ref, *, seq):
    j = pl.program_id(1)
    rows = IDX_QROWS
    qblocks = rows // LANES
    n_tiles = seq // LANES
    tiles_per_chunk = IDX_KCHUNK // LANES
    qi = qi_ref[0].astype(BF)
    w = tailq_ref[0][:, IDX_DIM:IDX_DIM + IDX_HEADS] * (IDX_DIM ** -0.5 * IDX_HEADS ** -0.5)

    n_chunks = (j * rows + rows - 1) // IDX_KCHUNK + 1
    q_pos = j * rows + lax.broadcasted_iota(I32, (rows, IDX_KCHUNK), 0)

    def chunk_step(ci, carry):
        k0 = pl.multiple_of(ci * IDX_KCHUNK, IDX_KCHUNK)
        kic = tail_ref[0, pl.ds(k0, IDX_KCHUNK), :][:, 0:IDX_DIM].astype(BF)
        acc = jnp.zeros((rows, IDX_KCHUNK), F32)
        for h in range(IDX_HEADS):
            s = _dot_nt(qi[:, h * IDX_DIM:(h + 1) * IDX_DIM], kic)
            acc = acc + w[:, h:h + 1] * jnp.maximum(s, 0.0)
        k_pos = k0 + lax.broadcasted_iota(I32, (rows, IDX_KCHUNK), 1)
        acc = jnp.where(k_pos <= q_pos, acc, -jnp.inf)
        for i in range(tiles_per_chunk):
            score_ref[ci * tiles_per_chunk + i] = acc[:, i * LANES:(i + 1) * LANES]
        return carry

    lax.fori_loop(0, n_chunks, chunk_step, 0)

    def write_tile(t, m):
        mb = m.astype(BF)
        for qb in range(qblocks):
            mask_ref[0, qb, t] = mb[qb * LANES:(qb + 1) * LANES]

    _topk_mask(score_ref, u_ref, write_tile, n_tiles, n_chunks, tiles_per_chunk, TOPK)


def indexer_prompt(z3, k):
    bsz, seq = z3.shape[:2]
    nqb = seq // LANES
    rows = IDX_QROWS
    tail_block = TAIL_BLOCK
    assert k == TOPK and seq % rows == 0 and seq % IDX_KCHUNK == 0
    return pl.pallas_call(
        functools.partial(_indexer_prompt_body, seq=seq),
        grid=(bsz, seq // rows),
        in_specs=[pl.BlockSpec((1, rows, IDX_HEADS * IDX_DIM), lambda b, j: (b, j, 7)),
                  pl.BlockSpec((1, rows, LANES), lambda b, j: (b, j, tail_block)),
                  pl.BlockSpec((1, seq, LANES), lambda b, j: (b, 0, tail_block))],
        out_specs=pl.BlockSpec((1, rows // LANES, nqb, LANES, LANES), lambda b, j: (b, j, 0, 0, 0)),
        out_shape=jax.ShapeDtypeStruct((bsz, nqb, nqb, LANES, LANES), BF),
        scratch_shapes=[pltpu.VMEM((nqb, rows, LANES), F32),
                        pltpu.VMEM((nqb, rows, LANES), I32)],
        compiler_params=_cparams(("parallel", "arbitrary")),
        name="indexer_prompt",
    )(z3, z3, z3)


ATTN_KSTEP = 512
ATTN_TILES = ATTN_KSTEP // LANES


def _attn_prompt_body(bfar_ref, q_ref, k_ref, v_ref, mask_ref, bias_ref, o_ref, m_ref, l_ref, acc_ref):
    j = pl.program_id(1)
    scale = HEAD_DIM ** -0.5
    q = q_ref[0].astype(BF)
    m_ref[...] = jnp.full(m_ref.shape, NEG_BIG, F32)
    l_ref[...] = jnp.zeros(l_ref.shape, F32)
    acc_ref[...] = jnp.zeros(acc_ref.shape, F32)

    def process(sb, near):
        k0 = pl.multiple_of(sb * ATTN_KSTEP, ATTN_KSTEP)
        kblk = k_ref[0, pl.ds(k0, ATTN_KSTEP), :]
        vblk = v_ref[0, pl.ds(k0, ATTN_KSTEP), :]
        tiles = [sb * ATTN_TILES + i for i in range(ATTN_TILES)]
        sel = jnp.concatenate([mask_ref[0, 0, t] for t in tiles], axis=1).astype(F32) > 0.0
        if near:
            bidx = [jnp.where(j - t == 0, 0, jnp.where(j - t == 1, 1, 2)) for t in tiles]
        for h in range(HEADS):
            cols = slice(h * HEAD_DIM, (h + 1) * HEAD_DIM)
            lg = _dot_nt(q[:, cols], kblk[:, cols]) * scale
            if near:
                lg = lg + jnp.concatenate([bias_ref[bi, h] for bi in bidx], axis=1)
            else:
                lg = lg + bfar_ref[h]
            lg = jnp.where(sel, lg, NEG_BIG)
            m_old = m_ref[h]
            m_new = jnp.maximum(m_old, jnp.max(lg, axis=-1, keepdims=True))
            p = jnp.exp(lg - jnp.concatenate([m_new] * ATTN_TILES, axis=1))
            alpha = jnp.exp(m_old - m_new)
            l_ref[h] = alpha * l_ref[h] + jnp.sum(p, axis=-1, keepdims=True)
            acc_ref[:, cols] = alpha * acc_ref[:, cols] + _dot(p.astype(BF), vblk[:, cols])
            m_ref[h] = m_new

    n_far = jnp.maximum((j - 1) // ATTN_TILES, 0)

    def far_step(sb, carry):
        process(sb, False)
        return carry

    def near_step(sb, carry):
        process(sb, True)
        return carry

    lax.fori_loop(0, n_far, far_step, 0)
    lax.fori_loop(n_far, j // ATTN_TILES + 1, near_step, 0)

    for h in range(HEADS):
        cols = slice(h * HEAD_DIM, (h + 1) * HEAD_DIM)
        o_ref[0, :, cols] = acc_ref[:, cols] / l_ref[h]


def attn_prompt(z3, kv_bf, mask, bias_tiles, bias_far):
    bsz, seq = z3.shape[:2]
    nqb = seq // LANES
    width = HEADS * HEAD_DIM
    assert seq % ATTN_KSTEP == 0
    return pl.pallas_call(
        _attn_prompt_body,
        grid=(bsz, nqb),
        in_specs=[pl.BlockSpec(memory_space=pltpu.SMEM),
                  pl.BlockSpec((1, LANES, width), lambda b, j: (b, j, 4)),
                  pl.BlockSpec((1, seq, width), lambda b, j: (b, 0, 0)),
                  pl.BlockSpec((1, seq, width), lambda b, j: (b, 0, 1)),
                  pl.BlockSpec((1, 1, nqb, LANES, LANES), lambda b, j: (b, j, 0, 0, 0)),
                  pl.BlockSpec(bias_tiles.shape, lambda b, j: (0, 0, 0, 0))],
        out_specs=pl.BlockSpec((1, LANES, width), lambda b, j: (b, j, 0)),
        out_shape=jax.ShapeDtypeStruct((bsz, seq, width), F32),
        scratch_shapes=[pltpu.VMEM((HEADS, LANES, LANES), F32),
                        pltpu.VMEM((HEADS, LANES, LANES), F32),
                        pltpu.VMEM((LANES, width), F32)],
        compiler_params=_cparams(("parallel", "arbitrary")),
        name="attn_prompt",
    )(bias_far, z3, kv_bf, kv_bf, mask, bias_tiles)


QW = 256
SUB = 8


def _row_all(x8, op):
    return jnp.broadcast_to(op(x8, axis=0, keepdims=True), x8.shape)


def _count_t(u_ref, n_groups, group, thr_row, cmp):
    qw = u_ref.shape[2]
    thr = jnp.broadcast_to(thr_row, (SUB, qw))

    def body(gi, acc):
        for i in range(group):
            u = u_ref[gi * group + i].reshape(LANES // SUB, SUB, qw)
            acc = acc + jnp.sum(cmp(u, thr[None]).astype(F32), axis=0)
        return acc

    acc = lax.fori_loop(0, n_groups, body, jnp.zeros((SUB, qw), F32))
    return jnp.sum(acc, axis=0, keepdims=True)


def _topk_mask_t(score_ref, u_ref, write_tile, n_tiles, n_groups, group, k):
    qw = score_ref.shape[2]
    n_live = n_groups * group

    def to_key(t, carry):
        bits = pltpu.bitcast(score_ref[t], I32)
        u_ref[t] = jnp.where(bits < 0, bits ^ np.int32(0x7FFFFFFF), bits)
        return carry

    lax.fori_loop(0, n_live, to_key, 0)

    kf = float(k)
    ge = lambda u, t: u >= t
    cnt = _count_t(u_ref, n_groups, group, jnp.zeros((1, qw), I32), ge)
    lo = jnp.where(cnt >= kf, np.int32(0), _INT_MIN)

    def bit_step(i, lo):
        cand = lo | (np.int32(1) << (30 - i))
        cnt = _count_t(u_ref, n_groups, group, cand, ge)
        return jnp.where(cnt >= kf, cand, lo)

    lo = lax.fori_loop(0, 31, bit_step, lo)
    c_gt = _count_t(u_ref, n_groups, group, lo, lambda u, t: u > t)
    c_eq = _count_t(u_ref, n_groups, group, lo, lambda u, t: u == t)
    need = kf - c_gt
    excess = jnp.where((lo > _KEY_NEG_INF) & (c_eq > need), 1.0, 0.0)
    any_excess = jnp.max(excess) > 0.0
    lo_b = jnp.broadcast_to(lo, (LANES, qw))

    @pl.when(jnp.logical_not(any_excess))
    def _():
        def emit(t, carry):
            u = u_ref[t]
            write_tile(t, jnp.where((u >= lo_b) & (u > _KEY_NEG_INF), 1.0, 0.0))
            return carry

        lax.fori_loop(0, n_live, emit, 0)

    @pl.when(any_excess)
    def _():
        lower = (lax.broadcasted_iota(I32, (LANES, LANES), 1)
                 <= lax.broadcasted_iota(I32, (LANES, LANES), 0)).astype(BF)

        def emit(t, seen):
            u = u_ref[t]
            eq = jnp.where(u == lo_b, 1.0, 0.0)
            prefix = seen + _dot(lower, eq.astype(BF))
            take = (u > lo_b) | ((u == lo_b) & (prefix <= need))
            write_tile(t, jnp.where(take & (u > _KEY_NEG_INF), 1.0, 0.0))
            return seen + jnp.sum(eq, axis=0, keepdims=True)

        lax.fori_loop(0, n_live, emit, jnp.zeros((1, qw), F32))

    def clear(t, carry):
        write_tile(t, jnp.zeros((LANES, qw), F32))
        return carry

    lax.fori_loop(n_live, n_tiles, clear, 0)


def _indexer_t_body(qi_ref, wt_ref, tail_ref, mask_ref, score_ref, u_ref, *, seq):
    j = pl.program_id(1)
    n_tiles = seq // LANES
    tiles_per_chunk = IDX_KCHUNK // LANES
    qi = qi_ref[0].astype(BF)
    wt = wt_ref[0] * (IDX_DIM ** -0.5 * IDX_HEADS ** -0.5)

    n_chunks = (j * QW + QW - 1) // IDX_KCHUNK + 1
    q_pos = j * QW + lax.broadcasted_iota(I32, (IDX_KCHUNK, QW), 1)

    def chunk_step(ci, carry):
        k0 = pl.multiple_of(ci * IDX_KCHUNK, IDX_KCHUNK)
        kic = tail_ref[0, pl.ds(k0, IDX_KCHUNK), :][:, 0:IDX_DIM].astype(BF)
        acc = jnp.zeros((IDX_KCHUNK, QW), F32)
        for h in range(IDX_HEADS):
            s = _dot_nt(kic, qi[:, h * IDX_DIM:(h + 1) * IDX_DIM])
            acc = acc + wt[h:h + 1] * jnp.maximum(s, 0.0)
        k_pos = k0 + lax.broadcasted_iota(I32, (IDX_KCHUNK, QW), 0)
        acc = jnp.where(k_pos <= q_pos, acc, -jnp.inf)
        for i in range(tiles_per_chunk):
            score_ref[ci * tiles_per_chunk + i] = acc[i * LANES:(i + 1) * LANES]
        return carry

    lax.fori_loop(0, n_chunks, chunk_step, 0)

    def write_tile(t, m):
        mask_ref[0, 0, t] = m.astype(BF)

    _topk_mask_t(score_ref, u_ref, write_tile, n_tiles, n_chunks, tiles_per_chunk, TOPK)


def indexer_prompt_t(z3, wt):
    bsz, seq = z3.shape[:2]
    n_tiles = seq // LANES
    tail_block = TAIL_BLOCK
    assert seq % QW == 0 and seq % IDX_KCHUNK == 0
    return pl.pallas_call(
        functools.partial(_indexer_t_body, seq=seq),
        grid=(bsz, seq // QW),
        in_specs=[pl.BlockSpec((1, QW, IDX_HEADS * IDX_DIM), lambda b, j: (b, j, 7)),
                  pl.BlockSpec((1, IDX_HEADS, QW), lambda b, j: (b, 0, j)),
                  pl.BlockSpec((1, seq, LANES), lambda b, j: (b, 0, tail_block))],
        out_specs=pl.BlockSpec((1, 1, n_tiles, LANES, QW), lambda b, j: (b, j, 0, 0, 0)),
        out_shape=jax.ShapeDtypeStruct((bsz, seq // QW, n_tiles, LANES, QW), BF),
        scratch_shapes=[pltpu.VMEM((n_tiles, LANES, QW), F32),
                        pltpu.VMEM((n_tiles, LANES, QW), I32)],
        compiler_params=_cparams(("parallel", "arbitrary")),
        name="indexer_prompt_t",
    )(z3, wt, z3)


def _attn_t_body(bfar_ref, q_ref, k_ref, vt_ref, mask_ref, bias_ref, o_ref, m_ref, l_ref, acc_ref):
    j = pl.program_id(1)
    scale = HEAD_DIM ** -0.5
    qsub = QW // LANES
    q = q_ref[0].astype(BF)
    m_ref[...] = jnp.full(m_ref.shape, NEG_BIG, F32)
    l_ref[...] = jnp.zeros(l_ref.shape, F32)
    acc_ref[...] = jnp.zeros(acc_ref.shape, F32)

    def process(sb, near):
        k0 = pl.multiple_of(sb * ATTN_KSTEP, ATTN_KSTEP)
        kblk = k_ref[0, pl.ds(k0, ATTN_KSTEP), :]
        vblk = vt_ref[0, pl.ds(k0, ATTN_KSTEP), :]
        tiles = [sb * ATTN_TILES + i for i in range(ATTN_TILES)]
        sel = jnp.concatenate([mask_ref[0, 0, t] for t in tiles], axis=0).astype(F32) > 0.0
        if near:
            def tile_bias(t, h):
                row = []
                for s in range(qsub):
                    d = j * qsub + s - t
                    row.append(bias_ref[jnp.where(d == 0, 0, jnp.where(d == 1, 1, 2)), h])
                return jnp.concatenate(row, axis=1)
        head_cols = [slice(h * HEAD_DIM, (h + 1) * HEAD_DIM) for h in range(HEADS)]
        qk = lambda h: _dot_nt(kblk[:, head_cols[h]], q[:, head_cols[h]])
        lg_next = qk(0)
        for h in range(HEADS):
            cols = head_cols[h]
            lg = lg_next * scale
            if h + 1 < HEADS:
                lg_next = qk(h + 1)
            if near:
                lg = lg + jnp.concatenate([tile_bias(t, h) for t in tiles], axis=0)
            else:
                lg = lg + bfar_ref[h]
            lg = jnp.where(sel, lg, NEG_BIG)
            m_old = m_ref[h]
            part = jnp.max(lg.reshape(ATTN_KSTEP // SUB, SUB, QW), axis=0)
            m_new = jnp.maximum(m_old, _row_all(part, jnp.max))
            p = jnp.exp(lg - m_new[0:1])
            alpha = jnp.exp(m_old - m_new)
            psum = jnp.sum(p.reshape(ATTN_KSTEP // SUB, SUB, QW), axis=0)
            l_ref[h] = alpha * l_ref[h] + _row_all(psum, jnp.sum)
            pv = _dot_tn(vblk[:, cols], p.astype(BF))
            acc_ref[cols, :] = alpha[0:1] * acc_ref[cols, :] + pv
            m_ref[h] = m_new

    n_far = jnp.maximum((qsub * j - 1) // ATTN_TILES, 0)
    n_steps = (qsub * j + qsub - 1) // ATTN_TILES + 1

    def far_step(sb, carry):
        process(sb, False)
        return carry

    def near_step(sb, carry):
        process(sb, True)
        return carry

    lax.fori_loop(0, n_far, far_step, 0)
    lax.fori_loop(n_far, n_steps, near_step, 0)

    for h in range(HEADS):
        cols = slice(h * HEAD_DIM, (h + 1) * HEAD_DIM)
        o_ref[0, cols, :] = acc_ref[cols, :] / l_ref[h][0:1]


def attn_prompt_t(z3, kv_bf, mask_t, bias_tiles_t, bias_far):
    bsz, seq = z3.shape[:2]
    n_tiles = seq // LANES
    width = HEADS * HEAD_DIM
    assert seq % ATTN_KSTEP == 0 and seq % QW == 0
    return pl.pallas_call(
        _attn_t_body,
        grid=(bsz, seq // QW),
        in_specs=[pl.BlockSpec(memory_space=pltpu.SMEM),
                  pl.BlockSpec((1, QW, width), lambda b, j: (b, j, 4)),
                  pl.BlockSpec((1, seq, width), lambda b, j: (b, 0, 0)),
                  pl.BlockSpec((1, seq, width), lambda b, j: (b, 0, 1)),
                  pl.BlockSpec((1, 1, n_tiles, LANES, QW), lambda b, j: (b, j, 0, 0, 0)),
                  pl.BlockSpec(bias_tiles_t.shape, lambda b, j: (0, 0, 0, 0))],
        out_specs=pl.BlockSpec((1, width, QW), lambda b, j: (b, 0, j)),
        out_shape=jax.ShapeDtypeStruct((bsz, width, seq), F32),
        scratch_shapes=[pltpu.VMEM((HEADS, SUB, QW), F32),
                        pltpu.VMEM((HEADS, SUB, QW), F32),
                        pltpu.VMEM((width, QW), F32)],
        compiler_params=_cparams(("parallel", "arbitrary")),
        name="attn_prompt_t",
    )(bias_far, z3, kv_bf, kv_bf, mask_t, bias_tiles_t)


def _indexer_sample_body(pt_ref, qi_ref, wcol_ref, kinew_ref, *rest, n_pages):
    ki_refs = rest[:n_pages]
    out_ref = rest[n_pages]
    qi = qi_ref[0].astype(BF)
    w = wcol_ref[0] * (IDX_DIM ** -0.5 * IDX_HEADS ** -0.5)
    for i in range(n_pages):
        s = _dot(qi, ki_refs[i][0].astype(BF))
        out_ref[i, 0] = jnp.sum(w * jnp.maximum(s, 0.0), axis=0, keepdims=True)
    kn = kinew_ref[0].astype(BF).astype(F32)
    sn = jnp.sum(qi.astype(F32) * kn, axis=-1, keepdims=True)
    new = jnp.sum(w * jnp.maximum(sn, 0.0), axis=0, keepdims=True)
    lane = lax.broadcasted_iota(I32, (1, LANES), 1)
    out_ref[n_pages, 0] = jnp.where(lane == 0, new, -jnp.inf)


def indexer_sample(page_table, qi3, wcol, ki_new, ki_pool_t):
    bsz, n_pages = page_table.shape
    ki_spec = lambda i: pl.BlockSpec((1, IDX_DIM, PAGE), lambda b, pt, i=i: (pt[b, i], 0, 0))
    grid_spec = pltpu.PrefetchScalarGridSpec(
        num_scalar_prefetch=1,
        grid=(bsz,),
        in_specs=[pl.BlockSpec((1, IDX_HEADS, IDX_DIM), lambda b, pt: (b, 0, 0)),
                  pl.BlockSpec((1, IDX_HEADS, 1), lambda b, pt: (b, 0, 0)),
                  pl.BlockSpec((1, 1, IDX_DIM), lambda b, pt: (b, 0, 0))]
                 + [ki_spec(i) for i in range(n_pages)],
        out_specs=pl.BlockSpec((n_pages + 1, 1, 1, LANES), lambda b, pt: (0, b, 0, 0)),
    )
    out = pl.pallas_call(
        functools.partial(_indexer_sample_body, n_pages=n_pages),
        grid_spec=grid_spec,
        out_shape=jax.ShapeDtypeStruct((n_pages + 1, bsz, 1, LANES), F32),
        compiler_params=_cparams(("arbitrary",)),
        name="indexer_sample",
    )(page_table, qi3, wcol, ki_new, *([ki_pool_t] * n_pages))
    return out.reshape(n_pages + 1, bsz, LANES)


def _select_body(score_ref, mask_ref, u_ref, *, n_tiles, k):
    def write_tile(t, m):
        mask_ref[t] = m

    _topk_mask(score_ref, u_ref, write_tile, n_tiles, n_tiles, 1, k)


def select_topk(scores, k):
    n_tiles, rows, _ = scores.shape
    return pl.pallas_call(
        functools.partial(_select_body, n_tiles=n_tiles, k=k),
        grid=(1,),
        in_specs=[pl.BlockSpec(scores.shape, lambda i: (0, 0, 0))],
        out_specs=pl.BlockSpec(scores.shape, lambda i: (0, 0, 0)),
        out_shape=jax.ShapeDtypeStruct(scores.shape, F32),
        scratch_shapes=[pltpu.VMEM(scores.shape, I32)],
        compiler_params=_cparams(("arbitrary",)),
        name="select_topk",
    )(scores)


SC_CORES = 2
SC_SUBCORES = 16
SC_GATHER_CHUNK = 32


def sc_gather_rows(tables, idx):
    n_rows = idx.shape[0]
    workers = SC_CORES * SC_SUBCORES
    per_worker = n_rows // workers
    chunk = SC_GATHER_CHUNK
    assert n_rows % workers == 0 and per_worker % chunk == 0 and chunk % 8 == 0
    row_shape = tables[0].shape[1:]
    n_tab = len(tables)
    mesh = plsc.VectorSubcoreMesh(core_axis_name="c", subcore_axis_name="s",
                                  num_cores=SC_CORES, num_subcores=SC_SUBCORES)

    def body(*refs):
        tab_refs = refs[:n_tab]
        idx_hbm = refs[n_tab]
        out_refs = refs[n_tab + 1:2 * n_tab + 1]
        idx_v, rows_v, sem = refs[2 * n_tab + 1:]
        wid = lax.axis_index("s") * SC_CORES + lax.axis_index("c")

        @pl.loop(0, per_worker // chunk)
        def _(ci):
            off = pl.multiple_of(wid * per_worker + ci * chunk, 8)
            pltpu.sync_copy(idx_hbm.at[pl.ds(off, chunk)], idx_v)
            for tab, out in zip(tab_refs, out_refs):
                pltpu.async_copy(tab.at[idx_v], rows_v, sem).wait()
                pltpu.sync_copy(rows_v, out.at[pl.ds(off, chunk)])

    return pl.kernel(
        body,
        out_type=[jax.ShapeDtypeStruct((n_rows, *row_shape), t.dtype) for t in tables],
        mesh=mesh,
        scratch_types=[pltpu.VMEM((chunk,), I32),
                       pltpu.VMEM((chunk, *row_shape), tables[0].dtype),
                       pltpu.SemaphoreType.DMA],
        compiler_params=pltpu.CompilerParams(use_tc_tiling_on_sc=True),
        name="sc_gather_rows",
    )(*tables, idx)


SAMPLE_SEQ_GROUP = 4


def _attn_compact_body(dist_ref, near_ref, q_ref, kn_ref, vn_ref, btab_ref, kc_ref, vc_ref,
                       o_ref, bbuf):
    step = pl.program_id(0)
    scale = HEAD_DIM ** -0.5
    ones = jnp.ones((HEAD_DIM, LANES), BF)
    far_bias = btab_ref[REL_MAX_DIST]
    last = lax.broadcasted_iota(I32, (TOPK, 1, 1), 0) == TOPK - 1
    for g in range(SAMPLE_SEQ_GROUP):
        b = step * SAMPLE_SEQ_GROUP + g
        bbuf[g] = jnp.broadcast_to(far_bias[None], (TOPK, HEADS, LANES))

        def fill(t, carry, g=g, b=b):
            bbuf[g, t] = btab_ref[dist_ref[b, t]]
            return carry

        lax.fori_loop(near_ref[b], TOPK, fill, 0)

        is_new = dist_ref[b, TOPK - 1] == 0
        rows = slice(g * TOPK, (g + 1) * TOPK)
        rnd = lambda a: a.astype(BF).astype(F32)
        kc = rnd(jnp.where(last & is_new, kn_ref[g][None], kc_ref[rows]))
        vc = rnd(jnp.where(last & is_new, vn_ref[g][None], vc_ref[rows]))
        prod = (kc * rnd(q_ref[g])[None]).reshape(TOPK * HEADS, HEAD_DIM)
        hi = prod.astype(BF)
        lo = (prod - hi.astype(F32)).astype(BF)
        lg = (_dot(hi, ones) + _dot(lo, ones)).reshape(TOPK, HEADS, LANES) * scale + bbuf[g]
        m = jnp.max(lg, axis=0, keepdims=True)
        p = jnp.exp(lg - m)
        l = jnp.sum(p, axis=0)
        o_ref[g] = jnp.sum(rnd(p) * vc, axis=0) / l


def attn_sample_compact(dist, first_near, zs3, k_sel, v_sel, bias_by_dist):
    bsz = dist.shape[0]
    g = SAMPLE_SEQ_GROUP
    assert bsz % g == 0
    zspec = lambda cb: pl.BlockSpec((g, HEADS, HEAD_DIM), lambda i, d, n, cb=cb: (i, cb, 0))
    sel_spec = pl.BlockSpec((g * TOPK, HEADS, HEAD_DIM), lambda i, d, n: (i, 0, 0))
    grid_spec = pltpu.PrefetchScalarGridSpec(
        num_scalar_prefetch=2,
        grid=(bsz // g,),
        in_specs=[zspec(4), zspec(5), zspec(6),
                  pl.BlockSpec(bias_by_dist.shape, lambda i, d, n: (0, 0, 0)),
                  sel_spec, sel_spec],
        out_specs=pl.BlockSpec((g, HEADS, HEAD_DIM), lambda i, d, n: (i, 0, 0)),
        scratch_shapes=[pltpu.VMEM((g, TOPK, HEADS, HEAD_DIM), F32)],
    )
    return pl.pallas_call(
        _attn_compact_body,
        grid_spec=grid_spec,
        out_shape=jax.ShapeDtypeStruct((bsz, HEADS, HEAD_DIM), F32),
        compiler_params=_cparams(("arbitrary",)),
        name="attn_sample_compact",
    )(dist, first_near, zs3, zs3, zs3, bias_by_dist, k_sel, v_sel)


def _bucket_table(max_dist):
    exact = REL_BUCKETS // 2
    d = np.arange(max_dist + 1)
    df = np.maximum(d, 1).astype(np.float32)
    far = exact + (np.log(df / exact) / np.float32(math.log(REL_MAX_DIST / exact))
                   * (REL_BUCKETS - exact)).astype(np.int32)
    return np.where(d < exact, d, np.minimum(far, REL_BUCKETS - 1)).astype(np.int32)


def _bias_tables(rel_bias):
    tab = _bucket_table(2 * LANES)
    assert np.all(tab[REL_MAX_DIST:] == REL_BUCKETS - 1)
    i = np.arange(LANES)
    dist0 = np.maximum(i[:, None] - i[None, :], 0)
    dist1 = LANES + i[:, None] - i[None, :]
    far = np.full((LANES, LANES), REL_BUCKETS - 1)
    idx = np.stack([tab[dist0], tab[dist1], far])
    buckets = jnp.arange(REL_BUCKETS)
    lookup = lambda ix, spec: jnp.einsum(spec, (jnp.asarray(ix)[..., None] == buckets).astype(F32),
                                         rel_bias.astype(F32), precision=lax.Precision.HIGHEST)
    tiles = lookup(idx, "tijb,bh->thij")
    by_dist = jnp.broadcast_to(lookup(tab[:REL_MAX_DIST + 1], "db,bh->dh")[:, :, None],
                               (REL_MAX_DIST + 1, HEADS, LANES))
    return tiles.astype(F32), rel_bias[REL_BUCKETS - 1].astype(F32), by_dist.astype(F32)


def _conv_prompt_body(bg_ref, cg_ref, xt_ref, w_ref, v_ref, st_ref, carry_ref, *, tblock):
    t = pl.program_id(1)

    @pl.when(t == 0)
    def _():
        carry_ref[...] = jnp.zeros_like(carry_ref)

    u = cg_ref[0] * xt_ref[0]
    row = lax.broadcasted_iota(I32, u.shape, 0)
    c0 = carry_ref[0:1]
    c1 = carry_ref[1:2]
    u1 = jnp.where(row == 0, c1, pltpu.roll(u, 1, axis=0))
    u2 = jnp.where(row == 0, c0, jnp.where(row == 1, c1, pltpu.roll(u, 2, axis=0)))
    conv = w_ref[0:1] * u2 + w_ref[1:2] * u1 + w_ref[2:3] * u
    v_ref[0] = bg_ref[0] * conv
    last = u[tblock - 2:tblock]
    carry_ref[0:2] = last

    @pl.when(t == pl.num_programs(1) - 1)
    def _():
        st_ref[0] = last


def conv_prompt(zc3, w_conv, tblock=256):
    bsz, seq = zc3.shape[:2]
    c = zc3.shape[2] // 3
    zspec = lambda cb: pl.BlockSpec((1, tblock, c), lambda b, t, cb=cb: (b, t, cb))
    return pl.pallas_call(
        functools.partial(_conv_prompt_body, tblock=tblock),
        grid=(bsz, seq // tblock),
        in_specs=[zspec(0), zspec(1), zspec(2), pl.BlockSpec((3, c), lambda b, t: (0, 0))],
        out_specs=[pl.BlockSpec((1, tblock, c), lambda b, t: (b, t, 0)),
                   pl.BlockSpec((1, 2, c), lambda b, t: (b, 0, 0))],
        out_shape=[jax.ShapeDtypeStruct((bsz, seq, c), F32),
                   jax.ShapeDtypeStruct((bsz, 2, c), F32)],
        scratch_shapes=[pltpu.VMEM((8, c), F32)],
        compiler_params=_cparams(("parallel", "arbitrary")),
        name="conv_prompt",
    )(zc3, zc3, zc3, w_conv)


def _conv_sample_body(bg_ref, cg_ref, xt_ref, w_ref, s0_ref, s1_ref, v_ref, n0_ref, n1_ref):
    u = cg_ref[...] * xt_ref[...]
    conv = w_ref[0:1] * s0_ref[...] + w_ref[1:2] * s1_ref[...] + w_ref[2:3] * u
    v_ref[...] = bg_ref[...] * conv
    n0_ref[...] = s1_ref[...]
    n1_ref[...] = u


def conv_sample(zc, w_conv, s0, s1):
    bsz = zc.shape[0]
    c = zc.shape[1] // 3
    zspec = lambda cb: pl.BlockSpec((bsz, c), lambda i, cb=cb: (0, cb))
    full = pl.BlockSpec((bsz, c), lambda i: (0, 0))
    return pl.pallas_call(
        _conv_sample_body,
        grid=(1,),
        in_specs=[zspec(0), zspec(1), zspec(2), pl.BlockSpec((3, c), lambda i: (0, 0)), full, full],
        out_specs=[full, full, full],
        out_shape=[jax.ShapeDtypeStruct((bsz, c), F32)] * 3,
        compiler_params=_cparams(("arbitrary",)),
        name="conv_sample",
    )(zc, zc, zc, w_conv, s0, s1)


def _router_body(x_ref, g_ref, wr_ref, br_ref, h_ref, route_ref):
    x = x_ref[...]
    ms = jnp.mean(x * x, axis=-1, keepdims=True)
    h = x * lax.rsqrt(ms + RMS_EPS) * g_ref[...]
    hb = h.astype(BF)
    half = h.shape[1] // 2
    bits = pltpu.bitcast(hb.astype(F32), jnp.uint32)
    h_ref[...] = (bits[:, :half] & jnp.uint32(0xFFFF0000)) | (bits[:, half:] >> 16)
    logits = _dot(hb, wr_ref[...]) + br_ref[...]
    lane = lax.broadcasted_iota(I32, logits.shape, 1)
    big = np.int32(1 << 20)
    lg = jnp.where(lane < N_GROUPS, logits, -jnp.inf)
    g_max = jnp.max(lg, axis=-1, keepdims=True)
    g_idx = jnp.min(jnp.where(lg == g_max, lane, big), axis=-1, keepdims=True)
    g_w = 1.0 / jnp.sum(jnp.exp(lg - g_max), axis=-1, keepdims=True)
    first = N_GROUPS + EXP_PER_GROUP * g_idx
    le = jnp.where((lane >= first) & (lane < first + EXP_PER_GROUP), logits, -jnp.inf)
    l1 = jnp.max(le, axis=-1, keepdims=True)
    i1 = jnp.min(jnp.where(le == l1, lane, big), axis=-1, keepdims=True)
    le2 = jnp.where(lane == i1, -jnp.inf, le)
    l2 = jnp.max(le2, axis=-1, keepdims=True)
    i2 = jnp.min(jnp.where(le2 == l2, lane, big), axis=-1, keepdims=True)
    r = jnp.exp(l2 - l1)
    w1 = g_w / (1.0 + r)
    w2 = g_w * r / (1.0 + r)
    e1 = (i1 - N_GROUPS).astype(F32)
    e2 = (i2 - N_GROUPS).astype(F32)
    route_ref[...] = jnp.where(lane == 0, e1, jnp.where(lane == 1, e2,
                               jnp.where(lane == 2, w1, jnp.where(lane == 3, w2, 0.0))))


def moe_router(x, g, wr, br, tm):
    m, k = x.shape
    return pl.pallas_call(
        _router_body,
        grid=(m // tm,),
        in_specs=[pl.BlockSpec((tm, k), lambda i: (i, 0)),
                  pl.BlockSpec((1, k), lambda i: (0, 0)),
                  pl.BlockSpec((k, LANES), lambda i: (0, 0)),
                  pl.BlockSpec((1, LANES), lambda i: (0, 0))],
        out_specs=[pl.BlockSpec((tm, k // 2), lambda i: (i, 0)),
                   pl.BlockSpec((tm, LANES), lambda i: (i, 0))],
        out_shape=[jax.ShapeDtypeStruct((m, k // 2), jnp.uint32),
                   jax.ShapeDtypeStruct((m, LANES), F32)],
        compiler_params=_cparams(("parallel",)),
        name="moe_router",
    )(x, g.reshape(1, k), wr, br)


def _experts_body(te_ref, nu_ref, h_ref, wg_ref, wu_ref, wd_ref, o_ref, wg_bf, wu_bf, wd_bf):
    i = pl.program_id(0)

    @pl.when((i == 0) | (te_ref[i] != te_ref[jnp.maximum(i - 1, 0)]))
    def _():
        wg_bf[...] = wg_ref[0].astype(BF)
        wu_bf[...] = wu_ref[0].astype(BF)
        wd_bf[...] = wd_ref[0].astype(BF)

    @pl.when(i < nu_ref[0])
    def _():
        words = h_ref[...]
        left = pltpu.bitcast(words & jnp.uint32(0xFFFF0000), F32)
        right = pltpu.bitcast(words << 16, F32)
        h = jnp.concatenate([left, right], axis=1).astype(BF)
        a = _dot(h, wg_bf[...])
        b = _dot(h, wu_bf[...])
        hid = a * _sigmoid(a) * b
        o_ref[...] = _dot(hid.astype(BF), wd_bf[...])

    @pl.when(i >= nu_ref[0])
    def _():
        o_ref[...] = jnp.zeros_like(o_ref)


def moe_experts(tile_expert, n_used, hs, wg, wu, wd):
    p = hs.shape[0]
    k = wg.shape[1]
    f = wg.shape[2]
    n_tiles = p // MOE_TILE
    grid_spec = pltpu.PrefetchScalarGridSpec(
        num_scalar_prefetch=2,
        grid=(n_tiles,),
        in_specs=[pl.BlockSpec((MOE_TILE, k // 2), lambda i, te, nu: (i, 0)),
                  pl.BlockSpec((1, k, f), lambda i, te, nu: (te[i], 0, 0)),
                  pl.BlockSpec((1, k, f), lambda i, te, nu: (te[i], 0, 0)),
                  pl.BlockSpec((1, f, k), lambda i, te, nu: (te[i], 0, 0))],
        out_specs=pl.BlockSpec((MOE_TILE, k), lambda i, te, nu: (i, 0)),
        scratch_shapes=[pltpu.VMEM((k, f), BF), pltpu.VMEM((k, f), BF), pltpu.VMEM((f, k), BF)],
    )
    return pl.pallas_call(
        _experts_body,
        grid_spec=grid_spec,
        out_shape=jax.ShapeDtypeStruct((p, k), F32),
        compiler_params=_cparams(("arbitrary",)),
        name="moe_experts",
    )(tile_expert, n_used, hs, wg, wu, wd)


def _rank_within_expert(onehot):
    n, e = onehot.shape
    blk = LANES
    assert n % blk == 0
    oh = onehot.astype(F32).reshape(n // blk, blk, e)
    strict = jnp.asarray(np.tril(np.ones((blk, blk), np.float32), -1))
    within = jnp.einsum("ij,bjk->bik", strict, oh, precision=lax.Precision.HIGHEST)
    totals = jnp.sum(oh, axis=1)
    before = jnp.cumsum(totals, axis=0) - totals
    rank = (within + before[:, None, :]).reshape(n, e)
    return jnp.sum(rank * onehot.astype(F32), axis=1).astype(I32), jnp.sum(totals, axis=0).astype(I32)


def hier_moe(xs, tms, g, wrg, brg, wre, bre, wg, wu, wd, layer):
    d = xs[0].shape[1]
    m = sum(x.shape[0] for x in xs)
    wr = jnp.zeros((d, LANES), F32)
    wr = wr.at[:, :N_GROUPS].set(wrg).at[:, N_GROUPS:N_GROUPS + N_EXPERTS].set(wre.reshape(d, N_EXPERTS))
    br = jnp.zeros((1, LANES), F32)
    br = br.at[0, :N_GROUPS].set(brg).at[0, N_GROUPS:N_GROUPS + N_EXPERTS].set(bre.reshape(N_EXPERTS))
    routed = [moe_router(x, g, wr.astype(BF), br, tm) for x, tm in zip(xs, tms)]
    h_bf = jnp.concatenate([r[0] for r in routed], axis=0)
    route = jnp.concatenate([r[1] for r in routed], axis=0)

    eid = route[:, 0:2].astype(I32).reshape(-1)
    onehot = eid[:, None] == jnp.arange(N_EXPERTS, dtype=I32)[None, :]
    rank, counts = _rank_within_expert(onehot)
    padded = ((counts + MOE_TILE - 1) // MOE_TILE) * MOE_TILE
    ends = jnp.cumsum(padded)
    pos = jnp.sum(jnp.where(onehot, (ends - padded)[None, :], 0), axis=1) + rank
    n_rows = 2 * m + N_EXPERTS * MOE_TILE
    n_rows = -(-n_rows // MOE_TILE) * MOE_TILE
    token = jnp.zeros((n_rows,), I32).at[pos].set(jnp.arange(2 * m, dtype=I32) // 2)
    tile_start = jnp.arange(n_rows // MOE_TILE, dtype=I32) * MOE_TILE
    tile_expert = jnp.minimum(jnp.sum(tile_start[:, None] >= ends[None, :], axis=1),
                              N_EXPERTS - 1).astype(I32)
    n_used = (ends[-1] // MOE_TILE).astype(I32).reshape(1)

    hs = jnp.take(h_bf, token, axis=0, mode="clip")
    out = moe_experts(tile_expert + layer * N_EXPERTS, n_used, hs, wg, wu, wd)
    pos2 = pos.reshape(m, 2)
    res, r0 = [], 0
    for x in xs:
        rows = slice(r0, r0 + x.shape[0])
        res.append(x + route[rows, 2:3] * jnp.take(out, pos2[rows, 0], axis=0, mode="clip")
                   + route[rows, 3:4] * jnp.take(out, pos2[rows, 1], axis=0, mode="clip"))
        r0 += x.shape[0]
    return res


def kernel(x_prompt, x_sample, cache_k, cache_v, cache_ki, state_hgrn, state_conv, page_table,
           norm_mix_g, norm_ffn_g, final_g, w_in_even, w_out_even, hgrn_lb_logits, hgrn_norm_g,
           rel_bias, w_in_conv, w_conv, w_out_conv, w_router_g, b_router_g, w_router_e,
           b_router_e, w_gate, w_up, w_down):
    bsz, seq, d = x_prompt.shape
    dec = x_sample.shape[0]
    n_p = bsz * seq
    tm_p, tm_s = 512, 128
    tms = (tm_p, tm_s)
    assert n_p % tm_p == 0 and dec % tm_s == 0 and x_sample.shape[1] == 1
    width = HEADS * HEAD_DIM
    even_in = w_in_even.shape[2]
    even_pad = -(-even_in // IN_PROJ_TN) * IN_PROJ_TN
    tail0 = TAIL_BLOCK * LANES
    n_pool = cache_k.shape[1]

    xp = x_prompt.reshape(n_p, d)
    xs = x_sample.reshape(dec, d)

    lbs = jnp.cumsum(jax.nn.softmax(hgrn_lb_logits.astype(F32), axis=0), axis=0)[:-1]
    bias_tiles, bias_far, bias_by_dist = _bias_tables(rel_bias)
    expert_w = [w.reshape(-1, *w.shape[2:]) for w in (w_gate, w_up, w_down)]

    w_in = jnp.pad(w_in_even[0], ((0, 0), (0, even_pad - even_in))).astype(BF)
    zp = norm_matmul(xp, norm_mix_g[0], w_in, 2 * tm_p, IN_PROJ_TN)
    zs = norm_matmul(xs, norm_mix_g[0], w_in, tm_s, IN_PROJ_TN)
    zp3 = zp.reshape(bsz, seq, even_pad)
    zs3 = zs.reshape(dec, even_pad // LANES, LANES)

    oa_p, hgrn_p = hgrn_prompt(zp3, lbs[0], hgrn_norm_g[0])
    oa_s, hgrn_s = hgrn_sample(zs3, lbs[0], hgrn_norm_g[0], state_hgrn[0])

    wt = jnp.swapaxes(zp3[:, :, tail0 + IDX_DIM:tail0 + IDX_DIM + IDX_HEADS], 1, 2)
    mask_p = indexer_prompt_t(zp3, wt)
    kv_bf = zp3[:, :, 5 * width:7 * width].astype(BF)
    ob_p = jnp.swapaxes(attn_prompt_t(zp3, kv_bf, mask_p,
                                      jnp.swapaxes(bias_tiles, -1, -2), bias_far), 1, 2)

    qi3 = zs[:, 7 * width:8 * width].reshape(dec, IDX_HEADS, IDX_DIM)
    ki_new = zs[:, tail0:tail0 + IDX_DIM].reshape(dec, 1, IDX_DIM)
    wcol = zs[:, tail0 + IDX_DIM:tail0 + IDX_DIM + IDX_HEADS].reshape(dec, IDX_HEADS, 1)
    pages = page_table + 0 * n_pool
    scores_s = indexer_sample(pages, qi3, wcol, ki_new,
                              jnp.swapaxes(cache_ki, -1, -2).reshape(-1, IDX_DIM, PAGE))
    mask_s = select_topk(scores_s, TOPK)
    sel = jnp.transpose(mask_s, (1, 0, 2))
    n_pages = page_table.shape[1]
    sel_off = jnp.concatenate([jnp.zeros((dec, 1), F32),
                               jnp.cumsum(jnp.sum(sel, axis=2), axis=1)], axis=1)
    slot = jnp.arange(TOPK, dtype=F32)
    page_of = jnp.sum(sel_off[:, None, 1:] <= slot[None, :, None], axis=2)
    page_1h = (page_of[:, :, None] == jnp.arange(n_pages + 1)[None, None, :]).astype(F32)
    local = slot[None, :] - jnp.einsum("brp,bp->br", page_1h, sel_off[:, :-1],
                                       precision=lax.Precision.HIGHEST)
    within = jnp.einsum("brp,bpl->brl", page_1h, jnp.cumsum(sel, axis=2),
                        precision=lax.Precision.HIGHEST)
    lane_of = jnp.sum(within <= local[:, :, None], axis=2)
    sel_idx = (page_of * PAGE + lane_of).astype(I32)
    past = n_pages * PAGE
    page_id = jnp.einsum("brp,bp->br", page_1h[:, :, :n_pages], pages.astype(F32),
                         precision=lax.Precision.HIGHEST).astype(I32)
    rows = jnp.where(page_of < n_pages, page_id * PAGE + lane_of, 0).astype(I32).reshape(-1)
    k_sel, v_sel = sc_gather_rows([cache_k.reshape(-1, HEADS, HEAD_DIM),
                                   cache_v.reshape(-1, HEADS, HEAD_DIM)], rows)
    dist = jnp.minimum(past - sel_idx, REL_MAX_DIST).astype(I32)
    first_near = jnp.sum(dist >= REL_MAX_DIST, axis=1).astype(I32)
    ob_s = attn_sample_compact(dist, first_near, zs3, k_sel, v_sel, bias_by_dist)

    w_out = w_out_even[0].astype(BF)
    w_out_ab = [w_out[:width], w_out[width:]]
    xp = matmul_residual([oa_p.reshape(n_p, width), ob_p.reshape(n_p, width)], w_out_ab, xp, 2 * tm_p, 512)
    xs = matmul_residual([oa_s.reshape(dec, width), ob_s.reshape(dec, width)], w_out_ab, xs, tm_s, 512)

    xp, xs = hier_moe([xp, xs], tms, norm_ffn_g[0], w_router_g[0], b_router_g[0], w_router_e[0],
                      b_router_e[0], *expert_w, 0)

    w_in_c = w_in_conv[0].astype(BF)
    zcp = norm_matmul(xp, norm_mix_g[1], w_in_c, 2 * tm_p, 1024)
    zcs = norm_matmul(xs, norm_mix_g[1], w_in_c, tm_s, 512)
    cw = zcp.shape[1] // 3
    v_p, conv_p = conv_prompt(zcp.reshape(bsz, seq, 3 * cw), w_conv[0])
    v_s, cs0, cs1 = conv_sample(zcs, w_conv[0], state_conv[0, :, 0], state_conv[0, :, 1])
    w_out_c = [w_out_conv[0].astype(BF)]
    xp = matmul_residual([v_p.reshape(n_p, cw)], w_out_c, xp, 2 * tm_p, 512)
    xs = matmul_residual([v_s], w_out_c, xs, tm_s, 512)

    xp, xs = hier_moe([xp, xs], tms, norm_ffn_g[1], w_router_g[1], b_router_g[1], w_router_e[1],
                      b_router_e[1], *expert_w, 1)

    yp = rmsnorm(xp, final_g, tm_p)
    ys = rmsnorm(xs, final_g, tm_s)

    kcol, vcol = 5 * width, 6 * width
    heads = lambda a, n: a.reshape(1, *n, HEADS, HEAD_DIM)
    return (yp.reshape(bsz, seq, d),
            ys.reshape(dec, 1, d),
            heads(zp[:, kcol:kcol + width], (bsz, seq)),
            heads(zp[:, vcol:vcol + width], (bsz, seq)),
            zp[:, tail0:tail0 + IDX_DIM].reshape(1, bsz, seq, IDX_DIM),
            hgrn_p[None],
            conv_p[None],
            heads(zs[:, kcol:kcol + width], (dec, 1)),
            heads(zs[:, vcol:vcol + width], (dec, 1)),
            zs[:, tail0:tail0 + IDX_DIM].reshape(1, dec, 1, IDX_DIM),
            hgrn_s[None],
            jnp.stack([cs0, cs1], axis=1)[None])
```

```python
import functools
import math

import numpy as np
import jax
import jax.numpy as jnp
from jax import lax
from jax.experimental import pallas as pl
from jax.experimental.pallas import tpu as pltpu
from jax.experimental.pallas import tpu_sc as plsc

F32 = jnp.float32
BF = jnp.bfloat16
I32 = jnp.int32

RMS_EPS = 1e-6
LANES = 128
NEG_BIG = -1e30
VMEM_LIMIT = 56 * 1024 * 1024

D_MODEL = 2048
HEADS = 8
HEAD_DIM = 128
IDX_HEADS = 16
IDX_DIM = 64
TOPK = 256
REL_BUCKETS = 32
REL_MAX_DIST = 128
N_EXPERTS = 16
EXP_PER_GROUP = 4
N_GROUPS = 4
D_EXPERT = 512
PAGE = 128

HGRN_CHUNK = 128
HGRN_HEAD_GROUP = 8
MOE_TILE = 256
IN_PROJ_TN = 640
TAIL_BLOCK = 8 * HEADS * HEAD_DIM // LANES


def _cparams(sem):
    return pltpu.CompilerParams(dimension_semantics=sem, vmem_limit_bytes=VMEM_LIMIT)


def _dot(a, b):
    return jnp.dot(a, b, preferred_element_type=F32)


def _dot_nt(a, b):
    return lax.dot_general(a, b, (((1,), (1,)), ((), ())), preferred_element_type=F32)


def _dot_tn(a, b):
    return lax.dot_general(a, b, (((0,), (0,)), ((), ())), preferred_element_type=F32)


def _sigmoid(x):
    return 1.0 / (1.0 + jnp.exp(-x))


def _norm_mm_body(x_ref, g_ref, w_ref, o_ref, h_ref):
    @pl.when(pl.program_id(1) == 0)
    def _():
        x = x_ref[...]
        ms = jnp.mean(x * x, axis=-1, keepdims=True)
        h_ref[...] = (x * lax.rsqrt(ms + RMS_EPS) * g_ref[...]).astype(BF)

    o_ref[...] = _dot(h_ref[...], w_ref[...])


def norm_matmul(x, g, w_bf, tm, tn):
    m, k = x.shape
    n = w_bf.shape[1]
    return pl.pallas_call(
        _norm_mm_body,
        grid=(m // tm, n // tn),
        in_specs=[pl.BlockSpec((tm, k), lambda i, j: (i, 0)),
                  pl.BlockSpec((1, k), lambda i, j: (0, 0)),
                  pl.BlockSpec((k, tn), lambda i, j: (0, j))],
        out_specs=pl.BlockSpec((tm, tn), lambda i, j: (i, j)),
        out_shape=jax.ShapeDtypeStruct((m, n), F32),
        scratch_shapes=[pltpu.VMEM((tm, k), BF)],
        compiler_params=_cparams(("parallel", "arbitrary")),
        name="norm_matmul",
    )(x, g.reshape(1, k), w_bf)


EVEN_TN = 512


def _even_proj_body(x_ref, g_ref, w_ref, z_ref, k_ref, v_ref, kv_ref, h_ref, *, per):
    j = pl.program_id(1)

    @pl.when(j == 0)
    def _():
        x = x_ref[...]
        ms = jnp.mean(x * x, axis=-1, keepdims=True)
        h_ref[...] = (x * lax.rsqrt(ms + RMS_EPS) * g_ref[...]).astype(BF)

    res = _dot(h_ref[...], w_ref[...])
    sec = j // per

    @pl.when((sec < 5) | (sec == 7))
    def _():
        z_ref[...] = res

    @pl.when(sec == 5)
    def _():
        k_ref[...] = res
        kv_ref[...] = res.astype(BF)

    @pl.when(sec == 6)
    def _():
        v_ref[...] = res
        kv_ref[...] = res.astype(BF)


def even_projection(x, g, w_bf, tm):
    m, k = x.shape
    width = HEADS * HEAD_DIM
    per = width // EVEN_TN
    clip = lambda a, lo, hi: jnp.minimum(jnp.maximum(a, lo), hi)
    return pl.pallas_call(
        functools.partial(_even_proj_body, per=per),
        grid=(m // tm, 8 * per),
        in_specs=[pl.BlockSpec((tm, k), lambda i, j: (i, 0)),
                  pl.BlockSpec((1, k), lambda i, j: (0, 0)),
                  pl.BlockSpec((k, EVEN_TN), lambda i, j: (0, j))],
        out_specs=[pl.BlockSpec((tm, EVEN_TN), lambda i, j: (i, j - clip(j - (5 * per - 1), 0, 2 * per))),
                   pl.BlockSpec((tm, EVEN_TN), lambda i, j: (i, clip(j - 5 * per, 0, per - 1))),
                   pl.BlockSpec((tm, EVEN_TN), lambda i, j: (i, clip(j - 6 * per, 0, per - 1))),
                   pl.BlockSpec((tm, EVEN_TN), lambda i, j: (i, clip(j - 5 * per, 0, 2 * per - 1)))],
        out_shape=[jax.ShapeDtypeStruct((m, 6 * width), F32),
                   jax.ShapeDtypeStruct((m, width), F32),
                   jax.ShapeDtypeStruct((m, width), F32),
                   jax.ShapeDtypeStruct((m, 2 * width), BF)],
        scratch_shapes=[pltpu.VMEM((tm, k), BF)],
        compiler_params=_cparams(("parallel", "arbitrary")),
        name="even_projection",
    )(x, g.reshape(1, k), w_bf)


def _mm_res_body(*refs, n_lhs):
    a_refs = refs[:n_lhs]
    w_refs = refs[n_lhs:2 * n_lhs]
    r_ref = refs[2 * n_lhs]
    o_ref = refs[2 * n_lhs + 1]
    s_refs = refs[2 * n_lhs + 2:]

    @pl.when(pl.program_id(1) == 0)
    def _():
        for a_ref, s_ref in zip(a_refs, s_refs):
            s_ref[...] = a_ref[...].astype(BF)

    acc = r_ref[...]
    for s_ref, w_ref in zip(s_refs, w_refs):
        acc = acc + _dot(s_ref[...], w_ref[...])
    o_ref[...] = acc


def matmul_residual(lhs, ws_bf, res, tm, tn):
    m, n = res.shape
    n_lhs = len(lhs)
    in_specs = ([pl.BlockSpec((tm, a.shape[1]), lambda i, j: (i, 0)) for a in lhs]
                + [pl.BlockSpec((w.shape[0], tn), lambda i, j: (0, j)) for w in ws_bf]
                + [pl.BlockSpec((tm, tn), lambda i, j: (i, j))])
    return pl.pallas_call(
        functools.partial(_mm_res_body, n_lhs=n_lhs),
        grid=(m // tm, n // tn),
        in_specs=in_specs,
        out_specs=pl.BlockSpec((tm, tn), lambda i, j: (i, j)),
        out_shape=jax.ShapeDtypeStruct((m, n), F32),
        scratch_shapes=[pltpu.VMEM((tm, a.shape[1]), BF) for a in lhs],
        compiler_params=_cparams(("parallel", "arbitrary")),
        name="matmul_residual",
    )(*lhs, *ws_bf, res)


def _rmsnorm_body(x_ref, g_ref, o_ref):
    x = x_ref[...]
    ms = jnp.mean(x * x, axis=-1, keepdims=True)
    o_ref[...] = x * lax.rsqrt(ms + RMS_EPS) * g_ref[...]


def rmsnorm(x, g, tm):
    m, k = x.shape
    return pl.pallas_call(
        _rmsnorm_body,
        grid=(m // tm,),
        in_specs=[pl.BlockSpec((tm, k), lambda i: (i, 0)),
                  pl.BlockSpec((1, k), lambda i: (0, 0))],
        out_specs=pl.BlockSpec((tm, k), lambda i: (i, 0)),
        out_shape=jax.ShapeDtypeStruct((m, k), F32),
        compiler_params=_cparams(("parallel",)),
        name="rmsnorm",
    )(x, g.reshape(1, k))


def _hgrn_static(c):
    levels = []
    m = 1
    while m < c:
        levels.append(m)
        m *= 2
    t = np.arange(c)
    rows = [t[None, :] <= t[:, None]]
    masks = [np.eye(c, dtype=bool)]
    for m in levels:
        blk = t // (2 * m)
        pos = t % (2 * m)
        bnd = blk * 2 * m + m - 1
        right = pos >= m
        left = pos < m
        e_rows = (t[None, :] > bnd[:, None]) & (t[None, :] <= t[:, None]) & right[:, None]
        f_rows = (t[None, :] > t[:, None]) & (t[None, :] <= bnd[:, None]) & left[:, None]
        rows.append(e_rows | f_rows)
        masks.append((blk[:, None] == blk[None, :]) & right[:, None] & left[None, :])
    m_all = np.stack(rows).astype(np.float32)
    masks = np.stack(masks).astype(np.float32)
    return m_all, masks, len(levels)


def _hgrn_gates(qa, fa, lb):
    f = lb + (1.0 - lb) * _sigmoid(fa)
    q = qa * _sigmoid(qa)
    return q, f


def _hgrn_prompt_body(qa_ref, fa_ref, ia_ref, ga_ref, lb_ref, gn_ref, mall_ref, masks_ref,
                      oa_ref, st_out_ref, st_ref, *, chunk, tblock, n_levels):
    c = chunk
    t_idx = pl.program_id(1)

    @pl.when(t_idx == 0)
    def _():
        st_ref[...] = jnp.zeros_like(st_ref)

    gn = gn_ref[...]

    def chunk_step(ci, carry):
        r0 = pl.multiple_of(ci * c, c)
        for h0 in range(0, HEADS, HGRN_HEAD_GROUP):
            hs = range(h0, h0 + HGRN_HEAD_GROUP)
            cols = {h: slice(h * HEAD_DIM, (h + 1) * HEAD_DIM) for h in hs}
            q, k, g_hi, g_lo, v = {}, {}, {}, {}, {}
            for h in hs:
                q[h], f = _hgrn_gates(qa_ref[0, pl.ds(r0, c), cols[h]],
                                      fa_ref[0, pl.ds(r0, c), cols[h]], lb_ref[:, cols[h]])
                k[h] = 1.0 - f
                g = jnp.log(f)
                g_hi[h] = g.astype(BF)
                g_lo[h] = (g - g_hi[h].astype(F32)).astype(BF)
                v[h] = ia_ref[0, pl.ds(r0, c), cols[h]].astype(BF)
            b = {h: _dot(mall_ref[0], g_hi[h]) + _dot(mall_ref[0], g_lo[h]) for h in hs}
            o = {h: _dot_nt((q[h] * jnp.exp(b[h])).astype(BF), st_ref[h].astype(BF)) for h in hs}
            a = {h: masks_ref[0] * _dot_nt(q[h].astype(BF), k[h].astype(BF)) for h in hs}
            for li in range(n_levels):
                ml = mall_ref[1 + li]
                w = {h: jnp.exp(_dot(ml, g_hi[h]) + _dot(ml, g_lo[h])) for h in hs}
                for h in hs:
                    a[h] = a[h] + masks_ref[1 + li] * _dot_nt((q[h] * w[h]).astype(BF),
                                                              (k[h] * w[h]).astype(BF))
            for h in hs:
                o[h] = o[h] + _dot(a[h].astype(BF), v[h])
            for h in hs:
                b_last = b[h][c - 1:c]
                k_st = (k[h] * jnp.exp(b_last - b[h])).astype(BF)
                st_ref[h] = st_ref[h] * jnp.exp(b_last) + _dot_tn(v[h], k_st)
            for h in hs:
                ga = ga_ref[0, pl.ds(r0, c), cols[h]]
                ms = jnp.mean(o[h] * o[h], axis=-1, keepdims=True)
                oa_ref[0, pl.ds(r0, c), cols[h]] = (o[h] * lax.rsqrt(ms + RMS_EPS) * gn
                                                    * (ga * _sigmoid(ga)))
        return carry

    lax.fori_loop(0, tblock // c, chunk_step, 0)

    @pl.when(t_idx == pl.num_programs(1) - 1)
    def _():
        st_out_ref[0] = st_ref[...]


def hgrn_prompt(z3, lb, gn, tblock=256, chunk=HGRN_CHUNK):
    bsz, seq = z3.shape[:2]
    width = HEADS * HEAD_DIM
    m_all, masks, n_levels = _hgrn_static(chunk)
    zspec = lambda cb: pl.BlockSpec((1, tblock, width), lambda b, t, cb=cb: (b, t, cb))
    oa, st = pl.pallas_call(
        functools.partial(_hgrn_prompt_body, chunk=chunk, tblock=tblock, n_levels=n_levels),
        grid=(bsz, seq // tblock),
        in_specs=[zspec(0), zspec(1), zspec(2), zspec(3),
                  pl.BlockSpec((1, width), lambda b, t: (0, 0)),
                  pl.BlockSpec((1, HEAD_DIM), lambda b, t: (0, 0)),
                  pl.BlockSpec(m_all.shape, lambda b, t: (0, 0, 0)),
                  pl.BlockSpec(masks.shape, lambda b, t: (0, 0, 0))],
        out_specs=[pl.BlockSpec((1, tblock, width), lambda b, t: (b, t, 0)),
                   pl.BlockSpec((1, HEADS, HEAD_DIM, HEAD_DIM), lambda b, t: (b, 0, 0, 0))],
        out_shape=[jax.ShapeDtypeStruct((bsz, seq, width), F32),
                   jax.ShapeDtypeStruct((bsz, HEADS, HEAD_DIM, HEAD_DIM), F32)],
        scratch_shapes=[pltpu.VMEM((HEADS, HEAD_DIM, HEAD_DIM), F32)],
        compiler_params=_cparams(("parallel", "arbitrary")),
        name="hgrn_prompt",
    )(z3, z3, z3, z3, lb.reshape(1, width), gn.reshape(1, HEAD_DIM),
      jnp.asarray(m_all, BF), jnp.asarray(masks, F32))
    return oa, jnp.swapaxes(st, -1, -2)


def _col(row, eye):
    return jnp.sum(eye * row, axis=1, keepdims=True)


def _hgrn_sample_body(qa_ref, fa_ref, ia_ref, ga_ref, lb_ref, gn_ref, s_ref, oa_ref, so_ref):
    eye = (lax.broadcasted_iota(I32, (HEAD_DIM, HEAD_DIM), 0)
           == lax.broadcasted_iota(I32, (HEAD_DIM, HEAD_DIM), 1)).astype(F32)
    rnd = lambda a: a.astype(BF).astype(F32)
    for g in range(SAMPLE_SEQ_GROUP):
        q8, f8 = _hgrn_gates(qa_ref[g], fa_ref[g], lb_ref[...])
        ga = ga_ref[g]
        gate = ga * _sigmoid(ga)
        kr = rnd(1.0 - f8)
        vr = rnd(ia_ref[g])
        qfr = rnd(q8 * f8)
        qk = rnd(jnp.sum(rnd(q8) * kr, axis=-1, keepdims=True))
        outs = []
        for h in range(HEADS):
            f_col = _col(f8[h:h + 1], eye)
            s_old = s_ref[g, h]
            so_ref[g, h] = f_col * s_old + _col(kr[h:h + 1], eye) * vr[h:h + 1]
            outs.append(qk[h:h + 1] * vr[h:h + 1]
                        + jnp.sum(_col(qfr[h:h + 1], eye) * rnd(s_old), axis=0, keepdims=True))
        o = jnp.concatenate(outs, axis=0)
        ms = jnp.mean(o * o, axis=-1, keepdims=True)
        oa_ref[g] = o * lax.rsqrt(ms + RMS_EPS) * gn_ref[...] * gate


def hgrn_sample(zs3, lb, gn, s0):
    bsz = zs3.shape[0]
    g = SAMPLE_SEQ_GROUP
    assert bsz % g == 0
    zspec = lambda cb: pl.BlockSpec((g, HEADS, HEAD_DIM), lambda b, cb=cb: (b, cb, 0))
    sspec = pl.BlockSpec((g, HEADS, HEAD_DIM, HEAD_DIM), lambda b: (b, 0, 0, 0))
    return pl.pallas_call(
        _hgrn_sample_body,
        grid=(bsz // g,),
        in_specs=[zspec(0), zspec(1), zspec(2), zspec(3),
                  pl.BlockSpec((HEADS, HEAD_DIM), lambda b: (0, 0)),
                  pl.BlockSpec((1, HEAD_DIM), lambda b: (0, 0)),
                  sspec],
        out_specs=[pl.BlockSpec((g, HEADS, HEAD_DIM), lambda b: (b, 0, 0)), sspec],
        out_shape=[jax.ShapeDtypeStruct((bsz, HEADS, HEAD_DIM), F32),
                   jax.ShapeDtypeStruct(s0.shape, F32)],
        compiler_params=_cparams(("parallel",)),
        name="hgrn_sample",
    )(zs3, zs3, zs3, zs3, lb.reshape(HEADS, HEAD_DIM), gn.reshape(1, HEAD_DIM), s0)


_KEY_NEG_INF = np.int32(np.uint32(0x807FFFFF).astype(np.int64) - (1 << 32))
_INT_MIN = np.int32(-(1 << 31))


def _count(u_ref, n_groups, group, thr, cmp):
    rows = u_ref.shape[1]
    step = min(rows, LANES)
    parts = []
    for r0 in range(0, rows, step):
        t = jnp.broadcast_to(thr[r0:r0 + step], (step, LANES))

        def body(gi, acc, r0=r0, t=t):
            for i in range(group):
                acc = acc + cmp(u_ref[gi * group + i, r0:r0 + step], t).astype(F32)
            return acc

        acc = lax.fori_loop(0, n_groups, body, jnp.zeros((step, LANES), F32))
        parts.append(jnp.sum(acc, axis=-1, keepdims=True))
    return parts[0] if len(parts) == 1 else jnp.concatenate(parts, axis=0)


def _topk_mask(score_ref, u_ref, write_tile, n_tiles, n_groups, group, k):
    rows = score_ref.shape[1]
    n_live = n_groups * group

    def to_key(t, carry):
        bits = pltpu.bitcast(score_ref[t], I32)
        u_ref[t] = jnp.where(bits < 0, bits ^ np.int32(0x7FFFFFFF), bits)
        return carry

    lax.fori_loop(0, n_live, to_key, 0)

    kf = float(k)
    ge = lambda u, t: u >= t
    cnt = _count(u_ref, n_groups, group, jnp.zeros((rows, 1), I32), ge)
    lo = jnp.where(cnt >= kf, np.int32(0), _INT_MIN)

    def bit_step(i, lo):
        cand = lo | (np.int32(1) << (30 - i))
        cnt = _count(u_ref, n_groups, group, cand, ge)
        return jnp.where(cnt >= kf, cand, lo)

    lo = lax.fori_loop(0, 31, bit_step, lo)
    c_gt = _count(u_ref, n_groups, group, lo, lambda u, t: u > t)
    c_eq = _count(u_ref, n_groups, group, lo, lambda u, t: u == t)
    need = kf - c_gt
    real = lo > _KEY_NEG_INF
    excess = jnp.where(real & (c_eq > need), 1.0, 0.0)
    any_excess = jnp.max(excess) > 0.0

    @pl.when(jnp.logical_not(any_excess))
    def _():
        def emit(t, carry):
            u = u_ref[t]
            write_tile(t, jnp.where((u >= lo) & (u > _KEY_NEG_INF), 1.0, 0.0))
            return carry

        lax.fori_loop(0, n_live, emit, 0)

    @pl.when(any_excess)
    def _():
        upper = (lax.broadcasted_iota(I32, (LANES, LANES), 0)
                 <= lax.broadcasted_iota(I32, (LANES, LANES), 1)).astype(BF)

        def emit(t, seen):
            u = u_ref[t]
            eq = jnp.where(u == lo, 1.0, 0.0)
            prefix = seen + _dot(eq.astype(BF), upper)
            take = (u > lo) | ((u == lo) & (prefix <= need))
            write_tile(t, jnp.where(take & (u > _KEY_NEG_INF), 1.0, 0.0))
            return seen + jnp.sum(eq, axis=-1, keepdims=True)

        lax.fori_loop(0, n_live, emit, jnp.zeros((rows, 1), F32))

    def clear(t, carry):
        write_tile(t, jnp.zeros((rows, LANES), F32))
        return carry

    lax.fori_loop(n_live, n_tiles, clear, 0)


QW = 256
SUB = 8
IDX_KCHUNK = 512
ATTN_KSTEP = 512
ATTN_TILES = ATTN_KSTEP // LANES


def _row_all(x8, op):
    return jnp.broadcast_to(op(x8, axis=0, keepdims=True), x8.shape)


def _count_t(u_ref, n_groups, group, thr_row, cmp):
    qw = u_ref.shape[2]
    thr = jnp.broadcast_to(thr_row, (SUB, qw))

    def body(gi, acc):
        for i in range(group):
            u = u_ref[gi * group + i].reshape(LANES // SUB, SUB, qw)
            acc = acc + jnp.sum(cmp(u, thr[None]).astype(F32), axis=0)
        return acc

    acc = lax.fori_loop(0, n_groups, body, jnp.zeros((SUB, qw), F32))
    return jnp.sum(acc, axis=0, keepdims=True)


def _topk_mask_t(score_ref, u_ref, write_tile, n_tiles, n_groups, group, k):
    qw = score_ref.shape[2]
    n_live = n_groups * group

    def to_key(t, carry):
        bits = pltpu.bitcast(score_ref[t], I32)
        u_ref[t] = jnp.where(bits < 0, bits ^ np.int32(0x7FFFFFFF), bits)
        return carry

    lax.fori_loop(0, n_live, to_key, 0)

    kf = float(k)
    ge = lambda u, t: u >= t
    cnt = _count_t(u_ref, n_groups, group, jnp.zeros((1, qw), I32), ge)
    lo = jnp.where(cnt >= kf, np.int32(0), _INT_MIN)

    def bit_step(i, lo):
        cand = lo | (np.int32(1) << (30 - i))
        cnt = _count_t(u_ref, n_groups, group, cand, ge)
        return jnp.where(cnt >= kf, cand, lo)

    lo = lax.fori_loop(0, 31, bit_step, lo)
    c_gt = _count_t(u_ref, n_groups, group, lo, lambda u, t: u > t)
    c_eq = _count_t(u_ref, n_groups, group, lo, lambda u, t: u == t)
    need = kf - c_gt
    excess = jnp.where((lo > _KEY_NEG_INF) & (c_eq > need), 1.0, 0.0)
    any_excess = jnp.max(excess) > 0.0
    lo_b = jnp.broadcast_to(lo, (LANES, qw))

    @pl.when(jnp.logical_not(any_excess))
    def _():
        def emit(t, carry):
            u = u_ref[t]
            write_tile(t, jnp.where((u >= lo_b) & (u > _KEY_NEG_INF), 1.0, 0.0))
            return carry

        lax.fori_loop(0, n_live, emit, 0)

    @pl.when(any_excess)
    def _():
        lower = (lax.broadcasted_iota(I32, (LANES, LANES), 1)
                 <= lax.broadcasted_iota(I32, (LANES, LANES), 0)).astype(BF)

        def emit(t, seen):
            u = u_ref[t]
            eq = jnp.where(u == lo_b, 1.0, 0.0)
            prefix = seen + _dot(lower, eq.astype(BF))
            take = (u > lo_b) | ((u == lo_b) & (prefix <= need))
            write_tile(t, jnp.where(take & (u > _KEY_NEG_INF), 1.0, 0.0))
            return seen + jnp.sum(eq, axis=0, keepdims=True)

        lax.fori_loop(0, n_live, emit, jnp.zeros((1, qw), F32))

    def clear(t, carry):
        write_tile(t, jnp.zeros((LANES, qw), F32))
        return carry

    lax.fori_loop(n_live, n_tiles, clear, 0)


def _indexer_t_body(qi_ref, wt_ref, tail_ref, mask_ref, score_ref, u_ref, *, seq):
    j = pl.program_id(1)
    n_tiles = seq // LANES
    tiles_per_chunk = IDX_KCHUNK // LANES
    qi = qi_ref[0].astype(BF)
    wt = wt_ref[0] * (IDX_DIM ** -0.5 * IDX_HEADS ** -0.5)

    n_chunks = (j * QW + QW - 1) // IDX_KCHUNK + 1
    q_pos = j * QW + lax.broadcasted_iota(I32, (IDX_KCHUNK, QW), 1)

    def chunk_step(ci, carry):
        k0 = pl.multiple_of(ci * IDX_KCHUNK, IDX_KCHUNK)
        kic = tail_ref[0, pl.ds(k0, IDX_KCHUNK), :][:, 0:IDX_DIM].astype(BF)
        acc = jnp.zeros((IDX_KCHUNK, QW), F32)
        for h in range(IDX_HEADS):
            s = _dot_nt(kic, qi[:, h * IDX_DIM:(h + 1) * IDX_DIM])
            acc = acc + wt[h:h + 1] * jnp.maximum(s, 0.0)
        k_pos = k0 + lax.broadcasted_iota(I32, (IDX_KCHUNK, QW), 0)
        acc = jnp.where(k_pos <= q_pos, acc, -jnp.inf)
        for i in range(tiles_per_chunk):
            score_ref[ci * tiles_per_chunk + i] = acc[i * LANES:(i + 1) * LANES]
        return carry

    lax.fori_loop(0, n_chunks, chunk_step, 0)

    def write_tile(t, m):
        mask_ref[0, 0, t] = m.astype(BF)

    _topk_mask_t(score_ref, u_ref, write_tile, n_tiles, n_chunks, tiles_per_chunk, TOPK)


def indexer_prompt_t(z3, wt, tail3):
    bsz, seq = z3.shape[:2]
    n_tiles = seq // LANES
    assert seq % QW == 0 and seq % IDX_KCHUNK == 0
    return pl.pallas_call(
        functools.partial(_indexer_t_body, seq=seq),
        grid=(bsz, seq // QW),
        in_specs=[pl.BlockSpec((1, QW, IDX_HEADS * IDX_DIM), lambda b, j: (b, j, 5)),
                  pl.BlockSpec((1, IDX_HEADS, QW), lambda b, j: (b, 0, j)),
                  pl.BlockSpec((1, seq, LANES), lambda b, j: (b, 0, 0))],
        out_specs=pl.BlockSpec((1, 1, n_tiles, LANES, QW), lambda b, j: (b, j, 0, 0, 0)),
        out_shape=jax.ShapeDtypeStruct((bsz, seq // QW, n_tiles, LANES, QW), BF),
        scratch_shapes=[pltpu.VMEM((n_tiles, LANES, QW), F32),
                        pltpu.VMEM((n_tiles, LANES, QW), I32)],
        compiler_params=_cparams(("parallel", "arbitrary")),
        name="indexer_prompt_t",
    )(z3, wt, tail3)


def _attn_t_body(bfar_ref, q_ref, k_ref, vt_ref, mask_ref, bias_ref, o_ref, m_ref, l_ref, acc_ref):
    j = pl.program_id(1)
    scale = HEAD_DIM ** -0.5
    qsub = QW // LANES
    q = q_ref[0].astype(BF)
    m_ref[...] = jnp.full(m_ref.shape, NEG_BIG, F32)
    l_ref[...] = jnp.zeros(l_ref.shape, F32)
    acc_ref[...] = jnp.zeros(acc_ref.shape, F32)

    def process(sb, near):
        k0 = pl.multiple_of(sb * ATTN_KSTEP, ATTN_KSTEP)
        kblk = k_ref[0, pl.ds(k0, ATTN_KSTEP), :]
        vblk = vt_ref[0, pl.ds(k0, ATTN_KSTEP), :]
        tiles = [sb * ATTN_TILES + i for i in range(ATTN_TILES)]
        sel = jnp.concatenate([mask_ref[0, 0, t] for t in tiles], axis=0).astype(F32) > 0.0
        if near:
            def tile_bias(t, h):
                row = []
                for s in range(qsub):
                    d = j * qsub + s - t
                    row.append(bias_ref[jnp.where(d == 0, 0, jnp.where(d == 1, 1, 2)), h])
                return jnp.concatenate(row, axis=1)
        head_cols = [slice(h * HEAD_DIM, (h + 1) * HEAD_DIM) for h in range(HEADS)]
        qk = lambda h: _dot_nt(kblk[:, head_cols[h]], q[:, head_cols[h]])
        lg_next = qk(0)
        for h in range(HEADS):
            cols = head_cols[h]
            lg = lg_next * scale
            if h + 1 < HEADS:
                lg_next = qk(h + 1)
            if near:
                lg = lg + jnp.concatenate([tile_bias(t, h) for t in tiles], axis=0)
            else:
                lg = lg + bfar_ref[h]
            lg = jnp.where(sel, lg, NEG_BIG)
            m_old = m_ref[h]
            part = jnp.max(lg.reshape(ATTN_KSTEP // SUB, SUB, QW), axis=0)
            m_new = jnp.maximum(m_old, _row_all(part, jnp.max))
            p = jnp.exp(lg - m_new[0:1])
            alpha = jnp.exp(m_old - m_new)
            psum = jnp.sum(p.reshape(ATTN_KSTEP // SUB, SUB, QW), axis=0)
            l_ref[h] = alpha * l_ref[h] + _row_all(psum, jnp.sum)
            pv = _dot_tn(vblk[:, cols], p.astype(BF))
            acc_ref[cols, :] = alpha[0:1] * acc_ref[cols, :] + pv
            m_ref[h] = m_new

    n_far = jnp.maximum((qsub * j - 1) // ATTN_TILES, 0)
    n_steps = (qsub * j + qsub - 1) // ATTN_TILES + 1

    def far_step(sb, carry):
        process(sb, False)
        return carry

    def near_step(sb, carry):
        process(sb, True)
        return carry

    lax.fori_loop(0, n_far, far_step, 0)
    lax.fori_loop(n_far, n_steps, near_step, 0)

    for h in range(HEADS):
        cols = slice(h * HEAD_DIM, (h + 1) * HEAD_DIM)
        o_ref[0, cols, :] = acc_ref[cols, :] / l_ref[h][0:1]


def attn_prompt_t(z3, kv_bf, mask_t, bias_tiles_t, bias_far):
    bsz, seq = z3.shape[:2]
    n_tiles = seq // LANES
    width = HEADS * HEAD_DIM
    assert seq % ATTN_KSTEP == 0 and seq % QW == 0
    return pl.pallas_call(
        _attn_t_body,
        grid=(bsz, seq // QW),
        in_specs=[pl.BlockSpec(memory_space=pltpu.SMEM),
                  pl.BlockSpec((1, QW, width), lambda b, j: (b, j, 4)),
                  pl.BlockSpec((1, seq, width), lambda b, j: (b, 0, 0)),
                  pl.BlockSpec((1, seq, width), lambda b, j: (b, 0, 1)),
                  pl.BlockSpec((1, 1, n_tiles, LANES, QW), lambda b, j: (b, j, 0, 0, 0)),
                  pl.BlockSpec(bias_tiles_t.shape, lambda b, j: (0, 0, 0, 0))],
        out_specs=pl.BlockSpec((1, width, QW), lambda b, j: (b, 0, j)),
        out_shape=jax.ShapeDtypeStruct((bsz, width, seq), F32),
        scratch_shapes=[pltpu.VMEM((HEADS, SUB, QW), F32),
                        pltpu.VMEM((HEADS, SUB, QW), F32),
                        pltpu.VMEM((width, QW), F32)],
        compiler_params=_cparams(("parallel", "arbitrary")),
        name="attn_prompt_t",
    )(bias_far, z3, kv_bf, kv_bf, mask_t, bias_tiles_t)


def _indexer_sample_body(pt_ref, qi_ref, wcol_ref, kinew_ref, *rest, n_pages):
    ki_refs = rest[:n_pages]
    out_ref = rest[n_pages]
    qi = qi_ref[0].astype(BF)
    w = wcol_ref[0] * (IDX_DIM ** -0.5 * IDX_HEADS ** -0.5)
    for i in range(n_pages):
        s = _dot(qi, ki_refs[i][0].astype(BF))
        out_ref[i, 0] = jnp.sum(w * jnp.maximum(s, 0.0), axis=0, keepdims=True)
    kn = kinew_ref[0].astype(BF).astype(F32)
    sn = jnp.sum(qi.astype(F32) * kn, axis=-1, keepdims=True)
    new = jnp.sum(w * jnp.maximum(sn, 0.0), axis=0, keepdims=True)
    lane = lax.broadcasted_iota(I32, (1, LANES), 1)
    out_ref[n_pages, 0] = jnp.where(lane == 0, new, -jnp.inf)


def indexer_sample(page_table, qi3, wcol, ki_new, ki_pool_t):
    bsz, n_pages = page_table.shape
    ki_spec = lambda i: pl.BlockSpec((1, IDX_DIM, PAGE), lambda b, pt, i=i: (pt[b, i], 0, 0))
    grid_spec = pltpu.PrefetchScalarGridSpec(
        num_scalar_prefetch=1,
        grid=(bsz,),
        in_specs=[pl.BlockSpec((1, IDX_HEADS, IDX_DIM), lambda b, pt: (b, 0, 0)),
                  pl.BlockSpec((1, IDX_HEADS, 1), lambda b, pt: (b, 0, 0)),
                  pl.BlockSpec((1, 1, IDX_DIM), lambda b, pt: (b, 0, 0))]
                 + [ki_spec(i) for i in range(n_pages)],
        out_specs=pl.BlockSpec((n_pages + 1, 1, 1, LANES), lambda b, pt: (0, b, 0, 0)),
    )
    out = pl.pallas_call(
        functools.partial(_indexer_sample_body, n_pages=n_pages),
        grid_spec=grid_spec,
        out_shape=jax.ShapeDtypeStruct((n_pages + 1, bsz, 1, LANES), F32),
        compiler_params=_cparams(("arbitrary",)),
        name="indexer_sample",
    )(page_table, qi3, wcol, ki_new, *([ki_pool_t] * n_pages))
    return out.reshape(n_pages + 1, bsz, LANES)


def _select_body(score_ref, mask_ref, u_ref, *, n_tiles, k):
    def write_tile(t, m):
        mask_ref[t] = m

    _topk_mask(score_ref, u_ref, write_tile, n_tiles, n_tiles, 1, k)


def select_topk(scores, k):
    n_tiles, rows, _ = scores.shape
    return pl.pallas_call(
        functools.partial(_select_body, n_tiles=n_tiles, k=k),
        grid=(1,),
        in_specs=[pl.BlockSpec(scores.shape, lambda i: (0, 0, 0))],
        out_specs=pl.BlockSpec(scores.shape, lambda i: (0, 0, 0)),
        out_shape=jax.ShapeDtypeStruct(scores.shape, F32),
        scratch_shapes=[pltpu.VMEM(scores.shape, I32)],
        compiler_params=_cparams(("arbitrary",)),
        name="select_topk",
    )(scores)


SC_CORES = 2
SC_SUBCORES = 16
SC_GATHER_CHUNK = 32


def sc_gather_rows(tables, idx):
    n_rows = idx.shape[0]
    workers = SC_CORES * SC_SUBCORES
    per_worker = n_rows // workers
    chunk = SC_GATHER_CHUNK
    assert n_rows % workers == 0 and per_worker % chunk == 0 and chunk % 8 == 0
    row_shape = tables[0].shape[1:]
    n_tab = len(tables)
    mesh = plsc.VectorSubcoreMesh(core_axis_name="c", subcore_axis_name="s",
                                  num_cores=SC_CORES, num_subcores=SC_SUBCORES)

    def body(*refs):
        tab_refs = refs[:n_tab]
        idx_hbm = refs[n_tab]
        out_refs = refs[n_tab + 1:2 * n_tab + 1]
        idx_v, rows_v, sem = refs[2 * n_tab + 1:]
        wid = lax.axis_index("s") * SC_CORES + lax.axis_index("c")

        @pl.loop(0, per_worker // chunk)
        def _(ci):
            off = pl.multiple_of(wid * per_worker + ci * chunk, 8)
            pltpu.sync_copy(idx_hbm.at[pl.ds(off, chunk)], idx_v)
            for tab, out in zip(tab_refs, out_refs):
                pltpu.async_copy(tab.at[idx_v], rows_v, sem).wait()
                pltpu.sync_copy(rows_v, out.at[pl.ds(off, chunk)])

    return pl.kernel(
        body,
        out_type=[jax.ShapeDtypeStruct((n_rows, *row_shape), t.dtype) for t in tables],
        mesh=mesh,
        scratch_types=[pltpu.VMEM((chunk,), I32),
                       pltpu.VMEM((chunk, *row_shape), tables[0].dtype),
                       pltpu.SemaphoreType.DMA],
        compiler_params=pltpu.CompilerParams(use_tc_tiling_on_sc=True),
        name="sc_gather_rows",
    )(*tables, idx)


SAMPLE_SEQ_GROUP = 4


def _attn_compact_body(dist_ref, near_ref, q_ref, kn_ref, vn_ref, btab_ref, kc_ref, vc_ref,
                       o_ref, bbuf):
    step = pl.program_id(0)
    scale = HEAD_DIM ** -0.5
    ones = jnp.ones((HEAD_DIM, LANES), BF)
    far_bias = btab_ref[REL_MAX_DIST]
    last = lax.broadcasted_iota(I32, (TOPK, 1, 1), 0) == TOPK - 1
    for g in range(SAMPLE_SEQ_GROUP):
        b = step * SAMPLE_SEQ_GROUP + g
        bbuf[g] = jnp.broadcast_to(far_bias[None], (TOPK, HEADS, LANES))

        def fill(t, carry, g=g, b=b):
            bbuf[g, t] = btab_ref[dist_ref[b, t]]
            return carry

        lax.fori_loop(near_ref[b], TOPK, fill, 0)

        is_new = dist_ref[b, TOPK - 1] == 0
        rows = slice(g * TOPK, (g + 1) * TOPK)
        rnd = lambda a: a.astype(BF).astype(F32)
        kc = rnd(jnp.where(last & is_new, kn_ref[g][None], kc_ref[rows]))
        vc = rnd(jnp.where(last & is_new, vn_ref[g][None], vc_ref[rows]))
        prod = (kc * rnd(q_ref[g])[None]).reshape(TOPK * HEADS, HEAD_DIM)
        hi = prod.astype(BF)
        lo = (prod - hi.astype(F32)).astype(BF)
        lg = (_dot(hi, ones) + _dot(lo, ones)).reshape(TOPK, HEADS, LANES) * scale + bbuf[g]
        m = jnp.max(lg, axis=0, keepdims=True)
        p = jnp.exp(lg - m)
        l = jnp.sum(p, axis=0)
        o_ref[g] = jnp.sum(rnd(p) * vc, axis=0) / l


def attn_sample_compact(dist, first_near, zs3, k_sel, v_sel, bias_by_dist):
    bsz = dist.shape[0]
    g = SAMPLE_SEQ_GROUP
    assert bsz % g == 0
    zspec = lambda cb: pl.BlockSpec((g, HEADS, HEAD_DIM), lambda i, d, n, cb=cb: (i, cb, 0))
    sel_spec = pl.BlockSpec((g * TOPK, HEADS, HEAD_DIM), lambda i, d, n: (i, 0, 0))
    grid_spec = pltpu.PrefetchScalarGridSpec(
        num_scalar_prefetch=2,
        grid=(bsz // g,),
        in_specs=[zspec(4), zspec(5), zspec(6),
                  pl.BlockSpec(bias_by_dist.shape, lambda i, d, n: (0, 0, 0)),
                  sel_spec, sel_spec],
        out_specs=pl.BlockSpec((g, HEADS, HEAD_DIM), lambda i, d, n: (i, 0, 0)),
        scratch_shapes=[pltpu.VMEM((g, TOPK, HEADS, HEAD_DIM), F32)],
    )
    return pl.pallas_call(
        _attn_compact_body,
        grid_spec=grid_spec,
        out_shape=jax.ShapeDtypeStruct((bsz, HEADS, HEAD_DIM), F32),
        compiler_params=_cparams(("arbitrary",)),
        name="attn_sample_compact",
    )(dist, first_near, zs3, zs3, zs3, bias_by_dist, k_sel, v_sel)


def _bucket_table(max_dist):
    exact = REL_BUCKETS // 2
    d = np.arange(max_dist + 1)
    df = np.maximum(d, 1).astype(np.float32)
    far = exact + (np.log(df / exact) / np.float32(math.log(REL_MAX_DIST / exact))
                   * (REL_BUCKETS - exact)).astype(np.int32)
    return np.where(d < exact, d, np.minimum(far, REL_BUCKETS - 1)).astype(np.int32)


def _bias_tables(rel_bias):
    tab = _bucket_table(2 * LANES)
    assert np.all(tab[REL_MAX_DIST:] == REL_BUCKETS - 1)
    i = np.arange(LANES)
    dist0 = np.maximum(i[:, None] - i[None, :], 0)
    dist1 = LANES + i[:, None] - i[None, :]
    far = np.full((LANES, LANES), REL_BUCKETS - 1)
    idx = np.stack([tab[dist0], tab[dist1], far])
    buckets = jnp.arange(REL_BUCKETS)
    lookup = lambda ix, spec: jnp.einsum(spec, (jnp.asarray(ix)[..., None] == buckets).astype(F32),
                                         rel_bias.astype(F32), precision=lax.Precision.HIGHEST)
    tiles = lookup(idx, "tijb,bh->thij")
    by_dist = jnp.broadcast_to(lookup(tab[:REL_MAX_DIST + 1], "db,bh->dh")[:, :, None],
                               (REL_MAX_DIST + 1, HEADS, LANES))
    return tiles.astype(F32), rel_bias[REL_BUCKETS - 1].astype(F32), by_dist.astype(F32)


def _conv_prompt_body(bg_ref, cg_ref, xt_ref, w_ref, v_ref, st_ref, carry_ref, *, tblock):
    t = pl.program_id(1)

    @pl.when(t == 0)
    def _():
        carry_ref[...] = jnp.zeros_like(carry_ref)

    u = cg_ref[0] * xt_ref[0]
    row = lax.broadcasted_iota(I32, u.shape, 0)
    c0 = carry_ref[0:1]
    c1 = carry_ref[1:2]
    u1 = jnp.where(row == 0, c1, pltpu.roll(u, 1, axis=0))
    u2 = jnp.where(row == 0, c0, jnp.where(row == 1, c1, pltpu.roll(u, 2, axis=0)))
    conv = w_ref[0:1] * u2 + w_ref[1:2] * u1 + w_ref[2:3] * u
    v_ref[0] = bg_ref[0] * conv
    last = u[tblock - 2:tblock]
    carry_ref[0:2] = last

    @pl.when(t == pl.num_programs(1) - 1)
    def _():
        st_ref[0] = last


def conv_prompt(zc3, w_conv, tblock=256):
    bsz, seq = zc3.shape[:2]
    c = zc3.shape[2] // 3
    zspec = lambda cb: pl.BlockSpec((1, tblock, c), lambda b, t, cb=cb: (b, t, cb))
    return pl.pallas_call(
        functools.partial(_conv_prompt_body, tblock=tblock),
        grid=(bsz, seq // tblock),
        in_specs=[zspec(0), zspec(1), zspec(2), pl.BlockSpec((3, c), lambda b, t: (0, 0))],
        out_specs=[pl.BlockSpec((1, tblock, c), lambda b, t: (b, t, 0)),
                   pl.BlockSpec((1, 2, c), lambda b, t: (b, 0, 0))],
        out_shape=[jax.ShapeDtypeStruct((bsz, seq, c), F32),
                   jax.ShapeDtypeStruct((bsz, 2, c), F32)],
        scratch_shapes=[pltpu.VMEM((8, c), F32)],
        compiler_params=_cparams(("parallel", "arbitrary")),
        name="conv_prompt",
    )(zc3, zc3, zc3, w_conv)


def _conv_sample_body(bg_ref, cg_ref, xt_ref, w_ref, s0_ref, s1_ref, v_ref, n0_ref, n1_ref):
    u = cg_ref[...] * xt_ref[...]
    conv = w_ref[0:1] * s0_ref[...] + w_ref[1:2] * s1_ref[...] + w_ref[2:3] * u
    v_ref[...] = bg_ref[...] * conv
    n0_ref[...] = s1_ref[...]
    n1_ref[...] = u


def conv_sample(zc, w_conv, s0, s1):
    bsz = zc.shape[0]
    c = zc.shape[1] // 3
    zspec = lambda cb: pl.BlockSpec((bsz, c), lambda i, cb=cb: (0, cb))
    full = pl.BlockSpec((bsz, c), lambda i: (0, 0))
    return pl.pallas_call(
        _conv_sample_body,
        grid=(1,),
        in_specs=[zspec(0), zspec(1), zspec(2), pl.BlockSpec((3, c), lambda i: (0, 0)), full, full],
        out_specs=[full, full, full],
        out_shape=[jax.ShapeDtypeStruct((bsz, c), F32)] * 3,
        compiler_params=_cparams(("arbitrary",)),
        name="conv_sample",
    )(zc, zc, zc, w_conv, s0, s1)


def _router_body(x_ref, g_ref, wr_ref, br_ref, h_ref, route_ref):
    x = x_ref[...]
    ms = jnp.mean(x * x, axis=-1, keepdims=True)
    h = x * lax.rsqrt(ms + RMS_EPS) * g_ref[...]
    hb = h.astype(BF)
    half = h.shape[1] // 2
    bits = pltpu.bitcast(hb.astype(F32), jnp.uint32)
    h_ref[...] = (bits[:, :half] & jnp.uint32(0xFFFF0000)) | (bits[:, half:] >> 16)
    logits = _dot(hb, wr_ref[...]) + br_ref[...]
    lane = lax.broadcasted_iota(I32, logits.shape, 1)
    big = np.int32(1 << 20)
    lg = jnp.where(lane < N_GROUPS, logits, -jnp.inf)
    g_max = jnp.max(lg, axis=-1, keepdims=True)
    g_idx = jnp.min(jnp.where(lg == g_max, lane, big), axis=-1, keepdims=True)
    g_w = 1.0 / jnp.sum(jnp.exp(lg - g_max), axis=-1, keepdims=True)
    first = N_GROUPS + EXP_PER_GROUP * g_idx
    le = jnp.where((lane >= first) & (lane < first + EXP_PER_GROUP), logits, -jnp.inf)
    l1 = jnp.max(le, axis=-1, keepdims=True)
    i1 = jnp.min(jnp.where(le == l1, lane, big), axis=-1, keepdims=True)
    le2 = jnp.where(lane == i1, -jnp.inf, le)
    l2 = jnp.max(le2, axis=-1, keepdims=True)
    i2 = jnp.min(jnp.where(le2 == l2, lane, big), axis=-1, keepdims=True)
    r = jnp.exp(l2 - l1)
    w1 = g_w / (1.0 + r)
    w2 = g_w * r / (1.0 + r)
    e1 = (i1 - N_GROUPS).astype(F32)
    e2 = (i2 - N_GROUPS).astype(F32)
    route_ref[...] = jnp.where(lane == 0, e1, jnp.where(lane == 1, e2,
                               jnp.where(lane == 2, w1, jnp.where(lane == 3, w2, 0.0))))


def moe_router(x, g, wr, br, tm):
    m, k = x.shape
    return pl.pallas_call(
        _router_body,
        grid=(m // tm,),
        in_specs=[pl.BlockSpec((tm, k), lambda i: (i, 0)),
                  pl.BlockSpec((1, k), lambda i: (0, 0)),
                  pl.BlockSpec((k, LANES), lambda i: (0, 0)),
                  pl.BlockSpec((1, LANES), lambda i: (0, 0))],
        out_specs=[pl.BlockSpec((tm, k // 2), lambda i: (i, 0)),
                   pl.BlockSpec((tm, LANES), lambda i: (i, 0))],
        out_shape=[jax.ShapeDtypeStruct((m, k // 2), jnp.uint32),
                   jax.ShapeDtypeStruct((m, LANES), F32)],
        compiler_params=_cparams(("parallel",)),
        name="moe_router",
    )(x, g.reshape(1, k), wr, br)


def _experts_body(te_ref, nu_ref, h_ref, wg_ref, wu_ref, wd_ref, o_ref, wg_bf, wu_bf, wd_bf):
    i = pl.program_id(0)

    @pl.when((i == 0) | (te_ref[i] != te_ref[jnp.maximum(i - 1, 0)]))
    def _():
        wg_bf[...] = wg_ref[0].astype(BF)
        wu_bf[...] = wu_ref[0].astype(BF)
        wd_bf[...] = wd_ref[0].astype(BF)

    @pl.when(i < nu_ref[0])
    def _():
        words = h_ref[...]
        left = pltpu.bitcast(words & jnp.uint32(0xFFFF0000), F32)
        right = pltpu.bitcast(words << 16, F32)
        h = jnp.concatenate([left, right], axis=1).astype(BF)
        a = _dot(h, wg_bf[...])
        b = _dot(h, wu_bf[...])
        hid = a * _sigmoid(a) * b
        o_ref[...] = _dot(hid.astype(BF), wd_bf[...])

    @pl.when(i >= nu_ref[0])
    def _():
        o_ref[...] = jnp.zeros_like(o_ref)


def moe_experts(tile_expert, n_used, hs, wg, wu, wd):
    p = hs.shape[0]
    k = wg.shape[1]
    f = wg.shape[2]
    n_tiles = p // MOE_TILE
    grid_spec = pltpu.PrefetchScalarGridSpec(
        num_scalar_prefetch=2,
        grid=(n_tiles,),
        in_specs=[pl.BlockSpec((MOE_TILE, k // 2), lambda i, te, nu: (i, 0)),
                  pl.BlockSpec((1, k, f), lambda i, te, nu: (te[i], 0, 0)),
                  pl.BlockSpec((1, k, f), lambda i, te, nu: (te[i], 0, 0)),
                  pl.BlockSpec((1, f, k), lambda i, te, nu: (te[i], 0, 0))],
        out_specs=pl.BlockSpec((MOE_TILE, k), lambda i, te, nu: (i, 0)),
        scratch_shapes=[pltpu.VMEM((k, f), BF), pltpu.VMEM((k, f), BF), pltpu.VMEM((f, k), BF)],
    )
    return pl.pallas_call(
        _experts_body,
        grid_spec=grid_spec,
        out_shape=jax.ShapeDtypeStruct((p, k), F32),
        compiler_params=_cparams(("arbitrary",)),
        name="moe_experts",
    )(tile_expert, n_used, hs, wg, wu, wd)


def _rank_within_expert(onehot):
    n, e = onehot.shape
    blk = LANES
    assert n % blk == 0
    oh = onehot.astype(F32).reshape(n // blk, blk, e)
    strict = jnp.asarray(np.tril(np.ones((blk, blk), np.float32), -1))
    within = jnp.einsum("ij,bjk->bik", strict, oh, precision=lax.Precision.HIGHEST)
    totals = jnp.sum(oh, axis=1)
    before = jnp.cumsum(totals, axis=0) - totals
    rank = (within + before[:, None, :]).reshape(n, e)
    return jnp.sum(rank * onehot.astype(F32), axis=1).astype(I32), jnp.sum(totals, axis=0).astype(I32)


def hier_moe(xs, tms, g, wrg, brg, wre, bre, wg, wu, wd, layer):
    d = xs[0].shape[1]
    m = sum(x.shape[0] for x in xs)
    wr = jnp.zeros((d, LANES), F32)
    wr = wr.at[:, :N_GROUPS].set(wrg).at[:, N_GROUPS:N_GROUPS + N_EXPERTS].set(wre.reshape(d, N_EXPERTS))
    br = jnp.zeros((1, LANES), F32)
    br = br.at[0, :N_GROUPS].set(brg).at[0, N_GROUPS:N_GROUPS + N_EXPERTS].set(bre.reshape(N_EXPERTS))
    routed = [moe_router(x, g, wr.astype(BF), br, tm) for x, tm in zip(xs, tms)]
    h_bf = jnp.concatenate([r[0] for r in routed], axis=0)
    route = jnp.concatenate([r[1] for r in routed], axis=0)

    eid = route[:, 0:2].astype(I32).reshape(-1)
    onehot = eid[:, None] == jnp.arange(N_EXPERTS, dtype=I32)[None, :]
    rank, counts = _rank_within_expert(onehot)
    padded = ((counts + MOE_TILE - 1) // MOE_TILE) * MOE_TILE
    ends = jnp.cumsum(padded)
    pos = jnp.sum(jnp.where(onehot, (ends - padded)[None, :], 0), axis=1) + rank
    n_rows = 2 * m + N_EXPERTS * MOE_TILE
    n_rows = -(-n_rows // MOE_TILE) * MOE_TILE
    token = jnp.zeros((n_rows,), I32).at[pos].set(jnp.arange(2 * m, dtype=I32) // 2)
    tile_start = jnp.arange(n_rows // MOE_TILE, dtype=I32) * MOE_TILE
    tile_expert = jnp.minimum(jnp.sum(tile_start[:, None] >= ends[None, :], axis=1),
                              N_EXPERTS - 1).astype(I32)
    n_used = (ends[-1] // MOE_TILE).astype(I32).reshape(1)

    hs = jnp.take(h_bf, token, axis=0, mode="clip")
    out = moe_experts(tile_expert + layer * N_EXPERTS, n_used, hs, wg, wu, wd)
    pos2 = pos.reshape(m, 2)
    res, r0 = [], 0
    for x in xs:
        rows = slice(r0, r0 + x.shape[0])
        res.append(x + route[rows, 2:3] * jnp.take(out, pos2[rows, 0], axis=0, mode="clip")
                   + route[rows, 3:4] * jnp.take(out, pos2[rows, 1], axis=0, mode="clip"))
        r0 += x.shape[0]
    return res


def kernel(x_prompt, x_sample, cache_k, cache_v, cache_ki, state_hgrn, state_conv, page_table,
           norm_mix_g, norm_ffn_g, final_g, w_in_even, w_out_even, hgrn_lb_logits, hgrn_norm_g,
           rel_bias, w_in_conv, w_conv, w_out_conv, w_router_g, b_router_g, w_router_e,
           b_router_e, w_gate, w_up, w_down):
    bsz, seq, d = x_prompt.shape
    dec = x_sample.shape[0]
    n_p = bsz * seq
    tm_p, tm_s = 512, 128
    tms = (tm_p, tm_s)
    assert n_p % tm_p == 0 and dec % tm_s == 0 and x_sample.shape[1] == 1
    width = HEADS * HEAD_DIM
    even_in = w_in_even.shape[2]
    even_pad = -(-even_in // IN_PROJ_TN) * IN_PROJ_TN
    tail0 = TAIL_BLOCK * LANES
    n_pool = cache_k.shape[1]

    xp = x_prompt.reshape(n_p, d)
    xs = x_sample.reshape(dec, d)

    lbs = jnp.cumsum(jax.nn.softmax(hgrn_lb_logits.astype(F32), axis=0), axis=0)[:-1]
    bias_tiles, bias_far, bias_by_dist = _bias_tables(rel_bias)
    expert_w = [w.reshape(-1, *w.shape[2:]) for w in (w_gate, w_up, w_down)]

    w_in = jnp.pad(w_in_even[0], ((0, 0), (0, even_pad - even_in))).astype(BF)
    zp, k_new_p, v_new_p, kv_bf = even_projection(xp, norm_mix_g[0], w_in, 2 * tm_p)
    zt = norm_matmul(xp, norm_mix_g[0], w_in[:, tail0:tail0 + LANES], 2 * tm_p, LANES)
    zs = norm_matmul(xs, norm_mix_g[0], w_in, tm_s, IN_PROJ_TN)
    zp3 = zp.reshape(bsz, seq, 6 * width)
    zt3 = zt.reshape(bsz, seq, LANES)
    zs3 = zs.reshape(dec, even_pad // LANES, LANES)

    oa_p, hgrn_p = hgrn_prompt(zp3, lbs[0], hgrn_norm_g[0])
    oa_s, hgrn_s = hgrn_sample(zs3, lbs[0], hgrn_norm_g[0], state_hgrn[0])

    wt = jnp.swapaxes(zt3[:, :, IDX_DIM:IDX_DIM + IDX_HEADS], 1, 2)
    mask_p = indexer_prompt_t(zp3, wt, zt3)
    ob_p = jnp.swapaxes(attn_prompt_t(zp3, kv_bf.reshape(bsz, seq, 2 * width), mask_p,
                                      jnp.swapaxes(bias_tiles, -1, -2), bias_far), 1, 2)

    qi3 = zs[:, 7 * width:8 * width].reshape(dec, IDX_HEADS, IDX_DIM)
    ki_new = zs[:, tail0:tail0 + IDX_DIM].reshape(dec, 1, IDX_DIM)
    wcol = zs[:, tail0 + IDX_DIM:tail0 + IDX_DIM + IDX_HEADS].reshape(dec, IDX_HEADS, 1)
    pages = page_table + 0 * n_pool
    scores_s = indexer_sample(pages, qi3, wcol, ki_new,
                              jnp.swapaxes(cache_ki, -1, -2).reshape(-1, IDX_DIM, PAGE))
    mask_s = select_topk(scores_s, TOPK)
    sel = jnp.transpose(mask_s, (1, 0, 2))
    n_pages = page_table.shape[1]
    sel_off = jnp.concatenate([jnp.zeros((dec, 1), F32),
                               jnp.cumsum(jnp.sum(sel, axis=2), axis=1)], axis=1)
    slot = jnp.arange(TOPK, dtype=F32)
    page_of = jnp.sum(sel_off[:, None, 1:] <= slot[None, :, None], axis=2)
    page_1h = (page_of[:, :, None] == jnp.arange(n_pages + 1)[None, None, :]).astype(F32)
    local = slot[None, :] - jnp.einsum("brp,bp->br", page_1h, sel_off[:, :-1],
                                       precision=lax.Precision.HIGHEST)
    within = jnp.einsum("brp,bpl->brl", page_1h, jnp.cumsum(sel, axis=2),
                        precision=lax.Precision.HIGHEST)
    lane_of = jnp.sum(within <= local[:, :, None], axis=2)
    sel_idx = (page_of * PAGE + lane_of).astype(I32)
    past = n_pages * PAGE
    page_id = jnp.einsum("brp,bp->br", page_1h[:, :, :n_pages], pages.astype(F32),
                         precision=lax.Precision.HIGHEST).astype(I32)
    rows = jnp.where(page_of < n_pages, page_id * PAGE + lane_of, 0).astype(I32).reshape(-1)
    k_sel, v_sel = sc_gather_rows([cache_k.reshape(-1, HEADS, HEAD_DIM),
                                   cache_v.reshape(-1, HEADS, HEAD_DIM)], rows)
    dist = jnp.minimum(past - sel_idx, REL_MAX_DIST).astype(I32)
    first_near = jnp.sum(dist >= REL_MAX_DIST, axis=1).astype(I32)
    ob_s = attn_sample_compact(dist, first_near, zs3, k_sel, v_sel, bias_by_dist)

    w_out = w_out_even[0].astype(BF)
    w_out_ab = [w_out[:width], w_out[width:]]
    xp = matmul_residual([oa_p.reshape(n_p, width), ob_p.reshape(n_p, width)], w_out_ab, xp, 2 * tm_p, 512)
    xs = matmul_residual([oa_s.reshape(dec, width), ob_s.reshape(dec, width)], w_out_ab, xs, tm_s, 512)

    xp, xs = hier_moe([xp, xs], tms, norm_ffn_g[0], w_router_g[0], b_router_g[0], w_router_e[0],
                      b_router_e[0], *expert_w, 0)

    w_in_c = w_in_conv[0].astype(BF)
    zcp = norm_matmul(xp, norm_mix_g[1], w_in_c, 2 * tm_p, 1024)
    zcs = norm_matmul(xs, norm_mix_g[1], w_in_c, tm_s, 512)
    cw = zcp.shape[1] // 3
    v_p, conv_p = conv_prompt(zcp.reshape(bsz, seq, 3 * cw), w_conv[0])
    v_s, cs0, cs1 = conv_sample(zcs, w_conv[0], state_conv[0, :, 0], state_conv[0, :, 1])
    w_out_c = [w_out_conv[0].astype(BF)]
    xp = matmul_residual([v_p.reshape(n_p, cw)], w_out_c, xp, 2 * tm_p, 512)
    xs = matmul_residual([v_s], w_out_c, xs, tm_s, 512)

    xp, xs = hier_moe([xp, xs], tms, norm_ffn_g[1], w_router_g[1], b_router_g[1], w_router_e[1],
                      b_router_e[1], *expert_w, 1)

    yp = rmsnorm(xp, final_g, tm_p)
    ys = rmsnorm(xs, final_g, tm_s)

    kcol, vcol = 5 * width, 6 * width
    heads = lambda a, n: a.reshape(1, *n, HEADS, HEAD_DIM)
    return (yp.reshape(bsz, seq, d),
            ys.reshape(dec, 1, d),
            heads(k_new_p, (bsz, seq)),
            heads(v_new_p, (bsz, seq)),
            zt[:, :IDX_DIM].reshape(1, bsz, seq, IDX_DIM),
            hgrn_p[None],
            conv_p[None],
            heads(zs[:, kcol:kcol + width], (dec, 1)),
            heads(zs[:, vcol:vcol + width], (dec, 1)),
            zs[:, tail0:tail0 + IDX_DIM].reshape(1, dec, 1, IDX_DIM),
            hgrn_s[None],
            jnp.stack([cs0, cs1], axis=1)[None])
```

```python
import functools
import math

import numpy as np
import jax
import jax.numpy as jnp
from jax import lax
from jax.experimental import pallas as pl
from jax.experimental.pallas import tpu as pltpu
from jax.experimental.pallas import tpu_sc as plsc

F32 = jnp.float32
BF = jnp.bfloat16
I32 = jnp.int32

RMS_EPS = 1e-6
LANES = 128
NEG_BIG = -1e30
VMEM_LIMIT = 56 * 1024 * 1024

D_MODEL = 2048
HEADS = 8
HEAD_DIM = 128
IDX_HEADS = 16
IDX_DIM = 64
TOPK = 256
REL_BUCKETS = 32
REL_MAX_DIST = 128
N_EXPERTS = 16
EXP_PER_GROUP = 4
N_GROUPS = 4
D_EXPERT = 512
PAGE = 128

HGRN_CHUNK = 128
HGRN_HEAD_GROUP = 8
MOE_TILE = 256
IN_PROJ_TN = 640
TAIL_BLOCK = 8 * HEADS * HEAD_DIM // LANES


def _cparams(sem):
    return pltpu.CompilerParams(dimension_semantics=sem, vmem_limit_bytes=VMEM_LIMIT)


def _dot(a, b):
    return jnp.dot(a, b, preferred_element_type=F32)


def _dot_nt(a, b):
    return lax.dot_general(a, b, (((1,), (1,)), ((), ())), preferred_element_type=F32)


def _dot_tn(a, b):
    return lax.dot_general(a, b, (((0,), (0,)), ((), ())), preferred_element_type=F32)


def _sigmoid(x):
    return 1.0 / (1.0 + jnp.exp(-x))


def _norm_mm_body(x_ref, g_ref, w_ref, o_ref, h_ref):
    @pl.when(pl.program_id(1) == 0)
    def _():
        x = x_ref[...]
        ms = jnp.mean(x * x, axis=-1, keepdims=True)
        h_ref[...] = (x * lax.rsqrt(ms + RMS_EPS) * g_ref[...]).astype(BF)

    o_ref[...] = _dot(h_ref[...], w_ref[...])


def norm_matmul(x, g, w_bf, tm, tn):
    m, k = x.shape
    n = w_bf.shape[1]
    return pl.pallas_call(
        _norm_mm_body,
        grid=(m // tm, n // tn),
        in_specs=[pl.BlockSpec((tm, k), lambda i, j: (i, 0)),
                  pl.BlockSpec((1, k), lambda i, j: (0, 0)),
                  pl.BlockSpec((k, tn), lambda i, j: (0, j))],
        out_specs=pl.BlockSpec((tm, tn), lambda i, j: (i, j)),
        out_shape=jax.ShapeDtypeStruct((m, n), F32),
        scratch_shapes=[pltpu.VMEM((tm, k), BF)],
        compiler_params=_cparams(("parallel", "arbitrary")),
        name="norm_matmul",
    )(x, g.reshape(1, k), w_bf)


EVEN_TN = 512


def _even_proj_body(x_ref, g_ref, w_ref, z_ref, k_ref, v_ref, kv_ref, h_ref, *, per):
    j = pl.program_id(1)

    @pl.when(j == 0)
    def _():
        x = x_ref[...]
        ms = jnp.mean(x * x, axis=-1, keepdims=True)
        h_ref[...] = (x * lax.rsqrt(ms + RMS_EPS) * g_ref[...]).astype(BF)

    res = _dot(h_ref[...], w_ref[...])
    sec = j // per

    @pl.when((sec < 5) | (sec == 7))
    def _():
        z_ref[...] = res

    @pl.when(sec == 5)
    def _():
        k_ref[...] = res
        kv_ref[...] = res.astype(BF)

    @pl.when(sec == 6)
    def _():
        v_ref[...] = res
        kv_ref[...] = res.astype(BF)


def even_projection(x, g, w_bf, tm):
    m, k = x.shape
    width = HEADS * HEAD_DIM
    per = width // EVEN_TN
    clip = lambda a, lo, hi: jnp.minimum(jnp.maximum(a, lo), hi)
    return pl.pallas_call(
        functools.partial(_even_proj_body, per=per),
        grid=(m // tm, 8 * per),
        in_specs=[pl.BlockSpec((tm, k), lambda i, j: (i, 0)),
                  pl.BlockSpec((1, k), lambda i, j: (0, 0)),
                  pl.BlockSpec((k, EVEN_TN), lambda i, j: (0, j))],
        out_specs=[pl.BlockSpec((tm, EVEN_TN), lambda i, j: (i, j - clip(j - (5 * per - 1), 0, 2 * per))),
                   pl.BlockSpec((tm, EVEN_TN), lambda i, j: (i, clip(j - 5 * per, 0, per - 1))),
                   pl.BlockSpec((tm, EVEN_TN), lambda i, j: (i, clip(j - 6 * per, 0, per - 1))),
                   pl.BlockSpec((tm, EVEN_TN), lambda i, j: (i, clip(j - 5 * per, 0, 2 * per - 1)))],
        out_shape=[jax.ShapeDtypeStruct((m, 6 * width), F32),
                   jax.ShapeDtypeStruct((m, width), F32),
                   jax.ShapeDtypeStruct((m, width), F32),
                   jax.ShapeDtypeStruct((m, 2 * width), BF)],
        scratch_shapes=[pltpu.VMEM((tm, k), BF)],
        compiler_params=_cparams(("parallel", "arbitrary")),
        name="even_projection",
    )(x, g.reshape(1, k), w_bf)


def _mm_res_body(*refs, n_lhs):
    a_refs = refs[:n_lhs]
    w_refs = refs[n_lhs:2 * n_lhs]
    r_ref = refs[2 * n_lhs]
    o_ref = refs[2 * n_lhs + 1]
    s_refs = refs[2 * n_lhs + 2:]

    @pl.when(pl.program_id(1) == 0)
    def _():
        for a_ref, s_ref in zip(a_refs, s_refs):
            s_ref[...] = a_ref[...].astype(BF)

    acc = r_ref[...]
    for s_ref, w_ref in zip(s_refs, w_refs):
        acc = acc + _dot(s_ref[...], w_ref[...])
    o_ref[...] = acc


def matmul_residual(lhs, ws_bf, res, tm, tn):
    m, n = res.shape
    n_lhs = len(lhs)
    in_specs = ([pl.BlockSpec((tm, a.shape[1]), lambda i, j: (i, 0)) for a in lhs]
                + [pl.BlockSpec((w.shape[0], tn), lambda i, j: (0, j)) for w in ws_bf]
                + [pl.BlockSpec((tm, tn), lambda i, j: (i, j))])
    return pl.pallas_call(
        functools.partial(_mm_res_body, n_lhs=n_lhs),
        grid=(m // tm, n // tn),
        in_specs=in_specs,
        out_specs=pl.BlockSpec((tm, tn), lambda i, j: (i, j)),
        out_shape=jax.ShapeDtypeStruct((m, n), F32),
        scratch_shapes=[pltpu.VMEM((tm, a.shape[1]), BF) for a in lhs],
        compiler_params=_cparams(("parallel", "arbitrary")),
        name="matmul_residual",
    )(*lhs, *ws_bf, res)


def _hgrn_static(c):
    levels = []
    m = 1
    while m < c:
        levels.append(m)
        m *= 2
    t = np.arange(c)
    rows = [t[None, :] <= t[:, None]]
    masks = [np.eye(c, dtype=bool)]
    for m in levels:
        blk = t // (2 * m)
        pos = t % (2 * m)
        bnd = blk * 2 * m + m - 1
        right = pos >= m
        left = pos < m
        e_rows = (t[None, :] > bnd[:, None]) & (t[None, :] <= t[:, None]) & right[:, None]
        f_rows = (t[None, :] > t[:, None]) & (t[None, :] <= bnd[:, None]) & left[:, None]
        rows.append(e_rows | f_rows)
        masks.append((blk[:, None] == blk[None, :]) & right[:, None] & left[None, :])
    m_all = np.stack(rows).astype(np.float32)
    masks = np.stack(masks).astype(np.float32)
    return m_all, masks, len(levels)


def _hgrn_gates(qa, fa, lb):
    f = lb + (1.0 - lb) * _sigmoid(fa)
    q = qa * _sigmoid(qa)
    return q, f


def _hgrn_prompt_body(qa_ref, fa_ref, ia_ref, ga_ref, lb_ref, gn_ref, mall_ref, masks_ref,
                      oa_ref, st_out_ref, st_ref, *, chunk, tblock, n_levels):
    c = chunk
    t_idx = pl.program_id(1)

    @pl.when(t_idx == 0)
    def _():
        st_ref[...] = jnp.zeros_like(st_ref)

    gn = gn_ref[...]

    def chunk_step(ci, carry):
        r0 = pl.multiple_of(ci * c, c)
        for h0 in range(0, HEADS, HGRN_HEAD_GROUP):
            hs = range(h0, h0 + HGRN_HEAD_GROUP)
            cols = {h: slice(h * HEAD_DIM, (h + 1) * HEAD_DIM) for h in hs}
            q, k, g_hi, g_lo, v = {}, {}, {}, {}, {}
            for h in hs:
                q[h], f = _hgrn_gates(qa_ref[0, pl.ds(r0, c), cols[h]],
                                      fa_ref[0, pl.ds(r0, c), cols[h]], lb_ref[:, cols[h]])
                k[h] = 1.0 - f
                g = jnp.log(f)
                g_hi[h] = g.astype(BF)
                g_lo[h] = (g - g_hi[h].astype(F32)).astype(BF)
                v[h] = ia_ref[0, pl.ds(r0, c), cols[h]].astype(BF)
            b = {h: _dot(mall_ref[0], g_hi[h]) + _dot(mall_ref[0], g_lo[h]) for h in hs}
            o = {h: _dot_nt((q[h] * jnp.exp(b[h])).astype(BF), st_ref[h].astype(BF)) for h in hs}
            a = {h: masks_ref[0] * _dot_nt(q[h].astype(BF), k[h].astype(BF)) for h in hs}
            for li in range(n_levels):
                ml = mall_ref[1 + li]
                w = {h: jnp.exp(_dot(ml, g_hi[h]) + _dot(ml, g_lo[h])) for h in hs}
                for h in hs:
                    a[h] = a[h] + masks_ref[1 + li] * _dot_nt((q[h] * w[h]).astype(BF),
                                                              (k[h] * w[h]).astype(BF))
            for h in hs:
                o[h] = o[h] + _dot(a[h].astype(BF), v[h])
            for h in hs:
                b_last = b[h][c - 1:c]
                k_st = (k[h] * jnp.exp(b_last - b[h])).astype(BF)
                st_ref[h] = st_ref[h] * jnp.exp(b_last) + _dot_tn(v[h], k_st)
            for h in hs:
                ga = ga_ref[0, pl.ds(r0, c), cols[h]]
                ms = jnp.mean(o[h] * o[h], axis=-1, keepdims=True)
                oa_ref[0, pl.ds(r0, c), cols[h]] = (o[h] * lax.rsqrt(ms + RMS_EPS) * gn
                                                    * (ga * _sigmoid(ga)))
        return carry

    lax.fori_loop(0, tblock // c, chunk_step, 0)

    @pl.when(t_idx == pl.num_programs(1) - 1)
    def _():
        st_out_ref[0] = st_ref[...]


def hgrn_prompt(z3, lb, gn, tblock=256, chunk=HGRN_CHUNK):
    bsz, seq = z3.shape[:2]
    width = HEADS * HEAD_DIM
    m_all, masks, n_levels = _hgrn_static(chunk)
    zspec = lambda cb: pl.BlockSpec((1, tblock, width), lambda b, t, cb=cb: (b, t, cb))
    oa, st = pl.pallas_call(
        functools.partial(_hgrn_prompt_body, chunk=chunk, tblock=tblock, n_levels=n_levels),
        grid=(bsz, seq // tblock),
        in_specs=[zspec(0), zspec(1), zspec(2), zspec(3),
                  pl.BlockSpec((1, width), lambda b, t: (0, 0)),
                  pl.BlockSpec((1, HEAD_DIM), lambda b, t: (0, 0)),
                  pl.BlockSpec(m_all.shape, lambda b, t: (0, 0, 0)),
                  pl.BlockSpec(masks.shape, lambda b, t: (0, 0, 0))],
        out_specs=[pl.BlockSpec((1, tblock, width), lambda b, t: (b, t, 0)),
                   pl.BlockSpec((1, HEADS, HEAD_DIM, HEAD_DIM), lambda b, t: (b, 0, 0, 0))],
        out_shape=[jax.ShapeDtypeStruct((bsz, seq, width), F32),
                   jax.ShapeDtypeStruct((bsz, HEADS, HEAD_DIM, HEAD_DIM), F32)],
        scratch_shapes=[pltpu.VMEM((HEADS, HEAD_DIM, HEAD_DIM), F32)],
        compiler_params=_cparams(("parallel", "arbitrary")),
        name="hgrn_prompt",
    )(z3, z3, z3, z3, lb.reshape(1, width), gn.reshape(1, HEAD_DIM),
      jnp.asarray(m_all, BF), jnp.asarray(masks, F32))
    return oa, jnp.swapaxes(st, -1, -2)


def _col(row, eye):
    return jnp.sum(eye * row, axis=1, keepdims=True)


def _hgrn_sample_body(qa_ref, fa_ref, ia_ref, ga_ref, lb_ref, gn_ref, s_ref, oa_ref, so_ref):
    eye = (lax.broadcasted_iota(I32, (HEAD_DIM, HEAD_DIM), 0)
           == lax.broadcasted_iota(I32, (HEAD_DIM, HEAD_DIM), 1)).astype(F32)
    rnd = lambda a: a.astype(BF).astype(F32)
    for g in range(SAMPLE_SEQ_GROUP):
        q8, f8 = _hgrn_gates(qa_ref[g], fa_ref[g], lb_ref[...])
        ga = ga_ref[g]
        gate = ga * _sigmoid(ga)
        kr = rnd(1.0 - f8)
        vr = rnd(ia_ref[g])
        qfr = rnd(q8 * f8)
        qk = rnd(jnp.sum(rnd(q8) * kr, axis=-1, keepdims=True))
        outs = []
        for h in range(HEADS):
            f_col = _col(f8[h:h + 1], eye)
            s_old = s_ref[g, h]
            so_ref[g, h] = f_col * s_old + _col(kr[h:h + 1], eye) * vr[h:h + 1]
            outs.append(qk[h:h + 1] * vr[h:h + 1]
                        + jnp.sum(_col(qfr[h:h + 1], eye) * rnd(s_old), axis=0, keepdims=True))
        o = jnp.concatenate(outs, axis=0)
        ms = jnp.mean(o * o, axis=-1, keepdims=True)
        oa_ref[g] = o * lax.rsqrt(ms + RMS_EPS) * gn_ref[...] * gate


def hgrn_sample(zs3, lb, gn, s0):
    bsz = zs3.shape[0]
    g = SAMPLE_SEQ_GROUP
    assert bsz % g == 0
    zspec = lambda cb: pl.BlockSpec((g, HEADS, HEAD_DIM), lambda b, cb=cb: (b, cb, 0))
    sspec = pl.BlockSpec((g, HEADS, HEAD_DIM, HEAD_DIM), lambda b: (b, 0, 0, 0))
    return pl.pallas_call(
        _hgrn_sample_body,
        grid=(bsz // g,),
        in_specs=[zspec(0), zspec(1), zspec(2), zspec(3),
                  pl.BlockSpec((HEADS, HEAD_DIM), lambda b: (0, 0)),
                  pl.BlockSpec((1, HEAD_DIM), lambda b: (0, 0)),
                  sspec],
        out_specs=[pl.BlockSpec((g, HEADS, HEAD_DIM), lambda b: (b, 0, 0)), sspec],
        out_shape=[jax.ShapeDtypeStruct((bsz, HEADS, HEAD_DIM), F32),
                   jax.ShapeDtypeStruct(s0.shape, F32)],
        compiler_params=_cparams(("parallel",)),
        name="hgrn_sample",
    )(zs3, zs3, zs3, zs3, lb.reshape(HEADS, HEAD_DIM), gn.reshape(1, HEAD_DIM), s0)


_KEY_NEG_INF = np.int32(np.uint32(0x807FFFFF).astype(np.int64) - (1 << 32))
_INT_MIN = np.int32(-(1 << 31))


def _count(u_ref, n_groups, group, thr, cmp):
    rows = u_ref.shape[1]
    step = min(rows, LANES)
    parts = []
    for r0 in range(0, rows, step):
        t = jnp.broadcast_to(thr[r0:r0 + step], (step, LANES))

        def body(gi, acc, r0=r0, t=t):
            for i in range(group):
                acc = acc + cmp(u_ref[gi * group + i, r0:r0 + step], t).astype(F32)
            return acc

        acc = lax.fori_loop(0, n_groups, body, jnp.zeros((step, LANES), F32))
        parts.append(jnp.sum(acc, axis=-1, keepdims=True))
    return parts[0] if len(parts) == 1 else jnp.concatenate(parts, axis=0)


def _topk_mask(score_ref, u_ref, write_tile, n_tiles, n_groups, group, k):
    rows = score_ref.shape[1]
    n_live = n_groups * group

    def to_key(t, carry):
        bits = pltpu.bitcast(score_ref[t], I32)
        u_ref[t] = jnp.where(bits < 0, bits ^ np.int32(0x7FFFFFFF), bits)
        return carry

    lax.fori_loop(0, n_live, to_key, 0)

    kf = float(k)
    ge = lambda u, t: u >= t
    cnt = _count(u_ref, n_groups, group, jnp.zeros((rows, 1), I32), ge)
    lo = jnp.where(cnt >= kf, np.int32(0), _INT_MIN)

    def bit_step(i, lo):
        cand = lo | (np.int32(1) << (30 - i))
        cnt = _count(u_ref, n_groups, group, cand, ge)
        return jnp.where(cnt >= kf, cand, lo)

    lo = lax.fori_loop(0, 31, bit_step, lo)
    c_gt = _count(u_ref, n_groups, group, lo, lambda u, t: u > t)
    c_eq = _count(u_ref, n_groups, group, lo, lambda u, t: u == t)
    need = kf - c_gt
    real = lo > _KEY_NEG_INF
    excess = jnp.where(real & (c_eq > need), 1.0, 0.0)
    any_excess = jnp.max(excess) > 0.0

    @pl.when(jnp.logical_not(any_excess))
    def _():
        def emit(t, carry):
            u = u_ref[t]
            write_tile(t, jnp.where((u >= lo) & (u > _KEY_NEG_INF), 1.0, 0.0))
            return carry

        lax.fori_loop(0, n_live, emit, 0)

    @pl.when(any_excess)
    def _():
        upper = (lax.broadcasted_iota(I32, (LANES, LANES), 0)
                 <= lax.broadcasted_iota(I32, (LANES, LANES), 1)).astype(BF)

        def emit(t, seen):
            u = u_ref[t]
            eq = jnp.where(u == lo, 1.0, 0.0)
            prefix = seen + _dot(eq.astype(BF), upper)
            take = (u > lo) | ((u == lo) & (prefix <= need))
            write_tile(t, jnp.where(take & (u > _KEY_NEG_INF), 1.0, 0.0))
            return seen + jnp.sum(eq, axis=-1, keepdims=True)

        lax.fori_loop(0, n_live, emit, jnp.zeros((rows, 1), F32))

    def clear(t, carry):
        write_tile(t, jnp.zeros((rows, LANES), F32))
        return carry

    lax.fori_loop(n_live, n_tiles, clear, 0)


QW = 256
SUB = 8
IDX_KCHUNK = 512
ATTN_KSTEP = 512
ATTN_TILES = ATTN_KSTEP // LANES


def _row_all(x8, op):
    return jnp.broadcast_to(op(x8, axis=0, keepdims=True), x8.shape)


def _count_t(u_ref, n_groups, group, thr_row, cmp):
    qw = u_ref.shape[2]
    thr = jnp.broadcast_to(thr_row, (SUB, qw))

    def body(gi, acc):
        for i in range(group):
            u = u_ref[gi * group + i].reshape(LANES // SUB, SUB, qw)
            acc = acc + jnp.sum(cmp(u, thr[None]).astype(F32), axis=0)
        return acc

    acc = lax.fori_loop(0, n_groups, body, jnp.zeros((SUB, qw), F32))
    return jnp.sum(acc, axis=0, keepdims=True)


def _topk_mask_t(score_ref, u_ref, write_tile, n_tiles, n_groups, group, k):
    qw = score_ref.shape[2]
    n_live = n_groups * group

    def to_key(t, carry):
        bits = pltpu.bitcast(score_ref[t], I32)
        u_ref[t] = jnp.where(bits < 0, bits ^ np.int32(0x7FFFFFFF), bits)
        return carry

    lax.fori_loop(0, n_live, to_key, 0)

    kf = float(k)
    ge = lambda u, t: u >= t
    cnt = _count_t(u_ref, n_groups, group, jnp.zeros((1, qw), I32), ge)
    lo = jnp.where(cnt >= kf, np.int32(0), _INT_MIN)

    def bit_step(i, lo):
        cand = lo | (np.int32(1) << (30 - i))
        cnt = _count_t(u_ref, n_groups, group, cand, ge)
        return jnp.where(cnt >= kf, cand, lo)

    lo = lax.fori_loop(0, 31, bit_step, lo)
    c_gt = _count_t(u_ref, n_groups, group, lo, lambda u, t: u > t)
    c_eq = _count_t(u_ref, n_groups, group, lo, lambda u, t: u == t)
    need = kf - c_gt
    excess = jnp.where((lo > _KEY_NEG_INF) & (c_eq > need), 1.0, 0.0)
    any_excess = jnp.max(excess) > 0.0
    lo_b = jnp.broadcast_to(lo, (LANES, qw))

    @pl.when(jnp.logical_not(any_excess))
    def _():
        def emit(t, carry):
            u = u_ref[t]
            write_tile(t, jnp.where((u >= lo_b) & (u > _KEY_NEG_INF), 1.0, 0.0))
            return carry

        lax.fori_loop(0, n_live, emit, 0)

    @pl.when(any_excess)
    def _():
        lower = (lax.broadcasted_iota(I32, (LANES, LANES), 1)
                 <= lax.broadcasted_iota(I32, (LANES, LANES), 0)).astype(BF)

        def emit(t, seen):
            u = u_ref[t]
            eq = jnp.where(u == lo_b, 1.0, 0.0)
            prefix = seen + _dot(lower, eq.astype(BF))
            take = (u > lo_b) | ((u == lo_b) & (prefix <= need))
            write_tile(t, jnp.where(take & (u > _KEY_NEG_INF), 1.0, 0.0))
            return seen + jnp.sum(eq, axis=0, keepdims=True)

        lax.fori_loop(0, n_live, emit, jnp.zeros((1, qw), F32))

    def clear(t, carry):
        write_tile(t, jnp.zeros((LANES, qw), F32))
        return carry

    lax.fori_loop(n_live, n_tiles, clear, 0)


def _indexer_t_body(qi_ref, wt_ref, tail_ref, mask_ref, score_ref, u_ref, *, seq):
    j = pl.program_id(1)
    n_tiles = seq // LANES
    tiles_per_chunk = IDX_KCHUNK // LANES
    qi = qi_ref[0].astype(BF)
    wt = wt_ref[0] * (IDX_DIM ** -0.5 * IDX_HEADS ** -0.5)

    n_chunks = (j * QW + QW - 1) // IDX_KCHUNK + 1
    q_pos = j * QW + lax.broadcasted_iota(I32, (IDX_KCHUNK, QW), 1)

    def chunk_step(ci, carry):
        k0 = pl.multiple_of(ci * IDX_KCHUNK, IDX_KCHUNK)
        kic = tail_ref[0, pl.ds(k0, IDX_KCHUNK), :][:, 0:IDX_DIM].astype(BF)
        acc = jnp.zeros((IDX_KCHUNK, QW), F32)
        for h in range(IDX_HEADS):
            s = _dot_nt(kic, qi[:, h * IDX_DIM:(h + 1) * IDX_DIM])
            acc = acc + wt[h:h + 1] * jnp.maximum(s, 0.0)
        k_pos = k0 + lax.broadcasted_iota(I32, (IDX_KCHUNK, QW), 0)
        acc = jnp.where(k_pos <= q_pos, acc, -jnp.inf)
        for i in range(tiles_per_chunk):
            score_ref[ci * tiles_per_chunk + i] = acc[i * LANES:(i + 1) * LANES]
        return carry

    lax.fori_loop(0, n_chunks, chunk_step, 0)

    def write_tile(t, m):
        mask_ref[0, 0, t] = m.astype(BF)

    _topk_mask_t(score_ref, u_ref, write_tile, n_tiles, n_chunks, tiles_per_chunk, TOPK)


def indexer_prompt_t(z3, wt, tail3):
    bsz, seq = z3.shape[:2]
    n_tiles = seq // LANES
    assert seq % QW == 0 and seq % IDX_KCHUNK == 0
    return pl.pallas_call(
        functools.partial(_indexer_t_body, seq=seq),
        grid=(bsz, seq // QW),
        in_specs=[pl.BlockSpec((1, QW, IDX_HEADS * IDX_DIM), lambda b, j: (b, j, 5)),
                  pl.BlockSpec((1, IDX_HEADS, QW), lambda b, j: (b, 0, j)),
                  pl.BlockSpec((1, seq, LANES), lambda b, j: (b, 0, 0))],
        out_specs=pl.BlockSpec((1, 1, n_tiles, LANES, QW), lambda b, j: (b, j, 0, 0, 0)),
        out_shape=jax.ShapeDtypeStruct((bsz, seq // QW, n_tiles, LANES, QW), BF),
        scratch_shapes=[pltpu.VMEM((n_tiles, LANES, QW), F32),
                        pltpu.VMEM((n_tiles, LANES, QW), I32)],
        compiler_params=_cparams(("parallel", "arbitrary")),
        name="indexer_prompt_t",
    )(z3, wt, tail3)


def _attn_t_body(bfar_ref, q_ref, k_ref, vt_ref, mask_ref, bias_ref, o_ref, m_ref, l_ref, acc_ref):
    j = pl.program_id(1)
    scale = HEAD_DIM ** -0.5
    qsub = QW // LANES
    q = q_ref[0].astype(BF)
    m_ref[...] = jnp.full(m_ref.shape, NEG_BIG, F32)
    l_ref[...] = jnp.zeros(l_ref.shape, F32)
    acc_ref[...] = jnp.zeros(acc_ref.shape, F32)

    def process(sb, near):
        k0 = pl.multiple_of(sb * ATTN_KSTEP, ATTN_KSTEP)
        kblk = k_ref[0, pl.ds(k0, ATTN_KSTEP), :]
        vblk = vt_ref[0, pl.ds(k0, ATTN_KSTEP), :]
        tiles = [sb * ATTN_TILES + i for i in range(ATTN_TILES)]
        sel = jnp.concatenate([mask_ref[0, 0, t] for t in tiles], axis=0).astype(F32) > 0.0
        if near:
            def tile_bias(t, h):
                row = []
                for s in range(qsub):
                    d = j * qsub + s - t
                    row.append(bias_ref[jnp.where(d == 0, 0, jnp.where(d == 1, 1, 2)), h])
                return jnp.concatenate(row, axis=1)
        head_cols = [slice(h * HEAD_DIM, (h + 1) * HEAD_DIM) for h in range(HEADS)]
        qk = lambda h: _dot_nt(kblk[:, head_cols[h]], q[:, head_cols[h]])
        lg_next = qk(0)
        for h in range(HEADS):
            cols = head_cols[h]
            lg = lg_next * scale
            if h + 1 < HEADS:
                lg_next = qk(h + 1)
            if near:
                lg = lg + jnp.concatenate([tile_bias(t, h) for t in tiles], axis=0)
            else:
                lg = lg + bfar_ref[h]
            lg = jnp.where(sel, lg, NEG_BIG)
            m_old = m_ref[h]
            part = jnp.max(lg.reshape(ATTN_KSTEP // SUB, SUB, QW), axis=0)
            m_new = jnp.maximum(m_old, _row_all(part, jnp.max))
            p = jnp.exp(lg - m_new[0:1])
            alpha = jnp.exp(m_old - m_new)
            psum = jnp.sum(p.reshape(ATTN_KSTEP // SUB, SUB, QW), axis=0)
            l_ref[h] = alpha * l_ref[h] + _row_all(psum, jnp.sum)
            pv = _dot_tn(vblk[:, cols], p.astype(BF))
            acc_ref[cols, :] = alpha[0:1] * acc_ref[cols, :] + pv
            m_ref[h] = m_new

    n_far = jnp.maximum((qsub * j - 1) // ATTN_TILES, 0)
    n_steps = (qsub * j + qsub - 1) // ATTN_TILES + 1

    def far_step(sb, carry):
        process(sb, False)
        return carry

    def near_step(sb, carry):
        process(sb, True)
        return carry

    lax.fori_loop(0, n_far, far_step, 0)
    lax.fori_loop(n_far, n_steps, near_step, 0)

    for h in range(HEADS):
        cols = slice(h * HEAD_DIM, (h + 1) * HEAD_DIM)
        o_ref[0, cols, :] = acc_ref[cols, :] / l_ref[h][0:1]


def attn_prompt_t(z3, kv_bf, mask_t, bias_tiles_t, bias_far):
    bsz, seq = z3.shape[:2]
    n_tiles = seq // LANES
    width = HEADS * HEAD_DIM
    assert seq % ATTN_KSTEP == 0 and seq % QW == 0
    return pl.pallas_call(
        _attn_t_body,
        grid=(bsz, seq // QW),
        in_specs=[pl.BlockSpec(memory_space=pltpu.SMEM),
                  pl.BlockSpec((1, QW, width), lambda b, j: (b, j, 4)),
                  pl.BlockSpec((1, seq, width), lambda b, j: (b, 0, 0)),
                  pl.BlockSpec((1, seq, width), lambda b, j: (b, 0, 1)),
                  pl.BlockSpec((1, 1, n_tiles, LANES, QW), lambda b, j: (b, j, 0, 0, 0)),
                  pl.BlockSpec(bias_tiles_t.shape, lambda b, j: (0, 0, 0, 0))],
        out_specs=pl.BlockSpec((1, width, QW), lambda b, j: (b, 0, j)),
        out_shape=jax.ShapeDtypeStruct((bsz, width, seq), F32),
        scratch_shapes=[pltpu.VMEM((HEADS, SUB, QW), F32),
                        pltpu.VMEM((HEADS, SUB, QW), F32),
                        pltpu.VMEM((width, QW), F32)],
        compiler_params=_cparams(("parallel", "arbitrary")),
        name="attn_prompt_t",
    )(bias_far, z3, kv_bf, kv_bf, mask_t, bias_tiles_t)


def _indexer_sample_body(pt_ref, qi_ref, wcol_ref, kinew_ref, *rest, n_pages):
    ki_refs = rest[:n_pages]
    out_ref = rest[n_pages]
    qi = qi_ref[0].astype(BF)
    w = wcol_ref[0] * (IDX_DIM ** -0.5 * IDX_HEADS ** -0.5)
    for i in range(n_pages):
        s = _dot(qi, ki_refs[i][0].astype(BF))
        out_ref[i, 0] = jnp.sum(w * jnp.maximum(s, 0.0), axis=0, keepdims=True)
    kn = kinew_ref[0].astype(BF).astype(F32)
    sn = jnp.sum(qi.astype(F32) * kn, axis=-1, keepdims=True)
    new = jnp.sum(w * jnp.maximum(sn, 0.0), axis=0, keepdims=True)
    lane = lax.broadcasted_iota(I32, (1, LANES), 1)
    out_ref[n_pages, 0] = jnp.where(lane == 0, new, -jnp.inf)


def indexer_sample(page_table, qi3, wcol, ki_new, ki_pool_t):
    bsz, n_pages = page_table.shape
    ki_spec = lambda i: pl.BlockSpec((1, IDX_DIM, PAGE), lambda b, pt, i=i: (pt[b, i], 0, 0))
    grid_spec = pltpu.PrefetchScalarGridSpec(
        num_scalar_prefetch=1,
        grid=(bsz,),
        in_specs=[pl.BlockSpec((1, IDX_HEADS, IDX_DIM), lambda b, pt: (b, 0, 0)),
                  pl.BlockSpec((1, IDX_HEADS, 1), lambda b, pt: (b, 0, 0)),
                  pl.BlockSpec((1, 1, IDX_DIM), lambda b, pt: (b, 0, 0))]
                 + [ki_spec(i) for i in range(n_pages)],
        out_specs=pl.BlockSpec((n_pages + 1, 1, 1, LANES), lambda b, pt: (0, b, 0, 0)),
    )
    out = pl.pallas_call(
        functools.partial(_indexer_sample_body, n_pages=n_pages),
        grid_spec=grid_spec,
        out_shape=jax.ShapeDtypeStruct((n_pages + 1, bsz, 1, LANES), F32),
        compiler_params=_cparams(("arbitrary",)),
        name="indexer_sample",
    )(page_table, qi3, wcol, ki_new, *([ki_pool_t] * n_pages))
    return out.reshape(n_pages + 1, bsz, LANES)


def _select_body(score_ref, mask_ref, u_ref, *, n_tiles, k):
    def write_tile(t, m):
        mask_ref[t] = m

    _topk_mask(score_ref, u_ref, write_tile, n_tiles, n_tiles, 1, k)


def select_topk(scores, k):
    n_tiles, rows, _ = scores.shape
    return pl.pallas_call(
        functools.partial(_select_body, n_tiles=n_tiles, k=k),
        grid=(1,),
        in_specs=[pl.BlockSpec(scores.shape, lambda i: (0, 0, 0))],
        out_specs=pl.BlockSpec(scores.shape, lambda i: (0, 0, 0)),
        out_shape=jax.ShapeDtypeStruct(scores.shape, F32),
        scratch_shapes=[pltpu.VMEM(scores.shape, I32)],
        compiler_params=_cparams(("arbitrary",)),
        name="select_topk",
    )(scores)


SC_CORES = 2
SC_SUBCORES = 16
SC_GATHER_CHUNK = 32


def sc_gather_rows(tables, idx):
    n_rows = idx.shape[0]
    workers = SC_CORES * SC_SUBCORES
    per_worker = n_rows // workers
    chunk = SC_GATHER_CHUNK
    assert n_rows % workers == 0 and per_worker % chunk == 0 and chunk % 8 == 0
    row_shape = tables[0].shape[1:]
    n_tab = len(tables)
    mesh = plsc.VectorSubcoreMesh(core_axis_name="c", subcore_axis_name="s",
                                  num_cores=SC_CORES, num_subcores=SC_SUBCORES)

    def body(*refs):
        tab_refs = refs[:n_tab]
        idx_hbm = refs[n_tab]
        out_refs = refs[n_tab + 1:2 * n_tab + 1]
        idx_v, rows_v, sem = refs[2 * n_tab + 1:]
        wid = lax.axis_index("s") * SC_CORES + lax.axis_index("c")

        @pl.loop(0, per_worker // chunk)
        def _(ci):
            off = pl.multiple_of(wid * per_worker + ci * chunk, 8)
            pltpu.sync_copy(idx_hbm.at[pl.ds(off, chunk)], idx_v)
            for tab, out in zip(tab_refs, out_refs):
                pltpu.async_copy(tab.at[idx_v], rows_v, sem).wait()
                pltpu.sync_copy(rows_v, out.at[pl.ds(off, chunk)])

    return pl.kernel(
        body,
        out_type=[jax.ShapeDtypeStruct((n_rows, *row_shape), t.dtype) for t in tables],
        mesh=mesh,
        scratch_types=[pltpu.VMEM((chunk,), I32),
                       pltpu.VMEM((chunk, *row_shape), tables[0].dtype),
                       pltpu.SemaphoreType.DMA],
        compiler_params=pltpu.CompilerParams(use_tc_tiling_on_sc=True),
        name="sc_gather_rows",
    )(*tables, idx)


SAMPLE_SEQ_GROUP = 8


def _attn_compact_body(dist_ref, near_ref, q_ref, kn_ref, vn_ref, btab_ref, kc_ref, vc_ref,
                       o_ref, bbuf):
    step = pl.program_id(0)
    scale = HEAD_DIM ** -0.5
    ones = jnp.ones((HEAD_DIM, LANES), BF)
    far_bias = btab_ref[REL_MAX_DIST]
    last = lax.broadcasted_iota(I32, (TOPK, 1, 1), 0) == TOPK - 1
    for g in range(SAMPLE_SEQ_GROUP):
        b = step * SAMPLE_SEQ_GROUP + g
        bbuf[g] = jnp.broadcast_to(far_bias[None], (TOPK, HEADS, LANES))

        def fill(t, carry, g=g, b=b):
            bbuf[g, t] = btab_ref[dist_ref[b, t]]
            return carry

        lax.fori_loop(near_ref[b], TOPK, fill, 0)

        is_new = dist_ref[b, TOPK - 1] == 0
        rows = slice(g * TOPK, (g + 1) * TOPK)
        rnd = lambda a: a.astype(BF).astype(F32)
        kc = rnd(jnp.where(last & is_new, kn_ref[g][None], kc_ref[rows]))
        vc = rnd(jnp.where(last & is_new, vn_ref[g][None], vc_ref[rows]))
        prod = (kc * rnd(q_ref[g])[None]).reshape(TOPK * HEADS, HEAD_DIM)
        hi = prod.astype(BF)
        lo = (prod - hi.astype(F32)).astype(BF)
        lg = (_dot(hi, ones) + _dot(lo, ones)).reshape(TOPK, HEADS, LANES) * scale + bbuf[g]
        m = jnp.max(lg, axis=0, keepdims=True)
        p = jnp.exp(lg - m)
        l = jnp.sum(p, axis=0)
        o_ref[g] = jnp.sum(rnd(p) * vc, axis=0) / l


def attn_sample_compact(dist, first_near, zs3, k_sel, v_sel, bias_by_dist):
    bsz = dist.shape[0]
    g = SAMPLE_SEQ_GROUP
    assert bsz % g == 0
    zspec = lambda cb: pl.BlockSpec((g, HEADS, HEAD_DIM), lambda i, d, n, cb=cb: (i, cb, 0))
    sel_spec = pl.BlockSpec((g * TOPK, HEADS, HEAD_DIM), lambda i, d, n: (i, 0, 0))
    grid_spec = pltpu.PrefetchScalarGridSpec(
        num_scalar_prefetch=2,
        grid=(bsz // g,),
        in_specs=[zspec(4), zspec(5), zspec(6),
                  pl.BlockSpec(bias_by_dist.shape, lambda i, d, n: (0, 0, 0)),
                  sel_spec, sel_spec],
        out_specs=pl.BlockSpec((g, HEADS, HEAD_DIM), lambda i, d, n: (i, 0, 0)),
        scratch_shapes=[pltpu.VMEM((g, TOPK, HEADS, HEAD_DIM), F32)],
    )
    return pl.pallas_call(
        _attn_compact_body,
        grid_spec=grid_spec,
        out_shape=jax.ShapeDtypeStruct((bsz, HEADS, HEAD_DIM), F32),
        compiler_params=_cparams(("arbitrary",)),
        name="attn_sample_compact",
    )(dist, first_near, zs3, zs3, zs3, bias_by_dist, k_sel, v_sel)


def _bucket_table(max_dist):
    exact = REL_BUCKETS // 2
    d = np.arange(max_dist + 1)
    df = np.maximum(d, 1).astype(np.float32)
    far = exact + (np.log(df / exact) / np.float32(math.log(REL_MAX_DIST / exact))
                   * (REL_BUCKETS - exact)).astype(np.int32)
    return np.where(d < exact, d, np.minimum(far, REL_BUCKETS - 1)).astype(np.int32)


def _bias_tables(rel_bias):
    tab = _bucket_table(2 * LANES)
    assert np.all(tab[REL_MAX_DIST:] == REL_BUCKETS - 1)
    i = np.arange(LANES)
    dist0 = np.maximum(i[:, None] - i[None, :], 0)
    dist1 = LANES + i[:, None] - i[None, :]
    far = np.full((LANES, LANES), REL_BUCKETS - 1)
    idx = np.stack([tab[dist0], tab[dist1], far])
    buckets = jnp.arange(REL_BUCKETS)
    lookup = lambda ix, spec: jnp.einsum(spec, (jnp.asarray(ix)[..., None] == buckets).astype(F32),
                                         rel_bias.astype(F32), precision=lax.Precision.HIGHEST)
    tiles = lookup(idx, "tijb,bh->thij")
    by_dist = jnp.broadcast_to(lookup(tab[:REL_MAX_DIST + 1], "db,bh->dh")[:, :, None],
                               (REL_MAX_DIST + 1, HEADS, LANES))
    return tiles.astype(F32), rel_bias[REL_BUCKETS - 1].astype(F32), by_dist.astype(F32)


def _conv_prompt_body(bg_ref, cg_ref, xt_ref, w_ref, v_ref, st_ref, carry_ref, *, tblock):
    t = pl.program_id(1)

    @pl.when(t == 0)
    def _():
        carry_ref[...] = jnp.zeros_like(carry_ref)

    u = cg_ref[0] * xt_ref[0]
    row = lax.broadcasted_iota(I32, u.shape, 0)
    c0 = carry_ref[0:1]
    c1 = carry_ref[1:2]
    u1 = jnp.where(row == 0, c1, pltpu.roll(u, 1, axis=0))
    u2 = jnp.where(row == 0, c0, jnp.where(row == 1, c1, pltpu.roll(u, 2, axis=0)))
    conv = w_ref[0:1] * u2 + w_ref[1:2] * u1 + w_ref[2:3] * u
    v_ref[0] = bg_ref[0] * conv
    last = u[tblock - 2:tblock]
    carry_ref[0:2] = last

    @pl.when(t == pl.num_programs(1) - 1)
    def _():
        st_ref[0] = last


def conv_prompt(zc3, w_conv, tblock=256):
    bsz, seq = zc3.shape[:2]
    c = zc3.shape[2] // 3
    zspec = lambda cb: pl.BlockSpec((1, tblock, c), lambda b, t, cb=cb: (b, t, cb))
    return pl.pallas_call(
        functools.partial(_conv_prompt_body, tblock=tblock),
        grid=(bsz, seq // tblock),
        in_specs=[zspec(0), zspec(1), zspec(2), pl.BlockSpec((3, c), lambda b, t: (0, 0))],
        out_specs=[pl.BlockSpec((1, tblock, c), lambda b, t: (b, t, 0)),
                   pl.BlockSpec((1, 2, c), lambda b, t: (b, 0, 0))],
        out_shape=[jax.ShapeDtypeStruct((bsz, seq, c), F32),
                   jax.ShapeDtypeStruct((bsz, 2, c), F32)],
        scratch_shapes=[pltpu.VMEM((8, c), F32)],
        compiler_params=_cparams(("parallel", "arbitrary")),
        name="conv_prompt",
    )(zc3, zc3, zc3, w_conv)


def _conv_sample_body(bg_ref, cg_ref, xt_ref, w_ref, s0_ref, s1_ref, v_ref, n0_ref, n1_ref):
    u = cg_ref[...] * xt_ref[...]
    conv = w_ref[0:1] * s0_ref[...] + w_ref[1:2] * s1_ref[...] + w_ref[2:3] * u
    v_ref[...] = bg_ref[...] * conv
    n0_ref[...] = s1_ref[...]
    n1_ref[...] = u


def conv_sample(zc, w_conv, s0, s1):
    bsz = zc.shape[0]
    c = zc.shape[1] // 3
    zspec = lambda cb: pl.BlockSpec((bsz, c), lambda i, cb=cb: (0, cb))
    full = pl.BlockSpec((bsz, c), lambda i: (0, 0))
    return pl.pallas_call(
        _conv_sample_body,
        grid=(1,),
        in_specs=[zspec(0), zspec(1), zspec(2), pl.BlockSpec((3, c), lambda i: (0, 0)), full, full],
        out_specs=[full, full, full],
        out_shape=[jax.ShapeDtypeStruct((bsz, c), F32)] * 3,
        compiler_params=_cparams(("arbitrary",)),
        name="conv_sample",
    )(zc, zc, zc, w_conv, s0, s1)


def _router_body(x_ref, g_ref, wr_ref, br_ref, h_ref, route_ref):
    x = x_ref[...]
    ms = jnp.mean(x * x, axis=-1, keepdims=True)
    h = x * lax.rsqrt(ms + RMS_EPS) * g_ref[...]
    hb = h.astype(BF)
    half = h.shape[1] // 2
    bits = pltpu.bitcast(hb.astype(F32), jnp.uint32)
    h_ref[...] = (bits[:, :half] & jnp.uint32(0xFFFF0000)) | (bits[:, half:] >> 16)
    logits = _dot(hb, wr_ref[...]) + br_ref[...]
    lane = lax.broadcasted_iota(I32, logits.shape, 1)
    big = np.int32(1 << 20)
    lg = jnp.where(lane < N_GROUPS, logits, -jnp.inf)
    g_max = jnp.max(lg, axis=-1, keepdims=True)
    g_idx = jnp.min(jnp.where(lg == g_max, lane, big), axis=-1, keepdims=True)
    g_w = 1.0 / jnp.sum(jnp.exp(lg - g_max), axis=-1, keepdims=True)
    first = N_GROUPS + EXP_PER_GROUP * g_idx
    le = jnp.where((lane >= first) & (lane < first + EXP_PER_GROUP), logits, -jnp.inf)
    l1 = jnp.max(le, axis=-1, keepdims=True)
    i1 = jnp.min(jnp.where(le == l1, lane, big), axis=-1, keepdims=True)
    le2 = jnp.where(lane == i1, -jnp.inf, le)
    l2 = jnp.max(le2, axis=-1, keepdims=True)
    i2 = jnp.min(jnp.where(le2 == l2, lane, big), axis=-1, keepdims=True)
    r = jnp.exp(l2 - l1)
    w1 = g_w / (1.0 + r)
    w2 = g_w * r / (1.0 + r)
    e1 = (i1 - N_GROUPS).astype(F32)
    e2 = (i2 - N_GROUPS).astype(F32)
    route_ref[...] = jnp.where(lane == 0, e1, jnp.where(lane == 1, e2,
                               jnp.where(lane == 2, w1, jnp.where(lane == 3, w2, 0.0))))


def moe_router(x, g, wr, br, tm):
    m, k = x.shape
    return pl.pallas_call(
        _router_body,
        grid=(m // tm,),
        in_specs=[pl.BlockSpec((tm, k), lambda i: (i, 0)),
                  pl.BlockSpec((1, k), lambda i: (0, 0)),
                  pl.BlockSpec((k, LANES), lambda i: (0, 0)),
                  pl.BlockSpec((1, LANES), lambda i: (0, 0))],
        out_specs=[pl.BlockSpec((tm, k // 2), lambda i: (i, 0)),
                   pl.BlockSpec((tm, LANES), lambda i: (i, 0))],
        out_shape=[jax.ShapeDtypeStruct((m, k // 2), jnp.uint32),
                   jax.ShapeDtypeStruct((m, LANES), F32)],
        compiler_params=_cparams(("parallel",)),
        name="moe_router",
    )(x, g.reshape(1, k), wr, br)


def _experts_body(te_ref, nu_ref, h_ref, wg_ref, wu_ref, wd_ref, o_ref, wg_bf, wu_bf, wd_bf):
    i = pl.program_id(0)

    @pl.when((i == 0) | (te_ref[i] != te_ref[jnp.maximum(i - 1, 0)]))
    def _():
        wg_bf[...] = wg_ref[0].astype(BF)
        wu_bf[...] = wu_ref[0].astype(BF)
        wd_bf[...] = wd_ref[0].astype(BF)

    @pl.when(i < nu_ref[0])
    def _():
        words = h_ref[...]
        left = pltpu.bitcast(words & jnp.uint32(0xFFFF0000), F32)
        right = pltpu.bitcast(words << 16, F32)
        h = jnp.concatenate([left, right], axis=1).astype(BF)
        a = _dot(h, wg_bf[...])
        b = _dot(h, wu_bf[...])
        hid = a * _sigmoid(a) * b
        o_ref[...] = _dot(hid.astype(BF), wd_bf[...])

    @pl.when(i >= nu_ref[0])
    def _():
        o_ref[...] = jnp.zeros_like(o_ref)


def moe_experts(tile_expert, n_used, hs, wg, wu, wd):
    p = hs.shape[0]
    k = wg.shape[1]
    f = wg.shape[2]
    n_tiles = p // MOE_TILE
    grid_spec = pltpu.PrefetchScalarGridSpec(
        num_scalar_prefetch=2,
        grid=(n_tiles,),
        in_specs=[pl.BlockSpec((MOE_TILE, k // 2), lambda i, te, nu: (i, 0)),
                  pl.BlockSpec((1, k, f), lambda i, te, nu: (te[i], 0, 0)),
                  pl.BlockSpec((1, k, f), lambda i, te, nu: (te[i], 0, 0)),
                  pl.BlockSpec((1, f, k), lambda i, te, nu: (te[i], 0, 0))],
        out_specs=pl.BlockSpec((MOE_TILE, k), lambda i, te, nu: (i, 0)),
        scratch_shapes=[pltpu.VMEM((k, f), BF), pltpu.VMEM((k, f), BF), pltpu.VMEM((f, k), BF)],
    )
    return pl.pallas_call(
        _experts_body,
        grid_spec=grid_spec,
        out_shape=jax.ShapeDtypeStruct((p, k), F32),
        compiler_params=_cparams(("arbitrary",)),
        name="moe_experts",
    )(tile_expert, n_used, hs, wg, wu, wd)


def _combine_body(x_ref, a_ref, b_ref, route_ref, g_ref, o_ref, *, normalize):
    y = x_ref[...] + route_ref[:, 2:3] * a_ref[...] + route_ref[:, 3:4] * b_ref[...]
    if normalize:
        ms = jnp.mean(y * y, axis=-1, keepdims=True)
        y = y * lax.rsqrt(ms + RMS_EPS) * g_ref[...]
    o_ref[...] = y


def moe_combine(x, out_a, out_b, route, tm, norm_g=None):
    m, k = x.shape
    row = pl.BlockSpec((tm, k), lambda i: (i, 0))
    g = jnp.ones((1, k), F32) if norm_g is None else norm_g.reshape(1, k)
    return pl.pallas_call(
        functools.partial(_combine_body, normalize=norm_g is not None),
        grid=(m // tm,),
        in_specs=[row, row, row, pl.BlockSpec((tm, LANES), lambda i: (i, 0)),
                  pl.BlockSpec((1, k), lambda i: (0, 0))],
        out_specs=row,
        out_shape=jax.ShapeDtypeStruct((m, k), F32),
        compiler_params=_cparams(("parallel",)),
        name="moe_combine",
    )(x, out_a, out_b, route, g)


def _rank_within_expert(onehot):
    n, e = onehot.shape
    blk = LANES
    assert n % blk == 0
    oh = onehot.astype(F32).reshape(n // blk, blk, e)
    strict = jnp.asarray(np.tril(np.ones((blk, blk), np.float32), -1))
    within = jnp.einsum("ij,bjk->bik", strict, oh, precision=lax.Precision.HIGHEST)
    totals = jnp.sum(oh, axis=1)
    before = jnp.cumsum(totals, axis=0) - totals
    rank = (within + before[:, None, :]).reshape(n, e)
    return jnp.sum(rank * onehot.astype(F32), axis=1).astype(I32), jnp.sum(totals, axis=0).astype(I32)


def hier_moe(xs, tms, g, wrg, brg, wre, bre, wg, wu, wd, layer, out_norm_g=None):
    d = xs[0].shape[1]
    m = sum(x.shape[0] for x in xs)
    wr = jnp.zeros((d, LANES), F32)
    wr = wr.at[:, :N_GROUPS].set(wrg).at[:, N_GROUPS:N_GROUPS + N_EXPERTS].set(wre.reshape(d, N_EXPERTS))
    br = jnp.zeros((1, LANES), F32)
    br = br.at[0, :N_GROUPS].set(brg).at[0, N_GROUPS:N_GROUPS + N_EXPERTS].set(bre.reshape(N_EXPERTS))
    routed = [moe_router(x, g, wr.astype(BF), br, tm) for x, tm in zip(xs, tms)]
    h_bf = jnp.concatenate([r[0] for r in routed], axis=0)
    route = jnp.concatenate([r[1] for r in routed], axis=0)

    eid = route[:, 0:2].astype(I32).reshape(-1)
    onehot = eid[:, None] == jnp.arange(N_EXPERTS, dtype=I32)[None, :]
    rank, counts = _rank_within_expert(onehot)
    padded = ((counts + MOE_TILE - 1) // MOE_TILE) * MOE_TILE
    ends = jnp.cumsum(padded)
    pos = jnp.sum(jnp.where(onehot, (ends - padded)[None, :], 0), axis=1) + rank
    n_rows = 2 * m + N_EXPERTS * MOE_TILE
    n_rows = -(-n_rows // MOE_TILE) * MOE_TILE
    token = jnp.zeros((n_rows,), I32).at[pos].set(jnp.arange(2 * m, dtype=I32) // 2)
    tile_start = jnp.arange(n_rows // MOE_TILE, dtype=I32) * MOE_TILE
    tile_expert = jnp.minimum(jnp.sum(tile_start[:, None] >= ends[None, :], axis=1),
                              N_EXPERTS - 1).astype(I32)
    n_used = (ends[-1] // MOE_TILE).astype(I32).reshape(1)

    hs = jnp.take(h_bf, token, axis=0, mode="clip")
    out = moe_experts(tile_expert + layer * N_EXPERTS, n_used, hs, wg, wu, wd)
    pos2 = pos.reshape(m, 2)
    res, r0 = [], 0
    for x, tm in zip(xs, tms):
        rows = slice(r0, r0 + x.shape[0])
        res.append(moe_combine(x, jnp.take(out, pos2[rows, 0], axis=0, mode="clip"),
                               jnp.take(out, pos2[rows, 1], axis=0, mode="clip"),
                               route[rows], tm, out_norm_g))
        r0 += x.shape[0]
    return res


def kernel(x_prompt, x_sample, cache_k, cache_v, cache_ki, state_hgrn, state_conv, page_table,
           norm_mix_g, norm_ffn_g, final_g, w_in_even, w_out_even, hgrn_lb_logits, hgrn_norm_g,
           rel_bias, w_in_conv, w_conv, w_out_conv, w_router_g, b_router_g, w_router_e,
           b_router_e, w_gate, w_up, w_down):
    bsz, seq, d = x_prompt.shape
    dec = x_sample.shape[0]
    n_p = bsz * seq
    tm_p, tm_s = 512, 128
    tms = (tm_p, tm_s)
    assert n_p % tm_p == 0 and dec % tm_s == 0 and x_sample.shape[1] == 1
    width = HEADS * HEAD_DIM
    even_in = w_in_even.shape[2]
    even_pad = -(-even_in // IN_PROJ_TN) * IN_PROJ_TN
    tail0 = TAIL_BLOCK * LANES
    n_pool = cache_k.shape[1]

    xp = x_prompt.reshape(n_p, d)
    xs = x_sample.reshape(dec, d)

    lbs = jnp.cumsum(jax.nn.softmax(hgrn_lb_logits.astype(F32), axis=0), axis=0)[:-1]
    bias_tiles, bias_far, bias_by_dist = _bias_tables(rel_bias)
    expert_w = [w.reshape(-1, *w.shape[2:]) for w in (w_gate, w_up, w_down)]

    w_in = jnp.pad(w_in_even[0], ((0, 0), (0, even_pad - even_in))).astype(BF)
    zp, k_new_p, v_new_p, kv_bf = even_projection(xp, norm_mix_g[0], w_in, 2 * tm_p)
    zt = norm_matmul(xp, norm_mix_g[0], w_in[:, tail0:tail0 + LANES], 2 * tm_p, LANES)
    zs = norm_matmul(xs, norm_mix_g[0], w_in, tm_s, IN_PROJ_TN)
    zp3 = zp.reshape(bsz, seq, 6 * width)
    zt3 = zt.reshape(bsz, seq, LANES)
    zs3 = zs.reshape(dec, even_pad // LANES, LANES)

    oa_p, hgrn_p = hgrn_prompt(zp3, lbs[0], hgrn_norm_g[0])
    oa_s, hgrn_s = hgrn_sample(zs3, lbs[0], hgrn_norm_g[0], state_hgrn[0])

    wt = jnp.swapaxes(zt3[:, :, IDX_DIM:IDX_DIM + IDX_HEADS], 1, 2)
    mask_p = indexer_prompt_t(zp3, wt, zt3)
    ob_p = jnp.swapaxes(attn_prompt_t(zp3, kv_bf.reshape(bsz, seq, 2 * width), mask_p,
                                      jnp.swapaxes(bias_tiles, -1, -2), bias_far), 1, 2)

    qi3 = zs[:, 7 * width:8 * width].reshape(dec, IDX_HEADS, IDX_DIM)
    ki_new = zs[:, tail0:tail0 + IDX_DIM].reshape(dec, 1, IDX_DIM)
    wcol = zs[:, tail0 + IDX_DIM:tail0 + IDX_DIM + IDX_HEADS].reshape(dec, IDX_HEADS, 1)
    pages = page_table + 0 * n_pool
    scores_s = indexer_sample(pages, qi3, wcol, ki_new,
                              jnp.swapaxes(cache_ki, -1, -2).reshape(-1, IDX_DIM, PAGE))
    mask_s = select_topk(scores_s, TOPK)
    sel = jnp.transpose(mask_s, (1, 0, 2))
    n_pages = page_table.shape[1]
    sel_off = jnp.concatenate([jnp.zeros((dec, 1), F32),
                               jnp.cumsum(jnp.sum(sel, axis=2), axis=1)], axis=1)
    slot = jnp.arange(TOPK, dtype=F32)
    page_of = jnp.sum(sel_off[:, None, 1:] <= slot[None, :, None], axis=2)
    page_1h = (page_of[:, :, None] == jnp.arange(n_pages + 1)[None, None, :]).astype(F32)
    local = slot[None, :] - jnp.einsum("brp,bp->br", page_1h, sel_off[:, :-1],
                                       precision=lax.Precision.HIGHEST)
    within = jnp.einsum("brp,bpl->brl", page_1h, jnp.cumsum(sel, axis=2),
                        precision=lax.Precision.HIGHEST)
    lane_of = jnp.sum(within <= local[:, :, None], axis=2)
    sel_idx = (page_of * PAGE + lane_of).astype(I32)
    past = n_pages * PAGE
    page_id = jnp.einsum("brp,bp->br", page_1h[:, :, :n_pages], pages.astype(F32),
                         precision=lax.Precision.HIGHEST).astype(I32)
    rows = jnp.where(page_of < n_pages, page_id * PAGE + lane_of, 0).astype(I32).reshape(-1)
    k_sel, v_sel = sc_gather_rows([cache_k.reshape(-1, HEADS, HEAD_DIM),
                                   cache_v.reshape(-1, HEADS, HEAD_DIM)], rows)
    dist = jnp.minimum(past - sel_idx, REL_MAX_DIST).astype(I32)
    first_near = jnp.sum(dist >= REL_MAX_DIST, axis=1).astype(I32)
    ob_s = attn_sample_compact(dist, first_near, zs3, k_sel, v_sel, bias_by_dist)

    w_out = w_out_even[0].astype(BF)
    w_out_ab = [w_out[:width], w_out[width:]]
    xp = matmul_residual([oa_p.reshape(n_p, width), ob_p.reshape(n_p, width)], w_out_ab, xp, 2 * tm_p, 1024)
    xs = matmul_residual([oa_s.reshape(dec, width), ob_s.reshape(dec, width)], w_out_ab, xs, tm_s, 512)

    xp, xs = hier_moe([xp, xs], tms, norm_ffn_g[0], w_router_g[0], b_router_g[0], w_router_e[0],
                      b_router_e[0], *expert_w, 0)

    w_in_c = w_in_conv[0].astype(BF)
    zcp = norm_matmul(xp, norm_mix_g[1], w_in_c, 2 * tm_p, 1024)
    zcs = norm_matmul(xs, norm_mix_g[1], w_in_c, tm_s, 512)
    cw = zcp.shape[1] // 3
    v_p, conv_p = conv_prompt(zcp.reshape(bsz, seq, 3 * cw), w_conv[0])
    v_s, cs0, cs1 = conv_sample(zcs, w_conv[0], state_conv[0, :, 0], state_conv[0, :, 1])
    w_out_c = [w_out_conv[0].astype(BF)]
    xp = matmul_residual([v_p.reshape(n_p, cw)], w_out_c, xp, 2 * tm_p, 1024)
    xs = matmul_residual([v_s], w_out_c, xs, tm_s, 512)

    yp, ys = hier_moe([xp, xs], tms, norm_ffn_g[1], w_router_g[1], b_router_g[1], w_router_e[1],
                      b_router_e[1], *expert_w, 1, out_norm_g=final_g)

    kcol, vcol = 5 * width, 6 * width
    heads = lambda a, n: a.reshape(1, *n, HEADS, HEAD_DIM)
    return (yp.reshape(bsz, seq, d),
            ys.reshape(dec, 1, d),
            heads(k_new_p, (bsz, seq)),
            heads(v_new_p, (bsz, seq)),
            zt[:, :IDX_DIM].reshape(1, bsz, seq, IDX_DIM),
            hgrn_p[None],
            conv_p[None],
            heads(zs[:, kcol:kcol + width], (dec, 1)),
            heads(zs[:, vcol:vcol + width], (dec, 1)),
            zs[:, tail0:tail0 + IDX_DIM].reshape(1, dec, 1, IDX_DIM),
            hgrn_s[None],
            jnp.stack([cs0, cs1], axis=1)[None])
```

```python
import functools
import math

import numpy as np
import jax
import jax.numpy as jnp
from jax import lax
from jax.experimental import pallas as pl
from jax.experimental.pallas import tpu as pltpu
from jax.experimental.pallas import tpu_sc as plsc

F32 = jnp.float32
BF = jnp.bfloat16
I32 = jnp.int32

RMS_EPS = 1e-6
LANES = 128
NEG_BIG = -1e30
VMEM_LIMIT = 56 * 1024 * 1024

D_MODEL = 2048
HEADS = 8
HEAD_DIM = 128
IDX_HEADS = 16
IDX_DIM = 64
TOPK = 256
REL_BUCKETS = 32
REL_MAX_DIST = 128
N_EXPERTS = 16
EXP_PER_GROUP = 4
N_GROUPS = 4
D_EXPERT = 512
PAGE = 128

HGRN_CHUNK = 128
HGRN_HEAD_GROUP = 8
MOE_TILE = 256
IN_PROJ_TN = 640
TAIL_BLOCK = 8 * HEADS * HEAD_DIM // LANES


def _cparams(sem):
    return pltpu.CompilerParams(dimension_semantics=sem, vmem_limit_bytes=VMEM_LIMIT)


def _dot(a, b):
    return jnp.dot(a, b, preferred_element_type=F32)


def _dot_nt(a, b):
    return lax.dot_general(a, b, (((1,), (1,)), ((), ())), preferred_element_type=F32)


def _dot_tn(a, b):
    return lax.dot_general(a, b, (((0,), (0,)), ((), ())), preferred_element_type=F32)


def _sigmoid(x):
    return 1.0 / (1.0 + jnp.exp(-x))


def _norm_mm_body(x_ref, g_ref, w_ref, o_ref, h_ref):
    @pl.when(pl.program_id(1) == 0)
    def _():
        x = x_ref[...]
        ms = jnp.mean(x * x, axis=-1, keepdims=True)
        h_ref[...] = (x * lax.rsqrt(ms + RMS_EPS) * g_ref[...]).astype(BF)

    o_ref[...] = _dot(h_ref[...], w_ref[...])


def norm_matmul(x, g, w_bf, tm, tn):
    m, k = x.shape
    n = w_bf.shape[1]
    return pl.pallas_call(
        _norm_mm_body,
        grid=(m // tm, n // tn),
        in_specs=[pl.BlockSpec((tm, k), lambda i, j: (i, 0)),
                  pl.BlockSpec((1, k), lambda i, j: (0, 0)),
                  pl.BlockSpec((k, tn), lambda i, j: (0, j))],
        out_specs=pl.BlockSpec((tm, tn), lambda i, j: (i, j)),
        out_shape=jax.ShapeDtypeStruct((m, n), F32),
        scratch_shapes=[pltpu.VMEM((tm, k), BF)],
        compiler_params=_cparams(("parallel", "arbitrary")),
        name="norm_matmul",
    )(x, g.reshape(1, k), w_bf)


EVEN_TN = 1024


def _even_proj_body(x_ref, g_ref, w_ref, z_ref, k_ref, v_ref, kv_ref, h_ref, *, per):
    j = pl.program_id(1)

    @pl.when(j == 0)
    def _():
        x = x_ref[...]
        ms = jnp.mean(x * x, axis=-1, keepdims=True)
        h_ref[...] = (x * lax.rsqrt(ms + RMS_EPS) * g_ref[...]).astype(BF)

    res = _dot(h_ref[...], w_ref[...])
    sec = j // per

    @pl.when((sec < 5) | (sec == 7))
    def _():
        z_ref[...] = res

    @pl.when(sec == 5)
    def _():
        k_ref[...] = res
        kv_ref[...] = res.astype(BF)

    @pl.when(sec == 6)
    def _():
        v_ref[...] = res
        kv_ref[...] = res.astype(BF)


def even_projection(x, g, w_bf, tm):
    m, k = x.shape
    width = HEADS * HEAD_DIM
    per = width // EVEN_TN
    clip = lambda a, lo, hi: jnp.minimum(jnp.maximum(a, lo), hi)
    return pl.pallas_call(
        functools.partial(_even_proj_body, per=per),
        grid=(m // tm, 8 * per),
        in_specs=[pl.BlockSpec((tm, k), lambda i, j: (i, 0)),
                  pl.BlockSpec((1, k), lambda i, j: (0, 0)),
                  pl.BlockSpec((k, EVEN_TN), lambda i, j: (0, j))],
        out_specs=[pl.BlockSpec((tm, EVEN_TN), lambda i, j: (i, j - clip(j - (5 * per - 1), 0, 2 * per))),
                   pl.BlockSpec((tm, EVEN_TN), lambda i, j: (i, clip(j - 5 * per, 0, per - 1)),
                                pipeline_mode=pl.Buffered(1)),
                   pl.BlockSpec((tm, EVEN_TN), lambda i, j: (i, clip(j - 6 * per, 0, per - 1)),
                                pipeline_mode=pl.Buffered(1)),
                   pl.BlockSpec((tm, EVEN_TN), lambda i, j: (i, clip(j - 5 * per, 0, 2 * per - 1)),
                                pipeline_mode=pl.Buffered(1))],
        out_shape=[jax.ShapeDtypeStruct((m, 6 * width), F32),
                   jax.ShapeDtypeStruct((m, width), F32),
                   jax.ShapeDtypeStruct((m, width), F32),
                   jax.ShapeDtypeStruct((m, 2 * width), BF)],
        scratch_shapes=[pltpu.VMEM((tm, k), BF)],
        compiler_params=_cparams(("parallel", "arbitrary")),
        name="even_projection",
    )(x, g.reshape(1, k), w_bf)


def _mm_res_body(*refs, n_lhs):
    a_refs = refs[:n_lhs]
    w_refs = refs[n_lhs:2 * n_lhs]
    r_ref = refs[2 * n_lhs]
    o_ref = refs[2 * n_lhs + 1]
    s_refs = refs[2 * n_lhs + 2:]

    @pl.when(pl.program_id(1) == 0)
    def _():
        for a_ref, s_ref in zip(a_refs, s_refs):
            s_ref[...] = a_ref[...].astype(BF)

    acc = r_ref[...]
    for s_ref, w_ref in zip(s_refs, w_refs):
        acc = acc + _dot(s_ref[...], w_ref[...])
    o_ref[...] = acc


def matmul_residual(lhs, ws_bf, res, tm, tn):
    m, n = res.shape
    n_lhs = len(lhs)
    in_specs = ([pl.BlockSpec((tm, a.shape[1]), lambda i, j: (i, 0)) for a in lhs]
                + [pl.BlockSpec((w.shape[0], tn), lambda i, j: (0, j)) for w in ws_bf]
                + [pl.BlockSpec((tm, tn), lambda i, j: (i, j))])
    return pl.pallas_call(
        functools.partial(_mm_res_body, n_lhs=n_lhs),
        grid=(m // tm, n // tn),
        in_specs=in_specs,
        out_specs=pl.BlockSpec((tm, tn), lambda i, j: (i, j)),
        out_shape=jax.ShapeDtypeStruct((m, n), F32),
        scratch_shapes=[pltpu.VMEM((tm, a.shape[1]), BF) for a in lhs],
        compiler_params=_cparams(("parallel", "arbitrary")),
        name="matmul_residual",
    )(*lhs, *ws_bf, res)


def _hgrn_static(c):
    levels = []
    m = 1
    while m < c:
        levels.append(m)
        m *= 2
    t = np.arange(c)
    rows = [t[None, :] <= t[:, None]]
    masks = [np.eye(c, dtype=bool)]
    for m in levels:
        blk = t // (2 * m)
        pos = t % (2 * m)
        bnd = blk * 2 * m + m - 1
        right = pos >= m
        left = pos < m
        e_rows = (t[None, :] > bnd[:, None]) & (t[None, :] <= t[:, None]) & right[:, None]
        f_rows = (t[None, :] > t[:, None]) & (t[None, :] <= bnd[:, None]) & left[:, None]
        rows.append(e_rows | f_rows)
        masks.append((blk[:, None] == blk[None, :]) & right[:, None] & left[None, :])
    m_all = np.stack(rows).astype(np.float32)
    masks = np.stack(masks).astype(np.float32)
    return m_all, masks, len(levels)


def _hgrn_gates(qa, fa, lb):
    f = lb + (1.0 - lb) * _sigmoid(fa)
    q = qa * _sigmoid(qa)
    return q, f


def _hgrn_prompt_body(qa_ref, fa_ref, ia_ref, ga_ref, lb_ref, gn_ref, mall_ref, masks_ref,
                      oa_ref, st_out_ref, st_ref, *, chunk, tblock, n_levels):
    c = chunk
    t_idx = pl.program_id(1)

    @pl.when(t_idx == 0)
    def _():
        st_ref[...] = jnp.zeros_like(st_ref)

    gn = gn_ref[...]

    def chunk_step(ci, carry):
        r0 = pl.multiple_of(ci * c, c)
        for h0 in range(0, HEADS, HGRN_HEAD_GROUP):
            hs = range(h0, h0 + HGRN_HEAD_GROUP)
            cols = {h: slice(h * HEAD_DIM, (h + 1) * HEAD_DIM) for h in hs}
            q, k, g_hi, g_lo, v = {}, {}, {}, {}, {}
            for h in hs:
                q[h], f = _hgrn_gates(qa_ref[0, pl.ds(r0, c), cols[h]],
                                      fa_ref[0, pl.ds(r0, c), cols[h]], lb_ref[:, cols[h]])
                k[h] = 1.0 - f
                g = jnp.log(f)
                g_hi[h] = g.astype(BF)
                g_lo[h] = (g - g_hi[h].astype(F32)).astype(BF)
                v[h] = ia_ref[0, pl.ds(r0, c), cols[h]].astype(BF)
            b = {h: _dot(mall_ref[0], g_hi[h]) + _dot(mall_ref[0], g_lo[h]) for h in hs}
            o = {h: _dot_nt((q[h] * jnp.exp(b[h])).astype(BF), st_ref[h].astype(BF)) for h in hs}
            a = {h: masks_ref[0] * _dot_nt(q[h].astype(BF), k[h].astype(BF)) for h in hs}
            for li in range(n_levels):
                ml = mall_ref[1 + li]
                w = {h: jnp.exp(_dot(ml, g_hi[h]) + _dot(ml, g_lo[h])) for h in hs}
                for h in hs:
                    a[h] = a[h] + masks_ref[1 + li] * _dot_nt((q[h] * w[h]).astype(BF),
                                                              (k[h] * w[h]).astype(BF))
            for h in hs:
                o[h] = o[h] + _dot(a[h].astype(BF), v[h])
            for h in hs:
                b_last = b[h][c - 1:c]
                k_st = (k[h] * jnp.exp(b_last - b[h])).astype(BF)
                st_ref[h] = st_ref[h] * jnp.exp(b_last) + _dot_tn(v[h], k_st)
            for h in hs:
                ga = ga_ref[0, pl.ds(r0, c), cols[h]]
                ms = jnp.mean(o[h] * o[h], axis=-1, keepdims=True)
                oa_ref[0, pl.ds(r0, c), cols[h]] = (o[h] * lax.rsqrt(ms + RMS_EPS) * gn
                                                    * (ga * _sigmoid(ga)))
        return carry

    lax.fori_loop(0, tblock // c, chunk_step, 0)

    @pl.when(t_idx == pl.num_programs(1) - 1)
    def _():
        st_out_ref[0] = st_ref[...]


def hgrn_prompt(z3, lb, gn, tblock=256, chunk=HGRN_CHUNK):
    bsz, seq = z3.shape[:2]
    width = HEADS * HEAD_DIM
    m_all, masks, n_levels = _hgrn_static(chunk)
    zspec = lambda cb: pl.BlockSpec((1, tblock, width), lambda b, t, cb=cb: (b, t, cb))
    oa, st = pl.pallas_call(
        functools.partial(_hgrn_prompt_body, chunk=chunk, tblock=tblock, n_levels=n_levels),
        grid=(bsz, seq // tblock),
        in_specs=[zspec(0), zspec(1), zspec(2), zspec(3),
                  pl.BlockSpec((1, width), lambda b, t: (0, 0)),
                  pl.BlockSpec((1, HEAD_DIM), lambda b, t: (0, 0)),
                  pl.BlockSpec(m_all.shape, lambda b, t: (0, 0, 0)),
                  pl.BlockSpec(masks.shape, lambda b, t: (0, 0, 0))],
        out_specs=[pl.BlockSpec((1, tblock, width), lambda b, t: (b, t, 0)),
                   pl.BlockSpec((1, HEADS, HEAD_DIM, HEAD_DIM), lambda b, t: (b, 0, 0, 0))],
        out_shape=[jax.ShapeDtypeStruct((bsz, seq, width), F32),
                   jax.ShapeDtypeStruct((bsz, HEADS, HEAD_DIM, HEAD_DIM), F32)],
        scratch_shapes=[pltpu.VMEM((HEADS, HEAD_DIM, HEAD_DIM), F32)],
        compiler_params=_cparams(("parallel", "arbitrary")),
        name="hgrn_prompt",
    )(z3, z3, z3, z3, lb.reshape(1, width), gn.reshape(1, HEAD_DIM),
      jnp.asarray(m_all, BF), jnp.asarray(masks, F32))
    return oa, jnp.swapaxes(st, -1, -2)


def _col(row, eye):
    return jnp.sum(eye * row, axis=1, keepdims=True)


def _hgrn_sample_body(qa_ref, fa_ref, ia_ref, ga_ref, lb_ref, gn_ref, s_ref, oa_ref, so_ref):
    eye = (lax.broadcasted_iota(I32, (HEAD_DIM, HEAD_DIM), 0)
           == lax.broadcasted_iota(I32, (HEAD_DIM, HEAD_DIM), 1)).astype(F32)
    rnd = lambda a: a.astype(BF).astype(F32)
    for g in range(SAMPLE_SEQ_GROUP):
        q8, f8 = _hgrn_gates(qa_ref[g], fa_ref[g], lb_ref[...])
        ga = ga_ref[g]
        gate = ga * _sigmoid(ga)
        kr = rnd(1.0 - f8)
        vr = rnd(ia_ref[g])
        qfr = rnd(q8 * f8)
        qk = rnd(jnp.sum(rnd(q8) * kr, axis=-1, keepdims=True))
        outs = []
        for h in range(HEADS):
            f_col = _col(f8[h:h + 1], eye)
            s_old = s_ref[g, h]
            so_ref[g, h] = f_col * s_old + _col(kr[h:h + 1], eye) * vr[h:h + 1]
            outs.append(qk[h:h + 1] * vr[h:h + 1]
                        + jnp.sum(_col(qfr[h:h + 1], eye) * rnd(s_old), axis=0, keepdims=True))
        o = jnp.concatenate(outs, axis=0)
        ms = jnp.mean(o * o, axis=-1, keepdims=True)
        oa_ref[g] = o * lax.rsqrt(ms + RMS_EPS) * gn_ref[...] * gate


def hgrn_sample(zs3, lb, gn, s0):
    bsz = zs3.shape[0]
    g = SAMPLE_SEQ_GROUP
    assert bsz % g == 0
    zspec = lambda cb: pl.BlockSpec((g, HEADS, HEAD_DIM), lambda b, cb=cb: (b, cb, 0))
    sspec = pl.BlockSpec((g, HEADS, HEAD_DIM, HEAD_DIM), lambda b: (b, 0, 0, 0))
    return pl.pallas_call(
        _hgrn_sample_body,
        grid=(bsz // g,),
        in_specs=[zspec(0), zspec(1), zspec(2), zspec(3),
                  pl.BlockSpec((HEADS, HEAD_DIM), lambda b: (0, 0)),
                  pl.BlockSpec((1, HEAD_DIM), lambda b: (0, 0)),
                  sspec],
        out_specs=[pl.BlockSpec((g, HEADS, HEAD_DIM), lambda b: (b, 0, 0)), sspec],
        out_shape=[jax.ShapeDtypeStruct((bsz, HEADS, HEAD_DIM), F32),
                   jax.ShapeDtypeStruct(s0.shape, F32)],
        compiler_params=_cparams(("parallel",)),
        name="hgrn_sample",
    )(zs3, zs3, zs3, zs3, lb.reshape(HEADS, HEAD_DIM), gn.reshape(1, HEAD_DIM), s0)


_KEY_NEG_INF = np.int32(np.uint32(0x807FFFFF).astype(np.int64) - (1 << 32))
_INT_MIN = np.int32(-(1 << 31))


def _count(u_ref, n_groups, group, thr, cmp):
    rows = u_ref.shape[1]
    step = min(rows, LANES)
    parts = []
    for r0 in range(0, rows, step):
        t = jnp.broadcast_to(thr[r0:r0 + step], (step, LANES))

        def body(gi, acc, r0=r0, t=t):
            for i in range(group):
                acc = acc + cmp(u_ref[gi * group + i, r0:r0 + step], t).astype(F32)
            return acc

        acc = lax.fori_loop(0, n_groups, body, jnp.zeros((step, LANES), F32))
        parts.append(jnp.sum(acc, axis=-1, keepdims=True))
    return parts[0] if len(parts) == 1 else jnp.concatenate(parts, axis=0)


def _topk_mask(score_ref, u_ref, write_tile, n_tiles, n_groups, group, k):
    rows = score_ref.shape[1]
    n_live = n_groups * group

    def to_key(t, carry):
        bits = pltpu.bitcast(score_ref[t], I32)
        u_ref[t] = jnp.where(bits < 0, bits ^ np.int32(0x7FFFFFFF), bits)
        return carry

    lax.fori_loop(0, n_live, to_key, 0)

    kf = float(k)
    ge = lambda u, t: u >= t
    cnt = _count(u_ref, n_groups, group, jnp.zeros((rows, 1), I32), ge)
    lo = jnp.where(cnt >= kf, np.int32(0), _INT_MIN)

    def bit_step(i, lo):
        cand = lo | (np.int32(1) << (30 - i))
        cnt = _count(u_ref, n_groups, group, cand, ge)
        return jnp.where(cnt >= kf, cand, lo)

    lo = lax.fori_loop(0, 31, bit_step, lo)
    c_gt = _count(u_ref, n_groups, group, lo, lambda u, t: u > t)
    c_eq = _count(u_ref, n_groups, group, lo, lambda u, t: u == t)
    need = kf - c_gt
    real = lo > _KEY_NEG_INF
    excess = jnp.where(real & (c_eq > need), 1.0, 0.0)
    any_excess = jnp.max(excess) > 0.0

    @pl.when(jnp.logical_not(any_excess))
    def _():
        def emit(t, carry):
            u = u_ref[t]
            write_tile(t, jnp.where((u >= lo) & (u > _KEY_NEG_INF), 1.0, 0.0))
            return carry

        lax.fori_loop(0, n_live, emit, 0)

    @pl.when(any_excess)
    def _():
        upper = (lax.broadcasted_iota(I32, (LANES, LANES), 0)
                 <= lax.broadcasted_iota(I32, (LANES, LANES), 1)).astype(BF)

        def emit(t, seen):
            u = u_ref[t]
            eq = jnp.where(u == lo, 1.0, 0.0)
            prefix = seen + _dot(eq.astype(BF), upper)
            take = (u > lo) | ((u == lo) & (prefix <= need))
            write_tile(t, jnp.where(take & (u > _KEY_NEG_INF), 1.0, 0.0))
            return seen + jnp.sum(eq, axis=-1, keepdims=True)

        lax.fori_loop(0, n_live, emit, jnp.zeros((rows, 1), F32))

    def clear(t, carry):
        write_tile(t, jnp.zeros((rows, LANES), F32))
        return carry

    lax.fori_loop(n_live, n_tiles, clear, 0)


QW = 256
SUB = 8
IDX_KCHUNK = 512
ATTN_KSTEP = 512
ATTN_TILES = ATTN_KSTEP // LANES


def _row_all(x8, op):
    return jnp.broadcast_to(op(x8, axis=0, keepdims=True), x8.shape)


def _count_t(u_ref, n_groups, group, thr_row, cmp):
    qw = u_ref.shape[2]
    thr = jnp.broadcast_to(thr_row, (SUB, qw))

    def body(gi, acc):
        for i in range(group):
            u = u_ref[gi * group + i].reshape(LANES // SUB, SUB, qw)
            acc = acc + jnp.sum(cmp(u, thr[None]).astype(F32), axis=0)
        return acc

    acc = lax.fori_loop(0, n_groups, body, jnp.zeros((SUB, qw), F32))
    return jnp.sum(acc, axis=0, keepdims=True)


def _topk_mask_t(score_ref, u_ref, write_tile, n_tiles, n_groups, group, k):
    qw = score_ref.shape[2]
    n_live = n_groups * group

    def to_key(t, carry):
        bits = pltpu.bitcast(score_ref[t], I32)
        u_ref[t] = jnp.where(bits < 0, bits ^ np.int32(0x7FFFFFFF), bits)
        return carry

    lax.fori_loop(0, n_live, to_key, 0)

    kf = float(k)
    ge = lambda u, t: u >= t
    cnt = _count_t(u_ref, n_groups, group, jnp.zeros((1, qw), I32), ge)
    lo = jnp.where(cnt >= kf, np.int32(0), _INT_MIN)

    def bit_step(i, lo):
        cand = lo | (np.int32(1) << (30 - i))
        cnt = _count_t(u_ref, n_groups, group, cand, ge)
        return jnp.where(cnt >= kf, cand, lo)

    lo = lax.fori_loop(0, 31, bit_step, lo)
    c_gt = _count_t(u_ref, n_groups, group, lo, lambda u, t: u > t)
    c_eq = _count_t(u_ref, n_groups, group, lo, lambda u, t: u == t)
    need = kf - c_gt
    excess = jnp.where((lo > _KEY_NEG_INF) & (c_eq > need), 1.0, 0.0)
    any_excess = jnp.max(excess) > 0.0
    lo_b = jnp.broadcast_to(lo, (LANES, qw))

    @pl.when(jnp.logical_not(any_excess))
    def _():
        def emit(t, carry):
            u = u_ref[t]
            write_tile(t, jnp.where((u >= lo_b) & (u > _KEY_NEG_INF), 1.0, 0.0))
            return carry

        lax.fori_loop(0, n_live, emit, 0)

    @pl.when(any_excess)
    def _():
        lower = (lax.broadcasted_iota(I32, (LANES, LANES), 1)
                 <= lax.broadcasted_iota(I32, (LANES, LANES), 0)).astype(BF)

        def emit(t, seen):
            u = u_ref[t]
            eq = jnp.where(u == lo_b, 1.0, 0.0)
            prefix = seen + _dot(lower, eq.astype(BF))
            take = (u > lo_b) | ((u == lo_b) & (prefix <= need))
            write_tile(t, jnp.where(take & (u > _KEY_NEG_INF), 1.0, 0.0))
            return seen + jnp.sum(eq, axis=0, keepdims=True)

        lax.fori_loop(0, n_live, emit, jnp.zeros((1, qw), F32))

    def clear(t, carry):
        write_tile(t, jnp.zeros((LANES, qw), F32))
        return carry

    lax.fori_loop(n_live, n_tiles, clear, 0)


def _indexer_t_body(qi_ref, wt_ref, tail_ref, mask_ref, score_ref, u_ref, *, seq):
    j = pl.program_id(1)
    n_tiles = seq // LANES
    tiles_per_chunk = IDX_KCHUNK // LANES
    qi = qi_ref[0].astype(BF)
    wt = wt_ref[0] * (IDX_DIM ** -0.5 * IDX_HEADS ** -0.5)

    n_chunks = (j * QW + QW - 1) // IDX_KCHUNK + 1
    q_pos = j * QW + lax.broadcasted_iota(I32, (IDX_KCHUNK, QW), 1)

    def chunk_step(ci, carry):
        k0 = pl.multiple_of(ci * IDX_KCHUNK, IDX_KCHUNK)
        kic = tail_ref[0, pl.ds(k0, IDX_KCHUNK), :][:, 0:IDX_DIM].astype(BF)
        acc = jnp.zeros((IDX_KCHUNK, QW), F32)
        for h in range(IDX_HEADS):
            s = _dot_nt(kic, qi[:, h * IDX_DIM:(h + 1) * IDX_DIM])
            acc = acc + wt[h:h + 1] * jnp.maximum(s, 0.0)
        k_pos = k0 + lax.broadcasted_iota(I32, (IDX_KCHUNK, QW), 0)
        acc = jnp.where(k_pos <= q_pos, acc, -jnp.inf)
        for i in range(tiles_per_chunk):
            score_ref[ci * tiles_per_chunk + i] = acc[i * LANES:(i + 1) * LANES]
        return carry

    lax.fori_loop(0, n_chunks, chunk_step, 0)

    def write_tile(t, m):
        mask_ref[0, 0, t] = m.astype(BF)

    _topk_mask_t(score_ref, u_ref, write_tile, n_tiles, n_chunks, tiles_per_chunk, TOPK)


def indexer_prompt_t(z3, wt, tail3):
    bsz, seq = z3.shape[:2]
    n_tiles = seq // LANES
    assert seq % QW == 0 and seq % IDX_KCHUNK == 0
    return pl.pallas_call(
        functools.partial(_indexer_t_body, seq=seq),
        grid=(bsz, seq // QW),
        in_specs=[pl.BlockSpec((1, QW, IDX_HEADS * IDX_DIM), lambda b, j: (b, j, 5)),
                  pl.BlockSpec((1, IDX_HEADS, QW), lambda b, j: (b, 0, j)),
                  pl.BlockSpec((1, seq, LANES), lambda b, j: (b, 0, 0))],
        out_specs=pl.BlockSpec((1, 1, n_tiles, LANES, QW), lambda b, j: (b, j, 0, 0, 0)),
        out_shape=jax.ShapeDtypeStruct((bsz, seq // QW, n_tiles, LANES, QW), BF),
        scratch_shapes=[pltpu.VMEM((n_tiles, LANES, QW), F32),
                        pltpu.VMEM((n_tiles, LANES, QW), I32)],
        compiler_params=_cparams(("parallel", "arbitrary")),
        name="indexer_prompt_t",
    )(z3, wt, tail3)


def _attn_t_body(bfar_ref, q_ref, k_ref, vt_ref, mask_ref, bias_ref, o_ref, m_ref, l_ref, acc_ref):
    j = pl.program_id(1)
    scale = HEAD_DIM ** -0.5
    qsub = QW // LANES
    q = q_ref[0].astype(BF)
    m_ref[...] = jnp.full(m_ref.shape, NEG_BIG, F32)
    l_ref[...] = jnp.zeros(l_ref.shape, F32)
    acc_ref[...] = jnp.zeros(acc_ref.shape, F32)

    def process(sb, near):
        k0 = pl.multiple_of(sb * ATTN_KSTEP, ATTN_KSTEP)
        kblk = k_ref[0, pl.ds(k0, ATTN_KSTEP), :]
        vblk = vt_ref[0, pl.ds(k0, ATTN_KSTEP), :]
        tiles = [sb * ATTN_TILES + i for i in range(ATTN_TILES)]
        sel = jnp.concatenate([mask_ref[0, 0, t] for t in tiles], axis=0).astype(F32) > 0.0
        if near:
            def tile_bias(t, h):
                row = []
                for s in range(qsub):
                    d = j * qsub + s - t
                    row.append(bias_ref[jnp.where(d == 0, 0, jnp.where(d == 1, 1, 2)), h])
                return jnp.concatenate(row, axis=1)
        head_cols = [slice(h * HEAD_DIM, (h + 1) * HEAD_DIM) for h in range(HEADS)]
        qk = lambda h: _dot_nt(kblk[:, head_cols[h]], q[:, head_cols[h]])
        lg_next = qk(0)
        for h in range(HEADS):
            cols = head_cols[h]
            lg = lg_next * scale
            if h + 1 < HEADS:
                lg_next = qk(h + 1)
            if near:
                lg = lg + jnp.concatenate([tile_bias(t, h) for t in tiles], axis=0)
            else:
                lg = lg + bfar_ref[h]
            lg = jnp.where(sel, lg, NEG_BIG)
            m_old = m_ref[h]
            part = jnp.max(lg.reshape(ATTN_KSTEP // SUB, SUB, QW), axis=0)
            m_new = jnp.maximum(m_old, _row_all(part, jnp.max))
            p = jnp.exp(lg - m_new[0:1])
            alpha = jnp.exp(m_old - m_new)
            psum = jnp.sum(p.reshape(ATTN_KSTEP // SUB, SUB, QW), axis=0)
            l_ref[h] = alpha * l_ref[h] + _row_all(psum, jnp.sum)
            pv = _dot_tn(vblk[:, cols], p.astype(BF))
            acc_ref[cols, :] = alpha[0:1] * acc_ref[cols, :] + pv
            m_ref[h] = m_new

    n_far = jnp.maximum((qsub * j - 1) // ATTN_TILES, 0)
    n_steps = (qsub * j + qsub - 1) // ATTN_TILES + 1

    def far_step(sb, carry):
        process(sb, False)
        return carry

    def near_step(sb, carry):
        process(sb, True)
        return carry

    lax.fori_loop(0, n_far, far_step, 0)
    lax.fori_loop(n_far, n_steps, near_step, 0)

    for h in range(HEADS):
        cols = slice(h * HEAD_DIM, (h + 1) * HEAD_DIM)
        o_ref[0, cols, :] = acc_ref[cols, :] / l_ref[h][0:1]


def attn_prompt_t(z3, kv_bf, mask_t, bias_tiles_t, bias_far):
    bsz, seq = z3.shape[:2]
    n_tiles = seq // LANES
    width = HEADS * HEAD_DIM
    assert seq % ATTN_KSTEP == 0 and seq % QW == 0
    return pl.pallas_call(
        _attn_t_body,
        grid=(bsz, seq // QW),
        in_specs=[pl.BlockSpec(memory_space=pltpu.SMEM),
                  pl.BlockSpec((1, QW, width), lambda b, j: (b, j, 4)),
                  pl.BlockSpec((1, seq, width), lambda b, j: (b, 0, 0)),
                  pl.BlockSpec((1, seq, width), lambda b, j: (b, 0, 1)),
                  pl.BlockSpec((1, 1, n_tiles, LANES, QW), lambda b, j: (b, j, 0, 0, 0)),
                  pl.BlockSpec(bias_tiles_t.shape, lambda b, j: (0, 0, 0, 0))],
        out_specs=pl.BlockSpec((1, width, QW), lambda b, j: (b, 0, j)),
        out_shape=jax.ShapeDtypeStruct((bsz, width, seq), F32),
        scratch_shapes=[pltpu.VMEM((HEADS, SUB, QW), F32),
                        pltpu.VMEM((HEADS, SUB, QW), F32),
                        pltpu.VMEM((width, QW), F32)],
        compiler_params=_cparams(("parallel", "arbitrary")),
        name="attn_prompt_t",
    )(bias_far, z3, kv_bf, kv_bf, mask_t, bias_tiles_t)


def _indexer_sample_body(pt_ref, qi_ref, wcol_ref, kinew_ref, *rest, n_pages):
    ki_refs = rest[:n_pages]
    out_ref = rest[n_pages]
    qi = qi_ref[0].astype(BF)
    w = wcol_ref[0] * (IDX_DIM ** -0.5 * IDX_HEADS ** -0.5)
    for i in range(n_pages):
        s = _dot(qi, ki_refs[i][0].astype(BF))
        out_ref[i, 0] = jnp.sum(w * jnp.maximum(s, 0.0), axis=0, keepdims=True)
    kn = kinew_ref[0].astype(BF).astype(F32)
    sn = jnp.sum(qi.astype(F32) * kn, axis=-1, keepdims=True)
    new = jnp.sum(w * jnp.maximum(sn, 0.0), axis=0, keepdims=True)
    lane = lax.broadcasted_iota(I32, (1, LANES), 1)
    out_ref[n_pages, 0] = jnp.where(lane == 0, new, -jnp.inf)


def indexer_sample(page_table, qi3, wcol, ki_new, ki_pool_t):
    bsz, n_pages = page_table.shape
    ki_spec = lambda i: pl.BlockSpec((1, IDX_DIM, PAGE), lambda b, pt, i=i: (pt[b, i], 0, 0))
    grid_spec = pltpu.PrefetchScalarGridSpec(
        num_scalar_prefetch=1,
        grid=(bsz,),
        in_specs=[pl.BlockSpec((1, IDX_HEADS, IDX_DIM), lambda b, pt: (b, 0, 0)),
                  pl.BlockSpec((1, IDX_HEADS, 1), lambda b, pt: (b, 0, 0)),
                  pl.BlockSpec((1, 1, IDX_DIM), lambda b, pt: (b, 0, 0))]
                 + [ki_spec(i) for i in range(n_pages)],
        out_specs=pl.BlockSpec((n_pages + 1, 1, 1, LANES), lambda b, pt: (0, b, 0, 0)),
    )
    out = pl.pallas_call(
        functools.partial(_indexer_sample_body, n_pages=n_pages),
        grid_spec=grid_spec,
        out_shape=jax.ShapeDtypeStruct((n_pages + 1, bsz, 1, LANES), F32),
        compiler_params=_cparams(("arbitrary",)),
        name="indexer_sample",
    )(page_table, qi3, wcol, ki_new, *([ki_pool_t] * n_pages))
    return out.reshape(n_pages + 1, bsz, LANES)


def _select_body(score_ref, mask_ref, u_ref, *, n_tiles, k):
    def write_tile(t, m):
        mask_ref[t] = m

    _topk_mask(score_ref, u_ref, write_tile, n_tiles, n_tiles, 1, k)


def select_topk(scores, k):
    n_tiles, rows, _ = scores.shape
    return pl.pallas_call(
        functools.partial(_select_body, n_tiles=n_tiles, k=k),
        grid=(1,),
        in_specs=[pl.BlockSpec(scores.shape, lambda i: (0, 0, 0))],
        out_specs=pl.BlockSpec(scores.shape, lambda i: (0, 0, 0)),
        out_shape=jax.ShapeDtypeStruct(scores.shape, F32),
        scratch_shapes=[pltpu.VMEM(scores.shape, I32)],
        compiler_params=_cparams(("arbitrary",)),
        name="select_topk",
    )(scores)


SC_CORES = 2
SC_SUBCORES = 16
SC_GATHER_CHUNK = 32


def sc_gather_rows(tables, idx):
    n_rows = idx.shape[0]
    workers = SC_CORES * SC_SUBCORES
    per_worker = n_rows // workers
    chunk = SC_GATHER_CHUNK
    assert n_rows % workers == 0 and per_worker % chunk == 0 and chunk % 8 == 0
    row_shape = tables[0].shape[1:]
    n_tab = len(tables)
    mesh = plsc.VectorSubcoreMesh(core_axis_name="c", subcore_axis_name="s",
                                  num_cores=SC_CORES, num_subcores=SC_SUBCORES)

    def body(*refs):
        tab_refs = refs[:n_tab]
        idx_hbm = refs[n_tab]
        out_refs = refs[n_tab + 1:2 * n_tab + 1]
        idx_v, rows_v, sem = refs[2 * n_tab + 1:]
        wid = lax.axis_index("s") * SC_CORES + lax.axis_index("c")

        @pl.loop(0, per_worker // chunk)
        def _(ci):
            off = pl.multiple_of(wid * per_worker + ci * chunk, 8)
            pltpu.sync_copy(idx_hbm.at[pl.ds(off, chunk)], idx_v)
            for tab, out in zip(tab_refs, out_refs):
                pltpu.async_copy(tab.at[idx_v], rows_v, sem).wait()
                pltpu.sync_copy(rows_v, out.at[pl.ds(off, chunk)])

    return pl.kernel(
        body,
        out_type=[jax.ShapeDtypeStruct((n_rows, *row_shape), t.dtype) for t in tables],
        mesh=mesh,
        scratch_types=[pltpu.VMEM((chunk,), I32),
                       pltpu.VMEM((chunk, *row_shape), tables[0].dtype),
                       pltpu.SemaphoreType.DMA],
        compiler_params=pltpu.CompilerParams(use_tc_tiling_on_sc=True),
        name="sc_gather_rows",
    )(*tables, idx)


SAMPLE_SEQ_GROUP = 8


def _attn_compact_body(dist_ref, near_ref, q_ref, kn_ref, vn_ref, btab_ref, kc_ref, vc_ref,
                       o_ref, bbuf):
    step = pl.program_id(0)
    scale = HEAD_DIM ** -0.5
    ones = jnp.ones((HEAD_DIM, LANES), BF)
    far_bias = btab_ref[REL_MAX_DIST]
    last = lax.broadcasted_iota(I32, (TOPK, 1, 1), 0) == TOPK - 1
    for g in range(SAMPLE_SEQ_GROUP):
        b = step * SAMPLE_SEQ_GROUP + g
        bbuf[g] = jnp.broadcast_to(far_bias[None], (TOPK, HEADS, LANES))

        def fill(t, carry, g=g, b=b):
            bbuf[g, t] = btab_ref[dist_ref[b, t]]
            return carry

        lax.fori_loop(near_ref[b], TOPK, fill, 0)

        is_new = dist_ref[b, TOPK - 1] == 0
        rows = slice(g * TOPK, (g + 1) * TOPK)
        rnd = lambda a: a.astype(BF).astype(F32)
        kc = rnd(jnp.where(last & is_new, kn_ref[g][None], kc_ref[rows]))
        vc = rnd(jnp.where(last & is_new, vn_ref[g][None], vc_ref[rows]))
        prod = (kc * rnd(q_ref[g])[None]).reshape(TOPK * HEADS, HEAD_DIM)
        hi = prod.astype(BF)
        lo = (prod - hi.astype(F32)).astype(BF)
        lg = (_dot(hi, ones) + _dot(lo, ones)).reshape(TOPK, HEADS, LANES) * scale + bbuf[g]
        m = jnp.max(lg, axis=0, keepdims=True)
        p = jnp.exp(lg - m)
        l = jnp.sum(p, axis=0)
        o_ref[g] = jnp.sum(rnd(p) * vc, axis=0) / l


def attn_sample_compact(dist, first_near, zs3, k_sel, v_sel, bias_by_dist):
    bsz = dist.shape[0]
    g = SAMPLE_SEQ_GROUP
    assert bsz % g == 0
    zspec = lambda cb: pl.BlockSpec((g, HEADS, HEAD_DIM), lambda i, d, n, cb=cb: (i, cb, 0))
    sel_spec = pl.BlockSpec((g * TOPK, HEADS, HEAD_DIM), lambda i, d, n: (i, 0, 0))
    grid_spec = pltpu.PrefetchScalarGridSpec(
        num_scalar_prefetch=2,
        grid=(bsz // g,),
        in_specs=[zspec(4), zspec(5), zspec(6),
                  pl.BlockSpec(bias_by_dist.shape, lambda i, d, n: (0, 0, 0)),
                  sel_spec, sel_spec],
        out_specs=pl.BlockSpec((g, HEADS, HEAD_DIM), lambda i, d, n: (i, 0, 0)),
        scratch_shapes=[pltpu.VMEM((g, TOPK, HEADS, HEAD_DIM), F32)],
    )
    return pl.pallas_call(
        _attn_compact_body,
        grid_spec=grid_spec,
        out_shape=jax.ShapeDtypeStruct((bsz, HEADS, HEAD_DIM), F32),
        compiler_params=_cparams(("arbitrary",)),
        name="attn_sample_compact",
    )(dist, first_near, zs3, zs3, zs3, bias_by_dist, k_sel, v_sel)


def _bucket_table(max_dist):
    exact = REL_BUCKETS // 2
    d = np.arange(max_dist + 1)
    df = np.maximum(d, 1).astype(np.float32)
    far = exact + (np.log(df / exact) / np.float32(math.log(REL_MAX_DIST / exact))
                   * (REL_BUCKETS - exact)).astype(np.int32)
    return np.where(d < exact, d, np.minimum(far, REL_BUCKETS - 1)).astype(np.int32)


def _bias_tables(rel_bias):
    tab = _bucket_table(2 * LANES)
    assert np.all(tab[REL_MAX_DIST:] == REL_BUCKETS - 1)
    i = np.arange(LANES)
    dist0 = np.maximum(i[:, None] - i[None, :], 0)
    dist1 = LANES + i[:, None] - i[None, :]
    far = np.full((LANES, LANES), REL_BUCKETS - 1)
    idx = np.stack([tab[dist0], tab[dist1], far])
    buckets = jnp.arange(REL_BUCKETS)
    lookup = lambda ix, spec: jnp.einsum(spec, (jnp.asarray(ix)[..., None] == buckets).astype(F32),
                                         rel_bias.astype(F32), precision=lax.Precision.HIGHEST)
    tiles = lookup(idx, "tijb,bh->thij")
    by_dist = jnp.broadcast_to(lookup(tab[:REL_MAX_DIST + 1], "db,bh->dh")[:, :, None],
                               (REL_MAX_DIST + 1, HEADS, LANES))
    return tiles.astype(F32), rel_bias[REL_BUCKETS - 1].astype(F32), by_dist.astype(F32)


def _conv_prompt_body(bg_ref, cg_ref, xt_ref, w_ref, v_ref, st_ref, carry_ref, *, tblock):
    t = pl.program_id(1)

    @pl.when(t == 0)
    def _():
        carry_ref[...] = jnp.zeros_like(carry_ref)

    u = cg_ref[0] * xt_ref[0]
    row = lax.broadcasted_iota(I32, u.shape, 0)
    c0 = carry_ref[0:1]
    c1 = carry_ref[1:2]
    u1 = jnp.where(row == 0, c1, pltpu.roll(u, 1, axis=0))
    u2 = jnp.where(row == 0, c0, jnp.where(row == 1, c1, pltpu.roll(u, 2, axis=0)))
    conv = w_ref[0:1] * u2 + w_ref[1:2] * u1 + w_ref[2:3] * u
    v_ref[0] = bg_ref[0] * conv
    last = u[tblock - 2:tblock]
    carry_ref[0:2] = last

    @pl.when(t == pl.num_programs(1) - 1)
    def _():
        st_ref[0] = last


def conv_prompt(zc3, w_conv, tblock=256):
    bsz, seq = zc3.shape[:2]
    c = zc3.shape[2] // 3
    zspec = lambda cb: pl.BlockSpec((1, tblock, c), lambda b, t, cb=cb: (b, t, cb))
    return pl.pallas_call(
        functools.partial(_conv_prompt_body, tblock=tblock),
        grid=(bsz, seq // tblock),
        in_specs=[zspec(0), zspec(1), zspec(2), pl.BlockSpec((3, c), lambda b, t: (0, 0))],
        out_specs=[pl.BlockSpec((1, tblock, c), lambda b, t: (b, t, 0)),
                   pl.BlockSpec((1, 2, c), lambda b, t: (b, 0, 0))],
        out_shape=[jax.ShapeDtypeStruct((bsz, seq, c), F32),
                   jax.ShapeDtypeStruct((bsz, 2, c), F32)],
        scratch_shapes=[pltpu.VMEM((8, c), F32)],
        compiler_params=_cparams(("parallel", "arbitrary")),
        name="conv_prompt",
    )(zc3, zc3, zc3, w_conv)


def _conv_sample_body(bg_ref, cg_ref, xt_ref, w_ref, s0_ref, s1_ref, v_ref, n0_ref, n1_ref):
    u = cg_ref[...] * xt_ref[...]
    conv = w_ref[0:1] * s0_ref[...] + w_ref[1:2] * s1_ref[...] + w_ref[2:3] * u
    v_ref[...] = bg_ref[...] * conv
    n0_ref[...] = s1_ref[...]
    n1_ref[...] = u


def conv_sample(zc, w_conv, s0, s1):
    bsz = zc.shape[0]
    c = zc.shape[1] // 3
    zspec = lambda cb: pl.BlockSpec((bsz, c), lambda i, cb=cb: (0, cb))
    full = pl.BlockSpec((bsz, c), lambda i: (0, 0))
    return pl.pallas_call(
        _conv_sample_body,
        grid=(1,),
        in_specs=[zspec(0), zspec(1), zspec(2), pl.BlockSpec((3, c), lambda i: (0, 0)), full, full],
        out_specs=[full, full, full],
        out_shape=[jax.ShapeDtypeStruct((bsz, c), F32)] * 3,
        compiler_params=_cparams(("arbitrary",)),
        name="conv_sample",
    )(zc, zc, zc, w_conv, s0, s1)


def _router_body(x_ref, g_ref, wr_ref, br_ref, *rest):
    h_ref, route_ref = rest[-2:]
    x = x_ref[...]
    ms = jnp.mean(x * x, axis=-1, keepdims=True)
    h = x * lax.rsqrt(ms + RMS_EPS) * g_ref[...]
    hb = h.astype(BF)
    half = h.shape[1] // 2
    bits = pltpu.bitcast(hb.astype(F32), jnp.uint32)
    h_ref[...] = (bits[:, :half] & jnp.uint32(0xFFFF0000)) | (bits[:, half:] >> 16)
    logits = _dot(hb, wr_ref[...]) + br_ref[...]
    lane = lax.broadcasted_iota(I32, logits.shape, 1)
    big = np.int32(1 << 20)
    lg = jnp.where(lane < N_GROUPS, logits, -jnp.inf)
    g_max = jnp.max(lg, axis=-1, keepdims=True)
    g_idx = jnp.min(jnp.where(lg == g_max, lane, big), axis=-1, keepdims=True)
    g_w = 1.0 / jnp.sum(jnp.exp(lg - g_max), axis=-1, keepdims=True)
    first = N_GROUPS + EXP_PER_GROUP * g_idx
    le = jnp.where((lane >= first) & (lane < first + EXP_PER_GROUP), logits, -jnp.inf)
    l1 = jnp.max(le, axis=-1, keepdims=True)
    i1 = jnp.min(jnp.where(le == l1, lane, big), axis=-1, keepdims=True)
    le2 = jnp.where(lane == i1, -jnp.inf, le)
    l2 = jnp.max(le2, axis=-1, keepdims=True)
    i2 = jnp.min(jnp.where(le2 == l2, lane, big), axis=-1, keepdims=True)
    r = jnp.exp(l2 - l1)
    w1 = g_w / (1.0 + r)
    w2 = g_w * r / (1.0 + r)
    e1 = (i1 - N_GROUPS).astype(F32)
    e2 = (i2 - N_GROUPS).astype(F32)
    route_ref[...] = jnp.where(lane == 0, e1, jnp.where(lane == 1, e2,
                               jnp.where(lane == 2, w1, jnp.where(lane == 3, w2, 0.0))))


def moe_router(x, g, wr, br, tm, total_rows, row0, bufs=None):
    m, k = x.shape
    assert row0 % tm == 0
    off = row0 // tm
    in_specs = [pl.BlockSpec((tm, k), lambda i: (i, 0)),
                pl.BlockSpec((1, k), lambda i: (0, 0)),
                pl.BlockSpec((k, LANES), lambda i: (0, 0)),
                pl.BlockSpec((1, LANES), lambda i: (0, 0))]
    args = [x, g.reshape(1, k), wr, br]
    aliases = {}
    if bufs is not None:
        in_specs += [pl.BlockSpec(memory_space=pl.ANY)] * 2
        args += list(bufs)
        aliases = {4: 0, 5: 1}
    return pl.pallas_call(
        _router_body,
        grid=(m // tm,),
        in_specs=in_specs,
        out_specs=[pl.BlockSpec((tm, k // 2), lambda i: (i + off, 0)),
                   pl.BlockSpec((tm, LANES), lambda i: (i + off, 0))],
        out_shape=[jax.ShapeDtypeStruct((total_rows, k // 2), jnp.uint32),
                   jax.ShapeDtypeStruct((total_rows, LANES), F32)],
        input_output_aliases=aliases,
        compiler_params=_cparams(("parallel",)),
        name="moe_router",
    )(*args)


def _experts_body(te_ref, nu_ref, h_ref, wg_ref, wu_ref, wd_ref, o_ref, wg_bf, wu_bf, wd_bf):
    i = pl.program_id(0)

    @pl.when((i == 0) | (te_ref[i] != te_ref[jnp.maximum(i - 1, 0)]))
    def _():
        wg_bf[...] = wg_ref[0].astype(BF)
        wu_bf[...] = wu_ref[0].astype(BF)
        wd_bf[...] = wd_ref[0].astype(BF)

    @pl.when(i < nu_ref[0])
    def _():
        words = h_ref[...]
        left = pltpu.bitcast(words & jnp.uint32(0xFFFF0000), F32)
        right = pltpu.bitcast(words << 16, F32)
        h = jnp.concatenate([left, right], axis=1).astype(BF)
        a = _dot(h, wg_bf[...])
        b = _dot(h, wu_bf[...])
        hid = a * _sigmoid(a) * b
        o_ref[...] = _dot(hid.astype(BF), wd_bf[...])

    @pl.when(i >= nu_ref[0])
    def _():
        o_ref[...] = jnp.zeros_like(o_ref)


def moe_experts(tile_expert, n_used, hs, wg, wu, wd):
    p = hs.shape[0]
    k = wg.shape[1]
    f = wg.shape[2]
    n_tiles = p // MOE_TILE
    grid_spec = pltpu.PrefetchScalarGridSpec(
        num_scalar_prefetch=2,
        grid=(n_tiles,),
        in_specs=[pl.BlockSpec((MOE_TILE, k // 2), lambda i, te, nu: (i, 0)),
                  pl.BlockSpec((1, k, f), lambda i, te, nu: (te[i], 0, 0)),
                  pl.BlockSpec((1, k, f), lambda i, te, nu: (te[i], 0, 0)),
                  pl.BlockSpec((1, f, k), lambda i, te, nu: (te[i], 0, 0))],
        out_specs=pl.BlockSpec((MOE_TILE, k), lambda i, te, nu: (i, 0)),
        scratch_shapes=[pltpu.VMEM((k, f), BF), pltpu.VMEM((k, f), BF), pltpu.VMEM((f, k), BF)],
    )
    return pl.pallas_call(
        _experts_body,
        grid_spec=grid_spec,
        out_shape=jax.ShapeDtypeStruct((p, k), F32),
        compiler_params=_cparams(("arbitrary",)),
        name="moe_experts",
    )(tile_expert, n_used, hs, wg, wu, wd)


def _combine_body(x_ref, a_ref, b_ref, route_ref, g_ref, o_ref, *, normalize):
    y = x_ref[...] + route_ref[:, 2:3] * a_ref[...] + route_ref[:, 3:4] * b_ref[...]
    if normalize:
        ms = jnp.mean(y * y, axis=-1, keepdims=True)
        y = y * lax.rsqrt(ms + RMS_EPS) * g_ref[...]
    o_ref[...] = y


def moe_combine(x, out_a, out_b, route, tm, norm_g=None):
    m, k = x.shape
    row = pl.BlockSpec((tm, k), lambda i: (i, 0))
    g = jnp.ones((1, k), F32) if norm_g is None else norm_g.reshape(1, k)
    return pl.pallas_call(
        functools.partial(_combine_body, normalize=norm_g is not None),
        grid=(m // tm,),
        in_specs=[row, row, row, pl.BlockSpec((tm, LANES), lambda i: (i, 0)),
                  pl.BlockSpec((1, k), lambda i: (0, 0))],
        out_specs=row,
        out_shape=jax.ShapeDtypeStruct((m, k), F32),
        compiler_params=_cparams(("parallel",)),
        name="moe_combine",
    )(x, out_a, out_b, route, g)


def _rank_within_expert(onehot):
    n, e = onehot.shape
    blk = LANES
    assert n % blk == 0
    oh = onehot.astype(F32).reshape(n // blk, blk, e)
    strict = jnp.asarray(np.tril(np.ones((blk, blk), np.float32), -1))
    within = jnp.einsum("ij,bjk->bik", strict, oh, precision=lax.Precision.HIGHEST)
    totals = jnp.sum(oh, axis=1)
    before = jnp.cumsum(totals, axis=0) - totals
    rank = (within + before[:, None, :]).reshape(n, e)
    return jnp.sum(rank * onehot.astype(F32), axis=1).astype(I32), jnp.sum(totals, axis=0).astype(I32)


def hier_moe(xs, tms, g, wrg, brg, wre, bre, wg, wu, wd, layer, out_norm_g=None):
    d = xs[0].shape[1]
    m = sum(x.shape[0] for x in xs)
    wr = jnp.zeros((d, LANES), F32)
    wr = wr.at[:, :N_GROUPS].set(wrg).at[:, N_GROUPS:N_GROUPS + N_EXPERTS].set(wre.reshape(d, N_EXPERTS))
    br = jnp.zeros((1, LANES), F32)
    br = br.at[0, :N_GROUPS].set(brg).at[0, N_GROUPS:N_GROUPS + N_EXPERTS].set(bre.reshape(N_EXPERTS))
    bufs, row0 = None, 0
    for x, tm in zip(xs, tms):
        bufs = moe_router(x, g, wr.astype(BF), br, tm, m, row0, bufs)
        row0 += x.shape[0]
    h_bf, route = bufs

    eid = route[:, 0:2].astype(I32).reshape(-1)
    onehot = eid[:, None] == jnp.arange(N_EXPERTS, dtype=I32)[None, :]
    rank, counts = _rank_within_expert(onehot)
    padded = ((counts + MOE_TILE - 1) // MOE_TILE) * MOE_TILE
    ends = jnp.cumsum(padded)
    pos = jnp.sum(jnp.where(onehot, (ends - padded)[None, :], 0), axis=1) + rank
    n_rows = 2 * m + N_EXPERTS * MOE_TILE
    n_rows = -(-n_rows // MOE_TILE) * MOE_TILE
    token = jnp.zeros((n_rows,), I32).at[pos].set(jnp.arange(2 * m, dtype=I32) // 2)
    tile_start = jnp.arange(n_rows // MOE_TILE, dtype=I32) * MOE_TILE
    tile_expert = jnp.minimum(jnp.sum(tile_start[:, None] >= ends[None, :], axis=1),
                              N_EXPERTS - 1).astype(I32)
    n_used = (ends[-1] // MOE_TILE).astype(I32).reshape(1)

    hs = jnp.take(h_bf, token, axis=0, mode="clip")
    out = moe_experts(tile_expert + layer * N_EXPERTS, n_used, hs, wg, wu, wd)
    pos2 = pos.reshape(m, 2)
    res, r0 = [], 0
    for x, tm in zip(xs, tms):
        rows = slice(r0, r0 + x.shape[0])
        res.append(moe_combine(x, jnp.take(out, pos2[rows, 0], axis=0, mode="clip"),
                               jnp.take(out, pos2[rows, 1], axis=0, mode="clip"),
                               route[rows], tm, out_norm_g))
        r0 += x.shape[0]
    return res


def kernel(x_prompt, x_sample, cache_k, cache_v, cache_ki, state_hgrn, state_conv, page_table,
           norm_mix_g, norm_ffn_g, final_g, w_in_even, w_out_even, hgrn_lb_logits, hgrn_norm_g,
           rel_bias, w_in_conv, w_conv, w_out_conv, w_router_g, b_router_g, w_router_e,
           b_router_e, w_gate, w_up, w_down):
    bsz, seq, d = x_prompt.shape
    dec = x_sample.shape[0]
    n_p = bsz * seq
    tm_p, tm_s = 512, 128
    tms = (tm_p, tm_s)
    assert n_p % tm_p == 0 and dec % tm_s == 0 and x_sample.shape[1] == 1
    width = HEADS * HEAD_DIM
    even_in = w_in_even.shape[2]
    even_pad = -(-even_in // IN_PROJ_TN) * IN_PROJ_TN
    tail0 = TAIL_BLOCK * LANES
    n_pool = cache_k.shape[1]

    xp = x_prompt.reshape(n_p, d)
    xs = x_sample.reshape(dec, d)

    lbs = jnp.cumsum(jax.nn.softmax(hgrn_lb_logits.astype(F32), axis=0), axis=0)[:-1]
    bias_tiles, bias_far, bias_by_dist = _bias_tables(rel_bias)
    expert_w = [w.reshape(-1, *w.shape[2:]) for w in (w_gate, w_up, w_down)]

    w_in = jnp.pad(w_in_even[0], ((0, 0), (0, even_pad - even_in))).astype(BF)
    zp, k_new_p, v_new_p, kv_bf = even_projection(xp, norm_mix_g[0], w_in, 2 * tm_p)
    zt = norm_matmul(xp, norm_mix_g[0], w_in[:, tail0:tail0 + LANES], 2 * tm_p, LANES)
    zs = norm_matmul(xs, norm_mix_g[0], w_in, tm_s, IN_PROJ_TN)
    zp3 = zp.reshape(bsz, seq, 6 * width)
    zt3 = zt.reshape(bsz, seq, LANES)
    zs3 = zs.reshape(dec, even_pad // LANES, LANES)

    oa_p, hgrn_p = hgrn_prompt(zp3, lbs[0], hgrn_norm_g[0])
    oa_s, hgrn_s = hgrn_sample(zs3, lbs[0], hgrn_norm_g[0], state_hgrn[0])

    wt = jnp.swapaxes(zt3[:, :, IDX_DIM:IDX_DIM + IDX_HEADS], 1, 2)
    mask_p = indexer_prompt_t(zp3, wt, zt3)
    ob_p = jnp.swapaxes(attn_prompt_t(zp3, kv_bf.reshape(bsz, seq, 2 * width), mask_p,
                                      jnp.swapaxes(bias_tiles, -1, -2), bias_far), 1, 2)

    qi3 = zs[:, 7 * width:8 * width].reshape(dec, IDX_HEADS, IDX_DIM)
    ki_new = zs[:, tail0:tail0 + IDX_DIM].reshape(dec, 1, IDX_DIM)
    wcol = zs[:, tail0 + IDX_DIM:tail0 + IDX_DIM + IDX_HEADS].reshape(dec, IDX_HEADS, 1)
    pages = page_table + 0 * n_pool
    scores_s = indexer_sample(pages, qi3, wcol, ki_new,
                              jnp.swapaxes(cache_ki, -1, -2).reshape(-1, IDX_DIM, PAGE))
    mask_s = select_topk(scores_s, TOPK)
    sel = jnp.transpose(mask_s, (1, 0, 2))
    n_pages = page_table.shape[1]
    sel_off = jnp.concatenate([jnp.zeros((dec, 1), F32),
                               jnp.cumsum(jnp.sum(sel, axis=2), axis=1)], axis=1)
    slot = jnp.arange(TOPK, dtype=F32)
    page_of = jnp.sum(sel_off[:, None, 1:] <= slot[None, :, None], axis=2)
    page_1h = (page_of[:, :, None] == jnp.arange(n_pages + 1)[None, None, :]).astype(F32)
    local = slot[None, :] - jnp.einsum("brp,bp->br", page_1h, sel_off[:, :-1],
                                       precision=lax.Precision.HIGHEST)
    within = jnp.einsum("brp,bpl->brl", page_1h, jnp.cumsum(sel, axis=2),
                        precision=lax.Precision.HIGHEST)
    lane_of = jnp.sum(within <= local[:, :, None], axis=2)
    sel_idx = (page_of * PAGE + lane_of).astype(I32)
    past = n_pages * PAGE
    page_id = jnp.einsum("brp,bp->br", page_1h[:, :, :n_pages], pages.astype(F32),
                         precision=lax.Precision.HIGHEST).astype(I32)
    rows = jnp.where(page_of < n_pages, page_id * PAGE + lane_of, 0).astype(I32).reshape(-1)
    k_sel, v_sel = sc_gather_rows([cache_k.reshape(-1, HEADS, HEAD_DIM),
                                   cache_v.reshape(-1, HEADS, HEAD_DIM)], rows)
    dist = jnp.minimum(past - sel_idx, REL_MAX_DIST).astype(I32)
    first_near = jnp.sum(dist >= REL_MAX_DIST, axis=1).astype(I32)
    ob_s = attn_sample_compact(dist, first_near, zs3, k_sel, v_sel, bias_by_dist)

    w_out = w_out_even[0].astype(BF)
    w_out_ab = [w_out[:width], w_out[width:]]
    xp = matmul_residual([oa_p.reshape(n_p, width), ob_p.reshape(n_p, width)], w_out_ab, xp, 2 * tm_p, 1024)
    xs = matmul_residual([oa_s.reshape(dec, width), ob_s.reshape(dec, width)], w_out_ab, xs, tm_s, 512)

    xp, xs = hier_moe([xp, xs], tms, norm_ffn_g[0], w_router_g[0], b_router_g[0], w_router_e[0],
                      b_router_e[0], *expert_w, 0)

    w_in_c = w_in_conv[0].astype(BF)
    zcp = norm_matmul(xp, norm_mix_g[1], w_in_c, 2 * tm_p, 1024)
    zcs = norm_matmul(xs, norm_mix_g[1], w_in_c, tm_s, 512)
    cw = zcp.shape[1] // 3
    v_p, conv_p = conv_prompt(zcp.reshape(bsz, seq, 3 * cw), w_conv[0])
    v_s, cs0, cs1 = conv_sample(zcs, w_conv[0], state_conv[0, :, 0], state_conv[0, :, 1])
    w_out_c = [w_out_conv[0].astype(BF)]
    xp = matmul_residual([v_p.reshape(n_p, cw)], w_out_c, xp, 2 * tm_p, 1024)
    xs = matmul_residual([v_s], w_out_c, xs, tm_s, 512)

    yp, ys = hier_moe([xp, xs], tms, norm_ffn_g[1], w_router_g[1], b_router_g[1], w_router_e[1],
                      b_router_e[1], *expert_w, 1, out_norm_g=final_g)

    kcol, vcol = 5 * width, 6 * width
    heads = lambda a, n: a.reshape(1, *n, HEADS, HEAD_DIM)
    return (yp.reshape(bsz, seq, d),
            ys.reshape(dec, 1, d),
            heads(k_new_p, (bsz, seq)),
            heads(v_new_p, (bsz, seq)),
            zt[:, :IDX_DIM].reshape(1, bsz, seq, IDX_DIM),
            hgrn_p[None],
            conv_p[None],
            heads(zs[:, kcol:kcol + width], (dec, 1)),
            heads(zs[:, vcol:vcol + width], (dec, 1)),
            zs[:, tail0:tail0 + IDX_DIM].reshape(1, dec, 1, IDX_DIM),
            hgrn_s[None],
            jnp.stack([cs0, cs1], axis=1)[None])
```

```python
import functools
import math

import numpy as np
import jax
import jax.numpy as jnp
from jax import lax
from jax.experimental import pallas as pl
from jax.experimental.pallas import tpu as pltpu
from jax.experimental.pallas import tpu_sc as plsc

F32 = jnp.float32
BF = jnp.bfloat16
I32 = jnp.int32

RMS_EPS = 1e-6
LANES = 128
NEG_BIG = -1e30
VMEM_LIMIT = 56 * 1024 * 1024

D_MODEL = 2048
HEADS = 8
HEAD_DIM = 128
IDX_HEADS = 16
IDX_DIM = 64
TOPK = 256
REL_BUCKETS = 32
REL_MAX_DIST = 128
N_EXPERTS = 16
EXP_PER_GROUP = 4
N_GROUPS = 4
D_EXPERT = 512
PAGE = 128

HGRN_CHUNK = 128
HGRN_HEAD_GROUP = 8
MOE_TILE = 256
IN_PROJ_TN = 640
TAIL_BLOCK = 8 * HEADS * HEAD_DIM // LANES


def _cparams(sem):
    return pltpu.CompilerParams(dimension_semantics=sem, vmem_limit_bytes=VMEM_LIMIT)


def _dot(a, b):
    return jnp.dot(a, b, preferred_element_type=F32)


def _dot_nt(a, b):
    return lax.dot_general(a, b, (((1,), (1,)), ((), ())), preferred_element_type=F32)


def _dot_tn(a, b):
    return lax.dot_general(a, b, (((0,), (0,)), ((), ())), preferred_element_type=F32)


def _sigmoid(x):
    return 1.0 / (1.0 + jnp.exp(-x))


def _norm_mm_body(x_ref, g_ref, w_ref, o_ref, h_ref):
    @pl.when(pl.program_id(1) == 0)
    def _():
        x = x_ref[...]
        ms = jnp.mean(x * x, axis=-1, keepdims=True)
        h_ref[...] = (x * lax.rsqrt(ms + RMS_EPS) * g_ref[...]).astype(BF)

    o_ref[...] = _dot(h_ref[...], w_ref[...])


def norm_matmul(x, g, w_bf, tm, tn):
    m, k = x.shape
    n = w_bf.shape[1]
    return pl.pallas_call(
        _norm_mm_body,
        grid=(m // tm, n // tn),
        in_specs=[pl.BlockSpec((tm, k), lambda i, j: (i, 0)),
                  pl.BlockSpec((1, k), lambda i, j: (0, 0)),
                  pl.BlockSpec((k, tn), lambda i, j: (0, j))],
        out_specs=pl.BlockSpec((tm, tn), lambda i, j: (i, j)),
        out_shape=jax.ShapeDtypeStruct((m, n), F32),
        scratch_shapes=[pltpu.VMEM((tm, k), BF)],
        compiler_params=_cparams(("parallel", "arbitrary")),
        name="norm_matmul",
    )(x, g.reshape(1, k), w_bf)


EVEN_TN = 1024


def _even_proj_body(x_ref, g_ref, w_ref, z_ref, k_ref, v_ref, kv_ref, h_ref, *, per):
    j = pl.program_id(1)

    @pl.when(j == 0)
    def _():
        x = x_ref[...]
        ms = jnp.mean(x * x, axis=-1, keepdims=True)
        h_ref[...] = (x * lax.rsqrt(ms + RMS_EPS) * g_ref[...]).astype(BF)

    res = _dot(h_ref[...], w_ref[...])
    sec = j // per

    @pl.when((sec < 5) | (sec == 7))
    def _():
        z_ref[...] = res

    @pl.when(sec == 5)
    def _():
        k_ref[...] = res
        kv_ref[...] = res.astype(BF)

    @pl.when(sec == 6)
    def _():
        v_ref[...] = res
        kv_ref[...] = res.astype(BF)


def even_projection(x, g, w_bf, tm):
    m, k = x.shape
    width = HEADS * HEAD_DIM
    per = width // EVEN_TN
    clip = lambda a, lo, hi: jnp.minimum(jnp.maximum(a, lo), hi)
    return pl.pallas_call(
        functools.partial(_even_proj_body, per=per),
        grid=(m // tm, 8 * per),
        in_specs=[pl.BlockSpec((tm, k), lambda i, j: (i, 0)),
                  pl.BlockSpec((1, k), lambda i, j: (0, 0)),
                  pl.BlockSpec((k, EVEN_TN), lambda i, j: (0, j))],
        out_specs=[pl.BlockSpec((tm, EVEN_TN), lambda i, j: (i, j - clip(j - (5 * per - 1), 0, 2 * per))),
                   pl.BlockSpec((tm, EVEN_TN), lambda i, j: (i, clip(j - 5 * per, 0, per - 1)),
                                pipeline_mode=pl.Buffered(1)),
                   pl.BlockSpec((tm, EVEN_TN), lambda i, j: (i, clip(j - 6 * per, 0, per - 1)),
                                pipeline_mode=pl.Buffered(1)),
                   pl.BlockSpec((tm, EVEN_TN), lambda i, j: (i, clip(j - 5 * per, 0, 2 * per - 1)),
                                pipeline_mode=pl.Buffered(1))],
        out_shape=[jax.ShapeDtypeStruct((m, 6 * width), F32),
                   jax.ShapeDtypeStruct((m, width), F32),
                   jax.ShapeDtypeStruct((m, width), F32),
                   jax.ShapeDtypeStruct((m, 2 * width), BF)],
        scratch_shapes=[pltpu.VMEM((tm, k), BF)],
        compiler_params=_cparams(("parallel", "arbitrary")),
        name="even_projection",
    )(x, g.reshape(1, k), w_bf)


def _mm_res_body(*refs, n_lhs):
    a_refs = refs[:n_lhs]
    w_refs = refs[n_lhs:2 * n_lhs]
    r_ref = refs[2 * n_lhs]
    o_ref = refs[2 * n_lhs + 1]
    s_refs = refs[2 * n_lhs + 2:]

    @pl.when(pl.program_id(1) == 0)
    def _():
        for a_ref, s_ref in zip(a_refs, s_refs):
            s_ref[...] = a_ref[...].astype(BF)

    acc = r_ref[...]
    for s_ref, w_ref in zip(s_refs, w_refs):
        acc = acc + _dot(s_ref[...], w_ref[...])
    o_ref[...] = acc


def matmul_residual(lhs, ws_bf, res, tm, tn):
    m, n = res.shape
    n_lhs = len(lhs)
    in_specs = ([pl.BlockSpec((tm, a.shape[1]), lambda i, j: (i, 0)) for a in lhs]
                + [pl.BlockSpec((w.shape[0], tn), lambda i, j: (0, j)) for w in ws_bf]
                + [pl.BlockSpec((tm, tn), lambda i, j: (i, j))])
    return pl.pallas_call(
        functools.partial(_mm_res_body, n_lhs=n_lhs),
        grid=(m // tm, n // tn),
        in_specs=in_specs,
        out_specs=pl.BlockSpec((tm, tn), lambda i, j: (i, j)),
        out_shape=jax.ShapeDtypeStruct((m, n), F32),
        scratch_shapes=[pltpu.VMEM((tm, a.shape[1]), BF) for a in lhs],
        compiler_params=_cparams(("parallel", "arbitrary")),
        name="matmul_residual",
    )(*lhs, *ws_bf, res)


def _hgrn_static(c):
    levels = []
    m = 1
    while m < c:
        levels.append(m)
        m *= 2
    t = np.arange(c)
    rows = [t[None, :] <= t[:, None]]
    masks = [np.eye(c, dtype=bool)]
    for m in levels:
        blk = t // (2 * m)
        pos = t % (2 * m)
        bnd = blk * 2 * m + m - 1
        right = pos >= m
        left = pos < m
        e_rows = (t[None, :] > bnd[:, None]) & (t[None, :] <= t[:, None]) & right[:, None]
        f_rows = (t[None, :] > t[:, None]) & (t[None, :] <= bnd[:, None]) & left[:, None]
        rows.append(e_rows | f_rows)
        masks.append((blk[:, None] == blk[None, :]) & right[:, None] & left[None, :])
    m_all = np.stack(rows).astype(np.float32)
    masks = np.stack(masks).astype(np.float32)
    return m_all, masks, len(levels)


def _hgrn_gates(qa, fa, lb):
    f = lb + (1.0 - lb) * _sigmoid(fa)
    q = qa * _sigmoid(qa)
    return q, f


def _hgrn_prompt_body(qa_ref, fa_ref, ia_ref, ga_ref, lb_ref, gn_ref, mall_ref, masks_ref,
                      oa_ref, st_out_ref, st_ref, *, chunk, tblock, n_levels):
    c = chunk
    t_idx = pl.program_id(1)

    @pl.when(t_idx == 0)
    def _():
        st_ref[...] = jnp.zeros_like(st_ref)

    gn = gn_ref[...]

    def chunk_step(ci, carry):
        r0 = pl.multiple_of(ci * c, c)
        for h0 in range(0, HEADS, HGRN_HEAD_GROUP):
            hs = range(h0, h0 + HGRN_HEAD_GROUP)
            cols = {h: slice(h * HEAD_DIM, (h + 1) * HEAD_DIM) for h in hs}
            q, k, g_hi, g_lo, v = {}, {}, {}, {}, {}
            for h in hs:
                q[h], f = _hgrn_gates(qa_ref[0, pl.ds(r0, c), cols[h]],
                                      fa_ref[0, pl.ds(r0, c), cols[h]], lb_ref[:, cols[h]])
                k[h] = 1.0 - f
                g = jnp.log(f)
                g_hi[h] = g.astype(BF)
                g_lo[h] = (g - g_hi[h].astype(F32)).astype(BF)
                v[h] = ia_ref[0, pl.ds(r0, c), cols[h]].astype(BF)
            b = {h: _dot(mall_ref[0], g_hi[h]) + _dot(mall_ref[0], g_lo[h]) for h in hs}
            o = {h: _dot_nt((q[h] * jnp.exp(b[h])).astype(BF), st_ref[h].astype(BF)) for h in hs}
            a = {h: masks_ref[0] * _dot_nt(q[h].astype(BF), k[h].astype(BF)) for h in hs}
            for li in range(n_levels):
                ml = mall_ref[1 + li]
                w = {h: jnp.exp(_dot(ml, g_hi[h]) + _dot(ml, g_lo[h])) for h in hs}
                for h in hs:
                    a[h] = a[h] + masks_ref[1 + li] * _dot_nt((q[h] * w[h]).astype(BF),
                                                              (k[h] * w[h]).astype(BF))
            for h in hs:
                o[h] = o[h] + _dot(a[h].astype(BF), v[h])
            for h in hs:
                b_last = b[h][c - 1:c]
                k_st = (k[h] * jnp.exp(b_last - b[h])).astype(BF)
                st_ref[h] = st_ref[h] * jnp.exp(b_last) + _dot_tn(v[h], k_st)
            for h in hs:
                ga = ga_ref[0, pl.ds(r0, c), cols[h]]
                ms = jnp.mean(o[h] * o[h], axis=-1, keepdims=True)
                oa_ref[0, pl.ds(r0, c), cols[h]] = (o[h] * lax.rsqrt(ms + RMS_EPS) * gn
                                                    * (ga * _sigmoid(ga)))
        return carry

    lax.fori_loop(0, tblock // c, chunk_step, 0)

    @pl.when(t_idx == pl.num_programs(1) - 1)
    def _():
        st_out_ref[0] = st_ref[...]


def hgrn_prompt(z3, lb, gn, tblock=256, chunk=HGRN_CHUNK):
    bsz, seq = z3.shape[:2]
    width = HEADS * HEAD_DIM
    m_all, masks, n_levels = _hgrn_static(chunk)
    zspec = lambda cb: pl.BlockSpec((1, tblock, width), lambda b, t, cb=cb: (b, t, cb))
    oa, st = pl.pallas_call(
        functools.partial(_hgrn_prompt_body, chunk=chunk, tblock=tblock, n_levels=n_levels),
        grid=(bsz, seq // tblock),
        in_specs=[zspec(0), zspec(1), zspec(2), zspec(3),
                  pl.BlockSpec((1, width), lambda b, t: (0, 0)),
                  pl.BlockSpec((1, HEAD_DIM), lambda b, t: (0, 0)),
                  pl.BlockSpec(m_all.shape, lambda b, t: (0, 0, 0)),
                  pl.BlockSpec(masks.shape, lambda b, t: (0, 0, 0))],
        out_specs=[pl.BlockSpec((1, tblock, width), lambda b, t: (b, t, 0)),
                   pl.BlockSpec((1, HEADS, HEAD_DIM, HEAD_DIM), lambda b, t: (b, 0, 0, 0))],
        out_shape=[jax.ShapeDtypeStruct((bsz, seq, width), F32),
                   jax.ShapeDtypeStruct((bsz, HEADS, HEAD_DIM, HEAD_DIM), F32)],
        scratch_shapes=[pltpu.VMEM((HEADS, HEAD_DIM, HEAD_DIM), F32)],
        compiler_params=_cparams(("parallel", "arbitrary")),
        name="hgrn_prompt",
    )(z3, z3, z3, z3, lb.reshape(1, width), gn.reshape(1, HEAD_DIM),
      jnp.asarray(m_all, BF), jnp.asarray(masks, F32))
    return oa, jnp.swapaxes(st, -1, -2)


def _col(row, eye):
    return jnp.sum(eye * row, axis=1, keepdims=True)


def _hgrn_sample_body(qa_ref, fa_ref, ia_ref, ga_ref, lb_ref, gn_ref, s_ref, oa_ref, so_ref):
    eye = (lax.broadcasted_iota(I32, (HEAD_DIM, HEAD_DIM), 0)
           == lax.broadcasted_iota(I32, (HEAD_DIM, HEAD_DIM), 1)).astype(F32)
    rnd = lambda a: a.astype(BF).astype(F32)
    for g in range(SAMPLE_SEQ_GROUP):
        q8, f8 = _hgrn_gates(qa_ref[g], fa_ref[g], lb_ref[...])
        ga = ga_ref[g]
        gate = ga * _sigmoid(ga)
        kr = rnd(1.0 - f8)
        vr = rnd(ia_ref[g])
        qfr = rnd(q8 * f8)
        qk = rnd(jnp.sum(rnd(q8) * kr, axis=-1, keepdims=True))
        outs = []
        for h in range(HEADS):
            f_col = _col(f8[h:h + 1], eye)
            s_old = s_ref[g, h]
            so_ref[g, h] = f_col * s_old + _col(kr[h:h + 1], eye) * vr[h:h + 1]
            outs.append(qk[h:h + 1] * vr[h:h + 1]
                        + jnp.sum(_col(qfr[h:h + 1], eye) * rnd(s_old), axis=0, keepdims=True))
        o = jnp.concatenate(outs, axis=0)
        ms = jnp.mean(o * o, axis=-1, keepdims=True)
        oa_ref[g] = o * lax.rsqrt(ms + RMS_EPS) * gn_ref[...] * gate


def hgrn_sample(zs3, lb, gn, s0):
    bsz = zs3.shape[0]
    g = SAMPLE_SEQ_GROUP
    assert bsz % g == 0
    zspec = lambda cb: pl.BlockSpec((g, HEADS, HEAD_DIM), lambda b, cb=cb: (b, cb, 0))
    sspec = pl.BlockSpec((g, HEADS, HEAD_DIM, HEAD_DIM), lambda b: (b, 0, 0, 0))
    return pl.pallas_call(
        _hgrn_sample_body,
        grid=(bsz // g,),
        in_specs=[zspec(0), zspec(1), zspec(2), zspec(3),
                  pl.BlockSpec((HEADS, HEAD_DIM), lambda b: (0, 0)),
                  pl.BlockSpec((1, HEAD_DIM), lambda b: (0, 0)),
                  sspec],
        out_specs=[pl.BlockSpec((g, HEADS, HEAD_DIM), lambda b: (b, 0, 0)), sspec],
        out_shape=[jax.ShapeDtypeStruct((bsz, HEADS, HEAD_DIM), F32),
                   jax.ShapeDtypeStruct(s0.shape, F32)],
        compiler_params=_cparams(("parallel",)),
        name="hgrn_sample",
    )(zs3, zs3, zs3, zs3, lb.reshape(HEADS, HEAD_DIM), gn.reshape(1, HEAD_DIM), s0)


_KEY_NEG_INF = np.int32(np.uint32(0x807FFFFF).astype(np.int64) - (1 << 32))
_INT_MIN = np.int32(-(1 << 31))


def _count(u_ref, n_groups, group, thr, cmp):
    rows = u_ref.shape[1]
    step = min(rows, LANES)
    parts = []
    for r0 in range(0, rows, step):
        t = jnp.broadcast_to(thr[r0:r0 + step], (step, LANES))

        def body(gi, acc, r0=r0, t=t):
            for i in range(group):
                acc = acc + cmp(u_ref[gi * group + i, r0:r0 + step], t).astype(F32)
            return acc

        acc = lax.fori_loop(0, n_groups, body, jnp.zeros((step, LANES), F32))
        parts.append(jnp.sum(acc, axis=-1, keepdims=True))
    return parts[0] if len(parts) == 1 else jnp.concatenate(parts, axis=0)


def _topk_mask(score_ref, u_ref, write_tile, n_tiles, n_groups, group, k):
    rows = score_ref.shape[1]
    n_live = n_groups * group

    def to_key(t, carry):
        bits = pltpu.bitcast(score_ref[t], I32)
        u_ref[t] = jnp.where(bits < 0, bits ^ np.int32(0x7FFFFFFF), bits)
        return carry

    lax.fori_loop(0, n_live, to_key, 0)

    kf = float(k)
    ge = lambda u, t: u >= t
    cnt = _count(u_ref, n_groups, group, jnp.zeros((rows, 1), I32), ge)
    lo = jnp.where(cnt >= kf, np.int32(0), _INT_MIN)

    def bit_step(i, lo):
        cand = lo | (np.int32(1) << (30 - i))
        cnt = _count(u_ref, n_groups, group, cand, ge)
        return jnp.where(cnt >= kf, cand, lo)

    lo = lax.fori_loop(0, 31, bit_step, lo)
    c_gt = _count(u_ref, n_groups, group, lo, lambda u, t: u > t)
    c_eq = _count(u_ref, n_groups, group, lo, lambda u, t: u == t)
    need = kf - c_gt
    real = lo > _KEY_NEG_INF
    excess = jnp.where(real & (c_eq > need), 1.0, 0.0)
    any_excess = jnp.max(excess) > 0.0

    @pl.when(jnp.logical_not(any_excess))
    def _():
        def emit(t, carry):
            u = u_ref[t]
            write_tile(t, jnp.where((u >= lo) & (u > _KEY_NEG_INF), 1.0, 0.0))
            return carry

        lax.fori_loop(0, n_live, emit, 0)

    @pl.when(any_excess)
    def _():
        upper = (lax.broadcasted_iota(I32, (LANES, LANES), 0)
                 <= lax.broadcasted_iota(I32, (LANES, LANES), 1)).astype(BF)

        def emit(t, seen):
            u = u_ref[t]
            eq = jnp.where(u == lo, 1.0, 0.0)
            prefix = seen + _dot(eq.astype(BF), upper)
            take = (u > lo) | ((u == lo) & (prefix <= need))
            write_tile(t, jnp.where(take & (u > _KEY_NEG_INF), 1.0, 0.0))
            return seen + jnp.sum(eq, axis=-1, keepdims=True)

        lax.fori_loop(0, n_live, emit, jnp.zeros((rows, 1), F32))

    def clear(t, carry):
        write_tile(t, jnp.zeros((rows, LANES), F32))
        return carry

    lax.fori_loop(n_live, n_tiles, clear, 0)


QW = 256
SUB = 8
IDX_KCHUNK = 512
ATTN_KSTEP = 512
ATTN_TILES = ATTN_KSTEP // LANES


def _row_all(x8, op):
    return jnp.broadcast_to(op(x8, axis=0, keepdims=True), x8.shape)


def _count_t(u_ref, n_groups, group, thr_row, cmp):
    qw = u_ref.shape[2]
    thr = jnp.broadcast_to(thr_row, (SUB, qw))

    def body(gi, acc):
        for i in range(group):
            u = u_ref[gi * group + i].reshape(LANES // SUB, SUB, qw)
            acc = acc + jnp.sum(cmp(u, thr[None]).astype(F32), axis=0)
        return acc

    acc = lax.fori_loop(0, n_groups, body, jnp.zeros((SUB, qw), F32))
    return jnp.sum(acc, axis=0, keepdims=True)


def _topk_mask_t(score_ref, u_ref, write_tile, n_tiles, n_groups, group, k):
    qw = score_ref.shape[2]
    n_live = n_groups * group

    def to_key(t, carry):
        bits = pltpu.bitcast(score_ref[t], I32)
        u_ref[t] = jnp.where(bits < 0, bits ^ np.int32(0x7FFFFFFF), bits)
        return carry

    lax.fori_loop(0, n_live, to_key, 0)

    kf = float(k)
    ge = lambda u, t: u >= t
    cnt = _count_t(u_ref, n_groups, group, jnp.zeros((1, qw), I32), ge)
    lo = jnp.where(cnt >= kf, np.int32(0), _INT_MIN)

    def bit_step(i, lo):
        cand = lo | (np.int32(1) << (30 - i))
        cnt = _count_t(u_ref, n_groups, group, cand, ge)
        return jnp.where(cnt >= kf, cand, lo)

    lo = lax.fori_loop(0, 31, bit_step, lo)
    c_gt = _count_t(u_ref, n_groups, group, lo, lambda u, t: u > t)
    c_eq = _count_t(u_ref, n_groups, group, lo, lambda u, t: u == t)
    need = kf - c_gt
    excess = jnp.where((lo > _KEY_NEG_INF) & (c_eq > need), 1.0, 0.0)
    any_excess = jnp.max(excess) > 0.0
    lo_b = jnp.broadcast_to(lo, (LANES, qw))

    @pl.when(jnp.logical_not(any_excess))
    def _():
        def emit(t, carry):
            u = u_ref[t]
            write_tile(t, jnp.where((u >= lo_b) & (u > _KEY_NEG_INF), 1.0, 0.0))
            return carry

        lax.fori_loop(0, n_live, emit, 0)

    @pl.when(any_excess)
    def _():
        lower = (lax.broadcasted_iota(I32, (LANES, LANES), 1)
                 <= lax.broadcasted_iota(I32, (LANES, LANES), 0)).astype(BF)

        def emit(t, seen):
            u = u_ref[t]
            eq = jnp.where(u == lo_b, 1.0, 0.0)
            prefix = seen + _dot(lower, eq.astype(BF))
            take = (u > lo_b) | ((u == lo_b) & (prefix <= need))
            write_tile(t, jnp.where(take & (u > _KEY_NEG_INF), 1.0, 0.0))
            return seen + jnp.sum(eq, axis=0, keepdims=True)

        lax.fori_loop(0, n_live, emit, jnp.zeros((1, qw), F32))

    def clear(t, carry):
        write_tile(t, jnp.zeros((LANES, qw), F32))
        return carry

    lax.fori_loop(n_live, n_tiles, clear, 0)


def _indexer_t_body(qi_ref, wt_ref, tail_ref, mask_ref, score_ref, u_ref, *, seq):
    j = pl.program_id(1)
    n_tiles = seq // LANES
    tiles_per_chunk = IDX_KCHUNK // LANES
    qi = qi_ref[0].astype(BF)
    wt = wt_ref[0] * (IDX_DIM ** -0.5 * IDX_HEADS ** -0.5)

    n_chunks = (j * QW + QW - 1) // IDX_KCHUNK + 1
    q_pos = j * QW + lax.broadcasted_iota(I32, (IDX_KCHUNK, QW), 1)

    def chunk_step(ci, carry):
        k0 = pl.multiple_of(ci * IDX_KCHUNK, IDX_KCHUNK)
        kic = tail_ref[0, pl.ds(k0, IDX_KCHUNK), :][:, 0:IDX_DIM].astype(BF)
        acc = jnp.zeros((IDX_KCHUNK, QW), F32)
        for h in range(IDX_HEADS):
            s = _dot_nt(kic, qi[:, h * IDX_DIM:(h + 1) * IDX_DIM])
            acc = acc + wt[h:h + 1] * jnp.maximum(s, 0.0)
        k_pos = k0 + lax.broadcasted_iota(I32, (IDX_KCHUNK, QW), 0)
        acc = jnp.where(k_pos <= q_pos, acc, -jnp.inf)
        for i in range(tiles_per_chunk):
            score_ref[ci * tiles_per_chunk + i] = acc[i * LANES:(i + 1) * LANES]
        return carry

    lax.fori_loop(0, n_chunks, chunk_step, 0)

    def write_tile(t, m):
        mask_ref[0, 0, t] = m.astype(BF)

    _topk_mask_t(score_ref, u_ref, write_tile, n_tiles, n_chunks, tiles_per_chunk, TOPK)


def indexer_prompt_t(z3, wt, tail3):
    bsz, seq = z3.shape[:2]
    n_tiles = seq // LANES
    assert seq % QW == 0 and seq % IDX_KCHUNK == 0
    return pl.pallas_call(
        functools.partial(_indexer_t_body, seq=seq),
        grid=(bsz, seq // QW),
        in_specs=[pl.BlockSpec((1, QW, IDX_HEADS * IDX_DIM), lambda b, j: (b, j, 5)),
                  pl.BlockSpec((1, IDX_HEADS, QW), lambda b, j: (b, 0, j)),
                  pl.BlockSpec((1, seq, LANES), lambda b, j: (b, 0, 0))],
        out_specs=pl.BlockSpec((1, 1, n_tiles, LANES, QW), lambda b, j: (b, j, 0, 0, 0)),
        out_shape=jax.ShapeDtypeStruct((bsz, seq // QW, n_tiles, LANES, QW), BF),
        scratch_shapes=[pltpu.VMEM((n_tiles, LANES, QW), F32),
                        pltpu.VMEM((n_tiles, LANES, QW), I32)],
        compiler_params=_cparams(("parallel", "arbitrary")),
        name="indexer_prompt_t",
    )(z3, wt, tail3)


def _attn_t_body(bfar_ref, q_ref, k_ref, vt_ref, mask_ref, bias_ref, o_ref, m_ref, l_ref, acc_ref):
    j = pl.program_id(1)
    scale = HEAD_DIM ** -0.5
    qsub = QW // LANES
    q = q_ref[0].astype(BF)
    m_ref[...] = jnp.full(m_ref.shape, NEG_BIG, F32)
    l_ref[...] = jnp.zeros(l_ref.shape, F32)
    acc_ref[...] = jnp.zeros(acc_ref.shape, F32)

    def process(sb, near):
        k0 = pl.multiple_of(sb * ATTN_KSTEP, ATTN_KSTEP)
        kblk = k_ref[0, pl.ds(k0, ATTN_KSTEP), :]
        vblk = vt_ref[0, pl.ds(k0, ATTN_KSTEP), :]
        tiles = [sb * ATTN_TILES + i for i in range(ATTN_TILES)]
        sel = jnp.concatenate([mask_ref[0, 0, t] for t in tiles], axis=0).astype(F32) > 0.0
        if near:
            def tile_bias(t, h):
                row = []
                for s in range(qsub):
                    d = j * qsub + s - t
                    row.append(bias_ref[jnp.where(d == 0, 0, jnp.where(d == 1, 1, 2)), h])
                return jnp.concatenate(row, axis=1)
        head_cols = [slice(h * HEAD_DIM, (h + 1) * HEAD_DIM) for h in range(HEADS)]
        qk = lambda h: _dot_nt(kblk[:, head_cols[h]], q[:, head_cols[h]])
        lg_next = qk(0)
        for h in range(HEADS):
            cols = head_cols[h]
            lg = lg_next * scale
            if h + 1 < HEADS:
                lg_next = qk(h + 1)
            if near:
                lg = lg + jnp.concatenate([tile_bias(t, h) for t in tiles], axis=0)
            else:
                lg = lg + bfar_ref[h]
            lg = jnp.where(sel, lg, NEG_BIG)
            m_old = m_ref[h]
            part = jnp.max(lg.reshape(ATTN_KSTEP // SUB, SUB, QW), axis=0)
            m_new = jnp.maximum(m_old, _row_all(part, jnp.max))
            p = jnp.exp(lg - m_new[0:1])
            alpha = jnp.exp(m_old - m_new)
            psum = jnp.sum(p.reshape(ATTN_KSTEP // SUB, SUB, QW), axis=0)
            l_ref[h] = alpha * l_ref[h] + _row_all(psum, jnp.sum)
            pv = _dot_tn(vblk[:, cols], p.astype(BF))
            acc_ref[cols, :] = alpha[0:1] * acc_ref[cols, :] + pv
            m_ref[h] = m_new

    n_far = jnp.maximum((qsub * j - 1) // ATTN_TILES, 0)
    n_steps = (qsub * j + qsub - 1) // ATTN_TILES + 1

    def far_step(sb, carry):
        process(sb, False)
        return carry

    def near_step(sb, carry):
        process(sb, True)
        return carry

    lax.fori_loop(0, n_far, far_step, 0)
    lax.fori_loop(n_far, n_steps, near_step, 0)

    for h in range(HEADS):
        cols = slice(h * HEAD_DIM, (h + 1) * HEAD_DIM)
        o_ref[0, cols, :] = acc_ref[cols, :] / l_ref[h][0:1]


def attn_prompt_t(z3, kv_bf, mask_t, bias_tiles_t, bias_far):
    bsz, seq = z3.shape[:2]
    n_tiles = seq // LANES
    width = HEADS * HEAD_DIM
    assert seq % ATTN_KSTEP == 0 and seq % QW == 0
    return pl.pallas_call(
        _attn_t_body,
        grid=(bsz, seq // QW),
        in_specs=[pl.BlockSpec(memory_space=pltpu.SMEM),
                  pl.BlockSpec((1, QW, width), lambda b, j: (b, j, 4)),
                  pl.BlockSpec((1, seq, width), lambda b, j: (b, 0, 0)),
                  pl.BlockSpec((1, seq, width), lambda b, j: (b, 0, 1)),
                  pl.BlockSpec((1, 1, n_tiles, LANES, QW), lambda b, j: (b, j, 0, 0, 0)),
                  pl.BlockSpec(bias_tiles_t.shape, lambda b, j: (0, 0, 0, 0))],
        out_specs=pl.BlockSpec((1, width, QW), lambda b, j: (b, 0, j)),
        out_shape=jax.ShapeDtypeStruct((bsz, width, seq), F32),
        scratch_shapes=[pltpu.VMEM((HEADS, SUB, QW), F32),
                        pltpu.VMEM((HEADS, SUB, QW), F32),
                        pltpu.VMEM((width, QW), F32)],
        compiler_params=_cparams(("parallel", "arbitrary")),
        name="attn_prompt_t",
    )(bias_far, z3, kv_bf, kv_bf, mask_t, bias_tiles_t)


def _indexer_sample_body(pt_ref, qi_ref, wcol_ref, kinew_ref, *rest, n_pages):
    ki_refs = rest[:n_pages]
    out_ref = rest[n_pages]
    qi = qi_ref[0].astype(BF)
    w = wcol_ref[0] * (IDX_DIM ** -0.5 * IDX_HEADS ** -0.5)
    for i in range(n_pages):
        s = _dot(qi, ki_refs[i][0].astype(BF))
        out_ref[i, 0] = jnp.sum(w * jnp.maximum(s, 0.0), axis=0, keepdims=True)
    kn = kinew_ref[0].astype(BF).astype(F32)
    sn = jnp.sum(qi.astype(F32) * kn, axis=-1, keepdims=True)
    new = jnp.sum(w * jnp.maximum(sn, 0.0), axis=0, keepdims=True)
    lane = lax.broadcasted_iota(I32, (1, LANES), 1)
    out_ref[n_pages, 0] = jnp.where(lane == 0, new, -jnp.inf)


def indexer_sample(page_table, qi3, wcol, ki_new, ki_pool_t):
    bsz, n_pages = page_table.shape
    ki_spec = lambda i: pl.BlockSpec((1, IDX_DIM, PAGE), lambda b, pt, i=i: (pt[b, i], 0, 0))
    grid_spec = pltpu.PrefetchScalarGridSpec(
        num_scalar_prefetch=1,
        grid=(bsz,),
        in_specs=[pl.BlockSpec((1, IDX_HEADS, IDX_DIM), lambda b, pt: (b, 0, 0)),
                  pl.BlockSpec((1, IDX_HEADS, 1), lambda b, pt: (b, 0, 0)),
                  pl.BlockSpec((1, 1, IDX_DIM), lambda b, pt: (b, 0, 0))]
                 + [ki_spec(i) for i in range(n_pages)],
        out_specs=pl.BlockSpec((n_pages + 1, 1, 1, LANES), lambda b, pt: (0, b, 0, 0)),
    )
    out = pl.pallas_call(
        functools.partial(_indexer_sample_body, n_pages=n_pages),
        grid_spec=grid_spec,
        out_shape=jax.ShapeDtypeStruct((n_pages + 1, bsz, 1, LANES), F32),
        compiler_params=_cparams(("arbitrary",)),
        name="indexer_sample",
    )(page_table, qi3, wcol, ki_new, *([ki_pool_t] * n_pages))
    return out.reshape(n_pages + 1, bsz, LANES)


def _select_body(score_ref, mask_ref, u_ref, *, n_tiles, k):
    def write_tile(t, m):
        mask_ref[t] = m

    _topk_mask(score_ref, u_ref, write_tile, n_tiles, n_tiles, 1, k)


def select_topk(scores, k):
    n_tiles, rows, _ = scores.shape
    return pl.pallas_call(
        functools.partial(_select_body, n_tiles=n_tiles, k=k),
        grid=(1,),
        in_specs=[pl.BlockSpec(scores.shape, lambda i: (0, 0, 0))],
        out_specs=pl.BlockSpec(scores.shape, lambda i: (0, 0, 0)),
        out_shape=jax.ShapeDtypeStruct(scores.shape, F32),
        scratch_shapes=[pltpu.VMEM(scores.shape, I32)],
        compiler_params=_cparams(("arbitrary",)),
        name="select_topk",
    )(scores)


SC_CORES = 2
SC_SUBCORES = 16
SC_GATHER_CHUNK = 32


def sc_gather_rows(tables, idx):
    n_rows = idx.shape[0]
    workers = SC_CORES * SC_SUBCORES
    per_worker = n_rows // workers
    chunk = SC_GATHER_CHUNK
    assert n_rows % workers == 0 and per_worker % chunk == 0 and chunk % 8 == 0
    row_shape = tables[0].shape[1:]
    n_tab = len(tables)
    mesh = plsc.VectorSubcoreMesh(core_axis_name="c", subcore_axis_name="s",
                                  num_cores=SC_CORES, num_subcores=SC_SUBCORES)

    def body(*refs):
        tab_refs = refs[:n_tab]
        idx_hbm = refs[n_tab]
        out_refs = refs[n_tab + 1:2 * n_tab + 1]
        idx_v, rows_v, sem = refs[2 * n_tab + 1:]
        wid = lax.axis_index("s") * SC_CORES + lax.axis_index("c")

        @pl.loop(0, per_worker // chunk)
        def _(ci):
            off = pl.multiple_of(wid * per_worker + ci * chunk, 8)
            pltpu.sync_copy(idx_hbm.at[pl.ds(off, chunk)], idx_v)
            for tab, out in zip(tab_refs, out_refs):
                pltpu.async_copy(tab.at[idx_v], rows_v, sem).wait()
                pltpu.sync_copy(rows_v, out.at[pl.ds(off, chunk)])

    return pl.kernel(
        body,
        out_type=[jax.ShapeDtypeStruct((n_rows, *row_shape), t.dtype) for t in tables],
        mesh=mesh,
        scratch_types=[pltpu.VMEM((chunk,), I32),
                       pltpu.VMEM((chunk, *row_shape), tables[0].dtype),
                       pltpu.SemaphoreType.DMA],
        compiler_params=pltpu.CompilerParams(use_tc_tiling_on_sc=True),
        name="sc_gather_rows",
    )(*tables, idx)


SAMPLE_SEQ_GROUP = 8


def _attn_compact_body(dist_ref, near_ref, q_ref, kn_ref, vn_ref, btab_ref, kc_ref, vc_ref,
                       o_ref, bbuf):
    step = pl.program_id(0)
    scale = HEAD_DIM ** -0.5
    ones = jnp.ones((HEAD_DIM, LANES), BF)
    far_bias = btab_ref[REL_MAX_DIST]
    last = lax.broadcasted_iota(I32, (TOPK, 1, 1), 0) == TOPK - 1
    for g in range(SAMPLE_SEQ_GROUP):
        b = step * SAMPLE_SEQ_GROUP + g
        bbuf[g] = jnp.broadcast_to(far_bias[None], (TOPK, HEADS, LANES))

        def fill(t, carry, g=g, b=b):
            bbuf[g, t] = btab_ref[dist_ref[b, t]]
            return carry

        lax.fori_loop(near_ref[b], TOPK, fill, 0)

        is_new = dist_ref[b, TOPK - 1] == 0
        rows = slice(g * TOPK, (g + 1) * TOPK)
        rnd = lambda a: a.astype(BF).astype(F32)
        kc = rnd(jnp.where(last & is_new, kn_ref[g][None], kc_ref[rows]))
        vc = rnd(jnp.where(last & is_new, vn_ref[g][None], vc_ref[rows]))
        prod = (kc * rnd(q_ref[g])[None]).reshape(TOPK * HEADS, HEAD_DIM)
        hi = prod.astype(BF)
        lo = (prod - hi.astype(F32)).astype(BF)
        lg = (_dot(hi, ones) + _dot(lo, ones)).reshape(TOPK, HEADS, LANES) * scale + bbuf[g]
        m = jnp.max(lg, axis=0, keepdims=True)
        p = jnp.exp(lg - m)
        l = jnp.sum(p, axis=0)
        o_ref[g] = jnp.sum(rnd(p) * vc, axis=0) / l


def attn_sample_compact(dist, first_near, zs3, k_sel, v_sel, bias_by_dist):
    bsz = dist.shape[0]
    g = SAMPLE_SEQ_GROUP
    assert bsz % g == 0
    zspec = lambda cb: pl.BlockSpec((g, HEADS, HEAD_DIM), lambda i, d, n, cb=cb: (i, cb, 0))
    sel_spec = pl.BlockSpec((g * TOPK, HEADS, HEAD_DIM), lambda i, d, n: (i, 0, 0))
    grid_spec = pltpu.PrefetchScalarGridSpec(
        num_scalar_prefetch=2,
        grid=(bsz // g,),
        in_specs=[zspec(4), zspec(5), zspec(6),
                  pl.BlockSpec(bias_by_dist.shape, lambda i, d, n: (0, 0, 0)),
                  sel_spec, sel_spec],
        out_specs=pl.BlockSpec((g, HEADS, HEAD_DIM), lambda i, d, n: (i, 0, 0)),
        scratch_shapes=[pltpu.VMEM((g, TOPK, HEADS, HEAD_DIM), F32)],
    )
    return pl.pallas_call(
        _attn_compact_body,
        grid_spec=grid_spec,
        out_shape=jax.ShapeDtypeStruct((bsz, HEADS, HEAD_DIM), F32),
        compiler_params=_cparams(("arbitrary",)),
        name="attn_sample_compact",
    )(dist, first_near, zs3, zs3, zs3, bias_by_dist, k_sel, v_sel)


def _bucket_table(max_dist):
    exact = REL_BUCKETS // 2
    d = np.arange(max_dist + 1)
    df = np.maximum(d, 1).astype(np.float32)
    far = exact + (np.log(df / exact) / np.float32(math.log(REL_MAX_DIST / exact))
                   * (REL_BUCKETS - exact)).astype(np.int32)
    return np.where(d < exact, d, np.minimum(far, REL_BUCKETS - 1)).astype(np.int32)


def _bias_tables(rel_bias):
    tab = _bucket_table(2 * LANES)
    assert np.all(tab[REL_MAX_DIST:] == REL_BUCKETS - 1)
    i = np.arange(LANES)
    dist0 = np.maximum(i[:, None] - i[None, :], 0)
    dist1 = LANES + i[:, None] - i[None, :]
    far = np.full((LANES, LANES), REL_BUCKETS - 1)
    idx = np.stack([tab[dist0], tab[dist1], far])
    buckets = jnp.arange(REL_BUCKETS)
    lookup = lambda ix, spec: jnp.einsum(spec, (jnp.asarray(ix)[..., None] == buckets).astype(F32),
                                         rel_bias.astype(F32), precision=lax.Precision.HIGHEST)
    tiles = lookup(idx, "tijb,bh->thij")
    by_dist = jnp.broadcast_to(lookup(tab[:REL_MAX_DIST + 1], "db,bh->dh")[:, :, None],
                               (REL_MAX_DIST + 1, HEADS, LANES))
    return tiles.astype(F32), rel_bias[REL_BUCKETS - 1].astype(F32), by_dist.astype(F32)


def _conv_prompt_body(bg_ref, cg_ref, xt_ref, w_ref, v_ref, st_ref, carry_ref, *, tblock):
    t = pl.program_id(1)

    @pl.when(t == 0)
    def _():
        carry_ref[...] = jnp.zeros_like(carry_ref)

    u = cg_ref[0] * xt_ref[0]
    row = lax.broadcasted_iota(I32, u.shape, 0)
    c0 = carry_ref[0:1]
    c1 = carry_ref[1:2]
    u1 = jnp.where(row == 0, c1, pltpu.roll(u, 1, axis=0))
    u2 = jnp.where(row == 0, c0, jnp.where(row == 1, c1, pltpu.roll(u, 2, axis=0)))
    conv = w_ref[0:1] * u2 + w_ref[1:2] * u1 + w_ref[2:3] * u
    v_ref[0] = bg_ref[0] * conv
    last = u[tblock - 2:tblock]
    carry_ref[0:2] = last

    @pl.when(t == pl.num_programs(1) - 1)
    def _():
        st_ref[0] = last


def conv_prompt(zc3, w_conv, tblock=256):
    bsz, seq = zc3.shape[:2]
    c = zc3.shape[2] // 3
    zspec = lambda cb: pl.BlockSpec((1, tblock, c), lambda b, t, cb=cb: (b, t, cb))
    return pl.pallas_call(
        functools.partial(_conv_prompt_body, tblock=tblock),
        grid=(bsz, seq // tblock),
        in_specs=[zspec(0), zspec(1), zspec(2), pl.BlockSpec((3, c), lambda b, t: (0, 0))],
        out_specs=[pl.BlockSpec((1, tblock, c), lambda b, t: (b, t, 0)),
                   pl.BlockSpec((1, 2, c), lambda b, t: (b, 0, 0))],
        out_shape=[jax.ShapeDtypeStruct((bsz, seq, c), F32),
                   jax.ShapeDtypeStruct((bsz, 2, c), F32)],
        scratch_shapes=[pltpu.VMEM((8, c), F32)],
        compiler_params=_cparams(("parallel", "arbitrary")),
        name="conv_prompt",
    )(zc3, zc3, zc3, w_conv)


def _conv_sample_body(bg_ref, cg_ref, xt_ref, w_ref, s0_ref, s1_ref, v_ref, n0_ref, n1_ref):
    u = cg_ref[...] * xt_ref[...]
    conv = w_ref[0:1] * s0_ref[...] + w_ref[1:2] * s1_ref[...] + w_ref[2:3] * u
    v_ref[...] = bg_ref[...] * conv
    n0_ref[...] = s1_ref[...]
    n1_ref[...] = u


def conv_sample(zc, w_conv, s0, s1):
    bsz = zc.shape[0]
    c = zc.shape[1] // 3
    zspec = lambda cb: pl.BlockSpec((bsz, c), lambda i, cb=cb: (0, cb))
    full = pl.BlockSpec((bsz, c), lambda i: (0, 0))
    return pl.pallas_call(
        _conv_sample_body,
        grid=(1,),
        in_specs=[zspec(0), zspec(1), zspec(2), pl.BlockSpec((3, c), lambda i: (0, 0)), full, full],
        out_specs=[full, full, full],
        out_shape=[jax.ShapeDtypeStruct((bsz, c), F32)] * 3,
        compiler_params=_cparams(("arbitrary",)),
        name="conv_sample",
    )(zc, zc, zc, w_conv, s0, s1)


def _router_body(x_ref, g_ref, wr_ref, br_ref, *rest):
    h_ref, route_ref = rest[-2:]
    x = x_ref[...]
    ms = jnp.mean(x * x, axis=-1, keepdims=True)
    h = x * lax.rsqrt(ms + RMS_EPS) * g_ref[...]
    hb = h.astype(BF)
    half = h.shape[1] // 2
    bits = pltpu.bitcast(hb.astype(F32), jnp.uint32)
    h_ref[...] = (bits[:, :half] & jnp.uint32(0xFFFF0000)) | (bits[:, half:] >> 16)
    logits = _dot(hb, wr_ref[...]) + br_ref[...]
    lane = lax.broadcasted_iota(I32, logits.shape, 1)
    big = np.int32(1 << 20)
    lg = jnp.where(lane < N_GROUPS, logits, -jnp.inf)
    g_max = jnp.max(lg, axis=-1, keepdims=True)
    g_idx = jnp.min(jnp.where(lg == g_max, lane, big), axis=-1, keepdims=True)
    g_w = 1.0 / jnp.sum(jnp.exp(lg - g_max), axis=-1, keepdims=True)
    first = N_GROUPS + EXP_PER_GROUP * g_idx
    le = jnp.where((lane >= first) & (lane < first + EXP_PER_GROUP), logits, -jnp.inf)
    l1 = jnp.max(le, axis=-1, keepdims=True)
    i1 = jnp.min(jnp.where(le == l1, lane, big), axis=-1, keepdims=True)
    le2 = jnp.where(lane == i1, -jnp.inf, le)
    l2 = jnp.max(le2, axis=-1, keepdims=True)
    i2 = jnp.min(jnp.where(le2 == l2, lane, big), axis=-1, keepdims=True)
    r = jnp.exp(l2 - l1)
    w1 = g_w / (1.0 + r)
    w2 = g_w * r / (1.0 + r)
    e1 = (i1 - N_GROUPS).astype(F32)
    e2 = (i2 - N_GROUPS).astype(F32)
    route_ref[...] = jnp.where(lane == 0, e1, jnp.where(lane == 1, e2,
                               jnp.where(lane == 2, w1, jnp.where(lane == 3, w2, 0.0))))


def moe_router(x, g, wr, br, tm, total_rows, row0, bufs=None):
    m, k = x.shape
    assert row0 % tm == 0
    off = row0 // tm
    in_specs = [pl.BlockSpec((tm, k), lambda i: (i, 0)),
                pl.BlockSpec((1, k), lambda i: (0, 0)),
                pl.BlockSpec((k, LANES), lambda i: (0, 0)),
                pl.BlockSpec((1, LANES), lambda i: (0, 0))]
    args = [x, g.reshape(1, k), wr, br]
    aliases = {}
    if bufs is not None:
        in_specs += [pl.BlockSpec(memory_space=pl.ANY)] * 2
        args += list(bufs)
        aliases = {4: 0, 5: 1}
    return pl.pallas_call(
        _router_body,
        grid=(m // tm,),
        in_specs=in_specs,
        out_specs=[pl.BlockSpec((tm, k // 2), lambda i: (i + off, 0)),
                   pl.BlockSpec((tm, LANES), lambda i: (i + off, 0))],
        out_shape=[jax.ShapeDtypeStruct((total_rows, k // 2), jnp.uint32),
                   jax.ShapeDtypeStruct((total_rows, LANES), F32)],
        input_output_aliases=aliases,
        compiler_params=_cparams(("parallel",)),
        name="moe_router",
    )(*args)


def _experts_body(te_ref, nu_ref, h_ref, wg_ref, wu_ref, wd_ref, o_ref, wg_bf, wu_bf, wd_bf):
    i = pl.program_id(0)

    @pl.when((i == 0) | (te_ref[i] != te_ref[jnp.maximum(i - 1, 0)]))
    def _():
        wg_bf[...] = wg_ref[0].astype(BF)
        wu_bf[...] = wu_ref[0].astype(BF)
        wd_bf[...] = wd_ref[0].astype(BF)

    @pl.when(i < nu_ref[0])
    def _():
        words = h_ref[...]
        left = pltpu.bitcast(words & jnp.uint32(0xFFFF0000), F32)
        right = pltpu.bitcast(words << 16, F32)
        h = jnp.concatenate([left, right], axis=1).astype(BF)
        a = _dot(h, wg_bf[...])
        b = _dot(h, wu_bf[...])
        hid = a * _sigmoid(a) * b
        o_ref[...] = _dot(hid.astype(BF), wd_bf[...])

    @pl.when(i >= nu_ref[0])
    def _():
        o_ref[...] = jnp.zeros_like(o_ref)


def moe_experts(tile_expert, n_used, hs, wg, wu, wd):
    p = hs.shape[0]
    k = wg.shape[1]
    f = wg.shape[2]
    n_tiles = p // MOE_TILE
    grid_spec = pltpu.PrefetchScalarGridSpec(
        num_scalar_prefetch=2,
        grid=(n_tiles,),
        in_specs=[pl.BlockSpec((MOE_TILE, k // 2), lambda i, te, nu: (i, 0)),
                  pl.BlockSpec((1, k, f), lambda i, te, nu: (te[i], 0, 0)),
                  pl.BlockSpec((1, k, f), lambda i, te, nu: (te[i], 0, 0)),
                  pl.BlockSpec((1, f, k), lambda i, te, nu: (te[i], 0, 0))],
        out_specs=pl.BlockSpec((MOE_TILE, k), lambda i, te, nu: (i, 0)),
        scratch_shapes=[pltpu.VMEM((k, f), BF), pltpu.VMEM((k, f), BF), pltpu.VMEM((f, k), BF)],
    )
    return pl.pallas_call(
        _experts_body,
        grid_spec=grid_spec,
        out_shape=jax.ShapeDtypeStruct((p, k), F32),
        compiler_params=_cparams(("arbitrary",)),
        name="moe_experts",
    )(tile_expert, n_used, hs, wg, wu, wd)


def _combine_body(x_ref, a_ref, b_ref, route_ref, g_ref, o_ref, *, normalize):
    y = x_ref[...] + route_ref[:, 2:3] * a_ref[...] + route_ref[:, 3:4] * b_ref[...]
    if normalize:
        ms = jnp.mean(y * y, axis=-1, keepdims=True)
        y = y * lax.rsqrt(ms + RMS_EPS) * g_ref[...]
    o_ref[...] = y


def moe_combine(x, out_a, out_b, route, tm, norm_g=None):
    m, k = x.shape
    row = pl.BlockSpec((tm, k), lambda i: (i, 0))
    g = jnp.ones((1, k), F32) if norm_g is None else norm_g.reshape(1, k)
    return pl.pallas_call(
        functools.partial(_combine_body, normalize=norm_g is not None),
        grid=(m // tm,),
        in_specs=[row, row, row, pl.BlockSpec((tm, LANES), lambda i: (i, 0)),
                  pl.BlockSpec((1, k), lambda i: (0, 0))],
        out_specs=row,
        out_shape=jax.ShapeDtypeStruct((m, k), F32),
        compiler_params=_cparams(("parallel",)),
        name="moe_combine",
    )(x, out_a, out_b, route, g)


def _rank_within_expert(onehot):
    n, e = onehot.shape
    blk = LANES
    assert n % blk == 0
    oh = onehot.astype(F32).reshape(n // blk, blk, e)
    strict = jnp.asarray(np.tril(np.ones((blk, blk), np.float32), -1))
    within = jnp.einsum("ij,bjk->bik", strict, oh, precision=lax.Precision.HIGHEST)
    totals = jnp.sum(oh, axis=1)
    before = jnp.cumsum(totals, axis=0) - totals
    rank = (within + before[:, None, :]).reshape(n, e)
    return jnp.sum(rank * onehot.astype(F32), axis=1).astype(I32), jnp.sum(totals, axis=0).astype(I32)


def hier_moe(xs, tms, g, wrg, brg, wre, bre, wg, wu, wd, layer, out_norm_g=None):
    d = xs[0].shape[1]
    m = sum(x.shape[0] for x in xs)
    wr = jnp.zeros((d, LANES), F32)
    wr = wr.at[:, :N_GROUPS].set(wrg).at[:, N_GROUPS:N_GROUPS + N_EXPERTS].set(wre.reshape(d, N_EXPERTS))
    br = jnp.zeros((1, LANES), F32)
    br = br.at[0, :N_GROUPS].set(brg).at[0, N_GROUPS:N_GROUPS + N_EXPERTS].set(bre.reshape(N_EXPERTS))
    bufs, row0 = None, 0
    for x, tm in zip(xs, tms):
        bufs = moe_router(x, g, wr.astype(BF), br, tm, m, row0, bufs)
        row0 += x.shape[0]
    h_bf, route = bufs

    eid = route[:, 0:2].astype(I32).reshape(-1)
    onehot = eid[:, None] == jnp.arange(N_EXPERTS, dtype=I32)[None, :]
    rank, counts = _rank_within_expert(onehot)
    padded = ((counts + MOE_TILE - 1) // MOE_TILE) * MOE_TILE
    ends = jnp.cumsum(padded)
    pos = jnp.sum(jnp.where(onehot, (ends - padded)[None, :], 0), axis=1) + rank
    n_rows = 2 * m + N_EXPERTS * MOE_TILE
    n_rows = -(-n_rows // MOE_TILE) * MOE_TILE
    token = jnp.zeros((n_rows,), I32).at[pos].set(jnp.arange(2 * m, dtype=I32) // 2)
    tile_start = jnp.arange(n_rows // MOE_TILE, dtype=I32) * MOE_TILE
    tile_expert = jnp.minimum(jnp.sum(tile_start[:, None] >= ends[None, :], axis=1),
                              N_EXPERTS - 1).astype(I32)
    n_used = (ends[-1] // MOE_TILE).astype(I32).reshape(1)

    hs = jnp.take(h_bf, token, axis=0, mode="clip")
    out = moe_experts(tile_expert + layer * N_EXPERTS, n_used, hs, wg, wu, wd)
    pos2 = pos.reshape(m, 2)
    res, r0 = [], 0
    for x, tm in zip(xs, tms):
        rows = slice(r0, r0 + x.shape[0])
        res.append(moe_combine(x, jnp.take(out, pos2[rows, 0], axis=0, mode="clip"),
                               jnp.take(out, pos2[rows, 1], axis=0, mode="clip"),
                               route[rows], tm, out_norm_g))
        r0 += x.shape[0]
    return res


def kernel(x_prompt, x_sample, cache_k, cache_v, cache_ki, state_hgrn, state_conv, page_table,
           norm_mix_g, norm_ffn_g, final_g, w_in_even, w_out_even, hgrn_lb_logits, hgrn_norm_g,
           rel_bias, w_in_conv, w_conv, w_out_conv, w_router_g, b_router_g, w_router_e,
           b_router_e, w_gate, w_up, w_down):
    bsz, seq, d = x_prompt.shape
    dec = x_sample.shape[0]
    n_p = bsz * seq
    tm_p, tm_s = 512, 128
    tms = (tm_p, tm_s)
    assert n_p % tm_p == 0 and dec % tm_s == 0 and x_sample.shape[1] == 1
    width = HEADS * HEAD_DIM
    even_in = w_in_even.shape[2]
    even_pad = -(-even_in // IN_PROJ_TN) * IN_PROJ_TN
    tail0 = TAIL_BLOCK * LANES
    n_pool = cache_k.shape[1]

    xp = x_prompt.reshape(n_p, d)
    xs = x_sample.reshape(dec, d)

    lbs = jnp.cumsum(jax.nn.softmax(hgrn_lb_logits.astype(F32), axis=0), axis=0)[:-1]
    bias_tiles, bias_far, bias_by_dist = _bias_tables(rel_bias)
    expert_w = [w.reshape(-1, *w.shape[2:]) for w in (w_gate, w_up, w_down)]

    w_in = jnp.pad(w_in_even[0], ((0, 0), (0, even_pad - even_in))).astype(BF)
    zp, k_new_p, v_new_p, kv_bf = even_projection(xp, norm_mix_g[0], w_in, 2 * tm_p)
    zt = norm_matmul(xp, norm_mix_g[0], w_in[:, tail0:tail0 + LANES], 2 * tm_p, LANES)
    zs = norm_matmul(xs, norm_mix_g[0], w_in, tm_s, IN_PROJ_TN)
    zp3 = zp.reshape(bsz, seq, 6 * width)
    zt3 = zt.reshape(bsz, seq, LANES)
    zs3 = zs.reshape(dec, even_pad // LANES, LANES)

    oa_p, hgrn_p = hgrn_prompt(zp3, lbs[0], hgrn_norm_g[0])
    oa_s, hgrn_s = hgrn_sample(zs3, lbs[0], hgrn_norm_g[0], state_hgrn[0])

    wt = jnp.swapaxes(zt3[:, :, IDX_DIM:IDX_DIM + IDX_HEADS], 1, 2)
    mask_p = indexer_prompt_t(zp3, wt, zt3)
    ob_p = jnp.swapaxes(attn_prompt_t(zp3, kv_bf.reshape(bsz, seq, 2 * width), mask_p,
                                      jnp.swapaxes(bias_tiles, -1, -2), bias_far), 1, 2)

    qi3 = zs[:, 7 * width:8 * width].reshape(dec, IDX_HEADS, IDX_DIM)
    ki_new = zs[:, tail0:tail0 + IDX_DIM].reshape(dec, 1, IDX_DIM)
    wcol = zs[:, tail0 + IDX_DIM:tail0 + IDX_DIM + IDX_HEADS].reshape(dec, IDX_HEADS, 1)
    pages = page_table + 0 * n_pool
    scores_s = indexer_sample(pages, qi3, wcol, ki_new,
                              jnp.swapaxes(cache_ki, -1, -2).reshape(-1, IDX_DIM, PAGE))
    mask_s = select_topk(scores_s, TOPK)
    sel = jnp.transpose(mask_s, (1, 0, 2))
    n_pages = page_table.shape[1]
    sel_off = jnp.concatenate([jnp.zeros((dec, 1), F32),
                               jnp.cumsum(jnp.sum(sel, axis=2), axis=1)], axis=1)
    slot = jnp.arange(TOPK, dtype=F32)
    page_of = jnp.sum(sel_off[:, None, 1:] <= slot[None, :, None], axis=2)
    page_1h = (page_of[:, :, None] == jnp.arange(n_pages + 1)[None, None, :]).astype(F32)
    local = slot[None, :] - jnp.einsum("brp,bp->br", page_1h, sel_off[:, :-1],
                                       precision=lax.Precision.HIGHEST)
    within = jnp.einsum("brp,bpl->brl", page_1h, jnp.cumsum(sel, axis=2),
                        precision=lax.Precision.HIGHEST)
    lane_of = jnp.sum(within <= local[:, :, None], axis=2)
    sel_idx = (page_of * PAGE + lane_of).astype(I32)
    past = n_pages * PAGE
    page_id = jnp.einsum("brp,bp->br", page_1h[:, :, :n_pages], pages.astype(F32),
                         precision=lax.Precision.HIGHEST).astype(I32)
    rows = jnp.where(page_of < n_pages, page_id * PAGE + lane_of, 0).astype(I32).reshape(-1)
    rows, zp3 = lax.optimization_barrier((rows, zp3))
    k_sel, v_sel = sc_gather_rows([cache_k.reshape(-1, HEADS, HEAD_DIM),
                                   cache_v.reshape(-1, HEADS, HEAD_DIM)], rows)
    dist = jnp.minimum(past - sel_idx, REL_MAX_DIST).astype(I32)
    first_near = jnp.sum(dist >= REL_MAX_DIST, axis=1).astype(I32)
    ob_s = attn_sample_compact(dist, first_near, zs3, k_sel, v_sel, bias_by_dist)

    w_out = w_out_even[0].astype(BF)
    w_out_ab = [w_out[:width], w_out[width:]]
    xp = matmul_residual([oa_p.reshape(n_p, width), ob_p.reshape(n_p, width)], w_out_ab, xp, 2 * tm_p, 1024)
    xs = matmul_residual([oa_s.reshape(dec, width), ob_s.reshape(dec, width)], w_out_ab, xs, tm_s, 512)

    xp, xs = hier_moe([xp, xs], tms, norm_ffn_g[0], w_router_g[0], b_router_g[0], w_router_e[0],
                      b_router_e[0], *expert_w, 0)

    w_in_c = w_in_conv[0].astype(BF)
    zcp = norm_matmul(xp, norm_mix_g[1], w_in_c, 2 * tm_p, 1024)
    zcs = norm_matmul(xs, norm_mix_g[1], w_in_c, tm_s, 512)
    cw = zcp.shape[1] // 3
    v_p, conv_p = conv_prompt(zcp.reshape(bsz, seq, 3 * cw), w_conv[0])
    v_s, cs0, cs1 = conv_sample(zcs, w_conv[0], state_conv[0, :, 0], state_conv[0, :, 1])
    w_out_c = [w_out_conv[0].astype(BF)]
    xp = matmul_residual([v_p.reshape(n_p, cw)], w_out_c, xp, 2 * tm_p, 1024)
    xs = matmul_residual([v_s], w_out_c, xs, tm_s, 512)

    yp, ys = hier_moe([xp, xs], tms, norm_ffn_g[1], w_router_g[1], b_router_g[1], w_router_e[1],
                      b_router_e[1], *expert_w, 1, out_norm_g=final_g)

    kcol, vcol = 5 * width, 6 * width
    heads = lambda a, n: a.reshape(1, *n, HEADS, HEAD_DIM)
    return (yp.reshape(bsz, seq, d),
            ys.reshape(dec, 1, d),
            heads(k_new_p, (bsz, seq)),
            heads(v_new_p, (bsz, seq)),
            zt[:, :IDX_DIM].reshape(1, bsz, seq, IDX_DIM),
            hgrn_p[None],
            conv_p[None],
            heads(zs[:, kcol:kcol + width], (dec, 1)),
            heads(zs[:, vcol:vcol + width], (dec, 1)),
            zs[:, tail0:tail0 + IDX_DIM].reshape(1, dec, 1, IDX_DIM),
            hgrn_s[None],
            jnp.stack([cs0, cs1], axis=1)[None])
```

```python
import functools
import math

import numpy as np
import jax
import jax.numpy as jnp
from jax import lax
from jax.experimental import pallas as pl
from jax.experimental.pallas import tpu as pltpu
from jax.experimental.pallas import tpu_sc as plsc

F32 = jnp.float32
BF = jnp.bfloat16
I32 = jnp.int32

RMS_EPS = 1e-6
LANES = 128
NEG_BIG = -1e30
VMEM_LIMIT = 56 * 1024 * 1024

D_MODEL = 2048
HEADS = 8
HEAD_DIM = 128
IDX_HEADS = 16
IDX_DIM = 64
TOPK = 256
REL_BUCKETS = 32
REL_MAX_DIST = 128
N_EXPERTS = 16
EXP_PER_GROUP = 4
N_GROUPS = 4
D_EXPERT = 512
PAGE = 128

HGRN_CHUNK = 128
HGRN_HEAD_GROUP = 8
MOE_TILE = 256
IN_PROJ_TN = 640
TAIL_BLOCK = 8 * HEADS * HEAD_DIM // LANES


def _cparams(sem):
    return pltpu.CompilerParams(dimension_semantics=sem, vmem_limit_bytes=VMEM_LIMIT)


def _dot(a, b):
    return jnp.dot(a, b, preferred_element_type=F32)


def _dot_nt(a, b):
    return lax.dot_general(a, b, (((1,), (1,)), ((), ())), preferred_element_type=F32)


def _dot_tn(a, b):
    return lax.dot_general(a, b, (((0,), (0,)), ((), ())), preferred_element_type=F32)


def _sigmoid(x):
    return 1.0 / (1.0 + jnp.exp(-x))


def _norm_mm_body(x_ref, g_ref, w_ref, o_ref, h_ref):
    @pl.when(pl.program_id(1) == 0)
    def _():
        x = x_ref[...]
        ms = jnp.mean(x * x, axis=-1, keepdims=True)
        h_ref[...] = (x * lax.rsqrt(ms + RMS_EPS) * g_ref[...]).astype(BF)

    o_ref[...] = _dot(h_ref[...], w_ref[...])


def norm_matmul(x, g, w_bf, tm, tn):
    m, k = x.shape
    n = w_bf.shape[1]
    return pl.pallas_call(
        _norm_mm_body,
        grid=(m // tm, n // tn),
        in_specs=[pl.BlockSpec((tm, k), lambda i, j: (i, 0)),
                  pl.BlockSpec((1, k), lambda i, j: (0, 0)),
                  pl.BlockSpec((k, tn), lambda i, j: (0, j))],
        out_specs=pl.BlockSpec((tm, tn), lambda i, j: (i, j)),
        out_shape=jax.ShapeDtypeStruct((m, n), F32),
        scratch_shapes=[pltpu.VMEM((tm, k), BF)],
        compiler_params=_cparams(("parallel", "arbitrary")),
        name="norm_matmul",
    )(x, g.reshape(1, k), w_bf)


EVEN_TN = 1024


def _even_proj_body(x_ref, g_ref, w_ref, z_ref, k_ref, v_ref, kv_ref, h_ref, *, per):
    j = pl.program_id(1)

    @pl.when(j == 0)
    def _():
        x = x_ref[...]
        ms = jnp.mean(x * x, axis=-1, keepdims=True)
        h_ref[...] = (x * lax.rsqrt(ms + RMS_EPS) * g_ref[...]).astype(BF)

    res = _dot(h_ref[...], w_ref[...])
    sec = j // per

    @pl.when((sec < 5) | (sec == 7))
    def _():
        z_ref[...] = res

    @pl.when(sec == 5)
    def _():
        k_ref[...] = res
        kv_ref[...] = res.astype(BF)

    @pl.when(sec == 6)
    def _():
        v_ref[...] = res
        kv_ref[...] = res.astype(BF)


def even_projection(x, g, w_bf, tm):
    m, k = x.shape
    width = HEADS * HEAD_DIM
    per = width // EVEN_TN
    clip = lambda a, lo, hi: jnp.minimum(jnp.maximum(a, lo), hi)
    return pl.pallas_call(
        functools.partial(_even_proj_body, per=per),
        grid=(m // tm, 8 * per),
        in_specs=[pl.BlockSpec((tm, k), lambda i, j: (i, 0)),
                  pl.BlockSpec((1, k), lambda i, j: (0, 0)),
                  pl.BlockSpec((k, EVEN_TN), lambda i, j: (0, j))],
        out_specs=[pl.BlockSpec((tm, EVEN_TN), lambda i, j: (i, j - clip(j - (5 * per - 1), 0, 2 * per))),
                   pl.BlockSpec((tm, EVEN_TN), lambda i, j: (i, clip(j - 5 * per, 0, per - 1)),
                                pipeline_mode=pl.Buffered(1)),
                   pl.BlockSpec((tm, EVEN_TN), lambda i, j: (i, clip(j - 6 * per, 0, per - 1)),
                                pipeline_mode=pl.Buffered(1)),
                   pl.BlockSpec((tm, EVEN_TN), lambda i, j: (i, clip(j - 5 * per, 0, 2 * per - 1)),
                                pipeline_mode=pl.Buffered(1))],
        out_shape=[jax.ShapeDtypeStruct((m, 6 * width), F32),
                   jax.ShapeDtypeStruct((m, width), F32),
                   jax.ShapeDtypeStruct((m, width), F32),
                   jax.ShapeDtypeStruct((m, 2 * width), BF)],
        scratch_shapes=[pltpu.VMEM((tm, k), BF)],
        compiler_params=_cparams(("parallel", "arbitrary")),
        name="even_projection",
    )(x, g.reshape(1, k), w_bf)


def _mm_res_body(*refs, n_lhs):
    a_refs = refs[:n_lhs]
    w_refs = refs[n_lhs:2 * n_lhs]
    r_ref = refs[2 * n_lhs]
    o_ref = refs[2 * n_lhs + 1]
    acc = r_ref[...]
    for a_ref, w_ref in zip(a_refs, w_refs):
        acc = acc + _dot(a_ref[...], w_ref[...])
    o_ref[...] = acc


def matmul_residual(lhs, ws_bf, res, tm, tn):
    m, n = res.shape
    n_lhs = len(lhs)
    assert all(a.dtype == BF for a in lhs)
    in_specs = ([pl.BlockSpec((tm, a.shape[1]), lambda i, j: (i, 0)) for a in lhs]
                + [pl.BlockSpec((w.shape[0], tn), lambda i, j: (0, j)) for w in ws_bf]
                + [pl.BlockSpec((tm, tn), lambda i, j: (i, j))])
    return pl.pallas_call(
        functools.partial(_mm_res_body, n_lhs=n_lhs),
        grid=(m // tm, n // tn),
        in_specs=in_specs,
        out_specs=pl.BlockSpec((tm, tn), lambda i, j: (i, j)),
        out_shape=jax.ShapeDtypeStruct((m, n), F32),
        compiler_params=_cparams(("parallel", "arbitrary")),
        name="matmul_residual",
    )(*lhs, *ws_bf, res)


def _hgrn_static(c):
    levels = []
    m = 1
    while m < c:
        levels.append(m)
        m *= 2
    t = np.arange(c)
    rows = [t[None, :] <= t[:, None]]
    masks = [np.eye(c, dtype=bool)]
    for m in levels:
        blk = t // (2 * m)
        pos = t % (2 * m)
        bnd = blk * 2 * m + m - 1
        right = pos >= m
        left = pos < m
        e_rows = (t[None, :] > bnd[:, None]) & (t[None, :] <= t[:, None]) & right[:, None]
        f_rows = (t[None, :] > t[:, None]) & (t[None, :] <= bnd[:, None]) & left[:, None]
        rows.append(e_rows | f_rows)
        masks.append((blk[:, None] == blk[None, :]) & right[:, None] & left[None, :])
    m_all = np.stack(rows).astype(np.float32)
    masks = np.stack(masks).astype(np.float32)
    return m_all, masks, len(levels)


def _hgrn_gates(qa, fa, lb):
    f = lb + (1.0 - lb) * _sigmoid(fa)
    q = qa * _sigmoid(qa)
    return q, f


def _hgrn_prompt_body(qa_ref, fa_ref, ia_ref, ga_ref, lb_ref, gn_ref, mall_ref, masks_ref,
                      oa_ref, st_out_ref, st_ref, *, chunk, tblock, n_levels):
    c = chunk
    t_idx = pl.program_id(1)

    @pl.when(t_idx == 0)
    def _():
        st_ref[...] = jnp.zeros_like(st_ref)

    gn = gn_ref[...]

    def chunk_step(ci, carry):
        r0 = pl.multiple_of(ci * c, c)
        for h0 in range(0, HEADS, HGRN_HEAD_GROUP):
            hs = range(h0, h0 + HGRN_HEAD_GROUP)
            cols = {h: slice(h * HEAD_DIM, (h + 1) * HEAD_DIM) for h in hs}
            q, k, g_hi, g_lo, v = {}, {}, {}, {}, {}
            for h in hs:
                q[h], f = _hgrn_gates(qa_ref[0, pl.ds(r0, c), cols[h]],
                                      fa_ref[0, pl.ds(r0, c), cols[h]], lb_ref[:, cols[h]])
                k[h] = 1.0 - f
                g = jnp.log(f)
                g_hi[h] = g.astype(BF)
                g_lo[h] = (g - g_hi[h].astype(F32)).astype(BF)
                v[h] = ia_ref[0, pl.ds(r0, c), cols[h]].astype(BF)
            b = {h: _dot(mall_ref[0], g_hi[h]) + _dot(mall_ref[0], g_lo[h]) for h in hs}
            o = {h: _dot_nt((q[h] * jnp.exp(b[h])).astype(BF), st_ref[h].astype(BF)) for h in hs}
            a = {h: masks_ref[0] * _dot_nt(q[h].astype(BF), k[h].astype(BF)) for h in hs}
            for li in range(n_levels):
                ml = mall_ref[1 + li]
                w = {h: jnp.exp(_dot(ml, g_hi[h]) + _dot(ml, g_lo[h])) for h in hs}
                for h in hs:
                    a[h] = a[h] + masks_ref[1 + li] * _dot_nt((q[h] * w[h]).astype(BF),
                                                              (k[h] * w[h]).astype(BF))
            for h in hs:
                o[h] = o[h] + _dot(a[h].astype(BF), v[h])
            for h in hs:
                b_last = b[h][c - 1:c]
                k_st = (k[h] * jnp.exp(b_last - b[h])).astype(BF)
                st_ref[h] = st_ref[h] * jnp.exp(b_last) + _dot_tn(v[h], k_st)
            for h in hs:
                ga = ga_ref[0, pl.ds(r0, c), cols[h]]
                ms = jnp.mean(o[h] * o[h], axis=-1, keepdims=True)
                oa_ref[0, pl.ds(r0, c), cols[h]] = (o[h] * lax.rsqrt(ms + RMS_EPS) * gn
                                                    * (ga * _sigmoid(ga))).astype(BF)
        return carry

    lax.fori_loop(0, tblock // c, chunk_step, 0)

    @pl.when(t_idx == pl.num_programs(1) - 1)
    def _():
        st_out_ref[0] = st_ref[...]


def hgrn_prompt(z3, lb, gn, tblock=256, chunk=HGRN_CHUNK):
    bsz, seq = z3.shape[:2]
    width = HEADS * HEAD_DIM
    m_all, masks, n_levels = _hgrn_static(chunk)
    zspec = lambda cb: pl.BlockSpec((1, tblock, width), lambda b, t, cb=cb: (b, t, cb))
    oa, st = pl.pallas_call(
        functools.partial(_hgrn_prompt_body, chunk=chunk, tblock=tblock, n_levels=n_levels),
        grid=(bsz, seq // tblock),
        in_specs=[zspec(0), zspec(1), zspec(2), zspec(3),
                  pl.BlockSpec((1, width), lambda b, t: (0, 0)),
                  pl.BlockSpec((1, HEAD_DIM), lambda b, t: (0, 0)),
                  pl.BlockSpec(m_all.shape, lambda b, t: (0, 0, 0)),
                  pl.BlockSpec(masks.shape, lambda b, t: (0, 0, 0))],
        out_specs=[pl.BlockSpec((1, tblock, width), lambda b, t: (b, t, 0)),
                   pl.BlockSpec((1, HEADS, HEAD_DIM, HEAD_DIM), lambda b, t: (b, 0, 0, 0))],
        out_shape=[jax.ShapeDtypeStruct((bsz, seq, width), BF),
                   jax.ShapeDtypeStruct((bsz, HEADS, HEAD_DIM, HEAD_DIM), F32)],
        scratch_shapes=[pltpu.VMEM((HEADS, HEAD_DIM, HEAD_DIM), F32)],
        compiler_params=_cparams(("parallel", "arbitrary")),
        name="hgrn_prompt",
    )(z3, z3, z3, z3, lb.reshape(1, width), gn.reshape(1, HEAD_DIM),
      jnp.asarray(m_all, BF), jnp.asarray(masks, F32))
    return oa, jnp.swapaxes(st, -1, -2)


def _col(row, eye):
    return jnp.sum(eye * row, axis=1, keepdims=True)


def _hgrn_sample_body(qa_ref, fa_ref, ia_ref, ga_ref, lb_ref, gn_ref, s_ref, oa_ref, so_ref):
    eye = (lax.broadcasted_iota(I32, (HEAD_DIM, HEAD_DIM), 0)
           == lax.broadcasted_iota(I32, (HEAD_DIM, HEAD_DIM), 1)).astype(F32)
    rnd = lambda a: a.astype(BF).astype(F32)
    for g in range(SAMPLE_SEQ_GROUP):
        q8, f8 = _hgrn_gates(qa_ref[g], fa_ref[g], lb_ref[...])
        ga = ga_ref[g]
        gate = ga * _sigmoid(ga)
        kr = rnd(1.0 - f8)
        vr = rnd(ia_ref[g])
        qfr = rnd(q8 * f8)
        qk = rnd(jnp.sum(rnd(q8) * kr, axis=-1, keepdims=True))
        outs = []
        for h in range(HEADS):
            f_col = _col(f8[h:h + 1], eye)
            s_old = s_ref[g, h]
            so_ref[g, h] = f_col * s_old + _col(kr[h:h + 1], eye) * vr[h:h + 1]
            outs.append(qk[h:h + 1] * vr[h:h + 1]
                        + jnp.sum(_col(qfr[h:h + 1], eye) * rnd(s_old), axis=0, keepdims=True))
        o = jnp.concatenate(outs, axis=0)
        ms = jnp.mean(o * o, axis=-1, keepdims=True)
        oa_ref[g] = (o * lax.rsqrt(ms + RMS_EPS) * gn_ref[...] * gate).astype(BF)


def hgrn_sample(zs3, lb, gn, s0):
    bsz = zs3.shape[0]
    g = SAMPLE_SEQ_GROUP
    assert bsz % g == 0
    zspec = lambda cb: pl.BlockSpec((g, HEADS, HEAD_DIM), lambda b, cb=cb: (b, cb, 0))
    sspec = pl.BlockSpec((g, HEADS, HEAD_DIM, HEAD_DIM), lambda b: (b, 0, 0, 0))
    return pl.pallas_call(
        _hgrn_sample_body,
        grid=(bsz // g,),
        in_specs=[zspec(0), zspec(1), zspec(2), zspec(3),
                  pl.BlockSpec((HEADS, HEAD_DIM), lambda b: (0, 0)),
                  pl.BlockSpec((1, HEAD_DIM), lambda b: (0, 0)),
                  sspec],
        out_specs=[pl.BlockSpec((g, HEADS, HEAD_DIM), lambda b: (b, 0, 0)), sspec],
        out_shape=[jax.ShapeDtypeStruct((bsz, HEADS, HEAD_DIM), BF),
                   jax.ShapeDtypeStruct(s0.shape, F32)],
        compiler_params=_cparams(("parallel",)),
        name="hgrn_sample",
    )(zs3, zs3, zs3, zs3, lb.reshape(HEADS, HEAD_DIM), gn.reshape(1, HEAD_DIM), s0)


_KEY_NEG_INF = np.int32(np.uint32(0x807FFFFF).astype(np.int64) - (1 << 32))
_INT_MIN = np.int32(-(1 << 31))


def _count(u_ref, n_groups, group, thr, cmp):
    rows = u_ref.shape[1]
    step = min(rows, LANES)
    parts = []
    for r0 in range(0, rows, step):
        t = jnp.broadcast_to(thr[r0:r0 + step], (step, LANES))

        def body(gi, acc, r0=r0, t=t):
            for i in range(group):
                acc = acc + cmp(u_ref[gi * group + i, r0:r0 + step], t).astype(F32)
            return acc

        acc = lax.fori_loop(0, n_groups, body, jnp.zeros((step, LANES), F32))
        parts.append(jnp.sum(acc, axis=-1, keepdims=True))
    return parts[0] if len(parts) == 1 else jnp.concatenate(parts, axis=0)


def _topk_mask(score_ref, u_ref, write_tile, n_tiles, n_groups, group, k):
    rows = score_ref.shape[1]
    n_live = n_groups * group

    def to_key(t, carry):
        bits = pltpu.bitcast(score_ref[t], I32)
        u_ref[t] = jnp.where(bits < 0, bits ^ np.int32(0x7FFFFFFF), bits)
        return carry

    lax.fori_loop(0, n_live, to_key, 0)

    kf = float(k)
    ge = lambda u, t: u >= t
    cnt = _count(u_ref, n_groups, group, jnp.zeros((rows, 1), I32), ge)
    lo = jnp.where(cnt >= kf, np.int32(0), _INT_MIN)

    def bit_step(i, lo):
        cand = lo | (np.int32(1) << (30 - i))
        cnt = _count(u_ref, n_groups, group, cand, ge)
        return jnp.where(cnt >= kf, cand, lo)

    lo = lax.fori_loop(0, 31, bit_step, lo)
    c_gt = _count(u_ref, n_groups, group, lo, lambda u, t: u > t)
    c_eq = _count(u_ref, n_groups, group, lo, lambda u, t: u == t)
    need = kf - c_gt
    real = lo > _KEY_NEG_INF
    excess = jnp.where(real & (c_eq > need), 1.0, 0.0)
    any_excess = jnp.max(excess) > 0.0

    @pl.when(jnp.logical_not(any_excess))
    def _():
        def emit(t, carry):
            u = u_ref[t]
            write_tile(t, jnp.where((u >= lo) & (u > _KEY_NEG_INF), 1.0, 0.0))
            return carry

        lax.fori_loop(0, n_live, emit, 0)

    @pl.when(any_excess)
    def _():
        upper = (lax.broadcasted_iota(I32, (LANES, LANES), 0)
                 <= lax.broadcasted_iota(I32, (LANES, LANES), 1)).astype(BF)

        def emit(t, seen):
            u = u_ref[t]
            eq = jnp.where(u == lo, 1.0, 0.0)
            prefix = seen + _dot(eq.astype(BF), upper)
            take = (u > lo) | ((u == lo) & (prefix <= need))
            write_tile(t, jnp.where(take & (u > _KEY_NEG_INF), 1.0, 0.0))
            return seen + jnp.sum(eq, axis=-1, keepdims=True)

        lax.fori_loop(0, n_live, emit, jnp.zeros((rows, 1), F32))

    def clear(t, carry):
        write_tile(t, jnp.zeros((rows, LANES), F32))
        return carry

    lax.fori_loop(n_live, n_tiles, clear, 0)


QW = 256
SUB = 8
IDX_KCHUNK = 512
ATTN_KSTEP = 512
ATTN_TILES = ATTN_KSTEP // LANES


def _row_all(x8, op):
    return jnp.broadcast_to(op(x8, axis=0, keepdims=True), x8.shape)


def _count_t(u_ref, n_groups, group, thr_row, cmp):
    qw = u_ref.shape[2]
    thr = jnp.broadcast_to(thr_row, (SUB, qw))

    def body(gi, acc):
        for i in range(group):
            u = u_ref[gi * group + i].reshape(LANES // SUB, SUB, qw)
            acc = acc + jnp.sum(cmp(u, thr[None]).astype(F32), axis=0)
        return acc

    acc = lax.fori_loop(0, n_groups, body, jnp.zeros((SUB, qw), F32))
    return jnp.sum(acc, axis=0, keepdims=True)


def _topk_mask_t(score_ref, u_ref, write_tile, n_tiles, n_groups, group, k):
    qw = score_ref.shape[2]
    n_live = n_groups * group

    def to_key(t, carry):
        bits = pltpu.bitcast(score_ref[t], I32)
        u_ref[t] = jnp.where(bits < 0, bits ^ np.int32(0x7FFFFFFF), bits)
        return carry

    lax.fori_loop(0, n_live, to_key, 0)

    kf = float(k)
    ge = lambda u, t: u >= t
    cnt = _count_t(u_ref, n_groups, group, jnp.zeros((1, qw), I32), ge)
    lo = jnp.where(cnt >= kf, np.int32(0), _INT_MIN)

    def bit_step(i, lo):
        cand = lo | (np.int32(1) << (30 - i))
        cnt = _count_t(u_ref, n_groups, group, cand, ge)
        return jnp.where(cnt >= kf, cand, lo)

    lo = lax.fori_loop(0, 31, bit_step, lo)
    c_gt = _count_t(u_ref, n_groups, group, lo, lambda u, t: u > t)
    c_eq = _count_t(u_ref, n_groups, group, lo, lambda u, t: u == t)
    need = kf - c_gt
    excess = jnp.where((lo > _KEY_NEG_INF) & (c_eq > need), 1.0, 0.0)
    any_excess = jnp.max(excess) > 0.0
    lo_b = jnp.broadcast_to(lo, (LANES, qw))

    @pl.when(jnp.logical_not(any_excess))
    def _():
        def emit(t, carry):
            u = u_ref[t]
            write_tile(t, jnp.where((u >= lo_b) & (u > _KEY_NEG_INF), 1.0, 0.0))
            return carry

        lax.fori_loop(0, n_live, emit, 0)

    @pl.when(any_excess)
    def _():
        lower = (lax.broadcasted_iota(I32, (LANES, LANES), 1)
                 <= lax.broadcasted_iota(I32, (LANES, LANES), 0)).astype(BF)

        def emit(t, seen):
            u = u_ref[t]
            eq = jnp.where(u == lo_b, 1.0, 0.0)
            prefix = seen + _dot(lower, eq.astype(BF))
            take = (u > lo_b) | ((u == lo_b) & (prefix <= need))
            write_tile(t, jnp.where(take & (u > _KEY_NEG_INF), 1.0, 0.0))
            return seen + jnp.sum(eq, axis=0, keepdims=True)

        lax.fori_loop(0, n_live, emit, jnp.zeros((1, qw), F32))

    def clear(t, carry):
        write_tile(t, jnp.zeros((LANES, qw), F32))
        return carry

    lax.fori_loop(n_live, n_tiles, clear, 0)


def _indexer_t_body(qi_ref, wt_ref, tail_ref, mask_ref, score_ref, u_ref, *, seq):
    j = pl.program_id(1)
    n_tiles = seq // LANES
    tiles_per_chunk = IDX_KCHUNK // LANES
    qi = qi_ref[0].astype(BF)
    wt = wt_ref[0] * (IDX_DIM ** -0.5 * IDX_HEADS ** -0.5)

    n_chunks = (j * QW + QW - 1) // IDX_KCHUNK + 1
    q_pos = j * QW + lax.broadcasted_iota(I32, (IDX_KCHUNK, QW), 1)

    def chunk_step(ci, carry):
        k0 = pl.multiple_of(ci * IDX_KCHUNK, IDX_KCHUNK)
        kic = tail_ref[0, pl.ds(k0, IDX_KCHUNK), :][:, 0:IDX_DIM].astype(BF)
        acc = jnp.zeros((IDX_KCHUNK, QW), F32)
        for h in range(IDX_HEADS):
            s = _dot_nt(kic, qi[:, h * IDX_DIM:(h + 1) * IDX_DIM])
            acc = acc + wt[h:h + 1] * jnp.maximum(s, 0.0)
        k_pos = k0 + lax.broadcasted_iota(I32, (IDX_KCHUNK, QW), 0)
        acc = jnp.where(k_pos <= q_pos, acc, -jnp.inf)
        for i in range(tiles_per_chunk):
            score_ref[ci * tiles_per_chunk + i] = acc[i * LANES:(i + 1) * LANES]
        return carry

    lax.fori_loop(0, n_chunks, chunk_step, 0)

    def write_tile(t, m):
        mask_ref[0, 0, t] = m.astype(BF)

    _topk_mask_t(score_ref, u_ref, write_tile, n_tiles, n_chunks, tiles_per_chunk, TOPK)


def indexer_prompt_t(z3, wt, tail3):
    bsz, seq = z3.shape[:2]
    n_tiles = seq // LANES
    assert seq % QW == 0 and seq % IDX_KCHUNK == 0
    return pl.pallas_call(
        functools.partial(_indexer_t_body, seq=seq),
        grid=(bsz, seq // QW),
        in_specs=[pl.BlockSpec((1, QW, IDX_HEADS * IDX_DIM), lambda b, j: (b, j, 5)),
                  pl.BlockSpec((1, IDX_HEADS, QW), lambda b, j: (b, 0, j)),
                  pl.BlockSpec((1, seq, LANES), lambda b, j: (b, 0, 0))],
        out_specs=pl.BlockSpec((1, 1, n_tiles, LANES, QW), lambda b, j: (b, j, 0, 0, 0)),
        out_shape=jax.ShapeDtypeStruct((bsz, seq // QW, n_tiles, LANES, QW), BF),
        scratch_shapes=[pltpu.VMEM((n_tiles, LANES, QW), F32),
                        pltpu.VMEM((n_tiles, LANES, QW), I32)],
        compiler_params=_cparams(("parallel", "arbitrary")),
        name="indexer_prompt_t",
    )(z3, wt, tail3)


def _attn_t_body(bfar_ref, q_ref, k_ref, vt_ref, mask_ref, bias_ref, o_ref, m_ref, l_ref, acc_ref):
    j = pl.program_id(1)
    scale = HEAD_DIM ** -0.5
    qsub = QW // LANES
    q = q_ref[0].astype(BF)
    m_ref[...] = jnp.full(m_ref.shape, NEG_BIG, F32)
    l_ref[...] = jnp.zeros(l_ref.shape, F32)
    acc_ref[...] = jnp.zeros(acc_ref.shape, F32)

    def process(sb, near):
        k0 = pl.multiple_of(sb * ATTN_KSTEP, ATTN_KSTEP)
        kblk = k_ref[0, pl.ds(k0, ATTN_KSTEP), :]
        vblk = vt_ref[0, pl.ds(k0, ATTN_KSTEP), :]
        tiles = [sb * ATTN_TILES + i for i in range(ATTN_TILES)]
        sel = jnp.concatenate([mask_ref[0, 0, t] for t in tiles], axis=0).astype(F32) > 0.0
        if near:
            def tile_bias(t, h):
                row = []
                for s in range(qsub):
                    d = j * qsub + s - t
                    row.append(bias_ref[jnp.where(d == 0, 0, jnp.where(d == 1, 1, 2)), h])
                return jnp.concatenate(row, axis=1)
        head_cols = [slice(h * HEAD_DIM, (h + 1) * HEAD_DIM) for h in range(HEADS)]
        qk = lambda h: _dot_nt(kblk[:, head_cols[h]], q[:, head_cols[h]])
        lg_next = qk(0)
        for h in range(HEADS):
            cols = head_cols[h]
            lg = lg_next * scale
            if h + 1 < HEADS:
                lg_next = qk(h + 1)
            if near:
                lg = lg + jnp.concatenate([tile_bias(t, h) for t in tiles], axis=0)
            else:
                lg = lg + bfar_ref[h]
            lg = jnp.where(sel, lg, NEG_BIG)
            m_old = m_ref[h]
            part = jnp.max(lg.reshape(ATTN_KSTEP // SUB, SUB, QW), axis=0)
            m_new = jnp.maximum(m_old, _row_all(part, jnp.max))
            p = jnp.exp(lg - m_new[0:1])
            alpha = jnp.exp(m_old - m_new)
            psum = jnp.sum(p.reshape(ATTN_KSTEP // SUB, SUB, QW), axis=0)
            l_ref[h] = alpha * l_ref[h] + _row_all(psum, jnp.sum)
            pv = _dot_tn(vblk[:, cols], p.astype(BF))
            acc_ref[cols, :] = alpha[0:1] * acc_ref[cols, :] + pv
            m_ref[h] = m_new

    n_far = jnp.maximum((qsub * j - 1) // ATTN_TILES, 0)
    n_steps = (qsub * j + qsub - 1) // ATTN_TILES + 1

    def far_step(sb, carry):
        process(sb, False)
        return carry

    def near_step(sb, carry):
        process(sb, True)
        return carry

    lax.fori_loop(0, n_far, far_step, 0)
    lax.fori_loop(n_far, n_steps, near_step, 0)

    for h in range(HEADS):
        cols = slice(h * HEAD_DIM, (h + 1) * HEAD_DIM)
        o_ref[0, cols, :] = (acc_ref[cols, :] / l_ref[h][0:1]).astype(BF)


def attn_prompt_t(z3, kv_bf, mask_t, bias_tiles_t, bias_far):
    bsz, seq = z3.shape[:2]
    n_tiles = seq // LANES
    width = HEADS * HEAD_DIM
    assert seq % ATTN_KSTEP == 0 and seq % QW == 0
    return pl.pallas_call(
        _attn_t_body,
        grid=(bsz, seq // QW),
        in_specs=[pl.BlockSpec(memory_space=pltpu.SMEM),
                  pl.BlockSpec((1, QW, width), lambda b, j: (b, j, 4)),
                  pl.BlockSpec((1, seq, width), lambda b, j: (b, 0, 0)),
                  pl.BlockSpec((1, seq, width), lambda b, j: (b, 0, 1)),
                  pl.BlockSpec((1, 1, n_tiles, LANES, QW), lambda b, j: (b, j, 0, 0, 0)),
                  pl.BlockSpec(bias_tiles_t.shape, lambda b, j: (0, 0, 0, 0))],
        out_specs=pl.BlockSpec((1, width, QW), lambda b, j: (b, 0, j)),
        out_shape=jax.ShapeDtypeStruct((bsz, width, seq), BF),
        scratch_shapes=[pltpu.VMEM((HEADS, SUB, QW), F32),
                        pltpu.VMEM((HEADS, SUB, QW), F32),
                        pltpu.VMEM((width, QW), F32)],
        compiler_params=_cparams(("parallel", "arbitrary")),
        name="attn_prompt_t",
    )(bias_far, z3, kv_bf, kv_bf, mask_t, bias_tiles_t)


def _indexer_sample_body(pt_ref, qi_ref, wcol_ref, kinew_ref, *rest, n_pages):
    ki_refs = rest[:n_pages]
    out_ref = rest[n_pages]
    qi = qi_ref[0].astype(BF)
    w = wcol_ref[0] * (IDX_DIM ** -0.5 * IDX_HEADS ** -0.5)
    for i in range(n_pages):
        s = _dot(qi, ki_refs[i][0].astype(BF))
        out_ref[i, 0] = jnp.sum(w * jnp.maximum(s, 0.0), axis=0, keepdims=True)
    kn = kinew_ref[0].astype(BF).astype(F32)
    sn = jnp.sum(qi.astype(F32) * kn, axis=-1, keepdims=True)
    new = jnp.sum(w * jnp.maximum(sn, 0.0), axis=0, keepdims=True)
    lane = lax.broadcasted_iota(I32, (1, LANES), 1)
    out_ref[n_pages, 0] = jnp.where(lane == 0, new, -jnp.inf)


def indexer_sample(page_table, qi3, wcol, ki_new, ki_pool_t):
    bsz, n_pages = page_table.shape
    ki_spec = lambda i: pl.BlockSpec((1, IDX_DIM, PAGE), lambda b, pt, i=i: (pt[b, i], 0, 0))
    grid_spec = pltpu.PrefetchScalarGridSpec(
        num_scalar_prefetch=1,
        grid=(bsz,),
        in_specs=[pl.BlockSpec((1, IDX_HEADS, IDX_DIM), lambda b, pt: (b, 0, 0)),
                  pl.BlockSpec((1, IDX_HEADS, 1), lambda b, pt: (b, 0, 0)),
                  pl.BlockSpec((1, 1, IDX_DIM), lambda b, pt: (b, 0, 0))]
                 + [ki_spec(i) for i in range(n_pages)],
        out_specs=pl.BlockSpec((n_pages + 1, 1, 1, LANES), lambda b, pt: (0, b, 0, 0)),
    )
    out = pl.pallas_call(
        functools.partial(_indexer_sample_body, n_pages=n_pages),
        grid_spec=grid_spec,
        out_shape=jax.ShapeDtypeStruct((n_pages + 1, bsz, 1, LANES), F32),
        compiler_params=_cparams(("arbitrary",)),
        name="indexer_sample",
    )(page_table, qi3, wcol, ki_new, *([ki_pool_t] * n_pages))
    return out.reshape(n_pages + 1, bsz, LANES)


def _select_body(score_ref, mask_ref, u_ref, *, n_tiles, k):
    def write_tile(t, m):
        mask_ref[t] = m

    _topk_mask(score_ref, u_ref, write_tile, n_tiles, n_tiles, 1, k)


def select_topk(scores, k):
    n_tiles, rows, _ = scores.shape
    return pl.pallas_call(
        functools.partial(_select_body, n_tiles=n_tiles, k=k),
        grid=(1,),
        in_specs=[pl.BlockSpec(scores.shape, lambda i: (0, 0, 0))],
        out_specs=pl.BlockSpec(scores.shape, lambda i: (0, 0, 0)),
        out_shape=jax.ShapeDtypeStruct(scores.shape, F32),
        scratch_shapes=[pltpu.VMEM(scores.shape, I32)],
        compiler_params=_cparams(("arbitrary",)),
        name="select_topk",
    )(scores)


SC_CORES = 2
SC_SUBCORES = 16
SC_GATHER_CHUNK = 32


def sc_gather_rows(tables, idx):
    n_rows = idx.shape[0]
    workers = SC_CORES * SC_SUBCORES
    per_worker = n_rows // workers
    chunk = SC_GATHER_CHUNK
    assert n_rows % workers == 0 and per_worker % chunk == 0 and chunk % 8 == 0
    row_shape = tables[0].shape[1:]
    n_tab = len(tables)
    mesh = plsc.VectorSubcoreMesh(core_axis_name="c", subcore_axis_name="s",
                                  num_cores=SC_CORES, num_subcores=SC_SUBCORES)

    def body(*refs):
        tab_refs = refs[:n_tab]
        idx_hbm = refs[n_tab]
        out_refs = refs[n_tab + 1:2 * n_tab + 1]
        idx_v, rows_v, sem = refs[2 * n_tab + 1:]
        wid = lax.axis_index("s") * SC_CORES + lax.axis_index("c")

        @pl.loop(0, per_worker // chunk)
        def _(ci):
            off = pl.multiple_of(wid * per_worker + ci * chunk, 8)
            pltpu.sync_copy(idx_hbm.at[pl.ds(off, chunk)], idx_v)
            for tab, out in zip(tab_refs, out_refs):
                pltpu.async_copy(tab.at[idx_v], rows_v, sem).wait()
                pltpu.sync_copy(rows_v, out.at[pl.ds(off, chunk)])

    return pl.kernel(
        body,
        out_type=[jax.ShapeDtypeStruct((n_rows, *row_shape), t.dtype) for t in tables],
        mesh=mesh,
        scratch_types=[pltpu.VMEM((chunk,), I32),
                       pltpu.VMEM((chunk, *row_shape), tables[0].dtype),
                       pltpu.SemaphoreType.DMA],
        compiler_params=pltpu.CompilerParams(use_tc_tiling_on_sc=True),
        name="sc_gather_rows",
    )(*tables, idx)


SAMPLE_SEQ_GROUP = 8


def _attn_compact_body(dist_ref, near_ref, q_ref, kn_ref, vn_ref, btab_ref, kc_ref, vc_ref,
                       o_ref, bbuf):
    step = pl.program_id(0)
    scale = HEAD_DIM ** -0.5
    ones = jnp.ones((HEAD_DIM, LANES), BF)
    far_bias = btab_ref[REL_MAX_DIST]
    last = lax.broadcasted_iota(I32, (TOPK, 1, 1), 0) == TOPK - 1
    for g in range(SAMPLE_SEQ_GROUP):
        b = step * SAMPLE_SEQ_GROUP + g
        bbuf[g] = jnp.broadcast_to(far_bias[None], (TOPK, HEADS, LANES))

        def fill(t, carry, g=g, b=b):
            bbuf[g, t] = btab_ref[dist_ref[b, t]]
            return carry

        lax.fori_loop(near_ref[b], TOPK, fill, 0)

        is_new = dist_ref[b, TOPK - 1] == 0
        rows = slice(g * TOPK, (g + 1) * TOPK)
        rnd = lambda a: a.astype(BF).astype(F32)
        kc = rnd(jnp.where(last & is_new, kn_ref[g][None], kc_ref[rows]))
        vc = rnd(jnp.where(last & is_new, vn_ref[g][None], vc_ref[rows]))
        prod = (kc * rnd(q_ref[g])[None]).reshape(TOPK * HEADS, HEAD_DIM)
        hi = prod.astype(BF)
        lo = (prod - hi.astype(F32)).astype(BF)
        lg = (_dot(hi, ones) + _dot(lo, ones)).reshape(TOPK, HEADS, LANES) * scale + bbuf[g]
        m = jnp.max(lg, axis=0, keepdims=True)
        p = jnp.exp(lg - m)
        l = jnp.sum(p, axis=0)
        o_ref[g] = (jnp.sum(rnd(p) * vc, axis=0) / l).astype(BF)


def attn_sample_compact(dist, first_near, zs3, k_sel, v_sel, bias_by_dist):
    bsz = dist.shape[0]
    g = SAMPLE_SEQ_GROUP
    assert bsz % g == 0
    zspec = lambda cb: pl.BlockSpec((g, HEADS, HEAD_DIM), lambda i, d, n, cb=cb: (i, cb, 0))
    sel_spec = pl.BlockSpec((g * TOPK, HEADS, HEAD_DIM), lambda i, d, n: (i, 0, 0))
    grid_spec = pltpu.PrefetchScalarGridSpec(
        num_scalar_prefetch=2,
        grid=(bsz // g,),
        in_specs=[zspec(4), zspec(5), zspec(6),
                  pl.BlockSpec(bias_by_dist.shape, lambda i, d, n: (0, 0, 0)),
                  sel_spec, sel_spec],
        out_specs=pl.BlockSpec((g, HEADS, HEAD_DIM), lambda i, d, n: (i, 0, 0)),
        scratch_shapes=[pltpu.VMEM((g, TOPK, HEADS, HEAD_DIM), F32)],
    )
    return pl.pallas_call(
        _attn_compact_body,
        grid_spec=grid_spec,
        out_shape=jax.ShapeDtypeStruct((bsz, HEADS, HEAD_DIM), BF),
        compiler_params=_cparams(("arbitrary",)),
        name="attn_sample_compact",
    )(dist, first_near, zs3, zs3, zs3, bias_by_dist, k_sel, v_sel)


def _bucket_table(max_dist):
    exact = REL_BUCKETS // 2
    d = np.arange(max_dist + 1)
    df = np.maximum(d, 1).astype(np.float32)
    far = exact + (np.log(df / exact) / np.float32(math.log(REL_MAX_DIST / exact))
                   * (REL_BUCKETS - exact)).astype(np.int32)
    return np.where(d < exact, d, np.minimum(far, REL_BUCKETS - 1)).astype(np.int32)


def _bias_tables(rel_bias):
    tab = _bucket_table(2 * LANES)
    assert np.all(tab[REL_MAX_DIST:] == REL_BUCKETS - 1)
    i = np.arange(LANES)
    dist0 = np.maximum(i[:, None] - i[None, :], 0)
    dist1 = LANES + i[:, None] - i[None, :]
    far = np.full((LANES, LANES), REL_BUCKETS - 1)
    idx = np.stack([tab[dist0], tab[dist1], far])
    buckets = jnp.arange(REL_BUCKETS)
    lookup = lambda ix, spec: jnp.einsum(spec, (jnp.asarray(ix)[..., None] == buckets).astype(F32),
                                         rel_bias.astype(F32), precision=lax.Precision.HIGHEST)
    tiles = lookup(idx, "tijb,bh->thij")
    by_dist = jnp.broadcast_to(lookup(tab[:REL_MAX_DIST + 1], "db,bh->dh")[:, :, None],
                               (REL_MAX_DIST + 1, HEADS, LANES))
    return tiles.astype(F32), rel_bias[REL_BUCKETS - 1].astype(F32), by_dist.astype(F32)


def _conv_prompt_body(bg_ref, cg_ref, xt_ref, w_ref, v_ref, st_ref, carry_ref, *, tblock):
    t = pl.program_id(1)

    @pl.when(t == 0)
    def _():
        carry_ref[...] = jnp.zeros_like(carry_ref)

    u = cg_ref[0] * xt_ref[0]
    row = lax.broadcasted_iota(I32, u.shape, 0)
    c0 = carry_ref[0:1]
    c1 = carry_ref[1:2]
    u1 = jnp.where(row == 0, c1, pltpu.roll(u, 1, axis=0))
    u2 = jnp.where(row == 0, c0, jnp.where(row == 1, c1, pltpu.roll(u, 2, axis=0)))
    conv = w_ref[0:1] * u2 + w_ref[1:2] * u1 + w_ref[2:3] * u
    v_ref[0] = (bg_ref[0] * conv).astype(BF)
    last = u[tblock - 2:tblock]
    carry_ref[0:2] = last

    @pl.when(t == pl.num_programs(1) - 1)
    def _():
        st_ref[0] = last


def conv_prompt(zc3, w_conv, tblock=256):
    bsz, seq = zc3.shape[:2]
    c = zc3.shape[2] // 3
    zspec = lambda cb: pl.BlockSpec((1, tblock, c), lambda b, t, cb=cb: (b, t, cb))
    return pl.pallas_call(
        functools.partial(_conv_prompt_body, tblock=tblock),
        grid=(bsz, seq // tblock),
        in_specs=[zspec(0), zspec(1), zspec(2), pl.BlockSpec((3, c), lambda b, t: (0, 0))],
        out_specs=[pl.BlockSpec((1, tblock, c), lambda b, t: (b, t, 0)),
                   pl.BlockSpec((1, 2, c), lambda b, t: (b, 0, 0))],
        out_shape=[jax.ShapeDtypeStruct((bsz, seq, c), BF),
                   jax.ShapeDtypeStruct((bsz, 2, c), F32)],
        scratch_shapes=[pltpu.VMEM((8, c), F32)],
        compiler_params=_cparams(("parallel", "arbitrary")),
        name="conv_prompt",
    )(zc3, zc3, zc3, w_conv)


def _conv_sample_body(bg_ref, cg_ref, xt_ref, w_ref, s0_ref, s1_ref, v_ref, n0_ref, n1_ref):
    u = cg_ref[...] * xt_ref[...]
    conv = w_ref[0:1] * s0_ref[...] + w_ref[1:2] * s1_ref[...] + w_ref[2:3] * u
    v_ref[...] = (bg_ref[...] * conv).astype(BF)
    n0_ref[...] = s1_ref[...]
    n1_ref[...] = u


def conv_sample(zc, w_conv, s0, s1):
    bsz = zc.shape[0]
    c = zc.shape[1] // 3
    zspec = lambda cb: pl.BlockSpec((bsz, c), lambda i, cb=cb: (0, cb))
    full = pl.BlockSpec((bsz, c), lambda i: (0, 0))
    return pl.pallas_call(
        _conv_sample_body,
        grid=(1,),
        in_specs=[zspec(0), zspec(1), zspec(2), pl.BlockSpec((3, c), lambda i: (0, 0)), full, full],
        out_specs=[full, full, full],
        out_shape=[jax.ShapeDtypeStruct((bsz, c), BF)] + [jax.ShapeDtypeStruct((bsz, c), F32)] * 2,
        compiler_params=_cparams(("arbitrary",)),
        name="conv_sample",
    )(zc, zc, zc, w_conv, s0, s1)


def _router_body(x_ref, g_ref, wr_ref, br_ref, *rest):
    h_ref, route_ref = rest[-2:]
    x = x_ref[...]
    ms = jnp.mean(x * x, axis=-1, keepdims=True)
    h = x * lax.rsqrt(ms + RMS_EPS) * g_ref[...]
    hb = h.astype(BF)
    half = h.shape[1] // 2
    bits = pltpu.bitcast(hb.astype(F32), jnp.uint32)
    h_ref[...] = (bits[:, :half] & jnp.uint32(0xFFFF0000)) | (bits[:, half:] >> 16)
    logits = _dot(hb, wr_ref[...]) + br_ref[...]
    lane = lax.broadcasted_iota(I32, logits.shape, 1)
    big = np.int32(1 << 20)
    lg = jnp.where(lane < N_GROUPS, logits, -jnp.inf)
    g_max = jnp.max(lg, axis=-1, keepdims=True)
    g_idx = jnp.min(jnp.where(lg == g_max, lane, big), axis=-1, keepdims=True)
    g_w = 1.0 / jnp.sum(jnp.exp(lg - g_max), axis=-1, keepdims=True)
    first = N_GROUPS + EXP_PER_GROUP * g_idx
    le = jnp.where((lane >= first) & (lane < first + EXP_PER_GROUP), logits, -jnp.inf)
    l1 = jnp.max(le, axis=-1, keepdims=True)
    i1 = jnp.min(jnp.where(le == l1, lane, big), axis=-1, keepdims=True)
    le2 = jnp.where(lane == i1, -jnp.inf, le)
    l2 = jnp.max(le2, axis=-1, keepdims=True)
    i2 = jnp.min(jnp.where(le2 == l2, lane, big), axis=-1, keepdims=True)
    r = jnp.exp(l2 - l1)
    w1 = g_w / (1.0 + r)
    w2 = g_w * r / (1.0 + r)
    e1 = (i1 - N_GROUPS).astype(F32)
    e2 = (i2 - N_GROUPS).astype(F32)
    route_ref[...] = jnp.where(lane == 0, e1, jnp.where(lane == 1, e2,
                               jnp.where(lane == 2, w1, jnp.where(lane == 3, w2, 0.0))))


def moe_router(x, g, wr, br, tm, total_rows, row0, bufs=None):
    m, k = x.shape
    assert row0 % tm == 0
    off = row0 // tm
    in_specs = [pl.BlockSpec((tm, k), lambda i: (i, 0)),
                pl.BlockSpec((1, k), lambda i: (0, 0)),
                pl.BlockSpec((k, LANES), lambda i: (0, 0)),
                pl.BlockSpec((1, LANES), lambda i: (0, 0))]
    args = [x, g.reshape(1, k), wr, br]
    aliases = {}
    if bufs is not None:
        in_specs += [pl.BlockSpec(memory_space=pl.ANY)] * 2
        args += list(bufs)
        aliases = {4: 0, 5: 1}
    return pl.pallas_call(
        _router_body,
        grid=(m // tm,),
        in_specs=in_specs,
        out_specs=[pl.BlockSpec((tm, k // 2), lambda i: (i + off, 0)),
                   pl.BlockSpec((tm, LANES), lambda i: (i + off, 0))],
        out_shape=[jax.ShapeDtypeStruct((total_rows, k // 2), jnp.uint32),
                   jax.ShapeDtypeStruct((total_rows, LANES), F32)],
        input_output_aliases=aliases,
        compiler_params=_cparams(("parallel",)),
        name="moe_router",
    )(*args)


def _experts_body(te_ref, nu_ref, h_ref, wg_ref, wu_ref, wd_ref, o_ref, wg_bf, wu_bf, wd_bf):
    i = pl.program_id(0)

    @pl.when((i == 0) | (te_ref[i] != te_ref[jnp.maximum(i - 1, 0)]))
    def _():
        wg_bf[...] = wg_ref[0].astype(BF)
        wu_bf[...] = wu_ref[0].astype(BF)
        wd_bf[...] = wd_ref[0].astype(BF)

    @pl.when(i < nu_ref[0])
    def _():
        words = h_ref[...]
        left = pltpu.bitcast(words & jnp.uint32(0xFFFF0000), F32)
        right = pltpu.bitcast(words << 16, F32)
        h = jnp.concatenate([left, right], axis=1).astype(BF)
        a = _dot(h, wg_bf[...])
        b = _dot(h, wu_bf[...])
        hid = a * _sigmoid(a) * b
        o_ref[...] = _dot(hid.astype(BF), wd_bf[...])

    @pl.when(i >= nu_ref[0])
    def _():
        o_ref[...] = jnp.zeros_like(o_ref)


def moe_experts(tile_expert, n_used, hs, wg, wu, wd):
    p = hs.shape[0]
    k = wg.shape[1]
    f = wg.shape[2]
    n_tiles = p // MOE_TILE
    grid_spec = pltpu.PrefetchScalarGridSpec(
        num_scalar_prefetch=2,
        grid=(n_tiles,),
        in_specs=[pl.BlockSpec((MOE_TILE, k // 2), lambda i, te, nu: (i, 0)),
                  pl.BlockSpec((1, k, f), lambda i, te, nu: (te[i], 0, 0)),
                  pl.BlockSpec((1, k, f), lambda i, te, nu: (te[i], 0, 0)),
                  pl.BlockSpec((1, f, k), lambda i, te, nu: (te[i], 0, 0))],
        out_specs=pl.BlockSpec((MOE_TILE, k), lambda i, te, nu: (i, 0)),
        scratch_shapes=[pltpu.VMEM((k, f), BF), pltpu.VMEM((k, f), BF), pltpu.VMEM((f, k), BF)],
    )
    return pl.pallas_call(
        _experts_body,
        grid_spec=grid_spec,
        out_shape=jax.ShapeDtypeStruct((p, k), F32),
        compiler_params=_cparams(("arbitrary",)),
        name="moe_experts",
    )(tile_expert, n_used, hs, wg, wu, wd)


def _combine_body(x_ref, a_ref, b_ref, route_ref, g_ref, o_ref, *, normalize):
    y = x_ref[...] + route_ref[:, 2:3] * a_ref[...] + route_ref[:, 3:4] * b_ref[...]
    if normalize:
        ms = jnp.mean(y * y, axis=-1, keepdims=True)
        y = y * lax.rsqrt(ms + RMS_EPS) * g_ref[...]
    o_ref[...] = y


def moe_combine(x, out_a, out_b, route, tm, norm_g=None):
    m, k = x.shape
    row = pl.BlockSpec((tm, k), lambda i: (i, 0))
    g = jnp.ones((1, k), F32) if norm_g is None else norm_g.reshape(1, k)
    return pl.pallas_call(
        functools.partial(_combine_body, normalize=norm_g is not None),
        grid=(m // tm,),
        in_specs=[row, row, row, pl.BlockSpec((tm, LANES), lambda i: (i, 0)),
                  pl.BlockSpec((1, k), lambda i: (0, 0))],
        out_specs=row,
        out_shape=jax.ShapeDtypeStruct((m, k), F32),
        compiler_params=_cparams(("parallel",)),
        name="moe_combine",
    )(x, out_a, out_b, route, g)


def _rank_within_expert(onehot):
    n, e = onehot.shape
    blk = LANES
    assert n % blk == 0
    oh = onehot.astype(F32).reshape(n // blk, blk, e)
    strict = jnp.asarray(np.tril(np.ones((blk, blk), np.float32), -1))
    within = jnp.einsum("ij,bjk->bik", strict, oh, precision=lax.Precision.HIGHEST)
    totals = jnp.sum(oh, axis=1)
    before = jnp.cumsum(totals, axis=0) - totals
    rank = (within + before[:, None, :]).reshape(n, e)
    return jnp.sum(rank * onehot.astype(F32), axis=1).astype(I32), jnp.sum(totals, axis=0).astype(I32)


def hier_moe(xs, tms, g, wrg, brg, wre, bre, wg, wu, wd, layer, out_norm_g=None):
    d = xs[0].shape[1]
    m = sum(x.shape[0] for x in xs)
    wr = jnp.zeros((d, LANES), F32)
    wr = wr.at[:, :N_GROUPS].set(wrg).at[:, N_GROUPS:N_GROUPS + N_EXPERTS].set(wre.reshape(d, N_EXPERTS))
    br = jnp.zeros((1, LANES), F32)
    br = br.at[0, :N_GROUPS].set(brg).at[0, N_GROUPS:N_GROUPS + N_EXPERTS].set(bre.reshape(N_EXPERTS))
    bufs, row0 = None, 0
    for x, tm in zip(xs, tms):
        bufs = moe_router(x, g, wr.astype(BF), br, tm, m, row0, bufs)
        row0 += x.shape[0]
    h_bf, route = bufs

    eid = route[:, 0:2].astype(I32).reshape(-1)
    onehot = eid[:, None] == jnp.arange(N_EXPERTS, dtype=I32)[None, :]
    rank, counts = _rank_within_expert(onehot)
    padded = ((counts + MOE_TILE - 1) // MOE_TILE) * MOE_TILE
    ends = jnp.cumsum(padded)
    pos = jnp.sum(jnp.where(onehot, (ends - padded)[None, :], 0), axis=1) + rank
    n_rows = 2 * m + N_EXPERTS * MOE_TILE
    n_rows = -(-n_rows // MOE_TILE) * MOE_TILE
    token = jnp.zeros((n_rows,), I32).at[pos].set(jnp.arange(2 * m, dtype=I32) // 2)
    tile_start = jnp.arange(n_rows // MOE_TILE, dtype=I32) * MOE_TILE
    tile_expert = jnp.minimum(jnp.sum(tile_start[:, None] >= ends[None, :], axis=1),
                              N_EXPERTS - 1).astype(I32)
    n_used = (ends[-1] // MOE_TILE).astype(I32).reshape(1)

    hs = jnp.take(h_bf, token, axis=0, mode="clip")
    out = moe_experts(tile_expert + layer * N_EXPERTS, n_used, hs, wg, wu, wd)
    pos2 = pos.reshape(m, 2)
    res, r0 = [], 0
    for x, tm in zip(xs, tms):
        rows = slice(r0, r0 + x.shape[0])
        res.append(moe_combine(x, jnp.take(out, pos2[rows, 0], axis=0, mode="clip"),
                               jnp.take(out, pos2[rows, 1], axis=0, mode="clip"),
                               route[rows], tm, out_norm_g))
        r0 += x.shape[0]
    return res


def kernel(x_prompt, x_sample, cache_k, cache_v, cache_ki, state_hgrn, state_conv, page_table,
           norm_mix_g, norm_ffn_g, final_g, w_in_even, w_out_even, hgrn_lb_logits, hgrn_norm_g,
           rel_bias, w_in_conv, w_conv, w_out_conv, w_router_g, b_router_g, w_router_e,
           b_router_e, w_gate, w_up, w_down):
    bsz, seq, d = x_prompt.shape
    dec = x_sample.shape[0]
    n_p = bsz * seq
    tm_p, tm_s = 512, 128
    tms = (tm_p, tm_s)
    assert n_p % tm_p == 0 and dec % tm_s == 0 and x_sample.shape[1] == 1
    width = HEADS * HEAD_DIM
    even_in = w_in_even.shape[2]
    even_pad = -(-even_in // IN_PROJ_TN) * IN_PROJ_TN
    tail0 = TAIL_BLOCK * LANES
    n_pool = cache_k.shape[1]

    xp = x_prompt.reshape(n_p, d)
    xs = x_sample.reshape(dec, d)

    lbs = jnp.cumsum(jax.nn.softmax(hgrn_lb_logits.astype(F32), axis=0), axis=0)[:-1]
    bias_tiles, bias_far, bias_by_dist = _bias_tables(rel_bias)
    expert_w = [w.reshape(-1, *w.shape[2:]) for w in (w_gate, w_up, w_down)]

    w_in = jnp.pad(w_in_even[0], ((0, 0), (0, even_pad - even_in))).astype(BF)
    zp, k_new_p, v_new_p, kv_bf = even_projection(xp, norm_mix_g[0], w_in, 2 * tm_p)
    zt = norm_matmul(xp, norm_mix_g[0], w_in[:, tail0:tail0 + LANES], 2 * tm_p, LANES)
    zs = norm_matmul(xs, norm_mix_g[0], w_in, tm_s, IN_PROJ_TN)
    zp3 = zp.reshape(bsz, seq, 6 * width)
    zt3 = zt.reshape(bsz, seq, LANES)
    zs3 = zs.reshape(dec, even_pad // LANES, LANES)

    oa_p, hgrn_p = hgrn_prompt(zp3, lbs[0], hgrn_norm_g[0])
    oa_s, hgrn_s = hgrn_sample(zs3, lbs[0], hgrn_norm_g[0], state_hgrn[0])

    wt = jnp.swapaxes(zt3[:, :, IDX_DIM:IDX_DIM + IDX_HEADS], 1, 2)
    mask_p = indexer_prompt_t(zp3, wt, zt3)
    ob_p = jnp.swapaxes(attn_prompt_t(zp3, kv_bf.reshape(bsz, seq, 2 * width), mask_p,
                                      jnp.swapaxes(bias_tiles, -1, -2), bias_far), 1, 2)

    qi3 = zs[:, 7 * width:8 * width].reshape(dec, IDX_HEADS, IDX_DIM)
    ki_new = zs[:, tail0:tail0 + IDX_DIM].reshape(dec, 1, IDX_DIM)
    wcol = zs[:, tail0 + IDX_DIM:tail0 + IDX_DIM + IDX_HEADS].reshape(dec, IDX_HEADS, 1)
    pages = page_table + 0 * n_pool
    scores_s = indexer_sample(pages, qi3, wcol, ki_new,
                              jnp.swapaxes(cache_ki, -1, -2).reshape(-1, IDX_DIM, PAGE))
    mask_s = select_topk(scores_s, TOPK)
    sel = jnp.transpose(mask_s, (1, 0, 2))
    n_pages = page_table.shape[1]
    sel_off = jnp.concatenate([jnp.zeros((dec, 1), F32),
                               jnp.cumsum(jnp.sum(sel, axis=2), axis=1)], axis=1)
    slot = jnp.arange(TOPK, dtype=F32)
    page_of = jnp.sum(sel_off[:, None, 1:] <= slot[None, :, None], axis=2)
    page_1h = (page_of[:, :, None] == jnp.arange(n_pages + 1)[None, None, :]).astype(F32)
    local = slot[None, :] - jnp.einsum("brp,bp->br", page_1h, sel_off[:, :-1],
                                       precision=lax.Precision.HIGHEST)
    within = jnp.einsum("brp,bpl->brl", page_1h, jnp.cumsum(sel, axis=2),
                        precision=lax.Precision.HIGHEST)
    lane_of = jnp.sum(within <= local[:, :, None], axis=2)
    sel_idx = (page_of * PAGE + lane_of).astype(I32)
    past = n_pages * PAGE
    page_id = jnp.einsum("brp,bp->br", page_1h[:, :, :n_pages], pages.astype(F32),
                         precision=lax.Precision.HIGHEST).astype(I32)
    rows = jnp.where(page_of < n_pages, page_id * PAGE + lane_of, 0).astype(I32).reshape(-1)
    rows, zp3 = lax.optimization_barrier((rows, zp3))
    k_sel, v_sel = sc_gather_rows([cache_k.reshape(-1, HEADS, HEAD_DIM),
                                   cache_v.reshape(-1, HEADS, HEAD_DIM)], rows)
    dist = jnp.minimum(past - sel_idx, REL_MAX_DIST).astype(I32)
    first_near = jnp.sum(dist >= REL_MAX_DIST, axis=1).astype(I32)
    ob_s = attn_sample_compact(dist, first_near, zs3, k_sel, v_sel, bias_by_dist)

    w_out = w_out_even[0].astype(BF)
    w_out_ab = [w_out[:width], w_out[width:]]
    xp = matmul_residual([oa_p.reshape(n_p, width), ob_p.reshape(n_p, width)], w_out_ab, xp, 2 * tm_p, 1024)
    xs = matmul_residual([oa_s.reshape(dec, width), ob_s.reshape(dec, width)], w_out_ab, xs, tm_s, 512)

    xp, xs = hier_moe([xp, xs], tms, norm_ffn_g[0], w_router_g[0], b_router_g[0], w_router_e[0],
                      b_router_e[0], *expert_w, 0)

    w_in_c = w_in_conv[0].astype(BF)
    zcp = norm_matmul(xp, norm_mix_g[1], w_in_c, 2 * tm_p, 1024)
    zcs = norm_matmul(xs, norm_mix_g[1], w_in_c, tm_s, 512)
    cw = zcp.shape[1] // 3
    v_p, conv_p = conv_prompt(zcp.reshape(bsz, seq, 3 * cw), w_conv[0])
    v_s, cs0, cs1 = conv_sample(zcs, w_conv[0], state_conv[0, :, 0], state_conv[0, :, 1])
    w_out_c = [w_out_conv[0].astype(BF)]
    xp = matmul_residual([v_p.reshape(n_p, cw)], w_out_c, xp, 2 * tm_p, 1024)
    xs = matmul_residual([v_s], w_out_c, xs, tm_s, 512)

    yp, ys = hier_moe([xp, xs], tms, norm_ffn_g[1], w_router_g[1], b_router_g[1], w_router_e[1],
                      b_router_e[1], *expert_w, 1, out_norm_g=final_g)

    kcol, vcol = 5 * width, 6 * width
    heads = lambda a, n: a.reshape(1, *n, HEADS, HEAD_DIM)
    return (yp.reshape(bsz, seq, d),
            ys.reshape(dec, 1, d),
            heads(k_new_p, (bsz, seq)),
            heads(v_new_p, (bsz, seq)),
            zt[:, :IDX_DIM].reshape(1, bsz, seq, IDX_DIM),
            hgrn_p[None],
            conv_p[None],
            heads(zs[:, kcol:kcol + width], (dec, 1)),
            heads(zs[:, vcol:vcol + width], (dec, 1)),
            zs[:, tail0:tail0 + IDX_DIM].reshape(1, dec, 1, IDX_DIM),
            hgrn_s[None],
            jnp.stack([cs0, cs1], axis=1)[None])
```

```python
import functools
import math

import numpy as np
import jax
import jax.numpy as jnp
from jax import lax
from jax.experimental import pallas as pl
from jax.experimental.pallas import tpu as pltpu
from jax.experimental.pallas import tpu_sc as plsc

F32 = jnp.float32
BF = jnp.bfloat16
I32 = jnp.int32

RMS_EPS = 1e-6
LANES = 128
NEG_BIG = -1e30
VMEM_LIMIT = 56 * 1024 * 1024

HEADS = 8
HEAD_DIM = 128
IDX_HEADS = 16
IDX_DIM = 64
TOPK = 256
REL_BUCKETS = 32
REL_MAX_DIST = 128
N_EXPERTS = 16
EXP_PER_GROUP = 4
N_GROUPS = 4
PAGE = 128

HGRN_CHUNK = 128
HGRN_HEAD_GROUP = 8
MOE_TILE = 256
IN_PROJ_TN = 640
TAIL_BLOCK = 8 * HEADS * HEAD_DIM // LANES


def _cparams(sem):
    return pltpu.CompilerParams(dimension_semantics=sem, vmem_limit_bytes=VMEM_LIMIT)


def _dot(a, b):
    return jnp.dot(a, b, preferred_element_type=F32)


def _dot_nt(a, b):
    return lax.dot_general(a, b, (((1,), (1,)), ((), ())), preferred_element_type=F32)


def _dot_tn(a, b):
    return lax.dot_general(a, b, (((0,), (0,)), ((), ())), preferred_element_type=F32)


def _sigmoid(x):
    return 1.0 / (1.0 + jnp.exp(-x))


def _norm_mm_body(x_ref, g_ref, w_ref, o_ref, h_ref):
    @pl.when(pl.program_id(1) == 0)
    def _():
        x = x_ref[...]
        ms = jnp.mean(x * x, axis=-1, keepdims=True)
        h_ref[...] = (x * lax.rsqrt(ms + RMS_EPS) * g_ref[...]).astype(BF)

    o_ref[...] = _dot(h_ref[...], w_ref[...])


def norm_matmul(x, g, w_bf, tm, tn):
    m, k = x.shape
    n = w_bf.shape[1]
    return pl.pallas_call(
        _norm_mm_body,
        grid=(m // tm, n // tn),
        in_specs=[pl.BlockSpec((tm, k), lambda i, j: (i, 0)),
                  pl.BlockSpec((1, k), lambda i, j: (0, 0)),
                  pl.BlockSpec((k, tn), lambda i, j: (0, j))],
        out_specs=pl.BlockSpec((tm, tn), lambda i, j: (i, j)),
        out_shape=jax.ShapeDtypeStruct((m, n), F32),
        scratch_shapes=[pltpu.VMEM((tm, k), BF)],
        compiler_params=_cparams(("parallel", "arbitrary")),
        name="norm_matmul",
    )(x, g.reshape(1, k), w_bf)


EVEN_TN = 1024


def _even_proj_body(x_ref, g_ref, w_ref, z_ref, k_ref, v_ref, kv_ref, h_ref, *, per):
    j = pl.program_id(1)

    @pl.when(j == 0)
    def _():
        x = x_ref[...]
        ms = jnp.mean(x * x, axis=-1, keepdims=True)
        h_ref[...] = (x * lax.rsqrt(ms + RMS_EPS) * g_ref[...]).astype(BF)

    res = _dot(h_ref[...], w_ref[...])
    sec = j // per

    @pl.when((sec < 5) | (sec == 7))
    def _():
        z_ref[...] = res

    @pl.when(sec == 5)
    def _():
        k_ref[...] = res
        kv_ref[...] = res.astype(BF)

    @pl.when(sec == 6)
    def _():
        v_ref[...] = res
        kv_ref[...] = res.astype(BF)


def even_projection(x, g, w_bf, tm):
    m, k = x.shape
    width = HEADS * HEAD_DIM
    per = width // EVEN_TN
    clip = lambda a, lo, hi: jnp.minimum(jnp.maximum(a, lo), hi)
    return pl.pallas_call(
        functools.partial(_even_proj_body, per=per),
        grid=(m // tm, 8 * per),
        in_specs=[pl.BlockSpec((tm, k), lambda i, j: (i, 0)),
                  pl.BlockSpec((1, k), lambda i, j: (0, 0)),
                  pl.BlockSpec((k, EVEN_TN), lambda i, j: (0, j))],
        out_specs=[pl.BlockSpec((tm, EVEN_TN), lambda i, j: (i, j - clip(j - (5 * per - 1), 0, 2 * per))),
                   pl.BlockSpec((tm, EVEN_TN), lambda i, j: (i, clip(j - 5 * per, 0, per - 1)),
                                pipeline_mode=pl.Buffered(1)),
                   pl.BlockSpec((tm, EVEN_TN), lambda i, j: (i, clip(j - 6 * per, 0, per - 1)),
                                pipeline_mode=pl.Buffered(1)),
                   pl.BlockSpec((tm, EVEN_TN), lambda i, j: (i, clip(j - 5 * per, 0, 2 * per - 1)),
                                pipeline_mode=pl.Buffered(1))],
        out_shape=[jax.ShapeDtypeStruct((m, 6 * width), F32),
                   jax.ShapeDtypeStruct((m, width), F32),
                   jax.ShapeDtypeStruct((m, width), F32),
                   jax.ShapeDtypeStruct((m, 2 * width), BF)],
        scratch_shapes=[pltpu.VMEM((tm, k), BF)],
        compiler_params=_cparams(("parallel", "arbitrary")),
        name="even_projection",
    )(x, g.reshape(1, k), w_bf)


def _mm_res_body(*refs, n_lhs):
    a_refs = refs[:n_lhs]
    w_refs = refs[n_lhs:2 * n_lhs]
    r_ref = refs[2 * n_lhs]
    o_ref = refs[2 * n_lhs + 1]
    acc = r_ref[...]
    for a_ref, w_ref in zip(a_refs, w_refs):
        acc = acc + _dot(a_ref[...], w_ref[...])
    o_ref[...] = acc


def matmul_residual(lhs, ws_bf, res, tm, tn):
    m, n = res.shape
    n_lhs = len(lhs)
    assert all(a.dtype == BF for a in lhs)
    in_specs = ([pl.BlockSpec((tm, a.shape[1]), lambda i, j: (i, 0)) for a in lhs]
                + [pl.BlockSpec((w.shape[0], tn), lambda i, j: (0, j)) for w in ws_bf]
                + [pl.BlockSpec((tm, tn), lambda i, j: (i, j))])
    return pl.pallas_call(
        functools.partial(_mm_res_body, n_lhs=n_lhs),
        grid=(m // tm, n // tn),
        in_specs=in_specs,
        out_specs=pl.BlockSpec((tm, tn), lambda i, j: (i, j)),
        out_shape=jax.ShapeDtypeStruct((m, n), F32),
        compiler_params=_cparams(("parallel", "arbitrary")),
        name="matmul_residual",
    )(*lhs, *ws_bf, res)


def _hgrn_static(c):
    levels = []
    m = 1
    while m < c:
        levels.append(m)
        m *= 2
    t = np.arange(c)
    rows = [t[None, :] <= t[:, None]]
    masks = [np.eye(c, dtype=bool)]
    for m in levels:
        blk = t // (2 * m)
        pos = t % (2 * m)
        bnd = blk * 2 * m + m - 1
        right = pos >= m
        left = pos < m
        e_rows = (t[None, :] > bnd[:, None]) & (t[None, :] <= t[:, None]) & right[:, None]
        f_rows = (t[None, :] > t[:, None]) & (t[None, :] <= bnd[:, None]) & left[:, None]
        rows.append(e_rows | f_rows)
        masks.append((blk[:, None] == blk[None, :]) & right[:, None] & left[None, :])
    m_all = np.stack(rows).astype(np.float32)
    masks = np.stack(masks).astype(np.float32)
    return m_all, masks, len(levels)


def _hgrn_gates(qa, fa, lb):
    f = lb + (1.0 - lb) * _sigmoid(fa)
    q = qa * _sigmoid(qa)
    return q, f


def _hgrn_prompt_body(qa_ref, fa_ref, ia_ref, ga_ref, lb_ref, gn_ref, mall_ref, masks_ref,
                      oa_ref, st_out_ref, st_ref, *, chunk, tblock, n_levels):
    c = chunk
    t_idx = pl.program_id(1)

    @pl.when(t_idx == 0)
    def _():
        st_ref[...] = jnp.zeros_like(st_ref)

    gn = gn_ref[...]

    def chunk_step(ci, carry):
        r0 = pl.multiple_of(ci * c, c)
        for h0 in range(0, HEADS, HGRN_HEAD_GROUP):
            hs = range(h0, h0 + HGRN_HEAD_GROUP)
            cols = {h: slice(h * HEAD_DIM, (h + 1) * HEAD_DIM) for h in hs}
            q, k, g_hi, g_lo, v = {}, {}, {}, {}, {}
            for h in hs:
                q[h], f = _hgrn_gates(qa_ref[0, pl.ds(r0, c), cols[h]],
                                      fa_ref[0, pl.ds(r0, c), cols[h]], lb_ref[:, cols[h]])
                k[h] = 1.0 - f
                g = jnp.log(f)
                g_hi[h] = g.astype(BF)
                g_lo[h] = (g - g_hi[h].astype(F32)).astype(BF)
                v[h] = ia_ref[0, pl.ds(r0, c), cols[h]].astype(BF)
            b = {h: _dot(mall_ref[0], g_hi[h]) + _dot(mall_ref[0], g_lo[h]) for h in hs}
            o = {h: _dot_nt((q[h] * jnp.exp(b[h])).astype(BF), st_ref[h].astype(BF)) for h in hs}
            a = {h: masks_ref[0] * _dot_nt(q[h].astype(BF), k[h].astype(BF)) for h in hs}
            for li in range(n_levels):
                ml = mall_ref[1 + li]
                w = {h: jnp.exp(_dot(ml, g_hi[h]) + _dot(ml, g_lo[h])) for h in hs}
                for h in hs:
                    a[h] = a[h] + masks_ref[1 + li] * _dot_nt((q[h] * w[h]).astype(BF),
                                                              (k[h] * w[h]).astype(BF))
            for h in hs:
                o[h] = o[h] + _dot(a[h].astype(BF), v[h])
            for h in hs:
                b_last = b[h][c - 1:c]
                k_st = (k[h] * jnp.exp(b_last - b[h])).astype(BF)
                st_ref[h] = st_ref[h] * jnp.exp(b_last) + _dot_tn(v[h], k_st)
            for h in hs:
                ga = ga_ref[0, pl.ds(r0, c), cols[h]]
                ms = jnp.mean(o[h] * o[h], axis=-1, keepdims=True)
                oa_ref[0, pl.ds(r0, c), cols[h]] = (o[h] * lax.rsqrt(ms + RMS_EPS) * gn
                                                    * (ga * _sigmoid(ga))).astype(BF)
        return carry

    lax.fori_loop(0, tblock // c, chunk_step, 0)

    @pl.when(t_idx == pl.num_programs(1) - 1)
    def _():
        st_out_ref[0] = st_ref[...]


def hgrn_prompt(z3, lb, gn, tblock=256, chunk=HGRN_CHUNK):
    bsz, seq = z3.shape[:2]
    width = HEADS * HEAD_DIM
    m_all, masks, n_levels = _hgrn_static(chunk)
    zspec = lambda cb: pl.BlockSpec((1, tblock, width), lambda b, t, cb=cb: (b, t, cb))
    oa, st = pl.pallas_call(
        functools.partial(_hgrn_prompt_body, chunk=chunk, tblock=tblock, n_levels=n_levels),
        grid=(bsz, seq // tblock),
        in_specs=[zspec(0), zspec(1), zspec(2), zspec(3),
                  pl.BlockSpec((1, width), lambda b, t: (0, 0)),
                  pl.BlockSpec((1, HEAD_DIM), lambda b, t: (0, 0)),
                  pl.BlockSpec(m_all.shape, lambda b, t: (0, 0, 0)),
                  pl.BlockSpec(masks.shape, lambda b, t: (0, 0, 0))],
        out_specs=[pl.BlockSpec((1, tblock, width), lambda b, t: (b, t, 0)),
                   pl.BlockSpec((1, HEADS, HEAD_DIM, HEAD_DIM), lambda b, t: (b, 0, 0, 0))],
        out_shape=[jax.ShapeDtypeStruct((bsz, seq, width), BF),
                   jax.ShapeDtypeStruct((bsz, HEADS, HEAD_DIM, HEAD_DIM), F32)],
        scratch_shapes=[pltpu.VMEM((HEADS, HEAD_DIM, HEAD_DIM), F32)],
        compiler_params=_cparams(("parallel", "arbitrary")),
        name="hgrn_prompt",
    )(z3, z3, z3, z3, lb.reshape(1, width), gn.reshape(1, HEAD_DIM),
      jnp.asarray(m_all, BF), jnp.asarray(masks, F32))
    return oa, jnp.swapaxes(st, -1, -2)


def _col(row, eye):
    return jnp.sum(eye * row, axis=1, keepdims=True)


def _hgrn_sample_body(qa_ref, fa_ref, ia_ref, ga_ref, lb_ref, gn_ref, s_ref, oa_ref, so_ref):
    eye = (lax.broadcasted_iota(I32, (HEAD_DIM, HEAD_DIM), 0)
           == lax.broadcasted_iota(I32, (HEAD_DIM, HEAD_DIM), 1)).astype(F32)
    rnd = lambda a: a.astype(BF).astype(F32)
    for g in range(SAMPLE_SEQ_GROUP):
        q8, f8 = _hgrn_gates(qa_ref[g], fa_ref[g], lb_ref[...])
        ga = ga_ref[g]
        gate = ga * _sigmoid(ga)
        kr = rnd(1.0 - f8)
        vr = rnd(ia_ref[g])
        qfr = rnd(q8 * f8)
        qk = rnd(jnp.sum(rnd(q8) * kr, axis=-1, keepdims=True))
        outs = []
        for h in range(HEADS):
            f_col = _col(f8[h:h + 1], eye)
            s_old = s_ref[g, h]
            so_ref[g, h] = f_col * s_old + _col(kr[h:h + 1], eye) * vr[h:h + 1]
            outs.append(qk[h:h + 1] * vr[h:h + 1]
                        + jnp.sum(_col(qfr[h:h + 1], eye) * rnd(s_old), axis=0, keepdims=True))
        o = jnp.concatenate(outs, axis=0)
        ms = jnp.mean(o * o, axis=-1, keepdims=True)
        oa_ref[g] = (o * lax.rsqrt(ms + RMS_EPS) * gn_ref[...] * gate).astype(BF)


def hgrn_sample(zs3, lb, gn, s0):
    bsz = zs3.shape[0]
    g = SAMPLE_SEQ_GROUP
    assert bsz % g == 0
    zspec = lambda cb: pl.BlockSpec((g, HEADS, HEAD_DIM), lambda b, cb=cb: (b, cb, 0))
    sspec = pl.BlockSpec((g, HEADS, HEAD_DIM, HEAD_DIM), lambda b: (b, 0, 0, 0))
    return pl.pallas_call(
        _hgrn_sample_body,
        grid=(bsz // g,),
        in_specs=[zspec(0), zspec(1), zspec(2), zspec(3),
                  pl.BlockSpec((HEADS, HEAD_DIM), lambda b: (0, 0)),
                  pl.BlockSpec((1, HEAD_DIM), lambda b: (0, 0)),
                  sspec],
        out_specs=[pl.BlockSpec((g, HEADS, HEAD_DIM), lambda b: (b, 0, 0)), sspec],
        out_shape=[jax.ShapeDtypeStruct((bsz, HEADS, HEAD_DIM), BF),
                   jax.ShapeDtypeStruct(s0.shape, F32)],
        compiler_params=_cparams(("parallel",)),
        name="hgrn_sample",
    )(zs3, zs3, zs3, zs3, lb.reshape(HEADS, HEAD_DIM), gn.reshape(1, HEAD_DIM), s0)


_KEY_NEG_INF = np.int32(np.uint32(0x807FFFFF).astype(np.int64) - (1 << 32))
_INT_MIN = np.int32(-(1 << 31))


def _count(u_ref, n_groups, group, thr, cmp):
    rows = u_ref.shape[1]
    step = min(rows, LANES)
    parts = []
    for r0 in range(0, rows, step):
        t = jnp.broadcast_to(thr[r0:r0 + step], (step, LANES))

        def body(gi, acc, r0=r0, t=t):
            for i in range(group):
                acc = acc + cmp(u_ref[gi * group + i, r0:r0 + step], t).astype(F32)
            return acc

        acc = lax.fori_loop(0, n_groups, body, jnp.zeros((step, LANES), F32))
        parts.append(jnp.sum(acc, axis=-1, keepdims=True))
    return parts[0] if len(parts) == 1 else jnp.concatenate(parts, axis=0)


def _topk_mask(score_ref, u_ref, write_tile, n_tiles, n_groups, group, k):
    rows = score_ref.shape[1]
    n_live = n_groups * group

    def to_key(t, carry):
        bits = pltpu.bitcast(score_ref[t], I32)
        u_ref[t] = jnp.where(bits < 0, bits ^ np.int32(0x7FFFFFFF), bits)
        return carry

    lax.fori_loop(0, n_live, to_key, 0)

    kf = float(k)
    ge = lambda u, t: u >= t
    cnt = _count(u_ref, n_groups, group, jnp.zeros((rows, 1), I32), ge)
    lo = jnp.where(cnt >= kf, np.int32(0), _INT_MIN)

    def bit_step(i, lo):
        cand = lo | (np.int32(1) << (30 - i))
        cnt = _count(u_ref, n_groups, group, cand, ge)
        return jnp.where(cnt >= kf, cand, lo)

    lo = lax.fori_loop(0, 31, bit_step, lo)
    c_gt = _count(u_ref, n_groups, group, lo, lambda u, t: u > t)
    c_eq = _count(u_ref, n_groups, group, lo, lambda u, t: u == t)
    need = kf - c_gt
    real = lo > _KEY_NEG_INF
    excess = jnp.where(real & (c_eq > need), 1.0, 0.0)
    any_excess = jnp.max(excess) > 0.0

    @pl.when(jnp.logical_not(any_excess))
    def _():
        def emit(t, carry):
            u = u_ref[t]
            write_tile(t, jnp.where((u >= lo) & (u > _KEY_NEG_INF), 1.0, 0.0))
            return carry

        lax.fori_loop(0, n_live, emit, 0)

    @pl.when(any_excess)
    def _():
        upper = (lax.broadcasted_iota(I32, (LANES, LANES), 0)
                 <= lax.broadcasted_iota(I32, (LANES, LANES), 1)).astype(BF)

        def emit(t, seen):
            u = u_ref[t]
            eq = jnp.where(u == lo, 1.0, 0.0)
            prefix = seen + _dot(eq.astype(BF), upper)
            take = (u > lo) | ((u == lo) & (prefix <= need))
            write_tile(t, jnp.where(take & (u > _KEY_NEG_INF), 1.0, 0.0))
            return seen + jnp.sum(eq, axis=-1, keepdims=True)

        lax.fori_loop(0, n_live, emit, jnp.zeros((rows, 1), F32))

    def clear(t, carry):
        write_tile(t, jnp.zeros((rows, LANES), F32))
        return carry

    lax.fori_loop(n_live, n_tiles, clear, 0)


QW = 256
SUB = 8
IDX_KCHUNK = 512
ATTN_KSTEP = 512
ATTN_TILES = ATTN_KSTEP // LANES


def _row_all(x8, op):
    return jnp.broadcast_to(op(x8, axis=0, keepdims=True), x8.shape)


def _count_t(u_ref, n_groups, group, thr_row, cmp):
    qw = u_ref.shape[2]
    thr = jnp.broadcast_to(thr_row, (SUB, qw))

    def body(gi, acc):
        for i in range(group):
            u = u_ref[gi * group + i].reshape(LANES // SUB, SUB, qw)
            acc = acc + jnp.sum(cmp(u, thr[None]).astype(F32), axis=0)
        return acc

    acc = lax.fori_loop(0, n_groups, body, jnp.zeros((SUB, qw), F32))
    return jnp.sum(acc, axis=0, keepdims=True)


def _topk_mask_t(score_ref, u_ref, write_tile, n_tiles, n_groups, group, k):
    qw = score_ref.shape[2]
    n_live = n_groups * group

    def to_key(t, carry):
        bits = pltpu.bitcast(score_ref[t], I32)
        u_ref[t] = jnp.where(bits < 0, bits ^ np.int32(0x7FFFFFFF), bits)
        return carry

    lax.fori_loop(0, n_live, to_key, 0)

    kf = float(k)
    ge = lambda u, t: u >= t
    cnt = _count_t(u_ref, n_groups, group, jnp.zeros((1, qw), I32), ge)
    lo = jnp.where(cnt >= kf, np.int32(0), _INT_MIN)

    def bit_step(i, lo):
        cand = lo | (np.int32(1) << (30 - i))
        cnt = _count_t(u_ref, n_groups, group, cand, ge)
        return jnp.where(cnt >= kf, cand, lo)

    lo = lax.fori_loop(0, 31, bit_step, lo)
    c_gt = _count_t(u_ref, n_groups, group, lo, lambda u, t: u > t)
    c_eq = _count_t(u_ref, n_groups, group, lo, lambda u, t: u == t)
    need = kf - c_gt
    excess = jnp.where((lo > _KEY_NEG_INF) & (c_eq > need), 1.0, 0.0)
    any_excess = jnp.max(excess) > 0.0
    lo_b = jnp.broadcast_to(lo, (LANES, qw))

    @pl.when(jnp.logical_not(any_excess))
    def _():
        def emit(t, carry):
            u = u_ref[t]
            write_tile(t, jnp.where((u >= lo_b) & (u > _KEY_NEG_INF), 1.0, 0.0))
            return carry

        lax.fori_loop(0, n_live, emit, 0)

    @pl.when(any_excess)
    def _():
        lower = (lax.broadcasted_iota(I32, (LANES, LANES), 1)
                 <= lax.broadcasted_iota(I32, (LANES, LANES), 0)).astype(BF)

        def emit(t, seen):
            u = u_ref[t]
            eq = jnp.where(u == lo_b, 1.0, 0.0)
            prefix = seen + _dot(lower, eq.astype(BF))
            take = (u > lo_b) | ((u == lo_b) & (prefix <= need))
            write_tile(t, jnp.where(take & (u > _KEY_NEG_INF), 1.0, 0.0))
            return seen + jnp.sum(eq, axis=0, keepdims=True)

        lax.fori_loop(0, n_live, emit, jnp.zeros((1, qw), F32))

    def clear(t, carry):
        write_tile(t, jnp.zeros((LANES, qw), F32))
        return carry

    lax.fori_loop(n_live, n_tiles, clear, 0)


def _indexer_t_body(qi_ref, wt_ref, tail_ref, mask_ref, score_ref, u_ref, *, seq):
    j = pl.program_id(1)
    n_tiles = seq // LANES
    tiles_per_chunk = IDX_KCHUNK // LANES
    qi = qi_ref[0].astype(BF)
    wt = wt_ref[0] * (IDX_DIM ** -0.5 * IDX_HEADS ** -0.5)

    n_chunks = (j * QW + QW - 1) // IDX_KCHUNK + 1
    q_pos = j * QW + lax.broadcasted_iota(I32, (IDX_KCHUNK, QW), 1)

    def chunk_step(ci, carry):
        k0 = pl.multiple_of(ci * IDX_KCHUNK, IDX_KCHUNK)
        kic = tail_ref[0, pl.ds(k0, IDX_KCHUNK), :][:, 0:IDX_DIM].astype(BF)
        acc = jnp.zeros((IDX_KCHUNK, QW), F32)
        for h in range(IDX_HEADS):
            s = _dot_nt(kic, qi[:, h * IDX_DIM:(h + 1) * IDX_DIM])
            acc = acc + wt[h:h + 1] * jnp.maximum(s, 0.0)
        k_pos = k0 + lax.broadcasted_iota(I32, (IDX_KCHUNK, QW), 0)
        acc = jnp.where(k_pos <= q_pos, acc, -jnp.inf)
        for i in range(tiles_per_chunk):
            score_ref[ci * tiles_per_chunk + i] = acc[i * LANES:(i + 1) * LANES]
        return carry

    lax.fori_loop(0, n_chunks, chunk_step, 0)

    def write_tile(t, m):
        mask_ref[0, 0, t] = m.astype(BF)

    _topk_mask_t(score_ref, u_ref, write_tile, n_tiles, n_chunks, tiles_per_chunk, TOPK)


def indexer_prompt_t(z3, wt, tail3):
    bsz, seq = z3.shape[:2]
    n_tiles = seq // LANES
    assert seq % QW == 0 and seq % IDX_KCHUNK == 0
    return pl.pallas_call(
        functools.partial(_indexer_t_body, seq=seq),
        grid=(bsz, seq // QW),
        in_specs=[pl.BlockSpec((1, QW, IDX_HEADS * IDX_DIM), lambda b, j: (b, j, 5)),
                  pl.BlockSpec((1, IDX_HEADS, QW), lambda b, j: (b, 0, j)),
                  pl.BlockSpec((1, seq, LANES), lambda b, j: (b, 0, 0))],
        out_specs=pl.BlockSpec((1, 1, n_tiles, LANES, QW), lambda b, j: (b, j, 0, 0, 0)),
        out_shape=jax.ShapeDtypeStruct((bsz, seq // QW, n_tiles, LANES, QW), BF),
        scratch_shapes=[pltpu.VMEM((n_tiles, LANES, QW), F32),
                        pltpu.VMEM((n_tiles, LANES, QW), I32)],
        compiler_params=_cparams(("parallel", "arbitrary")),
        name="indexer_prompt_t",
    )(z3, wt, tail3)


def _attn_t_body(bfar_ref, q_ref, k_ref, vt_ref, mask_ref, bias_ref, o_ref, m_ref, l_ref, acc_ref):
    j = pl.program_id(1)
    scale = HEAD_DIM ** -0.5
    qsub = QW // LANES
    q = q_ref[0].astype(BF)
    m_ref[...] = jnp.full(m_ref.shape, NEG_BIG, F32)
    l_ref[...] = jnp.zeros(l_ref.shape, F32)
    acc_ref[...] = jnp.zeros(acc_ref.shape, F32)

    def process(sb, near):
        k0 = pl.multiple_of(sb * ATTN_KSTEP, ATTN_KSTEP)
        kblk = k_ref[0, pl.ds(k0, ATTN_KSTEP), :]
        vblk = vt_ref[0, pl.ds(k0, ATTN_KSTEP), :]
        tiles = [sb * ATTN_TILES + i for i in range(ATTN_TILES)]
        sel = jnp.concatenate([mask_ref[0, 0, t] for t in tiles], axis=0).astype(F32) > 0.0
        if near:
            def tile_bias(t, h):
                row = []
                for s in range(qsub):
                    d = j * qsub + s - t
                    row.append(bias_ref[jnp.where(d == 0, 0, jnp.where(d == 1, 1, 2)), h])
                return jnp.concatenate(row, axis=1)
        head_cols = [slice(h * HEAD_DIM, (h + 1) * HEAD_DIM) for h in range(HEADS)]
        qk = lambda h: _dot_nt(kblk[:, head_cols[h]], q[:, head_cols[h]])
        lg_next = qk(0)
        for h in range(HEADS):
            cols = head_cols[h]
            lg = lg_next * scale
            if h + 1 < HEADS:
                lg_next = qk(h + 1)
            if near:
                lg = lg + jnp.concatenate([tile_bias(t, h) for t in tiles], axis=0)
            else:
                lg = lg + bfar_ref[h]
            lg = jnp.where(sel, lg, NEG_BIG)
            m_old = m_ref[h]
            part = jnp.max(lg.reshape(ATTN_KSTEP // SUB, SUB, QW), axis=0)
            m_new = jnp.maximum(m_old, _row_all(part, jnp.max))
            p = jnp.exp(lg - m_new[0:1])
            alpha = jnp.exp(m_old - m_new)
            psum = jnp.sum(p.reshape(ATTN_KSTEP // SUB, SUB, QW), axis=0)
            l_ref[h] = alpha * l_ref[h] + _row_all(psum, jnp.sum)
            pv = _dot_tn(vblk[:, cols], p.astype(BF))
            acc_ref[cols, :] = alpha[0:1] * acc_ref[cols, :] + pv
            m_ref[h] = m_new

    n_far = jnp.maximum((qsub * j - 1) // ATTN_TILES, 0)
    n_steps = (qsub * j + qsub - 1) // ATTN_TILES + 1

    def far_step(sb, carry):
        process(sb, False)
        return carry

    def near_step(sb, carry):
        process(sb, True)
        return carry

    lax.fori_loop(0, n_far, far_step, 0)
    lax.fori_loop(n_far, n_steps, near_step, 0)

    for h in range(HEADS):
        cols = slice(h * HEAD_DIM, (h + 1) * HEAD_DIM)
        o_ref[0, cols, :] = (acc_ref[cols, :] / l_ref[h][0:1]).astype(BF)


def attn_prompt_t(z3, kv_bf, mask_t, bias_tiles_t, bias_far):
    bsz, seq = z3.shape[:2]
    n_tiles = seq // LANES
    width = HEADS * HEAD_DIM
    assert seq % ATTN_KSTEP == 0 and seq % QW == 0
    return pl.pallas_call(
        _attn_t_body,
        grid=(bsz, seq // QW),
        in_specs=[pl.BlockSpec(memory_space=pltpu.SMEM),
                  pl.BlockSpec((1, QW, width), lambda b, j: (b, j, 4)),
                  pl.BlockSpec((1, seq, width), lambda b, j: (b, 0, 0)),
                  pl.BlockSpec((1, seq, width), lambda b, j: (b, 0, 1)),
                  pl.BlockSpec((1, 1, n_tiles, LANES, QW), lambda b, j: (b, j, 0, 0, 0)),
                  pl.BlockSpec(bias_tiles_t.shape, lambda b, j: (0, 0, 0, 0))],
        out_specs=pl.BlockSpec((1, width, QW), lambda b, j: (b, 0, j)),
        out_shape=jax.ShapeDtypeStruct((bsz, width, seq), BF),
        scratch_shapes=[pltpu.VMEM((HEADS, SUB, QW), F32),
                        pltpu.VMEM((HEADS, SUB, QW), F32),
                        pltpu.VMEM((width, QW), F32)],
        compiler_params=_cparams(("parallel", "arbitrary")),
        name="attn_prompt_t",
    )(bias_far, z3, kv_bf, kv_bf, mask_t, bias_tiles_t)


IDX_SEQ_GROUP = 2


def _indexer_sample_body(pt_ref, qi_ref, wcol_ref, kinew_ref, *rest, n_pages):
    ki_refs = rest[:IDX_SEQ_GROUP * n_pages]
    out_ref = rest[IDX_SEQ_GROUP * n_pages]
    lane = lax.broadcasted_iota(I32, (1, LANES), 1)
    for g in range(IDX_SEQ_GROUP):
        qi = qi_ref[g].astype(BF)
        w = wcol_ref[g] * (IDX_DIM ** -0.5 * IDX_HEADS ** -0.5)
        for i in range(n_pages):
            s = _dot(qi, ki_refs[g * n_pages + i][0].astype(BF))
            out_ref[i, g] = jnp.sum(w * jnp.maximum(s, 0.0), axis=0, keepdims=True)
        kn = kinew_ref[g].astype(BF).astype(F32)
        sn = jnp.sum(qi.astype(F32) * kn, axis=-1, keepdims=True)
        new = jnp.sum(w * jnp.maximum(sn, 0.0), axis=0, keepdims=True)
        out_ref[n_pages, g] = jnp.where(lane == 0, new, -jnp.inf)


def indexer_sample(page_table, qi3, wcol, ki_new, ki_pool_t):
    bsz, n_pages = page_table.shape
    grp = IDX_SEQ_GROUP
    assert bsz % grp == 0
    ki_spec = lambda g, i: pl.BlockSpec((1, IDX_DIM, PAGE),
                                        lambda b, pt, g=g, i=i: (pt[b * grp + g, i], 0, 0))
    grid_spec = pltpu.PrefetchScalarGridSpec(
        num_scalar_prefetch=1,
        grid=(bsz // grp,),
        in_specs=[pl.BlockSpec((grp, IDX_HEADS, IDX_DIM), lambda b, pt: (b, 0, 0)),
                  pl.BlockSpec((grp, IDX_HEADS, 1), lambda b, pt: (b, 0, 0)),
                  pl.BlockSpec((grp, 1, IDX_DIM), lambda b, pt: (b, 0, 0))]
                 + [ki_spec(g, i) for g in range(grp) for i in range(n_pages)],
        out_specs=pl.BlockSpec((n_pages + 1, grp, 1, LANES), lambda b, pt: (0, b, 0, 0)),
    )
    out = pl.pallas_call(
        functools.partial(_indexer_sample_body, n_pages=n_pages),
        grid_spec=grid_spec,
        out_shape=jax.ShapeDtypeStruct((n_pages + 1, bsz, 1, LANES), F32),
        compiler_params=_cparams(("arbitrary",)),
        name="indexer_sample",
    )(page_table, qi3, wcol, ki_new, *([ki_pool_t] * (grp * n_pages)))
    return out.reshape(n_pages + 1, bsz, LANES)


def _select_body(score_ref, mask_ref, u_ref, *, n_tiles, k):
    def write_tile(t, m):
        mask_ref[t] = m

    _topk_mask(score_ref, u_ref, write_tile, n_tiles, n_tiles, 1, k)


def select_topk(scores, k):
    n_tiles, rows, _ = scores.shape
    return pl.pallas_call(
        functools.partial(_select_body, n_tiles=n_tiles, k=k),
        grid=(1,),
        in_specs=[pl.BlockSpec(scores.shape, lambda i: (0, 0, 0))],
        out_specs=pl.BlockSpec(scores.shape, lambda i: (0, 0, 0)),
        out_shape=jax.ShapeDtypeStruct(scores.shape, F32),
        scratch_shapes=[pltpu.VMEM(scores.shape, I32)],
        compiler_params=_cparams(("arbitrary",)),
        name="select_topk",
    )(scores)


SC_CORES = 2
SC_SUBCORES = 16
SC_GATHER_CHUNK = 32


def sc_gather_rows(tables, idx):
    n_rows = idx.shape[0]
    workers = SC_CORES * SC_SUBCORES
    per_worker = n_rows // workers
    chunk = SC_GATHER_CHUNK
    assert n_rows % workers == 0 and per_worker % chunk == 0 and chunk % 8 == 0
    row_shape = tables[0].shape[1:]
    n_tab = len(tables)
    mesh = plsc.VectorSubcoreMesh(core_axis_name="c", subcore_axis_name="s",
                                  num_cores=SC_CORES, num_subcores=SC_SUBCORES)

    def body(*refs):
        tab_refs = refs[:n_tab]
        idx_hbm = refs[n_tab]
        out_refs = refs[n_tab + 1:2 * n_tab + 1]
        idx_v, rows_v, sem = refs[2 * n_tab + 1:]
        wid = lax.axis_index("s") * SC_CORES + lax.axis_index("c")

        @pl.loop(0, per_worker // chunk)
        def _(ci):
            off = pl.multiple_of(wid * per_worker + ci * chunk, 8)
            pltpu.sync_copy(idx_hbm.at[pl.ds(off, chunk)], idx_v)
            for tab, out in zip(tab_refs, out_refs):
                pltpu.async_copy(tab.at[idx_v], rows_v, sem).wait()
                pltpu.sync_copy(rows_v, out.at[pl.ds(off, chunk)])

    return pl.kernel(
        body,
        out_type=[jax.ShapeDtypeStruct((n_rows, *row_shape), t.dtype) for t in tables],
        mesh=mesh,
        scratch_types=[pltpu.VMEM((chunk,), I32),
                       pltpu.VMEM((chunk, *row_shape), tables[0].dtype),
                       pltpu.SemaphoreType.DMA],
        compiler_params=pltpu.CompilerParams(use_tc_tiling_on_sc=True),
        name="sc_gather_rows",
    )(*tables, idx)


SAMPLE_SEQ_GROUP = 8


def _attn_compact_body(dist_ref, near_ref, q_ref, kn_ref, vn_ref, btab_ref, kc_ref, vc_ref,
                       o_ref, bbuf):
    step = pl.program_id(0)
    scale = HEAD_DIM ** -0.5
    ones = jnp.ones((HEAD_DIM, LANES), BF)
    far_bias = btab_ref[REL_MAX_DIST]
    last = lax.broadcasted_iota(I32, (TOPK, 1, 1), 0) == TOPK - 1
    for g in range(SAMPLE_SEQ_GROUP):
        b = step * SAMPLE_SEQ_GROUP + g
        bbuf[g] = jnp.broadcast_to(far_bias[None], (TOPK, HEADS, LANES))

        def fill(t, carry, g=g, b=b):
            bbuf[g, t] = btab_ref[dist_ref[b, t]]
            return carry

        lax.fori_loop(near_ref[b], TOPK, fill, 0)

        is_new = dist_ref[b, TOPK - 1] == 0
        rows = slice(g * TOPK, (g + 1) * TOPK)
        rnd = lambda a: a.astype(BF).astype(F32)
        kc = rnd(jnp.where(last & is_new, kn_ref[g][None], kc_ref[rows]))
        vc = rnd(jnp.where(last & is_new, vn_ref[g][None], vc_ref[rows]))
        prod = (kc * rnd(q_ref[g])[None]).reshape(TOPK * HEADS, HEAD_DIM)
        hi = prod.astype(BF)
        lo = (prod - hi.astype(F32)).astype(BF)
        lg = (_dot(hi, ones) + _dot(lo, ones)).reshape(TOPK, HEADS, LANES) * scale + bbuf[g]
        m = jnp.max(lg, axis=0, keepdims=True)
        p = jnp.exp(lg - m)
        l = jnp.sum(p, axis=0)
        o_ref[g] = (jnp.sum(rnd(p) * vc, axis=0) / l).astype(BF)


def attn_sample_compact(dist, first_near, zs3, k_sel, v_sel, bias_by_dist):
    bsz = dist.shape[0]
    g = SAMPLE_SEQ_GROUP
    assert bsz % g == 0
    zspec = lambda cb: pl.BlockSpec((g, HEADS, HEAD_DIM), lambda i, d, n, cb=cb: (i, cb, 0))
    sel_spec = pl.BlockSpec((g * TOPK, HEADS, HEAD_DIM), lambda i, d, n: (i, 0, 0))
    grid_spec = pltpu.PrefetchScalarGridSpec(
        num_scalar_prefetch=2,
        grid=(bsz // g,),
        in_specs=[zspec(4), zspec(5), zspec(6),
                  pl.BlockSpec(bias_by_dist.shape, lambda i, d, n: (0, 0, 0)),
                  sel_spec, sel_spec],
        out_specs=pl.BlockSpec((g, HEADS, HEAD_DIM), lambda i, d, n: (i, 0, 0)),
        scratch_shapes=[pltpu.VMEM((g, TOPK, HEADS, HEAD_DIM), F32)],
    )
    return pl.pallas_call(
        _attn_compact_body,
        grid_spec=grid_spec,
        out_shape=jax.ShapeDtypeStruct((bsz, HEADS, HEAD_DIM), BF),
        compiler_params=_cparams(("arbitrary",)),
        name="attn_sample_compact",
    )(dist, first_near, zs3, zs3, zs3, bias_by_dist, k_sel, v_sel)


def _bucket_table(max_dist):
    exact = REL_BUCKETS // 2
    d = np.arange(max_dist + 1)
    df = np.maximum(d, 1).astype(np.float32)
    far = exact + (np.log(df / exact) / np.float32(math.log(REL_MAX_DIST / exact))
                   * (REL_BUCKETS - exact)).astype(np.int32)
    return np.where(d < exact, d, np.minimum(far, REL_BUCKETS - 1)).astype(np.int32)


def _bias_tables(rel_bias):
    tab = _bucket_table(2 * LANES)
    assert np.all(tab[REL_MAX_DIST:] == REL_BUCKETS - 1)
    i = np.arange(LANES)
    dist0 = np.maximum(i[:, None] - i[None, :], 0)
    dist1 = LANES + i[:, None] - i[None, :]
    far = np.full((LANES, LANES), REL_BUCKETS - 1)
    idx = np.stack([tab[dist0], tab[dist1], far])
    buckets = jnp.arange(REL_BUCKETS)
    lookup = lambda ix, spec: jnp.einsum(spec, (jnp.asarray(ix)[..., None] == buckets).astype(F32),
                                         rel_bias.astype(F32), precision=lax.Precision.HIGHEST)
    tiles = lookup(idx, "tijb,bh->thij")
    by_dist = jnp.broadcast_to(lookup(tab[:REL_MAX_DIST + 1], "db,bh->dh")[:, :, None],
                               (REL_MAX_DIST + 1, HEADS, LANES))
    return tiles.astype(F32), rel_bias[REL_BUCKETS - 1].astype(F32), by_dist.astype(F32)


def _conv_prompt_body(bg_ref, cg_ref, xt_ref, w_ref, v_ref, st_ref, carry_ref, *, tblock):
    t = pl.program_id(1)

    @pl.when(t == 0)
    def _():
        carry_ref[...] = jnp.zeros_like(carry_ref)

    u = cg_ref[0] * xt_ref[0]
    row = lax.broadcasted_iota(I32, u.shape, 0)
    c0 = carry_ref[0:1]
    c1 = carry_ref[1:2]
    u1 = jnp.where(row == 0, c1, pltpu.roll(u, 1, axis=0))
    u2 = jnp.where(row == 0, c0, jnp.where(row == 1, c1, pltpu.roll(u, 2, axis=0)))
    conv = w_ref[0:1] * u2 + w_ref[1:2] * u1 + w_ref[2:3] * u
    v_ref[0] = (bg_ref[0] * conv).astype(BF)
    last = u[tblock - 2:tblock]
    carry_ref[0:2] = last

    @pl.when(t == pl.num_programs(1) - 1)
    def _():
        st_ref[0] = last


def conv_prompt(zc3, w_conv, tblock=256):
    bsz, seq = zc3.shape[:2]
    c = zc3.shape[2] // 3
    zspec = lambda cb: pl.BlockSpec((1, tblock, c), lambda b, t, cb=cb: (b, t, cb))
    return pl.pallas_call(
        functools.partial(_conv_prompt_body, tblock=tblock),
        grid=(bsz, seq // tblock),
        in_specs=[zspec(0), zspec(1), zspec(2), pl.BlockSpec((3, c), lambda b, t: (0, 0))],
        out_specs=[pl.BlockSpec((1, tblock, c), lambda b, t: (b, t, 0)),
                   pl.BlockSpec((1, 2, c), lambda b, t: (b, 0, 0))],
        out_shape=[jax.ShapeDtypeStruct((bsz, seq, c), BF),
                   jax.ShapeDtypeStruct((bsz, 2, c), F32)],
        scratch_shapes=[pltpu.VMEM((8, c), F32)],
        compiler_params=_cparams(("parallel", "arbitrary")),
        name="conv_prompt",
    )(zc3, zc3, zc3, w_conv)


def _conv_sample_body(bg_ref, cg_ref, xt_ref, w_ref, s0_ref, s1_ref, v_ref, n0_ref, n1_ref):
    u = cg_ref[...] * xt_ref[...]
    conv = w_ref[0:1] * s0_ref[...] + w_ref[1:2] * s1_ref[...] + w_ref[2:3] * u
    v_ref[...] = (bg_ref[...] * conv).astype(BF)
    n0_ref[...] = s1_ref[...]
    n1_ref[...] = u


def conv_sample(zc, w_conv, s0, s1):
    bsz = zc.shape[0]
    c = zc.shape[1] // 3
    zspec = lambda cb: pl.BlockSpec((bsz, c), lambda i, cb=cb: (0, cb))
    full = pl.BlockSpec((bsz, c), lambda i: (0, 0))
    return pl.pallas_call(
        _conv_sample_body,
        grid=(1,),
        in_specs=[zspec(0), zspec(1), zspec(2), pl.BlockSpec((3, c), lambda i: (0, 0)), full, full],
        out_specs=[full, full, full],
        out_shape=[jax.ShapeDtypeStruct((bsz, c), BF)] + [jax.ShapeDtypeStruct((bsz, c), F32)] * 2,
        compiler_params=_cparams(("arbitrary",)),
        name="conv_sample",
    )(zc, zc, zc, w_conv, s0, s1)


def _router_body(x_ref, g_ref, wr_ref, br_ref, *rest):
    h_ref, route_ref = rest[-2:]
    x = x_ref[...]
    ms = jnp.mean(x * x, axis=-1, keepdims=True)
    h = x * lax.rsqrt(ms + RMS_EPS) * g_ref[...]
    hb = h.astype(BF)
    half = h.shape[1] // 2
    bits = pltpu.bitcast(hb.astype(F32), jnp.uint32)
    h_ref[...] = (bits[:, :half] & jnp.uint32(0xFFFF0000)) | (bits[:, half:] >> 16)
    logits = _dot(hb, wr_ref[...]) + br_ref[...]
    lane = lax.broadcasted_iota(I32, logits.shape, 1)
    big = np.int32(1 << 20)
    lg = jnp.where(lane < N_GROUPS, logits, -jnp.inf)
    g_max = jnp.max(lg, axis=-1, keepdims=True)
    g_idx = jnp.min(jnp.where(lg == g_max, lane, big), axis=-1, keepdims=True)
    g_w = 1.0 / jnp.sum(jnp.exp(lg - g_max), axis=-1, keepdims=True)
    first = N_GROUPS + EXP_PER_GROUP * g_idx
    le = jnp.where((lane >= first) & (lane < first + EXP_PER_GROUP), logits, -jnp.inf)
    l1 = jnp.max(le, axis=-1, keepdims=True)
    i1 = jnp.min(jnp.where(le == l1, lane, big), axis=-1, keepdims=True)
    le2 = jnp.where(lane == i1, -jnp.inf, le)
    l2 = jnp.max(le2, axis=-1, keepdims=True)
    i2 = jnp.min(jnp.where(le2 == l2, lane, big), axis=-1, keepdims=True)
    r = jnp.exp(l2 - l1)
    w1 = g_w / (1.0 + r)
    w2 = g_w * r / (1.0 + r)
    e1 = (i1 - N_GROUPS).astype(F32)
    e2 = (i2 - N_GROUPS).astype(F32)
    route_ref[...] = jnp.where(lane == 0, e1, jnp.where(lane == 1, e2,
                               jnp.where(lane == 2, w1, jnp.where(lane == 3, w2, 0.0))))


def moe_router(x, g, wr, br, tm, total_rows, row0, bufs=None):
    m, k = x.shape
    assert row0 % tm == 0
    off = row0 // tm
    in_specs = [pl.BlockSpec((tm, k), lambda i: (i, 0)),
                pl.BlockSpec((1, k), lambda i: (0, 0)),
                pl.BlockSpec((k, LANES), lambda i: (0, 0)),
                pl.BlockSpec((1, LANES), lambda i: (0, 0))]
    args = [x, g.reshape(1, k), wr, br]
    aliases = {}
    if bufs is not None:
        in_specs += [pl.BlockSpec(memory_space=pl.ANY)] * 2
        args += list(bufs)
        aliases = {4: 0, 5: 1}
    return pl.pallas_call(
        _router_body,
        grid=(m // tm,),
        in_specs=in_specs,
        out_specs=[pl.BlockSpec((tm, k // 2), lambda i: (i + off, 0)),
                   pl.BlockSpec((tm, LANES), lambda i: (i + off, 0))],
        out_shape=[jax.ShapeDtypeStruct((total_rows, k // 2), jnp.uint32),
                   jax.ShapeDtypeStruct((total_rows, LANES), F32)],
        input_output_aliases=aliases,
        compiler_params=_cparams(("parallel",)),
        name="moe_router",
    )(*args)


def _experts_body(te_ref, nu_ref, h_ref, wg_ref, wu_ref, wd_ref, o_ref, wg_bf, wu_bf, wd_bf):
    i = pl.program_id(0)

    @pl.when((i == 0) | (te_ref[i] != te_ref[jnp.maximum(i - 1, 0)]))
    def _():
        wg_bf[...] = wg_ref[0].astype(BF)
        wu_bf[...] = wu_ref[0].astype(BF)
        wd_bf[...] = wd_ref[0].astype(BF)

    @pl.when(i < nu_ref[0])
    def _():
        words = h_ref[...]
        left = pltpu.bitcast(words & jnp.uint32(0xFFFF0000), F32)
        right = pltpu.bitcast(words << 16, F32)
        h = jnp.concatenate([left, right], axis=1).astype(BF)
        a = _dot(h, wg_bf[...])
        b = _dot(h, wu_bf[...])
        hid = a * _sigmoid(a) * b
        o_ref[...] = _dot(hid.astype(BF), wd_bf[...])

    @pl.when(i >= nu_ref[0])
    def _():
        o_ref[...] = jnp.zeros_like(o_ref)


def moe_experts(tile_expert, n_used, hs, wg, wu, wd):
    p = hs.shape[0]
    k = wg.shape[1]
    f = wg.shape[2]
    n_tiles = p // MOE_TILE
    grid_spec = pltpu.PrefetchScalarGridSpec(
        num_scalar_prefetch=2,
        grid=(n_tiles,),
        in_specs=[pl.BlockSpec((MOE_TILE, k // 2), lambda i, te, nu: (i, 0)),
                  pl.BlockSpec((1, k, f), lambda i, te, nu: (te[i], 0, 0)),
                  pl.BlockSpec((1, k, f), lambda i, te, nu: (te[i], 0, 0)),
                  pl.BlockSpec((1, f, k), lambda i, te, nu: (te[i], 0, 0))],
        out_specs=pl.BlockSpec((MOE_TILE, k), lambda i, te, nu: (i, 0)),
        scratch_shapes=[pltpu.VMEM((k, f), BF), pltpu.VMEM((k, f), BF), pltpu.VMEM((f, k), BF)],
    )
    return pl.pallas_call(
        _experts_body,
        grid_spec=grid_spec,
        out_shape=jax.ShapeDtypeStruct((p, k), F32),
        compiler_params=_cparams(("arbitrary",)),
        name="moe_experts",
    )(tile_expert, n_used, hs, wg, wu, wd)


def _combine_body(x_ref, a_ref, b_ref, route_ref, g_ref, o_ref, *, normalize):
    y = x_ref[...] + route_ref[:, 2:3] * a_ref[...] + route_ref[:, 3:4] * b_ref[...]
    if normalize:
        ms = jnp.mean(y * y, axis=-1, keepdims=True)
        y = y * lax.rsqrt(ms + RMS_EPS) * g_ref[...]
    o_ref[...] = y


def moe_combine(x, out_a, out_b, route, tm, norm_g=None):
    m, k = x.shape
    row = pl.BlockSpec((tm, k), lambda i: (i, 0))
    g = jnp.ones((1, k), F32) if norm_g is None else norm_g.reshape(1, k)
    return pl.pallas_call(
        functools.partial(_combine_body, normalize=norm_g is not None),
        grid=(m // tm,),
        in_specs=[row, row, row, pl.BlockSpec((tm, LANES), lambda i: (i, 0)),
                  pl.BlockSpec((1, k), lambda i: (0, 0))],
        out_specs=row,
        out_shape=jax.ShapeDtypeStruct((m, k), F32),
        compiler_params=_cparams(("parallel",)),
        name="moe_combine",
    )(x, out_a, out_b, route, g)


def _rank_within_expert(onehot):
    n, e = onehot.shape
    blk = LANES
    assert n % blk == 0
    oh = onehot.astype(F32).reshape(n // blk, blk, e)
    strict = jnp.asarray(np.tril(np.ones((blk, blk), np.float32), -1))
    within = jnp.einsum("ij,bjk->bik", strict, oh, precision=lax.Precision.HIGHEST)
    totals = jnp.sum(oh, axis=1)
    before = jnp.cumsum(totals, axis=0) - totals
    rank = (within + before[:, None, :]).reshape(n, e)
    return jnp.sum(rank * onehot.astype(F32), axis=1).astype(I32), jnp.sum(totals, axis=0).astype(I32)


def hier_moe(xs, tms, g, wrg, brg, wre, bre, wg, wu, wd, layer, out_norm_g=None):
    d = xs[0].shape[1]
    m = sum(x.shape[0] for x in xs)
    wr = jnp.zeros((d, LANES), F32)
    wr = wr.at[:, :N_GROUPS].set(wrg).at[:, N_GROUPS:N_GROUPS + N_EXPERTS].set(wre.reshape(d, N_EXPERTS))
    br = jnp.zeros((1, LANES), F32)
    br = br.at[0, :N_GROUPS].set(brg).at[0, N_GROUPS:N_GROUPS + N_EXPERTS].set(bre.reshape(N_EXPERTS))
    bufs, row0 = None, 0
    for x, tm in zip(xs, tms):
        bufs = moe_router(x, g, wr.astype(BF), br, tm, m, row0, bufs)
        row0 += x.shape[0]
    h_bf, route = bufs

    eid = route[:, 0:2].astype(I32).reshape(-1)
    onehot = eid[:, None] == jnp.arange(N_EXPERTS, dtype=I32)[None, :]
    rank, counts = _rank_within_expert(onehot)
    padded = ((counts + MOE_TILE - 1) // MOE_TILE) * MOE_TILE
    ends = jnp.cumsum(padded)
    pos = jnp.sum(jnp.where(onehot, (ends - padded)[None, :], 0), axis=1) + rank
    n_rows = 2 * m + N_EXPERTS * MOE_TILE
    n_rows = -(-n_rows // MOE_TILE) * MOE_TILE
    token = jnp.zeros((n_rows,), I32).at[pos].set(jnp.arange(2 * m, dtype=I32) // 2)
    tile_start = jnp.arange(n_rows // MOE_TILE, dtype=I32) * MOE_TILE
    tile_expert = jnp.minimum(jnp.sum(tile_start[:, None] >= ends[None, :], axis=1),
                              N_EXPERTS - 1).astype(I32)
    n_used = (ends[-1] // MOE_TILE).astype(I32).reshape(1)

    hs = jnp.take(h_bf, token, axis=0, mode="clip")
    out = moe_experts(tile_expert + layer * N_EXPERTS, n_used, hs, wg, wu, wd)
    pos2 = pos.reshape(m, 2)
    res, r0 = [], 0
    for x, tm in zip(xs, tms):
        rows = slice(r0, r0 + x.shape[0])
        res.append(moe_combine(x, jnp.take(out, pos2[rows, 0], axis=0, mode="clip"),
                               jnp.take(out, pos2[rows, 1], axis=0, mode="clip"),
                               route[rows], tm, out_norm_g))
        r0 += x.shape[0]
    return res


def kernel(x_prompt, x_sample, cache_k, cache_v, cache_ki, state_hgrn, state_conv, page_table,
           norm_mix_g, norm_ffn_g, final_g, w_in_even, w_out_even, hgrn_lb_logits, hgrn_norm_g,
           rel_bias, w_in_conv, w_conv, w_out_conv, w_router_g, b_router_g, w_router_e,
           b_router_e, w_gate, w_up, w_down):
    bsz, seq, d = x_prompt.shape
    dec = x_sample.shape[0]
    n_p = bsz * seq
    tm_p, tm_s = 512, 128
    tms = (tm_p, tm_s)
    assert n_p % tm_p == 0 and dec % tm_s == 0 and x_sample.shape[1] == 1
    width = HEADS * HEAD_DIM
    even_in = w_in_even.shape[2]
    even_pad = -(-even_in // IN_PROJ_TN) * IN_PROJ_TN
    tail0 = TAIL_BLOCK * LANES
    n_pool = cache_k.shape[1]

    xp = x_prompt.reshape(n_p, d)
    xs = x_sample.reshape(dec, d)

    lbs = jnp.cumsum(jax.nn.softmax(hgrn_lb_logits.astype(F32), axis=0), axis=0)[:-1]
    bias_tiles, bias_far, bias_by_dist = _bias_tables(rel_bias)
    expert_w = [w.reshape(-1, *w.shape[2:]) for w in (w_gate, w_up, w_down)]

    w_in = jnp.pad(w_in_even[0], ((0, 0), (0, even_pad - even_in))).astype(BF)
    zp, k_new_p, v_new_p, kv_bf = even_projection(xp, norm_mix_g[0], w_in, 2 * tm_p)
    zt = norm_matmul(xp, norm_mix_g[0], w_in[:, tail0:tail0 + LANES], 2 * tm_p, LANES)
    zs = norm_matmul(xs, norm_mix_g[0], w_in, tm_s, IN_PROJ_TN)
    zp3 = zp.reshape(bsz, seq, 6 * width)
    zt3 = zt.reshape(bsz, seq, LANES)
    zs3 = zs.reshape(dec, even_pad // LANES, LANES)

    oa_p, hgrn_p = hgrn_prompt(zp3, lbs[0], hgrn_norm_g[0])
    oa_s, hgrn_s = hgrn_sample(zs3, lbs[0], hgrn_norm_g[0], state_hgrn[0])

    wt = jnp.swapaxes(zt3[:, :, IDX_DIM:IDX_DIM + IDX_HEADS], 1, 2)
    mask_p = indexer_prompt_t(zp3, wt, zt3)
    ob_p = jnp.swapaxes(attn_prompt_t(zp3, kv_bf.reshape(bsz, seq, 2 * width), mask_p,
                                      jnp.swapaxes(bias_tiles, -1, -2), bias_far), 1, 2)

    qi3 = zs[:, 7 * width:8 * width].reshape(dec, IDX_HEADS, IDX_DIM)
    ki_new = zs[:, tail0:tail0 + IDX_DIM].reshape(dec, 1, IDX_DIM)
    wcol = zs[:, tail0 + IDX_DIM:tail0 + IDX_DIM + IDX_HEADS].reshape(dec, IDX_HEADS, 1)
    pages = page_table + 0 * n_pool
    scores_s = indexer_sample(pages, qi3, wcol, ki_new,
                              jnp.swapaxes(cache_ki, -1, -2).reshape(-1, IDX_DIM, PAGE))
    mask_s = select_topk(scores_s, TOPK)
    sel = jnp.transpose(mask_s, (1, 0, 2))
    n_pages = page_table.shape[1]
    sel_off = jnp.concatenate([jnp.zeros((dec, 1), F32),
                               jnp.cumsum(jnp.sum(sel, axis=2), axis=1)], axis=1)
    slot = jnp.arange(TOPK, dtype=F32)
    page_of = jnp.sum(sel_off[:, None, 1:] <= slot[None, :, None], axis=2)
    page_1h = (page_of[:, :, None] == jnp.arange(n_pages + 1)[None, None, :]).astype(F32)
    local = slot[None, :] - jnp.einsum("brp,bp->br", page_1h, sel_off[:, :-1],
                                       precision=lax.Precision.HIGHEST)
    within = jnp.einsum("brp,bpl->brl", page_1h, jnp.cumsum(sel, axis=2),
                        precision=lax.Precision.HIGHEST)
    lane_of = jnp.sum(within <= local[:, :, None], axis=2)
    sel_idx = (page_of * PAGE + lane_of).astype(I32)
    past = n_pages * PAGE
    page_id = jnp.einsum("brp,bp->br", page_1h[:, :, :n_pages], pages.astype(F32),
                         precision=lax.Precision.HIGHEST).astype(I32)
    rows = jnp.where(page_of < n_pages, page_id * PAGE + lane_of, 0).astype(I32).reshape(-1)
    rows, zp3 = lax.optimization_barrier((rows, zp3))
    k_sel, v_sel = sc_gather_rows([cache_k.reshape(-1, HEADS, HEAD_DIM),
                                   cache_v.reshape(-1, HEADS, HEAD_DIM)], rows)
    dist = jnp.minimum(past - sel_idx, REL_MAX_DIST).astype(I32)
    first_near = jnp.sum(dist >= REL_MAX_DIST, axis=1).astype(I32)
    ob_s = attn_sample_compact(dist, first_near, zs3, k_sel, v_sel, bias_by_dist)

    w_out = w_out_even[0].astype(BF)
    w_out_ab = [w_out[:width], w_out[width:]]
    xp = matmul_residual([oa_p.reshape(n_p, width), ob_p.reshape(n_p, width)], w_out_ab, xp, 2 * tm_p, 1024)
    xs = matmul_residual([oa_s.reshape(dec, width), ob_s.reshape(dec, width)], w_out_ab, xs, tm_s, 512)

    xp, xs = hier_moe([xp, xs], tms, norm_ffn_g[0], w_router_g[0], b_router_g[0], w_router_e[0],
                      b_router_e[0], *expert_w, 0)

    w_in_c = w_in_conv[0].astype(BF)
    zcp = norm_matmul(xp, norm_mix_g[1], w_in_c, 2 * tm_p, 1024)
    zcs = norm_matmul(xs, norm_mix_g[1], w_in_c, tm_s, 512)
    cw = zcp.shape[1] // 3
    v_p, conv_p = conv_prompt(zcp.reshape(bsz, seq, 3 * cw), w_conv[0])
    v_s, cs0, cs1 = conv_sample(zcs, w_conv[0], state_conv[0, :, 0], state_conv[0, :, 1])
    w_out_c = [w_out_conv[0].astype(BF)]
    xp = matmul_residual([v_p.reshape(n_p, cw)], w_out_c, xp, 2 * tm_p, 1024)
    xs = matmul_residual([v_s], w_out_c, xs, tm_s, 512)

    yp, ys = hier_moe([xp, xs], tms, norm_ffn_g[1], w_router_g[1], b_router_g[1], w_router_e[1],
                      b_router_e[1], *expert_w, 1, out_norm_g=final_g)

    kcol, vcol = 5 * width, 6 * width
    heads = lambda a, n: a.reshape(1, *n, HEADS, HEAD_DIM)
    return (yp.reshape(bsz, seq, d),
            ys.reshape(dec, 1, d),
            heads(k_new_p, (bsz, seq)),
            heads(v_new_p, (bsz, seq)),
            zt[:, :IDX_DIM].reshape(1, bsz, seq, IDX_DIM),
            hgrn_p[None],
            conv_p[None],
            heads(zs[:, kcol:kcol + width], (dec, 1)),
            heads(zs[:, vcol:vcol + width], (dec, 1)),
            zs[:, tail0:tail0 + IDX_DIM].reshape(1, dec, 1, IDX_DIM),
            hgrn_s[None],
            jnp.stack([cs0, cs1], axis=1)[None])
```

```python
import functools
import math

import numpy as np
import jax
import jax.numpy as jnp
from jax import lax
from jax.experimental import pallas as pl
from jax.experimental.pallas import tpu as pltpu
from jax.experimental.pallas import tpu_sc as plsc

F32 = jnp.float32
BF = jnp.bfloat16
I32 = jnp.int32

RMS_EPS = 1e-6
LANES = 128
NEG_BIG = -1e30
VMEM_LIMIT = 56 * 1024 * 1024

HEADS = 8
HEAD_DIM = 128
IDX_HEADS = 16
IDX_DIM = 64
TOPK = 256
REL_BUCKETS = 32
REL_MAX_DIST = 128
N_EXPERTS = 16
EXP_PER_GROUP = 4
N_GROUPS = 4
PAGE = 128

HGRN_CHUNK = 128
HGRN_HEAD_GROUP = 8
MOE_TILE = 256
IN_PROJ_TN = 640
TAIL_BLOCK = 8 * HEADS * HEAD_DIM // LANES


def _cparams(sem):
    return pltpu.CompilerParams(dimension_semantics=sem, vmem_limit_bytes=VMEM_LIMIT)


def _dot(a, b):
    return jnp.dot(a, b, preferred_element_type=F32)


def _dot_nt(a, b):
    return lax.dot_general(a, b, (((1,), (1,)), ((), ())), preferred_element_type=F32)


def _dot_tn(a, b):
    return lax.dot_general(a, b, (((0,), (0,)), ((), ())), preferred_element_type=F32)


def _sigmoid(x):
    return 1.0 / (1.0 + jnp.exp(-x))


def _norm_mm_body(x_ref, g_ref, w_ref, o_ref, h_ref):
    @pl.when(pl.program_id(1) == 0)
    def _():
        x = x_ref[...]
        ms = jnp.mean(x * x, axis=-1, keepdims=True)
        h_ref[...] = (x * lax.rsqrt(ms + RMS_EPS) * g_ref[...]).astype(BF)

    o_ref[...] = _dot(h_ref[...], w_ref[...])


def norm_matmul(x, g, w_bf, tm, tn):
    m, k = x.shape
    n = w_bf.shape[1]
    return pl.pallas_call(
        _norm_mm_body,
        grid=(m // tm, n // tn),
        in_specs=[pl.BlockSpec((tm, k), lambda i, j: (i, 0)),
                  pl.BlockSpec((1, k), lambda i, j: (0, 0)),
                  pl.BlockSpec((k, tn), lambda i, j: (0, j))],
        out_specs=pl.BlockSpec((tm, tn), lambda i, j: (i, j)),
        out_shape=jax.ShapeDtypeStruct((m, n), F32),
        scratch_shapes=[pltpu.VMEM((tm, k), BF)],
        compiler_params=_cparams(("parallel", "arbitrary")),
        name="norm_matmul",
    )(x, g.reshape(1, k), w_bf)


EVEN_TN = 1024


def _even_proj_body(x_ref, g_ref, w_ref, z_ref, k_ref, v_ref, kv_ref, h_ref, *, per):
    j = pl.program_id(1)

    @pl.when(j == 0)
    def _():
        x = x_ref[...]
        ms = jnp.mean(x * x, axis=-1, keepdims=True)
        h_ref[...] = (x * lax.rsqrt(ms + RMS_EPS) * g_ref[...]).astype(BF)

    res = _dot(h_ref[...], w_ref[...])
    sec = j // per

    @pl.when((sec < 5) | (sec == 7))
    def _():
        z_ref[...] = res

    @pl.when(sec == 5)
    def _():
        k_ref[...] = res
        kv_ref[...] = res.astype(BF)

    @pl.when(sec == 6)
    def _():
        v_ref[...] = res
        kv_ref[...] = res.astype(BF)


def even_projection(x, g, w_bf, tm):
    m, k = x.shape
    width = HEADS * HEAD_DIM
    per = width // EVEN_TN
    clip = lambda a, lo, hi: jnp.minimum(jnp.maximum(a, lo), hi)
    return pl.pallas_call(
        functools.partial(_even_proj_body, per=per),
        grid=(m // tm, 8 * per),
        in_specs=[pl.BlockSpec((tm, k), lambda i, j: (i, 0)),
                  pl.BlockSpec((1, k), lambda i, j: (0, 0)),
                  pl.BlockSpec((k, EVEN_TN), lambda i, j: (0, j))],
        out_specs=[pl.BlockSpec((tm, EVEN_TN), lambda i, j: (i, j - clip(j - (5 * per - 1), 0, 2 * per))),
                   pl.BlockSpec((tm, EVEN_TN), lambda i, j: (i, clip(j - 5 * per, 0, per - 1)),
                                pipeline_mode=pl.Buffered(1)),
                   pl.BlockSpec((tm, EVEN_TN), lambda i, j: (i, clip(j - 6 * per, 0, per - 1)),
                                pipeline_mode=pl.Buffered(1)),
                   pl.BlockSpec((tm, EVEN_TN), lambda i, j: (i, clip(j - 5 * per, 0, 2 * per - 1)),
                                pipeline_mode=pl.Buffered(1))],
        out_shape=[jax.ShapeDtypeStruct((m, 6 * width), F32),
                   jax.ShapeDtypeStruct((m, width), F32),
                   jax.ShapeDtypeStruct((m, width), F32),
                   jax.ShapeDtypeStruct((m, 2 * width), BF)],
        scratch_shapes=[pltpu.VMEM((tm, k), BF)],
        compiler_params=_cparams(("parallel", "arbitrary")),
        name="even_projection",
    )(x, g.reshape(1, k), w_bf)


def _mm_res_body(*refs, n_lhs):
    a_refs = refs[:n_lhs]
    w_refs = refs[n_lhs:2 * n_lhs]
    r_ref = refs[2 * n_lhs]
    o_ref = refs[2 * n_lhs + 1]
    acc = r_ref[...]
    for a_ref, w_ref in zip(a_refs, w_refs):
        acc = acc + _dot(a_ref[...], w_ref[...])
    o_ref[...] = acc


def matmul_residual(lhs, ws_bf, res, tm, tn):
    m, n = res.shape
    n_lhs = len(lhs)
    assert all(a.dtype == BF for a in lhs)
    in_specs = ([pl.BlockSpec((tm, a.shape[1]), lambda i, j: (i, 0)) for a in lhs]
                + [pl.BlockSpec((w.shape[0], tn), lambda i, j: (0, j)) for w in ws_bf]
                + [pl.BlockSpec((tm, tn), lambda i, j: (i, j))])
    return pl.pallas_call(
        functools.partial(_mm_res_body, n_lhs=n_lhs),
        grid=(m // tm, n // tn),
        in_specs=in_specs,
        out_specs=pl.BlockSpec((tm, tn), lambda i, j: (i, j)),
        out_shape=jax.ShapeDtypeStruct((m, n), F32),
        compiler_params=_cparams(("parallel", "arbitrary")),
        name="matmul_residual",
    )(*lhs, *ws_bf, res)


def _hgrn_static(c):
    levels = []
    m = 1
    while m < c:
        levels.append(m)
        m *= 2
    t = np.arange(c)
    rows = [t[None, :] <= t[:, None]]
    masks = [np.eye(c, dtype=bool)]
    for m in levels:
        blk = t // (2 * m)
        pos = t % (2 * m)
        bnd = blk * 2 * m + m - 1
        right = pos >= m
        left = pos < m
        e_rows = (t[None, :] > bnd[:, None]) & (t[None, :] <= t[:, None]) & right[:, None]
        f_rows = (t[None, :] > t[:, None]) & (t[None, :] <= bnd[:, None]) & left[:, None]
        rows.append(e_rows | f_rows)
        masks.append((blk[:, None] == blk[None, :]) & right[:, None] & left[None, :])
    m_all = np.stack(rows).astype(np.float32)
    masks = np.stack(masks).astype(np.float32)
    return m_all, masks, len(levels)


def _hgrn_gates(qa, fa, lb):
    f = lb + (1.0 - lb) * _sigmoid(fa)
    q = qa * _sigmoid(qa)
    return q, f


def _hgrn_prompt_body(qa_ref, fa_ref, ia_ref, ga_ref, lb_ref, gn_ref, mall_ref, masks_ref,
                      oa_ref, st_out_ref, st_ref, *, chunk, tblock, n_levels):
    c = chunk
    t_idx = pl.program_id(1)

    @pl.when(t_idx == 0)
    def _():
        st_ref[...] = jnp.zeros_like(st_ref)

    gn = gn_ref[...]

    def chunk_step(ci, carry):
        r0 = pl.multiple_of(ci * c, c)
        for h0 in range(0, HEADS, HGRN_HEAD_GROUP):
            hs = range(h0, h0 + HGRN_HEAD_GROUP)
            cols = {h: slice(h * HEAD_DIM, (h + 1) * HEAD_DIM) for h in hs}
            q, k, g_hi, g_lo, v = {}, {}, {}, {}, {}
            for h in hs:
                q[h], f = _hgrn_gates(qa_ref[0, pl.ds(r0, c), cols[h]],
                                      fa_ref[0, pl.ds(r0, c), cols[h]], lb_ref[:, cols[h]])
                k[h] = 1.0 - f
                g = jnp.log(f)
                g_hi[h] = g.astype(BF)
                g_lo[h] = (g - g_hi[h].astype(F32)).astype(BF)
                v[h] = ia_ref[0, pl.ds(r0, c), cols[h]].astype(BF)
            b = {h: _dot(mall_ref[0], g_hi[h]) + _dot(mall_ref[0], g_lo[h]) for h in hs}
            o = {h: _dot_nt((q[h] * jnp.exp(b[h])).astype(BF), st_ref[h].astype(BF)) for h in hs}
            a = {h: masks_ref[0] * _dot_nt(q[h].astype(BF), k[h].astype(BF)) for h in hs}
            for li in range(n_levels):
                ml = mall_ref[1 + li]
                w = {h: jnp.exp(_dot(ml, g_hi[h]) + _dot(ml, g_lo[h])) for h in hs}
                for h in hs:
                    a[h] = a[h] + masks_ref[1 + li] * _dot_nt((q[h] * w[h]).astype(BF),
                                                              (k[h] * w[h]).astype(BF))
            for h in hs:
                o[h] = o[h] + _dot(a[h].astype(BF), v[h])
            for h in hs:
                b_last = b[h][c - 1:c]
                k_st = (k[h] * jnp.exp(b_last - b[h])).astype(BF)
                st_ref[h] = st_ref[h] * jnp.exp(b_last) + _dot_tn(v[h], k_st)
            for h in hs:
                ga = ga_ref[0, pl.ds(r0, c), cols[h]]
                ms = jnp.mean(o[h] * o[h], axis=-1, keepdims=True)
                oa_ref[0, pl.ds(r0, c), cols[h]] = (o[h] * lax.rsqrt(ms + RMS_EPS) * gn
                                                    * (ga * _sigmoid(ga))).astype(BF)
        return carry

    lax.fori_loop(0, tblock // c, chunk_step, 0)

    @pl.when(t_idx == pl.num_programs(1) - 1)
    def _():
        st_out_ref[0] = st_ref[...]


def hgrn_prompt(z3, lb, gn, tblock=256, chunk=HGRN_CHUNK):
    bsz, seq = z3.shape[:2]
    width = HEADS * HEAD_DIM
    m_all, masks, n_levels = _hgrn_static(chunk)
    zspec = lambda cb: pl.BlockSpec((1, tblock, width), lambda b, t, cb=cb: (b, t, cb))
    oa, st = pl.pallas_call(
        functools.partial(_hgrn_prompt_body, chunk=chunk, tblock=tblock, n_levels=n_levels),
        grid=(bsz, seq // tblock),
        in_specs=[zspec(0), zspec(1), zspec(2), zspec(3),
                  pl.BlockSpec((1, width), lambda b, t: (0, 0)),
                  pl.BlockSpec((1, HEAD_DIM), lambda b, t: (0, 0)),
                  pl.BlockSpec(m_all.shape, lambda b, t: (0, 0, 0)),
                  pl.BlockSpec(masks.shape, lambda b, t: (0, 0, 0))],
        out_specs=[pl.BlockSpec((1, tblock, width), lambda b, t: (b, t, 0)),
                   pl.BlockSpec((1, HEADS, HEAD_DIM, HEAD_DIM), lambda b, t: (b, 0, 0, 0))],
        out_shape=[jax.ShapeDtypeStruct((bsz, seq, width), BF),
                   jax.ShapeDtypeStruct((bsz, HEADS, HEAD_DIM, HEAD_DIM), F32)],
        scratch_shapes=[pltpu.VMEM((HEADS, HEAD_DIM, HEAD_DIM), F32)],
        compiler_params=_cparams(("parallel", "arbitrary")),
        name="hgrn_prompt",
    )(z3, z3, z3, z3, lb.reshape(1, width), gn.reshape(1, HEAD_DIM),
      jnp.asarray(m_all, BF), jnp.asarray(masks, F32))
    return oa, jnp.swapaxes(st, -1, -2)


def _col(row, eye):
    return jnp.sum(eye * row, axis=1, keepdims=True)


def _hgrn_sample_body(qa_ref, fa_ref, ia_ref, ga_ref, lb_ref, gn_ref, s_ref, oa_ref, so_ref):
    eye = (lax.broadcasted_iota(I32, (HEAD_DIM, HEAD_DIM), 0)
           == lax.broadcasted_iota(I32, (HEAD_DIM, HEAD_DIM), 1)).astype(F32)
    rnd = lambda a: a.astype(BF).astype(F32)
    for g in range(SAMPLE_SEQ_GROUP):
        q8, f8 = _hgrn_gates(qa_ref[g], fa_ref[g], lb_ref[...])
        ga = ga_ref[g]
        gate = ga * _sigmoid(ga)
        kr = rnd(1.0 - f8)
        vr = rnd(ia_ref[g])
        qfr = rnd(q8 * f8)
        qk = rnd(jnp.sum(rnd(q8) * kr, axis=-1, keepdims=True))
        outs = []
        for h in range(HEADS):
            f_col = _col(f8[h:h + 1], eye)
            s_old = s_ref[g, h]
            so_ref[g, h] = f_col * s_old + _col(kr[h:h + 1], eye) * vr[h:h + 1]
            outs.append(qk[h:h + 1] * vr[h:h + 1]
                        + jnp.sum(_col(qfr[h:h + 1], eye) * rnd(s_old), axis=0, keepdims=True))
        o = jnp.concatenate(outs, axis=0)
        ms = jnp.mean(o * o, axis=-1, keepdims=True)
        oa_ref[g] = (o * lax.rsqrt(ms + RMS_EPS) * gn_ref[...] * gate).astype(BF)


def hgrn_sample(zs3, lb, gn, s0):
    bsz = zs3.shape[0]
    g = SAMPLE_SEQ_GROUP
    assert bsz % g == 0
    zspec = lambda cb: pl.BlockSpec((g, HEADS, HEAD_DIM), lambda b, cb=cb: (b, cb, 0))
    sspec = pl.BlockSpec((g, HEADS, HEAD_DIM, HEAD_DIM), lambda b: (b, 0, 0, 0))
    return pl.pallas_call(
        _hgrn_sample_body,
        grid=(bsz // g,),
        in_specs=[zspec(0), zspec(1), zspec(2), zspec(3),
                  pl.BlockSpec((HEADS, HEAD_DIM), lambda b: (0, 0)),
                  pl.BlockSpec((1, HEAD_DIM), lambda b: (0, 0)),
                  sspec],
        out_specs=[pl.BlockSpec((g, HEADS, HEAD_DIM), lambda b: (b, 0, 0)), sspec],
        out_shape=[jax.ShapeDtypeStruct((bsz, HEADS, HEAD_DIM), BF),
                   jax.ShapeDtypeStruct(s0.shape, F32)],
        compiler_params=_cparams(("parallel",)),
        name="hgrn_sample",
    )(zs3, zs3, zs3, zs3, lb.reshape(HEADS, HEAD_DIM), gn.reshape(1, HEAD_DIM), s0)


_KEY_NEG_INF = np.int32(np.uint32(0x807FFFFF).astype(np.int64) - (1 << 32))
_INT_MIN = np.int32(-(1 << 31))


def _count(u_ref, n_groups, group, thr, cmp):
    rows = u_ref.shape[1]
    step = min(rows, LANES)
    parts = []
    for r0 in range(0, rows, step):
        t = jnp.broadcast_to(thr[r0:r0 + step], (step, LANES))

        def body(gi, acc, r0=r0, t=t):
            for i in range(group):
                acc = acc + cmp(u_ref[gi * group + i, r0:r0 + step], t).astype(F32)
            return acc

        acc = lax.fori_loop(0, n_groups, body, jnp.zeros((step, LANES), F32))
        parts.append(jnp.sum(acc, axis=-1, keepdims=True))
    return parts[0] if len(parts) == 1 else jnp.concatenate(parts, axis=0)


def _topk_mask(score_ref, u_ref, write_tile, n_tiles, n_groups, group, k):
    rows = score_ref.shape[1]
    n_live = n_groups * group

    def to_key(t, carry):
        bits = pltpu.bitcast(score_ref[t], I32)
        u_ref[t] = jnp.where(bits < 0, bits ^ np.int32(0x7FFFFFFF), bits)
        return carry

    lax.fori_loop(0, n_live, to_key, 0)

    kf = float(k)
    ge = lambda u, t: u >= t
    cnt = _count(u_ref, n_groups, group, jnp.zeros((rows, 1), I32), ge)
    lo = jnp.where(cnt >= kf, np.int32(0), _INT_MIN)

    def bit_step(i, lo):
        cand = lo | (np.int32(1) << (30 - i))
        cnt = _count(u_ref, n_groups, group, cand, ge)
        return jnp.where(cnt >= kf, cand, lo)

    lo = lax.fori_loop(0, 31, bit_step, lo)
    c_gt = _count(u_ref, n_groups, group, lo, lambda u, t: u > t)
    c_eq = _count(u_ref, n_groups, group, lo, lambda u, t: u == t)
    need = kf - c_gt
    real = lo > _KEY_NEG_INF
    excess = jnp.where(real & (c_eq > need), 1.0, 0.0)
    any_excess = jnp.max(excess) > 0.0

    @pl.when(jnp.logical_not(any_excess))
    def _():
        def emit(t, carry):
            u = u_ref[t]
            write_tile(t, jnp.where((u >= lo) & (u > _KEY_NEG_INF), 1.0, 0.0))
            return carry

        lax.fori_loop(0, n_live, emit, 0)

    @pl.when(any_excess)
    def _():
        upper = (lax.broadcasted_iota(I32, (LANES, LANES), 0)
                 <= lax.broadcasted_iota(I32, (LANES, LANES), 1)).astype(BF)

        def emit(t, seen):
            u = u_ref[t]
            eq = jnp.where(u == lo, 1.0, 0.0)
            prefix = seen + _dot(eq.astype(BF), upper)
            take = (u > lo) | ((u == lo) & (prefix <= need))
            write_tile(t, jnp.where(take & (u > _KEY_NEG_INF), 1.0, 0.0))
            return seen + jnp.sum(eq, axis=-1, keepdims=True)

        lax.fori_loop(0, n_live, emit, jnp.zeros((rows, 1), F32))

    def clear(t, carry):
        write_tile(t, jnp.zeros((rows, LANES), F32))
        return carry

    lax.fori_loop(n_live, n_tiles, clear, 0)


QW = 256
SUB = 8
IDX_KCHUNK = 512
ATTN_KSTEP = 512
ATTN_TILES = ATTN_KSTEP // LANES


def _row_all(x8, op):
    return jnp.broadcast_to(op(x8, axis=0, keepdims=True), x8.shape)


def _count_t(u_ref, n_groups, group, thr_row, cmp):
    qw = u_ref.shape[2]
    thr = jnp.broadcast_to(thr_row, (SUB, qw))

    def body(gi, acc):
        for i in range(group):
            u = u_ref[gi * group + i].reshape(LANES // SUB, SUB, qw)
            acc = acc + jnp.sum(cmp(u, thr[None]).astype(F32), axis=0)
        return acc

    acc = lax.fori_loop(0, n_groups, body, jnp.zeros((SUB, qw), F32))
    return jnp.sum(acc, axis=0, keepdims=True)


def _topk_mask_t(score_ref, u_ref, write_tile, n_tiles, n_groups, group, k):
    qw = score_ref.shape[2]
    n_live = n_groups * group

    def to_key(t, carry):
        bits = pltpu.bitcast(score_ref[t], I32)
        u_ref[t] = jnp.where(bits < 0, bits ^ np.int32(0x7FFFFFFF), bits)
        return carry

    lax.fori_loop(0, n_live, to_key, 0)

    kf = float(k)
    ge = lambda u, t: u >= t
    cnt = _count_t(u_ref, n_groups, group, jnp.zeros((1, qw), I32), ge)
    lo = jnp.where(cnt >= kf, np.int32(0), _INT_MIN)

    def bit_step(i, lo):
        cand = lo | (np.int32(1) << (30 - i))
        cnt = _count_t(u_ref, n_groups, group, cand, ge)
        return jnp.where(cnt >= kf, cand, lo)

    lo = lax.fori_loop(0, 31, bit_step, lo)
    c_gt = _count_t(u_ref, n_groups, group, lo, lambda u, t: u > t)
    c_eq = _count_t(u_ref, n_groups, group, lo, lambda u, t: u == t)
    need = kf - c_gt
    excess = jnp.where((lo > _KEY_NEG_INF) & (c_eq > need), 1.0, 0.0)
    any_excess = jnp.max(excess) > 0.0
    lo_b = jnp.broadcast_to(lo, (LANES, qw))

    @pl.when(jnp.logical_not(any_excess))
    def _():
        def emit(t, carry):
            u = u_ref[t]
            write_tile(t, jnp.where((u >= lo_b) & (u > _KEY_NEG_INF), 1.0, 0.0))
            return carry

        lax.fori_loop(0, n_live, emit, 0)

    @pl.when(any_excess)
    def _():
        lower = (lax.broadcasted_iota(I32, (LANES, LANES), 1)
                 <= lax.broadcasted_iota(I32, (LANES, LANES), 0)).astype(BF)

        def emit(t, seen):
            u = u_ref[t]
            eq = jnp.where(u == lo_b, 1.0, 0.0)
            prefix = seen + _dot(lower, eq.astype(BF))
            take = (u > lo_b) | ((u == lo_b) & (prefix <= need))
            write_tile(t, jnp.where(take & (u > _KEY_NEG_INF), 1.0, 0.0))
            return seen + jnp.sum(eq, axis=0, keepdims=True)

        lax.fori_loop(0, n_live, emit, jnp.zeros((1, qw), F32))

    def clear(t, carry):
        write_tile(t, jnp.zeros((LANES, qw), F32))
        return carry

    lax.fori_loop(n_live, n_tiles, clear, 0)


def _indexer_t_body(qi_ref, wt_ref, tail_ref, mask_ref, score_ref, u_ref, *, seq):
    j = pl.program_id(1)
    n_tiles = seq // LANES
    tiles_per_chunk = IDX_KCHUNK // LANES
    qi = qi_ref[0].astype(BF)
    wt = wt_ref[0] * (IDX_DIM ** -0.5 * IDX_HEADS ** -0.5)

    n_chunks = (j * QW + QW - 1) // IDX_KCHUNK + 1
    q_pos = j * QW + lax.broadcasted_iota(I32, (IDX_KCHUNK, QW), 1)

    def chunk_step(ci, carry):
        k0 = pl.multiple_of(ci * IDX_KCHUNK, IDX_KCHUNK)
        kic = tail_ref[0, pl.ds(k0, IDX_KCHUNK), :][:, 0:IDX_DIM].astype(BF)
        acc = jnp.zeros((IDX_KCHUNK, QW), F32)
        for h in range(IDX_HEADS):
            s = _dot_nt(kic, qi[:, h * IDX_DIM:(h + 1) * IDX_DIM])
            acc = acc + wt[h:h + 1] * jnp.maximum(s, 0.0)
        k_pos = k0 + lax.broadcasted_iota(I32, (IDX_KCHUNK, QW), 0)
        acc = jnp.where(k_pos <= q_pos, acc, -jnp.inf)
        for i in range(tiles_per_chunk):
            score_ref[ci * tiles_per_chunk + i] = acc[i * LANES:(i + 1) * LANES]
        return carry

    lax.fori_loop(0, n_chunks, chunk_step, 0)

    def write_tile(t, m):
        mask_ref[0, 0, t] = m.astype(BF)

    _topk_mask_t(score_ref, u_ref, write_tile, n_tiles, n_chunks, tiles_per_chunk, TOPK)


def indexer_prompt_t(z3, wt, tail3):
    bsz, seq = z3.shape[:2]
    n_tiles = seq // LANES
    assert seq % QW == 0 and seq % IDX_KCHUNK == 0
    return pl.pallas_call(
        functools.partial(_indexer_t_body, seq=seq),
        grid=(bsz, seq // QW),
        in_specs=[pl.BlockSpec((1, QW, IDX_HEADS * IDX_DIM), lambda b, j: (b, j, 5)),
                  pl.BlockSpec((1, IDX_HEADS, QW), lambda b, j: (b, 0, j)),
                  pl.BlockSpec((1, seq, LANES), lambda b, j: (b, 0, 0))],
        out_specs=pl.BlockSpec((1, 1, n_tiles, LANES, QW), lambda b, j: (b, j, 0, 0, 0)),
        out_shape=jax.ShapeDtypeStruct((bsz, seq // QW, n_tiles, LANES, QW), BF),
        scratch_shapes=[pltpu.VMEM((n_tiles, LANES, QW), F32),
                        pltpu.VMEM((n_tiles, LANES, QW), I32)],
        compiler_params=_cparams(("parallel", "arbitrary")),
        name="indexer_prompt_t",
    )(z3, wt, tail3)


def _attn_t_body(bfar_ref, q_ref, k_ref, vt_ref, mask_ref, bias_ref, o_ref, m_ref, l_ref, acc_ref):
    j = pl.program_id(1)
    scale = HEAD_DIM ** -0.5
    qsub = QW // LANES
    q = q_ref[0].astype(BF)
    m_ref[...] = jnp.full(m_ref.shape, NEG_BIG, F32)
    l_ref[...] = jnp.zeros(l_ref.shape, F32)
    acc_ref[...] = jnp.zeros(acc_ref.shape, F32)

    def process(sb, near):
        k0 = pl.multiple_of(sb * ATTN_KSTEP, ATTN_KSTEP)
        kblk = k_ref[0, pl.ds(k0, ATTN_KSTEP), :]
        vblk = vt_ref[0, pl.ds(k0, ATTN_KSTEP), :]
        tiles = [sb * ATTN_TILES + i for i in range(ATTN_TILES)]
        sel = jnp.concatenate([mask_ref[0, 0, t] for t in tiles], axis=0).astype(F32) > 0.0
        if near:
            def tile_bias(t, h):
                row = []
                for s in range(qsub):
                    d = j * qsub + s - t
                    row.append(bias_ref[jnp.where(d == 0, 0, jnp.where(d == 1, 1, 2)), h])
                return jnp.concatenate(row, axis=1)
        head_cols = [slice(h * HEAD_DIM, (h + 1) * HEAD_DIM) for h in range(HEADS)]
        qk = lambda h: _dot_nt(kblk[:, head_cols[h]], q[:, head_cols[h]])
        lg_next = qk(0)
        for h in range(HEADS):
            cols = head_cols[h]
            lg = lg_next * scale
            if h + 1 < HEADS:
                lg_next = qk(h + 1)
            if near:
                lg = lg + jnp.concatenate([tile_bias(t, h) for t in tiles], axis=0)
            else:
                lg = lg + bfar_ref[h]
            lg = jnp.where(sel, lg, NEG_BIG)
            m_old = m_ref[h]
            part = jnp.max(lg.reshape(ATTN_KSTEP // SUB, SUB, QW), axis=0)
            m_new = jnp.maximum(m_old, _row_all(part, jnp.max))
            p = jnp.exp(lg - m_new[0:1])
            alpha = jnp.exp(m_old - m_new)
            psum = jnp.sum(p.reshape(ATTN_KSTEP // SUB, SUB, QW), axis=0)
            l_ref[h] = alpha * l_ref[h] + _row_all(psum, jnp.sum)
            pv = _dot_tn(vblk[:, cols], p.astype(BF))
            acc_ref[cols, :] = alpha[0:1] * acc_ref[cols, :] + pv
            m_ref[h] = m_new

    n_far = jnp.maximum((qsub * j - 1) // ATTN_TILES, 0)
    n_steps = (qsub * j + qsub - 1) // ATTN_TILES + 1

    def far_step(sb, carry):
        process(sb, False)
        return carry

    def near_step(sb, carry):
        process(sb, True)
        return carry

    lax.fori_loop(0, n_far, far_step, 0)
    lax.fori_loop(n_far, n_steps, near_step, 0)

    for h in range(HEADS):
        cols = slice(h * HEAD_DIM, (h + 1) * HEAD_DIM)
        o_ref[0, cols, :] = (acc_ref[cols, :] / l_ref[h][0:1]).astype(BF)


def attn_prompt_t(z3, kv_bf, mask_t, bias_tiles_t, bias_far):
    bsz, seq = z3.shape[:2]
    n_tiles = seq // LANES
    width = HEADS * HEAD_DIM
    assert seq % ATTN_KSTEP == 0 and seq % QW == 0
    return pl.pallas_call(
        _attn_t_body,
        grid=(bsz, seq // QW),
        in_specs=[pl.BlockSpec(memory_space=pltpu.SMEM),
                  pl.BlockSpec((1, QW, width), lambda b, j: (b, j, 4)),
                  pl.BlockSpec((1, seq, width), lambda b, j: (b, 0, 0)),
                  pl.BlockSpec((1, seq, width), lambda b, j: (b, 0, 1)),
                  pl.BlockSpec((1, 1, n_tiles, LANES, QW), lambda b, j: (b, j, 0, 0, 0)),
                  pl.BlockSpec(bias_tiles_t.shape, lambda b, j: (0, 0, 0, 0))],
        out_specs=pl.BlockSpec((1, width, QW), lambda b, j: (b, 0, j)),
        out_shape=jax.ShapeDtypeStruct((bsz, width, seq), BF),
        scratch_shapes=[pltpu.VMEM((HEADS, SUB, QW), F32),
                        pltpu.VMEM((HEADS, SUB, QW), F32),
                        pltpu.VMEM((width, QW), F32)],
        compiler_params=_cparams(("parallel", "arbitrary")),
        name="attn_prompt_t",
    )(bias_far, z3, kv_bf, kv_bf, mask_t, bias_tiles_t)


IDX_SEQ_GROUP = 2


def _indexer_sample_body(pt_ref, qi_ref, wcol_ref, kinew_ref, *rest, n_pages):
    ki_refs = rest[:IDX_SEQ_GROUP * n_pages]
    out_ref = rest[IDX_SEQ_GROUP * n_pages]
    lane = lax.broadcasted_iota(I32, (1, LANES), 1)
    for g in range(IDX_SEQ_GROUP):
        qi = qi_ref[g].astype(BF)
        w = wcol_ref[g] * (IDX_DIM ** -0.5 * IDX_HEADS ** -0.5)
        for i in range(n_pages):
            s = _dot(qi, ki_refs[g * n_pages + i][0].astype(BF))
            out_ref[i, g] = jnp.sum(w * jnp.maximum(s, 0.0), axis=0, keepdims=True)
        kn = kinew_ref[g].astype(BF).astype(F32)
        sn = jnp.sum(qi.astype(F32) * kn, axis=-1, keepdims=True)
        new = jnp.sum(w * jnp.maximum(sn, 0.0), axis=0, keepdims=True)
        out_ref[n_pages, g] = jnp.where(lane == 0, new, -jnp.inf)


def indexer_sample(page_table, qi3, wcol, ki_new, ki_pool_t):
    bsz, n_pages = page_table.shape
    grp = IDX_SEQ_GROUP
    assert bsz % grp == 0
    ki_spec = lambda g, i: pl.BlockSpec((1, IDX_DIM, PAGE),
                                        lambda b, pt, g=g, i=i: (pt[b * grp + g, i], 0, 0))
    grid_spec = pltpu.PrefetchScalarGridSpec(
        num_scalar_prefetch=1,
        grid=(bsz // grp,),
        in_specs=[pl.BlockSpec((grp, IDX_HEADS, IDX_DIM), lambda b, pt: (b, 0, 0)),
                  pl.BlockSpec((grp, IDX_HEADS, 1), lambda b, pt: (b, 0, 0)),
                  pl.BlockSpec((grp, 1, IDX_DIM), lambda b, pt: (b, 0, 0))]
                 + [ki_spec(g, i) for g in range(grp) for i in range(n_pages)],
        out_specs=pl.BlockSpec((n_pages + 1, grp, 1, LANES), lambda b, pt: (0, b, 0, 0)),
    )
    out = pl.pallas_call(
        functools.partial(_indexer_sample_body, n_pages=n_pages),
        grid_spec=grid_spec,
        out_shape=jax.ShapeDtypeStruct((n_pages + 1, bsz, 1, LANES), F32),
        compiler_params=_cparams(("arbitrary",)),
        name="indexer_sample",
    )(page_table, qi3, wcol, ki_new, *([ki_pool_t] * (grp * n_pages)))
    return out.reshape(n_pages + 1, bsz, LANES)


def _select_body(score_ref, mask_ref, u_ref, *, n_tiles, k):
    def write_tile(t, m):
        mask_ref[t] = m

    _topk_mask(score_ref, u_ref, write_tile, n_tiles, n_tiles, 1, k)


def select_topk(scores, k):
    n_tiles, rows, _ = scores.shape
    return pl.pallas_call(
        functools.partial(_select_body, n_tiles=n_tiles, k=k),
        grid=(1,),
        in_specs=[pl.BlockSpec(scores.shape, lambda i: (0, 0, 0))],
        out_specs=pl.BlockSpec(scores.shape, lambda i: (0, 0, 0)),
        out_shape=jax.ShapeDtypeStruct(scores.shape, F32),
        scratch_shapes=[pltpu.VMEM(scores.shape, I32)],
        compiler_params=_cparams(("arbitrary",)),
        name="select_topk",
    )(scores)


SC_CORES = 2
SC_SUBCORES = 16
SC_GATHER_CHUNK = 32


def sc_gather_rows(tables, idx):
    n_rows = idx.shape[0]
    workers = SC_CORES * SC_SUBCORES
    per_worker = n_rows // workers
    chunk = SC_GATHER_CHUNK
    assert n_rows % workers == 0 and per_worker % chunk == 0 and chunk % 8 == 0
    row_shape = tables[0].shape[1:]
    n_tab = len(tables)
    mesh = plsc.VectorSubcoreMesh(core_axis_name="c", subcore_axis_name="s",
                                  num_cores=SC_CORES, num_subcores=SC_SUBCORES)

    def body(*refs):
        tab_refs = refs[:n_tab]
        idx_hbm = refs[n_tab]
        out_refs = refs[n_tab + 1:2 * n_tab + 1]
        idx_v, rows_v, sem = refs[2 * n_tab + 1:]
        wid = lax.axis_index("s") * SC_CORES + lax.axis_index("c")

        @pl.loop(0, per_worker // chunk)
        def _(ci):
            off = pl.multiple_of(wid * per_worker + ci * chunk, 8)
            pltpu.sync_copy(idx_hbm.at[pl.ds(off, chunk)], idx_v)
            for tab, out in zip(tab_refs, out_refs):
                pltpu.async_copy(tab.at[idx_v], rows_v, sem).wait()
                pltpu.sync_copy(rows_v, out.at[pl.ds(off, chunk)])

    return pl.kernel(
        body,
        out_type=[jax.ShapeDtypeStruct((n_rows, *row_shape), t.dtype) for t in tables],
        mesh=mesh,
        scratch_types=[pltpu.VMEM((chunk,), I32),
                       pltpu.VMEM((chunk, *row_shape), tables[0].dtype),
                       pltpu.SemaphoreType.DMA],
        compiler_params=pltpu.CompilerParams(use_tc_tiling_on_sc=True),
        name="sc_gather_rows",
    )(*tables, idx)


SAMPLE_SEQ_GROUP = 8


def _attn_compact_body(dist_ref, near_ref, q_ref, kn_ref, vn_ref, btab_ref, kc_ref, vc_ref,
                       o_ref, bbuf):
    step = pl.program_id(0)
    scale = HEAD_DIM ** -0.5
    ones = jnp.ones((HEAD_DIM, LANES), BF)
    far_bias = btab_ref[REL_MAX_DIST]
    last = lax.broadcasted_iota(I32, (TOPK, 1, 1), 0) == TOPK - 1
    for g in range(SAMPLE_SEQ_GROUP):
        b = step * SAMPLE_SEQ_GROUP + g
        bbuf[g] = jnp.broadcast_to(far_bias[None], (TOPK, HEADS, LANES))

        def fill(t, carry, g=g, b=b):
            bbuf[g, t] = btab_ref[dist_ref[b, t]]
            return carry

        lax.fori_loop(near_ref[b], TOPK, fill, 0)

    for g in range(SAMPLE_SEQ_GROUP):
        b = step * SAMPLE_SEQ_GROUP + g
        is_new = dist_ref[b, TOPK - 1] == 0
        rows = slice(g * TOPK, (g + 1) * TOPK)
        rnd = lambda a: a.astype(BF).astype(F32)
        kc = rnd(jnp.where(last & is_new, kn_ref[g][None], kc_ref[rows]))
        vc = rnd(jnp.where(last & is_new, vn_ref[g][None], vc_ref[rows]))
        prod = (kc * rnd(q_ref[g])[None]).reshape(TOPK * HEADS, HEAD_DIM)
        hi = prod.astype(BF)
        lo = (prod - hi.astype(F32)).astype(BF)
        lg = (_dot(hi, ones) + _dot(lo, ones)).reshape(TOPK, HEADS, LANES) * scale + bbuf[g]
        m = jnp.max(lg, axis=0, keepdims=True)
        p = jnp.exp(lg - m)
        l = jnp.sum(p, axis=0)
        o_ref[g] = (jnp.sum(rnd(p) * vc, axis=0) / l).astype(BF)


def attn_sample_compact(dist, first_near, zs3, k_sel, v_sel, bias_by_dist):
    bsz = dist.shape[0]
    g = SAMPLE_SEQ_GROUP
    assert bsz % g == 0
    zspec = lambda cb: pl.BlockSpec((g, HEADS, HEAD_DIM), lambda i, d, n, cb=cb: (i, cb, 0))
    sel_spec = pl.BlockSpec((g * TOPK, HEADS, HEAD_DIM), lambda i, d, n: (i, 0, 0))
    grid_spec = pltpu.PrefetchScalarGridSpec(
        num_scalar_prefetch=2,
        grid=(bsz // g,),
        in_specs=[zspec(4), zspec(5), zspec(6),
                  pl.BlockSpec(bias_by_dist.shape, lambda i, d, n: (0, 0, 0)),
                  sel_spec, sel_spec],
        out_specs=pl.BlockSpec((g, HEADS, HEAD_DIM), lambda i, d, n: (i, 0, 0)),
        scratch_shapes=[pltpu.VMEM((g, TOPK, HEADS, HEAD_DIM), F32)],
    )
    return pl.pallas_call(
        _attn_compact_body,
        grid_spec=grid_spec,
        out_shape=jax.ShapeDtypeStruct((bsz, HEADS, HEAD_DIM), BF),
        compiler_params=_cparams(("arbitrary",)),
        name="attn_sample_compact",
    )(dist, first_near, zs3, zs3, zs3, bias_by_dist, k_sel, v_sel)


def _bucket_table(max_dist):
    exact = REL_BUCKETS // 2
    d = np.arange(max_dist + 1)
    df = np.maximum(d, 1).astype(np.float32)
    far = exact + (np.log(df / exact) / np.float32(math.log(REL_MAX_DIST / exact))
                   * (REL_BUCKETS - exact)).astype(np.int32)
    return np.where(d < exact, d, np.minimum(far, REL_BUCKETS - 1)).astype(np.int32)


def _bias_tables(rel_bias):
    tab = _bucket_table(2 * LANES)
    assert np.all(tab[REL_MAX_DIST:] == REL_BUCKETS - 1)
    i = np.arange(LANES)
    dist0 = np.maximum(i[:, None] - i[None, :], 0)
    dist1 = LANES + i[:, None] - i[None, :]
    far = np.full((LANES, LANES), REL_BUCKETS - 1)
    idx = np.stack([tab[dist0], tab[dist1], far])
    buckets = jnp.arange(REL_BUCKETS)
    lookup = lambda ix, spec: jnp.einsum(spec, (jnp.asarray(ix)[..., None] == buckets).astype(F32),
                                         rel_bias.astype(F32), precision=lax.Precision.HIGHEST)
    tiles = lookup(idx, "tijb,bh->thij")
    by_dist = jnp.broadcast_to(lookup(tab[:REL_MAX_DIST + 1], "db,bh->dh")[:, :, None],
                               (REL_MAX_DIST + 1, HEADS, LANES))
    return tiles.astype(F32), rel_bias[REL_BUCKETS - 1].astype(F32), by_dist.astype(F32)


def _conv_prompt_body(bg_ref, cg_ref, xt_ref, w_ref, v_ref, st_ref, carry_ref, *, tblock):
    t = pl.program_id(1)

    @pl.when(t == 0)
    def _():
        carry_ref[...] = jnp.zeros_like(carry_ref)

    u = cg_ref[0] * xt_ref[0]
    row = lax.broadcasted_iota(I32, u.shape, 0)
    c0 = carry_ref[0:1]
    c1 = carry_ref[1:2]
    u1 = jnp.where(row == 0, c1, pltpu.roll(u, 1, axis=0))
    u2 = jnp.where(row == 0, c0, jnp.where(row == 1, c1, pltpu.roll(u, 2, axis=0)))
    conv = w_ref[0:1] * u2 + w_ref[1:2] * u1 + w_ref[2:3] * u
    v_ref[0] = (bg_ref[0] * conv).astype(BF)
    last = u[tblock - 2:tblock]
    carry_ref[0:2] = last

    @pl.when(t == pl.num_programs(1) - 1)
    def _():
        st_ref[0] = last


def conv_prompt(zc3, w_conv, tblock=256):
    bsz, seq = zc3.shape[:2]
    c = zc3.shape[2] // 3
    zspec = lambda cb: pl.BlockSpec((1, tblock, c), lambda b, t, cb=cb: (b, t, cb))
    return pl.pallas_call(
        functools.partial(_conv_prompt_body, tblock=tblock),
        grid=(bsz, seq // tblock),
        in_specs=[zspec(0), zspec(1), zspec(2), pl.BlockSpec((3, c), lambda b, t: (0, 0))],
        out_specs=[pl.BlockSpec((1, tblock, c), lambda b, t: (b, t, 0)),
                   pl.BlockSpec((1, 2, c), lambda b, t: (b, 0, 0))],
        out_shape=[jax.ShapeDtypeStruct((bsz, seq, c), BF),
                   jax.ShapeDtypeStruct((bsz, 2, c), F32)],
        scratch_shapes=[pltpu.VMEM((8, c), F32)],
        compiler_params=_cparams(("parallel", "arbitrary")),
        name="conv_prompt",
    )(zc3, zc3, zc3, w_conv)


def _conv_sample_body(bg_ref, cg_ref, xt_ref, w_ref, s0_ref, s1_ref, v_ref, n0_ref, n1_ref):
    u = cg_ref[...] * xt_ref[...]
    conv = w_ref[0:1] * s0_ref[...] + w_ref[1:2] * s1_ref[...] + w_ref[2:3] * u
    v_ref[...] = (bg_ref[...] * conv).astype(BF)
    n0_ref[...] = s1_ref[...]
    n1_ref[...] = u


def conv_sample(zc, w_conv, s0, s1):
    bsz = zc.shape[0]
    c = zc.shape[1] // 3
    zspec = lambda cb: pl.BlockSpec((bsz, c), lambda i, cb=cb: (0, cb))
    full = pl.BlockSpec((bsz, c), lambda i: (0, 0))
    return pl.pallas_call(
        _conv_sample_body,
        grid=(1,),
        in_specs=[zspec(0), zspec(1), zspec(2), pl.BlockSpec((3, c), lambda i: (0, 0)), full, full],
        out_specs=[full, full, full],
        out_shape=[jax.ShapeDtypeStruct((bsz, c), BF)] + [jax.ShapeDtypeStruct((bsz, c), F32)] * 2,
        compiler_params=_cparams(("arbitrary",)),
        name="conv_sample",
    )(zc, zc, zc, w_conv, s0, s1)


def _router_body(x_ref, g_ref, wr_ref, br_ref, *rest):
    h_ref, route_ref = rest[-2:]
    x = x_ref[...]
    ms = jnp.mean(x * x, axis=-1, keepdims=True)
    h = x * lax.rsqrt(ms + RMS_EPS) * g_ref[...]
    hb = h.astype(BF)
    half = h.shape[1] // 2
    bits = pltpu.bitcast(hb.astype(F32), jnp.uint32)
    h_ref[...] = (bits[:, :half] & jnp.uint32(0xFFFF0000)) | (bits[:, half:] >> 16)
    logits = _dot(hb, wr_ref[...]) + br_ref[...]
    lane = lax.broadcasted_iota(I32, logits.shape, 1)
    big = np.int32(1 << 20)
    lg = jnp.where(lane < N_GROUPS, logits, -jnp.inf)
    g_max = jnp.max(lg, axis=-1, keepdims=True)
    g_idx = jnp.min(jnp.where(lg == g_max, lane, big), axis=-1, keepdims=True)
    g_w = 1.0 / jnp.sum(jnp.exp(lg - g_max), axis=-1, keepdims=True)
    first = N_GROUPS + EXP_PER_GROUP * g_idx
    le = jnp.where((lane >= first) & (lane < first + EXP_PER_GROUP), logits, -jnp.inf)
    l1 = jnp.max(le, axis=-1, keepdims=True)
    i1 = jnp.min(jnp.where(le == l1, lane, big), axis=-1, keepdims=True)
    le2 = jnp.where(lane == i1, -jnp.inf, le)
    l2 = jnp.max(le2, axis=-1, keepdims=True)
    i2 = jnp.min(jnp.where(le2 == l2, lane, big), axis=-1, keepdims=True)
    r = jnp.exp(l2 - l1)
    w1 = g_w / (1.0 + r)
    w2 = g_w * r / (1.0 + r)
    e1 = (i1 - N_GROUPS).astype(F32)
    e2 = (i2 - N_GROUPS).astype(F32)
    route_ref[...] = jnp.where(lane == 0, e1, jnp.where(lane == 1, e2,
                               jnp.where(lane == 2, w1, jnp.where(lane == 3, w2, 0.0))))


def moe_router(x, g, wr, br, tm, total_rows, row0, bufs=None):
    m, k = x.shape
    assert row0 % tm == 0
    off = row0 // tm
    in_specs = [pl.BlockSpec((tm, k), lambda i: (i, 0)),
                pl.BlockSpec((1, k), lambda i: (0, 0)),
                pl.BlockSpec((k, LANES), lambda i: (0, 0)),
                pl.BlockSpec((1, LANES), lambda i: (0, 0))]
    args = [x, g.reshape(1, k), wr, br]
    aliases = {}
    if bufs is not None:
        in_specs += [pl.BlockSpec(memory_space=pl.ANY)] * 2
        args += list(bufs)
        aliases = {4: 0, 5: 1}
    return pl.pallas_call(
        _router_body,
        grid=(m // tm,),
        in_specs=in_specs,
        out_specs=[pl.BlockSpec((tm, k // 2), lambda i: (i + off, 0)),
                   pl.BlockSpec((tm, LANES), lambda i: (i + off, 0))],
        out_shape=[jax.ShapeDtypeStruct((total_rows, k // 2), jnp.uint32),
                   jax.ShapeDtypeStruct((total_rows, LANES), F32)],
        input_output_aliases=aliases,
        compiler_params=_cparams(("parallel",)),
        name="moe_router",
    )(*args)


def _experts_body(te_ref, nu_ref, h_ref, wg_ref, wu_ref, wd_ref, o_ref, wg_bf, wu_bf, wd_bf):
    i = pl.program_id(0)

    @pl.when((i == 0) | (te_ref[i] != te_ref[jnp.maximum(i - 1, 0)]))
    def _():
        wg_bf[...] = wg_ref[0].astype(BF)
        wu_bf[...] = wu_ref[0].astype(BF)
        wd_bf[...] = wd_ref[0].astype(BF)

    @pl.when(i < nu_ref[0])
    def _():
        words = h_ref[...]
        left = pltpu.bitcast(words & jnp.uint32(0xFFFF0000), F32)
        right = pltpu.bitcast(words << 16, F32)
        h = jnp.concatenate([left, right], axis=1).astype(BF)
        a = _dot(h, wg_bf[...])
        b = _dot(h, wu_bf[...])
        hid = a * _sigmoid(a) * b
        o_ref[...] = _dot(hid.astype(BF), wd_bf[...])

    @pl.when(i >= nu_ref[0])
    def _():
        o_ref[...] = jnp.zeros_like(o_ref)


def moe_experts(tile_expert, n_used, hs, wg, wu, wd):
    p = hs.shape[0]
    k = wg.shape[1]
    f = wg.shape[2]
    n_tiles = p // MOE_TILE
    grid_spec = pltpu.PrefetchScalarGridSpec(
        num_scalar_prefetch=2,
        grid=(n_tiles,),
        in_specs=[pl.BlockSpec((MOE_TILE, k // 2), lambda i, te, nu: (i, 0)),
                  pl.BlockSpec((1, k, f), lambda i, te, nu: (te[i], 0, 0)),
                  pl.BlockSpec((1, k, f), lambda i, te, nu: (te[i], 0, 0)),
                  pl.BlockSpec((1, f, k), lambda i, te, nu: (te[i], 0, 0))],
        out_specs=pl.BlockSpec((MOE_TILE, k), lambda i, te, nu: (i, 0)),
        scratch_shapes=[pltpu.VMEM((k, f), BF), pltpu.VMEM((k, f), BF), pltpu.VMEM((f, k), BF)],
    )
    return pl.pallas_call(
        _experts_body,
        grid_spec=grid_spec,
        out_shape=jax.ShapeDtypeStruct((p, k), F32),
        compiler_params=_cparams(("arbitrary",)),
        name="moe_experts",
    )(tile_expert, n_used, hs, wg, wu, wd)


def _combine_body(x_ref, a_ref, b_ref, route_ref, g_ref, o_ref, *, normalize):
    y = x_ref[...] + route_ref[:, 2:3] * a_ref[...] + route_ref[:, 3:4] * b_ref[...]
    if normalize:
        ms = jnp.mean(y * y, axis=-1, keepdims=True)
        y = y * lax.rsqrt(ms + RMS_EPS) * g_ref[...]
    o_ref[...] = y


def moe_combine(x, out_a, out_b, route, tm, norm_g=None):
    m, k = x.shape
    row = pl.BlockSpec((tm, k), lambda i: (i, 0))
    g = jnp.ones((1, k), F32) if norm_g is None else norm_g.reshape(1, k)
    return pl.pallas_call(
        functools.partial(_combine_body, normalize=norm_g is not None),
        grid=(m // tm,),
        in_specs=[row, row, row, pl.BlockSpec((tm, LANES), lambda i: (i, 0)),
                  pl.BlockSpec((1, k), lambda i: (0, 0))],
        out_specs=row,
        out_shape=jax.ShapeDtypeStruct((m, k), F32),
        compiler_params=_cparams(("parallel",)),
        name="moe_combine",
    )(x, out_a, out_b, route, g)


def _rank_within_expert(onehot):
    n, e = onehot.shape
    blk = LANES
    assert n % blk == 0
    oh = onehot.astype(F32).reshape(n // blk, blk, e)
    strict = jnp.asarray(np.tril(np.ones((blk, blk), np.float32), -1))
    within = jnp.einsum("ij,bjk->bik", strict, oh, precision=lax.Precision.HIGHEST)
    totals = jnp.sum(oh, axis=1)
    before = jnp.cumsum(totals, axis=0) - totals
    rank = (within + before[:, None, :]).reshape(n, e)
    return jnp.sum(rank * onehot.astype(F32), axis=1).astype(I32), jnp.sum(totals, axis=0).astype(I32)


def hier_moe(xs, tms, g, wrg, brg, wre, bre, wg, wu, wd, layer, out_norm_g=None):
    d = xs[0].shape[1]
    m = sum(x.shape[0] for x in xs)
    wr = jnp.zeros((d, LANES), F32)
    wr = wr.at[:, :N_GROUPS].set(wrg).at[:, N_GROUPS:N_GROUPS + N_EXPERTS].set(wre.reshape(d, N_EXPERTS))
    br = jnp.zeros((1, LANES), F32)
    br = br.at[0, :N_GROUPS].set(brg).at[0, N_GROUPS:N_GROUPS + N_EXPERTS].set(bre.reshape(N_EXPERTS))
    bufs, row0 = None, 0
    for x, tm in zip(xs, tms):
        bufs = moe_router(x, g, wr.astype(BF), br, tm, m, row0, bufs)
        row0 += x.shape[0]
    h_bf, route = bufs

    eid = route[:, 0:2].astype(I32).reshape(-1)
    onehot = eid[:, None] == jnp.arange(N_EXPERTS, dtype=I32)[None, :]
    rank, counts = _rank_within_expert(onehot)
    padded = ((counts + MOE_TILE - 1) // MOE_TILE) * MOE_TILE
    ends = jnp.cumsum(padded)
    pos = jnp.sum(jnp.where(onehot, (ends - padded)[None, :], 0), axis=1) + rank
    n_rows = 2 * m + N_EXPERTS * MOE_TILE
    n_rows = -(-n_rows // MOE_TILE) * MOE_TILE
    token = jnp.zeros((n_rows,), I32).at[pos].set(jnp.arange(2 * m, dtype=I32) // 2)
    tile_start = jnp.arange(n_rows // MOE_TILE, dtype=I32) * MOE_TILE
    tile_expert = jnp.minimum(jnp.sum(tile_start[:, None] >= ends[None, :], axis=1),
                              N_EXPERTS - 1).astype(I32)
    n_used = (ends[-1] // MOE_TILE).astype(I32).reshape(1)

    hs = jnp.take(h_bf, token, axis=0, mode="clip")
    out = moe_experts(tile_expert + layer * N_EXPERTS, n_used, hs, wg, wu, wd)
    pos2 = pos.reshape(m, 2)
    res, r0 = [], 0
    for x, tm in zip(xs, tms):
        rows = slice(r0, r0 + x.shape[0])
        res.append(moe_combine(x, jnp.take(out, pos2[rows, 0], axis=0, mode="clip"),
                               jnp.take(out, pos2[rows, 1], axis=0, mode="clip"),
                               route[rows], tm, out_norm_g))
        r0 += x.shape[0]
    return res


def kernel(x_prompt, x_sample, cache_k, cache_v, cache_ki, state_hgrn, state_conv, page_table,
           norm_mix_g, norm_ffn_g, final_g, w_in_even, w_out_even, hgrn_lb_logits, hgrn_norm_g,
           rel_bias, w_in_conv, w_conv, w_out_conv, w_router_g, b_router_g, w_router_e,
           b_router_e, w_gate, w_up, w_down):
    bsz, seq, d = x_prompt.shape
    dec = x_sample.shape[0]
    n_p = bsz * seq
    tm_p, tm_s = 512, 128
    tms = (tm_p, tm_s)
    assert n_p % tm_p == 0 and dec % tm_s == 0 and x_sample.shape[1] == 1
    width = HEADS * HEAD_DIM
    even_in = w_in_even.shape[2]
    even_pad = -(-even_in // IN_PROJ_TN) * IN_PROJ_TN
    tail0 = TAIL_BLOCK * LANES
    n_pool = cache_k.shape[1]

    xp = x_prompt.reshape(n_p, d)
    xs = x_sample.reshape(dec, d)

    lbs = jnp.cumsum(jax.nn.softmax(hgrn_lb_logits.astype(F32), axis=0), axis=0)[:-1]
    bias_tiles, bias_far, bias_by_dist = _bias_tables(rel_bias)
    expert_w = [w.reshape(-1, *w.shape[2:]) for w in (w_gate, w_up, w_down)]

    w_in = jnp.pad(w_in_even[0], ((0, 0), (0, even_pad - even_in))).astype(BF)
    zp, k_new_p, v_new_p, kv_bf = even_projection(xp, norm_mix_g[0], w_in, 2 * tm_p)
    zt = norm_matmul(xp, norm_mix_g[0], w_in[:, tail0:tail0 + LANES], 2 * tm_p, LANES)
    zs = norm_matmul(xs, norm_mix_g[0], w_in, tm_s, IN_PROJ_TN)
    zp3 = zp.reshape(bsz, seq, 6 * width)
    zt3 = zt.reshape(bsz, seq, LANES)
    zs3 = zs.reshape(dec, even_pad // LANES, LANES)

    oa_p, hgrn_p = hgrn_prompt(zp3, lbs[0], hgrn_norm_g[0])
    oa_s, hgrn_s = hgrn_sample(zs3, lbs[0], hgrn_norm_g[0], state_hgrn[0])

    wt = jnp.swapaxes(zt3[:, :, IDX_DIM:IDX_DIM + IDX_HEADS], 1, 2)
    mask_p = indexer_prompt_t(zp3, wt, zt3)
    ob_p = jnp.swapaxes(attn_prompt_t(zp3, kv_bf.reshape(bsz, seq, 2 * width), mask_p,
                                      jnp.swapaxes(bias_tiles, -1, -2), bias_far), 1, 2)

    qi3 = zs[:, 7 * width:8 * width].reshape(dec, IDX_HEADS, IDX_DIM)
    ki_new = zs[:, tail0:tail0 + IDX_DIM].reshape(dec, 1, IDX_DIM)
    wcol = zs[:, tail0 + IDX_DIM:tail0 + IDX_DIM + IDX_HEADS].reshape(dec, IDX_HEADS, 1)
    pages = page_table + 0 * n_pool
    scores_s = indexer_sample(pages, qi3, wcol, ki_new,
                              jnp.swapaxes(cache_ki, -1, -2).reshape(-1, IDX_DIM, PAGE))
    mask_s = select_topk(scores_s, TOPK)
    sel = jnp.transpose(mask_s, (1, 0, 2))
    n_pages = page_table.shape[1]
    sel_off = jnp.concatenate([jnp.zeros((dec, 1), F32),
                               jnp.cumsum(jnp.sum(sel, axis=2), axis=1)], axis=1)
    slot = jnp.arange(TOPK, dtype=F32)
    page_of = jnp.sum(sel_off[:, None, 1:] <= slot[None, :, None], axis=2)
    page_1h = (page_of[:, :, None] == jnp.arange(n_pages + 1)[None, None, :]).astype(F32)
    local = slot[None, :] - jnp.einsum("brp,bp->br", page_1h, sel_off[:, :-1],
                                       precision=lax.Precision.HIGHEST)
    within = jnp.einsum("brp,bpl->brl", page_1h, jnp.cumsum(sel, axis=2),
                        precision=lax.Precision.HIGHEST)
    lane_of = jnp.sum(within <= local[:, :, None], axis=2)
    sel_idx = (page_of * PAGE + lane_of).astype(I32)
    past = n_pages * PAGE
    page_id = jnp.einsum("brp,bp->br", page_1h[:, :, :n_pages], pages.astype(F32),
                         precision=lax.Precision.HIGHEST).astype(I32)
    rows = jnp.where(page_of < n_pages, page_id * PAGE + lane_of, 0).astype(I32).reshape(-1)
    rows, zp3 = lax.optimization_barrier((rows, zp3))
    k_sel, v_sel = sc_gather_rows([cache_k.reshape(-1, HEADS, HEAD_DIM),
                                   cache_v.reshape(-1, HEADS, HEAD_DIM)], rows)
    dist = jnp.minimum(past - sel_idx, REL_MAX_DIST).astype(I32)
    first_near = jnp.sum(dist >= REL_MAX_DIST, axis=1).astype(I32)
    ob_s = attn_sample_compact(dist, first_near, zs3, k_sel, v_sel, bias_by_dist)

    w_out = w_out_even[0].astype(BF)
    w_out_ab = [w_out[:width], w_out[width:]]
    xp = matmul_residual([oa_p.reshape(n_p, width), ob_p.reshape(n_p, width)], w_out_ab, xp, 2 * tm_p, 1024)
    xs = matmul_residual([oa_s.reshape(dec, width), ob_s.reshape(dec, width)], w_out_ab, xs, tm_s, 512)

    xp, xs = hier_moe([xp, xs], tms, norm_ffn_g[0], w_router_g[0], b_router_g[0], w_router_e[0],
                      b_router_e[0], *expert_w, 0)

    w_in_c = w_in_conv[0].astype(BF)
    zcp = norm_matmul(xp, norm_mix_g[1], w_in_c, 2 * tm_p, 1024)
    zcs = norm_matmul(xs, norm_mix_g[1], w_in_c, tm_s, 512)
    cw = zcp.shape[1] // 3
    v_p, conv_p = conv_prompt(zcp.reshape(bsz, seq, 3 * cw), w_conv[0])
    v_s, cs0, cs1 = conv_sample(zcs, w_conv[0], state_conv[0, :, 0], state_conv[0, :, 1])
    w_out_c = [w_out_conv[0].astype(BF)]
    xp = matmul_residual([v_p.reshape(n_p, cw)], w_out_c, xp, 2 * tm_p, 1024)
    xs = matmul_residual([v_s], w_out_c, xs, tm_s, 512)

    yp, ys = hier_moe([xp, xs], tms, norm_ffn_g[1], w_router_g[1], b_router_g[1], w_router_e[1],
                      b_router_e[1], *expert_w, 1, out_norm_g=final_g)

    kcol, vcol = 5 * width, 6 * width
    heads = lambda a, n: a.reshape(1, *n, HEADS, HEAD_DIM)
    return (yp.reshape(bsz, seq, d),
            ys.reshape(dec, 1, d),
            heads(k_new_p, (bsz, seq)),
            heads(v_new_p, (bsz, seq)),
            zt[:, :IDX_DIM].reshape(1, bsz, seq, IDX_DIM),
            hgrn_p[None],
            conv_p[None],
            heads(zs[:, kcol:kcol + width], (dec, 1)),
            heads(zs[:, vcol:vcol + width], (dec, 1)),
            zs[:, tail0:tail0 + IDX_DIM].reshape(1, dec, 1, IDX_DIM),
            hgrn_s[None],
            jnp.stack([cs0, cs1], axis=1)[None])
```

```python
import functools
import math

import numpy as np
import jax
import jax.numpy as jnp
from jax import lax
from jax.experimental import pallas as pl
from jax.experimental.pallas import tpu as pltpu
from jax.experimental.pallas import tpu_sc as plsc

F32 = jnp.float32
BF = jnp.bfloat16
I32 = jnp.int32

RMS_EPS = 1e-6
LANES = 128
NEG_BIG = -1e30
VMEM_LIMIT = 56 * 1024 * 1024

HEADS = 8
HEAD_DIM = 128
IDX_HEADS = 16
IDX_DIM = 64
TOPK = 256
REL_BUCKETS = 32
REL_MAX_DIST = 128
N_EXPERTS = 16
EXP_PER_GROUP = 4
N_GROUPS = 4
PAGE = 128

HGRN_CHUNK = 128
HGRN_HEAD_GROUP = 8
MOE_TILE = 256
IN_PROJ_TN = 640
TAIL_BLOCK = 8 * HEADS * HEAD_DIM // LANES


def _cparams(sem):
    return pltpu.CompilerParams(dimension_semantics=sem, vmem_limit_bytes=VMEM_LIMIT)


def _dot(a, b):
    return jnp.dot(a, b, preferred_element_type=F32)


def _dot_nt(a, b):
    return lax.dot_general(a, b, (((1,), (1,)), ((), ())), preferred_element_type=F32)


def _dot_tn(a, b):
    return lax.dot_general(a, b, (((0,), (0,)), ((), ())), preferred_element_type=F32)


def _sigmoid(x):
    return 1.0 / (1.0 + jnp.exp(-x))


def _norm_mm_body(x_ref, g_ref, w_ref, o_ref, h_ref):
    @pl.when(pl.program_id(1) == 0)
    def _():
        x = x_ref[...]
        ms = jnp.mean(x * x, axis=-1, keepdims=True)
        h_ref[...] = (x * lax.rsqrt(ms + RMS_EPS) * g_ref[...]).astype(BF)

    o_ref[...] = _dot(h_ref[...], w_ref[...])


def norm_matmul(x, g, w_bf, tm, tn):
    m, k = x.shape
    n = w_bf.shape[1]
    return pl.pallas_call(
        _norm_mm_body,
        grid=(m // tm, n // tn),
        in_specs=[pl.BlockSpec((tm, k), lambda i, j: (i, 0)),
                  pl.BlockSpec((1, k), lambda i, j: (0, 0)),
                  pl.BlockSpec((k, tn), lambda i, j: (0, j))],
        out_specs=pl.BlockSpec((tm, tn), lambda i, j: (i, j)),
        out_shape=jax.ShapeDtypeStruct((m, n), F32),
        scratch_shapes=[pltpu.VMEM((tm, k), BF)],
        compiler_params=_cparams(("parallel", "arbitrary")),
        name="norm_matmul",
    )(x, g.reshape(1, k), w_bf)


EVEN_TN = 1024


def _even_proj_body(x_ref, g_ref, w_ref, z_ref, k_ref, v_ref, kv_ref, h_ref, *, per):
    j = pl.program_id(1)

    @pl.when(j == 0)
    def _():
        x = x_ref[...]
        ms = jnp.mean(x * x, axis=-1, keepdims=True)
        h_ref[...] = (x * lax.rsqrt(ms + RMS_EPS) * g_ref[...]).astype(BF)

    res = _dot(h_ref[...], w_ref[...])
    sec = j // per

    @pl.when((sec < 5) | (sec == 7))
    def _():
        z_ref[...] = res

    @pl.when(sec == 5)
    def _():
        k_ref[...] = res
        kv_ref[...] = res.astype(BF)

    @pl.when(sec == 6)
    def _():
        v_ref[...] = res
        kv_ref[...] = res.astype(BF)


def even_projection(x, g, w_bf, tm):
    m, k = x.shape
    width = HEADS * HEAD_DIM
    per = width // EVEN_TN
    clip = lambda a, lo, hi: jnp.minimum(jnp.maximum(a, lo), hi)
    return pl.pallas_call(
        functools.partial(_even_proj_body, per=per),
        grid=(m // tm, 8 * per),
        in_specs=[pl.BlockSpec((tm, k), lambda i, j: (i, 0)),
                  pl.BlockSpec((1, k), lambda i, j: (0, 0)),
                  pl.BlockSpec((k, EVEN_TN), lambda i, j: (0, j))],
        out_specs=[pl.BlockSpec((tm, EVEN_TN), lambda i, j: (i, j - clip(j - (5 * per - 1), 0, 2 * per))),
                   pl.BlockSpec((tm, EVEN_TN), lambda i, j: (i, clip(j - 5 * per, 0, per - 1)),
                                pipeline_mode=pl.Buffered(1)),
                   pl.BlockSpec((tm, EVEN_TN), lambda i, j: (i, clip(j - 6 * per, 0, per - 1)),
                                pipeline_mode=pl.Buffered(1)),
                   pl.BlockSpec((tm, EVEN_TN), lambda i, j: (i, clip(j - 5 * per, 0, 2 * per - 1)),
                                pipeline_mode=pl.Buffered(1))],
        out_shape=[jax.ShapeDtypeStruct((m, 6 * width), F32),
                   jax.ShapeDtypeStruct((m, width), F32),
                   jax.ShapeDtypeStruct((m, width), F32),
                   jax.ShapeDtypeStruct((m, 2 * width), BF)],
        scratch_shapes=[pltpu.VMEM((tm, k), BF)],
        compiler_params=_cparams(("parallel", "arbitrary")),
        name="even_projection",
    )(x, g.reshape(1, k), w_bf)


def _mm_res_body(*refs, n_lhs):
    a_refs = refs[:n_lhs]
    w_refs = refs[n_lhs:2 * n_lhs]
    r_ref = refs[2 * n_lhs]
    o_ref = refs[2 * n_lhs + 1]
    acc = r_ref[...]
    for a_ref, w_ref in zip(a_refs, w_refs):
        acc = acc + _dot(a_ref[...], w_ref[...])
    o_ref[...] = acc


def matmul_residual(lhs, ws_bf, res, tm, tn):
    m, n = res.shape
    n_lhs = len(lhs)
    assert all(a.dtype == BF for a in lhs)
    in_specs = ([pl.BlockSpec((tm, a.shape[1]), lambda i, j: (i, 0)) for a in lhs]
                + [pl.BlockSpec((w.shape[0], tn), lambda i, j: (0, j)) for w in ws_bf]
                + [pl.BlockSpec((tm, tn), lambda i, j: (i, j))])
    return pl.pallas_call(
        functools.partial(_mm_res_body, n_lhs=n_lhs),
        grid=(m // tm, n // tn),
        in_specs=in_specs,
        out_specs=pl.BlockSpec((tm, tn), lambda i, j: (i, j)),
        out_shape=jax.ShapeDtypeStruct((m, n), F32),
        compiler_params=_cparams(("parallel", "arbitrary")),
        name="matmul_residual",
    )(*lhs, *ws_bf, res)


def _hgrn_static(c):
    levels = []
    m = 1
    while m < c:
        levels.append(m)
        m *= 2
    t = np.arange(c)
    rows = [t[None, :] <= t[:, None]]
    masks = [np.eye(c, dtype=bool)]
    for m in levels:
        blk = t // (2 * m)
        pos = t % (2 * m)
        bnd = blk * 2 * m + m - 1
        right = pos >= m
        left = pos < m
        e_rows = (t[None, :] > bnd[:, None]) & (t[None, :] <= t[:, None]) & right[:, None]
        f_rows = (t[None, :] > t[:, None]) & (t[None, :] <= bnd[:, None]) & left[:, None]
        rows.append(e_rows | f_rows)
        masks.append((blk[:, None] == blk[None, :]) & right[:, None] & left[None, :])
    m_all = np.stack(rows).astype(np.float32)
    masks = np.stack(masks).astype(np.float32)
    return m_all, masks, len(levels)


def _hgrn_gates(qa, fa, lb):
    f = lb + (1.0 - lb) * _sigmoid(fa)
    q = qa * _sigmoid(qa)
    return q, f


def _hgrn_prompt_body(qa_ref, fa_ref, ia_ref, ga_ref, lb_ref, gn_ref, mall_ref, masks_ref,
                      oa_ref, st_out_ref, st_ref, *, chunk, tblock, n_levels):
    c = chunk
    t_idx = pl.program_id(1)

    @pl.when(t_idx == 0)
    def _():
        st_ref[...] = jnp.zeros_like(st_ref)

    gn = gn_ref[...]

    def chunk_step(ci, carry):
        r0 = pl.multiple_of(ci * c, c)
        for h0 in range(0, HEADS, HGRN_HEAD_GROUP):
            hs = range(h0, h0 + HGRN_HEAD_GROUP)
            cols = {h: slice(h * HEAD_DIM, (h + 1) * HEAD_DIM) for h in hs}
            q, k, g_hi, g_lo, v = {}, {}, {}, {}, {}
            for h in hs:
                q[h], f = _hgrn_gates(qa_ref[0, pl.ds(r0, c), cols[h]],
                                      fa_ref[0, pl.ds(r0, c), cols[h]], lb_ref[:, cols[h]])
                k[h] = 1.0 - f
                g = jnp.log(f)
                g_hi[h] = g.astype(BF)
                g_lo[h] = (g - g_hi[h].astype(F32)).astype(BF)
                v[h] = ia_ref[0, pl.ds(r0, c), cols[h]].astype(BF)
            b = {h: _dot(mall_ref[0], g_hi[h]) + _dot(mall_ref[0], g_lo[h]) for h in hs}
            o = {h: _dot_nt((q[h] * jnp.exp(b[h])).astype(BF), st_ref[h].astype(BF)) for h in hs}
            a = {h: masks_ref[0] * _dot_nt(q[h].astype(BF), k[h].astype(BF)) for h in hs}
            for li in range(n_levels):
                ml = mall_ref[1 + li]
                w = {h: jnp.exp(_dot(ml, g_hi[h]) + _dot(ml, g_lo[h])) for h in hs}
                for h in hs:
                    a[h] = a[h] + masks_ref[1 + li] * _dot_nt((q[h] * w[h]).astype(BF),
                                                              (k[h] * w[h]).astype(BF))
            for h in hs:
                o[h] = o[h] + _dot(a[h].astype(BF), v[h])
            for h in hs:
                b_last = b[h][c - 1:c]
                k_st = (k[h] * jnp.exp(b_last - b[h])).astype(BF)
                st_ref[h] = st_ref[h] * jnp.exp(b_last) + _dot_tn(v[h], k_st)
            for h in hs:
                ga = ga_ref[0, pl.ds(r0, c), cols[h]]
                ms = jnp.mean(o[h] * o[h], axis=-1, keepdims=True)
                oa_ref[0, pl.ds(r0, c), cols[h]] = (o[h] * lax.rsqrt(ms + RMS_EPS) * gn
                                                    * (ga * _sigmoid(ga))).astype(BF)
        return carry

    lax.fori_loop(0, tblock // c, chunk_step, 0)

    @pl.when(t_idx == pl.num_programs(1) - 1)
    def _():
        st_out_ref[0] = st_ref[...]


def hgrn_prompt(z3, lb, gn, tblock=256, chunk=HGRN_CHUNK):
    bsz, seq = z3.shape[:2]
    width = HEADS * HEAD_DIM
    m_all, masks, n_levels = _hgrn_static(chunk)
    zspec = lambda cb: pl.BlockSpec((1, tblock, width), lambda b, t, cb=cb: (b, t, cb))
    oa, st = pl.pallas_call(
        functools.partial(_hgrn_prompt_body, chunk=chunk, tblock=tblock, n_levels=n_levels),
        grid=(bsz, seq // tblock),
        in_specs=[zspec(0), zspec(1), zspec(2), zspec(3),
                  pl.BlockSpec((1, width), lambda b, t: (0, 0)),
                  pl.BlockSpec((1, HEAD_DIM), lambda b, t: (0, 0)),
                  pl.BlockSpec(m_all.shape, lambda b, t: (0, 0, 0)),
                  pl.BlockSpec(masks.shape, lambda b, t: (0, 0, 0))],
        out_specs=[pl.BlockSpec((1, tblock, width), lambda b, t: (b, t, 0)),
                   pl.BlockSpec((1, HEADS, HEAD_DIM, HEAD_DIM), lambda b, t: (b, 0, 0, 0))],
        out_shape=[jax.ShapeDtypeStruct((bsz, seq, width), BF),
                   jax.ShapeDtypeStruct((bsz, HEADS, HEAD_DIM, HEAD_DIM), F32)],
        scratch_shapes=[pltpu.VMEM((HEADS, HEAD_DIM, HEAD_DIM), F32)],
        compiler_params=_cparams(("parallel", "arbitrary")),
        name="hgrn_prompt",
    )(z3, z3, z3, z3, lb.reshape(1, width), gn.reshape(1, HEAD_DIM),
      jnp.asarray(m_all, BF), jnp.asarray(masks, F32))
    return oa, jnp.swapaxes(st, -1, -2)


def _col(row, eye):
    return jnp.sum(eye * row, axis=1, keepdims=True)


def _hgrn_sample_body(qa_ref, fa_ref, ia_ref, ga_ref, lb_ref, gn_ref, s_ref, oa_ref, so_ref):
    eye = (lax.broadcasted_iota(I32, (HEAD_DIM, HEAD_DIM), 0)
           == lax.broadcasted_iota(I32, (HEAD_DIM, HEAD_DIM), 1)).astype(F32)
    rnd = lambda a: a.astype(BF).astype(F32)
    for g in range(SAMPLE_SEQ_GROUP):
        q8, f8 = _hgrn_gates(qa_ref[g], fa_ref[g], lb_ref[...])
        ga = ga_ref[g]
        gate = ga * _sigmoid(ga)
        kr = rnd(1.0 - f8)
        vr = rnd(ia_ref[g])
        qfr = rnd(q8 * f8)
        qk = rnd(jnp.sum(rnd(q8) * kr, axis=-1, keepdims=True))
        outs = []
        for h in range(HEADS):
            f_col = _col(f8[h:h + 1], eye)
            s_old = s_ref[g, h]
            so_ref[g, h] = f_col * s_old + _col(kr[h:h + 1], eye) * vr[h:h + 1]
            outs.append(qk[h:h + 1] * vr[h:h + 1]
                        + jnp.sum(_col(qfr[h:h + 1], eye) * rnd(s_old), axis=0, keepdims=True))
        o = jnp.concatenate(outs, axis=0)
        ms = jnp.mean(o * o, axis=-1, keepdims=True)
        oa_ref[g] = (o * lax.rsqrt(ms + RMS_EPS) * gn_ref[...] * gate).astype(BF)


def hgrn_sample(zs3, lb, gn, s0):
    bsz = zs3.shape[0]
    g = SAMPLE_SEQ_GROUP
    assert bsz % g == 0
    zspec = lambda cb: pl.BlockSpec((g, HEADS, HEAD_DIM), lambda b, cb=cb: (b, cb, 0))
    sspec = pl.BlockSpec((g, HEADS, HEAD_DIM, HEAD_DIM), lambda b: (b, 0, 0, 0))
    return pl.pallas_call(
        _hgrn_sample_body,
        grid=(bsz // g,),
        in_specs=[zspec(0), zspec(1), zspec(2), zspec(3),
                  pl.BlockSpec((HEADS, HEAD_DIM), lambda b: (0, 0)),
                  pl.BlockSpec((1, HEAD_DIM), lambda b: (0, 0)),
                  sspec],
        out_specs=[pl.BlockSpec((g, HEADS, HEAD_DIM), lambda b: (b, 0, 0)), sspec],
        out_shape=[jax.ShapeDtypeStruct((bsz, HEADS, HEAD_DIM), BF),
                   jax.ShapeDtypeStruct(s0.shape, F32)],
        compiler_params=_cparams(("parallel",)),
        name="hgrn_sample",
    )(zs3, zs3, zs3, zs3, lb.reshape(HEADS, HEAD_DIM), gn.reshape(1, HEAD_DIM), s0)


_KEY_NEG_INF = np.int32(np.uint32(0x807FFFFF).astype(np.int64) - (1 << 32))
_INT_MIN = np.int32(-(1 << 31))


def _count(u_ref, n_groups, group, thr, cmp):
    rows = u_ref.shape[1]
    step = min(rows, LANES)
    parts = []
    for r0 in range(0, rows, step):
        t = jnp.broadcast_to(thr[r0:r0 + step], (step, LANES))

        def body(gi, acc, r0=r0, t=t):
            for i in range(group):
                acc = acc + cmp(u_ref[gi * group + i, r0:r0 + step], t).astype(F32)
            return acc

        acc = lax.fori_loop(0, n_groups, body, jnp.zeros((step, LANES), F32))
        parts.append(jnp.sum(acc, axis=-1, keepdims=True))
    return parts[0] if len(parts) == 1 else jnp.concatenate(parts, axis=0)


def _topk_mask(score_ref, u_ref, write_tile, n_tiles, n_groups, group, k):
    rows = score_ref.shape[1]
    n_live = n_groups * group

    def to_key(t, carry):
        bits = pltpu.bitcast(score_ref[t], I32)
        u_ref[t] = jnp.where(bits < 0, bits ^ np.int32(0x7FFFFFFF), bits)
        return carry

    lax.fori_loop(0, n_live, to_key, 0)

    kf = float(k)
    ge = lambda u, t: u >= t
    cnt = _count(u_ref, n_groups, group, jnp.zeros((rows, 1), I32), ge)
    lo = jnp.where(cnt >= kf, np.int32(0), _INT_MIN)

    def bit_step(i, lo):
        cand = lo | (np.int32(1) << (30 - i))
        cnt = _count(u_ref, n_groups, group, cand, ge)
        return jnp.where(cnt >= kf, cand, lo)

    lo = lax.fori_loop(0, 31, bit_step, lo)
    c_gt = _count(u_ref, n_groups, group, lo, lambda u, t: u > t)
    c_eq = _count(u_ref, n_groups, group, lo, lambda u, t: u == t)
    need = kf - c_gt
    real = lo > _KEY_NEG_INF
    excess = jnp.where(real & (c_eq > need), 1.0, 0.0)
    any_excess = jnp.max(excess) > 0.0

    @pl.when(jnp.logical_not(any_excess))
    def _():
        def emit(t, carry):
            u = u_ref[t]
            write_tile(t, jnp.where((u >= lo) & (u > _KEY_NEG_INF), 1.0, 0.0))
            return carry

        lax.fori_loop(0, n_live, emit, 0)

    @pl.when(any_excess)
    def _():
        upper = (lax.broadcasted_iota(I32, (LANES, LANES), 0)
                 <= lax.broadcasted_iota(I32, (LANES, LANES), 1)).astype(BF)

        def emit(t, seen):
            u = u_ref[t]
            eq = jnp.where(u == lo, 1.0, 0.0)
            prefix = seen + _dot(eq.astype(BF), upper)
            take = (u > lo) | ((u == lo) & (prefix <= need))
            write_tile(t, jnp.where(take & (u > _KEY_NEG_INF), 1.0, 0.0))
            return seen + jnp.sum(eq, axis=-1, keepdims=True)

        lax.fori_loop(0, n_live, emit, jnp.zeros((rows, 1), F32))

    def clear(t, carry):
        write_tile(t, jnp.zeros((rows, LANES), F32))
        return carry

    lax.fori_loop(n_live, n_tiles, clear, 0)


QW = 512
SUB = 8
IDX_KCHUNK = 512
ATTN_KSTEP = 512
ATTN_TILES = ATTN_KSTEP // LANES


def _row_all(x8, op):
    return jnp.broadcast_to(op(x8, axis=0, keepdims=True), x8.shape)


def _count_t(u_ref, n_groups, group, thr_row, cmp):
    qw = u_ref.shape[2]
    thr = jnp.broadcast_to(thr_row, (SUB, qw))

    def body(gi, acc):
        for i in range(group):
            u = u_ref[gi * group + i].reshape(LANES // SUB, SUB, qw)
            acc = acc + jnp.sum(cmp(u, thr[None]).astype(F32), axis=0)
        return acc

    acc = lax.fori_loop(0, n_groups, body, jnp.zeros((SUB, qw), F32))
    return jnp.sum(acc, axis=0, keepdims=True)


def _topk_mask_t(score_ref, u_ref, write_tile, n_tiles, n_groups, group, k):
    qw = score_ref.shape[2]
    n_live = n_groups * group

    def to_key(t, carry):
        bits = pltpu.bitcast(score_ref[t], I32)
        u_ref[t] = jnp.where(bits < 0, bits ^ np.int32(0x7FFFFFFF), bits)
        return carry

    lax.fori_loop(0, n_live, to_key, 0)

    kf = float(k)
    ge = lambda u, t: u >= t
    cnt = _count_t(u_ref, n_groups, group, jnp.zeros((1, qw), I32), ge)
    lo = jnp.where(cnt >= kf, np.int32(0), _INT_MIN)

    def bit_step(i, lo):
        cand = lo | (np.int32(1) << (30 - i))
        cnt = _count_t(u_ref, n_groups, group, cand, ge)
        return jnp.where(cnt >= kf, cand, lo)

    lo = lax.fori_loop(0, 31, bit_step, lo)
    c_gt = _count_t(u_ref, n_groups, group, lo, lambda u, t: u > t)
    c_eq = _count_t(u_ref, n_groups, group, lo, lambda u, t: u == t)
    need = kf - c_gt
    excess = jnp.where((lo > _KEY_NEG_INF) & (c_eq > need), 1.0, 0.0)
    any_excess = jnp.max(excess) > 0.0
    lo_b = jnp.broadcast_to(lo, (LANES, qw))

    @pl.when(jnp.logical_not(any_excess))
    def _():
        def emit(t, carry):
            u = u_ref[t]
            write_tile(t, jnp.where((u >= lo_b) & (u > _KEY_NEG_INF), 1.0, 0.0))
            return carry

        lax.fori_loop(0, n_live, emit, 0)

    @pl.when(any_excess)
    def _():
        lower = (lax.broadcasted_iota(I32, (LANES, LANES), 1)
                 <= lax.broadcasted_iota(I32, (LANES, LANES), 0)).astype(BF)

        def emit(t, seen):
            u = u_ref[t]
            eq = jnp.where(u == lo_b, 1.0, 0.0)
            prefix = seen + _dot(lower, eq.astype(BF))
            take = (u > lo_b) | ((u == lo_b) & (prefix <= need))
            write_tile(t, jnp.where(take & (u > _KEY_NEG_INF), 1.0, 0.0))
            return seen + jnp.sum(eq, axis=0, keepdims=True)

        lax.fori_loop(0, n_live, emit, jnp.zeros((1, qw), F32))

    def clear(t, carry):
        write_tile(t, jnp.zeros((LANES, qw), F32))
        return carry

    lax.fori_loop(n_live, n_tiles, clear, 0)


def _indexer_t_body(qi_ref, wt_ref, tail_ref, mask_ref, score_ref, u_ref, *, seq):
    j = pl.program_id(1)
    n_tiles = seq // LANES
    tiles_per_chunk = IDX_KCHUNK // LANES
    qi = qi_ref[0].astype(BF)
    wt = wt_ref[0] * (IDX_DIM ** -0.5 * IDX_HEADS ** -0.5)

    n_chunks = (j * QW + QW - 1) // IDX_KCHUNK + 1
    q_pos = j * QW + lax.broadcasted_iota(I32, (IDX_KCHUNK, QW), 1)

    def chunk_step(ci, carry):
        k0 = pl.multiple_of(ci * IDX_KCHUNK, IDX_KCHUNK)
        kic = tail_ref[0, pl.ds(k0, IDX_KCHUNK), :][:, 0:IDX_DIM].astype(BF)
        acc = jnp.zeros((IDX_KCHUNK, QW), F32)
        for h in range(IDX_HEADS):
            s = _dot_nt(kic, qi[:, h * IDX_DIM:(h + 1) * IDX_DIM])
            acc = acc + wt[h:h + 1] * jnp.maximum(s, 0.0)
        k_pos = k0 + lax.broadcasted_iota(I32, (IDX_KCHUNK, QW), 0)
        acc = jnp.where(k_pos <= q_pos, acc, -jnp.inf)
        for i in range(tiles_per_chunk):
            score_ref[ci * tiles_per_chunk + i] = acc[i * LANES:(i + 1) * LANES]
        return carry

    lax.fori_loop(0, n_chunks, chunk_step, 0)

    def write_tile(t, m):
        mask_ref[0, 0, t] = m.astype(BF)

    _topk_mask_t(score_ref, u_ref, write_tile, n_tiles, n_chunks, tiles_per_chunk, TOPK)


def indexer_prompt_t(z3, wt, tail3):
    bsz, seq = z3.shape[:2]
    n_tiles = seq // LANES
    assert seq % QW == 0 and seq % IDX_KCHUNK == 0
    return pl.pallas_call(
        functools.partial(_indexer_t_body, seq=seq),
        grid=(bsz, seq // QW),
        in_specs=[pl.BlockSpec((1, QW, IDX_HEADS * IDX_DIM), lambda b, j: (b, j, 5)),
                  pl.BlockSpec((1, IDX_HEADS, QW), lambda b, j: (b, 0, j)),
                  pl.BlockSpec((1, seq, LANES), lambda b, j: (b, 0, 0))],
        out_specs=pl.BlockSpec((1, 1, n_tiles, LANES, QW), lambda b, j: (b, j, 0, 0, 0)),
        out_shape=jax.ShapeDtypeStruct((bsz, seq // QW, n_tiles, LANES, QW), BF),
        scratch_shapes=[pltpu.VMEM((n_tiles, LANES, QW), F32),
                        pltpu.VMEM((n_tiles, LANES, QW), I32)],
        compiler_params=_cparams(("parallel", "arbitrary")),
        name="indexer_prompt_t",
    )(z3, wt, tail3)


def _attn_t_body(bfar_ref, q_ref, k_ref, vt_ref, mask_ref, bias_ref, o_ref, m_ref, l_ref, acc_ref):
    j = pl.program_id(1)
    scale = HEAD_DIM ** -0.5
    qsub = QW // LANES
    q = q_ref[0].astype(BF)
    m_ref[...] = jnp.full(m_ref.shape, NEG_BIG, F32)
    l_ref[...] = jnp.zeros(l_ref.shape, F32)
    acc_ref[...] = jnp.zeros(acc_ref.shape, F32)

    def process(sb, near):
        k0 = pl.multiple_of(sb * ATTN_KSTEP, ATTN_KSTEP)
        kblk = k_ref[0, pl.ds(k0, ATTN_KSTEP), :]
        vblk = vt_ref[0, pl.ds(k0, ATTN_KSTEP), :]
        tiles = [sb * ATTN_TILES + i for i in range(ATTN_TILES)]
        sel = jnp.concatenate([mask_ref[0, 0, t] for t in tiles], axis=0).astype(F32) > 0.0
        if near:
            def tile_bias(t, h):
                row = []
                for s in range(qsub):
                    d = j * qsub + s - t
                    row.append(bias_ref[jnp.where(d == 0, 0, jnp.where(d == 1, 1, 2)), h])
                return jnp.concatenate(row, axis=1)
        head_cols = [slice(h * HEAD_DIM, (h + 1) * HEAD_DIM) for h in range(HEADS)]
        qk = lambda h: _dot_nt(kblk[:, head_cols[h]], q[:, head_cols[h]])
        lg_next = qk(0)
        for h in range(HEADS):
            cols = head_cols[h]
            lg = lg_next * scale
            if h + 1 < HEADS:
                lg_next = qk(h + 1)
            if near:
                lg = lg + jnp.concatenate([tile_bias(t, h) for t in tiles], axis=0)
            else:
                lg = lg + bfar_ref[h]
            lg = jnp.where(sel, lg, NEG_BIG)
            m_old = m_ref[h]
            part = jnp.max(lg.reshape(ATTN_KSTEP // SUB, SUB, QW), axis=0)
            m_new = jnp.maximum(m_old, _row_all(part, jnp.max))
            p = jnp.exp(lg - m_new[0:1])
            alpha = jnp.exp(m_old - m_new)
            psum = jnp.sum(p.reshape(ATTN_KSTEP // SUB, SUB, QW), axis=0)
            l_ref[h] = alpha * l_ref[h] + _row_all(psum, jnp.sum)
            pv = _dot_tn(vblk[:, cols], p.astype(BF))
            acc_ref[cols, :] = alpha[0:1] * acc_ref[cols, :] + pv
            m_ref[h] = m_new

    n_far = jnp.maximum((qsub * j - 1) // ATTN_TILES, 0)
    n_steps = (qsub * j + qsub - 1) // ATTN_TILES + 1

    def far_step(sb, carry):
        process(sb, False)
        return carry

    def near_step(sb, carry):
        process(sb, True)
        return carry

    lax.fori_loop(0, n_far, far_step, 0)
    lax.fori_loop(n_far, n_steps, near_step, 0)

    for h in range(HEADS):
        cols = slice(h * HEAD_DIM, (h + 1) * HEAD_DIM)
        o_ref[0, cols, :] = (acc_ref[cols, :] / l_ref[h][0:1]).astype(BF)


def attn_prompt_t(z3, kv_bf, mask_t, bias_tiles_t, bias_far):
    bsz, seq = z3.shape[:2]
    n_tiles = seq // LANES
    width = HEADS * HEAD_DIM
    assert seq % ATTN_KSTEP == 0 and seq % QW == 0
    return pl.pallas_call(
        _attn_t_body,
        grid=(bsz, seq // QW),
        in_specs=[pl.BlockSpec(memory_space=pltpu.SMEM),
                  pl.BlockSpec((1, QW, width), lambda b, j: (b, j, 4)),
                  pl.BlockSpec((1, seq, width), lambda b, j: (b, 0, 0)),
                  pl.BlockSpec((1, seq, width), lambda b, j: (b, 0, 1)),
                  pl.BlockSpec((1, 1, n_tiles, LANES, QW), lambda b, j: (b, j, 0, 0, 0)),
                  pl.BlockSpec(bias_tiles_t.shape, lambda b, j: (0, 0, 0, 0))],
        out_specs=pl.BlockSpec((1, width, QW), lambda b, j: (b, 0, j)),
        out_shape=jax.ShapeDtypeStruct((bsz, width, seq), BF),
        scratch_shapes=[pltpu.VMEM((HEADS, SUB, QW), F32),
                        pltpu.VMEM((HEADS, SUB, QW), F32),
                        pltpu.VMEM((width, QW), F32)],
        compiler_params=_cparams(("parallel", "arbitrary")),
        name="attn_prompt_t",
    )(bias_far, z3, kv_bf, kv_bf, mask_t, bias_tiles_t)


IDX_SEQ_GROUP = 2


def _indexer_sample_body(pt_ref, qi_ref, wcol_ref, kinew_ref, *rest, n_pages):
    ki_refs = rest[:IDX_SEQ_GROUP * n_pages]
    out_ref = rest[IDX_SEQ_GROUP * n_pages]
    lane = lax.broadcasted_iota(I32, (1, LANES), 1)
    for g in range(IDX_SEQ_GROUP):
        qi = qi_ref[g].astype(BF)
        w = wcol_ref[g] * (IDX_DIM ** -0.5 * IDX_HEADS ** -0.5)
        for i in range(n_pages):
            s = _dot(qi, ki_refs[g * n_pages + i][0].astype(BF))
            out_ref[i, g] = jnp.sum(w * jnp.maximum(s, 0.0), axis=0, keepdims=True)
        kn = kinew_ref[g].astype(BF).astype(F32)
        sn = jnp.sum(qi.astype(F32) * kn, axis=-1, keepdims=True)
        new = jnp.sum(w * jnp.maximum(sn, 0.0), axis=0, keepdims=True)
        out_ref[n_pages, g] = jnp.where(lane == 0, new, -jnp.inf)


def indexer_sample(page_table, qi3, wcol, ki_new, ki_pool_t):
    bsz, n_pages = page_table.shape
    grp = IDX_SEQ_GROUP
    assert bsz % grp == 0
    ki_spec = lambda g, i: pl.BlockSpec((1, IDX_DIM, PAGE),
                                        lambda b, pt, g=g, i=i: (pt[b * grp + g, i], 0, 0))
    grid_spec = pltpu.PrefetchScalarGridSpec(
        num_scalar_prefetch=1,
        grid=(bsz // grp,),
        in_specs=[pl.BlockSpec((grp, IDX_HEADS, IDX_DIM), lambda b, pt: (b, 0, 0)),
                  pl.BlockSpec((grp, IDX_HEADS, 1), lambda b, pt: (b, 0, 0)),
                  pl.BlockSpec((grp, 1, IDX_DIM), lambda b, pt: (b, 0, 0))]
                 + [ki_spec(g, i) for g in range(grp) for i in range(n_pages)],
        out_specs=pl.BlockSpec((n_pages + 1, grp, 1, LANES), lambda b, pt: (0, b, 0, 0)),
    )
    out = pl.pallas_call(
        functools.partial(_indexer_sample_body, n_pages=n_pages),
        grid_spec=grid_spec,
        out_shape=jax.ShapeDtypeStruct((n_pages + 1, bsz, 1, LANES), F32),
        compiler_params=_cparams(("arbitrary",)),
        name="indexer_sample",
    )(page_table, qi3, wcol, ki_new, *([ki_pool_t] * (grp * n_pages)))
    return out.reshape(n_pages + 1, bsz, LANES)


def _select_body(score_ref, mask_ref, u_ref, *, n_tiles, k):
    def write_tile(t, m):
        mask_ref[t] = m

    _topk_mask(score_ref, u_ref, write_tile, n_tiles, n_tiles, 1, k)


def select_topk(scores, k):
    n_tiles, rows, _ = scores.shape
    return pl.pallas_call(
        functools.partial(_select_body, n_tiles=n_tiles, k=k),
        grid=(1,),
        in_specs=[pl.BlockSpec(scores.shape, lambda i: (0, 0, 0))],
        out_specs=pl.BlockSpec(scores.shape, lambda i: (0, 0, 0)),
        out_shape=jax.ShapeDtypeStruct(scores.shape, F32),
        scratch_shapes=[pltpu.VMEM(scores.shape, I32)],
        compiler_params=_cparams(("arbitrary",)),
        name="select_topk",
    )(scores)


SC_CORES = 2
SC_SUBCORES = 16
SC_GATHER_CHUNK = 32


def sc_gather_rows(tables, idx):
    n_rows = idx.shape[0]
    workers = SC_CORES * SC_SUBCORES
    per_worker = n_rows // workers
    chunk = SC_GATHER_CHUNK
    assert n_rows % workers == 0 and per_worker % chunk == 0 and chunk % 8 == 0
    row_shape = tables[0].shape[1:]
    n_tab = len(tables)
    mesh = plsc.VectorSubcoreMesh(core_axis_name="c", subcore_axis_name="s",
                                  num_cores=SC_CORES, num_subcores=SC_SUBCORES)

    def body(*refs):
        tab_refs = refs[:n_tab]
        idx_hbm = refs[n_tab]
        out_refs = refs[n_tab + 1:2 * n_tab + 1]
        idx_v, rows_v, sem = refs[2 * n_tab + 1:]
        wid = lax.axis_index("s") * SC_CORES + lax.axis_index("c")

        @pl.loop(0, per_worker // chunk)
        def _(ci):
            off = pl.multiple_of(wid * per_worker + ci * chunk, 8)
            pltpu.sync_copy(idx_hbm.at[pl.ds(off, chunk)], idx_v)
            for tab, out in zip(tab_refs, out_refs):
                pltpu.async_copy(tab.at[idx_v], rows_v, sem).wait()
                pltpu.sync_copy(rows_v, out.at[pl.ds(off, chunk)])

    return pl.kernel(
        body,
        out_type=[jax.ShapeDtypeStruct((n_rows, *row_shape), t.dtype) for t in tables],
        mesh=mesh,
        scratch_types=[pltpu.VMEM((chunk,), I32),
                       pltpu.VMEM((chunk, *row_shape), tables[0].dtype),
                       pltpu.SemaphoreType.DMA],
        compiler_params=pltpu.CompilerParams(use_tc_tiling_on_sc=True),
        name="sc_gather_rows",
    )(*tables, idx)


SAMPLE_SEQ_GROUP = 8


def _attn_compact_body(dist_ref, near_ref, q_ref, kn_ref, vn_ref, btab_ref, kc_ref, vc_ref,
                       o_ref, bbuf):
    step = pl.program_id(0)
    scale = HEAD_DIM ** -0.5
    ones = jnp.ones((HEAD_DIM, LANES), BF)
    far_bias = btab_ref[REL_MAX_DIST]
    last = lax.broadcasted_iota(I32, (TOPK, 1, 1), 0) == TOPK - 1
    for g in range(SAMPLE_SEQ_GROUP):
        b = step * SAMPLE_SEQ_GROUP + g
        bbuf[g] = jnp.broadcast_to(far_bias[None], (TOPK, HEADS, LANES))

        def fill(t, carry, g=g, b=b):
            bbuf[g, t] = btab_ref[dist_ref[b, t]]
            return carry

        lax.fori_loop(near_ref[b], TOPK, fill, 0)

    for g in range(SAMPLE_SEQ_GROUP):
        b = step * SAMPLE_SEQ_GROUP + g
        is_new = dist_ref[b, TOPK - 1] == 0
        rows = slice(g * TOPK, (g + 1) * TOPK)
        rnd = lambda a: a.astype(BF).astype(F32)
        kc = rnd(jnp.where(last & is_new, kn_ref[g][None], kc_ref[rows]))
        vc = rnd(jnp.where(last & is_new, vn_ref[g][None], vc_ref[rows]))
        prod = (kc * rnd(q_ref[g])[None]).reshape(TOPK * HEADS, HEAD_DIM)
        hi = prod.astype(BF)
        lo = (prod - hi.astype(F32)).astype(BF)
        lg = (_dot(hi, ones) + _dot(lo, ones)).reshape(TOPK, HEADS, LANES) * scale + bbuf[g]
        m = jnp.max(lg, axis=0, keepdims=True)
        p = jnp.exp(lg - m)
        l = jnp.sum(p, axis=0)
        o_ref[g] = (jnp.sum(rnd(p) * vc, axis=0) / l).astype(BF)


def attn_sample_compact(dist, first_near, zs3, k_sel, v_sel, bias_by_dist):
    bsz = dist.shape[0]
    g = SAMPLE_SEQ_GROUP
    assert bsz % g == 0
    zspec = lambda cb: pl.BlockSpec((g, HEADS, HEAD_DIM), lambda i, d, n, cb=cb: (i, cb, 0))
    sel_spec = pl.BlockSpec((g * TOPK, HEADS, HEAD_DIM), lambda i, d, n: (i, 0, 0))
    grid_spec = pltpu.PrefetchScalarGridSpec(
        num_scalar_prefetch=2,
        grid=(bsz // g,),
        in_specs=[zspec(4), zspec(5), zspec(6),
                  pl.BlockSpec(bias_by_dist.shape, lambda i, d, n: (0, 0, 0)),
                  sel_spec, sel_spec],
        out_specs=pl.BlockSpec((g, HEADS, HEAD_DIM), lambda i, d, n: (i, 0, 0)),
        scratch_shapes=[pltpu.VMEM((g, TOPK, HEADS, HEAD_DIM), F32)],
    )
    return pl.pallas_call(
        _attn_compact_body,
        grid_spec=grid_spec,
        out_shape=jax.ShapeDtypeStruct((bsz, HEADS, HEAD_DIM), BF),
        compiler_params=_cparams(("arbitrary",)),
        name="attn_sample_compact",
    )(dist, first_near, zs3, zs3, zs3, bias_by_dist, k_sel, v_sel)


def _bucket_table(max_dist):
    exact = REL_BUCKETS // 2
    d = np.arange(max_dist + 1)
    df = np.maximum(d, 1).astype(np.float32)
    far = exact + (np.log(df / exact) / np.float32(math.log(REL_MAX_DIST / exact))
                   * (REL_BUCKETS - exact)).astype(np.int32)
    return np.where(d < exact, d, np.minimum(far, REL_BUCKETS - 1)).astype(np.int32)


def _bias_tables(rel_bias):
    tab = _bucket_table(2 * LANES)
    assert np.all(tab[REL_MAX_DIST:] == REL_BUCKETS - 1)
    i = np.arange(LANES)
    dist0 = np.maximum(i[:, None] - i[None, :], 0)
    dist1 = LANES + i[:, None] - i[None, :]
    far = np.full((LANES, LANES), REL_BUCKETS - 1)
    idx = np.stack([tab[dist0], tab[dist1], far])
    buckets = jnp.arange(REL_BUCKETS)
    lookup = lambda ix, spec: jnp.einsum(spec, (jnp.asarray(ix)[..., None] == buckets).astype(F32),
                                         rel_bias.astype(F32), precision=lax.Precision.HIGHEST)
    tiles = lookup(idx, "tijb,bh->thij")
    by_dist = jnp.broadcast_to(lookup(tab[:REL_MAX_DIST + 1], "db,bh->dh")[:, :, None],
                               (REL_MAX_DIST + 1, HEADS, LANES))
    return tiles.astype(F32), rel_bias[REL_BUCKETS - 1].astype(F32), by_dist.astype(F32)


def _conv_prompt_body(bg_ref, cg_ref, xt_ref, w_ref, v_ref, st_ref, carry_ref, *, tblock):
    t = pl.program_id(1)

    @pl.when(t == 0)
    def _():
        carry_ref[...] = jnp.zeros_like(carry_ref)

    u = cg_ref[0] * xt_ref[0]
    row = lax.broadcasted_iota(I32, u.shape, 0)
    c0 = carry_ref[0:1]
    c1 = carry_ref[1:2]
    u1 = jnp.where(row == 0, c1, pltpu.roll(u, 1, axis=0))
    u2 = jnp.where(row == 0, c0, jnp.where(row == 1, c1, pltpu.roll(u, 2, axis=0)))
    conv = w_ref[0:1] * u2 + w_ref[1:2] * u1 + w_ref[2:3] * u
    v_ref[0] = (bg_ref[0] * conv).astype(BF)
    last = u[tblock - 2:tblock]
    carry_ref[0:2] = last

    @pl.when(t == pl.num_programs(1) - 1)
    def _():
        st_ref[0] = last


def conv_prompt(zc3, w_conv, tblock=256):
    bsz, seq = zc3.shape[:2]
    c = zc3.shape[2] // 3
    zspec = lambda cb: pl.BlockSpec((1, tblock, c), lambda b, t, cb=cb: (b, t, cb))
    return pl.pallas_call(
        functools.partial(_conv_prompt_body, tblock=tblock),
        grid=(bsz, seq // tblock),
        in_specs=[zspec(0), zspec(1), zspec(2), pl.BlockSpec((3, c), lambda b, t: (0, 0))],
        out_specs=[pl.BlockSpec((1, tblock, c), lambda b, t: (b, t, 0)),
                   pl.BlockSpec((1, 2, c), lambda b, t: (b, 0, 0))],
        out_shape=[jax.ShapeDtypeStruct((bsz, seq, c), BF),
                   jax.ShapeDtypeStruct((bsz, 2, c), F32)],
        scratch_shapes=[pltpu.VMEM((8, c), F32)],
        compiler_params=_cparams(("parallel", "arbitrary")),
        name="conv_prompt",
    )(zc3, zc3, zc3, w_conv)


def _conv_sample_body(bg_ref, cg_ref, xt_ref, w_ref, s0_ref, s1_ref, v_ref, n0_ref, n1_ref):
    u = cg_ref[...] * xt_ref[...]
    conv = w_ref[0:1] * s0_ref[...] + w_ref[1:2] * s1_ref[...] + w_ref[2:3] * u
    v_ref[...] = (bg_ref[...] * conv).astype(BF)
    n0_ref[...] = s1_ref[...]
    n1_ref[...] = u


def conv_sample(zc, w_conv, s0, s1):
    bsz = zc.shape[0]
    c = zc.shape[1] // 3
    zspec = lambda cb: pl.BlockSpec((bsz, c), lambda i, cb=cb: (0, cb))
    full = pl.BlockSpec((bsz, c), lambda i: (0, 0))
    return pl.pallas_call(
        _conv_sample_body,
        grid=(1,),
        in_specs=[zspec(0), zspec(1), zspec(2), pl.BlockSpec((3, c), lambda i: (0, 0)), full, full],
        out_specs=[full, full, full],
        out_shape=[jax.ShapeDtypeStruct((bsz, c), BF)] + [jax.ShapeDtypeStruct((bsz, c), F32)] * 2,
        compiler_params=_cparams(("arbitrary",)),
        name="conv_sample",
    )(zc, zc, zc, w_conv, s0, s1)


def _router_body(x_ref, g_ref, wr_ref, br_ref, *rest):
    h_ref, route_ref = rest[-2:]
    x = x_ref[...]
    ms = jnp.mean(x * x, axis=-1, keepdims=True)
    h = x * lax.rsqrt(ms + RMS_EPS) * g_ref[...]
    hb = h.astype(BF)
    half = h.shape[1] // 2
    bits = pltpu.bitcast(hb.astype(F32), jnp.uint32)
    h_ref[...] = (bits[:, :half] & jnp.uint32(0xFFFF0000)) | (bits[:, half:] >> 16)
    logits = _dot(hb, wr_ref[...]) + br_ref[...]
    lane = lax.broadcasted_iota(I32, logits.shape, 1)
    big = np.int32(1 << 20)
    lg = jnp.where(lane < N_GROUPS, logits, -jnp.inf)
    g_max = jnp.max(lg, axis=-1, keepdims=True)
    g_idx = jnp.min(jnp.where(lg == g_max, lane, big), axis=-1, keepdims=True)
    g_w = 1.0 / jnp.sum(jnp.exp(lg - g_max), axis=-1, keepdims=True)
    first = N_GROUPS + EXP_PER_GROUP * g_idx
    le = jnp.where((lane >= first) & (lane < first + EXP_PER_GROUP), logits, -jnp.inf)
    l1 = jnp.max(le, axis=-1, keepdims=True)
    i1 = jnp.min(jnp.where(le == l1, lane, big), axis=-1, keepdims=True)
    le2 = jnp.where(lane == i1, -jnp.inf, le)
    l2 = jnp.max(le2, axis=-1, keepdims=True)
    i2 = jnp.min(jnp.where(le2 == l2, lane, big), axis=-1, keepdims=True)
    r = jnp.exp(l2 - l1)
    w1 = g_w / (1.0 + r)
    w2 = g_w * r / (1.0 + r)
    e1 = (i1 - N_GROUPS).astype(F32)
    e2 = (i2 - N_GROUPS).astype(F32)
    route_ref[...] = jnp.where(lane == 0, e1, jnp.where(lane == 1, e2,
                               jnp.where(lane == 2, w1, jnp.where(lane == 3, w2, 0.0))))


def moe_router(x, g, wr, br, tm, total_rows, row0, bufs=None):
    m, k = x.shape
    assert row0 % tm == 0
    off = row0 // tm
    in_specs = [pl.BlockSpec((tm, k), lambda i: (i, 0)),
                pl.BlockSpec((1, k), lambda i: (0, 0)),
                pl.BlockSpec((k, LANES), lambda i: (0, 0)),
                pl.BlockSpec((1, LANES), lambda i: (0, 0))]
    args = [x, g.reshape(1, k), wr, br]
    aliases = {}
    if bufs is not None:
        in_specs += [pl.BlockSpec(memory_space=pl.ANY)] * 2
        args += list(bufs)
        aliases = {4: 0, 5: 1}
    return pl.pallas_call(
        _router_body,
        grid=(m // tm,),
        in_specs=in_specs,
        out_specs=[pl.BlockSpec((tm, k // 2), lambda i: (i + off, 0)),
                   pl.BlockSpec((tm, LANES), lambda i: (i + off, 0))],
        out_shape=[jax.ShapeDtypeStruct((total_rows, k // 2), jnp.uint32),
                   jax.ShapeDtypeStruct((total_rows, LANES), F32)],
        input_output_aliases=aliases,
        compiler_params=_cparams(("parallel",)),
        name="moe_router",
    )(*args)


def _experts_body(te_ref, nu_ref, h_ref, wg_ref, wu_ref, wd_ref, o_ref, wg_bf, wu_bf, wd_bf):
    i = pl.program_id(0)

    @pl.when((i == 0) | (te_ref[i] != te_ref[jnp.maximum(i - 1, 0)]))
    def _():
        wg_bf[...] = wg_ref[0].astype(BF)
        wu_bf[...] = wu_ref[0].astype(BF)
        wd_bf[...] = wd_ref[0].astype(BF)

    @pl.when(i < nu_ref[0])
    def _():
        words = h_ref[...]
        left = pltpu.bitcast(words & jnp.uint32(0xFFFF0000), F32)
        right = pltpu.bitcast(words << 16, F32)
        h = jnp.concatenate([left, right], axis=1).astype(BF)
        a = _dot(h, wg_bf[...])
        b = _dot(h, wu_bf[...])
        hid = a * _sigmoid(a) * b
        o_ref[...] = _dot(hid.astype(BF), wd_bf[...])

    @pl.when(i >= nu_ref[0])
    def _():
        o_ref[...] = jnp.zeros_like(o_ref)


def moe_experts(tile_expert, n_used, hs, wg, wu, wd):
    p = hs.shape[0]
    k = wg.shape[1]
    f = wg.shape[2]
    n_tiles = p // MOE_TILE
    grid_spec = pltpu.PrefetchScalarGridSpec(
        num_scalar_prefetch=2,
        grid=(n_tiles,),
        in_specs=[pl.BlockSpec((MOE_TILE, k // 2), lambda i, te, nu: (i, 0)),
                  pl.BlockSpec((1, k, f), lambda i, te, nu: (te[i], 0, 0)),
                  pl.BlockSpec((1, k, f), lambda i, te, nu: (te[i], 0, 0)),
                  pl.BlockSpec((1, f, k), lambda i, te, nu: (te[i], 0, 0))],
        out_specs=pl.BlockSpec((MOE_TILE, k), lambda i, te, nu: (i, 0)),
        scratch_shapes=[pltpu.VMEM((k, f), BF), pltpu.VMEM((k, f), BF), pltpu.VMEM((f, k), BF)],
    )
    return pl.pallas_call(
        _experts_body,
        grid_spec=grid_spec,
        out_shape=jax.ShapeDtypeStruct((p, k), F32),
        compiler_params=_cparams(("arbitrary",)),
        name="moe_experts",
    )(tile_expert, n_used, hs, wg, wu, wd)


def _combine_body(x_ref, a_ref, b_ref, route_ref, g_ref, o_ref, *, normalize):
    y = x_ref[...] + route_ref[:, 2:3] * a_ref[...] + route_ref[:, 3:4] * b_ref[...]
    if normalize:
        ms = jnp.mean(y * y, axis=-1, keepdims=True)
        y = y * lax.rsqrt(ms + RMS_EPS) * g_ref[...]
    o_ref[...] = y


def moe_combine(x, out_a, out_b, route, tm, norm_g=None):
    m, k = x.shape
    row = pl.BlockSpec((tm, k), lambda i: (i, 0))
    g = jnp.ones((1, k), F32) if norm_g is None else norm_g.reshape(1, k)
    return pl.pallas_call(
        functools.partial(_combine_body, normalize=norm_g is not None),
        grid=(m // tm,),
        in_specs=[row, row, row, pl.BlockSpec((tm, LANES), lambda i: (i, 0)),
                  pl.BlockSpec((1, k), lambda i: (0, 0))],
        out_specs=row,
        out_shape=jax.ShapeDtypeStruct((m, k), F32),
        compiler_params=_cparams(("parallel",)),
        name="moe_combine",
    )(x, out_a, out_b, route, g)


def _rank_within_expert(onehot):
    n, e = onehot.shape
    blk = LANES
    assert n % blk == 0
    oh = onehot.astype(F32).reshape(n // blk, blk, e)
    strict = jnp.asarray(np.tril(np.ones((blk, blk), np.float32), -1))
    within = jnp.einsum("ij,bjk->bik", strict, oh, precision=lax.Precision.HIGHEST)
    totals = jnp.sum(oh, axis=1)
    before = jnp.cumsum(totals, axis=0) - totals
    rank = (within + before[:, None, :]).reshape(n, e)
    return jnp.sum(rank * onehot.astype(F32), axis=1).astype(I32), jnp.sum(totals, axis=0).astype(I32)


def hier_moe(xs, tms, g, wrg, brg, wre, bre, wg, wu, wd, layer, out_norm_g=None):
    d = xs[0].shape[1]
    m = sum(x.shape[0] for x in xs)
    wr = jnp.zeros((d, LANES), F32)
    wr = wr.at[:, :N_GROUPS].set(wrg).at[:, N_GROUPS:N_GROUPS + N_EXPERTS].set(wre.reshape(d, N_EXPERTS))
    br = jnp.zeros((1, LANES), F32)
    br = br.at[0, :N_GROUPS].set(brg).at[0, N_GROUPS:N_GROUPS + N_EXPERTS].set(bre.reshape(N_EXPERTS))
    bufs, row0 = None, 0
    for x, tm in zip(xs, tms):
        bufs = moe_router(x, g, wr.astype(BF), br, tm, m, row0, bufs)
        row0 += x.shape[0]
    h_bf, route = bufs

    eid = route[:, 0:2].astype(I32).reshape(-1)
    onehot = eid[:, None] == jnp.arange(N_EXPERTS, dtype=I32)[None, :]
    rank, counts = _rank_within_expert(onehot)
    padded = ((counts + MOE_TILE - 1) // MOE_TILE) * MOE_TILE
    ends = jnp.cumsum(padded)
    pos = jnp.sum(jnp.where(onehot, (ends - padded)[None, :], 0), axis=1) + rank
    n_rows = 2 * m + N_EXPERTS * MOE_TILE
    n_rows = -(-n_rows // MOE_TILE) * MOE_TILE
    token = jnp.zeros((n_rows,), I32).at[pos].set(jnp.arange(2 * m, dtype=I32) // 2)
    tile_start = jnp.arange(n_rows // MOE_TILE, dtype=I32) * MOE_TILE
    tile_expert = jnp.minimum(jnp.sum(tile_start[:, None] >= ends[None, :], axis=1),
                              N_EXPERTS - 1).astype(I32)
    n_used = (ends[-1] // MOE_TILE).astype(I32).reshape(1)

    hs = jnp.take(h_bf, token, axis=0, mode="clip")
    out = moe_experts(tile_expert + layer * N_EXPERTS, n_used, hs, wg, wu, wd)
    pos2 = pos.reshape(m, 2)
    res, r0 = [], 0
    for x, tm in zip(xs, tms):
        rows = slice(r0, r0 + x.shape[0])
        res.append(moe_combine(x, jnp.take(out, pos2[rows, 0], axis=0, mode="clip"),
                               jnp.take(out, pos2[rows, 1], axis=0, mode="clip"),
                               route[rows], tm, out_norm_g))
        r0 += x.shape[0]
    return res


def kernel(x_prompt, x_sample, cache_k, cache_v, cache_ki, state_hgrn, state_conv, page_table,
           norm_mix_g, norm_ffn_g, final_g, w_in_even, w_out_even, hgrn_lb_logits, hgrn_norm_g,
           rel_bias, w_in_conv, w_conv, w_out_conv, w_router_g, b_router_g, w_router_e,
           b_router_e, w_gate, w_up, w_down):
    bsz, seq, d = x_prompt.shape
    dec = x_sample.shape[0]
    n_p = bsz * seq
    tm_p, tm_s = 512, 128
    tms = (tm_p, tm_s)
    assert n_p % tm_p == 0 and dec % tm_s == 0 and x_sample.shape[1] == 1
    width = HEADS * HEAD_DIM
    even_in = w_in_even.shape[2]
    even_pad = -(-even_in // IN_PROJ_TN) * IN_PROJ_TN
    tail0 = TAIL_BLOCK * LANES
    n_pool = cache_k.shape[1]

    xp = x_prompt.reshape(n_p, d)
    xs = x_sample.reshape(dec, d)

    lbs = jnp.cumsum(jax.nn.softmax(hgrn_lb_logits.astype(F32), axis=0), axis=0)[:-1]
    bias_tiles, bias_far, bias_by_dist = _bias_tables(rel_bias)
    expert_w = [w.reshape(-1, *w.shape[2:]) for w in (w_gate, w_up, w_down)]

    w_in = jnp.pad(w_in_even[0], ((0, 0), (0, even_pad - even_in))).astype(BF)
    zp, k_new_p, v_new_p, kv_bf = even_projection(xp, norm_mix_g[0], w_in, 2 * tm_p)
    zt = norm_matmul(xp, norm_mix_g[0], w_in[:, tail0:tail0 + LANES], 2 * tm_p, LANES)
    zs = norm_matmul(xs, norm_mix_g[0], w_in, tm_s, IN_PROJ_TN)
    zp3 = zp.reshape(bsz, seq, 6 * width)
    zt3 = zt.reshape(bsz, seq, LANES)
    zs3 = zs.reshape(dec, even_pad // LANES, LANES)

    oa_p, hgrn_p = hgrn_prompt(zp3, lbs[0], hgrn_norm_g[0])
    oa_s, hgrn_s = hgrn_sample(zs3, lbs[0], hgrn_norm_g[0], state_hgrn[0])

    wt = jnp.swapaxes(zt3[:, :, IDX_DIM:IDX_DIM + IDX_HEADS], 1, 2)
    mask_p = indexer_prompt_t(zp3, wt, zt3)
    ob_p = jnp.swapaxes(attn_prompt_t(zp3, kv_bf.reshape(bsz, seq, 2 * width), mask_p,
                                      jnp.swapaxes(bias_tiles, -1, -2), bias_far), 1, 2)

    qi3 = zs[:, 7 * width:8 * width].reshape(dec, IDX_HEADS, IDX_DIM)
    ki_new = zs[:, tail0:tail0 + IDX_DIM].reshape(dec, 1, IDX_DIM)
    wcol = zs[:, tail0 + IDX_DIM:tail0 + IDX_DIM + IDX_HEADS].reshape(dec, IDX_HEADS, 1)
    pages = page_table + 0 * n_pool
    scores_s = indexer_sample(pages, qi3, wcol, ki_new,
                              jnp.swapaxes(cache_ki, -1, -2).reshape(-1, IDX_DIM, PAGE))
    mask_s = select_topk(scores_s, TOPK)
    sel = jnp.transpose(mask_s, (1, 0, 2))
    n_pages = page_table.shape[1]
    sel_off = jnp.concatenate([jnp.zeros((dec, 1), F32),
                               jnp.cumsum(jnp.sum(sel, axis=2), axis=1)], axis=1)
    slot = jnp.arange(TOPK, dtype=F32)
    page_of = jnp.sum(sel_off[:, None, 1:] <= slot[None, :, None], axis=2)
    page_1h = (page_of[:, :, None] == jnp.arange(n_pages + 1)[None, None, :]).astype(F32)
    local = slot[None, :] - jnp.einsum("brp,bp->br", page_1h, sel_off[:, :-1],
                                       precision=lax.Precision.HIGHEST)
    within = jnp.einsum("brp,bpl->brl", page_1h, jnp.cumsum(sel, axis=2),
                        precision=lax.Precision.HIGHEST)
    lane_of = jnp.sum(within <= local[:, :, None], axis=2)
    sel_idx = (page_of * PAGE + lane_of).astype(I32)
    past = n_pages * PAGE
    page_id = jnp.einsum("brp,bp->br", page_1h[:, :, :n_pages], pages.astype(F32),
                         precision=lax.Precision.HIGHEST).astype(I32)
    rows = jnp.where(page_of < n_pages, page_id * PAGE + lane_of, 0).astype(I32).reshape(-1)
    rows, zp3 = lax.optimization_barrier((rows, zp3))
    k_sel, v_sel = sc_gather_rows([cache_k.reshape(-1, HEADS, HEAD_DIM),
                                   cache_v.reshape(-1, HEADS, HEAD_DIM)], rows)
    dist = jnp.minimum(past - sel_idx, REL_MAX_DIST).astype(I32)
    first_near = jnp.sum(dist >= REL_MAX_DIST, axis=1).astype(I32)
    ob_s = attn_sample_compact(dist, first_near, zs3, k_sel, v_sel, bias_by_dist)

    w_out = w_out_even[0].astype(BF)
    w_out_ab = [w_out[:width], w_out[width:]]
    xp = matmul_residual([oa_p.reshape(n_p, width), ob_p.reshape(n_p, width)], w_out_ab, xp, 2 * tm_p, 1024)
    xs = matmul_residual([oa_s.reshape(dec, width), ob_s.reshape(dec, width)], w_out_ab, xs, tm_s, 512)

    xp, xs = hier_moe([xp, xs], tms, norm_ffn_g[0], w_router_g[0], b_router_g[0], w_router_e[0],
                      b_router_e[0], *expert_w, 0)

    w_in_c = w_in_conv[0].astype(BF)
    zcp = norm_matmul(xp, norm_mix_g[1], w_in_c, 2 * tm_p, 1024)
    zcs = norm_matmul(xs, norm_mix_g[1], w_in_c, tm_s, 512)
    cw = zcp.shape[1] // 3
    v_p, conv_p = conv_prompt(zcp.reshape(bsz, seq, 3 * cw), w_conv[0])
    v_s, cs0, cs1 = conv_sample(zcs, w_conv[0], state_conv[0, :, 0], state_conv[0, :, 1])
    w_out_c = [w_out_conv[0].astype(BF)]
    xp = matmul_residual([v_p.reshape(n_p, cw)], w_out_c, xp, 2 * tm_p, 1024)
    xs = matmul_residual([v_s], w_out_c, xs, tm_s, 512)

    yp, ys = hier_moe([xp, xs], tms, norm_ffn_g[1], w_router_g[1], b_router_g[1], w_router_e[1],
                      b_router_e[1], *expert_w, 1, out_norm_g=final_g)

    kcol, vcol = 5 * width, 6 * width
    heads = lambda a, n: a.reshape(1, *n, HEADS, HEAD_DIM)
    return (yp.reshape(bsz, seq, d),
            ys.reshape(dec, 1, d),
            heads(k_new_p, (bsz, seq)),
            heads(v_new_p, (bsz, seq)),
            zt[:, :IDX_DIM].reshape(1, bsz, seq, IDX_DIM),
            hgrn_p[None],
            conv_p[None],
            heads(zs[:, kcol:kcol + width], (dec, 1)),
            heads(zs[:, vcol:vcol + width], (dec, 1)),
            zs[:, tail0:tail0 + IDX_DIM].reshape(1, dec, 1, IDX_DIM),
            hgrn_s[None],
            jnp.stack([cs0, cs1], axis=1)[None])
```
